```python
import math
import jax, jax.numpy as jnp
from jax import lax
import numpy as np


D_MODEL = 1024
BATCH = 32
SEQ = 256
DEPTH = 2
DEC_BATCH = 8
DEC_SEQ = 1024
PAST_LEN = 512

GRID_W = 64
N_EVEN = (DEPTH + 1) // 2
N_ODD = DEPTH // 2
N_DIR = 2
HY_W = D_MODEL // 2
HY_ORDER = 2
HY_BANDS = 16
HY_EMB = 1 + 2 * HY_BANDS
HY_FF = 64
ML_W = D_MODEL // 2
ML_HEADS = 4
ML_HD = ML_W // ML_HEADS
ML_CHUNK = 64
EVEN_IN = 3 * HY_W + 4 * ML_W + N_DIR * 2 * ML_HEADS
ATT_HEADS = 16
ATT_HD = D_MODEL // ATT_HEADS
ATT_KV = 4
ATT_G = ATT_HEADS // ATT_KV
WINDOW = 128
QBLOCK = 128
ROPE_BASE = 10000.0
ODD_IN = (ATT_HEADS + 2 * ATT_KV) * ATT_HD
D_FF = 2816
N_EXPERTS = 8
TOP_K = 2
EPS = 1e-6
NEG = -1e30

kernel_name = "hyena_mlstm_swa_moe_diffusion_step"


def rmsnorm(x, g):
    xf = x.astype(jnp.float32)
    y = xf * lax.rsqrt(jnp.mean(xf * xf, axis=-1, keepdims=True) + EPS)
    return (y * g.astype(jnp.float32)).astype(x.dtype)


def adaln(cond, w_ada, b_ada):
    mod = (jax.nn.silu(cond) @ w_ada + b_ada)[..., None, :]
    return jnp.split(mod, 6, axis=-1)


def modulate(x, shift, scale):
    return x * (1 + scale) + shift


def swiglu(x, w1, w3, w2):
    return (jax.nn.silu(x @ w1) * (x @ w3)) @ w2


def short_conv3(x, w):
    L = x.shape[1]
    xp = jnp.pad(x, ((0, 0), (1, 1), (0, 0)))
    return xp[:, :L] * w[0] + xp[:, 1:L + 1] * w[1] + xp[:, 2:] * w[2]


def hyena_filters(L, w1, b1, w2, b2, w3, freq):
    f32 = jnp.float32
    t = jnp.linspace(0.0, 1.0, L, dtype=f32)[:, None]
    w = 2.0 * math.pi * jnp.arange(L, dtype=f32)[:, None] / L
    bands = jnp.linspace(1e-4, HY_BANDS - 1, HY_BANDS, dtype=f32)[None, :]
    z = jnp.concatenate([t, jnp.cos(bands * w), -jnp.sin(bands * w)], axis=-1)
    fr = freq.astype(f32)
    h = jnp.sin(fr[0] * (z @ w1.astype(f32) + b1.astype(f32)))
    h = jnp.sin(fr[1] * (h @ w2.astype(f32) + b2.astype(f32)))
    h = h @ w3.astype(f32)
    max_decay = math.log(1e-2) / 0.3
    min_decay = math.log(1e-2) / 1.5
    deltas = jnp.linspace(min_decay, max_decay, HY_W, dtype=f32)
    decay = jnp.exp(-t * jnp.abs(deltas))
    h = h.reshape(L, HY_ORDER, N_DIR, HY_W) * decay[:, None, None, :]
    h = jnp.moveaxis(h, 0, 2)
    return h / jnp.sum(jnp.abs(h), axis=(1, 2), keepdims=True)


def long_conv(z, h):
    L = z.shape[1]
    k = jnp.concatenate([h[0], jnp.zeros_like(h[0][:1]), h[1][:0:-1]], axis=0)
    zf = jnp.fft.rfft(z, n=2 * L, axis=1)
    kf = jnp.fft.rfft(k, n=2 * L, axis=0)
    return jnp.fft.irfft(zf * kf[None], n=2 * L, axis=1)[:, :L]


def hyena_mixer(u, conv_w, w1, b1, w2, b2, w3, freq, d_skip):
    L = u.shape[1]
    u = short_conv3(u, conv_w).astype(jnp.float32)
    v, x1, x2 = jnp.split(u, 3, axis=-1)
    h = hyena_filters(L, w1, b1, w2, b2, w3, freq)
    d = d_skip.astype(jnp.float32)
    z = x1 * (long_conv(v, h[0]) + d[0] * v)
    z = x2 * (long_conv(z, h[1]) + d[1] * z)
    return z


def mlstm_scan(q, k, v, li, lf, C0, n0, m0):
    B, H, L, d = q.shape
    nc = L // ML_CHUNK
    tri = jnp.tril(jnp.ones((ML_CHUNK, ML_CHUNK), dtype=bool))

    def to_chunks(a):
        a = a.reshape(a.shape[:2] + (nc, ML_CHUNK) + a.shape[3:])
        return jnp.moveaxis(a, 2, 0)

    def step(carry, xs):
        C, n, m = carry
        qc, kc, vc, lic, lfc = xs
        b = jnp.cumsum(lfc, axis=-1)
        dmat = jnp.where(tri, b[..., :, None] - b[..., None, :] + lic[..., None, :], -jnp.inf)
        inter = b + m[..., None]
        m_t = jnp.maximum(inter, jnp.max(dmat, axis=-1))
        w_intra = jnp.exp(dmat - m_t[..., None])
        w_inter = jnp.exp(inter - m_t)
        s = jnp.einsum("bhtd,bhsd->bhts", qc, kc) * w_intra
        num = jnp.einsum("bhts,bhse->bhte", s, vc) + w_inter[..., None] * jnp.einsum("bhtd,bhde->bhte", qc, C)
        den = jnp.sum(s, axis=-1) + w_inter * jnp.einsum("bhtd,bhd->bht", qc, n)
        h = num / jnp.maximum(jnp.abs(den), jnp.exp(-m_t))[..., None]
        b_end = b[..., -1]
        g = b_end[..., None] - b + lic
        m_new = jnp.maximum(b_end + m, jnp.max(g, axis=-1))
        w_tok = jnp.exp(g - m_new[..., None])
        decay = jnp.exp(b_end + m - m_new)
        C_new = decay[..., None, None] * C + jnp.einsum("bhs,bhsd,bhse->bhde", w_tok, kc, vc)
        n_new = decay[..., None] * n + jnp.einsum("bhs,bhsd->bhd", w_tok, kc)
        return (C_new, n_new, m_new), h

    xs = (to_chunks(q), to_chunks(k * (1.0 / math.sqrt(d))), to_chunks(v), to_chunks(li), to_chunks(lf))
    (C, n, m), h = lax.scan(step, (C0, n0, m0), xs)
    h = jnp.moveaxis(h, 0, 2).reshape(B, H, L, d)
    return h, C, n, m


def mlstm_bidir(q, k, v, gates, C0, n0, m0):
    f32 = jnp.float32
    h_sum = jnp.zeros(q.shape, f32)
    Cs, ns, ms = [], [], []
    for dr in range(N_DIR):
        li = jnp.moveaxis(gates[:, :, dr, 0], 1, 2)
        lf = jax.nn.log_sigmoid(jnp.moveaxis(gates[:, :, dr, 1], 1, 2))
        seq = (q, k, v, li, lf)
        if dr == 1:
            seq = tuple(jnp.flip(a, axis=2) for a in seq)
        h, C, n, m = mlstm_scan(*seq, C0[:, dr].astype(f32), n0[:, dr].astype(f32), m0[:, dr].astype(f32))
        if dr == 1:
            h = jnp.flip(h, axis=2)
        h_sum = h_sum + h
        Cs.append(C)
        ns.append(n)
        ms.append(m)
    return h_sum, jnp.stack(Cs, axis=1), jnp.stack(ns, axis=1), jnp.stack(ms, axis=1)


def even_mixer(xm, w_in, conv_w, hy_w1, hy_b1, hy_w2, hy_b2, hy_w3, hy_freq, hy_d,
               b_gate, ml_norm_g, w_out, C0, n0, m0):
    B, L, _ = xm.shape
    f32 = jnp.float32
    u = xm @ w_in
    y_hy = hyena_mixer(u[..., :3 * HY_W], conv_w, hy_w1, hy_b1, hy_w2, hy_b2, hy_w3, hy_freq, hy_d)
    base = 3 * HY_W
    q, k, v, o = (u[..., base + i * ML_W: base + (i + 1) * ML_W] for i in range(4))
    gates = u[..., base + 4 * ML_W:].astype(f32).reshape(B, L, N_DIR, 2, ML_HEADS) + b_gate.astype(f32)
    heads = lambda a: a.reshape(B, L, ML_HEADS, ML_HD).transpose(0, 2, 1, 3).astype(f32)
    h, C, n, m = mlstm_bidir(heads(q), heads(k), heads(v), gates, C0, n0, m0)
    h = rmsnorm(h.transpose(0, 2, 1, 3), ml_norm_g).reshape(B, L, ML_W) * jax.nn.sigmoid(o.astype(f32))
    y = jnp.concatenate([y_hy, h], axis=-1).astype(xm.dtype) @ w_out
    return y, C, n, m


def attn_qkv(xm, w_qkv, q_g, k_g):
    B, L, _ = xm.shape
    u = xm @ w_qkv
    q = u[..., :ATT_HEADS * ATT_HD].reshape(B, L, ATT_HEADS, ATT_HD)
    k = u[..., ATT_HEADS * ATT_HD:(ATT_HEADS + ATT_KV) * ATT_HD].reshape(B, L, ATT_KV, ATT_HD)
    v = u[..., (ATT_HEADS + ATT_KV) * ATT_HD:].reshape(B, L, ATT_KV, ATT_HD)
    return rmsnorm(q, q_g), rmsnorm(k, k_g), v


def rope_2d(x):
    f32 = jnp.float32
    L = x.shape[1]
    n_rows = L // GRID_W
    rows = jnp.repeat(jnp.arange(n_rows), GRID_W).astype(f32)
    cols = jnp.tile(jnp.arange(GRID_W), n_rows).astype(f32)
    half = ATT_HD // 2
    inv = ROPE_BASE ** (-jnp.arange(0, half, 2, dtype=f32) / half)

    def rot(xa, pos):
        ang = pos[:, None] * inv[None, :]
        cos = jnp.cos(ang)[None, :, None, :]
        sin = jnp.sin(ang)[None, :, None, :]
        a1, a2 = jnp.split(xa, 2, axis=-1)
        return jnp.concatenate([a1 * cos - a2 * sin, a1 * sin + a2 * cos], axis=-1)

    xf = x.astype(f32)
    return jnp.concatenate([rot(xf[..., :half], rows), rot(xf[..., half:], cols)], axis=-1).astype(x.dtype)


def attend(qb, K, V, mask, sink):
    logits = jnp.einsum("bqkgd,bksd->bkgqs", qb, K, preferred_element_type=jnp.float32)
    if mask is not None:
        logits = jnp.where(mask, logits, NEG)
    sink_col = jnp.broadcast_to(sink.astype(jnp.float32).reshape(1, ATT_KV, ATT_G, 1, 1), logits.shape[:-1] + (1,))
    p = jax.nn.softmax(jnp.concatenate([sink_col, logits], axis=-1), axis=-1)[..., 1:]
    return jnp.einsum("bkgqs,bksd->bqkgd", p.astype(V.dtype), V)


def q_blocks(q):
    B, L = q.shape[:2]
    qb = q.reshape(B, L // QBLOCK, QBLOCK, ATT_KV, ATT_G, ATT_HD) * (1.0 / math.sqrt(ATT_HD))
    return jnp.moveaxis(qb, 1, 0)


def merge_blocks(out, B, L):
    return jnp.moveaxis(out, 0, 1).reshape(B, L, ATT_HEADS * ATT_HD)


def ctx_attention(q, k, v, sink):
    B, L = q.shape[:2]
    out = lax.map(lambda qi: attend(qi, k, v, None, sink), q_blocks(q))
    return merge_blocks(out, B, L)


def lat_attention(q, k, v, ck, cv, sink):
    B, L = q.shape[:2]
    P = ck.shape[2]
    span = QBLOCK + 2 * WINDOW
    kp = jnp.pad(k, ((0, 0), (0, 0), (WINDOW, WINDOW), (0, 0)))
    vp = jnp.pad(v, ((0, 0), (0, 0), (WINDOW, WINDOW), (0, 0)))
    qpos = jnp.arange(QBLOCK)[:, None]
    rel = jnp.arange(span)[None, :]
    ctx_mask = jnp.ones((QBLOCK, P), dtype=bool)

    def block(args):
        j, qi = args
        start = j * QBLOCK
        kw = lax.dynamic_slice_in_dim(kp, start, span, axis=2)
        vw = lax.dynamic_slice_in_dim(vp, start, span, axis=2)
        t_pos = start + qpos
        s_pos = start - WINDOW + rel
        win = (jnp.abs(t_pos - s_pos) <= WINDOW) & (s_pos >= 0) & (s_pos < L)
        Kc = jnp.concatenate([ck.astype(kw.dtype), kw], axis=2)
        Vc = jnp.concatenate([cv.astype(vw.dtype), vw], axis=2)
        return attend(qi, Kc, Vc, jnp.concatenate([ctx_mask, win], axis=1), sink)

    out = lax.map(block, (jnp.arange(L // QBLOCK), q_blocks(q)))
    return merge_blocks(out, B, L)


def moe_swiglu(x, w_router, b_router, w1, w3, w2):
    B, L, D = x.shape
    xt = x.reshape(B * L, D)
    logits = (xt @ w_router).astype(jnp.float32) + b_router.astype(jnp.float32)
    top_v, top_i = lax.top_k(logits, TOP_K)
    top_w = jax.nn.softmax(top_v, axis=-1)
    gates = jnp.sum(jax.nn.one_hot(top_i, N_EXPERTS, dtype=jnp.float32) * top_w[..., None], axis=1)
    out = jnp.zeros(xt.shape, jnp.float32)
    for e in range(N_EXPERTS):
        out = out + gates[:, e:e + 1] * swiglu(xt, w1[e], w3[e], w2[e]).astype(jnp.float32)
    return out.astype(x.dtype).reshape(B, L, D)


def setup_inputs(seed: int = 0) -> dict:
    key = jax.random.key(seed)
    keys = iter(jax.random.split(key, 48))
    nrm = lambda shape, scale=1.0: jax.random.normal(next(keys), shape, jnp.float32) * scale
    D = D_MODEL
    b_i = nrm((N_EVEN, N_DIR, 1, ML_HEADS), 0.1)
    b_f = jnp.linspace(3.0, 6.0, ML_HEADS, dtype=jnp.float32) + nrm((N_EVEN, N_DIR, 1, ML_HEADS), 0.1)
    return {
        "x_prompt": nrm((BATCH, SEQ, D)),
        "x_sample": nrm((DEC_BATCH, DEC_SEQ, D)),
        "state_C": nrm((DEC_BATCH, N_EVEN, N_DIR, ML_HEADS, ML_HD, ML_HD), 0.05),
        "state_n": nrm((DEC_BATCH, N_EVEN, N_DIR, ML_HEADS, ML_HD), 0.1),
        "state_m": nrm((DEC_BATCH, N_EVEN, N_DIR, ML_HEADS), 0.5),
        "cache_k": nrm((DEC_BATCH, N_ODD, ATT_KV, PAST_LEN, ATT_HD)),
        "cache_v": nrm((DEC_BATCH, N_ODD, ATT_KV, PAST_LEN, ATT_HD)),
        "c": nrm((DEC_BATCH, D)),
        "c_ctx": nrm((D,)),
        "norm1_g": 1.0 + nrm((DEPTH, D), 0.05),
        "norm2_g": 1.0 + nrm((DEPTH, D), 0.05),
        "w_ada": nrm((DEPTH, D, 6 * D), D ** -0.5),
        "b_ada": nrm((DEPTH, 6 * D), 0.01),
        "ev_w_in": nrm((N_EVEN, D, EVEN_IN), D ** -0.5),
        "ev_conv": nrm((N_EVEN, 3, 3 * HY_W), 0.5),
        "hy_w1": nrm((N_EVEN, HY_EMB, HY_FF), HY_EMB ** -0.5),
        "hy_b1": nrm((N_EVEN, HY_FF), 0.1),
        "hy_w2": nrm((N_EVEN, HY_FF, HY_FF), HY_FF ** -0.5),
        "hy_b2": nrm((N_EVEN, HY_FF), 0.1),
        "hy_w3": nrm((N_EVEN, HY_FF, HY_ORDER * N_DIR * HY_W), HY_FF ** -0.5),
        "hy_freq": 1.0 + nrm((N_EVEN, 2, HY_FF), 0.1),
        "hy_d": nrm((N_EVEN, HY_ORDER, HY_W), 0.5),
        "ml_b_gate": jnp.concatenate([b_i, b_f], axis=2),
        "ml_norm_g": 1.0 + nrm((N_EVEN, ML_HEADS, ML_HD), 0.05),
        "ev_w_out": nrm((N_EVEN, HY_W + ML_W, D), (HY_W + ML_W) ** -0.5),
        "ff_w1": nrm((N_EVEN, D, D_FF), D ** -0.5),
        "ff_w3": nrm((N_EVEN, D, D_FF), D ** -0.5),
        "ff_w2": nrm((N_EVEN, D_FF, D), D_FF ** -0.5),
        "at_w_qkv": nrm((N_ODD, D, ODD_IN), D ** -0.5),
        "at_q_g": 1.0 + nrm((N_ODD, ATT_HD), 0.05),
        "at_k_g": 1.0 + nrm((N_ODD, ATT_HD), 0.05),
        "at_sink": nrm((N_ODD, ATT_HEADS), 0.5),
        "at_w_out": nrm((N_ODD, ATT_HEADS * ATT_HD, D), (ATT_HEADS * ATT_HD) ** -0.5),
        "moe_w_router": nrm((N_ODD, D, N_EXPERTS), D ** -0.5),
        "moe_b_router": nrm((N_ODD, N_EXPERTS), 0.01),
        "moe_w1": nrm((N_ODD, N_EXPERTS, D, D_FF), D ** -0.5),
        "moe_w3": nrm((N_ODD, N_EXPERTS, D, D_FF), D ** -0.5),
        "moe_w2": nrm((N_ODD, N_EXPERTS, D_FF, D), D_FF ** -0.5),
    }


def reference(x_prompt, x_sample, state_C, state_n, state_m, cache_k, cache_v, c, c_ctx,
              norm1_g, norm2_g, w_ada, b_ada,
              ev_w_in, ev_conv, hy_w1, hy_b1, hy_w2, hy_b2, hy_w3, hy_freq, hy_d, ml_b_gate, ml_norm_g, ev_w_out,
              ff_w1, ff_w3, ff_w2,
              at_w_qkv, at_q_g, at_k_g, at_sink, at_w_out,
              moe_w_router, moe_b_router, moe_w1, moe_w3, moe_w2):
    f32 = jnp.float32
    xp, xs = x_prompt, x_sample
    to_kv = lambda a: jnp.transpose(a, (0, 2, 1, 3))
    out_C, out_n, out_m, out_k, out_v = [], [], [], [], []
    for layer in range(DEPTH):
        i = layer // 2
        p_sh1, p_sc1, p_g1, p_sh2, p_sc2, p_g2 = adaln(c_ctx, w_ada[layer], b_ada[layer])
        s_sh1, s_sc1, s_g1, s_sh2, s_sc2, s_g2 = adaln(c, w_ada[layer], b_ada[layer])
        hp = modulate(rmsnorm(xp, norm1_g[layer]), p_sh1, p_sc1)
        hs = modulate(rmsnorm(xs, norm1_g[layer]), s_sh1, s_sc1)
        if layer % 2 == 0:
            ev = (ev_w_in[i], ev_conv[i], hy_w1[i], hy_b1[i], hy_w2[i], hy_b2[i], hy_w3[i], hy_freq[i],
                  hy_d[i], ml_b_gate[i], ml_norm_g[i], ev_w_out[i])
            bp = xp.shape[0]
            C0 = jnp.zeros((bp, N_DIR, ML_HEADS, ML_HD, ML_HD), f32)
            n0 = jnp.zeros((bp, N_DIR, ML_HEADS, ML_HD), f32)
            m0 = jnp.zeros((bp, N_DIR, ML_HEADS), f32)
            yp, Cp, n_p, m_p = even_mixer(hp, *ev, C0, n0, m0)
            ys, _, _, _ = even_mixer(hs, *ev, state_C[:, i], state_n[:, i], state_m[:, i])
            out_C.append(Cp)
            out_n.append(n_p)
            out_m.append(m_p)
        else:
            qp, kp, vp = attn_qkv(hp, at_w_qkv[i], at_q_g[i], at_k_g[i])
            kp, vp = to_kv(kp), to_kv(vp)
            out_k.append(kp)
            out_v.append(vp)
            yp = ctx_attention(qp, kp, vp, at_sink[i]) @ at_w_out[i]
            qs, ks_, vs = attn_qkv(hs, at_w_qkv[i], at_q_g[i], at_k_g[i])
            ys = lat_attention(rope_2d(qs), to_kv(rope_2d(ks_)), to_kv(vs), cache_k[:, i], cache_v[:, i],
                               at_sink[i]) @ at_w_out[i]
        xp = xp + p_g1 * yp
        xs = xs + s_g1 * ys
        hp = modulate(rmsnorm(xp, norm2_g[layer]), p_sh2, p_sc2)
        hs = modulate(rmsnorm(xs, norm2_g[layer]), s_sh2, s_sc2)
        if layer % 2 == 0:
            fp = swiglu(hp, ff_w1[i], ff_w3[i], ff_w2[i])
            fs = swiglu(hs, ff_w1[i], ff_w3[i], ff_w2[i])
        else:
            moe = (moe_w_router[i], moe_b_router[i], moe_w1[i], moe_w3[i], moe_w2[i])
            fp = moe_swiglu(hp, *moe)
            fs = moe_swiglu(hs, *moe)
        xp = xp + p_g2 * fp
        xs = xs + s_g2 * fs
    return (xp, xs, jnp.stack(out_C, axis=1), jnp.stack(out_n, axis=1), jnp.stack(out_m, axis=1),
            jnp.stack(out_k, axis=1), jnp.stack(out_v, axis=1))
```

```python
import functools
import math

import numpy as np
import jax
import jax.numpy as jnp
from jax import lax
from jax.experimental import pallas as pl
from jax.experimental.pallas import tpu as pltpu

F32 = jnp.float32
BF16 = jnp.bfloat16
HIGHEST = lax.Precision.HIGHEST

D = 1024
GROUP = 1024
N_PROMPT_GROUPS = 8
HY_W = 512
ML_HEADS = 4
ML_HD = 128
ML_CHUNK = 256
EVEN_MAIN = 3 * HY_W + 4 * 512
N_GATES = 16
ATT_HD = 64
ATT_HEADS = 16
ATT_KV = 4
WINDOW = 128
GRID_W = 64
ROPE_BASE = 10000.0
D_FF = 2816
N_EXPERTS = 8
EPS = 1e-6
NEG = -1e30
VMEM_LIMIT = 56 * 1024 * 1024


def _cparams(*sem):
    return pltpu.CompilerParams(dimension_semantics=sem, vmem_limit_bytes=VMEM_LIMIT)


def _mod_row(i, tm):
    return jnp.maximum(i * tm // GROUP - (N_PROMPT_GROUPS - 1), 0)


def _silu(x):
    return x * jax.nn.sigmoid(x)


def _bdot(a, b):
    return jnp.dot(a.astype(BF16), b.astype(BF16), preferred_element_type=F32)


def _norm_mod(x, g, sh, sc):
    y = x * lax.rsqrt(jnp.mean(x * x, axis=-1, keepdims=True) + EPS) * g
    return y * (1.0 + sc) + sh


def _adaln_kernel(c_ref, w_ref, b_ref, o_ref):
    s = _silu(c_ref[...])
    o_ref[...] = jnp.dot(s, w_ref[...], precision=HIGHEST, preferred_element_type=F32) + b_ref[...]


def adaln_table(cond, w_ada, b_ada):
    depth = w_ada.shape[0]
    tn = 1536
    out = pl.pallas_call(
        _adaln_kernel,
        grid=(depth, 6 * D // tn),
        in_specs=[pl.BlockSpec((16, D), lambda l, j: (0, 0)),
                  pl.BlockSpec((None, D, tn), lambda l, j: (l, 0, j)),
                  pl.BlockSpec((None, 1, tn), lambda l, j: (l, 0, j))],
        out_specs=pl.BlockSpec((None, 16, tn), lambda l, j: (l, 0, j)),
        out_shape=jax.ShapeDtypeStruct((depth, 16, 6 * D), F32),
        compiler_params=_cparams("parallel", "parallel"),
        name="adaln",
    )(cond, w_ada, b_ada.reshape(depth, 1, 6 * D))
    return out.reshape(depth, 16, 1, 6 * D)


def _mod_spec(k, tm):
    return pl.BlockSpec((None, 1, D), lambda i, *_: (_mod_row(i, tm), 0, k))


def _even_in_kernel(x_ref, g_ref, sh_ref, sc_ref, w_ref, wg_ref, bg_ref, u_ref, gate_ref, h_scr):
    @pl.when(pl.program_id(1) == 0)
    def _():
        h = _norm_mod(x_ref[...], g_ref[...], sh_ref[...], sc_ref[...]).astype(BF16)
        h_scr[...] = h
        gate_ref[...] = _bdot(h, wg_ref[...]) + bg_ref[...]

    u_ref[...] = _bdot(h_scr[...], w_ref[...]).astype(u_ref.dtype)


def even_in_proj(x, g, mods, w_in, b_gate, tm=1024, tn=512):
    T = x.shape[0]
    return pl.pallas_call(
        _even_in_kernel,
        grid=(T // tm, EVEN_MAIN // tn),
        in_specs=[pl.BlockSpec((tm, D), lambda i, j: (i, 0)),
                  pl.BlockSpec((1, D), lambda i, j: (0, 0)),
                  _mod_spec(0, tm), _mod_spec(1, tm),
                  pl.BlockSpec((D, tn), lambda i, j: (0, j)),
                  pl.BlockSpec((D, N_GATES), lambda i, j: (0, 0)),
                  pl.BlockSpec((1, N_GATES), lambda i, j: (0, 0))],
        out_specs=[pl.BlockSpec((tm, tn), lambda i, j: (i, j)),
                   pl.BlockSpec((tm, N_GATES), lambda i, j: (i, 0))],
        out_shape=[jax.ShapeDtypeStruct((T, EVEN_MAIN), BF16),
                   jax.ShapeDtypeStruct((T, N_GATES), F32)],
        scratch_shapes=[pltpu.VMEM((tm, D), BF16)],
        compiler_params=_cparams("parallel", "arbitrary"),
        name="even_in_proj",
    )(x, g, mods, mods, w_in, w_in[:, EVEN_MAIN:], b_gate)


def _dft_tables(L):
    n = 2 * L
    f = np.arange(L, dtype=np.int64)[:, None]
    s = np.arange(L, dtype=np.int64)[None, :]
    ang = 2.0 * np.pi * ((f * s) % n).astype(np.float64) / n
    fwd = np.concatenate([np.cos(ang), -np.sin(ang)], axis=0)
    fwd[L, :] = np.where(np.arange(L) % 2 == 0, 1.0, -1.0)
    t = np.arange(L, dtype=np.int64)[:, None]
    ff = np.arange(L, dtype=np.int64)[None, :]
    ang = 2.0 * np.pi * ((t * ff) % n).astype(np.float64) / n
    inv_re = 2.0 * np.cos(ang) / n
    inv_re[:, 0] = 1.0 / n
    inv_im = -2.0 * np.sin(ang) / n
    inv_im[:, 0] = np.where(np.arange(L) % 2 == 0, 1.0, -1.0) / n
    inv = np.concatenate([inv_re, inv_im], axis=1)
    return fwd.astype(np.float32), inv.astype(np.float32)


def _filter_tables(L):
    t = np.linspace(0.0, 1.0, L, dtype=np.float32).astype(np.float64)[:, None]
    w = 2.0 * math.pi * np.arange(L, dtype=np.float64)[:, None] / L
    bands = np.linspace(1e-4, 16 - 1, 16, dtype=np.float32).astype(np.float64)[None, :]
    z = np.concatenate([t, np.cos(bands * w), -np.sin(bands * w)], axis=-1)
    zp = np.zeros((L, 128), np.float64)
    zp[:, :z.shape[1]] = z
    max_decay = math.log(1e-2) / 0.3
    min_decay = math.log(1e-2) / 1.5
    deltas = np.linspace(min_decay, max_decay, HY_W, dtype=np.float32).astype(np.float64)
    decay = np.exp(-t * np.abs(deltas))
    return zp.astype(np.float32), decay.astype(np.float32)


def _hy_filter_kernel(z_ref, dec_ref, w1_ref, b1_ref, w2_ref, b2_ref, w3_ref, fr_ref, fwd_ref,
                      ka_ref, kb_ref):
    L = z_ref.shape[0]
    hdot = functools.partial(jnp.dot, precision=HIGHEST, preferred_element_type=F32)
    h = jnp.sin(fr_ref[0:1, :] * (hdot(z_ref[...], w1_ref[...]) + b1_ref[...]))
    h = jnp.sin(fr_ref[1:2, :] * (hdot(h, w2_ref[...]) + b2_ref[...]))
    h = hdot(h, w3_ref[...])
    row0 = lax.broadcasted_iota(jnp.int32, (L, 1), 0) == 0
    h0 = h[:, :HY_W] * dec_ref[...]
    h1 = h[:, HY_W:] * dec_ref[...]
    l1 = jnp.sum(jnp.abs(h0), axis=0, keepdims=True) + jnp.sum(jnp.abs(h1), axis=0, keepdims=True)
    inv = 1.0 / l1
    h0 = h0 * inv
    h1 = jnp.where(row0, 0.0, h1 * inv)
    f0 = _bdot(fwd_ref[...], h0)
    f1 = _bdot(fwd_ref[...], h1)
    ka_ref[...] = f0[:L] + f1[:L]
    kb_ref[...] = jnp.where(row0, f0[L:] + f1[L:], f0[L:] - f1[L:])


def _const_spec(a, n_grid):
    return pl.BlockSpec(a.shape, lambda *_: (0,) * a.ndim, pipeline_mode=pl.Buffered(1))


def hyena_filter_spectra(L, w1, b1, w2, b2, w3, freq, fwd):
    z, dec = _filter_tables(L)
    pad2 = lambda a, r, c: jnp.pad(a, ((0, r - a.shape[0]), (0, c - a.shape[1])))
    args = (jnp.asarray(z), jnp.asarray(dec), pad2(w1, 128, 128), pad2(b1[None], 1, 128),
            pad2(w2, 128, 128), pad2(b2[None], 1, 128), pad2(w3, 128, 4 * HY_W), pad2(freq, 2, 128), fwd)
    in_specs = [_const_spec(a, 1) for a in args]
    in_specs[6] = pl.BlockSpec((128, 2 * HY_W), lambda o: (0, o))
    shp = jax.ShapeDtypeStruct((2, L, HY_W), F32)
    out_spec = pl.BlockSpec((None, L, HY_W), lambda o: (o, 0, 0))
    return pl.pallas_call(
        _hy_filter_kernel,
        grid=(2,),
        in_specs=in_specs,
        out_specs=[out_spec, out_spec],
        out_shape=[shp, shp],
        compiler_params=_cparams("arbitrary"),
        name=f"hyena_filter_{L}",
    )(*args)


def _hyena_kernel(u_ref, cw_ref, d_ref, fwd_ref, inv_ref, ka_ref, kb_ref, o_ref):
    nb, L = u_ref.shape[0], u_ref.shape[1]
    row = lax.broadcasted_iota(jnp.int32, (L, 1), 0)
    first, last = row == 0, row == L - 1
    fwd = fwd_ref[...].astype(BF16)
    inv = inv_ref[...].astype(BF16)

    def long_conv(z, o):
        zf = jnp.dot(fwd, z.astype(BF16), preferred_element_type=F32)
        a, b = zf[:L], zf[L:]
        ka, kb = ka_ref[o], kb_ref[o]
        yr = a * ka - jnp.where(first, 0.0, b * kb)
        yi = jnp.where(first, b * kb, a * kb + b * ka)
        return (jnp.dot(inv[:, :L], yr.astype(BF16), preferred_element_type=F32)
                + jnp.dot(inv[:, L:], yi.astype(BF16), preferred_element_type=F32))

    for bi in range(nb):
        u = u_ref[bi].astype(F32)
        prev = jnp.where(first, 0.0, pltpu.roll(u, 1, 0))
        nxt = jnp.where(last, 0.0, pltpu.roll(u, L - 1, 0))
        u = prev * cw_ref[0:1, :] + u * cw_ref[1:2, :] + nxt * cw_ref[2:3, :]
        v, x1, x2 = u[:, :HY_W], u[:, HY_W:2 * HY_W], u[:, 2 * HY_W:]
        z = x1 * (long_conv(v, 0) + d_ref[0:1, :] * v)
        z = x2 * (long_conv(z, 1) + d_ref[1:2, :] * z)
        o_ref[bi] = z.astype(o_ref.dtype)


def hyena_mix(u3, conv_w, d_skip, fwd, inv, ka, kb, nb):
    B, L, _ = u3.shape
    full = lambda a: _const_spec(a, 1)
    return pl.pallas_call(
        _hyena_kernel,
        grid=(B // nb,),
        in_specs=[pl.BlockSpec((nb, L, 3 * HY_W), lambda b: (b, 0, 0)),
                  full(conv_w), full(d_skip), full(fwd), full(inv), full(ka), full(kb)],
        out_specs=pl.BlockSpec((nb, L, HY_W), lambda b: (b, 0, 0)),
        out_shape=jax.ShapeDtypeStruct((B, L, HY_W), BF16),
        compiler_params=_cparams("parallel"),
        name=f"hyena_{L}",
    )(u3, conv_w, d_skip, fwd, inv, ka, kb)


def _log_sigmoid(x):
    return jnp.minimum(x, 0.0) - jnp.log(1.0 + jnp.exp(-jnp.abs(x)))


def _mlstm_chunk(q, ks, v, li_c, lf_c, li_r, lf_r, C, n, m, rev):
    T = q.shape[0]
    ri = lax.broadcasted_iota(jnp.int32, (T, T), 0)
    ci = lax.broadcasted_iota(jnp.int32, (T, T), 1)
    lo = (ci <= ri).astype(F32)
    up = (ci >= ri).astype(F32)
    hdot = functools.partial(jnp.dot, precision=HIGHEST, preferred_element_type=F32)
    if rev:
        mask = ci >= ri
        b_c, b_r = hdot(up, lf_c), hdot(lf_r, lo)
        b_end = b_c[0:1, :]
    else:
        mask = ci <= ri
        b_c, b_r = hdot(lo, lf_c), hdot(lf_r, up)
        b_end = b_c[T - 1:T, :]
    dm = jnp.where(mask, b_c - b_r + li_r, NEG)
    inter = b_c + m
    m_t = jnp.maximum(inter, jnp.max(dm, axis=1, keepdims=True))
    w_intra = jnp.exp(dm - m_t)
    w_inter = jnp.exp(inter - m_t)
    s = lax.dot_general(q, ks, (((1,), (1,)), ((), ())), preferred_element_type=F32) * w_intra
    num = _bdot(s, v) + w_inter * _bdot(q, C)
    den = jnp.sum(s, axis=1, keepdims=True) + w_inter * jnp.sum(q.astype(F32) * n, axis=1, keepdims=True)
    h = num / jnp.maximum(jnp.abs(den), jnp.exp(-m_t))
    g_c = b_end - b_c + li_c
    g_r = b_end - b_r + li_r
    m_new = jnp.maximum(b_end + m, jnp.max(g_r, axis=1, keepdims=True))
    decay = jnp.exp(b_end + m - m_new)
    kw = ks.astype(F32) * jnp.exp(g_c - m_new)
    C_new = decay * C + lax.dot_general(kw.astype(BF16), v, (((0,), (0,)), ((), ())),
                                        preferred_element_type=F32)
    n_new = decay * n + jnp.sum(kw, axis=0, keepdims=True)
    return h, C_new, n_new, m_new


def _mlstm_kernel(*refs, has_state, want_state):
    q_ref, k_ref, v_ref, o_ref, gc_ref, gr_ref, ng_ref = refs[:7]
    refs = refs[7:]
    if has_state:
        c0_ref, n0_ref, m0_ref = refs[:3]
        refs = refs[3:]
    y_ref = refs[0]
    if want_state:
        c_out, n_out, m_out = refs[1:4]
    L = q_ref.shape[0]
    T = min(ML_CHUNK, L)
    nc = L // T
    scale = 1.0 / math.sqrt(ML_HD)
    hs = [None] * nc
    for dr in range(2):
        if has_state:
            C, n, m = c0_ref[dr], n0_ref[dr:dr + 1, :], m0_ref[dr:dr + 1, 0:1]
        else:
            C, n, m = jnp.zeros((ML_HD, ML_HD), F32), jnp.zeros((1, ML_HD), F32), jnp.zeros((1, 1), F32)
        order = range(nc) if dr == 0 else range(nc - 1, -1, -1)
        for j in order:
            sl = slice(j * T, (j + 1) * T)
            q = q_ref[sl, :]
            ks = (k_ref[sl, :].astype(F32) * scale).astype(BF16)
            v = v_ref[sl, :]
            li_c = gc_ref[sl, 2 * dr:2 * dr + 1]
            lf_c = _log_sigmoid(gc_ref[sl, 2 * dr + 1:2 * dr + 2])
            li_r = gr_ref[2 * dr:2 * dr + 1, sl]
            lf_r = _log_sigmoid(gr_ref[2 * dr + 1:2 * dr + 2, sl])
            h, C, n, m = _mlstm_chunk(q, ks, v, li_c, lf_c, li_r, lf_r, C, n, m, rev=(dr == 1))
            hs[j] = h if hs[j] is None else hs[j] + h
        if want_state:
            c_out[dr] = C
            n_out[dr:dr + 1, :] = n
            m_out[dr:dr + 1, :] = jnp.broadcast_to(m, (1, ML_HD))
    for j in range(nc):
        sl = slice(j * T, (j + 1) * T)
        h = hs[j]
        y = h * lax.rsqrt(jnp.mean(h * h, axis=-1, keepdims=True) + EPS) * ng_ref[...]
        y_ref[sl, :] = (y * jax.nn.sigmoid(o_ref[sl, :].astype(F32))).astype(y_ref.dtype)


def mlstm_mix(u3, gates, norm_g, state=None, want_state=False):
    B, L, _ = u3.shape
    g5 = gates.reshape(B, L, 2, 2, ML_HEADS)
    gc = g5.transpose(0, 4, 1, 2, 3).reshape(B, ML_HEADS, L, 4)
    gr = g5.transpose(0, 4, 2, 3, 1).reshape(B, ML_HEADS, 4, L)
    col = lambda i: pl.BlockSpec((None, L, ML_HD), lambda b, h: (b, 0, (3 * HY_W + i * 512) // ML_HD + h))
    in_specs = [col(0), col(1), col(2), col(3),
                pl.BlockSpec((None, None, L, 4), lambda b, h: (b, h, 0, 0)),
                pl.BlockSpec((None, None, 4, L), lambda b, h: (b, h, 0, 0)),
                pl.BlockSpec((None, 1, ML_HD), lambda b, h: (h, 0, 0))]
    args = [u3, u3, u3, u3, gc, gr, norm_g.reshape(ML_HEADS, 1, ML_HD)]
    if state is not None:
        C0, n0, m0 = state
        in_specs += [pl.BlockSpec((None, 2, None, ML_HD, ML_HD), lambda b, h: (b, 0, h, 0, 0)),
                     pl.BlockSpec((None, None, 2, ML_HD), lambda b, h: (b, h, 0, 0)),
                     pl.BlockSpec((None, None, 2, ML_HD), lambda b, h: (b, h, 0, 0))]
        args += [C0, n0.transpose(0, 2, 1, 3),
                 jnp.broadcast_to(m0.transpose(0, 2, 1)[..., None], (B, ML_HEADS, 2, ML_HD))]
    out_specs = [pl.BlockSpec((None, L, ML_HD), lambda b, h: (b, 0, h))]
    out_shape = [jax.ShapeDtypeStruct((B, L, ML_HEADS * ML_HD), BF16)]
    if want_state:
        out_specs += [pl.BlockSpec((None, 2, None, ML_HD, ML_HD), lambda b, h: (b, 0, h, 0, 0)),
                      pl.BlockSpec((None, None, 2, ML_HD), lambda b, h: (b, h, 0, 0)),
                      pl.BlockSpec((None, None, 2, ML_HD), lambda b, h: (b, h, 0, 0))]
        out_shape += [jax.ShapeDtypeStruct((B, 2, ML_HEADS, ML_HD, ML_HD), F32),
                      jax.ShapeDtypeStruct((B, ML_HEADS, 2, ML_HD), F32),
                      jax.ShapeDtypeStruct((B, ML_HEADS, 2, ML_HD), F32)]
    outs = pl.pallas_call(
        functools.partial(_mlstm_kernel, has_state=state is not None, want_state=want_state),
        grid=(B, ML_HEADS),
        in_specs=in_specs, out_specs=out_specs, out_shape=out_shape,
        compiler_params=_cparams("parallel", "parallel"),
        name=f"mlstm_{L}",
    )(*args)
    if not want_state:
        return outs[0]
    y, C, n, m = outs
    return y, C, n.transpose(0, 2, 1, 3), m[..., 0].transpose(0, 2, 1)


def _proj_res_kernel(*refs, n_in):
    a_refs = refs[:n_in]
    w_ref, x_ref, gate_ref, o_ref = refs[n_in:]
    k0 = 0
    acc = None
    for a_ref in a_refs:
        kw = a_ref.shape[1]
        part = _bdot(a_ref[...], w_ref[k0:k0 + kw, :])
        acc = part if acc is None else acc + part
        k0 += kw
    o_ref[...] = x_ref[...] + gate_ref[...] * acc


def proj_residual(acts, w, x, mods, gate_idx, tm=1024):
    T = x.shape[0]
    return pl.pallas_call(
        functools.partial(_proj_res_kernel, n_in=len(acts)),
        grid=(T // tm,),
        in_specs=[pl.BlockSpec((tm, a.shape[1]), lambda i: (i, 0)) for a in acts]
        + [pl.BlockSpec(w.shape, lambda i: (0, 0)),
           pl.BlockSpec((tm, D), lambda i: (i, 0)),
           _mod_spec(gate_idx, tm)],
        out_specs=pl.BlockSpec((tm, D), lambda i: (i, 0)),
        out_shape=jax.ShapeDtypeStruct((T, D), F32),
        compiler_params=_cparams("parallel"),
        name="proj_residual",
    )(*acts, w, x, mods)


def _ffn_kernel(x_ref, g_ref, sh_ref, sc_ref, gate_ref, w1_ref, w3_ref, w2_ref, o_ref, h_scr, acc_scr):
    c = pl.program_id(1)

    @pl.when(c == 0)
    def _():
        h_scr[...] = _norm_mod(x_ref[...], g_ref[...], sh_ref[...], sc_ref[...]).astype(BF16)
        acc_scr[...] = jnp.zeros_like(acc_scr)

    h = h_scr[...]
    mid = _silu(_bdot(h, w1_ref[...])) * _bdot(h, w3_ref[...])
    acc_scr[...] += _bdot(mid, w2_ref[...])

    @pl.when(c == pl.num_programs(1) - 1)
    def _():
        o_ref[...] = x_ref[...] + gate_ref[...] * acc_scr[...]


def ffn_residual(x, g, mods, w1, w3, w2, tm=1024, tf=256):
    T = x.shape[0]
    return pl.pallas_call(
        _ffn_kernel,
        grid=(T // tm, D_FF // tf),
        in_specs=[pl.BlockSpec((tm, D), lambda i, c: (i, 0)),
                  pl.BlockSpec((1, D), lambda i, c: (0, 0)),
                  _mod_spec(3, tm), _mod_spec(4, tm), _mod_spec(5, tm),
                  pl.BlockSpec((D, tf), lambda i, c: (0, c)),
                  pl.BlockSpec((D, tf), lambda i, c: (0, c)),
                  pl.BlockSpec((tf, D), lambda i, c: (c, 0))],
        out_specs=pl.BlockSpec((tm, D), lambda i, c: (i, 0)),
        out_shape=jax.ShapeDtypeStruct((T, D), F32),
        scratch_shapes=[pltpu.VMEM((tm, D), BF16), pltpu.VMEM((tm, D), F32)],
        compiler_params=_cparams("parallel", "arbitrary"),
        name="ffn",
    )(x, g, mods, mods, mods, w1, w3, w2)


def _qkv_kernel(x_ref, g_ref, sh_ref, sc_ref, w_ref, q_ref, kv_ref, h_scr):
    j = pl.program_id(1)

    @pl.when(j == 0)
    def _():
        h_scr[...] = _norm_mod(x_ref[...], g_ref[...], sh_ref[...], sc_ref[...]).astype(BF16)

    @pl.when(j < 2)
    def _():
        q_ref[...] = _bdot(h_scr[...], w_ref[...]).astype(q_ref.dtype)

    @pl.when(j == 2)
    def _():
        kv_ref[...] = _bdot(h_scr[...], w_ref[...])


def qkv_proj(x, g, mods, w_qkv, tm=1024):
    T = x.shape[0]
    tn = 512
    return pl.pallas_call(
        _qkv_kernel,
        grid=(T // tm, 3),
        in_specs=[pl.BlockSpec((tm, D), lambda i, j: (i, 0)),
                  pl.BlockSpec((1, D), lambda i, j: (0, 0)),
                  _mod_spec(0, tm), _mod_spec(1, tm),
                  pl.BlockSpec((D, tn), lambda i, j: (0, j))],
        out_specs=[pl.BlockSpec((tm, tn), lambda i, j: (i, jnp.minimum(j, 1))),
                   pl.BlockSpec((tm, tn), lambda i, j: (i, 0))],
        out_shape=[jax.ShapeDtypeStruct((T, ATT_HEADS * ATT_HD), BF16),
                   jax.ShapeDtypeStruct((T, 2 * ATT_KV * ATT_HD), F32)],
        scratch_shapes=[pltpu.VMEM((tm, D), BF16)],
        compiler_params=_cparams("parallel", "arbitrary"),
        name="qkv_proj",
    )(x, g, mods, mods, w_qkv)


def _rope_tables(L):
    half = ATT_HD // 2
    pos_r = (np.arange(L) // GRID_W).astype(np.float32)
    pos_c = (np.arange(L) % GRID_W).astype(np.float32)
    inv = (ROPE_BASE ** (-np.arange(0, half, 2, dtype=np.float32) / half)).astype(np.float32)
    cos = np.zeros((L, ATT_HD), np.float64)
    sin = np.zeros((L, ATT_HD), np.float64)
    for base, pos in ((0, pos_r), (half, pos_c)):
        ang = (pos[:, None] * inv[None, :]).astype(np.float32).astype(np.float64)
        cos[:, base:base + half] = np.concatenate([np.cos(ang), np.cos(ang)], axis=1)
        sin[:, base:base + half] = np.concatenate([-np.sin(ang), np.sin(ang)], axis=1)
    return (np.tile(cos, (1, 4)).astype(np.float32), np.tile(sin, (1, 4)).astype(np.float32))


def _seg_rms(x):
    w = x.shape[1]
    ri = lax.broadcasted_iota(jnp.int32, (w, w), 0) // ATT_HD
    ci = lax.broadcasted_iota(jnp.int32, (w, w), 1) // ATT_HD
    bd = (ri == ci).astype(F32)
    ss = jnp.dot(x * x, bd, precision=HIGHEST, preferred_element_type=F32)
    return x * lax.rsqrt(ss * (1.0 / ATT_HD) + EPS)


def _swap16(x):
    w = x.shape[1]
    lane = lax.broadcasted_iota(jnp.int32, x.shape, 1)
    return jnp.where(lane % 32 < 16, pltpu.roll(x, w - 16, 1), pltpu.roll(x, 16, 1))


def _attn_kernel(*refs, latent, tq):
    if latent:
        (q_ref, kv_ref, ck_ref, cv_ref, qg_ref, kg_ref, sink_ref, cosq_ref, sinq_ref, cosk_ref, sink_t_ref,
         o_ref, k_scr, v_scr, ck_scr, cv_scr) = refs
    else:
        q_ref, kv_ref, qg_ref, kg_ref, sink_ref, o_ref, ko_ref, vo_ref, k_scr, v_scr = refs
    L = kv_ref.shape[0]
    gw = ATT_KV * ATT_HD
    qb = pl.program_id(1)

    @pl.when(qb == 0)
    def _():
        kn = _seg_rms(kv_ref[:, :gw]) * kg_ref[...]
        v = kv_ref[:, gw:]
        if latent:
            kn = kn * cosk_ref[...] + _swap16(kn) * sink_t_ref[...]
            for c in range(ATT_KV):
                ck_scr[c] = ck_ref[c].astype(BF16)
                cv_scr[c] = cv_ref[c].astype(BF16)
        else:
            for c in range(ATT_KV):
                ko_ref[c] = kn[:, c * ATT_HD:(c + 1) * ATT_HD]
                vo_ref[c] = v[:, c * ATT_HD:(c + 1) * ATT_HD]
        k_scr[...] = kn.astype(BF16)
        v_scr[...] = v.astype(BF16)

    if latent:
        span = tq + 2 * WINDOW
        start = pl.multiple_of(jnp.clip(qb * tq - WINDOW, 0, L - span), WINDOW)
        t_pos = qb * tq + lax.broadcasted_iota(jnp.int32, (tq, span), 0)
        s_pos = start + lax.broadcasted_iota(jnp.int32, (tq, span), 1)
        win = jnp.abs(t_pos - s_pos) <= WINDOW
        kw = k_scr[pl.ds(start, span), :]
        vw = v_scr[pl.ds(start, span), :]
    else:
        kw, vw = k_scr[...], v_scr[...]

    nt = (((1,), (1,)), ((), ()))
    for c in range(ATT_KV):
        qc = _seg_rms(q_ref[:, c * gw:(c + 1) * gw].astype(F32)) * qg_ref[...]
        if latent:
            qc = qc * cosq_ref[...] + _swap16(qc) * sinq_ref[...]
        qc = (qc * (1.0 / math.sqrt(ATT_HD))).astype(BF16)
        hsl = slice(c * ATT_HD, (c + 1) * ATT_HD)
        kc, vc = kw[:, hsl], vw[:, hsl]
        outs = []
        for g in range(ATT_KV):
            qh = qc[:, g * ATT_HD:(g + 1) * ATT_HD]
            sink = sink_ref[:, c * ATT_KV + g:c * ATT_KV + g + 1]
            lw = lax.dot_general(qh, kc, nt, preferred_element_type=F32)
            if latent:
                lw = jnp.where(win, lw, NEG)
                lc = lax.dot_general(qh, ck_scr[c], nt, preferred_element_type=F32)
                mx = jnp.maximum(jnp.maximum(jnp.max(lw, axis=1, keepdims=True),
                                             jnp.max(lc, axis=1, keepdims=True)), sink)
                pw, pc = jnp.exp(lw - mx), jnp.exp(lc - mx)
                den = (jnp.sum(pw, axis=1, keepdims=True) + jnp.sum(pc, axis=1, keepdims=True)
                       + jnp.exp(sink - mx))
                o = _bdot(pw, vc) + _bdot(pc, cv_scr[c])
            else:
                mx = jnp.maximum(jnp.max(lw, axis=1, keepdims=True), sink)
                pw = jnp.exp(lw - mx)
                den = jnp.sum(pw, axis=1, keepdims=True) + jnp.exp(sink - mx)
                o = _bdot(pw, vc)
            outs.append(o / den)
        o_ref[:, c * gw:(c + 1) * gw] = jnp.concatenate(outs, axis=1).astype(o_ref.dtype)


def attention(q, kv, q_g, k_g, sink, B, L, cache=None, tq=256):
    latent = cache is not None
    gw = ATT_KV * ATT_HD
    qg = jnp.tile(q_g, ATT_KV)[None]
    kg = jnp.tile(k_g, ATT_KV)[None]
    nq = L // tq
    const = lambda a: pl.BlockSpec(a.shape, lambda b, i: (0,) * a.ndim)
    in_specs = [pl.BlockSpec((tq, ATT_HEADS * ATT_HD), lambda b, i: (b * nq + i, 0)),
                pl.BlockSpec((L, 2 * gw), lambda b, i: (b, 0))]
    args = [q, kv]
    scratch = [pltpu.VMEM((L, gw), BF16), pltpu.VMEM((L, gw), BF16)]
    out_specs = [pl.BlockSpec((tq, ATT_HEADS * ATT_HD), lambda b, i: (b * nq + i, 0))]
    out_shape = [jax.ShapeDtypeStruct((B * L, ATT_HEADS * ATT_HD), BF16)]
    if latent:
        ck, cv = cache
        P = ck.shape[2]
        cos, sin = (jnp.asarray(t) for t in _rope_tables(L))
        in_specs += [pl.BlockSpec((None, ATT_KV, P, ATT_HD), lambda b, i: (b, 0, 0, 0))] * 2
        args += [ck, cv]
        in_specs += [const(qg), const(kg), pl.BlockSpec((1, ATT_HEADS), lambda b, i: (0, 0)),
                     pl.BlockSpec((tq, gw), lambda b, i: (i, 0)), pl.BlockSpec((tq, gw), lambda b, i: (i, 0)),
                     const(cos), const(sin)]
        args += [qg, kg, sink[None], cos, sin, cos, sin]
        scratch += [pltpu.VMEM((ATT_KV, P, ATT_HD), BF16), pltpu.VMEM((ATT_KV, P, ATT_HD), BF16)]
    else:
        in_specs += [const(qg), const(kg), pl.BlockSpec((1, ATT_HEADS), lambda b, i: (0, 0))]
        args += [qg, kg, sink[None]]
        cache_spec = pl.BlockSpec((None, ATT_KV, L, ATT_HD), lambda b, i: (b, 0, 0, 0))
        out_specs += [cache_spec, cache_spec]
        out_shape += [jax.ShapeDtypeStruct((B, ATT_KV, L, ATT_HD), F32)] * 2
    outs = pl.pallas_call(
        functools.partial(_attn_kernel, latent=latent, tq=tq),
        grid=(B, nq),
        in_specs=in_specs, out_specs=out_specs, out_shape=out_shape,
        scratch_shapes=scratch,
        compiler_params=_cparams("parallel", "arbitrary"),
        name="attn_latent" if latent else "attn_context",
    )(*args)
    return outs[0] if latent else outs


def _router_kernel(x_ref, g_ref, sh_ref, sc_ref, wr_ref, br_ref, h_ref, gates_ref):
    h = _norm_mod(x_ref[...], g_ref[...], sh_ref[...], sc_ref[...])
    h_ref[...] = h.astype(h_ref.dtype)
    lg = lax.dot_general(wr_ref[...], h, (((1,), (1,)), ((), ())), precision=HIGHEST,
                         preferred_element_type=F32) + br_ref[...]
    row = lax.broadcasted_iota(jnp.int32, lg.shape, 0)
    m1 = jnp.max(lg, axis=0, keepdims=True)
    i1 = jnp.min(jnp.where(lg == m1, row, N_EXPERTS), axis=0, keepdims=True)
    l2 = jnp.where(row == i1, -jnp.inf, lg)
    m2 = jnp.max(l2, axis=0, keepdims=True)
    i2 = jnp.min(jnp.where(l2 == m2, row, N_EXPERTS), axis=0, keepdims=True)
    e2 = jnp.exp(m2 - m1)
    w1 = 1.0 / (1.0 + e2)
    gates_ref[...] = jnp.where(row == i1, w1, 0.0) + jnp.where(row == i2, e2 * w1, 0.0)


def moe_router(x, g, mods, w_router, b_router, tm=1024):
    T = x.shape[0]
    return pl.pallas_call(
        _router_kernel,
        grid=(T // tm,),
        in_specs=[pl.BlockSpec((tm, D), lambda i: (i, 0)),
                  pl.BlockSpec((1, D), lambda i: (0, 0)),
                  _mod_spec(3, tm), _mod_spec(4, tm),
                  pl.BlockSpec((N_EXPERTS, D), lambda i: (0, 0)),
                  pl.BlockSpec((N_EXPERTS, 1), lambda i: (0, 0))],
        out_specs=[pl.BlockSpec((tm, D), lambda i: (i, 0)),
                   pl.BlockSpec((N_EXPERTS, tm), lambda i: (0, i))],
        out_shape=[jax.ShapeDtypeStruct((T, D), BF16),
                   jax.ShapeDtypeStruct((N_EXPERTS, T), F32)],
        compiler_params=_cparams("parallel"),
        name="moe_router",
    )(x, g, mods, mods, w_router.T, b_router[:, None])


def _moe_dense_kernel(h_ref, x_ref, gw_ref, gate_ref, w1_ref, w3_ref, w2_ref, o_ref, acc_scr):
    e, c = pl.program_id(1), pl.program_id(2)

    @pl.when((e == 0) & (c == 0))
    def _():
        acc_scr[...] = jnp.zeros_like(acc_scr)

    h = h_ref[...]
    mid = _silu(_bdot(h, w1_ref[...])) * _bdot(h, w3_ref[...]) * gw_ref[...]
    acc_scr[...] += _bdot(mid, w2_ref[...])

    @pl.when((e == pl.num_programs(1) - 1) & (c == pl.num_programs(2) - 1))
    def _():
        o_ref[...] = x_ref[...] + gate_ref[...] * acc_scr[...]


def moe_dense_residual(h, x, gates, mods, w1, w3, w2, tm=1024, tf=256):
    T = x.shape[0]
    return pl.pallas_call(
        _moe_dense_kernel,
        grid=(T // tm, N_EXPERTS, D_FF // tf),
        in_specs=[pl.BlockSpec((tm, D), lambda i, e, c: (i, 0)),
                  pl.BlockSpec((tm, D), lambda i, e, c: (i, 0)),
                  pl.BlockSpec((None, tm, 1), lambda i, e, c: (e, i, 0)),
                  _mod_spec(5, tm),
                  pl.BlockSpec((None, D, tf), lambda i, e, c: (e, 0, c)),
                  pl.BlockSpec((None, D, tf), lambda i, e, c: (e, 0, c)),
                  pl.BlockSpec((None, tf, D), lambda i, e, c: (e, c, 0))],
        out_specs=pl.BlockSpec((tm, D), lambda i, e, c: (i, 0)),
        out_shape=jax.ShapeDtypeStruct((T, D), F32),
        scratch_shapes=[pltpu.VMEM((tm, D), F32)],
        compiler_params=_cparams("parallel", "arbitrary", "arbitrary"),
        name="moe_dense",
    )(h, x, gates[:, :, None], mods, w1, w3, w2)


def kernel(x_prompt, x_sample, state_C, state_n, state_m, cache_k, cache_v, c, c_ctx, norm1_g, norm2_g, w_ada, b_ada, ev_w_in, ev_conv, hy_w1, hy_b1, hy_w2, hy_b2, hy_w3, hy_freq, hy_d, ml_b_gate, ml_norm_g, ev_w_out, ff_w1, ff_w3, ff_w2, at_w_qkv, at_q_g, at_k_g, at_sink, at_w_out, moe_w_router, moe_b_router, moe_w1, moe_w3, moe_w2):
    BP, LP, _ = x_prompt.shape
    BS, LS, _ = x_sample.shape
    TP = BP * LP
    assert TP % GROUP == 0 and TP // GROUP == N_PROMPT_GROUPS and LS == GROUP and BS == 8

    x = jnp.concatenate([x_prompt.reshape(TP, D), x_sample.reshape(BS * LS, D)], axis=0)
    cond = jnp.concatenate([c_ctx[None], c, jnp.zeros((16 - 1 - BS, D), F32)], axis=0)
    mods = adaln_table(cond, w_ada, b_ada)

    u, gates = even_in_proj(x, norm1_g[0:1], mods[0], ev_w_in[0], ml_b_gate[0].reshape(1, N_GATES))
    up = u[:TP].reshape(BP, LP, EVEN_MAIN)
    us = u[TP:].reshape(BS, LS, EVEN_MAIN)
    hy = []
    for uu, L, nb in ((up, LP, 4), (us, LS, 1)):
        fwd, inv = (jnp.asarray(t).astype(BF16) for t in _dft_tables(L))
        ka, kb = hyena_filter_spectra(L, hy_w1[0], hy_b1[0], hy_w2[0], hy_b2[0], hy_w3[0], hy_freq[0], fwd)
        hy.append(hyena_mix(uu, ev_conv[0], hy_d[0], fwd, inv, ka, kb, nb).reshape(-1, HY_W))
    y_hy = jnp.concatenate(hy, axis=0)
    ml_p, new_C, new_n, new_m = mlstm_mix(up, gates[:TP], ml_norm_g[0], want_state=True)
    ml_s = mlstm_mix(us, gates[TP:], ml_norm_g[0], state=(state_C[:, 0], state_n[:, 0], state_m[:, 0]))
    y_ml = jnp.concatenate([ml_p.reshape(TP, 512), ml_s.reshape(BS * LS, 512)], axis=0)
    x = proj_residual([y_hy, y_ml], ev_w_out[0], x, mods[0], 2)
    x = ffn_residual(x, norm2_g[0:1], mods[0], ff_w1[0], ff_w3[0], ff_w2[0])

    q, kv = qkv_proj(x, norm1_g[1:2], mods[1], at_w_qkv[0])
    o_p, new_k, new_v = attention(q[:TP], kv[:TP], at_q_g[0], at_k_g[0], at_sink[0], BP, LP)
    o_s = attention(q[TP:], kv[TP:], at_q_g[0], at_k_g[0], at_sink[0], BS, LS,
                    cache=(cache_k[:, 0], cache_v[:, 0]))
    x = proj_residual([jnp.concatenate([o_p, o_s], axis=0)], at_w_out[0], x, mods[1], 2)
    h, gates_moe = moe_router(x, norm2_g[1:2], mods[1], moe_w_router[0], moe_b_router[0])
    x = moe_dense_residual(h, x, gates_moe, mods[1], moe_w1[0], moe_w3[0], moe_w2[0])

    return (x[:TP].reshape(BP, LP, D), x[TP:].reshape(BS, LS, D),
            new_C[:, None], new_n[:, None], new_m[:, None], new_k[:, None], new_v[:, None])
```

```python
import functools
import math

import numpy as np
import jax
import jax.numpy as jnp
from jax import lax
from jax.experimental import pallas as pl
from jax.experimental.pallas import tpu as pltpu

F32 = jnp.float32
BF16 = jnp.bfloat16
HIGHEST = lax.Precision.HIGHEST

D = 1024
GROUP = 1024
N_PROMPT_GROUPS = 8
HY_W = 512
ML_HEADS = 4
ML_HD = 128
ML_CHUNK = 256
EVEN_MAIN = 3 * HY_W + 4 * 512
N_GATES = 16
ATT_HD = 64
ATT_HEADS = 16
ATT_KV = 4
WINDOW = 128
GRID_W = 64
ROPE_BASE = 10000.0
D_FF = 2816
N_EXPERTS = 8
EPS = 1e-6
NEG = -1e30
VMEM_LIMIT = 56 * 1024 * 1024


def _cparams(*sem):
    return pltpu.CompilerParams(dimension_semantics=sem, vmem_limit_bytes=VMEM_LIMIT)


def _mod_row(i, tm):
    return jnp.maximum(i * tm // GROUP - (N_PROMPT_GROUPS - 1), 0)


def _silu(x):
    return x * jax.nn.sigmoid(x)


def _bdot(a, b):
    return jnp.dot(a.astype(BF16), b.astype(BF16), preferred_element_type=F32)


def _norm_mod(x, g, sh, sc):
    y = x * lax.rsqrt(jnp.mean(x * x, axis=-1, keepdims=True) + EPS) * g
    return y * (1.0 + sc) + sh


def _adaln_kernel(c_ref, w_ref, b_ref, o_ref):
    s = _silu(c_ref[...])
    o_ref[...] = jnp.dot(s, w_ref[...], precision=HIGHEST, preferred_element_type=F32) + b_ref[...]


def adaln_table(cond, w_ada, b_ada):
    depth = w_ada.shape[0]
    tn = 1536
    out = pl.pallas_call(
        _adaln_kernel,
        grid=(depth, 6 * D // tn),
        in_specs=[pl.BlockSpec((16, D), lambda l, j: (0, 0)),
                  pl.BlockSpec((None, D, tn), lambda l, j: (l, 0, j)),
                  pl.BlockSpec((None, 1, tn), lambda l, j: (l, 0, j))],
        out_specs=pl.BlockSpec((None, 16, tn), lambda l, j: (l, 0, j)),
        out_shape=jax.ShapeDtypeStruct((depth, 16, 6 * D), F32),
        compiler_params=_cparams("parallel", "parallel"),
        name="adaln",
    )(cond, w_ada, b_ada.reshape(depth, 1, 6 * D))
    return out.reshape(depth, 16, 1, 6 * D)


def _mod_spec(k, tm):
    return pl.BlockSpec((None, 1, D), lambda i, *_: (_mod_row(i, tm), 0, k))


def _even_in_kernel(x_ref, g_ref, sh_ref, sc_ref, w_ref, wg_ref, bg_ref, u_ref, gate_ref, h_scr):
    @pl.when(pl.program_id(1) == 0)
    def _():
        h = _norm_mod(x_ref[...], g_ref[...], sh_ref[...], sc_ref[...]).astype(BF16)
        h_scr[...] = h
        gate_ref[...] = _bdot(h, wg_ref[...]) + bg_ref[...]

    u_ref[...] = _bdot(h_scr[...], w_ref[...]).astype(u_ref.dtype)


def even_in_proj(x, g, mods, w_in, b_gate, tm=1024, tn=512):
    T = x.shape[0]
    return pl.pallas_call(
        _even_in_kernel,
        grid=(T // tm, EVEN_MAIN // tn),
        in_specs=[pl.BlockSpec((tm, D), lambda i, j: (i, 0)),
                  pl.BlockSpec((1, D), lambda i, j: (0, 0)),
                  _mod_spec(0, tm), _mod_spec(1, tm),
                  pl.BlockSpec((D, tn), lambda i, j: (0, j)),
                  pl.BlockSpec((D, N_GATES), lambda i, j: (0, 0)),
                  pl.BlockSpec((1, N_GATES), lambda i, j: (0, 0))],
        out_specs=[pl.BlockSpec((tm, tn), lambda i, j: (i, j)),
                   pl.BlockSpec((tm, N_GATES), lambda i, j: (i, 0))],
        out_shape=[jax.ShapeDtypeStruct((T, EVEN_MAIN), BF16),
                   jax.ShapeDtypeStruct((T, N_GATES), F32)],
        scratch_shapes=[pltpu.VMEM((tm, D), BF16)],
        compiler_params=_cparams("parallel", "arbitrary"),
        name="even_in_proj",
    )(x, g, mods, mods, w_in, w_in[:, EVEN_MAIN:], b_gate)


def _dft_tables(L):
    n = 2 * L
    f = np.arange(L, dtype=np.int64)[:, None]
    s = np.arange(L, dtype=np.int64)[None, :]
    ang = 2.0 * np.pi * ((f * s) % n).astype(np.float64) / n
    fwd = np.concatenate([np.cos(ang), -np.sin(ang)], axis=0)
    fwd[L, :] = np.where(np.arange(L) % 2 == 0, 1.0, -1.0)
    t = np.arange(L, dtype=np.int64)[:, None]
    ff = np.arange(L, dtype=np.int64)[None, :]
    ang = 2.0 * np.pi * ((t * ff) % n).astype(np.float64) / n
    inv_re = 2.0 * np.cos(ang) / n
    inv_re[:, 0] = 1.0 / n
    inv_im = -2.0 * np.sin(ang) / n
    inv_im[:, 0] = np.where(np.arange(L) % 2 == 0, 1.0, -1.0) / n
    inv = np.concatenate([inv_re, inv_im], axis=1)
    return fwd.astype(np.float32), inv.astype(np.float32)


def _filter_tables(L):
    t = np.linspace(0.0, 1.0, L, dtype=np.float32).astype(np.float64)[:, None]
    w = 2.0 * math.pi * np.arange(L, dtype=np.float64)[:, None] / L
    bands = np.linspace(1e-4, 16 - 1, 16, dtype=np.float32).astype(np.float64)[None, :]
    z = np.concatenate([t, np.cos(bands * w), -np.sin(bands * w)], axis=-1)
    zp = np.zeros((L, 128), np.float64)
    zp[:, :z.shape[1]] = z
    max_decay = math.log(1e-2) / 0.3
    min_decay = math.log(1e-2) / 1.5
    deltas = np.linspace(min_decay, max_decay, HY_W, dtype=np.float32).astype(np.float64)
    decay = np.exp(-t * np.abs(deltas))
    return zp.astype(np.float32), decay.astype(np.float32)


def _hy_filter_kernel(z_ref, dec_ref, w1_ref, b1_ref, w2_ref, b2_ref, w3_ref, fr_ref, fwd_ref,
                      ka_ref, kb_ref):
    L = z_ref.shape[0]
    hdot = functools.partial(jnp.dot, precision=HIGHEST, preferred_element_type=F32)
    h = jnp.sin(fr_ref[0:1, :] * (hdot(z_ref[...], w1_ref[...]) + b1_ref[...]))
    h = jnp.sin(fr_ref[1:2, :] * (hdot(h, w2_ref[...]) + b2_ref[...]))
    h = hdot(h, w3_ref[...])
    row0 = lax.broadcasted_iota(jnp.int32, (L, 1), 0) == 0
    h0 = h[:, :HY_W] * dec_ref[...]
    h1 = h[:, HY_W:] * dec_ref[...]
    l1 = jnp.sum(jnp.abs(h0), axis=0, keepdims=True) + jnp.sum(jnp.abs(h1), axis=0, keepdims=True)
    inv = 1.0 / l1
    h0 = h0 * inv
    h1 = jnp.where(row0, 0.0, h1 * inv)
    f0 = _bdot(fwd_ref[...], h0)
    f1 = _bdot(fwd_ref[...], h1)
    ka_ref[...] = f0[:L] + f1[:L]
    kb_ref[...] = jnp.where(row0, f0[L:] + f1[L:], f0[L:] - f1[L:])


def _const_spec(a, n_grid):
    return pl.BlockSpec(a.shape, lambda *_: (0,) * a.ndim, pipeline_mode=pl.Buffered(1))


def hyena_filter_spectra(L, w1, b1, w2, b2, w3, freq, fwd):
    z, dec = _filter_tables(L)
    pad2 = lambda a, r, c: jnp.pad(a, ((0, r - a.shape[0]), (0, c - a.shape[1])))
    args = (jnp.asarray(z), jnp.asarray(dec), pad2(w1, 128, 128), pad2(b1[None], 1, 128),
            pad2(w2, 128, 128), pad2(b2[None], 1, 128), pad2(w3, 128, 4 * HY_W), pad2(freq, 2, 128), fwd)
    in_specs = [_const_spec(a, 1) for a in args]
    in_specs[6] = pl.BlockSpec((128, 2 * HY_W), lambda o: (0, o))
    shp = jax.ShapeDtypeStruct((2, L, HY_W), F32)
    out_spec = pl.BlockSpec((None, L, HY_W), lambda o: (o, 0, 0))
    return pl.pallas_call(
        _hy_filter_kernel,
        grid=(2,),
        in_specs=in_specs,
        out_specs=[out_spec, out_spec],
        out_shape=[shp, shp],
        compiler_params=_cparams("arbitrary"),
        name=f"hyena_filter_{L}",
    )(*args)


def _hyena_kernel(u_ref, cw_ref, d_ref, fwd_ref, inv_ref, ka_ref, kb_ref, o_ref):
    nb, L = u_ref.shape[0], u_ref.shape[1]
    row = lax.broadcasted_iota(jnp.int32, (L, 1), 0)
    first, last = row == 0, row == L - 1
    fwd = fwd_ref[...].astype(BF16)
    inv = inv_ref[...].astype(BF16)

    def long_conv(z, o):
        zf = jnp.dot(fwd, z.astype(BF16), preferred_element_type=F32)
        a, b = zf[:L], zf[L:]
        ka, kb = ka_ref[o], kb_ref[o]
        yr = a * ka - jnp.where(first, 0.0, b * kb)
        yi = jnp.where(first, b * kb, a * kb + b * ka)
        return (jnp.dot(inv[:, :L], yr.astype(BF16), preferred_element_type=F32)
                + jnp.dot(inv[:, L:], yi.astype(BF16), preferred_element_type=F32))

    for bi in range(nb):
        u = u_ref[bi].astype(F32)
        prev = jnp.where(first, 0.0, pltpu.roll(u, 1, 0))
        nxt = jnp.where(last, 0.0, pltpu.roll(u, L - 1, 0))
        u = prev * cw_ref[0:1, :] + u * cw_ref[1:2, :] + nxt * cw_ref[2:3, :]
        v, x1, x2 = u[:, :HY_W], u[:, HY_W:2 * HY_W], u[:, 2 * HY_W:]
        z = x1 * (long_conv(v, 0) + d_ref[0:1, :] * v)
        z = x2 * (long_conv(z, 1) + d_ref[1:2, :] * z)
        o_ref[bi] = z.astype(o_ref.dtype)


def hyena_mix(u3, conv_w, d_skip, fwd, inv, ka, kb, nb):
    B, L, _ = u3.shape
    full = lambda a: _const_spec(a, 1)
    return pl.pallas_call(
        _hyena_kernel,
        grid=(B // nb,),
        in_specs=[pl.BlockSpec((nb, L, 3 * HY_W), lambda b: (b, 0, 0)),
                  full(conv_w), full(d_skip), full(fwd), full(inv), full(ka), full(kb)],
        out_specs=pl.BlockSpec((nb, L, HY_W), lambda b: (b, 0, 0)),
        out_shape=jax.ShapeDtypeStruct((B, L, HY_W), BF16),
        compiler_params=_cparams("parallel"),
        name=f"hyena_{L}",
    )(u3, conv_w, d_skip, fwd, inv, ka, kb)


def _log_sigmoid(x):
    return jnp.minimum(x, 0.0) - jnp.log(1.0 + jnp.exp(-jnp.abs(x)))


def _mlstm_chunk(q, ks, v, li_c, lf_c, li_r, lf_r, C, n, m, rev):
    T = q.shape[0]
    ri = lax.broadcasted_iota(jnp.int32, (T, T), 0)
    ci = lax.broadcasted_iota(jnp.int32, (T, T), 1)
    lo = (ci <= ri).astype(F32)
    up = (ci >= ri).astype(F32)
    hdot = functools.partial(jnp.dot, precision=HIGHEST, preferred_element_type=F32)
    if rev:
        mask = ci >= ri
        b_c, b_r = hdot(up, lf_c), hdot(lf_r, lo)
        b_end = b_c[0:1, :]
    else:
        mask = ci <= ri
        b_c, b_r = hdot(lo, lf_c), hdot(lf_r, up)
        b_end = b_c[T - 1:T, :]
    dm = jnp.where(mask, b_c - b_r + li_r, NEG)
    inter = b_c + m
    m_t = jnp.maximum(inter, jnp.max(dm, axis=1, keepdims=True))
    w_intra = jnp.exp(dm - m_t)
    w_inter = jnp.exp(inter - m_t)
    s = lax.dot_general(q, ks, (((1,), (1,)), ((), ())), preferred_element_type=F32) * w_intra
    num = _bdot(s, v) + w_inter * _bdot(q, C)
    den = jnp.sum(s, axis=1, keepdims=True) + w_inter * jnp.sum(q.astype(F32) * n, axis=1, keepdims=True)
    h = num / jnp.maximum(jnp.abs(den), jnp.exp(-m_t))
    g_c = b_end - b_c + li_c
    g_r = b_end - b_r + li_r
    m_new = jnp.maximum(b_end + m, jnp.max(g_r, axis=1, keepdims=True))
    decay = jnp.exp(b_end + m - m_new)
    kw = ks.astype(F32) * jnp.exp(g_c - m_new)
    C_new = decay * C + lax.dot_general(kw.astype(BF16), v, (((0,), (0,)), ((), ())),
                                        preferred_element_type=F32)
    n_new = decay * n + jnp.sum(kw, axis=0, keepdims=True)
    return h, C_new, n_new, m_new


def _mlstm_kernel(*refs, has_state, want_state):
    q_ref, k_ref, v_ref, o_ref, gc_ref, gr_ref, ng_ref = refs[:7]
    refs = refs[7:]
    if has_state:
        c0_ref, n0_ref, m0_ref = refs[:3]
        refs = refs[3:]
    y_ref = refs[0]
    if want_state:
        c_out, n_out, m_out = refs[1:4]
    L = q_ref.shape[0]
    T = min(ML_CHUNK, L)
    nc = L // T
    scale = 1.0 / math.sqrt(ML_HD)
    hs = [None] * nc
    for dr in range(2):
        if has_state:
            C, n, m = c0_ref[dr], n0_ref[dr:dr + 1, :], m0_ref[dr:dr + 1, 0:1]
        else:
            C, n, m = jnp.zeros((ML_HD, ML_HD), F32), jnp.zeros((1, ML_HD), F32), jnp.zeros((1, 1), F32)
        order = range(nc) if dr == 0 else range(nc - 1, -1, -1)
        for j in order:
            sl = slice(j * T, (j + 1) * T)
            q = q_ref[sl, :]
            ks = (k_ref[sl, :].astype(F32) * scale).astype(BF16)
            v = v_ref[sl, :]
            li_c = gc_ref[sl, 2 * dr:2 * dr + 1]
            lf_c = _log_sigmoid(gc_ref[sl, 2 * dr + 1:2 * dr + 2])
            li_r = gr_ref[2 * dr:2 * dr + 1, sl]
            lf_r = _log_sigmoid(gr_ref[2 * dr + 1:2 * dr + 2, sl])
            h, C, n, m = _mlstm_chunk(q, ks, v, li_c, lf_c, li_r, lf_r, C, n, m, rev=(dr == 1))
            hs[j] = h if hs[j] is None else hs[j] + h
        if want_state:
            c_out[dr] = C
            n_out[dr:dr + 1, :] = n
            m_out[dr:dr + 1, :] = jnp.broadcast_to(m, (1, ML_HD))
    for j in range(nc):
        sl = slice(j * T, (j + 1) * T)
        h = hs[j]
        y = h * lax.rsqrt(jnp.mean(h * h, axis=-1, keepdims=True) + EPS) * ng_ref[...]
        y_ref[sl, :] = (y * jax.nn.sigmoid(o_ref[sl, :].astype(F32))).astype(y_ref.dtype)


def mlstm_mix(u3, gates, norm_g, state=None, want_state=False):
    B, L, _ = u3.shape
    g5 = gates.reshape(B, L, 2, 2, ML_HEADS)
    gc = g5.transpose(0, 4, 1, 2, 3).reshape(B, ML_HEADS, L, 4)
    gr = g5.transpose(0, 4, 2, 3, 1).reshape(B, ML_HEADS, 4, L)
    col = lambda i: pl.BlockSpec((None, L, ML_HD), lambda b, h: (b, 0, (3 * HY_W + i * 512) // ML_HD + h))
    in_specs = [col(0), col(1), col(2), col(3),
                pl.BlockSpec((None, None, L, 4), lambda b, h: (b, h, 0, 0)),
                pl.BlockSpec((None, None, 4, L), lambda b, h: (b, h, 0, 0)),
                pl.BlockSpec((None, 1, ML_HD), lambda b, h: (h, 0, 0))]
    args = [u3, u3, u3, u3, gc, gr, norm_g.reshape(ML_HEADS, 1, ML_HD)]
    if state is not None:
        C0, n0, m0 = state
        in_specs += [pl.BlockSpec((None, 2, None, ML_HD, ML_HD), lambda b, h: (b, 0, h, 0, 0)),
                     pl.BlockSpec((None, None, 2, ML_HD), lambda b, h: (b, h, 0, 0)),
                     pl.BlockSpec((None, None, 2, ML_HD), lambda b, h: (b, h, 0, 0))]
        args += [C0, n0.transpose(0, 2, 1, 3),
                 jnp.broadcast_to(m0.transpose(0, 2, 1)[..., None], (B, ML_HEADS, 2, ML_HD))]
    out_specs = [pl.BlockSpec((None, L, ML_HD), lambda b, h: (b, 0, h))]
    out_shape = [jax.ShapeDtypeStruct((B, L, ML_HEADS * ML_HD), BF16)]
    if want_state:
        out_specs += [pl.BlockSpec((None, 2, None, ML_HD, ML_HD), lambda b, h: (b, 0, h, 0, 0)),
                      pl.BlockSpec((None, None, 2, ML_HD), lambda b, h: (b, h, 0, 0)),
                      pl.BlockSpec((None, None, 2, ML_HD), lambda b, h: (b, h, 0, 0))]
        out_shape += [jax.ShapeDtypeStruct((B, 2, ML_HEADS, ML_HD, ML_HD), F32),
                      jax.ShapeDtypeStruct((B, ML_HEADS, 2, ML_HD), F32),
                      jax.ShapeDtypeStruct((B, ML_HEADS, 2, ML_HD), F32)]
    outs = pl.pallas_call(
        functools.partial(_mlstm_kernel, has_state=state is not None, want_state=want_state),
        grid=(B, ML_HEADS),
        in_specs=in_specs, out_specs=out_specs, out_shape=out_shape,
        compiler_params=_cparams("parallel", "parallel"),
        name=f"mlstm_{L}",
    )(*args)
    if not want_state:
        return outs[0]
    y, C, n, m = outs
    return y, C, n.transpose(0, 2, 1, 3), m[..., 0].transpose(0, 2, 1)


def _proj_res_kernel(*refs, n_in):
    a_refs = refs[:n_in]
    w_ref, x_ref, gate_ref, o_ref = refs[n_in:]
    k0 = 0
    acc = None
    for a_ref in a_refs:
        kw = a_ref.shape[1]
        part = _bdot(a_ref[...], w_ref[k0:k0 + kw, :])
        acc = part if acc is None else acc + part
        k0 += kw
    o_ref[...] = x_ref[...] + gate_ref[...] * acc


def proj_residual(acts, w, x, mods, gate_idx, tm=1024):
    T = x.shape[0]
    return pl.pallas_call(
        functools.partial(_proj_res_kernel, n_in=len(acts)),
        grid=(T // tm,),
        in_specs=[pl.BlockSpec((tm, a.shape[1]), lambda i: (i, 0)) for a in acts]
        + [pl.BlockSpec(w.shape, lambda i: (0, 0)),
           pl.BlockSpec((tm, D), lambda i: (i, 0)),
           _mod_spec(gate_idx, tm)],
        out_specs=pl.BlockSpec((tm, D), lambda i: (i, 0)),
        out_shape=jax.ShapeDtypeStruct((T, D), F32),
        compiler_params=_cparams("parallel"),
        name="proj_residual",
    )(*acts, w, x, mods)


def _ffn_kernel(x_ref, g_ref, sh_ref, sc_ref, gate_ref, w1_ref, w3_ref, w2_ref, o_ref, h_scr, acc_scr):
    c = pl.program_id(1)

    @pl.when(c == 0)
    def _():
        h_scr[...] = _norm_mod(x_ref[...], g_ref[...], sh_ref[...], sc_ref[...]).astype(BF16)
        acc_scr[...] = jnp.zeros_like(acc_scr)

    h = h_scr[...]
    mid = _silu(_bdot(h, w1_ref[...])) * _bdot(h, w3_ref[...])
    acc_scr[...] += _bdot(mid, w2_ref[...])

    @pl.when(c == pl.num_programs(1) - 1)
    def _():
        o_ref[...] = x_ref[...] + gate_ref[...] * acc_scr[...]


def ffn_residual(x, g, mods, w1, w3, w2, tm=1024, tf=256):
    T = x.shape[0]
    return pl.pallas_call(
        _ffn_kernel,
        grid=(T // tm, D_FF // tf),
        in_specs=[pl.BlockSpec((tm, D), lambda i, c: (i, 0)),
                  pl.BlockSpec((1, D), lambda i, c: (0, 0)),
                  _mod_spec(3, tm), _mod_spec(4, tm), _mod_spec(5, tm),
                  pl.BlockSpec((D, tf), lambda i, c: (0, c)),
                  pl.BlockSpec((D, tf), lambda i, c: (0, c)),
                  pl.BlockSpec((tf, D), lambda i, c: (c, 0))],
        out_specs=pl.BlockSpec((tm, D), lambda i, c: (i, 0)),
        out_shape=jax.ShapeDtypeStruct((T, D), F32),
        scratch_shapes=[pltpu.VMEM((tm, D), BF16), pltpu.VMEM((tm, D), F32)],
        compiler_params=_cparams("parallel", "arbitrary"),
        name="ffn",
    )(x, g, mods, mods, mods, w1, w3, w2)


def _qkv_kernel(x_ref, g_ref, sh_ref, sc_ref, w_ref, q_ref, kv_ref, h_scr):
    j = pl.program_id(1)

    @pl.when(j == 0)
    def _():
        h_scr[...] = _norm_mod(x_ref[...], g_ref[...], sh_ref[...], sc_ref[...]).astype(BF16)

    @pl.when(j < 2)
    def _():
        q_ref[...] = _bdot(h_scr[...], w_ref[...]).astype(q_ref.dtype)

    @pl.when(j == 2)
    def _():
        kv_ref[...] = _bdot(h_scr[...], w_ref[...])


def qkv_proj(x, g, mods, w_qkv, tm=1024):
    T = x.shape[0]
    tn = 512
    return pl.pallas_call(
        _qkv_kernel,
        grid=(T // tm, 3),
        in_specs=[pl.BlockSpec((tm, D), lambda i, j: (i, 0)),
                  pl.BlockSpec((1, D), lambda i, j: (0, 0)),
                  _mod_spec(0, tm), _mod_spec(1, tm),
                  pl.BlockSpec((D, tn), lambda i, j: (0, j))],
        out_specs=[pl.BlockSpec((tm, tn), lambda i, j: (i, jnp.minimum(j, 1))),
                   pl.BlockSpec((tm, tn), lambda i, j: (i, 0))],
        out_shape=[jax.ShapeDtypeStruct((T, ATT_HEADS * ATT_HD), BF16),
                   jax.ShapeDtypeStruct((T, 2 * ATT_KV * ATT_HD), F32)],
        scratch_shapes=[pltpu.VMEM((tm, D), BF16)],
        compiler_params=_cparams("parallel", "arbitrary"),
        name="qkv_proj",
    )(x, g, mods, mods, w_qkv)


def _rope_tables(L):
    half = ATT_HD // 2
    pos_r = (np.arange(L) // GRID_W).astype(np.float32)
    pos_c = (np.arange(L) % GRID_W).astype(np.float32)
    inv = (ROPE_BASE ** (-np.arange(0, half, 2, dtype=np.float32) / half)).astype(np.float32)
    cos = np.zeros((L, ATT_HD), np.float64)
    sin = np.zeros((L, ATT_HD), np.float64)
    for base, pos in ((0, pos_r), (half, pos_c)):
        ang = (pos[:, None] * inv[None, :]).astype(np.float32).astype(np.float64)
        cos[:, base:base + half] = np.concatenate([np.cos(ang), np.cos(ang)], axis=1)
        sin[:, base:base + half] = np.concatenate([-np.sin(ang), np.sin(ang)], axis=1)
    return (np.tile(cos, (1, 4)).astype(np.float32), np.tile(sin, (1, 4)).astype(np.float32))


def _seg_rms(x):
    w = x.shape[1]
    ri = lax.broadcasted_iota(jnp.int32, (w, w), 0) // ATT_HD
    ci = lax.broadcasted_iota(jnp.int32, (w, w), 1) // ATT_HD
    bd = (ri == ci).astype(F32)
    ss = jnp.dot(x * x, bd, precision=HIGHEST, preferred_element_type=F32)
    return x * lax.rsqrt(ss * (1.0 / ATT_HD) + EPS)


def _swap16(x):
    w = x.shape[1]
    lane = lax.broadcasted_iota(jnp.int32, x.shape, 1)
    return jnp.where(lane % 32 < 16, pltpu.roll(x, w - 16, 1), pltpu.roll(x, 16, 1))


def _attn_kernel(*refs, latent, tq):
    if latent:
        (q_ref, kv_ref, ck_ref, cv_ref, qg_ref, kg_ref, sink_ref, cosq_ref, sinq_ref, cosk_ref, sink_t_ref,
         o_ref, k_scr, v_scr, ck_scr, cv_scr) = refs
    else:
        q_ref, kv_ref, qg_ref, kg_ref, sink_ref, o_ref, ko_ref, vo_ref, k_scr, v_scr = refs
    L = kv_ref.shape[0]
    gw = ATT_KV * ATT_HD
    qb = pl.program_id(1)

    @pl.when(qb == 0)
    def _():
        kn = _seg_rms(kv_ref[:, :gw]) * kg_ref[...]
        v = kv_ref[:, gw:]
        if latent:
            kn = kn * cosk_ref[...] + _swap16(kn) * sink_t_ref[...]
            for c in range(ATT_KV):
                ck_scr[c] = ck_ref[c].astype(BF16)
                cv_scr[c] = cv_ref[c].astype(BF16)
        else:
            for c in range(ATT_KV):
                ko_ref[c] = kn[:, c * ATT_HD:(c + 1) * ATT_HD]
                vo_ref[c] = v[:, c * ATT_HD:(c + 1) * ATT_HD]
        k_scr[...] = kn.astype(BF16)
        v_scr[...] = v.astype(BF16)

    if latent:
        span = tq + 2 * WINDOW
        start = pl.multiple_of(jnp.clip(qb * tq - WINDOW, 0, L - span), WINDOW)
        t_pos = qb * tq + lax.broadcasted_iota(jnp.int32, (tq, span), 0)
        s_pos = start + lax.broadcasted_iota(jnp.int32, (tq, span), 1)
        win = jnp.abs(t_pos - s_pos) <= WINDOW
        kw = k_scr[pl.ds(start, span), :]
        vw = v_scr[pl.ds(start, span), :]
    else:
        kw, vw = k_scr[...], v_scr[...]

    nt = (((1,), (1,)), ((), ()))
    for c in range(ATT_KV):
        qc = _seg_rms(q_ref[:, c * gw:(c + 1) * gw].astype(F32)) * qg_ref[...]
        if latent:
            qc = qc * cosq_ref[...] + _swap16(qc) * sinq_ref[...]
        qc = (qc * (1.0 / math.sqrt(ATT_HD))).astype(BF16)
        hsl = slice(c * ATT_HD, (c + 1) * ATT_HD)
        kc, vc = kw[:, hsl], vw[:, hsl]
        outs = []
        for g in range(ATT_KV):
            qh = qc[:, g * ATT_HD:(g + 1) * ATT_HD]
            sink = sink_ref[:, c * ATT_KV + g:c * ATT_KV + g + 1]
            lw = lax.dot_general(qh, kc, nt, preferred_element_type=F32)
            if latent:
                lw = jnp.where(win, lw, NEG)
                lc = lax.dot_general(qh, ck_scr[c], nt, preferred_element_type=F32)
                mx = jnp.maximum(jnp.maximum(jnp.max(lw, axis=1, keepdims=True),
                                             jnp.max(lc, axis=1, keepdims=True)), sink)
                pw, pc = jnp.exp(lw - mx), jnp.exp(lc - mx)
                den = (jnp.sum(pw, axis=1, keepdims=True) + jnp.sum(pc, axis=1, keepdims=True)
                       + jnp.exp(sink - mx))
                o = _bdot(pw, vc) + _bdot(pc, cv_scr[c])
            else:
                mx = jnp.maximum(jnp.max(lw, axis=1, keepdims=True), sink)
                pw = jnp.exp(lw - mx)
                den = jnp.sum(pw, axis=1, keepdims=True) + jnp.exp(sink - mx)
                o = _bdot(pw, vc)
            outs.append(o / den)
        o_ref[:, c * gw:(c + 1) * gw] = jnp.concatenate(outs, axis=1).astype(o_ref.dtype)


def attention(q, kv, q_g, k_g, sink, B, L, cache=None, tq=256):
    latent = cache is not None
    gw = ATT_KV * ATT_HD
    qg = jnp.tile(q_g, ATT_KV)[None]
    kg = jnp.tile(k_g, ATT_KV)[None]
    nq = L // tq
    const = lambda a: pl.BlockSpec(a.shape, lambda b, i: (0,) * a.ndim)
    in_specs = [pl.BlockSpec((tq, ATT_HEADS * ATT_HD), lambda b, i: (b * nq + i, 0)),
                pl.BlockSpec((L, 2 * gw), lambda b, i: (b, 0))]
    args = [q, kv]
    scratch = [pltpu.VMEM((L, gw), BF16), pltpu.VMEM((L, gw), BF16)]
    out_specs = [pl.BlockSpec((tq, ATT_HEADS * ATT_HD), lambda b, i: (b * nq + i, 0))]
    out_shape = [jax.ShapeDtypeStruct((B * L, ATT_HEADS * ATT_HD), BF16)]
    if latent:
        ck, cv = cache
        P = ck.shape[2]
        cos, sin = (jnp.asarray(t) for t in _rope_tables(L))
        in_specs += [pl.BlockSpec((None, ATT_KV, P, ATT_HD), lambda b, i: (b, 0, 0, 0))] * 2
        args += [ck, cv]
        in_specs += [const(qg), const(kg), pl.BlockSpec((1, ATT_HEADS), lambda b, i: (0, 0)),
                     pl.BlockSpec((tq, gw), lambda b, i: (i, 0)), pl.BlockSpec((tq, gw), lambda b, i: (i, 0)),
                     const(cos), const(sin)]
        args += [qg, kg, sink[None], cos, sin, cos, sin]
        scratch += [pltpu.VMEM((ATT_KV, P, ATT_HD), BF16), pltpu.VMEM((ATT_KV, P, ATT_HD), BF16)]
    else:
        in_specs += [const(qg), const(kg), pl.BlockSpec((1, ATT_HEADS), lambda b, i: (0, 0))]
        args += [qg, kg, sink[None]]
        cache_spec = pl.BlockSpec((None, ATT_KV, L, ATT_HD), lambda b, i: (b, 0, 0, 0))
        out_specs += [cache_spec, cache_spec]
        out_shape += [jax.ShapeDtypeStruct((B, ATT_KV, L, ATT_HD), F32)] * 2
    outs = pl.pallas_call(
        functools.partial(_attn_kernel, latent=latent, tq=tq),
        grid=(B, nq),
        in_specs=in_specs, out_specs=out_specs, out_shape=out_shape,
        scratch_shapes=scratch,
        compiler_params=_cparams("parallel", "arbitrary"),
        name="attn_latent" if latent else "attn_context",
    )(*args)
    return outs[0] if latent else outs


MOE_TM = 1024
MOE_MAX_TILES = 2 * 16384 // MOE_TM + N_EXPERTS


def _router_kernel(x_ref, g_ref, sh_ref, sc_ref, wr_ref, br_ref, tri_ref, eid_ref, rank_ref, wts_ref, cnt_ref):
    @pl.when(pl.program_id(0) == 0)
    def _():
        cnt_ref[...] = jnp.zeros_like(cnt_ref)

    h = _norm_mod(x_ref[...], g_ref[...], sh_ref[...], sc_ref[...])
    lg = lax.dot_general(wr_ref[...], h, (((1,), (1,)), ((), ())), precision=HIGHEST,
                         preferred_element_type=F32) + br_ref[...]
    row = lax.broadcasted_iota(jnp.int32, lg.shape, 0)
    m1 = jnp.max(lg, axis=0, keepdims=True)
    i1 = jnp.min(jnp.where(lg == m1, row, N_EXPERTS), axis=0, keepdims=True)
    l2 = jnp.where(row == i1, -jnp.inf, lg)
    m2 = jnp.max(l2, axis=0, keepdims=True)
    i2 = jnp.min(jnp.where(l2 == m2, row, N_EXPERTS), axis=0, keepdims=True)
    e2 = jnp.exp(m2 - m1)
    w1 = 1.0 / (1.0 + e2)
    eid_ref[...] = jnp.concatenate([i1, i2], axis=0)
    wts_ref[...] = jnp.concatenate([w1, e2 * w1], axis=0)
    oh1 = (row == i1).astype(F32)
    oh2 = (row == i2).astype(F32)
    cs1 = _bdot(oh1, tri_ref[...])
    cs2 = _bdot(oh2, tri_ref[...])
    tot1 = jnp.sum(oh1, axis=1, keepdims=True)
    cnt = cnt_ref[:, 0:1]
    r1 = jnp.sum(oh1 * (cnt + cs1), axis=0, keepdims=True)
    r2 = jnp.sum(oh2 * (cnt + tot1 + cs2), axis=0, keepdims=True)
    rank_ref[...] = jnp.concatenate([r1, r2], axis=0).astype(jnp.int32)
    cnt_ref[...] = cnt_ref[...] + tot1 + jnp.sum(oh2, axis=1, keepdims=True)


def moe_router(x, g, mods, w_router, b_router, tm=1024):
    T = x.shape[0]
    tri = jnp.asarray(np.triu(np.ones((tm, tm), np.float32), k=1)).astype(BF16)
    tok2 = lambda dt: jax.ShapeDtypeStruct((2, T), dt)
    return pl.pallas_call(
        _router_kernel,
        grid=(T // tm,),
        in_specs=[pl.BlockSpec((tm, D), lambda i: (i, 0)),
                  pl.BlockSpec((1, D), lambda i: (0, 0)),
                  _mod_spec(3, tm), _mod_spec(4, tm),
                  pl.BlockSpec((N_EXPERTS, D), lambda i: (0, 0)),
                  pl.BlockSpec((N_EXPERTS, 1), lambda i: (0, 0)),
                  _const_spec(tri, 1)],
        out_specs=[pl.BlockSpec((2, tm), lambda i: (0, i)),
                   pl.BlockSpec((2, tm), lambda i: (0, i)),
                   pl.BlockSpec((2, tm), lambda i: (0, i)),
                   pl.BlockSpec((N_EXPERTS, 128), lambda i: (0, 0))],
        out_shape=[tok2(jnp.int32), tok2(jnp.int32), tok2(F32),
                   jax.ShapeDtypeStruct((N_EXPERTS, 128), F32)],
        compiler_params=_cparams("arbitrary"),
        name="moe_router",
    )(x, g, mods, mods, w_router.T, b_router[:, None], tri)


def moe_layout(eid, rank, counts):
    cnt = counts[:, 0].astype(jnp.int32)
    tiles = (cnt + MOE_TM - 1) // MOE_TM
    tile_end = jnp.cumsum(tiles)
    start = (tile_end - tiles) * MOE_TM
    pos = rank + jnp.sum(jnp.where(eid[..., None] == jnp.arange(N_EXPERTS), start, 0), axis=-1)
    n_tiles = tile_end[-1]
    t = jnp.arange(MOE_MAX_TILES, dtype=jnp.int32)
    tile_e = jnp.sum(t[:, None] >= tile_end[None, :], axis=1).astype(jnp.int32)
    last_e = jnp.sum((n_tiles - 1) >= tile_end).astype(jnp.int32)
    tile_e = jnp.where(t < n_tiles, tile_e, last_e)
    return pos.astype(jnp.int32), tile_e, n_tiles.astype(jnp.int32).reshape(1)


def _row_copy(src, i, dst, j, sem):
    return pltpu.make_async_copy(src.at[pl.ds(i, 1), :], dst.at[pl.ds(j, 1), :], sem)


def _dispatch_kernel(pos_ref, x_ref, g_ref, sh_ref, sc_ref, xs_in_ref, xs_ref, h_scr, sem):
    del xs_in_ref
    tm = h_scr.shape[0]
    h_scr[...] = _norm_mod(x_ref[...], g_ref[...], sh_ref[...], sc_ref[...])

    def issue(r, carry):
        for s in range(2):
            _row_copy(h_scr, r, xs_ref, pos_ref[s, r], sem.at[s]).start()
        return carry

    lax.fori_loop(0, tm, issue, 0, unroll=8)
    for s in range(2):
        pltpu.make_async_copy(h_scr, xs_ref.at[pl.ds(0, tm), :], sem.at[s]).wait()


def moe_dispatch(x, g, mods, pos, tm=1024):
    T = x.shape[0]
    n_rows = MOE_MAX_TILES * MOE_TM
    pos3 = pos.reshape(2, T // tm, tm).transpose(1, 0, 2)
    return pl.pallas_call(
        _dispatch_kernel,
        grid=(T // tm,),
        in_specs=[pl.BlockSpec((None, 2, tm), lambda i: (i, 0, 0), memory_space=pltpu.SMEM),
                  pl.BlockSpec((tm, D), lambda i: (i, 0)),
                  pl.BlockSpec((1, D), lambda i: (0, 0)),
                  _mod_spec(3, tm), _mod_spec(4, tm),
                  pl.BlockSpec(memory_space=pl.ANY)],
        out_specs=pl.BlockSpec(memory_space=pl.ANY),
        out_shape=jax.ShapeDtypeStruct((n_rows, D), F32),
        scratch_shapes=[pltpu.VMEM((tm, D), F32), pltpu.SemaphoreType.DMA((2,))],
        input_output_aliases={5: 0},
        compiler_params=_cparams("arbitrary"),
        name="moe_dispatch",
    )(pos3, x, g, mods, mods, jnp.zeros((n_rows, D), F32))


def _moe_group_kernel(te_ref, nt_ref, x_ref, w1_ref, w3_ref, w2_ref, o_ref, h_scr, acc_scr):
    t, c = pl.program_id(0), pl.program_id(1)

    @pl.when(t < nt_ref[0])
    def _():
        @pl.when(c == 0)
        def _():
            h_scr[...] = x_ref[...].astype(BF16)
            acc_scr[...] = jnp.zeros_like(acc_scr)

        h = h_scr[...]
        mid = _silu(_bdot(h, w1_ref[...])) * _bdot(h, w3_ref[...])
        acc_scr[...] += _bdot(mid, w2_ref[...])

        @pl.when(c == pl.num_programs(1) - 1)
        def _():
            o_ref[...] = acc_scr[...]

    @pl.when((t >= nt_ref[0]) & (c == pl.num_programs(1) - 1))
    def _():
        o_ref[...] = jnp.zeros_like(o_ref)


def moe_grouped_swiglu(xs, tile_e, n_tiles, w1, w3, w2, tf=256):
    nc = D_FF // tf
    live = lambda t, nt: t < nt[0]
    row = lambda t, c, te, nt: (jnp.where(live(t, nt), t, nt[0] - 1), 0)
    wcol = lambda t, c, te, nt: (te[t], 0, jnp.where(live(t, nt), c, nc - 1))
    wrow = lambda t, c, te, nt: (te[t], jnp.where(live(t, nt), c, nc - 1), 0)
    return pl.pallas_call(
        _moe_group_kernel,
        grid_spec=pltpu.PrefetchScalarGridSpec(
            num_scalar_prefetch=2,
            grid=(MOE_MAX_TILES, nc),
            in_specs=[pl.BlockSpec((MOE_TM, D), row),
                      pl.BlockSpec((None, D, tf), wcol),
                      pl.BlockSpec((None, D, tf), wcol),
                      pl.BlockSpec((None, tf, D), wrow)],
            out_specs=pl.BlockSpec((MOE_TM, D), lambda t, c, te, nt: (t, 0)),
            scratch_shapes=[pltpu.VMEM((MOE_TM, D), BF16), pltpu.VMEM((MOE_TM, D), F32)]),
        out_shape=jax.ShapeDtypeStruct(xs.shape, F32),
        compiler_params=_cparams("arbitrary", "arbitrary"),
        name="moe_grouped",
    )(tile_e, n_tiles, xs, w1, w3, w2)


def _combine_kernel(pos_ref, x_ref, gate_ref, wt_ref, ys_ref, o_ref, a_scr, b_scr, sem):
    tm = a_scr.shape[0]
    bufs = (a_scr, b_scr)

    def issue(r, carry):
        for s in range(2):
            _row_copy(ys_ref, pos_ref[s, r], bufs[s], r, sem.at[s]).start()
        return carry

    lax.fori_loop(0, tm, issue, 0, unroll=8)
    for s in range(2):
        pltpu.make_async_copy(ys_ref.at[pl.ds(0, tm), :], bufs[s], sem.at[s]).wait()
    moe = wt_ref[:, 0:1] * a_scr[...] + wt_ref[:, 1:2] * b_scr[...]
    o_ref[...] = x_ref[...] + gate_ref[...] * moe


def moe_combine(x, mods, pos, wts, ys, tm=1024):
    T = x.shape[0]
    pos3 = pos.reshape(2, T // tm, tm).transpose(1, 0, 2)
    return pl.pallas_call(
        _combine_kernel,
        grid=(T // tm,),
        in_specs=[pl.BlockSpec((None, 2, tm), lambda i: (i, 0, 0), memory_space=pltpu.SMEM),
                  pl.BlockSpec((tm, D), lambda i: (i, 0)),
                  _mod_spec(5, tm),
                  pl.BlockSpec((tm, 2), lambda i: (i, 0)),
                  pl.BlockSpec(memory_space=pl.ANY)],
        out_specs=pl.BlockSpec((tm, D), lambda i: (i, 0)),
        out_shape=jax.ShapeDtypeStruct((T, D), F32),
        scratch_shapes=[pltpu.VMEM((tm, D), F32), pltpu.VMEM((tm, D), F32), pltpu.SemaphoreType.DMA((2,))],
        compiler_params=_cparams("arbitrary"),
        name="moe_combine",
    )(pos3, x, mods, wts.T, ys)


def kernel(x_prompt, x_sample, state_C, state_n, state_m, cache_k, cache_v, c, c_ctx, norm1_g, norm2_g, w_ada, b_ada, ev_w_in, ev_conv, hy_w1, hy_b1, hy_w2, hy_b2, hy_w3, hy_freq, hy_d, ml_b_gate, ml_norm_g, ev_w_out, ff_w1, ff_w3, ff_w2, at_w_qkv, at_q_g, at_k_g, at_sink, at_w_out, moe_w_router, moe_b_router, moe_w1, moe_w3, moe_w2):
    BP, LP, _ = x_prompt.shape
    BS, LS, _ = x_sample.shape
    TP = BP * LP
    assert TP % GROUP == 0 and TP // GROUP == N_PROMPT_GROUPS and LS == GROUP and BS == 8

    x = jnp.concatenate([x_prompt.reshape(TP, D), x_sample.reshape(BS * LS, D)], axis=0)
    cond = jnp.concatenate([c_ctx[None], c, jnp.zeros((16 - 1 - BS, D), F32)], axis=0)
    mods = adaln_table(cond, w_ada, b_ada)

    u, gates = even_in_proj(x, norm1_g[0:1], mods[0], ev_w_in[0], ml_b_gate[0].reshape(1, N_GATES))
    up = u[:TP].reshape(BP, LP, EVEN_MAIN)
    us = u[TP:].reshape(BS, LS, EVEN_MAIN)
    hy = []
    for uu, L, nb in ((up, LP, 4), (us, LS, 1)):
        fwd, inv = (jnp.asarray(t).astype(BF16) for t in _dft_tables(L))
        ka, kb = hyena_filter_spectra(L, hy_w1[0], hy_b1[0], hy_w2[0], hy_b2[0], hy_w3[0], hy_freq[0], fwd)
        hy.append(hyena_mix(uu, ev_conv[0], hy_d[0], fwd, inv, ka, kb, nb).reshape(-1, HY_W))
    y_hy = jnp.concatenate(hy, axis=0)
    ml_p, new_C, new_n, new_m = mlstm_mix(up, gates[:TP], ml_norm_g[0], want_state=True)
    ml_s = mlstm_mix(us, gates[TP:], ml_norm_g[0], state=(state_C[:, 0], state_n[:, 0], state_m[:, 0]))
    y_ml = jnp.concatenate([ml_p.reshape(TP, 512), ml_s.reshape(BS * LS, 512)], axis=0)
    x = proj_residual([y_hy, y_ml], ev_w_out[0], x, mods[0], 2)
    x = ffn_residual(x, norm2_g[0:1], mods[0], ff_w1[0], ff_w3[0], ff_w2[0])

    q, kv = qkv_proj(x, norm1_g[1:2], mods[1], at_w_qkv[0])
    o_p, new_k, new_v = attention(q[:TP], kv[:TP], at_q_g[0], at_k_g[0], at_sink[0], BP, LP)
    o_s = attention(q[TP:], kv[TP:], at_q_g[0], at_k_g[0], at_sink[0], BS, LS,
                    cache=(cache_k[:, 0], cache_v[:, 0]))
    x = proj_residual([jnp.concatenate([o_p, o_s], axis=0)], at_w_out[0], x, mods[1], 2)
    eid, rank, wts, counts = moe_router(x, norm2_g[1:2], mods[1], moe_w_router[0], moe_b_router[0])
    pos, tile_e, n_tiles = moe_layout(eid, rank, counts)
    xs = moe_dispatch(x, norm2_g[1:2], mods[1], pos)
    ys = moe_grouped_swiglu(xs, tile_e, n_tiles, moe_w1[0], moe_w3[0], moe_w2[0])
    x = moe_combine(x, mods[1], pos, wts, ys)

    return (x[:TP].reshape(BP, LP, D), x[TP:].reshape(BS, LS, D),
            new_C[:, None], new_n[:, None], new_m[:, None], new_k[:, None], new_v[:, None])
```

```python
import functools
import math

import numpy as np
import jax
import jax.numpy as jnp
from jax import lax
from jax.experimental import pallas as pl
from jax.experimental.pallas import tpu as pltpu

F32 = jnp.float32
BF16 = jnp.bfloat16
HIGHEST = lax.Precision.HIGHEST

D = 1024
GROUP = 1024
N_PROMPT_GROUPS = 8
HY_W = 512
ML_HEADS = 4
ML_HD = 128
ML_CHUNK = 256
EVEN_MAIN = 3 * HY_W + 4 * 512
N_GATES = 16
ATT_HD = 64
ATT_HEADS = 16
ATT_KV = 4
WINDOW = 128
GRID_W = 64
ROPE_BASE = 10000.0
D_FF = 2816
N_EXPERTS = 8
EPS = 1e-6
NEG = -1e30
VMEM_LIMIT = 56 * 1024 * 1024


def _cparams(*sem, flags=None):
    return pltpu.CompilerParams(dimension_semantics=sem, vmem_limit_bytes=VMEM_LIMIT, flags=flags)


def _mod_row(i, tm):
    return jnp.maximum(i * tm // GROUP - (N_PROMPT_GROUPS - 1), 0)


def _silu(x):
    return x * jax.nn.sigmoid(x)


def _bdot(a, b):
    return jnp.dot(a.astype(BF16), b.astype(BF16), preferred_element_type=F32)


def _norm_mod(x, g, sh, sc):
    y = x * lax.rsqrt(jnp.mean(x * x, axis=-1, keepdims=True) + EPS) * g
    return y * (1.0 + sc) + sh


def _adaln_kernel(c_ref, w_ref, b_ref, o_ref):
    s = _silu(c_ref[...])
    o_ref[...] = jnp.dot(s, w_ref[...], precision=HIGHEST, preferred_element_type=F32) + b_ref[...]


def adaln_table(cond, w_ada, b_ada):
    depth = w_ada.shape[0]
    tn = 1536
    out = pl.pallas_call(
        _adaln_kernel,
        grid=(depth, 6 * D // tn),
        in_specs=[pl.BlockSpec((16, D), lambda l, j: (0, 0)),
                  pl.BlockSpec((None, D, tn), lambda l, j: (l, 0, j)),
                  pl.BlockSpec((None, 1, tn), lambda l, j: (l, 0, j))],
        out_specs=pl.BlockSpec((None, 16, tn), lambda l, j: (l, 0, j)),
        out_shape=jax.ShapeDtypeStruct((depth, 16, 6 * D), F32),
        compiler_params=_cparams("parallel", "parallel"),
        name="adaln",
    )(cond, w_ada, b_ada.reshape(depth, 1, 6 * D))
    return out.reshape(depth, 16, 1, 6 * D)


def _mod_spec(k, tm):
    return pl.BlockSpec((None, 1, D), lambda i, *_: (_mod_row(i, tm), 0, k))


def _log_sigmoid(x):
    return jnp.minimum(x, 0.0) - jnp.log(1.0 + jnp.exp(-jnp.abs(x)))


def _split3(x):
    hi = x.astype(BF16)
    r = x - hi.astype(F32)
    mid = r.astype(BF16)
    return hi, mid, (r - mid.astype(F32)).astype(BF16)


def _even_in_kernel(x_ref, g_ref, sh_ref, sc_ref, w_ref, wg_ref, bg_ref, lo_ref, up_ref, u_ref, gate_ref, h_scr):
    @pl.when(pl.program_id(1) == 0)
    def _():
        h = _norm_mod(x_ref[...], g_ref[...], sh_ref[...], sc_ref[...]).astype(BF16)
        h_scr[...] = h
        gates = _bdot(h, wg_ref[...]) + bg_ref[...]
        lf = _log_sigmoid(gates)
        col = lax.broadcasted_iota(jnp.int32, (1, N_GATES), 1)
        is_forget = (col // ML_HEADS) % 2 == 1
        is_rev = col >= N_GATES // 2
        for ch in range(h.shape[0] // ML_CHUNK):
            sl = slice(ch * ML_CHUNK, (ch + 1) * ML_CHUNK)
            parts = _split3(lf[sl])
            cf = sum(jnp.dot(lo_ref[...], p, preferred_element_type=F32) for p in parts)
            cr = sum(jnp.dot(up_ref[...], p, preferred_element_type=F32) for p in parts)
            gate_ref[sl, :] = jnp.where(is_forget, jnp.where(is_rev, cr, cf), gates[sl])

    u_ref[...] = _bdot(h_scr[...], w_ref[...]).astype(u_ref.dtype)


def even_in_proj(x, g, mods, w_in, b_gate, tm=1024, tn=512):
    T = x.shape[0]
    tri = np.tril(np.ones((ML_CHUNK, ML_CHUNK), np.float32))
    lo, up = jnp.asarray(tri).astype(BF16), jnp.asarray(tri.T).astype(BF16)
    return pl.pallas_call(
        _even_in_kernel,
        grid=(T // tm, EVEN_MAIN // tn),
        in_specs=[pl.BlockSpec((tm, D), lambda i, j: (i, 0)),
                  pl.BlockSpec((1, D), lambda i, j: (0, 0)),
                  _mod_spec(0, tm), _mod_spec(1, tm),
                  pl.BlockSpec((D, tn), lambda i, j: (0, j)),
                  pl.BlockSpec((D, N_GATES), lambda i, j: (0, 0)),
                  pl.BlockSpec((1, N_GATES), lambda i, j: (0, 0)),
                  pl.BlockSpec((ML_CHUNK, ML_CHUNK), lambda i, j: (0, 0)),
                  pl.BlockSpec((ML_CHUNK, ML_CHUNK), lambda i, j: (0, 0))],
        out_specs=[pl.BlockSpec((tm, tn), lambda i, j: (i, j)),
                   pl.BlockSpec((tm, N_GATES), lambda i, j: (i, 0))],
        out_shape=[jax.ShapeDtypeStruct((T, EVEN_MAIN), BF16),
                   jax.ShapeDtypeStruct((T, N_GATES), F32)],
        scratch_shapes=[pltpu.VMEM((tm, D), BF16)],
        compiler_params=_cparams("parallel", "arbitrary"),
        name="even_in_proj",
    )(x, g, mods, mods, w_in, w_in[:, EVEN_MAIN:], b_gate, lo, up)


def _dft_tables(L):
    n = 2 * L
    f = np.arange(L, dtype=np.int64)[:, None]
    s = np.arange(L, dtype=np.int64)[None, :]
    ang = 2.0 * np.pi * ((f * s) % n).astype(np.float64) / n
    fwd = np.concatenate([np.cos(ang), -np.sin(ang)], axis=0)
    fwd[L, :] = np.where(np.arange(L) % 2 == 0, 1.0, -1.0)
    t = np.arange(L, dtype=np.int64)[:, None]
    ff = np.arange(L, dtype=np.int64)[None, :]
    ang = 2.0 * np.pi * ((t * ff) % n).astype(np.float64) / n
    inv_re = 2.0 * np.cos(ang) / n
    inv_re[:, 0] = 1.0 / n
    inv_im = -2.0 * np.sin(ang) / n
    inv_im[:, 0] = np.where(np.arange(L) % 2 == 0, 1.0, -1.0) / n
    inv = np.concatenate([inv_re, inv_im], axis=1)
    return fwd.astype(np.float32), inv.astype(np.float32)


def _filter_tables(L):
    t = np.linspace(0.0, 1.0, L, dtype=np.float32).astype(np.float64)[:, None]
    w = 2.0 * math.pi * np.arange(L, dtype=np.float64)[:, None] / L
    bands = np.linspace(1e-4, 16 - 1, 16, dtype=np.float32).astype(np.float64)[None, :]
    z = np.concatenate([t, np.cos(bands * w), -np.sin(bands * w)], axis=-1)
    zp = np.zeros((L, 128), np.float64)
    zp[:, :z.shape[1]] = z
    max_decay = math.log(1e-2) / 0.3
    min_decay = math.log(1e-2) / 1.5
    deltas = np.linspace(min_decay, max_decay, HY_W, dtype=np.float32).astype(np.float64)
    decay = np.exp(-t * np.abs(deltas))
    return zp.astype(np.float32), decay.astype(np.float32)


def _hy_filter_kernel(z_ref, dec_ref, w1_ref, b1_ref, w2_ref, b2_ref, w3_ref, fr_ref, fwd_ref,
                      ka_ref, kb_ref):
    L = z_ref.shape[0]
    hdot = functools.partial(jnp.dot, precision=HIGHEST, preferred_element_type=F32)
    h = jnp.sin(fr_ref[0:1, :] * (hdot(z_ref[...], w1_ref[...]) + b1_ref[...]))
    h = jnp.sin(fr_ref[1:2, :] * (hdot(h, w2_ref[...]) + b2_ref[...]))
    h = hdot(h, w3_ref[...])
    row0 = lax.broadcasted_iota(jnp.int32, (L, 1), 0) == 0
    h0 = h[:, :HY_W] * dec_ref[...]
    h1 = h[:, HY_W:] * dec_ref[...]
    l1 = jnp.sum(jnp.abs(h0), axis=0, keepdims=True) + jnp.sum(jnp.abs(h1), axis=0, keepdims=True)
    inv = 1.0 / l1
    h0 = h0 * inv
    h1 = jnp.where(row0, 0.0, h1 * inv)
    f0 = _bdot(fwd_ref[...], h0)
    f1 = _bdot(fwd_ref[...], h1)
    ka_ref[...] = f0[:L] + f1[:L]
    kb_ref[...] = jnp.where(row0, f0[L:] + f1[L:], f0[L:] - f1[L:])


def _const_spec(a, n_grid):
    return pl.BlockSpec(a.shape, lambda *_: (0,) * a.ndim, pipeline_mode=pl.Buffered(1))


def hyena_filter_spectra(L, w1, b1, w2, b2, w3, freq, fwd):
    z, dec = _filter_tables(L)
    pad2 = lambda a, r, c: jnp.pad(a, ((0, r - a.shape[0]), (0, c - a.shape[1])))
    args = (jnp.asarray(z), jnp.asarray(dec), pad2(w1, 128, 128), pad2(b1[None], 1, 128),
            pad2(w2, 128, 128), pad2(b2[None], 1, 128), pad2(w3, 128, 4 * HY_W), pad2(freq, 2, 128), fwd)
    in_specs = [_const_spec(a, 1) for a in args]
    in_specs[6] = pl.BlockSpec((128, 2 * HY_W), lambda o: (0, o))
    shp = jax.ShapeDtypeStruct((2, L, HY_W), F32)
    out_spec = pl.BlockSpec((None, L, HY_W), lambda o: (o, 0, 0))
    return pl.pallas_call(
        _hy_filter_kernel,
        grid=(2,),
        in_specs=in_specs,
        out_specs=[out_spec, out_spec],
        out_shape=[shp, shp],
        compiler_params=_cparams("arbitrary"),
        name=f"hyena_filter_{L}",
    )(*args)


def _hyena_kernel(u_ref, cw_ref, d_ref, fwd_ref, inv_ref, ka_ref, kb_ref, o_ref):
    nb, L = u_ref.shape[0], u_ref.shape[1]
    row = lax.broadcasted_iota(jnp.int32, (L, 1), 0)
    first, last = row == 0, row == L - 1
    fwd = fwd_ref[...].astype(BF16)
    inv = inv_ref[...].astype(BF16)

    def long_conv(z, o):
        zf = jnp.dot(fwd, z.astype(BF16), preferred_element_type=F32)
        a, b = zf[:L], zf[L:]
        ka, kb = ka_ref[o], kb_ref[o]
        yr = a * ka - jnp.where(first, 0.0, b * kb)
        yi = jnp.where(first, b * kb, a * kb + b * ka)
        return (jnp.dot(inv[:, :L], yr.astype(BF16), preferred_element_type=F32)
                + jnp.dot(inv[:, L:], yi.astype(BF16), preferred_element_type=F32))

    for bi in range(nb):
        u = u_ref[bi].astype(F32)
        prev = jnp.where(first, 0.0, pltpu.roll(u, 1, 0))
        nxt = jnp.where(last, 0.0, pltpu.roll(u, L - 1, 0))
        u = prev * cw_ref[0:1, :] + u * cw_ref[1:2, :] + nxt * cw_ref[2:3, :]
        v, x1, x2 = u[:, :HY_W], u[:, HY_W:2 * HY_W], u[:, 2 * HY_W:]
        z = x1 * (long_conv(v, 0) + d_ref[0:1, :] * v)
        z = x2 * (long_conv(z, 1) + d_ref[1:2, :] * z)
        o_ref[bi] = z.astype(o_ref.dtype)


def hyena_mix(u3, conv_w, d_skip, fwd, inv, ka, kb, nb):
    B, L, _ = u3.shape
    full = lambda a: _const_spec(a, 1)
    return pl.pallas_call(
        _hyena_kernel,
        grid=(B // nb,),
        in_specs=[pl.BlockSpec((nb, L, 3 * HY_W), lambda b: (b, 0, 0)),
                  full(conv_w), full(d_skip), full(fwd), full(inv), full(ka), full(kb)],
        out_specs=pl.BlockSpec((nb, L, HY_W), lambda b: (b, 0, 0)),
        out_shape=jax.ShapeDtypeStruct((B, L, HY_W), BF16),
        compiler_params=_cparams("parallel"),
        name=f"hyena_{L}",
    )(u3, conv_w, d_skip, fwd, inv, ka, kb)


def _mlstm_chunk(q, ks, vaug, li_c, b_c, li_r, b_r, caug, m, rev):
    T = q.shape[0]
    ri = lax.broadcasted_iota(jnp.int32, (T, T), 0)
    ci = lax.broadcasted_iota(jnp.int32, (T, T), 1)
    mask = (ci >= ri) if rev else (ci <= ri)
    b_end = b_c[0:1, :] if rev else b_c[T - 1:T, :]
    row = li_r - b_r
    dm = jnp.where(mask, b_c + row, NEG)
    inter = b_c + m
    m_t = jnp.maximum(inter, jnp.max(dm, axis=1, keepdims=True))
    w_intra = jnp.exp(dm - m_t)
    w_inter = jnp.exp(inter - m_t)
    s = lax.dot_general(q, ks, (((1,), (1,)), ((), ())), preferred_element_type=F32) * w_intra
    acc = _bdot(s, vaug) + w_inter * _bdot(q, caug)
    h = acc[:, :ML_HD] / jnp.maximum(jnp.abs(acc[:, ML_HD:]), jnp.exp(-m_t))
    m_new = jnp.maximum(b_end + m, jnp.max(b_end + row, axis=1, keepdims=True))
    decay = jnp.exp(b_end + m - m_new)
    kw = ks.astype(F32) * jnp.exp(b_end - b_c + li_c - m_new)
    caug_new = decay * caug + lax.dot_general(kw.astype(BF16), vaug, (((0,), (0,)), ((), ())),
                                              preferred_element_type=F32)
    return h, caug_new, m_new


def _mlstm_kernel(*refs, has_state, want_state):
    q_ref, k_ref, v_ref, o_ref, gc_ref, gr_ref, ng_ref = refs[:7]
    refs = refs[7:]
    if has_state:
        c0_ref, n0_ref, m0_ref = refs[:3]
        refs = refs[3:]
    y_ref = refs[0]
    if want_state:
        c_out, n_out, m_out = refs[1:4]
    L = q_ref.shape[0]
    T = min(ML_CHUNK, L)
    nc = L // T
    scale = 1.0 / math.sqrt(ML_HD)
    hs = [None] * nc
    for dr in range(2):
        if has_state:
            caug = jnp.concatenate([c0_ref[dr], n0_ref[dr]], axis=1)
            m = m0_ref[dr:dr + 1, 0:1]
        else:
            caug, m = jnp.zeros((ML_HD, 2 * ML_HD), F32), jnp.zeros((1, 1), F32)
        order = range(nc) if dr == 0 else range(nc - 1, -1, -1)
        for j in order:
            sl = slice(j * T, (j + 1) * T)
            q = q_ref[sl, :]
            ks = (k_ref[sl, :].astype(F32) * scale).astype(BF16)
            vaug = jnp.concatenate([v_ref[sl, :], jnp.ones((T, ML_HD), BF16)], axis=1)
            h, caug, m = _mlstm_chunk(q, ks, vaug, gc_ref[sl, 2 * dr:2 * dr + 1], gc_ref[sl, 2 * dr + 1:2 * dr + 2],
                                      gr_ref[2 * dr:2 * dr + 1, sl], gr_ref[2 * dr + 1:2 * dr + 2, sl],
                                      caug, m, rev=(dr == 1))
            hs[j] = h if hs[j] is None else hs[j] + h
        if want_state:
            c_out[dr] = caug[:, :ML_HD]
            n_out[dr:dr + 1, :] = caug[:, ML_HD:].T[0:1, :]
            m_out[dr:dr + 1, :] = jnp.broadcast_to(m, (1, ML_HD))
    for j in range(nc):
        sl = slice(j * T, (j + 1) * T)
        h = hs[j]
        y = h * lax.rsqrt(jnp.mean(h * h, axis=-1, keepdims=True) + EPS) * ng_ref[...]
        y_ref[sl, :] = (y * jax.nn.sigmoid(o_ref[sl, :].astype(F32))).astype(y_ref.dtype)


def mlstm_mix(u3, gates, norm_g, state=None, want_state=False):
    B, L, _ = u3.shape
    g5 = gates.reshape(B, L, 2, 2, ML_HEADS)
    gc = g5.transpose(0, 4, 1, 2, 3).reshape(B, ML_HEADS, L, 4)
    gr = g5.transpose(0, 4, 2, 3, 1).reshape(B, ML_HEADS, 4, L)
    col = lambda i: pl.BlockSpec((None, L, ML_HD), lambda b, h: (b, 0, (3 * HY_W + i * 512) // ML_HD + h))
    in_specs = [col(0), col(1), col(2), col(3),
                pl.BlockSpec((None, None, L, 4), lambda b, h: (b, h, 0, 0)),
                pl.BlockSpec((None, None, 4, L), lambda b, h: (b, h, 0, 0)),
                pl.BlockSpec((None, 1, ML_HD), lambda b, h: (h, 0, 0))]
    args = [u3, u3, u3, u3, gc, gr, norm_g.reshape(ML_HEADS, 1, ML_HD)]
    if state is not None:
        C0, n0, m0 = state
        cspec = pl.BlockSpec((None, 2, None, ML_HD, ML_HD), lambda b, h: (b, 0, h, 0, 0))
        in_specs += [cspec, cspec, pl.BlockSpec((None, None, 2, ML_HD), lambda b, h: (b, h, 0, 0))]
        args += [C0, jnp.broadcast_to(n0[..., None], C0.shape),
                 jnp.broadcast_to(m0.transpose(0, 2, 1)[..., None], (B, ML_HEADS, 2, ML_HD))]
    out_specs = [pl.BlockSpec((None, L, ML_HD), lambda b, h: (b, 0, h))]
    out_shape = [jax.ShapeDtypeStruct((B, L, ML_HEADS * ML_HD), BF16)]
    if want_state:
        out_specs += [pl.BlockSpec((None, 2, None, ML_HD, ML_HD), lambda b, h: (b, 0, h, 0, 0)),
                      pl.BlockSpec((None, None, 2, ML_HD), lambda b, h: (b, h, 0, 0)),
                      pl.BlockSpec((None, None, 2, ML_HD), lambda b, h: (b, h, 0, 0))]
        out_shape += [jax.ShapeDtypeStruct((B, 2, ML_HEADS, ML_HD, ML_HD), F32),
                      jax.ShapeDtypeStruct((B, ML_HEADS, 2, ML_HD), F32),
                      jax.ShapeDtypeStruct((B, ML_HEADS, 2, ML_HD), F32)]
    outs = pl.pallas_call(
        functools.partial(_mlstm_kernel, has_state=state is not None, want_state=want_state),
        grid=(B, ML_HEADS),
        in_specs=in_specs, out_specs=out_specs, out_shape=out_shape,
        compiler_params=_cparams("parallel", "parallel"),
        name=f"mlstm_{L}",
    )(*args)
    if not want_state:
        return outs[0]
    y, C, n, m = outs
    return y, C, n.transpose(0, 2, 1, 3), m[..., 0].transpose(0, 2, 1)


def _proj_res_kernel(*refs, n_in):
    a_refs = refs[:n_in]
    w_ref, x_ref, gate_ref, o_ref = refs[n_in:]
    k0 = 0
    acc = None
    for a_ref in a_refs:
        kw = a_ref.shape[1]
        part = _bdot(a_ref[...], w_ref[k0:k0 + kw, :])
        acc = part if acc is None else acc + part
        k0 += kw
    o_ref[...] = x_ref[...] + gate_ref[...] * acc


def proj_residual(acts, w, x, mods, gate_idx, tm=1024):
    T = x.shape[0]
    return pl.pallas_call(
        functools.partial(_proj_res_kernel, n_in=len(acts)),
        grid=(T // tm,),
        in_specs=[pl.BlockSpec((tm, a.shape[1]), lambda i: (i, 0)) for a in acts]
        + [pl.BlockSpec(w.shape, lambda i: (0, 0)),
           pl.BlockSpec((tm, D), lambda i: (i, 0)),
           _mod_spec(gate_idx, tm)],
        out_specs=pl.BlockSpec((tm, D), lambda i: (i, 0)),
        out_shape=jax.ShapeDtypeStruct((T, D), F32),
        compiler_params=_cparams("parallel"),
        name="proj_residual",
    )(*acts, w, x, mods)


def _ffn_kernel(x_ref, g_ref, sh_ref, sc_ref, gate_ref, w1_ref, w3_ref, w2_ref, o_ref, h_scr, acc_scr):
    c = pl.program_id(1)

    @pl.when(c == 0)
    def _():
        h_scr[...] = _norm_mod(x_ref[...], g_ref[...], sh_ref[...], sc_ref[...]).astype(BF16)
        acc_scr[...] = jnp.zeros_like(acc_scr)

    h = h_scr[...]
    mid = _silu(_bdot(h, w1_ref[...])) * _bdot(h, w3_ref[...])
    acc_scr[...] += _bdot(mid, w2_ref[...])

    @pl.when(c == pl.num_programs(1) - 1)
    def _():
        o_ref[...] = x_ref[...] + gate_ref[...] * acc_scr[...]


def ffn_residual(x, g, mods, w1, w3, w2, tm=1024, tf=256):
    T = x.shape[0]
    return pl.pallas_call(
        _ffn_kernel,
        grid=(T // tm, D_FF // tf),
        in_specs=[pl.BlockSpec((tm, D), lambda i, c: (i, 0)),
                  pl.BlockSpec((1, D), lambda i, c: (0, 0)),
                  _mod_spec(3, tm), _mod_spec(4, tm), _mod_spec(5, tm),
                  pl.BlockSpec((D, tf), lambda i, c: (0, c)),
                  pl.BlockSpec((D, tf), lambda i, c: (0, c)),
                  pl.BlockSpec((tf, D), lambda i, c: (c, 0))],
        out_specs=pl.BlockSpec((tm, D), lambda i, c: (i, 0)),
        out_shape=jax.ShapeDtypeStruct((T, D), F32),
        scratch_shapes=[pltpu.VMEM((tm, D), BF16), pltpu.VMEM((tm, D), F32)],
        compiler_params=_cparams("parallel", "arbitrary"),
        name="ffn",
    )(x, g, mods, mods, mods, w1, w3, w2)


def _qkv_kernel(x_ref, g_ref, sh_ref, sc_ref, w_ref, q_ref, kv_ref, h_scr):
    j = pl.program_id(1)

    @pl.when(j == 0)
    def _():
        h_scr[...] = _norm_mod(x_ref[...], g_ref[...], sh_ref[...], sc_ref[...]).astype(BF16)

    @pl.when(j < 2)
    def _():
        q_ref[...] = _bdot(h_scr[...], w_ref[...]).astype(q_ref.dtype)

    @pl.when(j == 2)
    def _():
        kv_ref[...] = _bdot(h_scr[...], w_ref[...])


def qkv_proj(x, g, mods, w_qkv, tm=1024):
    T = x.shape[0]
    tn = 512
    return pl.pallas_call(
        _qkv_kernel,
        grid=(T // tm, 3),
        in_specs=[pl.BlockSpec((tm, D), lambda i, j: (i, 0)),
                  pl.BlockSpec((1, D), lambda i, j: (0, 0)),
                  _mod_spec(0, tm), _mod_spec(1, tm),
                  pl.BlockSpec((D, tn), lambda i, j: (0, j))],
        out_specs=[pl.BlockSpec((tm, tn), lambda i, j: (i, jnp.minimum(j, 1))),
                   pl.BlockSpec((tm, tn), lambda i, j: (i, 0))],
        out_shape=[jax.ShapeDtypeStruct((T, ATT_HEADS * ATT_HD), BF16),
                   jax.ShapeDtypeStruct((T, 2 * ATT_KV * ATT_HD), F32)],
        scratch_shapes=[pltpu.VMEM((tm, D), BF16)],
        compiler_params=_cparams("parallel", "arbitrary"),
        name="qkv_proj",
    )(x, g, mods, mods, w_qkv)


def _rope_tables(L):
    half = ATT_HD // 2
    pos_r = (np.arange(L) // GRID_W).astype(np.float32)
    pos_c = (np.arange(L) % GRID_W).astype(np.float32)
    inv = (ROPE_BASE ** (-np.arange(0, half, 2, dtype=np.float32) / half)).astype(np.float32)
    cos = np.zeros((L, ATT_HD), np.float64)
    sin = np.zeros((L, ATT_HD), np.float64)
    for base, pos in ((0, pos_r), (half, pos_c)):
        ang = (pos[:, None] * inv[None, :]).astype(np.float32).astype(np.float64)
        cos[:, base:base + half] = np.concatenate([np.cos(ang), np.cos(ang)], axis=1)
        sin[:, base:base + half] = np.concatenate([-np.sin(ang), np.sin(ang)], axis=1)
    return (np.tile(cos, (1, 4)).astype(np.float32), np.tile(sin, (1, 4)).astype(np.float32))


def _seg_rms(x):
    w = x.shape[1]
    ri = lax.broadcasted_iota(jnp.int32, (w, w), 0) // ATT_HD
    ci = lax.broadcasted_iota(jnp.int32, (w, w), 1) // ATT_HD
    ss = _bdot(x * x, (ri == ci).astype(F32))
    return x * lax.rsqrt(ss * (1.0 / ATT_HD) + EPS)


LOG2E = 1.4426950408889634


def _exp2_bf16(x):
    return jnp.exp2(x.astype(BF16))


def _both_halves(tile, low):
    lane = lax.broadcasted_iota(jnp.int32, tile.shape, 1)
    other = pltpu.roll(tile, ATT_HD, 1)
    return jnp.where((lane < ATT_HD) == low, tile, other)


def _swap16(x):
    w = x.shape[1]
    lane = lax.broadcasted_iota(jnp.int32, x.shape, 1)
    return jnp.where(lane % 32 < 16, pltpu.roll(x, w - 16, 1), pltpu.roll(x, 16, 1))


def _attn_kernel(*refs, latent, tq):
    if latent:
        (q_ref, kv_ref, ck_ref, cv_ref, qg_ref, kg_ref, sink_ref, cosq_ref, sinq_ref, cosk_ref, sink_t_ref,
         o_ref, kk_scr, vt_scr, ckk_scr, cvt_scr) = refs
    else:
        q_ref, kv_ref, qg_ref, kg_ref, sink_ref, o_ref, ko_ref, vo_ref, kk_scr, vt_scr = refs
    L = kv_ref.shape[0]
    gw = ATT_KV * ATT_HD
    pw = 2 * ATT_HD
    qb = pl.program_id(1)

    vrows = vt_scr.shape[2]
    nblk = L // pw

    def vt_aug(tile, low):
        vt = tile.T[0:ATT_HD, :] if low else tile.T[ATT_HD:, :]
        return jnp.concatenate([vt, jnp.ones((vrows - ATT_HD, tile.shape[0]), F32)], axis=0).astype(BF16)

    @pl.when(qb == 0)
    def _():
        kn = _seg_rms(kv_ref[:, :gw]) * kg_ref[...]
        v = kv_ref[:, gw:]
        if latent:
            kn = kn * cosk_ref[...] + _swap16(kn) * sink_t_ref[...]
        else:
            for c in range(ATT_KV):
                ko_ref[c] = kn[:, c * ATT_HD:(c + 1) * ATT_HD]
                vo_ref[c] = v[:, c * ATT_HD:(c + 1) * ATT_HD]
        for c in range(ATT_KV):
            tile, low = slice((c // 2) * pw, (c // 2 + 1) * pw), c % 2 == 0
            kk_scr[c] = _both_halves(kn[:, tile], low).astype(BF16)
            for j in range(nblk):
                vt_scr[c, j] = vt_aug(v[j * pw:(j + 1) * pw, tile], low)
            if latent:
                ck, cv = ck_ref[c], cv_ref[c]
                ckk_scr[c] = jnp.concatenate([ck, ck], axis=1).astype(BF16)
                cvt_scr[c] = vt_aug(jnp.concatenate([cv, cv], axis=1), True)

    if latent:
        span = tq + 2 * WINDOW
        start = pl.multiple_of(jnp.clip(qb * tq - WINDOW, 0, L - span), WINDOW)
        blk0 = start // pw
        s_pos = start + lax.broadcasted_iota(jnp.int32, (span, tq), 0)
        t_pos = qb * tq + lax.broadcasted_iota(jnp.int32, (span, tq), 1)
        win = jnp.abs(t_pos - s_pos) <= WINDOW
    else:
        span, blk0 = L, 0

    nt = (((1,), (1,)), ((), ()))
    low_q = lax.broadcasted_iota(jnp.int32, (tq, pw), 1) < ATT_HD
    def group_scores(c):
        qc = _seg_rms(q_ref[:, c * gw:(c + 1) * gw].astype(F32)) * qg_ref[...]
        if latent:
            qc = qc * cosq_ref[...] + _swap16(qc) * sinq_ref[...]
            kw = kk_scr[c, pl.ds(start, span), :]
        else:
            kw = kk_scr[c]
        qc = qc * (LOG2E / math.sqrt(ATT_HD))
        scores = []
        for g in range(ATT_KV):
            qt = qc[:, (g // 2) * pw:(g // 2 + 1) * pw]
            qm = jnp.where(low_q if g % 2 == 0 else ~low_q, qt, 0.0).astype(BF16)
            lw = lax.dot_general(kw, qm, nt, preferred_element_type=F32)
            lc = lax.dot_general(ckk_scr[c], qm, nt, preferred_element_type=F32) if latent else None
            scores.append((lw, lc))
        return scores

    def group_outputs(c, scores):
        vw = jnp.concatenate([vt_scr[c, blk0 + j] for j in range(span // pw)], axis=1)
        outs = []
        for g in range(ATT_KV):
            head = c * ATT_KV + g
            sink = sink_ref[:, head:head + 1] * LOG2E
            lw, lc = scores[g]
            if latent:
                lw = jnp.where(win, lw, NEG)
                mx = jnp.maximum(jnp.maximum(jnp.max(lw, axis=0, keepdims=True),
                                             jnp.max(lc, axis=0, keepdims=True)), sink)
                r = jnp.dot(vw, _exp2_bf16(lw - mx), preferred_element_type=F32) + jnp.dot(
                    cvt_scr[c], _exp2_bf16(lc - mx), preferred_element_type=F32)
            else:
                mx = jnp.maximum(jnp.max(lw, axis=0, keepdims=True), sink)
                r = jnp.dot(vw, _exp2_bf16(lw - mx), preferred_element_type=F32)
            den = r[ATT_HD:ATT_HD + 1, :] + jnp.exp2(sink - mx)
            outs.append(r[0:ATT_HD, :] / den)
        for t in range(2):
            o_ref[:, c * gw + t * pw:c * gw + (t + 1) * pw] = (
                jnp.concatenate(outs[2 * t:2 * t + 2], axis=0).T.astype(o_ref.dtype))

    scores = group_scores(0)
    for c in range(ATT_KV):
        nxt = group_scores(c + 1) if c + 1 < ATT_KV else None
        group_outputs(c, scores)
        scores = nxt


def attention(q, kv, q_g, k_g, sink, B, L, cache=None, tq=256):
    latent = cache is not None
    gw = ATT_KV * ATT_HD
    qg = jnp.tile(q_g, ATT_KV)[None]
    kg = jnp.tile(k_g, ATT_KV)[None]
    nq = L // tq
    const = lambda a: pl.BlockSpec(a.shape, lambda b, i: (0,) * a.ndim)
    in_specs = [pl.BlockSpec((tq, ATT_HEADS * ATT_HD), lambda b, i: (b * nq + i, 0)),
                pl.BlockSpec((L, 2 * gw), lambda b, i: (b, 0))]
    args = [q, kv]
    vrows = ATT_HD + 16
    scratch = [pltpu.VMEM((ATT_KV, L, 2 * ATT_HD), BF16),
               pltpu.VMEM((ATT_KV, L // (2 * ATT_HD), vrows, 2 * ATT_HD), BF16)]
    out_specs = [pl.BlockSpec((tq, ATT_HEADS * ATT_HD), lambda b, i: (b * nq + i, 0))]
    out_shape = [jax.ShapeDtypeStruct((B * L, ATT_HEADS * ATT_HD), BF16)]
    if latent:
        ck, cv = cache
        P = ck.shape[2]
        cos, sin = (jnp.asarray(t) for t in _rope_tables(L))
        in_specs += [pl.BlockSpec((None, ATT_KV, P, ATT_HD), lambda b, i: (b, 0, 0, 0))] * 2
        args += [ck, cv]
        in_specs += [const(qg), const(kg), pl.BlockSpec((1, ATT_HEADS), lambda b, i: (0, 0)),
                     pl.BlockSpec((tq, gw), lambda b, i: (i, 0)), pl.BlockSpec((tq, gw), lambda b, i: (i, 0)),
                     const(cos), const(sin)]
        args += [qg, kg, sink[None], cos, sin, cos, sin]
        scratch += [pltpu.VMEM((ATT_KV, P, 2 * ATT_HD), BF16), pltpu.VMEM((ATT_KV, vrows, P), BF16)]
    else:
        in_specs += [const(qg), const(kg), pl.BlockSpec((1, ATT_HEADS), lambda b, i: (0, 0))]
        args += [qg, kg, sink[None]]
        cache_spec = pl.BlockSpec((None, ATT_KV, L, ATT_HD), lambda b, i: (b, 0, 0, 0))
        out_specs += [cache_spec, cache_spec]
        out_shape += [jax.ShapeDtypeStruct((B, ATT_KV, L, ATT_HD), F32)] * 2
    outs = pl.pallas_call(
        functools.partial(_attn_kernel, latent=latent, tq=tq),
        grid=(B, nq),
        in_specs=in_specs, out_specs=out_specs, out_shape=out_shape,
        scratch_shapes=scratch,
        compiler_params=_cparams("parallel", "arbitrary"),
        name="attn_latent" if latent else "attn_context",
    )(*args)
    return outs[0] if latent else outs


MOE_TM = 1024
MOE_MAX_TILES = 2 * 16384 // MOE_TM + N_EXPERTS


def _router_kernel(x_ref, g_ref, sh_ref, sc_ref, wr_ref, br_ref, tri_ref, eid_ref, rank_ref, wts_ref, cnt_ref):
    @pl.when(pl.program_id(0) == 0)
    def _():
        cnt_ref[...] = jnp.zeros_like(cnt_ref)

    h = _norm_mod(x_ref[...], g_ref[...], sh_ref[...], sc_ref[...])
    lg = lax.dot_general(wr_ref[...], h, (((1,), (1,)), ((), ())), precision=HIGHEST,
                         preferred_element_type=F32) + br_ref[...]
    row = lax.broadcasted_iota(jnp.int32, lg.shape, 0)
    m1 = jnp.max(lg, axis=0, keepdims=True)
    i1 = jnp.min(jnp.where(lg == m1, row, N_EXPERTS), axis=0, keepdims=True)
    l2 = jnp.where(row == i1, -jnp.inf, lg)
    m2 = jnp.max(l2, axis=0, keepdims=True)
    i2 = jnp.min(jnp.where(l2 == m2, row, N_EXPERTS), axis=0, keepdims=True)
    e2 = jnp.exp(m2 - m1)
    w1 = 1.0 / (1.0 + e2)
    eid_ref[...] = jnp.concatenate([i1, i2], axis=0)
    wts_ref[...] = jnp.concatenate([w1, e2 * w1], axis=0)
    oh1 = (row == i1).astype(F32)
    oh2 = (row == i2).astype(F32)
    cs1 = _bdot(oh1, tri_ref[...])
    cs2 = _bdot(oh2, tri_ref[...])
    tot1 = jnp.sum(oh1, axis=1, keepdims=True)
    cnt = cnt_ref[:, 0:1]
    r1 = jnp.sum(oh1 * (cnt + cs1), axis=0, keepdims=True)
    r2 = jnp.sum(oh2 * (cnt + tot1 + cs2), axis=0, keepdims=True)
    rank_ref[...] = jnp.concatenate([r1, r2], axis=0).astype(jnp.int32)
    cnt_ref[...] = cnt_ref[...] + tot1 + jnp.sum(oh2, axis=1, keepdims=True)


def moe_router(x, g, mods, w_router, b_router, tm=1024):
    T = x.shape[0]
    tri = jnp.asarray(np.triu(np.ones((tm, tm), np.float32), k=1)).astype(BF16)
    tok2 = lambda dt: jax.ShapeDtypeStruct((2, T), dt)
    return pl.pallas_call(
        _router_kernel,
        grid=(T // tm,),
        in_specs=[pl.BlockSpec((tm, D), lambda i: (i, 0)),
                  pl.BlockSpec((1, D), lambda i: (0, 0)),
                  _mod_spec(3, tm), _mod_spec(4, tm),
                  pl.BlockSpec((N_EXPERTS, D), lambda i: (0, 0)),
                  pl.BlockSpec((N_EXPERTS, 1), lambda i: (0, 0)),
                  _const_spec(tri, 1)],
        out_specs=[pl.BlockSpec((2, tm), lambda i: (0, i)),
                   pl.BlockSpec((2, tm), lambda i: (0, i)),
                   pl.BlockSpec((2, tm), lambda i: (0, i)),
                   pl.BlockSpec((N_EXPERTS, 128), lambda i: (0, 0))],
        out_shape=[tok2(jnp.int32), tok2(jnp.int32), tok2(F32),
                   jax.ShapeDtypeStruct((N_EXPERTS, 128), F32)],
        compiler_params=_cparams("arbitrary"),
        name="moe_router",
    )(x, g, mods, mods, w_router.T, b_router[:, None], tri)


def moe_layout(eid, rank, counts):
    cnt = counts[:, 0].astype(jnp.int32)
    tiles = (cnt + MOE_TM - 1) // MOE_TM
    tile_end = jnp.cumsum(tiles)
    start = (tile_end - tiles) * MOE_TM
    pos = rank + jnp.sum(jnp.where(eid[..., None] == jnp.arange(N_EXPERTS), start, 0), axis=-1)
    n_tiles = tile_end[-1]
    t = jnp.arange(MOE_MAX_TILES, dtype=jnp.int32)
    tile_e = jnp.sum(t[:, None] >= tile_end[None, :], axis=1).astype(jnp.int32)
    last_e = jnp.sum((n_tiles - 1) >= tile_end).astype(jnp.int32)
    tile_e = jnp.where(t < n_tiles, tile_e, last_e)
    return pos.astype(jnp.int32), tile_e, n_tiles.astype(jnp.int32).reshape(1)


def _row_copy(src, i, dst, j, sem):
    return pltpu.make_async_copy(src.at[pl.ds(i, 1), :], dst.at[pl.ds(j, 1), :], sem)


def _dispatch_kernel(pos_ref, x_ref, g_ref, sh_ref, sc_ref, xs_in_ref, xs_ref, h_scr, sem):
    del xs_in_ref
    tm = h_scr.shape[0]
    h_scr[...] = _norm_mod(x_ref[...], g_ref[...], sh_ref[...], sc_ref[...])

    def issue(r, carry):
        for s in range(2):
            _row_copy(h_scr, r, xs_ref, pos_ref[s, r], sem.at[s]).start()
        return carry

    lax.fori_loop(0, tm, issue, 0, unroll=8)
    for s in range(2):
        pltpu.make_async_copy(h_scr, xs_ref.at[pl.ds(0, tm), :], sem.at[s]).wait()


def moe_dispatch(x, g, mods, pos, tm=1024):
    T = x.shape[0]
    n_rows = MOE_MAX_TILES * MOE_TM
    pos3 = pos.reshape(2, T // tm, tm).transpose(1, 0, 2)
    return pl.pallas_call(
        _dispatch_kernel,
        grid=(T // tm,),
        in_specs=[pl.BlockSpec((None, 2, tm), lambda i: (i, 0, 0), memory_space=pltpu.SMEM),
                  pl.BlockSpec((tm, D), lambda i: (i, 0)),
                  pl.BlockSpec((1, D), lambda i: (0, 0)),
                  _mod_spec(3, tm), _mod_spec(4, tm),
                  pl.BlockSpec(memory_space=pl.ANY)],
        out_specs=pl.BlockSpec(memory_space=pl.ANY),
        out_shape=jax.ShapeDtypeStruct((n_rows, D), F32),
        scratch_shapes=[pltpu.VMEM((tm, D), F32), pltpu.SemaphoreType.DMA((2,))],
        input_output_aliases={5: 0},
        compiler_params=_cparams("arbitrary"),
        name="moe_dispatch",
    )(pos3, x, g, mods, mods, jnp.zeros((n_rows, D), F32))


def _moe_group_kernel(te_ref, nt_ref, x_ref, w1_ref, w3_ref, w2_ref, o_ref, h_scr, acc_scr):
    t, c = pl.program_id(0), pl.program_id(1)

    @pl.when(t < nt_ref[0])
    def _():
        @pl.when(c == 0)
        def _():
            h_scr[...] = x_ref[...].astype(BF16)
            acc_scr[...] = jnp.zeros_like(acc_scr)

        h = h_scr[...]
        mid = _silu(_bdot(h, w1_ref[...])) * _bdot(h, w3_ref[...])
        acc_scr[...] += _bdot(mid, w2_ref[...])

        @pl.when(c == pl.num_programs(1) - 1)
        def _():
            o_ref[...] = acc_scr[...]

    @pl.when((t >= nt_ref[0]) & (c == pl.num_programs(1) - 1))
    def _():
        o_ref[...] = jnp.zeros_like(o_ref)


def moe_grouped_swiglu(xs, tile_e, n_tiles, w1, w3, w2, tf=256):
    nc = D_FF // tf
    live = lambda t, nt: t < nt[0]
    row = lambda t, c, te, nt: (jnp.where(live(t, nt), t, nt[0] - 1), 0)
    wcol = lambda t, c, te, nt: (te[t], 0, jnp.where(live(t, nt), c, nc - 1))
    wrow = lambda t, c, te, nt: (te[t], jnp.where(live(t, nt), c, nc - 1), 0)
    return pl.pallas_call(
        _moe_group_kernel,
        grid_spec=pltpu.PrefetchScalarGridSpec(
            num_scalar_prefetch=2,
            grid=(MOE_MAX_TILES, nc),
            in_specs=[pl.BlockSpec((MOE_TM, D), row),
                      pl.BlockSpec((None, D, tf), wcol),
                      pl.BlockSpec((None, D, tf), wcol),
                      pl.BlockSpec((None, tf, D), wrow)],
            out_specs=pl.BlockSpec((MOE_TM, D), lambda t, c, te, nt: (t, 0)),
            scratch_shapes=[pltpu.VMEM((MOE_TM, D), BF16), pltpu.VMEM((MOE_TM, D), F32)]),
        out_shape=jax.ShapeDtypeStruct(xs.shape, F32),
        compiler_params=_cparams("arbitrary", "arbitrary"),
        name="moe_grouped",
    )(tile_e, n_tiles, xs, w1, w3, w2)


def _combine_kernel(pos_ref, x_ref, gate_ref, wt_ref, ys_ref, o_ref, a_scr, b_scr, sem):
    tm = a_scr.shape[0]
    bufs = (a_scr, b_scr)

    def issue(r, carry):
        for s in range(2):
            _row_copy(ys_ref, pos_ref[s, r], bufs[s], r, sem.at[s]).start()
        return carry

    lax.fori_loop(0, tm, issue, 0, unroll=8)
    for s in range(2):
        pltpu.make_async_copy(ys_ref.at[pl.ds(0, tm), :], bufs[s], sem.at[s]).wait()
    moe = wt_ref[:, 0:1] * a_scr[...] + wt_ref[:, 1:2] * b_scr[...]
    o_ref[...] = x_ref[...] + gate_ref[...] * moe


def moe_combine(x, mods, pos, wts, ys, tm=1024):
    T = x.shape[0]
    pos3 = pos.reshape(2, T // tm, tm).transpose(1, 0, 2)
    return pl.pallas_call(
        _combine_kernel,
        grid=(T // tm,),
        in_specs=[pl.BlockSpec((None, 2, tm), lambda i: (i, 0, 0), memory_space=pltpu.SMEM),
                  pl.BlockSpec((tm, D), lambda i: (i, 0)),
                  _mod_spec(5, tm),
                  pl.BlockSpec((tm, 2), lambda i: (i, 0)),
                  pl.BlockSpec(memory_space=pl.ANY)],
        out_specs=pl.BlockSpec((tm, D), lambda i: (i, 0)),
        out_shape=jax.ShapeDtypeStruct((T, D), F32),
        scratch_shapes=[pltpu.VMEM((tm, D), F32), pltpu.VMEM((tm, D), F32), pltpu.SemaphoreType.DMA((2,))],
        compiler_params=_cparams("arbitrary"),
        name="moe_combine",
    )(pos3, x, mods, wts.T, ys)


def kernel(x_prompt, x_sample, state_C, state_n, state_m, cache_k, cache_v, c, c_ctx, norm1_g, norm2_g, w_ada, b_ada, ev_w_in, ev_conv, hy_w1, hy_b1, hy_w2, hy_b2, hy_w3, hy_freq, hy_d, ml_b_gate, ml_norm_g, ev_w_out, ff_w1, ff_w3, ff_w2, at_w_qkv, at_q_g, at_k_g, at_sink, at_w_out, moe_w_router, moe_b_router, moe_w1, moe_w3, moe_w2):
    BP, LP, _ = x_prompt.shape
    BS, LS, _ = x_sample.shape
    TP = BP * LP
    assert TP % GROUP == 0 and TP // GROUP == N_PROMPT_GROUPS and LS == GROUP and BS == 8

    x = jnp.concatenate([x_prompt.reshape(TP, D), x_sample.reshape(BS * LS, D)], axis=0)
    cond = jnp.concatenate([c_ctx[None], c, jnp.zeros((16 - 1 - BS, D), F32)], axis=0)
    mods = adaln_table(cond, w_ada, b_ada)

    u, gates = even_in_proj(x, norm1_g[0:1], mods[0], ev_w_in[0], ml_b_gate[0].reshape(1, N_GATES))
    up = u[:TP].reshape(BP, LP, EVEN_MAIN)
    us = u[TP:].reshape(BS, LS, EVEN_MAIN)
    hy = []
    for uu, L, nb in ((up, LP, 4), (us, LS, 1)):
        fwd, inv = (jnp.asarray(t).astype(BF16) for t in _dft_tables(L))
        ka, kb = hyena_filter_spectra(L, hy_w1[0], hy_b1[0], hy_w2[0], hy_b2[0], hy_w3[0], hy_freq[0], fwd)
        hy.append(hyena_mix(uu, ev_conv[0], hy_d[0], fwd, inv, ka, kb, nb).reshape(-1, HY_W))
    y_hy = jnp.concatenate(hy, axis=0)
    ml_p, new_C, new_n, new_m = mlstm_mix(up, gates[:TP], ml_norm_g[0], want_state=True)
    ml_s = mlstm_mix(us, gates[TP:], ml_norm_g[0], state=(state_C[:, 0], state_n[:, 0], state_m[:, 0]))
    y_ml = jnp.concatenate([ml_p.reshape(TP, 512), ml_s.reshape(BS * LS, 512)], axis=0)
    x = proj_residual([y_hy, y_ml], ev_w_out[0], x, mods[0], 2)
    x = ffn_residual(x, norm2_g[0:1], mods[0], ff_w1[0], ff_w3[0], ff_w2[0])

    q, kv = qkv_proj(x, norm1_g[1:2], mods[1], at_w_qkv[0])
    o_p, new_k, new_v = attention(q[:TP], kv[:TP], at_q_g[0], at_k_g[0], at_sink[0], BP, LP)
    o_s = attention(q[TP:], kv[TP:], at_q_g[0], at_k_g[0], at_sink[0], BS, LS,
                    cache=(cache_k[:, 0], cache_v[:, 0]))
    x = proj_residual([jnp.concatenate([o_p, o_s], axis=0)], at_w_out[0], x, mods[1], 2)
    eid, rank, wts, counts = moe_router(x, norm2_g[1:2], mods[1], moe_w_router[0], moe_b_router[0])
    pos, tile_e, n_tiles = moe_layout(eid, rank, counts)
    xs = moe_dispatch(x, norm2_g[1:2], mods[1], pos)
    ys = moe_grouped_swiglu(xs, tile_e, n_tiles, moe_w1[0], moe_w3[0], moe_w2[0])
    x = moe_combine(x, mods[1], pos, wts, ys)

    return (x[:TP].reshape(BP, LP, D), x[TP:].reshape(BS, LS, D),
            new_C[:, None], new_n[:, None], new_m[:, None], new_k[:, None], new_v[:, None])
```

```python
import functools
import math

import numpy as np
import jax
import jax.numpy as jnp
from jax import lax
from jax.experimental import pallas as pl
from jax.experimental.pallas import tpu as pltpu

F32 = jnp.float32
BF16 = jnp.bfloat16
HIGHEST = lax.Precision.HIGHEST

D = 1024
GROUP = 1024
N_PROMPT_GROUPS = 8
HY_W = 512
ML_HEADS = 4
ML_HD = 128
ML_CHUNK = 256
EVEN_MAIN = 3 * HY_W + 4 * 512
N_GATES = 16
ATT_HD = 64
ATT_HEADS = 16
ATT_KV = 4
WINDOW = 128
GRID_W = 64
ROPE_BASE = 10000.0
D_FF = 2816
N_EXPERTS = 8
EPS = 1e-6
NEG = -1e30
VMEM_LIMIT = 56 * 1024 * 1024


def _cparams(*sem, flags=None):
    return pltpu.CompilerParams(dimension_semantics=sem, vmem_limit_bytes=VMEM_LIMIT, flags=flags)


def _mod_row(i, tm):
    return jnp.maximum(i * tm // GROUP - (N_PROMPT_GROUPS - 1), 0)


def _silu(x):
    return x * jax.nn.sigmoid(x)


def _bdot(a, b):
    return jnp.dot(a.astype(BF16), b.astype(BF16), preferred_element_type=F32)


def _norm_mod(x, g, sh, sc):
    y = x * lax.rsqrt(jnp.mean(x * x, axis=-1, keepdims=True) + EPS) * g
    return y * (1.0 + sc) + sh


def _adaln_kernel(c_ref, w_ref, b_ref, o_ref):
    s = _silu(c_ref[...])
    o_ref[...] = jnp.dot(s, w_ref[...], precision=HIGHEST, preferred_element_type=F32) + b_ref[...]


def adaln_table(cond, w_ada, b_ada):
    depth = w_ada.shape[0]
    tn = 1536
    out = pl.pallas_call(
        _adaln_kernel,
        grid=(depth, 6 * D // tn),
        in_specs=[pl.BlockSpec((16, D), lambda l, j: (0, 0)),
                  pl.BlockSpec((None, D, tn), lambda l, j: (l, 0, j)),
                  pl.BlockSpec((None, 1, tn), lambda l, j: (l, 0, j))],
        out_specs=pl.BlockSpec((None, 16, tn), lambda l, j: (l, 0, j)),
        out_shape=jax.ShapeDtypeStruct((depth, 16, 6 * D), F32),
        compiler_params=_cparams("parallel", "parallel"),
        name="adaln",
    )(cond, w_ada, b_ada.reshape(depth, 1, 6 * D))
    return out.reshape(depth, 16, 1, 6 * D)


def _part_specs(block, n_p):
    return [pl.BlockSpec(block, lambda i, *_: (jnp.minimum(i, n_p - 1), 0)),
            pl.BlockSpec(block, lambda i, *_: (jnp.maximum(i - n_p, 0), 0))]


def _pick(is_prompt, p_ref, s_ref):
    return jnp.where(is_prompt, p_ref[...], s_ref[...])


def _mod_spec(k, tm):
    return pl.BlockSpec((None, 1, D), lambda i, *_: (_mod_row(i, tm), 0, k))


def _log_sigmoid(x):
    return jnp.minimum(x, 0.0) - jnp.log(1.0 + jnp.exp(-jnp.abs(x)))


def _split3(x):
    hi = x.astype(BF16)
    r = x - hi.astype(F32)
    mid = r.astype(BF16)
    return hi, mid, (r - mid.astype(F32)).astype(BF16)


def _even_in_kernel(xp_ref, xs_ref, g_ref, sh_ref, sc_ref, w_ref, wg_ref, bg_ref, lo_ref, up_ref, u_ref, gate_ref,
                    h_scr, *, n_p):
    is_prompt = pl.program_id(0) < n_p

    @pl.when(pl.program_id(1) == 0)
    def _():
        h = _norm_mod(_pick(is_prompt, xp_ref, xs_ref), g_ref[...], sh_ref[...], sc_ref[...]).astype(BF16)
        h_scr[...] = h
        gates = _bdot(h, wg_ref[...]) + bg_ref[...]
        lf = _log_sigmoid(gates)
        col = lax.broadcasted_iota(jnp.int32, (1, N_GATES), 1)
        is_forget = (col // ML_HEADS) % 2 == 1
        is_rev = col >= N_GATES // 2
        for ch in range(h.shape[0] // ML_CHUNK):
            sl = slice(ch * ML_CHUNK, (ch + 1) * ML_CHUNK)
            parts = _split3(lf[sl])
            cf = sum(jnp.dot(lo_ref[...], p, preferred_element_type=F32) for p in parts)
            cr = sum(jnp.dot(up_ref[...], p, preferred_element_type=F32) for p in parts)
            gate_ref[sl, :] = jnp.where(is_forget, jnp.where(is_rev, cr, cf), gates[sl])

    u_ref[...] = _bdot(h_scr[...], w_ref[...]).astype(u_ref.dtype)


def even_in_proj(xp, xs, g, mods, w_in, b_gate, tm=1024, tn=512):
    T = xp.shape[0] + xs.shape[0]
    tri = np.tril(np.ones((ML_CHUNK, ML_CHUNK), np.float32))
    lo, up = jnp.asarray(tri).astype(BF16), jnp.asarray(tri.T).astype(BF16)
    return pl.pallas_call(
        functools.partial(_even_in_kernel, n_p=xp.shape[0] // tm),
        grid=(T // tm, EVEN_MAIN // tn),
        in_specs=_part_specs((tm, D), xp.shape[0] // tm) + [
                  pl.BlockSpec((1, D), lambda i, j: (0, 0)),
                  _mod_spec(0, tm), _mod_spec(1, tm),
                  pl.BlockSpec((D, tn), lambda i, j: (0, j)),
                  pl.BlockSpec((D, N_GATES), lambda i, j: (0, 0)),
                  pl.BlockSpec((1, N_GATES), lambda i, j: (0, 0)),
                  pl.BlockSpec((ML_CHUNK, ML_CHUNK), lambda i, j: (0, 0)),
                  pl.BlockSpec((ML_CHUNK, ML_CHUNK), lambda i, j: (0, 0))],
        out_specs=[pl.BlockSpec((tm, tn), lambda i, j: (i, j)),
                   pl.BlockSpec((tm, N_GATES), lambda i, j: (i, 0))],
        out_shape=[jax.ShapeDtypeStruct((T, EVEN_MAIN), BF16),
                   jax.ShapeDtypeStruct((T, N_GATES), F32)],
        scratch_shapes=[pltpu.VMEM((tm, D), BF16)],
        compiler_params=_cparams("parallel", "arbitrary"),
        name="even_in_proj",
    )(xp, xs, g, mods, mods, w_in, w_in[:, EVEN_MAIN:], b_gate, lo, up)


def _dft_tables(L):
    n = 2 * L
    f = np.arange(L, dtype=np.int64)[:, None]
    s = np.arange(L, dtype=np.int64)[None, :]
    ang = 2.0 * np.pi * ((f * s) % n).astype(np.float64) / n
    fwd = np.concatenate([np.cos(ang), -np.sin(ang)], axis=0)
    fwd[L, :] = np.where(np.arange(L) % 2 == 0, 1.0, -1.0)
    t = np.arange(L, dtype=np.int64)[:, None]
    ff = np.arange(L, dtype=np.int64)[None, :]
    ang = 2.0 * np.pi * ((t * ff) % n).astype(np.float64) / n
    inv_re = 2.0 * np.cos(ang) / n
    inv_re[:, 0] = 1.0 / n
    inv_im = -2.0 * np.sin(ang) / n
    inv_im[:, 0] = np.where(np.arange(L) % 2 == 0, 1.0, -1.0) / n
    inv = np.concatenate([inv_re, inv_im], axis=1)
    return fwd.astype(np.float32), inv.astype(np.float32)


def _filter_tables(L):
    t = np.linspace(0.0, 1.0, L, dtype=np.float32).astype(np.float64)[:, None]
    w = 2.0 * math.pi * np.arange(L, dtype=np.float64)[:, None] / L
    bands = np.linspace(1e-4, 16 - 1, 16, dtype=np.float32).astype(np.float64)[None, :]
    z = np.concatenate([t, np.cos(bands * w), -np.sin(bands * w)], axis=-1)
    zp = np.zeros((L, 128), np.float64)
    zp[:, :z.shape[1]] = z
    max_decay = math.log(1e-2) / 0.3
    min_decay = math.log(1e-2) / 1.5
    deltas = np.linspace(min_decay, max_decay, HY_W, dtype=np.float32).astype(np.float64)
    decay = np.exp(-t * np.abs(deltas))
    return zp.astype(np.float32), decay.astype(np.float32)


def _hy_filter_kernel(z_ref, dec_ref, w1_ref, b1_ref, w2_ref, b2_ref, w3_ref, fr_ref, fwd_ref,
                      ka_ref, kb_ref):
    L = z_ref.shape[0]
    hdot = functools.partial(jnp.dot, precision=HIGHEST, preferred_element_type=F32)
    h = jnp.sin(fr_ref[0:1, :] * (hdot(z_ref[...], w1_ref[...]) + b1_ref[...]))
    h = jnp.sin(fr_ref[1:2, :] * (hdot(h, w2_ref[...]) + b2_ref[...]))
    h = hdot(h, w3_ref[...])
    row0 = lax.broadcasted_iota(jnp.int32, (L, 1), 0) == 0
    h0 = h[:, :HY_W] * dec_ref[...]
    h1 = h[:, HY_W:] * dec_ref[...]
    l1 = jnp.sum(jnp.abs(h0), axis=0, keepdims=True) + jnp.sum(jnp.abs(h1), axis=0, keepdims=True)
    inv = 1.0 / l1
    h0 = h0 * inv
    h1 = jnp.where(row0, 0.0, h1 * inv)
    f0 = _bdot(fwd_ref[...], h0)
    f1 = _bdot(fwd_ref[...], h1)
    ka_ref[...] = f0[:L] + f1[:L]
    kb_ref[...] = jnp.where(row0, f0[L:] + f1[L:], f0[L:] - f1[L:])


def _const_spec(a, n_grid):
    return pl.BlockSpec(a.shape, lambda *_: (0,) * a.ndim, pipeline_mode=pl.Buffered(1))


def hyena_filter_spectra(L, w1, b1, w2, b2, w3, freq, fwd):
    z, dec = _filter_tables(L)
    pad2 = lambda a, r, c: jnp.pad(a, ((0, r - a.shape[0]), (0, c - a.shape[1])))
    args = (jnp.asarray(z), jnp.asarray(dec), pad2(w1, 128, 128), pad2(b1[None], 1, 128),
            pad2(w2, 128, 128), pad2(b2[None], 1, 128), pad2(w3, 128, 4 * HY_W), pad2(freq, 2, 128), fwd)
    in_specs = [_const_spec(a, 1) for a in args]
    in_specs[6] = pl.BlockSpec((128, 2 * HY_W), lambda o: (0, o))
    shp = jax.ShapeDtypeStruct((2, L, HY_W), F32)
    out_spec = pl.BlockSpec((None, L, HY_W), lambda o: (o, 0, 0))
    return pl.pallas_call(
        _hy_filter_kernel,
        grid=(2,),
        in_specs=in_specs,
        out_specs=[out_spec, out_spec],
        out_shape=[shp, shp],
        compiler_params=_cparams("arbitrary"),
        name=f"hyena_filter_{L}",
    )(*args)


def _hyena_kernel(u_ref, cw_ref, d_ref, fwd_ref, inv_ref, ka_ref, kb_ref, o_ref):
    nb, L = u_ref.shape[0], u_ref.shape[1]
    row = lax.broadcasted_iota(jnp.int32, (L, 1), 0)
    first, last = row == 0, row == L - 1
    fwd = fwd_ref[...].astype(BF16)
    inv = inv_ref[...].astype(BF16)

    def long_conv(z, o):
        zf = jnp.dot(fwd, z.astype(BF16), preferred_element_type=F32)
        a, b = zf[:L], zf[L:]
        ka, kb = ka_ref[o], kb_ref[o]
        yr = a * ka - jnp.where(first, 0.0, b * kb)
        yi = jnp.where(first, b * kb, a * kb + b * ka)
        return (jnp.dot(inv[:, :L], yr.astype(BF16), preferred_element_type=F32)
                + jnp.dot(inv[:, L:], yi.astype(BF16), preferred_element_type=F32))

    for bi in range(nb):
        u = u_ref[bi].astype(F32)
        prev = jnp.where(first, 0.0, pltpu.roll(u, 1, 0))
        nxt = jnp.where(last, 0.0, pltpu.roll(u, L - 1, 0))
        u = prev * cw_ref[0:1, :] + u * cw_ref[1:2, :] + nxt * cw_ref[2:3, :]
        v, x1, x2 = u[:, :HY_W], u[:, HY_W:2 * HY_W], u[:, 2 * HY_W:]
        z = x1 * (long_conv(v, 0) + d_ref[0:1, :] * v)
        z = x2 * (long_conv(z, 1) + d_ref[1:2, :] * z)
        o_ref[bi] = z.astype(o_ref.dtype)


def hyena_mix(u, seq0, B, L, conv_w, d_skip, fwd, inv, ka, kb, nb):
    u3 = u.reshape(-1, L, EVEN_MAIN)
    full = lambda a: _const_spec(a, 1)
    out = pl.pallas_call(
        _hyena_kernel,
        grid=(B // nb,),
        in_specs=[pl.BlockSpec((nb, L, 3 * HY_W), lambda b: (b + seq0 // nb, 0, 0)),
                  full(conv_w), full(d_skip), full(fwd), full(inv), full(ka), full(kb)],
        out_specs=pl.BlockSpec((nb, L, HY_W), lambda b: (b, 0, 0)),
        out_shape=jax.ShapeDtypeStruct((B, L, HY_W), BF16),
        compiler_params=_cparams("parallel"),
        name=f"hyena_{L}",
    )(u3, conv_w, d_skip, fwd, inv, ka, kb)
    return out.reshape(B * L, HY_W)


def _mlstm_chunk(q, ks, vaug, li_c, b_c, li_r, b_r, caug, m, rev):
    T = q.shape[0]
    ri = lax.broadcasted_iota(jnp.int32, (T, T), 0)
    ci = lax.broadcasted_iota(jnp.int32, (T, T), 1)
    mask = (ci >= ri) if rev else (ci <= ri)
    b_end = b_c[0:1, :] if rev else b_c[T - 1:T, :]
    row = li_r - b_r
    dm = jnp.where(mask, b_c + row, NEG)
    inter = b_c + m
    m_t = jnp.maximum(inter, jnp.max(dm, axis=1, keepdims=True))
    w_intra = jnp.exp(dm - m_t)
    w_inter = jnp.exp(inter - m_t)
    s = lax.dot_general(q, ks, (((1,), (1,)), ((), ())), preferred_element_type=F32) * w_intra
    acc = _bdot(s, vaug) + w_inter * _bdot(q, caug)
    h = acc[:, :ML_HD] / jnp.maximum(jnp.abs(acc[:, ML_HD:]), jnp.exp(-m_t))
    m_new = jnp.maximum(b_end + m, jnp.max(b_end + row, axis=1, keepdims=True))
    decay = jnp.exp(b_end + m - m_new)
    kw = ks.astype(F32) * jnp.exp(b_end - b_c + li_c - m_new)
    caug_new = decay * caug + lax.dot_general(kw.astype(BF16), vaug, (((0,), (0,)), ((), ())),
                                              preferred_element_type=F32)
    return h, caug_new, m_new


def _mlstm_kernel(*refs, has_state, want_state):
    q_ref, k_ref, v_ref, o_ref, gc_ref, gr_ref, ng_ref = refs[:7]
    refs = refs[7:]
    if has_state:
        c0_ref, n0_ref, m0_ref = refs[:3]
        refs = refs[3:]
    y_ref = refs[0]
    if want_state:
        c_out, n_out, m_out = refs[1:4]
    L = q_ref.shape[0]
    T = min(ML_CHUNK, L)
    nc = L // T
    scale = 1.0 / math.sqrt(ML_HD)
    hs = [None] * nc
    for dr in range(2):
        if has_state:
            caug = jnp.concatenate([c0_ref[dr], n0_ref[dr]], axis=1)
            m = m0_ref[dr:dr + 1, 0:1]
        else:
            caug, m = jnp.zeros((ML_HD, 2 * ML_HD), F32), jnp.zeros((1, 1), F32)
        order = range(nc) if dr == 0 else range(nc - 1, -1, -1)
        for j in order:
            sl = slice(j * T, (j + 1) * T)
            q = q_ref[sl, :]
            ks = (k_ref[sl, :].astype(F32) * scale).astype(BF16)
            vaug = jnp.concatenate([v_ref[sl, :], jnp.ones((T, ML_HD), BF16)], axis=1)
            h, caug, m = _mlstm_chunk(q, ks, vaug, gc_ref[sl, 2 * dr:2 * dr + 1], gc_ref[sl, 2 * dr + 1:2 * dr + 2],
                                      gr_ref[2 * dr:2 * dr + 1, sl], gr_ref[2 * dr + 1:2 * dr + 2, sl],
                                      caug, m, rev=(dr == 1))
            hs[j] = h if hs[j] is None else hs[j] + h
        if want_state:
            c_out[dr] = caug[:, :ML_HD]
            n_out[dr:dr + 1, :] = caug[:, ML_HD:].T[0:1, :]
            m_out[dr:dr + 1, :] = jnp.broadcast_to(m, (1, ML_HD))
    for j in range(nc):
        sl = slice(j * T, (j + 1) * T)
        h = hs[j]
        y = h * lax.rsqrt(jnp.mean(h * h, axis=-1, keepdims=True) + EPS) * ng_ref[...]
        y_ref[sl, :] = (y * jax.nn.sigmoid(o_ref[sl, :].astype(F32))).astype(y_ref.dtype)


def mlstm_mix(u, gates, seq0, B, L, norm_g, state=None, want_state=False):
    u3 = u.reshape(-1, L, EVEN_MAIN)
    g5 = gates.reshape(-1, L, 2, 2, ML_HEADS)[seq0:seq0 + B]
    gc = g5.transpose(0, 4, 1, 2, 3).reshape(B, ML_HEADS, L, 4)
    gr = g5.transpose(0, 4, 2, 3, 1).reshape(B, ML_HEADS, 4, L)
    col = lambda i: pl.BlockSpec((None, L, ML_HD),
                                 lambda b, h: (b + seq0, 0, (3 * HY_W + i * 512) // ML_HD + h))
    in_specs = [col(0), col(1), col(2), col(3),
                pl.BlockSpec((None, None, L, 4), lambda b, h: (b, h, 0, 0)),
                pl.BlockSpec((None, None, 4, L), lambda b, h: (b, h, 0, 0)),
                pl.BlockSpec((None, 1, ML_HD), lambda b, h: (h, 0, 0))]
    args = [u3, u3, u3, u3, gc, gr, norm_g.reshape(ML_HEADS, 1, ML_HD)]
    if state is not None:
        C0, n0, m0 = state
        cspec = pl.BlockSpec((None, 2, None, ML_HD, ML_HD), lambda b, h: (b, 0, h, 0, 0))
        in_specs += [cspec, cspec, pl.BlockSpec((None, None, 2, ML_HD), lambda b, h: (b, h, 0, 0))]
        args += [C0, jnp.broadcast_to(n0[..., None], C0.shape),
                 jnp.broadcast_to(m0.transpose(0, 2, 1)[..., None], (B, ML_HEADS, 2, ML_HD))]
    out_specs = [pl.BlockSpec((None, L, ML_HD), lambda b, h: (b, 0, h))]
    out_shape = [jax.ShapeDtypeStruct((B, L, ML_HEADS * ML_HD), BF16)]
    if want_state:
        out_specs += [pl.BlockSpec((None, 2, None, ML_HD, ML_HD), lambda b, h: (b, 0, h, 0, 0)),
                      pl.BlockSpec((None, None, 2, ML_HD), lambda b, h: (b, h, 0, 0)),
                      pl.BlockSpec((None, None, 2, ML_HD), lambda b, h: (b, h, 0, 0))]
        out_shape += [jax.ShapeDtypeStruct((B, 2, ML_HEADS, ML_HD, ML_HD), F32),
                      jax.ShapeDtypeStruct((B, ML_HEADS, 2, ML_HD), F32),
                      jax.ShapeDtypeStruct((B, ML_HEADS, 2, ML_HD), F32)]
    outs = pl.pallas_call(
        functools.partial(_mlstm_kernel, has_state=state is not None, want_state=want_state),
        grid=(B, ML_HEADS),
        in_specs=in_specs, out_specs=out_specs, out_shape=out_shape,
        compiler_params=_cparams("parallel", "parallel"),
        name=f"mlstm_{L}",
    )(*args)
    y = outs[0].reshape(B * L, ML_HEADS * ML_HD)
    if not want_state:
        return y
    _, C, n, m = outs
    return y, C, n.transpose(0, 2, 1, 3), m[..., 0].transpose(0, 2, 1)


def _proj_res_kernel(*refs, n_in, n_x, n_p):
    a_refs = refs[:2 * n_in]
    w_ref = refs[2 * n_in]
    x_refs = refs[2 * n_in + 1:2 * n_in + 1 + n_x]
    gate_ref, o_ref = refs[2 * n_in + 1 + n_x:]
    is_prompt = pl.program_id(0) < n_p
    k0 = 0
    acc = None
    for j in range(n_in):
        a = _pick(is_prompt, a_refs[2 * j], a_refs[2 * j + 1])
        kw = a.shape[1]
        part = _bdot(a, w_ref[k0:k0 + kw, :])
        acc = part if acc is None else acc + part
        k0 += kw
    x = _pick(is_prompt, *x_refs) if n_x == 2 else x_refs[0][...]
    o_ref[...] = x + gate_ref[...] * acc


def proj_residual(acts, w, x, mods, gate_idx, tm=1024):
    xs = tuple(x) if isinstance(x, (tuple, list)) else (x,)
    T = sum(a.shape[0] for a in acts[0])
    n_p = acts[0][0].shape[0] // tm
    in_specs = []
    for pair in acts:
        in_specs += _part_specs((tm, pair[0].shape[1]), n_p)
    in_specs.append(pl.BlockSpec(w.shape, lambda i: (0, 0)))
    in_specs += _part_specs((tm, D), n_p) if len(xs) == 2 else [pl.BlockSpec((tm, D), lambda i: (i, 0))]
    in_specs.append(_mod_spec(gate_idx, tm))
    return pl.pallas_call(
        functools.partial(_proj_res_kernel, n_in=len(acts), n_x=len(xs), n_p=n_p),
        grid=(T // tm,),
        in_specs=in_specs,
        out_specs=pl.BlockSpec((tm, D), lambda i: (i, 0)),
        out_shape=jax.ShapeDtypeStruct((T, D), F32),
        compiler_params=_cparams("parallel"),
        name="proj_residual",
    )(*[a for pair in acts for a in pair], w, *xs, mods)


SWIGLU_ROWS = 512


def _swiglu_accumulate(h_scr, acc_scr, w1_ref, w3_ref, w2_ref, rows):
    w1, w3, w2 = w1_ref[...].astype(BF16), w3_ref[...].astype(BF16), w2_ref[...].astype(BF16)
    groups = [slice(r, r + SWIGLU_ROWS) for r in range(0, rows, SWIGLU_ROWS)]
    ups = []
    for sl in groups:
        h = h_scr[sl, :]
        ups.append((jnp.dot(h, w1, preferred_element_type=F32), jnp.dot(h, w3, preferred_element_type=F32)))
    for sl, (a, b) in zip(groups, ups):
        mid = (_silu(a) * b).astype(BF16)
        acc_scr[sl, :] += jnp.dot(mid, w2, preferred_element_type=F32)


def _ffn_kernel(x_ref, g_ref, sh_ref, sc_ref, gate_ref, w1_ref, w3_ref, w2_ref, o_ref, h_scr, acc_scr):
    c = pl.program_id(1)

    @pl.when(c == 0)
    def _():
        h_scr[...] = _norm_mod(x_ref[...], g_ref[...], sh_ref[...], sc_ref[...]).astype(BF16)
        acc_scr[...] = jnp.zeros_like(acc_scr)

    _swiglu_accumulate(h_scr, acc_scr, w1_ref, w3_ref, w2_ref, h_scr.shape[0])

    @pl.when(c == pl.num_programs(1) - 1)
    def _():
        o_ref[...] = x_ref[...] + gate_ref[...] * acc_scr[...]


def ffn_residual(x, g, mods, w1, w3, w2, tm=1024, tf=256):
    T = x.shape[0]
    return pl.pallas_call(
        _ffn_kernel,
        grid=(T // tm, D_FF // tf),
        in_specs=[pl.BlockSpec((tm, D), lambda i, c: (i, 0)),
                  pl.BlockSpec((1, D), lambda i, c: (0, 0)),
                  _mod_spec(3, tm), _mod_spec(4, tm), _mod_spec(5, tm),
                  pl.BlockSpec((D, tf), lambda i, c: (0, c)),
                  pl.BlockSpec((D, tf), lambda i, c: (0, c)),
                  pl.BlockSpec((tf, D), lambda i, c: (c, 0))],
        out_specs=pl.BlockSpec((tm, D), lambda i, c: (i, 0)),
        out_shape=jax.ShapeDtypeStruct((T, D), F32),
        scratch_shapes=[pltpu.VMEM((tm, D), BF16), pltpu.VMEM((tm, D), F32)],
        compiler_params=_cparams("parallel", "arbitrary"),
        name="ffn",
    )(x, g, mods, mods, mods, w1, w3, w2)


def _qkv_kernel(x_ref, g_ref, sh_ref, sc_ref, w_ref, q_ref, kv_ref, h_scr):
    j = pl.program_id(1)

    @pl.when(j == 0)
    def _():
        h_scr[...] = _norm_mod(x_ref[...], g_ref[...], sh_ref[...], sc_ref[...]).astype(BF16)

    @pl.when(j < 2)
    def _():
        q_ref[...] = _bdot(h_scr[...], w_ref[...]).astype(q_ref.dtype)

    @pl.when(j == 2)
    def _():
        kv_ref[...] = _bdot(h_scr[...], w_ref[...])


def qkv_proj(x, g, mods, w_qkv, tm=1024):
    T = x.shape[0]
    tn = 512
    return pl.pallas_call(
        _qkv_kernel,
        grid=(T // tm, 3),
        in_specs=[pl.BlockSpec((tm, D), lambda i, j: (i, 0)),
                  pl.BlockSpec((1, D), lambda i, j: (0, 0)),
                  _mod_spec(0, tm), _mod_spec(1, tm),
                  pl.BlockSpec((D, tn), lambda i, j: (0, j))],
        out_specs=[pl.BlockSpec((tm, tn), lambda i, j: (i, jnp.minimum(j, 1))),
                   pl.BlockSpec((tm, tn), lambda i, j: (i, 0))],
        out_shape=[jax.ShapeDtypeStruct((T, ATT_HEADS * ATT_HD), BF16),
                   jax.ShapeDtypeStruct((T, 2 * ATT_KV * ATT_HD), F32)],
        scratch_shapes=[pltpu.VMEM((tm, D), BF16)],
        compiler_params=_cparams("parallel", "arbitrary"),
        name="qkv_proj",
    )(x, g, mods, mods, w_qkv)


def _rope_tables(L):
    half = ATT_HD // 2
    pos_r = (np.arange(L) // GRID_W).astype(np.float32)
    pos_c = (np.arange(L) % GRID_W).astype(np.float32)
    inv = (ROPE_BASE ** (-np.arange(0, half, 2, dtype=np.float32) / half)).astype(np.float32)
    cos = np.zeros((L, ATT_HD), np.float64)
    sin = np.zeros((L, ATT_HD), np.float64)
    for base, pos in ((0, pos_r), (half, pos_c)):
        ang = (pos[:, None] * inv[None, :]).astype(np.float32).astype(np.float64)
        cos[:, base:base + half] = np.concatenate([np.cos(ang), np.cos(ang)], axis=1)
        sin[:, base:base + half] = np.concatenate([-np.sin(ang), np.sin(ang)], axis=1)
    return (np.tile(cos, (1, 4)).astype(np.float32), np.tile(sin, (1, 4)).astype(np.float32))


def _seg_rms(x):
    w = x.shape[1]
    ri = lax.broadcasted_iota(jnp.int32, (w, w), 0) // ATT_HD
    ci = lax.broadcasted_iota(jnp.int32, (w, w), 1) // ATT_HD
    ss = _bdot(x * x, (ri == ci).astype(F32))
    return x * lax.rsqrt(ss * (1.0 / ATT_HD) + EPS)


LOG2E = 1.4426950408889634


def _exp2_bf16(x):
    return jnp.exp2(x.astype(BF16))


def _both_halves(tile, low):
    lane = lax.broadcasted_iota(jnp.int32, tile.shape, 1)
    other = pltpu.roll(tile, ATT_HD, 1)
    return jnp.where((lane < ATT_HD) == low, tile, other)


def _swap16(x):
    w = x.shape[1]
    lane = lax.broadcasted_iota(jnp.int32, x.shape, 1)
    return jnp.where(lane % 32 < 16, pltpu.roll(x, w - 16, 1), pltpu.roll(x, 16, 1))


def _attn_kernel(*refs, latent, tq):
    if latent:
        (q_ref, kv_ref, ck_ref, cv_ref, qg_ref, kg_ref, sink_ref, cosq_ref, sinq_ref, cosk_ref, sink_t_ref,
         o_ref, kk_scr, vt_scr, ckk_scr, cvt_scr) = refs
    else:
        q_ref, kv_ref, qg_ref, kg_ref, sink_ref, o_ref, ko_ref, vo_ref, kk_scr, vt_scr = refs
    L = kv_ref.shape[0]
    gw = ATT_KV * ATT_HD
    pw = 2 * ATT_HD
    qb = pl.program_id(1)

    vrows = vt_scr.shape[2]
    nblk = L // pw

    def vt_aug(tile, low):
        vt = tile.T[0:ATT_HD, :] if low else tile.T[ATT_HD:, :]
        return jnp.concatenate([vt, jnp.ones((vrows - ATT_HD, tile.shape[0]), F32)], axis=0).astype(BF16)

    @pl.when(qb == 0)
    def _():
        kn = _seg_rms(kv_ref[:, :gw]) * kg_ref[...]
        v = kv_ref[:, gw:]
        if latent:
            kn = kn * cosk_ref[...] + _swap16(kn) * sink_t_ref[...]
        else:
            for c in range(ATT_KV):
                ko_ref[c] = kn[:, c * ATT_HD:(c + 1) * ATT_HD]
                vo_ref[c] = v[:, c * ATT_HD:(c + 1) * ATT_HD]
        for c in range(ATT_KV):
            tile, low = slice((c // 2) * pw, (c // 2 + 1) * pw), c % 2 == 0
            kk_scr[c] = _both_halves(kn[:, tile], low).astype(BF16)
            for j in range(nblk):
                vt_scr[c, j] = vt_aug(v[j * pw:(j + 1) * pw, tile], low)
            if latent:
                ck, cv = ck_ref[c], cv_ref[c]
                ckk_scr[c] = jnp.concatenate([ck, ck], axis=1).astype(BF16)
                cvt_scr[c] = vt_aug(jnp.concatenate([cv, cv], axis=1), True)

    if latent:
        span = tq + 2 * WINDOW
        start = pl.multiple_of(jnp.clip(qb * tq - WINDOW, 0, L - span), WINDOW)
        blk0 = start // pw
        s_pos = start + lax.broadcasted_iota(jnp.int32, (span, tq), 0)
        t_pos = qb * tq + lax.broadcasted_iota(jnp.int32, (span, tq), 1)
        win_bias = jnp.where(jnp.abs(t_pos - s_pos) <= WINDOW, 0.0, NEG)
    else:
        span, blk0 = L, 0

    nt = (((1,), (1,)), ((), ()))
    low_q = lax.broadcasted_iota(jnp.int32, (tq, pw), 1) < ATT_HD
    def group_scores(c):
        qc = _seg_rms(q_ref[:, c * gw:(c + 1) * gw].astype(F32)) * qg_ref[...]
        if latent:
            qc = qc * cosq_ref[...] + _swap16(qc) * sinq_ref[...]
            kw = kk_scr[c, pl.ds(start, span), :]
        else:
            kw = kk_scr[c]
        qc = qc * (LOG2E / math.sqrt(ATT_HD))
        scores = []
        for g in range(ATT_KV):
            qt = qc[:, (g // 2) * pw:(g // 2 + 1) * pw]
            qm = jnp.where(low_q if g % 2 == 0 else ~low_q, qt, 0.0).astype(BF16)
            lw = lax.dot_general(kw, qm, nt, preferred_element_type=F32)
            lc = lax.dot_general(ckk_scr[c], qm, nt, preferred_element_type=F32) if latent else None
            scores.append((lw, lc))
        return scores

    def group_outputs(c, scores):
        vw = jnp.concatenate([vt_scr[c, blk0 + j] for j in range(span // pw)], axis=1)
        outs = []
        for g in range(ATT_KV):
            head = c * ATT_KV + g
            sink = sink_ref[:, head:head + 1] * LOG2E
            lw, lc = scores[g]
            if latent:
                lw = lw + win_bias
                mx = jnp.maximum(jnp.maximum(jnp.max(lw, axis=0, keepdims=True),
                                             jnp.max(lc, axis=0, keepdims=True)), sink)
                r = jnp.dot(vw, _exp2_bf16(lw - mx), preferred_element_type=F32) + jnp.dot(
                    cvt_scr[c], _exp2_bf16(lc - mx), preferred_element_type=F32)
            else:
                mx = jnp.maximum(jnp.max(lw, axis=0, keepdims=True), sink)
                r = jnp.dot(vw, _exp2_bf16(lw - mx), preferred_element_type=F32)
            den = r[ATT_HD:ATT_HD + 1, :] + jnp.exp2(sink - mx)
            outs.append(r[0:ATT_HD, :] / den)
        for t in range(2):
            o_ref[:, c * gw + t * pw:c * gw + (t + 1) * pw] = (
                jnp.concatenate(outs[2 * t:2 * t + 2], axis=0).T.astype(o_ref.dtype))

    scores = group_scores(0)
    for c in range(ATT_KV):
        nxt = group_scores(c + 1) if c + 1 < ATT_KV else None
        group_outputs(c, scores)
        scores = nxt


def attention(q, kv, row0, q_g, k_g, sink, B, L, cache=None, tq=256):
    latent = cache is not None
    gw = ATT_KV * ATT_HD
    qg = jnp.tile(q_g, ATT_KV)[None]
    kg = jnp.tile(k_g, ATT_KV)[None]
    nq = L // tq
    const = lambda a: pl.BlockSpec(a.shape, lambda b, i: (0,) * a.ndim)
    in_specs = [pl.BlockSpec((tq, ATT_HEADS * ATT_HD), lambda b, i: (row0 // tq + b * nq + i, 0)),
                pl.BlockSpec((L, 2 * gw), lambda b, i: (row0 // L + b, 0))]
    args = [q, kv]
    vrows = ATT_HD + 16
    scratch = [pltpu.VMEM((ATT_KV, L, 2 * ATT_HD), BF16),
               pltpu.VMEM((ATT_KV, L // (2 * ATT_HD), vrows, 2 * ATT_HD), BF16)]
    out_specs = [pl.BlockSpec((tq, ATT_HEADS * ATT_HD), lambda b, i: (b * nq + i, 0))]
    out_shape = [jax.ShapeDtypeStruct((B * L, ATT_HEADS * ATT_HD), BF16)]
    if latent:
        ck, cv = cache
        P = ck.shape[2]
        cos, sin = (jnp.asarray(t) for t in _rope_tables(L))
        in_specs += [pl.BlockSpec((None, ATT_KV, P, ATT_HD), lambda b, i: (b, 0, 0, 0))] * 2
        args += [ck, cv]
        in_specs += [const(qg), const(kg), pl.BlockSpec((1, ATT_HEADS), lambda b, i: (0, 0)),
                     pl.BlockSpec((tq, gw), lambda b, i: (i, 0)), pl.BlockSpec((tq, gw), lambda b, i: (i, 0)),
                     const(cos), const(sin)]
        args += [qg, kg, sink[None], cos, sin, cos, sin]
        scratch += [pltpu.VMEM((ATT_KV, P, 2 * ATT_HD), BF16), pltpu.VMEM((ATT_KV, vrows, P), BF16)]
    else:
        in_specs += [const(qg), const(kg), pl.BlockSpec((1, ATT_HEADS), lambda b, i: (0, 0))]
        args += [qg, kg, sink[None]]
        cache_spec = pl.BlockSpec((None, ATT_KV, L, ATT_HD), lambda b, i: (b, 0, 0, 0))
        out_specs += [cache_spec, cache_spec]
        out_shape += [jax.ShapeDtypeStruct((B, ATT_KV, L, ATT_HD), F32)] * 2
    outs = pl.pallas_call(
        functools.partial(_attn_kernel, latent=latent, tq=tq),
        grid=(B, nq),
        in_specs=in_specs, out_specs=out_specs, out_shape=out_shape,
        scratch_shapes=scratch,
        compiler_params=_cparams("parallel", "arbitrary"),
        name="attn_latent" if latent else "attn_context",
    )(*args)
    return outs[0] if latent else outs


MOE_TM = 1024
MOE_MAX_TILES = 2 * 16384 // MOE_TM + N_EXPERTS


def _router_kernel(x_ref, g_ref, sh_ref, sc_ref, wr_ref, br_ref, tri_ref, eid_ref, rank_ref, wts_ref, cnt_ref):
    @pl.when(pl.program_id(0) == 0)
    def _():
        cnt_ref[...] = jnp.zeros_like(cnt_ref)

    h = _norm_mod(x_ref[...], g_ref[...], sh_ref[...], sc_ref[...])
    lg = lax.dot_general(wr_ref[...], h, (((1,), (1,)), ((), ())), precision=HIGHEST,
                         preferred_element_type=F32) + br_ref[...]
    row = lax.broadcasted_iota(jnp.int32, lg.shape, 0)
    m1 = jnp.max(lg, axis=0, keepdims=True)
    i1 = jnp.min(jnp.where(lg == m1, row, N_EXPERTS), axis=0, keepdims=True)
    l2 = jnp.where(row == i1, -jnp.inf, lg)
    m2 = jnp.max(l2, axis=0, keepdims=True)
    i2 = jnp.min(jnp.where(l2 == m2, row, N_EXPERTS), axis=0, keepdims=True)
    e2 = jnp.exp(m2 - m1)
    w1 = 1.0 / (1.0 + e2)
    eid_ref[...] = jnp.concatenate([i1, i2], axis=0)
    wts_ref[...] = jnp.concatenate([w1, e2 * w1], axis=0)
    oh1 = (row == i1).astype(F32)
    oh2 = (row == i2).astype(F32)
    cs1 = _bdot(oh1, tri_ref[...])
    cs2 = _bdot(oh2, tri_ref[...])
    tot1 = jnp.sum(oh1, axis=1, keepdims=True)
    cnt = cnt_ref[:, 0:1]
    r1 = jnp.sum(oh1 * (cnt + cs1), axis=0, keepdims=True)
    r2 = jnp.sum(oh2 * (cnt + tot1 + cs2), axis=0, keepdims=True)
    rank_ref[...] = jnp.concatenate([r1, r2], axis=0).astype(jnp.int32)
    cnt_ref[...] = cnt_ref[...] + tot1 + jnp.sum(oh2, axis=1, keepdims=True)


def moe_router(x, g, mods, w_router, b_router, tm=1024):
    T = x.shape[0]
    tri = jnp.asarray(np.triu(np.ones((tm, tm), np.float32), k=1)).astype(BF16)
    tok2 = lambda dt: jax.ShapeDtypeStruct((2, T), dt)
    return pl.pallas_call(
        _router_kernel,
        grid=(T // tm,),
        in_specs=[pl.BlockSpec((tm, D), lambda i: (i, 0)),
                  pl.BlockSpec((1, D), lambda i: (0, 0)),
                  _mod_spec(3, tm), _mod_spec(4, tm),
                  pl.BlockSpec((N_EXPERTS, D), lambda i: (0, 0)),
                  pl.BlockSpec((N_EXPERTS, 1), lambda i: (0, 0)),
                  _const_spec(tri, 1)],
        out_specs=[pl.BlockSpec((2, tm), lambda i: (0, i)),
                   pl.BlockSpec((2, tm), lambda i: (0, i)),
                   pl.BlockSpec((2, tm), lambda i: (0, i)),
                   pl.BlockSpec((N_EXPERTS, 128), lambda i: (0, 0))],
        out_shape=[tok2(jnp.int32), tok2(jnp.int32), tok2(F32),
                   jax.ShapeDtypeStruct((N_EXPERTS, 128), F32)],
        compiler_params=_cparams("arbitrary"),
        name="moe_router",
    )(x, g, mods, mods, w_router.T, b_router[:, None], tri)


def moe_layout(eid, rank, counts):
    cnt = counts[:, 0].astype(jnp.int32)
    tiles = (cnt + MOE_TM - 1) // MOE_TM
    tile_end = jnp.cumsum(tiles)
    start = (tile_end - tiles) * MOE_TM
    pos = rank + jnp.sum(jnp.where(eid[..., None] == jnp.arange(N_EXPERTS), start, 0), axis=-1)
    n_tiles = tile_end[-1]
    t = jnp.arange(MOE_MAX_TILES, dtype=jnp.int32)
    tile_e = jnp.sum(t[:, None] >= tile_end[None, :], axis=1).astype(jnp.int32)
    last_e = jnp.sum((n_tiles - 1) >= tile_end).astype(jnp.int32)
    tile_e = jnp.where(t < n_tiles, tile_e, last_e)
    first = jnp.sum(jnp.where(tile_e[:, None] == jnp.arange(N_EXPERTS), (tile_end - tiles)[None, :], 0), axis=1)
    e_cnt = jnp.sum(jnp.where(tile_e[:, None] == jnp.arange(N_EXPERTS), cnt[None, :], 0), axis=1)
    tile_rows = jnp.where(t < n_tiles, jnp.clip(e_cnt - (t - first) * MOE_TM, 0, MOE_TM), 0).astype(jnp.int32)
    return pos.astype(jnp.int32), tile_e, n_tiles.astype(jnp.int32).reshape(1), tile_rows


def _row_copy(src, i, dst, j, sem):
    return pltpu.make_async_copy(src.at[pl.ds(i, 1), :], dst.at[pl.ds(j, 1), :], sem)


def _dispatch_kernel(pos_ref, x_ref, g_ref, sh_ref, sc_ref, xs_in_ref, xs_ref, h_scr, sem):
    del xs_in_ref
    tm = h_scr.shape[0]
    h_scr[...] = _norm_mod(x_ref[...], g_ref[...], sh_ref[...], sc_ref[...])

    def issue(r, carry):
        for s in range(2):
            _row_copy(h_scr, r, xs_ref, pos_ref[s, r], sem.at[s]).start()
        return carry

    lax.fori_loop(0, tm, issue, 0, unroll=8)
    for s in range(2):
        pltpu.make_async_copy(h_scr, xs_ref.at[pl.ds(0, tm), :], sem.at[s]).wait()


def moe_dispatch(x, g, mods, pos, tm=1024):
    T = x.shape[0]
    n_rows = MOE_MAX_TILES * MOE_TM
    pos3 = pos.reshape(2, T // tm, tm).transpose(1, 0, 2)
    return pl.pallas_call(
        _dispatch_kernel,
        grid=(T // tm,),
        in_specs=[pl.BlockSpec((None, 2, tm), lambda i: (i, 0, 0), memory_space=pltpu.SMEM),
                  pl.BlockSpec((tm, D), lambda i: (i, 0)),
                  pl.BlockSpec((1, D), lambda i: (0, 0)),
                  _mod_spec(3, tm), _mod_spec(4, tm),
                  pl.BlockSpec(memory_space=pl.ANY)],
        out_specs=pl.BlockSpec(memory_space=pl.ANY),
        out_shape=jax.ShapeDtypeStruct((n_rows, D), F32),
        scratch_shapes=[pltpu.VMEM((tm, D), F32), pltpu.SemaphoreType.DMA((2,))],
        input_output_aliases={5: 0},
        compiler_params=_cparams("arbitrary"),
        name="moe_dispatch",
    )(pos3, x, g, mods, mods, jnp.zeros((n_rows, D), F32))


def _moe_group_kernel(te_ref, nt_ref, tr_ref, x_ref, w1_ref, w3_ref, w2_ref, o_ref, h_scr, acc_scr):
    t, c = pl.program_id(0), pl.program_id(1)
    rows = tr_ref[t]
    half = MOE_TM // 2

    @pl.when(t < nt_ref[0])
    def _():
        @pl.when(c == 0)
        def _():
            h_scr[...] = x_ref[...].astype(BF16)
            acc_scr[...] = jnp.zeros_like(acc_scr)

        @pl.when(rows > half)
        def _():
            _swiglu_accumulate(h_scr, acc_scr, w1_ref, w3_ref, w2_ref, MOE_TM)

        @pl.when(rows <= half)
        def _():
            _swiglu_accumulate(h_scr, acc_scr, w1_ref, w3_ref, w2_ref, half)

        @pl.when(c == pl.num_programs(1) - 1)
        def _():
            o_ref[...] = acc_scr[...]

    @pl.when((t >= nt_ref[0]) & (c == pl.num_programs(1) - 1))
    def _():
        o_ref[...] = jnp.zeros_like(o_ref)


def moe_grouped_swiglu(xs, tile_e, n_tiles, tile_rows, w1, w3, w2, tf=256):
    nc = D_FF // tf
    live = lambda t, nt: t < nt[0]
    row = lambda t, c, te, nt, tr: (jnp.where(live(t, nt), t, nt[0] - 1), 0)
    wcol = lambda t, c, te, nt, tr: (te[t], 0, jnp.where(live(t, nt), c, nc - 1))
    wrow = lambda t, c, te, nt, tr: (te[t], jnp.where(live(t, nt), c, nc - 1), 0)
    return pl.pallas_call(
        _moe_group_kernel,
        grid_spec=pltpu.PrefetchScalarGridSpec(
            num_scalar_prefetch=3,
            grid=(MOE_MAX_TILES, nc),
            in_specs=[pl.BlockSpec((MOE_TM, D), row),
                      pl.BlockSpec((None, D, tf), wcol),
                      pl.BlockSpec((None, D, tf), wcol),
                      pl.BlockSpec((None, tf, D), wrow)],
            out_specs=pl.BlockSpec((MOE_TM, D), lambda t, c, te, nt, tr: (t, 0)),
            scratch_shapes=[pltpu.VMEM((MOE_TM, D), BF16), pltpu.VMEM((MOE_TM, D), F32)]),
        out_shape=jax.ShapeDtypeStruct(xs.shape, F32),
        compiler_params=_cparams("arbitrary", "arbitrary"),
        name="moe_grouped",
    )(tile_e, n_tiles, tile_rows, xs, w1, w3, w2)


def _combine_kernel(pos_ref, x_ref, gate_ref, wt_ref, ys_ref, op_ref, os_ref, a_scr, b_scr, sem, *, n_p):
    tm = a_scr.shape[0]
    bufs = (a_scr, b_scr)

    def issue(r, carry):
        for s in range(2):
            _row_copy(ys_ref, pos_ref[s, r], bufs[s], r, sem.at[s]).start()
        return carry

    lax.fori_loop(0, tm, issue, 0, unroll=8)
    for s in range(2):
        pltpu.make_async_copy(ys_ref.at[pl.ds(0, tm), :], bufs[s], sem.at[s]).wait()
    moe = wt_ref[:, 0:1] * a_scr[...] + wt_ref[:, 1:2] * b_scr[...]
    out = x_ref[...] + gate_ref[...] * moe

    @pl.when(pl.program_id(0) < n_p)
    def _():
        op_ref[...] = out

    @pl.when(pl.program_id(0) >= n_p)
    def _():
        os_ref[...] = out


def moe_combine(x, mods, pos, wts, ys, t_prompt, tm=1024):
    T = x.shape[0]
    n_p = t_prompt // tm
    pos3 = pos.reshape(2, T // tm, tm).transpose(1, 0, 2)
    return pl.pallas_call(
        functools.partial(_combine_kernel, n_p=n_p),
        grid=(T // tm,),
        in_specs=[pl.BlockSpec((None, 2, tm), lambda i: (i, 0, 0), memory_space=pltpu.SMEM),
                  pl.BlockSpec((tm, D), lambda i: (i, 0)),
                  _mod_spec(5, tm),
                  pl.BlockSpec((tm, 2), lambda i: (i, 0)),
                  pl.BlockSpec(memory_space=pl.ANY)],
        out_specs=_part_specs((tm, D), n_p),
        out_shape=[jax.ShapeDtypeStruct((t_prompt, D), F32), jax.ShapeDtypeStruct((T - t_prompt, D), F32)],
        scratch_shapes=[pltpu.VMEM((tm, D), F32), pltpu.VMEM((tm, D), F32), pltpu.SemaphoreType.DMA((2,))],
        compiler_params=_cparams("arbitrary"),
        name="moe_combine",
    )(pos3, x, mods, wts.T, ys)


def kernel(x_prompt, x_sample, state_C, state_n, state_m, cache_k, cache_v, c, c_ctx, norm1_g, norm2_g, w_ada, b_ada, ev_w_in, ev_conv, hy_w1, hy_b1, hy_w2, hy_b2, hy_w3, hy_freq, hy_d, ml_b_gate, ml_norm_g, ev_w_out, ff_w1, ff_w3, ff_w2, at_w_qkv, at_q_g, at_k_g, at_sink, at_w_out, moe_w_router, moe_b_router, moe_w1, moe_w3, moe_w2):
    BP, LP, _ = x_prompt.shape
    BS, LS, _ = x_sample.shape
    TP = BP * LP
    assert TP % GROUP == 0 and TP // GROUP == N_PROMPT_GROUPS and LS == GROUP and BS == 8

    xp, xs = x_prompt.reshape(TP, D), x_sample.reshape(BS * LS, D)
    cond = jnp.concatenate([c_ctx[None], c, jnp.zeros((16 - 1 - BS, D), F32)], axis=0)
    mods = adaln_table(cond, w_ada, b_ada)

    u, gates = even_in_proj(xp, xs, norm1_g[0:1], mods[0], ev_w_in[0], ml_b_gate[0].reshape(1, N_GATES))
    hy = []
    for seq0, B, L, nb in ((0, BP, LP, 4), (TP // LS, BS, LS, 1)):
        fwd, inv = (jnp.asarray(t).astype(BF16) for t in _dft_tables(L))
        ka, kb = hyena_filter_spectra(L, hy_w1[0], hy_b1[0], hy_w2[0], hy_b2[0], hy_w3[0], hy_freq[0], fwd)
        hy.append(hyena_mix(u, seq0, B, L, ev_conv[0], hy_d[0], fwd, inv, ka, kb, nb))
    ml_p, new_C, new_n, new_m = mlstm_mix(u, gates, 0, BP, LP, ml_norm_g[0], want_state=True)
    ml_s = mlstm_mix(u, gates, TP // LS, BS, LS, ml_norm_g[0],
                     state=(state_C[:, 0], state_n[:, 0], state_m[:, 0]))
    x = proj_residual([hy, (ml_p, ml_s)], ev_w_out[0], (xp, xs), mods[0], 2)
    x = ffn_residual(x, norm2_g[0:1], mods[0], ff_w1[0], ff_w3[0], ff_w2[0])

    q, kv = qkv_proj(x, norm1_g[1:2], mods[1], at_w_qkv[0])
    o_p, new_k, new_v = attention(q, kv, 0, at_q_g[0], at_k_g[0], at_sink[0], BP, LP)
    o_s = attention(q, kv, TP, at_q_g[0], at_k_g[0], at_sink[0], BS, LS, cache=(cache_k[:, 0], cache_v[:, 0]))
    x = proj_residual([(o_p, o_s)], at_w_out[0], x, mods[1], 2)
    eid, rank, wts, counts = moe_router(x, norm2_g[1:2], mods[1], moe_w_router[0], moe_b_router[0])
    pos, tile_e, n_tiles, tile_rows = moe_layout(eid, rank, counts)
    xsort = moe_dispatch(x, norm2_g[1:2], mods[1], pos)
    ysort = moe_grouped_swiglu(xsort, tile_e, n_tiles, tile_rows, moe_w1[0], moe_w3[0], moe_w2[0])
    yp, ys = moe_combine(x, mods[1], pos, wts, ysort, TP)

    return (yp.reshape(BP, LP, D), ys.reshape(BS, LS, D),
            new_C[:, None], new_n[:, None], new_m[:, None], new_k[:, None], new_v[:, None])
```

```python
import functools
import math

import numpy as np
import jax
import jax.numpy as jnp
from jax import lax
from jax.experimental import pallas as pl
from jax.experimental.pallas import tpu as pltpu

F32 = jnp.float32
BF16 = jnp.bfloat16
HIGHEST = lax.Precision.HIGHEST

D = 1024
GROUP = 1024
N_PROMPT_GROUPS = 8
HY_W = 512
ML_HEADS = 4
ML_HD = 128
ML_CHUNK = 256
EVEN_MAIN = 3 * HY_W + 4 * 512
N_GATES = 16
ATT_HD = 64
ATT_HEADS = 16
ATT_KV = 4
WINDOW = 128
GRID_W = 64
ROPE_BASE = 10000.0
D_FF = 2816
N_EXPERTS = 8
EPS = 1e-6
NEG = -1e30
VMEM_LIMIT = 56 * 1024 * 1024


def _cparams(*sem, flags=None):
    return pltpu.CompilerParams(dimension_semantics=sem, vmem_limit_bytes=VMEM_LIMIT, flags=flags)


def _mod_row(i, tm):
    return jnp.maximum(i * tm // GROUP - (N_PROMPT_GROUPS - 1), 0)


def _silu(x):
    return x * jax.nn.sigmoid(x)


def _bdot(a, b):
    return jnp.dot(a.astype(BF16), b.astype(BF16), preferred_element_type=F32)


def _norm_mod(x, g, sh, sc):
    y = x * lax.rsqrt(jnp.mean(x * x, axis=-1, keepdims=True) + EPS) * g
    return y * (1.0 + sc) + sh


def _adaln_kernel(c_ref, w_ref, b_ref, o_ref):
    s = _silu(c_ref[...])
    o_ref[...] = jnp.dot(s, w_ref[...], precision=HIGHEST, preferred_element_type=F32) + b_ref[...]


def adaln_table(cond, w_ada, b_ada):
    depth = w_ada.shape[0]
    tn = 1536
    out = pl.pallas_call(
        _adaln_kernel,
        grid=(depth, 6 * D // tn),
        in_specs=[pl.BlockSpec((16, D), lambda l, j: (0, 0)),
                  pl.BlockSpec((None, D, tn), lambda l, j: (l, 0, j)),
                  pl.BlockSpec((None, 1, tn), lambda l, j: (l, 0, j))],
        out_specs=pl.BlockSpec((None, 16, tn), lambda l, j: (l, 0, j)),
        out_shape=jax.ShapeDtypeStruct((depth, 16, 6 * D), F32),
        compiler_params=_cparams("parallel", "parallel"),
        name="adaln",
    )(cond, w_ada, b_ada.reshape(depth, 1, 6 * D))
    return out.reshape(depth, 16, 1, 6 * D)


def _part_specs(block, n_p):
    return [pl.BlockSpec(block, lambda i, *_: (jnp.minimum(i, n_p - 1), 0)),
            pl.BlockSpec(block, lambda i, *_: (jnp.maximum(i - n_p, 0), 0))]


def _pick(is_prompt, p_ref, s_ref):
    return jnp.where(is_prompt, p_ref[...], s_ref[...])


def _mod_spec(k, tm):
    return pl.BlockSpec((None, 1, D), lambda i, *_: (_mod_row(i, tm), 0, k))


def _log_sigmoid(x):
    return jnp.minimum(x, 0.0) - jnp.log(1.0 + jnp.exp(-jnp.abs(x)))


def _split3(x):
    hi = x.astype(BF16)
    r = x - hi.astype(F32)
    mid = r.astype(BF16)
    return hi, mid, (r - mid.astype(F32)).astype(BF16)


def _even_in_kernel(xp_ref, xs_ref, g_ref, sh_ref, sc_ref, w_ref, wg_ref, bg_ref, lo_ref, up_ref, u_ref, gate_ref,
                    h_scr, *, n_p):
    is_prompt = pl.program_id(0) < n_p

    @pl.when(pl.program_id(1) == 0)
    def _():
        h = _norm_mod(_pick(is_prompt, xp_ref, xs_ref), g_ref[...], sh_ref[...], sc_ref[...]).astype(BF16)
        h_scr[...] = h
        gates = _bdot(h, wg_ref[...]) + bg_ref[...]
        lf = _log_sigmoid(gates)
        col = lax.broadcasted_iota(jnp.int32, (1, N_GATES), 1)
        is_forget = (col // ML_HEADS) % 2 == 1
        is_rev = col >= N_GATES // 2
        for ch in range(h.shape[0] // ML_CHUNK):
            sl = slice(ch * ML_CHUNK, (ch + 1) * ML_CHUNK)
            parts = _split3(lf[sl])
            cf = sum(jnp.dot(lo_ref[...], p, preferred_element_type=F32) for p in parts)
            cr = sum(jnp.dot(up_ref[...], p, preferred_element_type=F32) for p in parts)
            gate_ref[sl, :] = jnp.where(is_forget, jnp.where(is_rev, cr, cf), gates[sl])

    u_ref[...] = _bdot(h_scr[...], w_ref[...]).astype(u_ref.dtype)


def even_in_proj(xp, xs, g, mods, w_in, b_gate, tm=1024, tn=512):
    T = xp.shape[0] + xs.shape[0]
    tri = np.tril(np.ones((ML_CHUNK, ML_CHUNK), np.float32))
    lo, up = jnp.asarray(tri).astype(BF16), jnp.asarray(tri.T).astype(BF16)
    return pl.pallas_call(
        functools.partial(_even_in_kernel, n_p=xp.shape[0] // tm),
        grid=(T // tm, EVEN_MAIN // tn),
        in_specs=_part_specs((tm, D), xp.shape[0] // tm) + [
                  pl.BlockSpec((1, D), lambda i, j: (0, 0)),
                  _mod_spec(0, tm), _mod_spec(1, tm),
                  pl.BlockSpec((D, tn), lambda i, j: (0, j)),
                  pl.BlockSpec((D, N_GATES), lambda i, j: (0, 0)),
                  pl.BlockSpec((1, N_GATES), lambda i, j: (0, 0)),
                  pl.BlockSpec((ML_CHUNK, ML_CHUNK), lambda i, j: (0, 0)),
                  pl.BlockSpec((ML_CHUNK, ML_CHUNK), lambda i, j: (0, 0))],
        out_specs=[pl.BlockSpec((tm, tn), lambda i, j: (i, j)),
                   pl.BlockSpec((tm, N_GATES), lambda i, j: (i, 0))],
        out_shape=[jax.ShapeDtypeStruct((T, EVEN_MAIN), BF16),
                   jax.ShapeDtypeStruct((T, N_GATES), F32)],
        scratch_shapes=[pltpu.VMEM((tm, D), BF16)],
        compiler_params=_cparams("parallel", "arbitrary"),
        name="even_in_proj",
    )(xp, xs, g, mods, mods, w_in, w_in[:, EVEN_MAIN:], b_gate, lo, up)


def _dft_tables(L):
    n = 2 * L
    f = np.arange(L, dtype=np.int64)[:, None]
    s = np.arange(L, dtype=np.int64)[None, :]
    ang = 2.0 * np.pi * ((f * s) % n).astype(np.float64) / n
    fwd = np.concatenate([np.cos(ang), -np.sin(ang)], axis=0)
    fwd[L, :] = np.where(np.arange(L) % 2 == 0, 1.0, -1.0)
    t = np.arange(L, dtype=np.int64)[:, None]
    ff = np.arange(L, dtype=np.int64)[None, :]
    ang = 2.0 * np.pi * ((t * ff) % n).astype(np.float64) / n
    inv_re = 2.0 * np.cos(ang) / n
    inv_re[:, 0] = 1.0 / n
    inv_im = -2.0 * np.sin(ang) / n
    inv_im[:, 0] = np.where(np.arange(L) % 2 == 0, 1.0, -1.0) / n
    inv = np.concatenate([inv_re, inv_im], axis=1)
    return fwd.astype(np.float32), inv.astype(np.float32)


def _filter_tables(L):
    t = np.linspace(0.0, 1.0, L, dtype=np.float32).astype(np.float64)[:, None]
    w = 2.0 * math.pi * np.arange(L, dtype=np.float64)[:, None] / L
    bands = np.linspace(1e-4, 16 - 1, 16, dtype=np.float32).astype(np.float64)[None, :]
    z = np.concatenate([t, np.cos(bands * w), -np.sin(bands * w)], axis=-1)
    zp = np.zeros((L, 128), np.float64)
    zp[:, :z.shape[1]] = z
    max_decay = math.log(1e-2) / 0.3
    min_decay = math.log(1e-2) / 1.5
    deltas = np.linspace(min_decay, max_decay, HY_W, dtype=np.float32).astype(np.float64)
    decay = np.exp(-t * np.abs(deltas))
    return zp.astype(np.float32), decay.astype(np.float32)


def _hy_filter_kernel(z_ref, dec_ref, w1_ref, b1_ref, w2_ref, b2_ref, w3_ref, fr_ref, fwd_ref,
                      ka_ref, kb_ref):
    L = z_ref.shape[0]
    hdot = functools.partial(jnp.dot, precision=HIGHEST, preferred_element_type=F32)
    h = jnp.sin(fr_ref[0:1, :] * (hdot(z_ref[...], w1_ref[...]) + b1_ref[...]))
    h = jnp.sin(fr_ref[1:2, :] * (hdot(h, w2_ref[...]) + b2_ref[...]))
    h = hdot(h, w3_ref[...])
    row0 = lax.broadcasted_iota(jnp.int32, (L, 1), 0) == 0
    h0 = h[:, :HY_W] * dec_ref[...]
    h1 = h[:, HY_W:] * dec_ref[...]
    l1 = jnp.sum(jnp.abs(h0), axis=0, keepdims=True) + jnp.sum(jnp.abs(h1), axis=0, keepdims=True)
    inv = 1.0 / l1
    h0 = h0 * inv
    h1 = jnp.where(row0, 0.0, h1 * inv)
    f0 = _bdot(fwd_ref[...], h0)
    f1 = _bdot(fwd_ref[...], h1)
    ka_ref[...] = f0[:L] + f1[:L]
    kb_ref[...] = jnp.where(row0, f0[L:] + f1[L:], f0[L:] - f1[L:])


def _const_spec(a, n_grid):
    return pl.BlockSpec(a.shape, lambda *_: (0,) * a.ndim, pipeline_mode=pl.Buffered(1))


def hyena_filter_spectra(L, w1, b1, w2, b2, w3, freq, fwd):
    z, dec = _filter_tables(L)
    pad2 = lambda a, r, c: jnp.pad(a, ((0, r - a.shape[0]), (0, c - a.shape[1])))
    args = (jnp.asarray(z), jnp.asarray(dec), pad2(w1, 128, 128), pad2(b1[None], 1, 128),
            pad2(w2, 128, 128), pad2(b2[None], 1, 128), pad2(w3, 128, 4 * HY_W), pad2(freq, 2, 128), fwd)
    in_specs = [_const_spec(a, 1) for a in args]
    in_specs[6] = pl.BlockSpec((128, 2 * HY_W), lambda o: (0, o))
    shp = jax.ShapeDtypeStruct((2, L, HY_W), F32)
    out_spec = pl.BlockSpec((None, L, HY_W), lambda o: (o, 0, 0))
    return pl.pallas_call(
        _hy_filter_kernel,
        grid=(2,),
        in_specs=in_specs,
        out_specs=[out_spec, out_spec],
        out_shape=[shp, shp],
        compiler_params=_cparams("arbitrary"),
        name=f"hyena_filter_{L}",
    )(*args)


def _hyena_kernel(u_ref, cw_ref, d_ref, fwd_ref, inv_ref, ka_ref, kb_ref, o_ref):
    nb, L = u_ref.shape[0], u_ref.shape[1]
    row = lax.broadcasted_iota(jnp.int32, (L, 1), 0)
    first, last = row == 0, row == L - 1
    fwd = fwd_ref[...].astype(BF16)
    inv = inv_ref[...].astype(BF16)

    def long_conv(z, o):
        zf = jnp.dot(fwd, z.astype(BF16), preferred_element_type=F32)
        a, b = zf[:L], zf[L:]
        ka, kb = ka_ref[o], kb_ref[o]
        yr = a * ka - jnp.where(first, 0.0, b * kb)
        yi = jnp.where(first, b * kb, a * kb + b * ka)
        return (jnp.dot(inv[:, :L], yr.astype(BF16), preferred_element_type=F32)
                + jnp.dot(inv[:, L:], yi.astype(BF16), preferred_element_type=F32))

    for bi in range(nb):
        u = u_ref[bi].astype(F32)
        prev = jnp.where(first, 0.0, pltpu.roll(u, 1, 0))
        nxt = jnp.where(last, 0.0, pltpu.roll(u, L - 1, 0))
        u = prev * cw_ref[0:1, :] + u * cw_ref[1:2, :] + nxt * cw_ref[2:3, :]
        v, x1, x2 = u[:, :HY_W], u[:, HY_W:2 * HY_W], u[:, 2 * HY_W:]
        z = x1 * (long_conv(v, 0) + d_ref[0:1, :] * v)
        z = x2 * (long_conv(z, 1) + d_ref[1:2, :] * z)
        o_ref[bi] = z.astype(o_ref.dtype)


def hyena_mix(u, seq0, B, L, conv_w, d_skip, fwd, inv, ka, kb, nb):
    u3 = u.reshape(-1, L, EVEN_MAIN)
    full = lambda a: _const_spec(a, 1)
    out = pl.pallas_call(
        _hyena_kernel,
        grid=(B // nb,),
        in_specs=[pl.BlockSpec((nb, L, 3 * HY_W), lambda b: (b + seq0 // nb, 0, 0)),
                  full(conv_w), full(d_skip), full(fwd), full(inv), full(ka), full(kb)],
        out_specs=pl.BlockSpec((nb, L, HY_W), lambda b: (b, 0, 0)),
        out_shape=jax.ShapeDtypeStruct((B, L, HY_W), BF16),
        compiler_params=_cparams("parallel"),
        name=f"hyena_{L}",
    )(u3, conv_w, d_skip, fwd, inv, ka, kb)
    return out.reshape(B * L, HY_W)


def _mlstm_kernel(*refs, has_state, want_state):
    q_ref, k_ref, v_ref, o_ref, gc_ref, gr_ref, ng_ref = refs[:7]
    refs = refs[7:]
    if has_state:
        c0t_ref, n0b_ref, m0_ref = refs[:3]
        refs = refs[3:]
    y_ref = refs[0]
    if want_state:
        c_out, n_out, m_out = refs[1:4]
    L, d = q_ref.shape[0], ML_HD
    T = min(ML_CHUNK, L)
    nc = L // T
    scale = 1.0 / math.sqrt(d)
    nt = (((1,), (1,)), ((), ()))
    si = lax.broadcasted_iota(jnp.int32, (T, T), 0)
    ti = lax.broadcasted_iota(jnp.int32, (T, T), 1)
    allowed = (si <= ti, si >= ti)
    chains = [(dr, h) for dr in range(2) for h in range(ML_HEADS)]
    gcol = lambda dr, gi, h: dr * 2 * ML_HEADS + gi * ML_HEADS + h

    caug_t, m = {}, {}
    for ch in chains:
        dr, h = ch
        if has_state:
            caug_t[ch] = jnp.concatenate([c0t_ref[dr, h], n0b_ref[dr, h]], axis=0)
            m[ch] = m0_ref[dr, h:h + 1, 0:1]
        else:
            caug_t[ch], m[ch] = jnp.zeros((2 * d, d), F32), jnp.zeros((1, 1), F32)

    chunk_cache = {}

    def chunk_data(h, j):
        if (h, j) not in chunk_cache:
            sl, hl = slice(j * T, (j + 1) * T), slice(h * d, (h + 1) * d)
            q = q_ref[sl, hl]
            ks = (k_ref[sl, hl].astype(F32) * scale).astype(BF16)
            v_t = v_ref[sl, hl].astype(F32).T
            vaug_t = jnp.concatenate([v_t, jnp.ones((d, T), F32)], axis=0).astype(BF16)
            s_raw = lax.dot_general(ks, q, nt, preferred_element_type=F32)
            chunk_cache[(h, j)] = (q, ks, v_t, vaug_t, s_raw)
        return chunk_cache[(h, j)]

    h_sum = {}
    for it in range(nc):
        step = {ch: (it if ch[0] == 0 else nc - 1 - it) for ch in chains}
        data = {ch: chunk_data(ch[1], step[ch]) for ch in chains}
        inter_t = {ch: lax.dot_general(caug_t[ch].astype(BF16), data[ch][0], nt, preferred_element_type=F32)
                   for ch in chains}
        gate = {}
        for ch in chains:
            dr, h = ch
            sl = slice(step[ch] * T, (step[ch] + 1) * T)
            li_r, b_r = gr_ref[gcol(dr, 0, h):gcol(dr, 0, h) + 1, sl], gr_ref[gcol(dr, 1, h):gcol(dr, 1, h) + 1, sl]
            src = gc_ref[sl, gcol(dr, 0, h):gcol(dr, 0, h) + 1] - gc_ref[sl, gcol(dr, 1, h):gcol(dr, 1, h) + 1]
            dm = jnp.where(allowed[dr], src + b_r, NEG)
            inter = b_r + m[ch]
            m_t = jnp.maximum(inter, jnp.max(dm, axis=0, keepdims=True))
            b_end = b_r[:, T - 1:T] if dr == 0 else b_r[:, 0:1]
            g_r = b_end - b_r + li_r
            m_new = jnp.maximum(b_end + m[ch], jnp.max(g_r, axis=1, keepdims=True))
            gate[ch] = (jnp.exp(dm - m_t), jnp.exp(inter - m_t), jnp.exp(-m_t), jnp.exp(g_r - m_new),
                        jnp.exp(b_end + m[ch] - m_new), m_new)
        for ch in chains:
            q, ks, v_t, vaug_t, s_raw = data[ch]
            w_intra, w_inter, floor, w_tok, decay, m_new = gate[ch]
            acc = jnp.dot(vaug_t, (s_raw * w_intra).astype(BF16), preferred_element_type=F32) + w_inter * inter_t[ch]
            h_t = acc[:d] / jnp.maximum(jnp.abs(acc[d:]), floor)
            key = (ch[1], step[ch])
            h_sum[key] = h_t if key not in h_sum else h_sum[key] + h_t
            vw_t = jnp.concatenate([v_t * w_tok, jnp.broadcast_to(w_tok, (d, T))], axis=0).astype(BF16)
            caug_t[ch] = decay * caug_t[ch] + jnp.dot(vw_t, ks, preferred_element_type=F32)
            m[ch] = m_new

    if want_state:
        for ch in chains:
            dr, h = ch
            c_out[dr, h] = caug_t[ch][:d].T
            n_out[dr, h:h + 1, :] = caug_t[ch][d:d + 1, :]
            m_out[dr, h:h + 1, :] = jnp.broadcast_to(m[ch], (1, d))
    for h in range(ML_HEADS):
        for j in range(nc):
            sl, hl = slice(j * T, (j + 1) * T), slice(h * d, (h + 1) * d)
            hv = h_sum[(h, j)].T
            y = hv * lax.rsqrt(jnp.mean(hv * hv, axis=-1, keepdims=True) + EPS) * ng_ref[:, hl]
            y_ref[sl, hl] = (y * jax.nn.sigmoid(o_ref[sl, hl].astype(F32))).astype(y_ref.dtype)


def mlstm_mix(u, gates, seq0, B, L, norm_g, state=None, want_state=False):
    u3 = u.reshape(-1, L, EVEN_MAIN)
    gc = gates.reshape(-1, L, N_GATES)[seq0:seq0 + B]
    gr = gc.transpose(0, 2, 1)
    width = ML_HEADS * ML_HD
    col = lambda i: pl.BlockSpec((None, L, width), lambda b: (b + seq0, 0, (3 * HY_W + i * width) // width))
    in_specs = [col(0), col(1), col(2), col(3),
                pl.BlockSpec((None, L, N_GATES), lambda b: (b, 0, 0)),
                pl.BlockSpec((None, N_GATES, L), lambda b: (b, 0, 0)),
                pl.BlockSpec((1, width), lambda b: (0, 0))]
    args = [u3, u3, u3, u3, gc, gr, norm_g.reshape(1, width)]
    sspec = pl.BlockSpec((None, 2, ML_HEADS, ML_HD, ML_HD), lambda b: (b, 0, 0, 0, 0))
    vspec = pl.BlockSpec((None, 2, ML_HEADS, ML_HD), lambda b: (b, 0, 0, 0))
    if state is not None:
        C0, n0, m0 = state
        in_specs += [sspec, sspec, vspec]
        args += [C0.swapaxes(-1, -2), jnp.broadcast_to(n0[..., None, :], C0.shape),
                 jnp.broadcast_to(m0[..., None], n0.shape)]
    out_specs = [pl.BlockSpec((None, L, width), lambda b: (b, 0, 0))]
    out_shape = [jax.ShapeDtypeStruct((B, L, width), BF16)]
    if want_state:
        out_specs += [sspec, vspec, vspec]
        out_shape += [jax.ShapeDtypeStruct((B, 2, ML_HEADS, ML_HD, ML_HD), F32),
                      jax.ShapeDtypeStruct((B, 2, ML_HEADS, ML_HD), F32),
                      jax.ShapeDtypeStruct((B, 2, ML_HEADS, ML_HD), F32)]
    outs = pl.pallas_call(
        functools.partial(_mlstm_kernel, has_state=state is not None, want_state=want_state),
        grid=(B,),
        in_specs=in_specs, out_specs=out_specs, out_shape=out_shape,
        compiler_params=_cparams("parallel"),
        name=f"mlstm_{L}",
    )(*args)
    y = outs[0].reshape(B * L, width)
    if not want_state:
        return y
    _, C, n, m = outs
    return y, C, n, m[..., 0]


def _proj_res_kernel(*refs, n_in, n_x, n_p):
    a_refs = refs[:2 * n_in]
    w_ref = refs[2 * n_in]
    x_refs = refs[2 * n_in + 1:2 * n_in + 1 + n_x]
    gate_ref, o_ref = refs[2 * n_in + 1 + n_x:]
    is_prompt = pl.program_id(0) < n_p
    k0 = 0
    acc = None
    for j in range(n_in):
        a = _pick(is_prompt, a_refs[2 * j], a_refs[2 * j + 1])
        kw = a.shape[1]
        part = _bdot(a, w_ref[k0:k0 + kw, :])
        acc = part if acc is None else acc + part
        k0 += kw
    x = _pick(is_prompt, *x_refs) if n_x == 2 else x_refs[0][...]
    o_ref[...] = x + gate_ref[...] * acc


def proj_residual(acts, w, x, mods, gate_idx, tm=1024):
    xs = tuple(x) if isinstance(x, (tuple, list)) else (x,)
    T = sum(a.shape[0] for a in acts[0])
    n_p = acts[0][0].shape[0] // tm
    in_specs = []
    for pair in acts:
        in_specs += _part_specs((tm, pair[0].shape[1]), n_p)
    in_specs.append(pl.BlockSpec(w.shape, lambda i: (0, 0)))
    in_specs += _part_specs((tm, D), n_p) if len(xs) == 2 else [pl.BlockSpec((tm, D), lambda i: (i, 0))]
    in_specs.append(_mod_spec(gate_idx, tm))
    return pl.pallas_call(
        functools.partial(_proj_res_kernel, n_in=len(acts), n_x=len(xs), n_p=n_p),
        grid=(T // tm,),
        in_specs=in_specs,
        out_specs=pl.BlockSpec((tm, D), lambda i: (i, 0)),
        out_shape=jax.ShapeDtypeStruct((T, D), F32),
        compiler_params=_cparams("parallel"),
        name="proj_residual",
    )(*[a for pair in acts for a in pair], w, *xs, mods)


SWIGLU_ROWS = 512


def _swiglu_accumulate(h_scr, acc_scr, w1_ref, w3_ref, w2_ref, rows):
    w1, w3, w2 = w1_ref[...].astype(BF16), w3_ref[...].astype(BF16), w2_ref[...].astype(BF16)
    groups = [slice(r, r + SWIGLU_ROWS) for r in range(0, rows, SWIGLU_ROWS)]
    ups = []
    for sl in groups:
        h = h_scr[sl, :]
        ups.append((jnp.dot(h, w1, preferred_element_type=F32), jnp.dot(h, w3, preferred_element_type=F32)))
    for sl, (a, b) in zip(groups, ups):
        mid = (_silu(a) * b).astype(BF16)
        acc_scr[sl, :] += jnp.dot(mid, w2, preferred_element_type=F32)


def _ffn_kernel(x_ref, g_ref, sh_ref, sc_ref, gate_ref, w1_ref, w3_ref, w2_ref, o_ref, h_scr, acc_scr):
    c = pl.program_id(1)

    @pl.when(c == 0)
    def _():
        h_scr[...] = _norm_mod(x_ref[...], g_ref[...], sh_ref[...], sc_ref[...]).astype(BF16)
        acc_scr[...] = jnp.zeros_like(acc_scr)

    _swiglu_accumulate(h_scr, acc_scr, w1_ref, w3_ref, w2_ref, h_scr.shape[0])

    @pl.when(c == pl.num_programs(1) - 1)
    def _():
        o_ref[...] = x_ref[...] + gate_ref[...] * acc_scr[...]


def ffn_residual(x, g, mods, w1, w3, w2, tm=1024, tf=256):
    T = x.shape[0]
    return pl.pallas_call(
        _ffn_kernel,
        grid=(T // tm, D_FF // tf),
        in_specs=[pl.BlockSpec((tm, D), lambda i, c: (i, 0)),
                  pl.BlockSpec((1, D), lambda i, c: (0, 0)),
                  _mod_spec(3, tm), _mod_spec(4, tm), _mod_spec(5, tm),
                  pl.BlockSpec((D, tf), lambda i, c: (0, c)),
                  pl.BlockSpec((D, tf), lambda i, c: (0, c)),
                  pl.BlockSpec((tf, D), lambda i, c: (c, 0))],
        out_specs=pl.BlockSpec((tm, D), lambda i, c: (i, 0)),
        out_shape=jax.ShapeDtypeStruct((T, D), F32),
        scratch_shapes=[pltpu.VMEM((tm, D), BF16), pltpu.VMEM((tm, D), F32)],
        compiler_params=_cparams("parallel", "arbitrary"),
        name="ffn",
    )(x, g, mods, mods, mods, w1, w3, w2)


def _qkv_kernel(x_ref, g_ref, sh_ref, sc_ref, w_ref, q_ref, kv_ref, h_scr):
    j = pl.program_id(1)

    @pl.when(j == 0)
    def _():
        h_scr[...] = _norm_mod(x_ref[...], g_ref[...], sh_ref[...], sc_ref[...]).astype(BF16)

    @pl.when(j < 2)
    def _():
        q_ref[...] = _bdot(h_scr[...], w_ref[...]).astype(q_ref.dtype)

    @pl.when(j == 2)
    def _():
        kv_ref[...] = _bdot(h_scr[...], w_ref[...])


def qkv_proj(x, g, mods, w_qkv, tm=1024):
    T = x.shape[0]
    tn = 512
    return pl.pallas_call(
        _qkv_kernel,
        grid=(T // tm, 3),
        in_specs=[pl.BlockSpec((tm, D), lambda i, j: (i, 0)),
                  pl.BlockSpec((1, D), lambda i, j: (0, 0)),
                  _mod_spec(0, tm), _mod_spec(1, tm),
                  pl.BlockSpec((D, tn), lambda i, j: (0, j))],
        out_specs=[pl.BlockSpec((tm, tn), lambda i, j: (i, jnp.minimum(j, 1))),
                   pl.BlockSpec((tm, tn), lambda i, j: (i, 0))],
        out_shape=[jax.ShapeDtypeStruct((T, ATT_HEADS * ATT_HD), BF16),
                   jax.ShapeDtypeStruct((T, 2 * ATT_KV * ATT_HD), F32)],
        scratch_shapes=[pltpu.VMEM((tm, D), BF16)],
        compiler_params=_cparams("parallel", "arbitrary"),
        name="qkv_proj",
    )(x, g, mods, mods, w_qkv)


def _rope_tables(L):
    half = ATT_HD // 2
    pos_r = (np.arange(L) // GRID_W).astype(np.float32)
    pos_c = (np.arange(L) % GRID_W).astype(np.float32)
    inv = (ROPE_BASE ** (-np.arange(0, half, 2, dtype=np.float32) / half)).astype(np.float32)
    cos = np.zeros((L, ATT_HD), np.float64)
    sin = np.zeros((L, ATT_HD), np.float64)
    for base, pos in ((0, pos_r), (half, pos_c)):
        ang = (pos[:, None] * inv[None, :]).astype(np.float32).astype(np.float64)
        cos[:, base:base + half] = np.concatenate([np.cos(ang), np.cos(ang)], axis=1)
        sin[:, base:base + half] = np.concatenate([-np.sin(ang), np.sin(ang)], axis=1)
    return (np.tile(cos, (1, 4)).astype(np.float32), np.tile(sin, (1, 4)).astype(np.float32))


def _seg_rms(x):
    w = x.shape[1]
    ri = lax.broadcasted_iota(jnp.int32, (w, w), 0) // ATT_HD
    ci = lax.broadcasted_iota(jnp.int32, (w, w), 1) // ATT_HD
    ss = _bdot(x * x, (ri == ci).astype(F32))
    return x * lax.rsqrt(ss * (1.0 / ATT_HD) + EPS)


LOG2E = 1.4426950408889634


def _exp2_bf16(x):
    return jnp.exp2(x.astype(BF16))


def _both_halves(tile, low):
    lane = lax.broadcasted_iota(jnp.int32, tile.shape, 1)
    other = pltpu.roll(tile, ATT_HD, 1)
    return jnp.where((lane < ATT_HD) == low, tile, other)


def _swap16(x):
    w = x.shape[1]
    lane = lax.broadcasted_iota(jnp.int32, x.shape, 1)
    return jnp.where(lane % 32 < 16, pltpu.roll(x, w - 16, 1), pltpu.roll(x, 16, 1))


def _attn_kernel(*refs, latent, tq):
    if latent:
        (q_ref, kv_ref, ck_ref, cv_ref, qg_ref, kg_ref, sink_ref, cosq_ref, sinq_ref, cosk_ref, sink_t_ref,
         o_ref, kk_scr, vt_scr, ckk_scr, cvt_scr) = refs
    else:
        q_ref, kv_ref, qg_ref, kg_ref, sink_ref, o_ref, ko_ref, vo_ref, kk_scr, vt_scr = refs
    L = kv_ref.shape[0]
    gw = ATT_KV * ATT_HD
    pw = 2 * ATT_HD
    qb = pl.program_id(1)

    vrows = vt_scr.shape[2]
    nblk = L // pw

    def vt_aug(tile, low):
        vt = tile.T[0:ATT_HD, :] if low else tile.T[ATT_HD:, :]
        return jnp.concatenate([vt, jnp.ones((vrows - ATT_HD, tile.shape[0]), F32)], axis=0).astype(BF16)

    @pl.when(qb == 0)
    def _():
        kn = _seg_rms(kv_ref[:, :gw]) * kg_ref[...]
        v = kv_ref[:, gw:]
        if latent:
            kn = kn * cosk_ref[...] + _swap16(kn) * sink_t_ref[...]
        else:
            for c in range(ATT_KV):
                ko_ref[c] = kn[:, c * ATT_HD:(c + 1) * ATT_HD]
                vo_ref[c] = v[:, c * ATT_HD:(c + 1) * ATT_HD]
        for c in range(ATT_KV):
            tile, low = slice((c // 2) * pw, (c // 2 + 1) * pw), c % 2 == 0
            kk_scr[c] = _both_halves(kn[:, tile], low).astype(BF16)
            for j in range(nblk):
                vt_scr[c, j] = vt_aug(v[j * pw:(j + 1) * pw, tile], low)
            if latent:
                ck, cv = ck_ref[c], cv_ref[c]
                ckk_scr[c] = jnp.concatenate([ck, ck], axis=1).astype(BF16)
                cvt_scr[c] = vt_aug(jnp.concatenate([cv, cv], axis=1), True)

    if latent:
        span = tq + 2 * WINDOW
        start = pl.multiple_of(jnp.clip(qb * tq - WINDOW, 0, L - span), WINDOW)
        blk0 = start // pw
        s_pos = start + lax.broadcasted_iota(jnp.int32, (span, tq), 0)
        t_pos = qb * tq + lax.broadcasted_iota(jnp.int32, (span, tq), 1)
        win_bias = jnp.where(jnp.abs(t_pos - s_pos) <= WINDOW, 0.0, NEG)
    else:
        span, blk0 = L, 0

    nt = (((1,), (1,)), ((), ()))
    low_q = lax.broadcasted_iota(jnp.int32, (tq, pw), 1) < ATT_HD
    def group_scores(c):
        qc = _seg_rms(q_ref[:, c * gw:(c + 1) * gw].astype(F32)) * qg_ref[...]
        if latent:
            qc = qc * cosq_ref[...] + _swap16(qc) * sinq_ref[...]
            kw = kk_scr[c, pl.ds(start, span), :]
        else:
            kw = kk_scr[c]
        qc = qc * (LOG2E / math.sqrt(ATT_HD))
        scores = []
        for g in range(ATT_KV):
            qt = qc[:, (g // 2) * pw:(g // 2 + 1) * pw]
            qm = jnp.where(low_q if g % 2 == 0 else ~low_q, qt, 0.0).astype(BF16)
            lw = lax.dot_general(kw, qm, nt, preferred_element_type=F32)
            lc = lax.dot_general(ckk_scr[c], qm, nt, preferred_element_type=F32) if latent else None
            scores.append((lw, lc))
        return scores

    def group_outputs(c, scores):
        vw = jnp.concatenate([vt_scr[c, blk0 + j] for j in range(span // pw)], axis=1)
        outs = []
        for g in range(ATT_KV):
            head = c * ATT_KV + g
            sink = sink_ref[:, head:head + 1] * LOG2E
            lw, lc = scores[g]
            if latent:
                lw = lw + win_bias
                mx = jnp.maximum(jnp.maximum(jnp.max(lw, axis=0, keepdims=True),
                                             jnp.max(lc, axis=0, keepdims=True)), sink)
                r = jnp.dot(vw, _exp2_bf16(lw - mx), preferred_element_type=F32) + jnp.dot(
                    cvt_scr[c], _exp2_bf16(lc - mx), preferred_element_type=F32)
            else:
                mx = jnp.maximum(jnp.max(lw, axis=0, keepdims=True), sink)
                r = jnp.dot(vw, _exp2_bf16(lw - mx), preferred_element_type=F32)
            den = r[ATT_HD:ATT_HD + 1, :] + jnp.exp2(sink - mx)
            outs.append(r[0:ATT_HD, :] / den)
        for t in range(2):
            o_ref[:, c * gw + t * pw:c * gw + (t + 1) * pw] = (
                jnp.concatenate(outs[2 * t:2 * t + 2], axis=0).T.astype(o_ref.dtype))

    scores = group_scores(0)
    for c in range(ATT_KV):
        nxt = group_scores(c + 1) if c + 1 < ATT_KV else None
        group_outputs(c, scores)
        scores = nxt


def attention(q, kv, row0, q_g, k_g, sink, B, L, cache=None, tq=256):
    latent = cache is not None
    gw = ATT_KV * ATT_HD
    qg = jnp.tile(q_g, ATT_KV)[None]
    kg = jnp.tile(k_g, ATT_KV)[None]
    nq = L // tq
    const = lambda a: pl.BlockSpec(a.shape, lambda b, i: (0,) * a.ndim)
    in_specs = [pl.BlockSpec((tq, ATT_HEADS * ATT_HD), lambda b, i: (row0 // tq + b * nq + i, 0)),
                pl.BlockSpec((L, 2 * gw), lambda b, i: (row0 // L + b, 0))]
    args = [q, kv]
    vrows = ATT_HD + 16
    scratch = [pltpu.VMEM((ATT_KV, L, 2 * ATT_HD), BF16),
               pltpu.VMEM((ATT_KV, L // (2 * ATT_HD), vrows, 2 * ATT_HD), BF16)]
    out_specs = [pl.BlockSpec((tq, ATT_HEADS * ATT_HD), lambda b, i: (b * nq + i, 0))]
    out_shape = [jax.ShapeDtypeStruct((B * L, ATT_HEADS * ATT_HD), BF16)]
    if latent:
        ck, cv = cache
        P = ck.shape[2]
        cos, sin = (jnp.asarray(t) for t in _rope_tables(L))
        in_specs += [pl.BlockSpec((None, ATT_KV, P, ATT_HD), lambda b, i: (b, 0, 0, 0))] * 2
        args += [ck, cv]
        in_specs += [const(qg), const(kg), pl.BlockSpec((1, ATT_HEADS), lambda b, i: (0, 0)),
                     pl.BlockSpec((tq, gw), lambda b, i: (i, 0)), pl.BlockSpec((tq, gw), lambda b, i: (i, 0)),
                     const(cos), const(sin)]
        args += [qg, kg, sink[None], cos, sin, cos, sin]
        scratch += [pltpu.VMEM((ATT_KV, P, 2 * ATT_HD), BF16), pltpu.VMEM((ATT_KV, vrows, P), BF16)]
    else:
        in_specs += [const(qg), const(kg), pl.BlockSpec((1, ATT_HEADS), lambda b, i: (0, 0))]
        args += [qg, kg, sink[None]]
        cache_spec = pl.BlockSpec((None, ATT_KV, L, ATT_HD), lambda b, i: (b, 0, 0, 0))
        out_specs += [cache_spec, cache_spec]
        out_shape += [jax.ShapeDtypeStruct((B, ATT_KV, L, ATT_HD), F32)] * 2
    outs = pl.pallas_call(
        functools.partial(_attn_kernel, latent=latent, tq=tq),
        grid=(B, nq),
        in_specs=in_specs, out_specs=out_specs, out_shape=out_shape,
        scratch_shapes=scratch,
        compiler_params=_cparams("parallel", "arbitrary"),
        name="attn_latent" if latent else "attn_context",
    )(*args)
    return outs[0] if latent else outs


MOE_TM = 1024
MOE_MAX_TILES = 2 * 16384 // MOE_TM + N_EXPERTS


def _router_kernel(x_ref, g_ref, sh_ref, sc_ref, wr_ref, br_ref, tri_ref, eid_ref, rank_ref, wts_ref, cnt_ref):
    @pl.when(pl.program_id(0) == 0)
    def _():
        cnt_ref[...] = jnp.zeros_like(cnt_ref)

    h = _norm_mod(x_ref[...], g_ref[...], sh_ref[...], sc_ref[...])
    lg = lax.dot_general(wr_ref[...], h, (((1,), (1,)), ((), ())), precision=HIGHEST,
                         preferred_element_type=F32) + br_ref[...]
    row = lax.broadcasted_iota(jnp.int32, lg.shape, 0)
    m1 = jnp.max(lg, axis=0, keepdims=True)
    i1 = jnp.min(jnp.where(lg == m1, row, N_EXPERTS), axis=0, keepdims=True)
    l2 = jnp.where(row == i1, -jnp.inf, lg)
    m2 = jnp.max(l2, axis=0, keepdims=True)
    i2 = jnp.min(jnp.where(l2 == m2, row, N_EXPERTS), axis=0, keepdims=True)
    e2 = jnp.exp(m2 - m1)
    w1 = 1.0 / (1.0 + e2)
    eid_ref[...] = jnp.concatenate([i1, i2], axis=0)
    wts_ref[...] = jnp.concatenate([w1, e2 * w1], axis=0)
    oh1 = (row == i1).astype(F32)
    oh2 = (row == i2).astype(F32)
    cs1 = _bdot(oh1, tri_ref[...])
    cs2 = _bdot(oh2, tri_ref[...])
    tot1 = jnp.sum(oh1, axis=1, keepdims=True)
    cnt = cnt_ref[:, 0:1]
    r1 = jnp.sum(oh1 * (cnt + cs1), axis=0, keepdims=True)
    r2 = jnp.sum(oh2 * (cnt + tot1 + cs2), axis=0, keepdims=True)
    rank_ref[...] = jnp.concatenate([r1, r2], axis=0).astype(jnp.int32)
    cnt_ref[...] = cnt_ref[...] + tot1 + jnp.sum(oh2, axis=1, keepdims=True)


def moe_router(x, g, mods, w_router, b_router, tm=1024):
    T = x.shape[0]
    tri = jnp.asarray(np.triu(np.ones((tm, tm), np.float32), k=1)).astype(BF16)
    tok2 = lambda dt: jax.ShapeDtypeStruct((2, T), dt)
    return pl.pallas_call(
        _router_kernel,
        grid=(T // tm,),
        in_specs=[pl.BlockSpec((tm, D), lambda i: (i, 0)),
                  pl.BlockSpec((1, D), lambda i: (0, 0)),
                  _mod_spec(3, tm), _mod_spec(4, tm),
                  pl.BlockSpec((N_EXPERTS, D), lambda i: (0, 0)),
                  pl.BlockSpec((N_EXPERTS, 1), lambda i: (0, 0)),
                  _const_spec(tri, 1)],
        out_specs=[pl.BlockSpec((2, tm), lambda i: (0, i)),
                   pl.BlockSpec((2, tm), lambda i: (0, i)),
                   pl.BlockSpec((2, tm), lambda i: (0, i)),
                   pl.BlockSpec((N_EXPERTS, 128), lambda i: (0, 0))],
        out_shape=[tok2(jnp.int32), tok2(jnp.int32), tok2(F32),
                   jax.ShapeDtypeStruct((N_EXPERTS, 128), F32)],
        compiler_params=_cparams("arbitrary"),
        name="moe_router",
    )(x, g, mods, mods, w_router.T, b_router[:, None], tri)


def moe_layout(eid, rank, counts):
    cnt = counts[:, 0].astype(jnp.int32)
    tiles = (cnt + MOE_TM - 1) // MOE_TM
    tile_end = jnp.cumsum(tiles)
    start = (tile_end - tiles) * MOE_TM
    pos = rank + jnp.sum(jnp.where(eid[..., None] == jnp.arange(N_EXPERTS), start, 0), axis=-1)
    n_tiles = tile_end[-1]
    t = jnp.arange(MOE_MAX_TILES, dtype=jnp.int32)
    tile_e = jnp.sum(t[:, None] >= tile_end[None, :], axis=1).astype(jnp.int32)
    last_e = jnp.sum((n_tiles - 1) >= tile_end).astype(jnp.int32)
    tile_e = jnp.where(t < n_tiles, tile_e, last_e)
    first = jnp.sum(jnp.where(tile_e[:, None] == jnp.arange(N_EXPERTS), (tile_end - tiles)[None, :], 0), axis=1)
    e_cnt = jnp.sum(jnp.where(tile_e[:, None] == jnp.arange(N_EXPERTS), cnt[None, :], 0), axis=1)
    tile_rows = jnp.where(t < n_tiles, jnp.clip(e_cnt - (t - first) * MOE_TM, 0, MOE_TM), 0).astype(jnp.int32)
    return pos.astype(jnp.int32), tile_e, n_tiles.astype(jnp.int32).reshape(1), tile_rows


def _row_copy(src, i, dst, j, sem):
    return pltpu.make_async_copy(src.at[pl.ds(i, 1), :], dst.at[pl.ds(j, 1), :], sem)


def _dispatch_kernel(pos_ref, x_ref, g_ref, sh_ref, sc_ref, xs_in_ref, xs_ref, h_scr, sem):
    del xs_in_ref
    tm = h_scr.shape[0]
    h_scr[...] = _norm_mod(x_ref[...], g_ref[...], sh_ref[...], sc_ref[...])

    def issue(r, carry):
        for s in range(2):
            _row_copy(h_scr, r, xs_ref, pos_ref[s, r], sem.at[s]).start()
        return carry

    lax.fori_loop(0, tm, issue, 0, unroll=8)
    for s in range(2):
        pltpu.make_async_copy(h_scr, xs_ref.at[pl.ds(0, tm), :], sem.at[s]).wait()


def moe_dispatch(x, g, mods, pos, tm=1024):
    T = x.shape[0]
    n_rows = MOE_MAX_TILES * MOE_TM
    pos3 = pos.reshape(2, T // tm, tm).transpose(1, 0, 2)
    return pl.pallas_call(
        _dispatch_kernel,
        grid=(T // tm,),
        in_specs=[pl.BlockSpec((None, 2, tm), lambda i: (i, 0, 0), memory_space=pltpu.SMEM),
                  pl.BlockSpec((tm, D), lambda i: (i, 0)),
                  pl.BlockSpec((1, D), lambda i: (0, 0)),
                  _mod_spec(3, tm), _mod_spec(4, tm),
                  pl.BlockSpec(memory_space=pl.ANY)],
        out_specs=pl.BlockSpec(memory_space=pl.ANY),
        out_shape=jax.ShapeDtypeStruct((n_rows, D), F32),
        scratch_shapes=[pltpu.VMEM((tm, D), F32), pltpu.SemaphoreType.DMA((2,))],
        input_output_aliases={5: 0},
        compiler_params=_cparams("arbitrary"),
        name="moe_dispatch",
    )(pos3, x, g, mods, mods, jnp.zeros((n_rows, D), F32))


def _moe_group_kernel(te_ref, nt_ref, tr_ref, x_ref, w1_ref, w3_ref, w2_ref, o_ref, h_scr, acc_scr):
    t, c = pl.program_id(0), pl.program_id(1)
    rows = tr_ref[t]
    half = MOE_TM // 2

    @pl.when(t < nt_ref[0])
    def _():
        @pl.when(c == 0)
        def _():
            h_scr[...] = x_ref[...].astype(BF16)
            acc_scr[...] = jnp.zeros_like(acc_scr)

        @pl.when(rows > half)
        def _():
            _swiglu_accumulate(h_scr, acc_scr, w1_ref, w3_ref, w2_ref, MOE_TM)

        @pl.when(rows <= half)
        def _():
            _swiglu_accumulate(h_scr, acc_scr, w1_ref, w3_ref, w2_ref, half)

        @pl.when(c == pl.num_programs(1) - 1)
        def _():
            o_ref[...] = acc_scr[...]

    @pl.when((t >= nt_ref[0]) & (c == pl.num_programs(1) - 1))
    def _():
        o_ref[...] = jnp.zeros_like(o_ref)


def moe_grouped_swiglu(xs, tile_e, n_tiles, tile_rows, w1, w3, w2, tf=256):
    nc = D_FF // tf
    live = lambda t, nt: t < nt[0]
    row = lambda t, c, te, nt, tr: (jnp.where(live(t, nt), t, nt[0] - 1), 0)
    wcol = lambda t, c, te, nt, tr: (te[t], 0, jnp.where(live(t, nt), c, nc - 1))
    wrow = lambda t, c, te, nt, tr: (te[t], jnp.where(live(t, nt), c, nc - 1), 0)
    return pl.pallas_call(
        _moe_group_kernel,
        grid_spec=pltpu.PrefetchScalarGridSpec(
            num_scalar_prefetch=3,
            grid=(MOE_MAX_TILES, nc),
            in_specs=[pl.BlockSpec((MOE_TM, D), row),
                      pl.BlockSpec((None, D, tf), wcol),
                      pl.BlockSpec((None, D, tf), wcol),
                      pl.BlockSpec((None, tf, D), wrow)],
            out_specs=pl.BlockSpec((MOE_TM, D), lambda t, c, te, nt, tr: (t, 0)),
            scratch_shapes=[pltpu.VMEM((MOE_TM, D), BF16), pltpu.VMEM((MOE_TM, D), F32)]),
        out_shape=jax.ShapeDtypeStruct(xs.shape, F32),
        compiler_params=_cparams("arbitrary", "arbitrary"),
        name="moe_grouped",
    )(tile_e, n_tiles, tile_rows, xs, w1, w3, w2)


def _combine_kernel(pos_ref, x_ref, gate_ref, wt_ref, ys_ref, op_ref, os_ref, a_scr, b_scr, sem, *, n_p):
    tm = a_scr.shape[0]
    bufs = (a_scr, b_scr)

    def issue(r, carry):
        for s in range(2):
            _row_copy(ys_ref, pos_ref[s, r], bufs[s], r, sem.at[s]).start()
        return carry

    lax.fori_loop(0, tm, issue, 0, unroll=8)
    for s in range(2):
        pltpu.make_async_copy(ys_ref.at[pl.ds(0, tm), :], bufs[s], sem.at[s]).wait()
    moe = wt_ref[:, 0:1] * a_scr[...] + wt_ref[:, 1:2] * b_scr[...]
    out = x_ref[...] + gate_ref[...] * moe

    @pl.when(pl.program_id(0) < n_p)
    def _():
        op_ref[...] = out

    @pl.when(pl.program_id(0) >= n_p)
    def _():
        os_ref[...] = out


def moe_combine(x, mods, pos, wts, ys, t_prompt, tm=1024):
    T = x.shape[0]
    n_p = t_prompt // tm
    pos3 = pos.reshape(2, T // tm, tm).transpose(1, 0, 2)
    return pl.pallas_call(
        functools.partial(_combine_kernel, n_p=n_p),
        grid=(T // tm,),
        in_specs=[pl.BlockSpec((None, 2, tm), lambda i: (i, 0, 0), memory_space=pltpu.SMEM),
                  pl.BlockSpec((tm, D), lambda i: (i, 0)),
                  _mod_spec(5, tm),
                  pl.BlockSpec((tm, 2), lambda i: (i, 0)),
                  pl.BlockSpec(memory_space=pl.ANY)],
        out_specs=_part_specs((tm, D), n_p),
        out_shape=[jax.ShapeDtypeStruct((t_prompt, D), F32), jax.ShapeDtypeStruct((T - t_prompt, D), F32)],
        scratch_shapes=[pltpu.VMEM((tm, D), F32), pltpu.VMEM((tm, D), F32), pltpu.SemaphoreType.DMA((2,))],
        compiler_params=_cparams("arbitrary"),
        name="moe_combine",
    )(pos3, x, mods, wts.T, ys)


def kernel(x_prompt, x_sample, state_C, state_n, state_m, cache_k, cache_v, c, c_ctx, norm1_g, norm2_g, w_ada, b_ada, ev_w_in, ev_conv, hy_w1, hy_b1, hy_w2, hy_b2, hy_w3, hy_freq, hy_d, ml_b_gate, ml_norm_g, ev_w_out, ff_w1, ff_w3, ff_w2, at_w_qkv, at_q_g, at_k_g, at_sink, at_w_out, moe_w_router, moe_b_router, moe_w1, moe_w3, moe_w2):
    BP, LP, _ = x_prompt.shape
    BS, LS, _ = x_sample.shape
    TP = BP * LP
    assert TP % GROUP == 0 and TP // GROUP == N_PROMPT_GROUPS and LS == GROUP and BS == 8

    xp, xs = x_prompt.reshape(TP, D), x_sample.reshape(BS * LS, D)
    cond = jnp.concatenate([c_ctx[None], c, jnp.zeros((16 - 1 - BS, D), F32)], axis=0)
    mods = adaln_table(cond, w_ada, b_ada)

    u, gates = even_in_proj(xp, xs, norm1_g[0:1], mods[0], ev_w_in[0], ml_b_gate[0].reshape(1, N_GATES))
    hy = []
    for seq0, B, L, nb in ((0, BP, LP, 4), (TP // LS, BS, LS, 1)):
        fwd, inv = (jnp.asarray(t).astype(BF16) for t in _dft_tables(L))
        ka, kb = hyena_filter_spectra(L, hy_w1[0], hy_b1[0], hy_w2[0], hy_b2[0], hy_w3[0], hy_freq[0], fwd)
        hy.append(hyena_mix(u, seq0, B, L, ev_conv[0], hy_d[0], fwd, inv, ka, kb, nb))
    ml_p, new_C, new_n, new_m = mlstm_mix(u, gates, 0, BP, LP, ml_norm_g[0], want_state=True)
    ml_s = mlstm_mix(u, gates, TP // LS, BS, LS, ml_norm_g[0],
                     state=(state_C[:, 0], state_n[:, 0], state_m[:, 0]))
    x = proj_residual([hy, (ml_p, ml_s)], ev_w_out[0], (xp, xs), mods[0], 2)
    x = ffn_residual(x, norm2_g[0:1], mods[0], ff_w1[0], ff_w3[0], ff_w2[0])

    q, kv = qkv_proj(x, norm1_g[1:2], mods[1], at_w_qkv[0])
    o_p, new_k, new_v = attention(q, kv, 0, at_q_g[0], at_k_g[0], at_sink[0], BP, LP)
    o_s = attention(q, kv, TP, at_q_g[0], at_k_g[0], at_sink[0], BS, LS, cache=(cache_k[:, 0], cache_v[:, 0]))
    x = proj_residual([(o_p, o_s)], at_w_out[0], x, mods[1], 2)
    eid, rank, wts, counts = moe_router(x, norm2_g[1:2], mods[1], moe_w_router[0], moe_b_router[0])
    pos, tile_e, n_tiles, tile_rows = moe_layout(eid, rank, counts)
    xsort = moe_dispatch(x, norm2_g[1:2], mods[1], pos)
    ysort = moe_grouped_swiglu(xsort, tile_e, n_tiles, tile_rows, moe_w1[0], moe_w3[0], moe_w2[0])
    yp, ys = moe_combine(x, mods[1], pos, wts, ysort, TP)

    return (yp.reshape(BP, LP, D), ys.reshape(BS, LS, D),
            new_C[:, None], new_n[:, None], new_m[:, None], new_k[:, None], new_v[:, None])
```

```python
import functools
import math

import numpy as np
import jax
import jax.numpy as jnp
from jax import lax
from jax.experimental import pallas as pl
from jax.experimental.pallas import tpu as pltpu

F32 = jnp.float32
BF16 = jnp.bfloat16
HIGHEST = lax.Precision.HIGHEST

D = 1024
GROUP = 1024
N_PROMPT_GROUPS = 8
HY_W = 512
ML_HEADS = 4
ML_HD = 128
ML_CHUNK = 256
EVEN_MAIN = 3 * HY_W + 4 * 512
N_GATES = 16
ATT_HD = 64
ATT_HEADS = 16
ATT_KV = 4
WINDOW = 128
GRID_W = 64
ROPE_BASE = 10000.0
D_FF = 2816
N_EXPERTS = 8
EPS = 1e-6
NEG = -1e30
VMEM_LIMIT = 56 * 1024 * 1024


def _cparams(*sem, flags=None):
    return pltpu.CompilerParams(dimension_semantics=sem, vmem_limit_bytes=VMEM_LIMIT, flags=flags)


def _mod_row(i, tm):
    return jnp.maximum(i * tm // GROUP - (N_PROMPT_GROUPS - 1), 0)


def _silu(x):
    return x * jax.nn.sigmoid(x)


def _bdot(a, b):
    return jnp.dot(a.astype(BF16), b.astype(BF16), preferred_element_type=F32)


def _norm_mod(x, g, sh, sc):
    y = x * lax.rsqrt(jnp.mean(x * x, axis=-1, keepdims=True) + EPS) * g
    return y * (1.0 + sc) + sh


def _adaln_kernel(c_ref, w_ref, b_ref, o_ref):
    s = _silu(c_ref[...])
    o_ref[...] = jnp.dot(s, w_ref[...], precision=HIGHEST, preferred_element_type=F32) + b_ref[...]


def adaln_table(cond, w_ada, b_ada):
    depth = w_ada.shape[0]
    tn = 1536
    out = pl.pallas_call(
        _adaln_kernel,
        grid=(depth, 6 * D // tn),
        in_specs=[pl.BlockSpec((16, D), lambda l, j: (0, 0)),
                  pl.BlockSpec((None, D, tn), lambda l, j: (l, 0, j)),
                  pl.BlockSpec((None, 1, tn), lambda l, j: (l, 0, j))],
        out_specs=pl.BlockSpec((None, 16, tn), lambda l, j: (l, 0, j)),
        out_shape=jax.ShapeDtypeStruct((depth, 16, 6 * D), F32),
        compiler_params=_cparams("parallel", "parallel"),
        name="adaln",
    )(cond, w_ada, b_ada.reshape(depth, 1, 6 * D))
    return out.reshape(depth, 16, 1, 6 * D)


def _part_specs(block, n_p):
    return [pl.BlockSpec(block, lambda i, *_: (jnp.minimum(i, n_p - 1), 0)),
            pl.BlockSpec(block, lambda i, *_: (jnp.maximum(i - n_p, 0), 0))]


def _pick(is_prompt, p_ref, s_ref):
    return jnp.where(is_prompt, p_ref[...], s_ref[...])


def _mod_spec(k, tm):
    return pl.BlockSpec((None, 1, D), lambda i, *_: (_mod_row(i, tm), 0, k))


def _log_sigmoid(x):
    return jnp.minimum(x, 0.0) - jnp.log(1.0 + jnp.exp(-jnp.abs(x)))


def _split3(x):
    hi = x.astype(BF16)
    r = x - hi.astype(F32)
    mid = r.astype(BF16)
    return hi, mid, (r - mid.astype(F32)).astype(BF16)


def _even_in_kernel(xp_ref, xs_ref, g_ref, sh_ref, sc_ref, w_ref, wg_ref, bg_ref, lo_ref, up_ref, u_ref, gate_ref,
                    h_scr, *, n_p):
    is_prompt = pl.program_id(0) < n_p

    @pl.when(pl.program_id(1) == 0)
    def _():
        h = _norm_mod(_pick(is_prompt, xp_ref, xs_ref), g_ref[...], sh_ref[...], sc_ref[...]).astype(BF16)
        h_scr[...] = h
        gates = _bdot(h, wg_ref[...]) + bg_ref[...]
        lf = _log_sigmoid(gates)
        col = lax.broadcasted_iota(jnp.int32, (1, N_GATES), 1)
        is_forget = (col // ML_HEADS) % 2 == 1
        is_rev = col >= N_GATES // 2
        for ch in range(h.shape[0] // ML_CHUNK):
            sl = slice(ch * ML_CHUNK, (ch + 1) * ML_CHUNK)
            parts = _split3(lf[sl])
            cf = sum(jnp.dot(lo_ref[...], p, preferred_element_type=F32) for p in parts)
            cr = sum(jnp.dot(up_ref[...], p, preferred_element_type=F32) for p in parts)
            gate_ref[sl, :] = jnp.where(is_forget, jnp.where(is_rev, cr, cf), gates[sl])

    u_ref[...] = _bdot(h_scr[...], w_ref[...]).astype(u_ref.dtype)


def even_in_proj(xp, xs, g, mods, w_in, b_gate, tm=1024, tn=512):
    T = xp.shape[0] + xs.shape[0]
    tri = np.tril(np.ones((ML_CHUNK, ML_CHUNK), np.float32))
    lo, up = jnp.asarray(tri).astype(BF16), jnp.asarray(tri.T).astype(BF16)
    return pl.pallas_call(
        functools.partial(_even_in_kernel, n_p=xp.shape[0] // tm),
        grid=(T // tm, EVEN_MAIN // tn),
        in_specs=_part_specs((tm, D), xp.shape[0] // tm) + [
                  pl.BlockSpec((1, D), lambda i, j: (0, 0)),
                  _mod_spec(0, tm), _mod_spec(1, tm),
                  pl.BlockSpec((D, tn), lambda i, j: (0, j)),
                  pl.BlockSpec((D, N_GATES), lambda i, j: (0, 0)),
                  pl.BlockSpec((1, N_GATES), lambda i, j: (0, 0)),
                  pl.BlockSpec((ML_CHUNK, ML_CHUNK), lambda i, j: (0, 0)),
                  pl.BlockSpec((ML_CHUNK, ML_CHUNK), lambda i, j: (0, 0))],
        out_specs=[pl.BlockSpec((tm, tn), lambda i, j: (i, j)),
                   pl.BlockSpec((tm, N_GATES), lambda i, j: (i, 0))],
        out_shape=[jax.ShapeDtypeStruct((T, EVEN_MAIN), BF16),
                   jax.ShapeDtypeStruct((T, N_GATES), F32)],
        scratch_shapes=[pltpu.VMEM((tm, D), BF16)],
        compiler_params=_cparams("parallel", "arbitrary"),
        name="even_in_proj",
    )(xp, xs, g, mods, mods, w_in, w_in[:, EVEN_MAIN:], b_gate, lo, up)


def _dft_tables(L):
    n = 2 * L
    f = np.arange(L, dtype=np.int64)[:, None]
    s = np.arange(L, dtype=np.int64)[None, :]
    ang = 2.0 * np.pi * ((f * s) % n).astype(np.float64) / n
    fwd = np.concatenate([np.cos(ang), -np.sin(ang)], axis=0)
    fwd[L, :] = np.where(np.arange(L) % 2 == 0, 1.0, -1.0)
    t = np.arange(L, dtype=np.int64)[:, None]
    ff = np.arange(L, dtype=np.int64)[None, :]
    ang = 2.0 * np.pi * ((t * ff) % n).astype(np.float64) / n
    inv_re = 2.0 * np.cos(ang) / n
    inv_re[:, 0] = 1.0 / n
    inv_im = -2.0 * np.sin(ang) / n
    inv_im[:, 0] = np.where(np.arange(L) % 2 == 0, 1.0, -1.0) / n
    inv = np.concatenate([inv_re, inv_im], axis=1)
    return fwd.astype(np.float32), inv.astype(np.float32)


def _filter_tables(L):
    t = np.linspace(0.0, 1.0, L, dtype=np.float32).astype(np.float64)[:, None]
    w = 2.0 * math.pi * np.arange(L, dtype=np.float64)[:, None] / L
    bands = np.linspace(1e-4, 16 - 1, 16, dtype=np.float32).astype(np.float64)[None, :]
    z = np.concatenate([t, np.cos(bands * w), -np.sin(bands * w)], axis=-1)
    zp = np.zeros((L, 128), np.float64)
    zp[:, :z.shape[1]] = z
    max_decay = math.log(1e-2) / 0.3
    min_decay = math.log(1e-2) / 1.5
    deltas = np.linspace(min_decay, max_decay, HY_W, dtype=np.float32).astype(np.float64)
    decay = np.exp(-t * np.abs(deltas))
    return zp.astype(np.float32), decay.astype(np.float32)


def _hy_filter_kernel(z_ref, dec_ref, w1_ref, b1_ref, w2_ref, b2_ref, w3_ref, fr_ref, fwd_ref,
                      ka_ref, kb_ref):
    L = z_ref.shape[0]
    hdot = functools.partial(jnp.dot, precision=HIGHEST, preferred_element_type=F32)
    h = jnp.sin(fr_ref[0:1, :] * (hdot(z_ref[...], w1_ref[...]) + b1_ref[...]))
    h = jnp.sin(fr_ref[1:2, :] * (hdot(h, w2_ref[...]) + b2_ref[...]))
    h = hdot(h, w3_ref[...])
    row0 = lax.broadcasted_iota(jnp.int32, (L, 1), 0) == 0
    h0 = h[:, :HY_W] * dec_ref[...]
    h1 = h[:, HY_W:] * dec_ref[...]
    l1 = jnp.sum(jnp.abs(h0), axis=0, keepdims=True) + jnp.sum(jnp.abs(h1), axis=0, keepdims=True)
    inv = 1.0 / l1
    h0 = h0 * inv
    h1 = jnp.where(row0, 0.0, h1 * inv)
    f0 = _bdot(fwd_ref[...], h0)
    f1 = _bdot(fwd_ref[...], h1)
    ka_ref[...] = f0[:L] + f1[:L]
    kb_ref[...] = jnp.where(row0, f0[L:] + f1[L:], f0[L:] - f1[L:])


def _const_spec(a, n_grid):
    return pl.BlockSpec(a.shape, lambda *_: (0,) * a.ndim, pipeline_mode=pl.Buffered(1))


def hyena_filter_spectra(L, w1, b1, w2, b2, w3, freq, fwd):
    z, dec = _filter_tables(L)
    pad2 = lambda a, r, c: jnp.pad(a, ((0, r - a.shape[0]), (0, c - a.shape[1])))
    args = (jnp.asarray(z), jnp.asarray(dec), pad2(w1, 128, 128), pad2(b1[None], 1, 128),
            pad2(w2, 128, 128), pad2(b2[None], 1, 128), pad2(w3, 128, 4 * HY_W), pad2(freq, 2, 128), fwd)
    in_specs = [_const_spec(a, 1) for a in args]
    in_specs[6] = pl.BlockSpec((128, 2 * HY_W), lambda o: (0, o))
    shp = jax.ShapeDtypeStruct((2, L, HY_W), F32)
    out_spec = pl.BlockSpec((None, L, HY_W), lambda o: (o, 0, 0))
    return pl.pallas_call(
        _hy_filter_kernel,
        grid=(2,),
        in_specs=in_specs,
        out_specs=[out_spec, out_spec],
        out_shape=[shp, shp],
        compiler_params=_cparams("arbitrary"),
        name=f"hyena_filter_{L}",
    )(*args)


def _hyena_kernel(u_ref, cw_ref, d_ref, fwd_ref, inv_ref, ka_ref, kb_ref, o_ref):
    nb, L = u_ref.shape[0], u_ref.shape[1]
    row = lax.broadcasted_iota(jnp.int32, (L, 1), 0)
    first, last = row == 0, row == L - 1
    fwd = fwd_ref[...].astype(BF16)
    inv = inv_ref[...].astype(BF16)

    def long_conv(z, o):
        zf = jnp.dot(fwd, z.astype(BF16), preferred_element_type=F32)
        a, b = zf[:L], zf[L:]
        ka, kb = ka_ref[o], kb_ref[o]
        yr = a * ka - jnp.where(first, 0.0, b * kb)
        yi = jnp.where(first, b * kb, a * kb + b * ka)
        return (jnp.dot(inv[:, :L], yr.astype(BF16), preferred_element_type=F32)
                + jnp.dot(inv[:, L:], yi.astype(BF16), preferred_element_type=F32))

    for bi in range(nb):
        u = u_ref[bi].astype(F32)
        prev = jnp.where(first, 0.0, pltpu.roll(u, 1, 0))
        nxt = jnp.where(last, 0.0, pltpu.roll(u, L - 1, 0))
        u = prev * cw_ref[0:1, :] + u * cw_ref[1:2, :] + nxt * cw_ref[2:3, :]
        v, x1, x2 = u[:, :HY_W], u[:, HY_W:2 * HY_W], u[:, 2 * HY_W:]
        z = x1 * (long_conv(v, 0) + d_ref[0:1, :] * v)
        z = x2 * (long_conv(z, 1) + d_ref[1:2, :] * z)
        o_ref[bi] = z.astype(o_ref.dtype)


def hyena_mix(u, seq0, B, L, conv_w, d_skip, fwd, inv, ka, kb, nb):
    u3 = u.reshape(-1, L, EVEN_MAIN)
    full = lambda a: _const_spec(a, 1)
    out = pl.pallas_call(
        _hyena_kernel,
        grid=(B // nb,),
        in_specs=[pl.BlockSpec((nb, L, 3 * HY_W), lambda b: (b + seq0 // nb, 0, 0)),
                  full(conv_w), full(d_skip), full(fwd), full(inv), full(ka), full(kb)],
        out_specs=pl.BlockSpec((nb, L, HY_W), lambda b: (b, 0, 0)),
        out_shape=jax.ShapeDtypeStruct((B, L, HY_W), BF16),
        compiler_params=_cparams("parallel"),
        name=f"hyena_{L}",
    )(u3, conv_w, d_skip, fwd, inv, ka, kb)
    return out.reshape(B * L, HY_W)


def _mlstm_kernel(*refs, has_state, want_state):
    q_ref, k_ref, v_ref, o_ref, gc_ref, gr_ref, ng_ref = refs[:7]
    refs = refs[7:]
    if has_state:
        c0t_ref, n0b_ref, m0_ref = refs[:3]
        refs = refs[3:]
    y_ref = refs[0]
    if want_state:
        c_out, n_out, m_out = refs[1:4]
    L, d = q_ref.shape[0], ML_HD
    T = min(ML_CHUNK, L)
    nc = L // T
    scale = 1.0 / math.sqrt(d)
    nt = (((1,), (1,)), ((), ()))
    si = lax.broadcasted_iota(jnp.int32, (T, T), 0)
    ti = lax.broadcasted_iota(jnp.int32, (T, T), 1)
    allowed = (si <= ti, si >= ti)
    chains = [(dr, h) for dr in range(2) for h in range(ML_HEADS)]
    gcol = lambda dr, gi, h: dr * 2 * ML_HEADS + gi * ML_HEADS + h

    caug_t, m = {}, {}
    for ch in chains:
        dr, h = ch
        if has_state:
            caug_t[ch] = jnp.concatenate([c0t_ref[dr, h], n0b_ref[dr, h]], axis=0)
            m[ch] = m0_ref[dr, h:h + 1, 0:1]
        else:
            caug_t[ch], m[ch] = jnp.zeros((2 * d, d), F32), jnp.zeros((1, 1), F32)

    chunk_cache = {}

    def chunk_data(h, j):
        if (h, j) not in chunk_cache:
            sl, hl = slice(j * T, (j + 1) * T), slice(h * d, (h + 1) * d)
            q = q_ref[sl, hl]
            ks = (k_ref[sl, hl].astype(F32) * scale).astype(BF16)
            v_t = v_ref[sl, hl].astype(F32).T
            vaug_t = jnp.concatenate([v_t, jnp.ones((d, T), F32)], axis=0).astype(BF16)
            s_raw = lax.dot_general(ks, q, nt, preferred_element_type=F32)
            chunk_cache[(h, j)] = (q, ks, v_t, vaug_t, s_raw)
        return chunk_cache[(h, j)]

    h_sum = {}
    for it in range(nc):
        step = {ch: (it if ch[0] == 0 else nc - 1 - it) for ch in chains}
        data = {ch: chunk_data(ch[1], step[ch]) for ch in chains}
        inter_t = {ch: lax.dot_general(caug_t[ch].astype(BF16), data[ch][0], nt, preferred_element_type=F32)
                   for ch in chains}
        gate = {}
        for ch in chains:
            dr, h = ch
            sl = slice(step[ch] * T, (step[ch] + 1) * T)
            li_r, b_r = gr_ref[gcol(dr, 0, h):gcol(dr, 0, h) + 1, sl], gr_ref[gcol(dr, 1, h):gcol(dr, 1, h) + 1, sl]
            src = gc_ref[sl, gcol(dr, 0, h):gcol(dr, 0, h) + 1] - gc_ref[sl, gcol(dr, 1, h):gcol(dr, 1, h) + 1]
            dm = jnp.where(allowed[dr], src + b_r, NEG)
            inter = b_r + m[ch]
            m_t = jnp.maximum(inter, jnp.max(dm, axis=0, keepdims=True))
            b_end = b_r[:, T - 1:T] if dr == 0 else b_r[:, 0:1]
            g_r = b_end - b_r + li_r
            m_new = jnp.maximum(b_end + m[ch], jnp.max(g_r, axis=1, keepdims=True))
            gate[ch] = (jnp.exp(dm - m_t), jnp.exp(inter - m_t), jnp.exp(-m_t), jnp.exp(g_r - m_new),
                        jnp.exp(b_end + m[ch] - m_new), m_new)
        for ch in chains:
            q, ks, v_t, vaug_t, s_raw = data[ch]
            w_intra, w_inter, floor, w_tok, decay, m_new = gate[ch]
            acc = jnp.dot(vaug_t, (s_raw * w_intra).astype(BF16), preferred_element_type=F32) + w_inter * inter_t[ch]
            h_t = acc[:d] / jnp.maximum(jnp.abs(acc[d:]), floor)
            key = (ch[1], step[ch])
            h_sum[key] = h_t if key not in h_sum else h_sum[key] + h_t
            vw_t = jnp.concatenate([v_t * w_tok, jnp.broadcast_to(w_tok, (d, T))], axis=0).astype(BF16)
            caug_t[ch] = decay * caug_t[ch] + jnp.dot(vw_t, ks, preferred_element_type=F32)
            m[ch] = m_new

    if want_state:
        for ch in chains:
            dr, h = ch
            c_out[dr, h] = caug_t[ch][:d].T
            n_out[dr, h:h + 1, :] = caug_t[ch][d:d + 1, :]
            m_out[dr, h:h + 1, :] = jnp.broadcast_to(m[ch], (1, d))
    for h in range(ML_HEADS):
        for j in range(nc):
            sl, hl = slice(j * T, (j + 1) * T), slice(h * d, (h + 1) * d)
            hv = h_sum[(h, j)].T
            y = hv * lax.rsqrt(jnp.mean(hv * hv, axis=-1, keepdims=True) + EPS) * ng_ref[:, hl]
            y_ref[sl, hl] = (y * jax.nn.sigmoid(o_ref[sl, hl].astype(F32))).astype(y_ref.dtype)


def mlstm_mix(u, gates, seq0, B, L, norm_g, state=None, want_state=False):
    u3 = u.reshape(-1, L, EVEN_MAIN)
    gc = gates.reshape(-1, L, N_GATES)[seq0:seq0 + B]
    gr = gc.transpose(0, 2, 1)
    width = ML_HEADS * ML_HD
    col = lambda i: pl.BlockSpec((None, L, width), lambda b: (b + seq0, 0, (3 * HY_W + i * width) // width))
    in_specs = [col(0), col(1), col(2), col(3),
                pl.BlockSpec((None, L, N_GATES), lambda b: (b, 0, 0)),
                pl.BlockSpec((None, N_GATES, L), lambda b: (b, 0, 0)),
                pl.BlockSpec((1, width), lambda b: (0, 0))]
    args = [u3, u3, u3, u3, gc, gr, norm_g.reshape(1, width)]
    sspec = pl.BlockSpec((None, 2, ML_HEADS, ML_HD, ML_HD), lambda b: (b, 0, 0, 0, 0))
    vspec = pl.BlockSpec((None, 2, ML_HEADS, ML_HD), lambda b: (b, 0, 0, 0))
    if state is not None:
        C0, n0, m0 = state
        in_specs += [sspec, sspec, vspec]
        args += [C0.swapaxes(-1, -2), jnp.broadcast_to(n0[..., None, :], C0.shape),
                 jnp.broadcast_to(m0[..., None], n0.shape)]
    out_specs = [pl.BlockSpec((None, L, width), lambda b: (b, 0, 0))]
    out_shape = [jax.ShapeDtypeStruct((B, L, width), BF16)]
    if want_state:
        out_specs += [sspec, vspec, vspec]
        out_shape += [jax.ShapeDtypeStruct((B, 2, ML_HEADS, ML_HD, ML_HD), F32),
                      jax.ShapeDtypeStruct((B, 2, ML_HEADS, ML_HD), F32),
                      jax.ShapeDtypeStruct((B, 2, ML_HEADS, ML_HD), F32)]
    outs = pl.pallas_call(
        functools.partial(_mlstm_kernel, has_state=state is not None, want_state=want_state),
        grid=(B,),
        in_specs=in_specs, out_specs=out_specs, out_shape=out_shape,
        compiler_params=_cparams("parallel"),
        name=f"mlstm_{L}",
    )(*args)
    y = outs[0].reshape(B * L, width)
    if not want_state:
        return y
    _, C, n, m = outs
    return y, C, n, m[..., 0]


def _proj_res_kernel(*refs, n_in, n_x, n_p):
    a_refs = refs[:2 * n_in]
    w_ref = refs[2 * n_in]
    x_refs = refs[2 * n_in + 1:2 * n_in + 1 + n_x]
    gate_ref, o_ref = refs[2 * n_in + 1 + n_x:]
    is_prompt = pl.program_id(0) < n_p
    k0 = 0
    acc = None
    for j in range(n_in):
        a = _pick(is_prompt, a_refs[2 * j], a_refs[2 * j + 1])
        kw = a.shape[1]
        part = _bdot(a, w_ref[k0:k0 + kw, :])
        acc = part if acc is None else acc + part
        k0 += kw
    x = _pick(is_prompt, *x_refs) if n_x == 2 else x_refs[0][...]
    o_ref[...] = x + gate_ref[...] * acc


def proj_residual(acts, w, x, mods, gate_idx, tm=1024):
    xs = tuple(x) if isinstance(x, (tuple, list)) else (x,)
    T = sum(a.shape[0] for a in acts[0])
    n_p = acts[0][0].shape[0] // tm
    in_specs = []
    for pair in acts:
        in_specs += _part_specs((tm, pair[0].shape[1]), n_p)
    in_specs.append(pl.BlockSpec(w.shape, lambda i: (0, 0)))
    in_specs += _part_specs((tm, D), n_p) if len(xs) == 2 else [pl.BlockSpec((tm, D), lambda i: (i, 0))]
    in_specs.append(_mod_spec(gate_idx, tm))
    return pl.pallas_call(
        functools.partial(_proj_res_kernel, n_in=len(acts), n_x=len(xs), n_p=n_p),
        grid=(T // tm,),
        in_specs=in_specs,
        out_specs=pl.BlockSpec((tm, D), lambda i: (i, 0)),
        out_shape=jax.ShapeDtypeStruct((T, D), F32),
        compiler_params=_cparams("parallel"),
        name="proj_residual",
    )(*[a for pair in acts for a in pair], w, *xs, mods)


SWIGLU_ROWS = 512


def _swiglu_accumulate(h_scr, acc_scr, w1_ref, w3_ref, w2_ref, rows):
    w1, w3, w2 = w1_ref[...].astype(BF16), w3_ref[...].astype(BF16), w2_ref[...].astype(BF16)
    groups = [slice(r, r + SWIGLU_ROWS) for r in range(0, rows, SWIGLU_ROWS)]
    ups = []
    for sl in groups:
        h = h_scr[sl, :]
        ups.append((jnp.dot(h, w1, preferred_element_type=F32), jnp.dot(h, w3, preferred_element_type=F32)))
    for sl, (a, b) in zip(groups, ups):
        mid = (_silu(a) * b).astype(BF16)
        acc_scr[sl, :] += jnp.dot(mid, w2, preferred_element_type=F32)


def _ffn_kernel(x_ref, g_ref, sh_ref, sc_ref, gate_ref, w1_ref, w3_ref, w2_ref, o_ref, h_scr, acc_scr):
    c = pl.program_id(1)

    @pl.when(c == 0)
    def _():
        h_scr[...] = _norm_mod(x_ref[...], g_ref[...], sh_ref[...], sc_ref[...]).astype(BF16)
        acc_scr[...] = jnp.zeros_like(acc_scr)

    _swiglu_accumulate(h_scr, acc_scr, w1_ref, w3_ref, w2_ref, h_scr.shape[0])

    @pl.when(c == pl.num_programs(1) - 1)
    def _():
        o_ref[...] = x_ref[...] + gate_ref[...] * acc_scr[...]


def ffn_residual(x, g, mods, w1, w3, w2, tm=1024, tf=256):
    T = x.shape[0]
    return pl.pallas_call(
        _ffn_kernel,
        grid=(T // tm, D_FF // tf),
        in_specs=[pl.BlockSpec((tm, D), lambda i, c: (i, 0)),
                  pl.BlockSpec((1, D), lambda i, c: (0, 0)),
                  _mod_spec(3, tm), _mod_spec(4, tm), _mod_spec(5, tm),
                  pl.BlockSpec((D, tf), lambda i, c: (0, c)),
                  pl.BlockSpec((D, tf), lambda i, c: (0, c)),
                  pl.BlockSpec((tf, D), lambda i, c: (c, 0))],
        out_specs=pl.BlockSpec((tm, D), lambda i, c: (i, 0)),
        out_shape=jax.ShapeDtypeStruct((T, D), F32),
        scratch_shapes=[pltpu.VMEM((tm, D), BF16), pltpu.VMEM((tm, D), F32)],
        compiler_params=_cparams("parallel", "arbitrary"),
        name="ffn",
    )(x, g, mods, mods, mods, w1, w3, w2)


def _qkv_kernel(x_ref, g_ref, sh_ref, sc_ref, w_ref, q_ref, kv_ref, h_scr):
    j = pl.program_id(1)

    @pl.when(j == 0)
    def _():
        h_scr[...] = _norm_mod(x_ref[...], g_ref[...], sh_ref[...], sc_ref[...]).astype(BF16)

    @pl.when(j < 2)
    def _():
        q_ref[...] = _bdot(h_scr[...], w_ref[...]).astype(q_ref.dtype)

    @pl.when(j == 2)
    def _():
        kv_ref[...] = _bdot(h_scr[...], w_ref[...])


def qkv_proj(x, g, mods, w_qkv, tm=1024):
    T = x.shape[0]
    tn = 512
    return pl.pallas_call(
        _qkv_kernel,
        grid=(T // tm, 3),
        in_specs=[pl.BlockSpec((tm, D), lambda i, j: (i, 0)),
                  pl.BlockSpec((1, D), lambda i, j: (0, 0)),
                  _mod_spec(0, tm), _mod_spec(1, tm),
                  pl.BlockSpec((D, tn), lambda i, j: (0, j))],
        out_specs=[pl.BlockSpec((tm, tn), lambda i, j: (i, jnp.minimum(j, 1))),
                   pl.BlockSpec((tm, tn), lambda i, j: (i, 0))],
        out_shape=[jax.ShapeDtypeStruct((T, ATT_HEADS * ATT_HD), BF16),
                   jax.ShapeDtypeStruct((T, 2 * ATT_KV * ATT_HD), F32)],
        scratch_shapes=[pltpu.VMEM((tm, D), BF16)],
        compiler_params=_cparams("parallel", "arbitrary"),
        name="qkv_proj",
    )(x, g, mods, mods, w_qkv)


def _rope_tables(L):
    half = ATT_HD // 2
    pos_r = (np.arange(L) // GRID_W).astype(np.float32)
    pos_c = (np.arange(L) % GRID_W).astype(np.float32)
    inv = (ROPE_BASE ** (-np.arange(0, half, 2, dtype=np.float32) / half)).astype(np.float32)
    cos = np.zeros((L, ATT_HD), np.float64)
    sin = np.zeros((L, ATT_HD), np.float64)
    for base, pos in ((0, pos_r), (half, pos_c)):
        ang = (pos[:, None] * inv[None, :]).astype(np.float32).astype(np.float64)
        cos[:, base:base + half] = np.concatenate([np.cos(ang), np.cos(ang)], axis=1)
        sin[:, base:base + half] = np.concatenate([-np.sin(ang), np.sin(ang)], axis=1)
    return (np.tile(cos, (1, 4)).astype(np.float32), np.tile(sin, (1, 4)).astype(np.float32))


def _seg_rms(x):
    w = x.shape[1]
    ri = lax.broadcasted_iota(jnp.int32, (w, w), 0) // ATT_HD
    ci = lax.broadcasted_iota(jnp.int32, (w, w), 1) // ATT_HD
    ss = _bdot(x * x, (ri == ci).astype(F32))
    return x * lax.rsqrt(ss * (1.0 / ATT_HD) + EPS)


LOG2E = 1.4426950408889634


def _exp2_bf16(x):
    return jnp.exp2(x.astype(BF16))


def _both_halves(tile, low):
    lane = lax.broadcasted_iota(jnp.int32, tile.shape, 1)
    other = pltpu.roll(tile, ATT_HD, 1)
    return jnp.where((lane < ATT_HD) == low, tile, other)


def _swap16(x):
    w = x.shape[1]
    lane = lax.broadcasted_iota(jnp.int32, x.shape, 1)
    return jnp.where(lane % 32 < 16, pltpu.roll(x, w - 16, 1), pltpu.roll(x, 16, 1))


def _attn_kernel(*refs, latent, tq):
    if latent:
        (q_ref, kv_ref, ck_ref, cv_ref, qg_ref, kg_ref, sink_ref, cosq_ref, sinq_ref, cosk_ref, sink_t_ref,
         o_ref, kk_scr, vt_scr, ckk_scr, cvt_scr) = refs
    else:
        q_ref, kv_ref, qg_ref, kg_ref, sink_ref, o_ref, ko_ref, vo_ref, kk_scr, vt_scr = refs
    L = kv_ref.shape[0]
    gw = ATT_KV * ATT_HD
    pw = 2 * ATT_HD
    qb = pl.program_id(1)

    vrows = vt_scr.shape[2]
    nblk = L // pw

    def vt_aug(tile, low):
        vt = tile.T[0:ATT_HD, :] if low else tile.T[ATT_HD:, :]
        return jnp.concatenate([vt, jnp.ones((vrows - ATT_HD, tile.shape[0]), F32)], axis=0).astype(BF16)

    @pl.when(qb == 0)
    def _():
        kn = _seg_rms(kv_ref[:, :gw]) * kg_ref[...]
        v = kv_ref[:, gw:]
        if latent:
            kn = kn * cosk_ref[...] + _swap16(kn) * sink_t_ref[...]
        else:
            for c in range(ATT_KV):
                ko_ref[c] = kn[:, c * ATT_HD:(c + 1) * ATT_HD]
                vo_ref[c] = v[:, c * ATT_HD:(c + 1) * ATT_HD]
        for c in range(ATT_KV):
            tile, low = slice((c // 2) * pw, (c // 2 + 1) * pw), c % 2 == 0
            kk_scr[c] = _both_halves(kn[:, tile], low).astype(BF16)
            for j in range(nblk):
                vt_scr[c, j] = vt_aug(v[j * pw:(j + 1) * pw, tile], low)
            if latent:
                ck, cv = ck_ref[c], cv_ref[c]
                ckk_scr[c] = jnp.concatenate([ck, ck], axis=1).astype(BF16)
                cvt_scr[c] = vt_aug(jnp.concatenate([cv, cv], axis=1), True)

    if latent:
        span = tq + 2 * WINDOW
        start = pl.multiple_of(jnp.clip(qb * tq - WINDOW, 0, L - span), WINDOW)
        blk0 = start // pw
        s_pos = start + lax.broadcasted_iota(jnp.int32, (span, tq), 0)
        t_pos = qb * tq + lax.broadcasted_iota(jnp.int32, (span, tq), 1)
        win_bias = jnp.where(jnp.abs(t_pos - s_pos) <= WINDOW, 0.0, NEG)
    else:
        span, blk0 = L, 0

    nt = (((1,), (1,)), ((), ()))
    low_q = lax.broadcasted_iota(jnp.int32, (tq, pw), 1) < ATT_HD
    def group_scores(c):
        qc = _seg_rms(q_ref[:, c * gw:(c + 1) * gw].astype(F32)) * qg_ref[...]
        if latent:
            qc = qc * cosq_ref[...] + _swap16(qc) * sinq_ref[...]
            kw = kk_scr[c, pl.ds(start, span), :]
        else:
            kw = kk_scr[c]
        qc = qc * (LOG2E / math.sqrt(ATT_HD))
        scores = []
        for g in range(ATT_KV):
            qt = qc[:, (g // 2) * pw:(g // 2 + 1) * pw]
            qm = jnp.where(low_q if g % 2 == 0 else ~low_q, qt, 0.0).astype(BF16)
            lw = lax.dot_general(kw, qm, nt, preferred_element_type=F32)
            lc = lax.dot_general(ckk_scr[c], qm, nt, preferred_element_type=F32) if latent else None
            scores.append((lw, lc))
        return scores

    def group_outputs(c, scores):
        vw = jnp.concatenate([vt_scr[c, blk0 + j] for j in range(span // pw)], axis=1)
        outs = []
        for g in range(ATT_KV):
            head = c * ATT_KV + g
            sink = sink_ref[:, head:head + 1] * LOG2E
            lw, lc = scores[g]
            if latent:
                lw = lw + win_bias
                mx = jnp.maximum(jnp.maximum(jnp.max(lw, axis=0, keepdims=True),
                                             jnp.max(lc, axis=0, keepdims=True)), sink)
                r = jnp.dot(vw, _exp2_bf16(lw - mx), preferred_element_type=F32) + jnp.dot(
                    cvt_scr[c], _exp2_bf16(lc - mx), preferred_element_type=F32)
            else:
                mx = jnp.maximum(jnp.max(lw, axis=0, keepdims=True), sink)
                r = jnp.dot(vw, _exp2_bf16(lw - mx), preferred_element_type=F32)
            den = r[ATT_HD:ATT_HD + 1, :] + jnp.exp2(sink - mx)
            outs.append(r[0:ATT_HD, :] / den)
        for t in range(2):
            o_ref[:, c * gw + t * pw:c * gw + (t + 1) * pw] = (
                jnp.concatenate(outs[2 * t:2 * t + 2], axis=0).T.astype(o_ref.dtype))

    scores = group_scores(0)
    for c in range(ATT_KV):
        nxt = group_scores(c + 1) if c + 1 < ATT_KV else None
        group_outputs(c, scores)
        scores = nxt


def attention(q, kv, row0, q_g, k_g, sink, B, L, cache=None, tq=256):
    latent = cache is not None
    gw = ATT_KV * ATT_HD
    qg = jnp.tile(q_g, ATT_KV)[None]
    kg = jnp.tile(k_g, ATT_KV)[None]
    nq = L // tq
    const = lambda a: pl.BlockSpec(a.shape, lambda b, i: (0,) * a.ndim)
    in_specs = [pl.BlockSpec((tq, ATT_HEADS * ATT_HD), lambda b, i: (row0 // tq + b * nq + i, 0)),
                pl.BlockSpec((L, 2 * gw), lambda b, i: (row0 // L + b, 0))]
    args = [q, kv]
    vrows = ATT_HD + 16
    scratch = [pltpu.VMEM((ATT_KV, L, 2 * ATT_HD), BF16),
               pltpu.VMEM((ATT_KV, L // (2 * ATT_HD), vrows, 2 * ATT_HD), BF16)]
    out_specs = [pl.BlockSpec((tq, ATT_HEADS * ATT_HD), lambda b, i: (b * nq + i, 0))]
    out_shape = [jax.ShapeDtypeStruct((B * L, ATT_HEADS * ATT_HD), BF16)]
    if latent:
        ck, cv = cache
        P = ck.shape[2]
        cos, sin = (jnp.asarray(t) for t in _rope_tables(L))
        in_specs += [pl.BlockSpec((None, ATT_KV, P, ATT_HD), lambda b, i: (b, 0, 0, 0))] * 2
        args += [ck, cv]
        in_specs += [const(qg), const(kg), pl.BlockSpec((1, ATT_HEADS), lambda b, i: (0, 0)),
                     pl.BlockSpec((tq, gw), lambda b, i: (i, 0)), pl.BlockSpec((tq, gw), lambda b, i: (i, 0)),
                     const(cos), const(sin)]
        args += [qg, kg, sink[None], cos, sin, cos, sin]
        scratch += [pltpu.VMEM((ATT_KV, P, 2 * ATT_HD), BF16), pltpu.VMEM((ATT_KV, vrows, P), BF16)]
    else:
        in_specs += [const(qg), const(kg), pl.BlockSpec((1, ATT_HEADS), lambda b, i: (0, 0))]
        args += [qg, kg, sink[None]]
        cache_spec = pl.BlockSpec((None, ATT_KV, L, ATT_HD), lambda b, i: (b, 0, 0, 0))
        out_specs += [cache_spec, cache_spec]
        out_shape += [jax.ShapeDtypeStruct((B, ATT_KV, L, ATT_HD), F32)] * 2
    outs = pl.pallas_call(
        functools.partial(_attn_kernel, latent=latent, tq=tq),
        grid=(B, nq),
        in_specs=in_specs, out_specs=out_specs, out_shape=out_shape,
        scratch_shapes=scratch,
        compiler_params=_cparams("parallel", "arbitrary"),
        name="attn_latent" if latent else "attn_context",
    )(*args)
    return outs[0] if latent else outs


MOE_TM = 1024
MOE_TOK = 1024
RUN_ALIGN = 8
MOE_LOCAL = 2 * MOE_TOK + N_EXPERTS * RUN_ALIGN
MOE_MAX_TILES = (2 * 16384 + (16384 // MOE_TOK) * N_EXPERTS * (RUN_ALIGN - 1)) // MOE_TM + N_EXPERTS + 1
RUN_SIZES = tuple(RUN_ALIGN << b for b in range(8, -1, -1))


def _router_kernel(x_ref, g_ref, sh_ref, sc_ref, wr_ref, br_ref, tri_ref, lp_ref, wts_ref, runs_ref, cnt_ref):
    @pl.when(pl.program_id(0) == 0)
    def _():
        cnt_ref[...] = jnp.zeros_like(cnt_ref)

    h = _norm_mod(x_ref[...], g_ref[...], sh_ref[...], sc_ref[...])
    lg = lax.dot_general(wr_ref[...], h, (((1,), (1,)), ((), ())), precision=HIGHEST,
                         preferred_element_type=F32) + br_ref[...]
    row = lax.broadcasted_iota(jnp.int32, lg.shape, 0)
    m1 = jnp.max(lg, axis=0, keepdims=True)
    i1 = jnp.min(jnp.where(lg == m1, row, N_EXPERTS), axis=0, keepdims=True)
    l2 = jnp.where(row == i1, -jnp.inf, lg)
    m2 = jnp.max(l2, axis=0, keepdims=True)
    i2 = jnp.min(jnp.where(l2 == m2, row, N_EXPERTS), axis=0, keepdims=True)
    e2 = jnp.exp(m2 - m1)
    w1 = 1.0 / (1.0 + e2)
    wts_ref[...] = jnp.concatenate([w1, e2 * w1], axis=0)
    oh1 = (row == i1).astype(F32)
    oh2 = (row == i2).astype(F32)
    cs1 = _bdot(oh1, tri_ref[...])
    cs2 = _bdot(oh2, tri_ref[...])
    tot1 = jnp.sum(oh1, axis=1, keepdims=True)
    run = jnp.ceil((tot1 + jnp.sum(oh2, axis=1, keepdims=True)) * (1.0 / RUN_ALIGN)) * RUN_ALIGN
    run_b = jnp.broadcast_to(run, (N_EXPERTS, 128))
    er = lax.broadcasted_iota(jnp.int32, (N_EXPERTS, N_EXPERTS), 0)
    ec = lax.broadcasted_iota(jnp.int32, (N_EXPERTS, N_EXPERTS), 1)
    start = jnp.dot((ec < er).astype(F32), run_b, precision=HIGHEST, preferred_element_type=F32)
    last = lax.broadcasted_iota(jnp.int32, (N_EXPERTS, 128), 0) == N_EXPERTS - 1
    run_b = jnp.where(last, MOE_LOCAL - start, run_b)
    st = start[:, 0:1]
    p1 = jnp.sum(oh1 * (st + cs1), axis=0, keepdims=True)
    p2 = jnp.sum(oh2 * (st + tot1 + cs2), axis=0, keepdims=True)
    lp_ref[...] = jnp.concatenate([p1, p2], axis=0).astype(jnp.int32)
    lane = lax.broadcasted_iota(jnp.int32, (N_EXPERTS, 128), 1)
    runs_ref[...] = jnp.where(lane == 0, run_b, jnp.where(lane == 1, start, cnt_ref[...]))
    cnt_ref[...] = cnt_ref[...] + run_b


def moe_router(x, g, mods, w_router, b_router, tm=MOE_TOK):
    T = x.shape[0]
    tri = jnp.asarray(np.triu(np.ones((tm, tm), np.float32), k=1)).astype(BF16)
    tok2 = lambda dt: jax.ShapeDtypeStruct((2, T), dt)
    return pl.pallas_call(
        _router_kernel,
        grid=(T // tm,),
        in_specs=[pl.BlockSpec((tm, D), lambda i: (i, 0)),
                  pl.BlockSpec((1, D), lambda i: (0, 0)),
                  _mod_spec(3, tm), _mod_spec(4, tm),
                  pl.BlockSpec((N_EXPERTS, D), lambda i: (0, 0)),
                  pl.BlockSpec((N_EXPERTS, 1), lambda i: (0, 0)),
                  _const_spec(tri, 1)],
        out_specs=[pl.BlockSpec((2, tm), lambda i: (0, i)),
                   pl.BlockSpec((2, tm), lambda i: (0, i)),
                   pl.BlockSpec((None, N_EXPERTS, 128), lambda i: (i, 0, 0)),
                   pl.BlockSpec((N_EXPERTS, 128), lambda i: (0, 0))],
        out_shape=[tok2(jnp.int32), tok2(F32),
                   jax.ShapeDtypeStruct((T // tm, N_EXPERTS, 128), F32),
                   jax.ShapeDtypeStruct((N_EXPERTS, 128), F32)],
        compiler_params=_cparams("arbitrary"),
        name="moe_router",
    )(x, g, mods, mods, w_router.T, b_router[:, None], tri)


def moe_layout(runs, totals):
    rows = totals[:, 0].astype(jnp.int32)
    tiles = (rows + MOE_TM - 1) // MOE_TM
    tile_end = jnp.cumsum(tiles)
    group = (tile_end - tiles) * MOE_TM
    run_len = runs[:, :, 0].astype(jnp.int32)
    run_src = runs[:, :, 1].astype(jnp.int32)
    run_dst = group[None, :] + runs[:, :, 2].astype(jnp.int32)
    tail = jnp.stack([group + rows, tiles * MOE_TM - rows]).astype(jnp.int32)
    n_tiles = tile_end[-1]
    t = jnp.arange(MOE_MAX_TILES, dtype=jnp.int32)
    tile_e = jnp.sum(t[:, None] >= tile_end[None, :], axis=1).astype(jnp.int32)
    last_e = jnp.sum((n_tiles - 1) >= tile_end).astype(jnp.int32)
    tile_e = jnp.where(t < n_tiles, tile_e, last_e)
    first = jnp.sum(jnp.where(tile_e[:, None] == jnp.arange(N_EXPERTS), (tile_end - tiles)[None, :], 0), axis=1)
    e_rows = jnp.sum(jnp.where(tile_e[:, None] == jnp.arange(N_EXPERTS), rows[None, :], 0), axis=1)
    tile_rows = jnp.where(t < n_tiles, jnp.clip(e_rows - (t - first) * MOE_TM, 0, MOE_TM), 0).astype(jnp.int32)
    run_tab = jnp.stack([run_len, run_src, run_dst]).reshape(3, -1)
    return run_tab, tail, tile_e, n_tiles.astype(jnp.int32).reshape(1), tile_rows


def _run_copies(tab_ref, i, local_ref, global_ref, sem, to_global):
    out = []
    for e in range(N_EXPERTS):
        k = i * N_EXPERTS + e
        n, src, dst = tab_ref[0, k], tab_ref[1, k], tab_ref[2, k]
        for size in RUN_SIZES:
            done = (n // (2 * size)) * (2 * size)
            loc = local_ref.at[pl.ds(pl.multiple_of(src + done, RUN_ALIGN), size), :]
            glo = global_ref.at[pl.ds(pl.multiple_of(dst + done, RUN_ALIGN), size), :]
            copy = pltpu.make_async_copy(loc, glo, sem) if to_global else pltpu.make_async_copy(glo, loc, sem)
            out.append(((n & size) != 0, copy))
    return out


def _start_then_wait(copies):
    for pred, copy in copies:
        pl.when(pred)(copy.start)
    for pred, copy in copies:
        pl.when(pred)(copy.wait)


def _dispatch_kernel(tab_ref, tail_ref, nt_ref, lp_ref, x_ref, g_ref, sh_ref, sc_ref, xs_ref, hs_scr, z_scr, sem):
    i = pl.program_id(0)
    tm = x_ref.shape[0]
    h = _norm_mod(x_ref[...], g_ref[...], sh_ref[...], sc_ref[...]).astype(BF16)
    slot = lax.broadcasted_iota(jnp.int32, (MOE_LOCAL, tm), 0)
    perm = jnp.where((slot == lp_ref[0:1, :]) | (slot == lp_ref[1:2, :]), 1.0, 0.0).astype(BF16)
    hs_scr[...] = jnp.dot(perm, h, preferred_element_type=F32)
    copies = _run_copies(tab_ref, i, hs_scr, xs_ref, sem.at[0], to_global=True)

    @pl.when(i == 0)
    def _():
        z_scr[...] = jnp.zeros_like(z_scr)
        zrows = z_scr.shape[0]

        def zero_tile(t, carry):
            for part in range(MOE_TM // zrows):
                dst = xs_ref.at[pl.ds(pl.multiple_of(t * MOE_TM + part * zrows, zrows), zrows), :]
                copy = pltpu.make_async_copy(z_scr, dst, sem.at[1])
                copy.start()
                copy.wait()
            return carry

        lax.fori_loop(nt_ref[0], MOE_MAX_TILES, zero_tile, 0)
        tails = []
        for e in range(N_EXPERTS):
            start, n = tail_ref[0, e], tail_ref[1, e]
            for size in RUN_SIZES:
                if size >= MOE_TM:
                    continue
                done = (n // (2 * size)) * (2 * size)
                dst = xs_ref.at[pl.ds(pl.multiple_of(start + done, RUN_ALIGN), size), :]
                tails.append(((n & size) != 0, pltpu.make_async_copy(z_scr.at[pl.ds(0, size), :], dst, sem.at[1])))
        _start_then_wait(tails)

    _start_then_wait(copies)


def moe_dispatch(x, g, mods, lp, run_tab, tail, n_tiles, tm=MOE_TOK):
    T = x.shape[0]
    n_rows = MOE_MAX_TILES * MOE_TM
    return pl.pallas_call(
        _dispatch_kernel,
        grid_spec=pltpu.PrefetchScalarGridSpec(
            num_scalar_prefetch=3,
            grid=(T // tm,),
            in_specs=[pl.BlockSpec((2, tm), lambda i, *_: (0, i)),
                      pl.BlockSpec((tm, D), lambda i, *_: (i, 0)),
                      pl.BlockSpec((1, D), lambda i, *_: (0, 0)),
                      _mod_spec(3, tm), _mod_spec(4, tm)],
            out_specs=pl.BlockSpec(memory_space=pl.ANY),
            scratch_shapes=[pltpu.VMEM((MOE_LOCAL, D), F32), pltpu.VMEM((MOE_TM // 2, D), F32),
                            pltpu.SemaphoreType.DMA((2,))]),
        out_shape=jax.ShapeDtypeStruct((n_rows, D), F32),
        compiler_params=_cparams("arbitrary"),
        name="moe_dispatch",
    )(run_tab, tail, n_tiles, lp, x, g, mods, mods)


def _moe_group_kernel(te_ref, nt_ref, tr_ref, x_ref, w1_ref, w3_ref, w2_ref, o_ref, h_scr, acc_scr):
    t, c = pl.program_id(0), pl.program_id(1)
    rows = tr_ref[t]
    half = MOE_TM // 2

    @pl.when(t < nt_ref[0])
    def _():
        @pl.when(c == 0)
        def _():
            h_scr[...] = x_ref[...].astype(BF16)
            acc_scr[...] = jnp.zeros_like(acc_scr)

        @pl.when(rows > half)
        def _():
            _swiglu_accumulate(h_scr, acc_scr, w1_ref, w3_ref, w2_ref, MOE_TM)

        @pl.when(rows <= half)
        def _():
            _swiglu_accumulate(h_scr, acc_scr, w1_ref, w3_ref, w2_ref, half)

        @pl.when(c == pl.num_programs(1) - 1)
        def _():
            o_ref[...] = acc_scr[...]

    @pl.when((t >= nt_ref[0]) & (c == pl.num_programs(1) - 1))
    def _():
        o_ref[...] = jnp.zeros_like(o_ref)


def moe_grouped_swiglu(xs, tile_e, n_tiles, tile_rows, w1, w3, w2, tf=256):
    nc = D_FF // tf
    live = lambda t, nt: t < nt[0]
    row = lambda t, c, te, nt, tr: (jnp.where(live(t, nt), t, nt[0] - 1), 0)
    wcol = lambda t, c, te, nt, tr: (te[t], 0, jnp.where(live(t, nt), c, nc - 1))
    wrow = lambda t, c, te, nt, tr: (te[t], jnp.where(live(t, nt), c, nc - 1), 0)
    return pl.pallas_call(
        _moe_group_kernel,
        grid_spec=pltpu.PrefetchScalarGridSpec(
            num_scalar_prefetch=3,
            grid=(MOE_MAX_TILES, nc),
            in_specs=[pl.BlockSpec((MOE_TM, D), row),
                      pl.BlockSpec((None, D, tf), wcol),
                      pl.BlockSpec((None, D, tf), wcol),
                      pl.BlockSpec((None, tf, D), wrow)],
            out_specs=pl.BlockSpec((MOE_TM, D), lambda t, c, te, nt, tr: (t, 0)),
            scratch_shapes=[pltpu.VMEM((MOE_TM, D), BF16), pltpu.VMEM((MOE_TM, D), F32)]),
        out_shape=jax.ShapeDtypeStruct(xs.shape, F32),
        compiler_params=_cparams("arbitrary", "arbitrary"),
        name="moe_grouped",
    )(tile_e, n_tiles, tile_rows, xs, w1, w3, w2)


def _combine_kernel(tab_ref, lp_ref, wt_ref, x_ref, gate_ref, ys_ref, op_ref, os_ref, yl_scr, sem, *, n_p):
    i = pl.program_id(0)
    tm = x_ref.shape[0]
    _start_then_wait(_run_copies(tab_ref, i, yl_scr, ys_ref, sem.at[0], to_global=False))
    slot = lax.broadcasted_iota(jnp.int32, (tm, MOE_LOCAL), 1)
    mix = (jnp.where(slot == lp_ref[:, 0:1], wt_ref[:, 0:1], 0.0)
           + jnp.where(slot == lp_ref[:, 1:2], wt_ref[:, 1:2], 0.0))
    moe = _bdot(mix, yl_scr[...])
    out = x_ref[...] + gate_ref[...] * moe

    @pl.when(pl.program_id(0) < n_p)
    def _():
        op_ref[...] = out

    @pl.when(pl.program_id(0) >= n_p)
    def _():
        os_ref[...] = out


def moe_combine(x, mods, lp, wts, run_tab, ys, t_prompt, tm=MOE_TOK):
    T = x.shape[0]
    n_p = t_prompt // tm
    return pl.pallas_call(
        functools.partial(_combine_kernel, n_p=n_p),
        grid_spec=pltpu.PrefetchScalarGridSpec(
            num_scalar_prefetch=1,
            grid=(T // tm,),
            in_specs=[pl.BlockSpec((tm, 2), lambda i, *_: (i, 0)),
                      pl.BlockSpec((tm, 2), lambda i, *_: (i, 0)),
                      pl.BlockSpec((tm, D), lambda i, *_: (i, 0)),
                      _mod_spec(5, tm),
                      pl.BlockSpec(memory_space=pl.ANY)],
            out_specs=_part_specs((tm, D), n_p),
            scratch_shapes=[pltpu.VMEM((MOE_LOCAL, D), F32), pltpu.SemaphoreType.DMA((1,))]),
        out_shape=[jax.ShapeDtypeStruct((t_prompt, D), F32), jax.ShapeDtypeStruct((T - t_prompt, D), F32)],
        compiler_params=_cparams("arbitrary"),
        name="moe_combine",
    )(run_tab, lp.T, wts.T, x, mods, ys)


def kernel(x_prompt, x_sample, state_C, state_n, state_m, cache_k, cache_v, c, c_ctx, norm1_g, norm2_g, w_ada, b_ada, ev_w_in, ev_conv, hy_w1, hy_b1, hy_w2, hy_b2, hy_w3, hy_freq, hy_d, ml_b_gate, ml_norm_g, ev_w_out, ff_w1, ff_w3, ff_w2, at_w_qkv, at_q_g, at_k_g, at_sink, at_w_out, moe_w_router, moe_b_router, moe_w1, moe_w3, moe_w2):
    BP, LP, _ = x_prompt.shape
    BS, LS, _ = x_sample.shape
    TP = BP * LP
    assert TP % GROUP == 0 and TP // GROUP == N_PROMPT_GROUPS and LS == GROUP and BS == 8

    xp, xs = x_prompt.reshape(TP, D), x_sample.reshape(BS * LS, D)
    cond = jnp.concatenate([c_ctx[None], c, jnp.zeros((16 - 1 - BS, D), F32)], axis=0)
    mods = adaln_table(cond, w_ada, b_ada)

    u, gates = even_in_proj(xp, xs, norm1_g[0:1], mods[0], ev_w_in[0], ml_b_gate[0].reshape(1, N_GATES))
    hy = []
    for seq0, B, L, nb in ((0, BP, LP, 4), (TP // LS, BS, LS, 1)):
        fwd, inv = (jnp.asarray(t).astype(BF16) for t in _dft_tables(L))
        ka, kb = hyena_filter_spectra(L, hy_w1[0], hy_b1[0], hy_w2[0], hy_b2[0], hy_w3[0], hy_freq[0], fwd)
        hy.append(hyena_mix(u, seq0, B, L, ev_conv[0], hy_d[0], fwd, inv, ka, kb, nb))
    ml_p, new_C, new_n, new_m = mlstm_mix(u, gates, 0, BP, LP, ml_norm_g[0], want_state=True)
    ml_s = mlstm_mix(u, gates, TP // LS, BS, LS, ml_norm_g[0],
                     state=(state_C[:, 0], state_n[:, 0], state_m[:, 0]))
    x = proj_residual([hy, (ml_p, ml_s)], ev_w_out[0], (xp, xs), mods[0], 2)
    x = ffn_residual(x, norm2_g[0:1], mods[0], ff_w1[0], ff_w3[0], ff_w2[0])

    q, kv = qkv_proj(x, norm1_g[1:2], mods[1], at_w_qkv[0])
    o_p, new_k, new_v = attention(q, kv, 0, at_q_g[0], at_k_g[0], at_sink[0], BP, LP)
    o_s = attention(q, kv, TP, at_q_g[0], at_k_g[0], at_sink[0], BS, LS, cache=(cache_k[:, 0], cache_v[:, 0]))
    x = proj_residual([(o_p, o_s)], at_w_out[0], x, mods[1], 2)
    lp, wts, runs, totals = moe_router(x, norm2_g[1:2], mods[1], moe_w_router[0], moe_b_router[0])
    run_tab, tail, tile_e, n_tiles, tile_rows = moe_layout(runs, totals)
    xsort = moe_dispatch(x, norm2_g[1:2], mods[1], lp, run_tab, tail, n_tiles)
    ysort = moe_grouped_swiglu(xsort, tile_e, n_tiles, tile_rows, moe_w1[0], moe_w3[0], moe_w2[0])
    yp, ys = moe_combine(x, mods[1], lp, wts, run_tab, ysort, TP)

    return (yp.reshape(BP, LP, D), ys.reshape(BS, LS, D),
            new_C[:, None], new_n[:, None], new_m[:, None], new_k[:, None], new_v[:, None])
```

```python
import functools
import math

import numpy as np
import jax
import jax.numpy as jnp
from jax import lax
from jax.experimental import pallas as pl
from jax.experimental.pallas import tpu as pltpu

F32 = jnp.float32
BF16 = jnp.bfloat16
HIGHEST = lax.Precision.HIGHEST

D = 1024
GROUP = 1024
N_PROMPT_GROUPS = 8
HY_W = 512
ML_HEADS = 4
ML_HD = 128
ML_CHUNK = 256
EVEN_MAIN = 3 * HY_W + 4 * 512
N_GATES = 16
ATT_HD = 64
ATT_HEADS = 16
ATT_KV = 4
WINDOW = 128
GRID_W = 64
ROPE_BASE = 10000.0
D_FF = 2816
N_EXPERTS = 8
EPS = 1e-6
NEG = -1e30
VMEM_LIMIT = 56 * 1024 * 1024


def _cparams(*sem, flags=None):
    return pltpu.CompilerParams(dimension_semantics=sem, vmem_limit_bytes=VMEM_LIMIT, flags=flags)


def _mod_row(i, tm):
    return jnp.maximum(i * tm // GROUP - (N_PROMPT_GROUPS - 1), 0)


def _silu(x):
    return x * jax.nn.sigmoid(x)


def _bdot(a, b):
    return jnp.dot(a.astype(BF16), b.astype(BF16), preferred_element_type=F32)


def _norm_mod(x, g, sh, sc):
    y = x * lax.rsqrt(jnp.mean(x * x, axis=-1, keepdims=True) + EPS) * g
    return y * (1.0 + sc) + sh


def _adaln_kernel(c_ref, w_ref, b_ref, o_ref):
    s = _silu(c_ref[...])
    o_ref[...] = jnp.dot(s, w_ref[...], precision=HIGHEST, preferred_element_type=F32) + b_ref[...]


def adaln_table(cond, w_ada, b_ada):
    depth = w_ada.shape[0]
    tn = 1536
    out = pl.pallas_call(
        _adaln_kernel,
        grid=(depth, 6 * D // tn),
        in_specs=[pl.BlockSpec((16, D), lambda l, j: (0, 0)),
                  pl.BlockSpec((None, D, tn), lambda l, j: (l, 0, j)),
                  pl.BlockSpec((None, 1, tn), lambda l, j: (l, 0, j))],
        out_specs=pl.BlockSpec((None, 16, tn), lambda l, j: (l, 0, j)),
        out_shape=jax.ShapeDtypeStruct((depth, 16, 6 * D), F32),
        compiler_params=_cparams("parallel", "parallel"),
        name="adaln",
    )(cond, w_ada, b_ada.reshape(depth, 1, 6 * D))
    return out.reshape(depth, 16, 1, 6 * D)


def _part_specs(block, n_p):
    return [pl.BlockSpec(block, lambda i, *_: (jnp.minimum(i, n_p - 1), 0)),
            pl.BlockSpec(block, lambda i, *_: (jnp.maximum(i - n_p, 0), 0))]


def _pick(is_prompt, p_ref, s_ref):
    return jnp.where(is_prompt, p_ref[...], s_ref[...])


def _mod_spec(k, tm):
    return pl.BlockSpec((None, 1, D), lambda i, *_: (_mod_row(i, tm), 0, k))


def _log_sigmoid(x):
    return jnp.minimum(x, 0.0) - jnp.log(1.0 + jnp.exp(-jnp.abs(x)))


def _split3(x):
    hi = x.astype(BF16)
    r = x - hi.astype(F32)
    mid = r.astype(BF16)
    return hi, mid, (r - mid.astype(F32)).astype(BF16)


def _even_in_kernel(xp_ref, xs_ref, g_ref, sh_ref, sc_ref, w_ref, wg_ref, bg_ref, lo_ref, up_ref, u_ref, gate_ref,
                    h_scr, *, n_p):
    is_prompt = pl.program_id(0) < n_p

    @pl.when(pl.program_id(1) == 0)
    def _():
        h = _norm_mod(_pick(is_prompt, xp_ref, xs_ref), g_ref[...], sh_ref[...], sc_ref[...]).astype(BF16)
        h_scr[...] = h
        gates = _bdot(h, wg_ref[...]) + bg_ref[...]
        lf = _log_sigmoid(gates)
        col = lax.broadcasted_iota(jnp.int32, (1, N_GATES), 1)
        is_forget = (col // ML_HEADS) % 2 == 1
        is_rev = col >= N_GATES // 2
        for ch in range(h.shape[0] // ML_CHUNK):
            sl = slice(ch * ML_CHUNK, (ch + 1) * ML_CHUNK)
            parts = _split3(lf[sl])
            cf = sum(jnp.dot(lo_ref[...], p, preferred_element_type=F32) for p in parts)
            cr = sum(jnp.dot(up_ref[...], p, preferred_element_type=F32) for p in parts)
            gate_ref[sl, :] = jnp.where(is_forget, jnp.where(is_rev, cr, cf), gates[sl])

    u_ref[...] = _bdot(h_scr[...], w_ref[...]).astype(u_ref.dtype)


def even_in_proj(xp, xs, g, mods, w_in, b_gate, tm=1024, tn=512):
    T = xp.shape[0] + xs.shape[0]
    tri = np.tril(np.ones((ML_CHUNK, ML_CHUNK), np.float32))
    lo, up = jnp.asarray(tri).astype(BF16), jnp.asarray(tri.T).astype(BF16)
    return pl.pallas_call(
        functools.partial(_even_in_kernel, n_p=xp.shape[0] // tm),
        grid=(T // tm, EVEN_MAIN // tn),
        in_specs=_part_specs((tm, D), xp.shape[0] // tm) + [
                  pl.BlockSpec((1, D), lambda i, j: (0, 0)),
                  _mod_spec(0, tm), _mod_spec(1, tm),
                  pl.BlockSpec((D, tn), lambda i, j: (0, j)),
                  pl.BlockSpec((D, N_GATES), lambda i, j: (0, 0)),
                  pl.BlockSpec((1, N_GATES), lambda i, j: (0, 0)),
                  pl.BlockSpec((ML_CHUNK, ML_CHUNK), lambda i, j: (0, 0)),
                  pl.BlockSpec((ML_CHUNK, ML_CHUNK), lambda i, j: (0, 0))],
        out_specs=[pl.BlockSpec((tm, tn), lambda i, j: (i, j)),
                   pl.BlockSpec((tm, N_GATES), lambda i, j: (i, 0))],
        out_shape=[jax.ShapeDtypeStruct((T, EVEN_MAIN), BF16),
                   jax.ShapeDtypeStruct((T, N_GATES), F32)],
        scratch_shapes=[pltpu.VMEM((tm, D), BF16)],
        compiler_params=_cparams("parallel", "arbitrary"),
        name="even_in_proj",
    )(xp, xs, g, mods, mods, w_in, w_in[:, EVEN_MAIN:], b_gate, lo, up)


def _dft_tables(L):
    n = 2 * L
    f = np.arange(L, dtype=np.int64)[:, None]
    s = np.arange(L, dtype=np.int64)[None, :]
    ang = 2.0 * np.pi * ((f * s) % n).astype(np.float64) / n
    fwd = np.concatenate([np.cos(ang), -np.sin(ang)], axis=0)
    fwd[L, :] = np.where(np.arange(L) % 2 == 0, 1.0, -1.0)
    t = np.arange(L, dtype=np.int64)[:, None]
    ff = np.arange(L, dtype=np.int64)[None, :]
    ang = 2.0 * np.pi * ((t * ff) % n).astype(np.float64) / n
    inv_re = 2.0 * np.cos(ang) / n
    inv_re[:, 0] = 1.0 / n
    inv_im = -2.0 * np.sin(ang) / n
    inv_im[:, 0] = np.where(np.arange(L) % 2 == 0, 1.0, -1.0) / n
    inv = np.concatenate([inv_re, inv_im], axis=1)
    return fwd.astype(np.float32), inv.astype(np.float32)


def _filter_tables(L):
    t = np.linspace(0.0, 1.0, L, dtype=np.float32).astype(np.float64)[:, None]
    w = 2.0 * math.pi * np.arange(L, dtype=np.float64)[:, None] / L
    bands = np.linspace(1e-4, 16 - 1, 16, dtype=np.float32).astype(np.float64)[None, :]
    z = np.concatenate([t, np.cos(bands * w), -np.sin(bands * w)], axis=-1)
    zp = np.zeros((L, 128), np.float64)
    zp[:, :z.shape[1]] = z
    max_decay = math.log(1e-2) / 0.3
    min_decay = math.log(1e-2) / 1.5
    deltas = np.linspace(min_decay, max_decay, HY_W, dtype=np.float32).astype(np.float64)
    decay = np.exp(-t * np.abs(deltas))
    return zp.astype(np.float32), decay.astype(np.float32)


def _hy_filter_kernel(z_ref, dec_ref, w1_ref, b1_ref, w2_ref, b2_ref, w3_ref, fr_ref, fwd_ref,
                      ka_ref, kb_ref):
    L = z_ref.shape[0]
    hdot = functools.partial(jnp.dot, precision=HIGHEST, preferred_element_type=F32)
    h = jnp.sin(fr_ref[0:1, :] * (hdot(z_ref[...], w1_ref[...]) + b1_ref[...]))
    h = jnp.sin(fr_ref[1:2, :] * (hdot(h, w2_ref[...]) + b2_ref[...]))
    h = hdot(h, w3_ref[...])
    row0 = lax.broadcasted_iota(jnp.int32, (L, 1), 0) == 0
    h0 = h[:, :HY_W] * dec_ref[...]
    h1 = h[:, HY_W:] * dec_ref[...]
    l1 = jnp.sum(jnp.abs(h0), axis=0, keepdims=True) + jnp.sum(jnp.abs(h1), axis=0, keepdims=True)
    inv = 1.0 / l1
    h0 = h0 * inv
    h1 = jnp.where(row0, 0.0, h1 * inv)
    f0 = _bdot(fwd_ref[...], h0)
    f1 = _bdot(fwd_ref[...], h1)
    ka_ref[...] = f0[:L] + f1[:L]
    kb_ref[...] = jnp.where(row0, f0[L:] + f1[L:], f0[L:] - f1[L:])


def _const_spec(a, n_grid):
    return pl.BlockSpec(a.shape, lambda *_: (0,) * a.ndim, pipeline_mode=pl.Buffered(1))


def hyena_filter_spectra(L, w1, b1, w2, b2, w3, freq, fwd):
    z, dec = _filter_tables(L)
    pad2 = lambda a, r, c: jnp.pad(a, ((0, r - a.shape[0]), (0, c - a.shape[1])))
    args = (jnp.asarray(z), jnp.asarray(dec), pad2(w1, 128, 128), pad2(b1[None], 1, 128),
            pad2(w2, 128, 128), pad2(b2[None], 1, 128), pad2(w3, 128, 4 * HY_W), pad2(freq, 2, 128), fwd)
    in_specs = [_const_spec(a, 1) for a in args]
    in_specs[6] = pl.BlockSpec((128, 2 * HY_W), lambda o: (0, o))
    shp = jax.ShapeDtypeStruct((2, L, HY_W), F32)
    out_spec = pl.BlockSpec((None, L, HY_W), lambda o: (o, 0, 0))
    return pl.pallas_call(
        _hy_filter_kernel,
        grid=(2,),
        in_specs=in_specs,
        out_specs=[out_spec, out_spec],
        out_shape=[shp, shp],
        compiler_params=_cparams("arbitrary"),
        name=f"hyena_filter_{L}",
    )(*args)


def _hyena_kernel(u_ref, cw_ref, d_ref, fwd_ref, inv_ref, ka_ref, kb_ref, o_ref):
    nb, L = u_ref.shape[0], u_ref.shape[1]
    row = lax.broadcasted_iota(jnp.int32, (L, 1), 0)
    first, last = row == 0, row == L - 1
    fwd = fwd_ref[...].astype(BF16)
    inv = inv_ref[...].astype(BF16)

    def long_conv(z, o):
        zf = jnp.dot(fwd, z.astype(BF16), preferred_element_type=F32)
        a, b = zf[:L], zf[L:]
        ka, kb = ka_ref[o], kb_ref[o]
        yr = a * ka - jnp.where(first, 0.0, b * kb)
        yi = jnp.where(first, b * kb, a * kb + b * ka)
        return (jnp.dot(inv[:, :L], yr.astype(BF16), preferred_element_type=F32)
                + jnp.dot(inv[:, L:], yi.astype(BF16), preferred_element_type=F32))

    for bi in range(nb):
        u = u_ref[bi].astype(F32)
        prev = jnp.where(first, 0.0, pltpu.roll(u, 1, 0))
        nxt = jnp.where(last, 0.0, pltpu.roll(u, L - 1, 0))
        u = prev * cw_ref[0:1, :] + u * cw_ref[1:2, :] + nxt * cw_ref[2:3, :]
        v, x1, x2 = u[:, :HY_W], u[:, HY_W:2 * HY_W], u[:, 2 * HY_W:]
        z = x1 * (long_conv(v, 0) + d_ref[0:1, :] * v)
        z = x2 * (long_conv(z, 1) + d_ref[1:2, :] * z)
        o_ref[bi] = z.astype(o_ref.dtype)


def hyena_mix(u, seq0, B, L, conv_w, d_skip, fwd, inv, ka, kb, nb):
    u3 = u.reshape(-1, L, EVEN_MAIN)
    full = lambda a: _const_spec(a, 1)
    out = pl.pallas_call(
        _hyena_kernel,
        grid=(B // nb,),
        in_specs=[pl.BlockSpec((nb, L, 3 * HY_W), lambda b: (b + seq0 // nb, 0, 0)),
                  full(conv_w), full(d_skip), full(fwd), full(inv), full(ka), full(kb)],
        out_specs=pl.BlockSpec((nb, L, HY_W), lambda b: (b, 0, 0)),
        out_shape=jax.ShapeDtypeStruct((B, L, HY_W), BF16),
        compiler_params=_cparams("parallel"),
        name=f"hyena_{L}",
    )(u3, conv_w, d_skip, fwd, inv, ka, kb)
    return out.reshape(B * L, HY_W)


def _mlstm_kernel(*refs, has_state, want_state):
    q_ref, k_ref, v_ref, o_ref, gc_ref, gr_ref, ng_ref = refs[:7]
    refs = refs[7:]
    if has_state:
        c0t_ref, n0b_ref, m0_ref = refs[:3]
        refs = refs[3:]
    y_ref = refs[0]
    if want_state:
        c_out, n_out, m_out = refs[1:4]
    L, d = q_ref.shape[0], ML_HD
    T = min(ML_CHUNK, L)
    nc = L // T
    scale = 1.0 / math.sqrt(d)
    nt = (((1,), (1,)), ((), ()))
    si = lax.broadcasted_iota(jnp.int32, (T, T), 0)
    ti = lax.broadcasted_iota(jnp.int32, (T, T), 1)
    allowed = (si <= ti, si >= ti)
    chains = [(dr, h) for dr in range(2) for h in range(ML_HEADS)]
    gcol = lambda dr, gi, h: dr * 2 * ML_HEADS + gi * ML_HEADS + h

    caug_t, m = {}, {}
    for ch in chains:
        dr, h = ch
        if has_state:
            caug_t[ch] = jnp.concatenate([c0t_ref[dr, h], n0b_ref[dr, h]], axis=0)
            m[ch] = m0_ref[dr, h:h + 1, 0:1]
        else:
            caug_t[ch], m[ch] = jnp.zeros((2 * d, d), F32), jnp.zeros((1, 1), F32)

    chunk_cache = {}

    def chunk_data(h, j):
        if (h, j) not in chunk_cache:
            sl, hl = slice(j * T, (j + 1) * T), slice(h * d, (h + 1) * d)
            q = q_ref[sl, hl]
            ks = (k_ref[sl, hl].astype(F32) * scale).astype(BF16)
            v_t = v_ref[sl, hl].astype(F32).T
            vaug_t = jnp.concatenate([v_t, jnp.ones((d, T), F32)], axis=0).astype(BF16)
            s_raw = lax.dot_general(ks, q, nt, preferred_element_type=F32)
            chunk_cache[(h, j)] = (q, ks, v_t, vaug_t, s_raw)
        return chunk_cache[(h, j)]

    h_sum = {}
    for it in range(nc):
        step = {ch: (it if ch[0] == 0 else nc - 1 - it) for ch in chains}
        data = {ch: chunk_data(ch[1], step[ch]) for ch in chains}
        inter_t = {ch: lax.dot_general(caug_t[ch].astype(BF16), data[ch][0], nt, preferred_element_type=F32)
                   for ch in chains}
        gate = {}
        for ch in chains:
            dr, h = ch
            sl = slice(step[ch] * T, (step[ch] + 1) * T)
            li_r, b_r = gr_ref[gcol(dr, 0, h):gcol(dr, 0, h) + 1, sl], gr_ref[gcol(dr, 1, h):gcol(dr, 1, h) + 1, sl]
            src = gc_ref[sl, gcol(dr, 0, h):gcol(dr, 0, h) + 1] - gc_ref[sl, gcol(dr, 1, h):gcol(dr, 1, h) + 1]
            dm = jnp.where(allowed[dr], src + b_r, NEG)
            inter = b_r + m[ch]
            m_t = jnp.maximum(inter, jnp.max(dm, axis=0, keepdims=True))
            b_end = b_r[:, T - 1:T] if dr == 0 else b_r[:, 0:1]
            g_r = b_end - b_r + li_r
            m_new = jnp.maximum(b_end + m[ch], jnp.max(g_r, axis=1, keepdims=True))
            gate[ch] = (jnp.exp(dm - m_t), jnp.exp(inter - m_t), jnp.exp(-m_t), jnp.exp(g_r - m_new),
                        jnp.exp(b_end + m[ch] - m_new), m_new)
        for ch in chains:
            q, ks, v_t, vaug_t, s_raw = data[ch]
            w_intra, w_inter, floor, w_tok, decay, m_new = gate[ch]
            acc = jnp.dot(vaug_t, (s_raw * w_intra).astype(BF16), preferred_element_type=F32) + w_inter * inter_t[ch]
            h_t = acc[:d] / jnp.maximum(jnp.abs(acc[d:]), floor)
            key = (ch[1], step[ch])
            h_sum[key] = h_t if key not in h_sum else h_sum[key] + h_t
            vw_t = jnp.concatenate([v_t * w_tok, jnp.broadcast_to(w_tok, (d, T))], axis=0).astype(BF16)
            caug_t[ch] = decay * caug_t[ch] + jnp.dot(vw_t, ks, preferred_element_type=F32)
            m[ch] = m_new

    if want_state:
        for ch in chains:
            dr, h = ch
            c_out[dr, h] = caug_t[ch][:d].T
            n_out[dr, h:h + 1, :] = caug_t[ch][d:d + 1, :]
            m_out[dr, h:h + 1, :] = jnp.broadcast_to(m[ch], (1, d))
    for h in range(ML_HEADS):
        for j in range(nc):
            sl, hl = slice(j * T, (j + 1) * T), slice(h * d, (h + 1) * d)
            hv = h_sum[(h, j)].T
            y = hv * lax.rsqrt(jnp.mean(hv * hv, axis=-1, keepdims=True) + EPS) * ng_ref[:, hl]
            y_ref[sl, hl] = (y * jax.nn.sigmoid(o_ref[sl, hl].astype(F32))).astype(y_ref.dtype)


def mlstm_mix(u, gates, seq0, B, L, norm_g, state=None, want_state=False):
    u3 = u.reshape(-1, L, EVEN_MAIN)
    gc = gates.reshape(-1, L, N_GATES)[seq0:seq0 + B]
    gr = gc.transpose(0, 2, 1)
    width = ML_HEADS * ML_HD
    col = lambda i: pl.BlockSpec((None, L, width), lambda b: (b + seq0, 0, (3 * HY_W + i * width) // width))
    in_specs = [col(0), col(1), col(2), col(3),
                pl.BlockSpec((None, L, N_GATES), lambda b: (b, 0, 0)),
                pl.BlockSpec((None, N_GATES, L), lambda b: (b, 0, 0)),
                pl.BlockSpec((1, width), lambda b: (0, 0))]
    args = [u3, u3, u3, u3, gc, gr, norm_g.reshape(1, width)]
    sspec = pl.BlockSpec((None, 2, ML_HEADS, ML_HD, ML_HD), lambda b: (b, 0, 0, 0, 0))
    vspec = pl.BlockSpec((None, 2, ML_HEADS, ML_HD), lambda b: (b, 0, 0, 0))
    if state is not None:
        C0, n0, m0 = state
        in_specs += [sspec, sspec, vspec]
        args += [C0.swapaxes(-1, -2), jnp.broadcast_to(n0[..., None, :], C0.shape),
                 jnp.broadcast_to(m0[..., None], n0.shape)]
    out_specs = [pl.BlockSpec((None, L, width), lambda b: (b, 0, 0))]
    out_shape = [jax.ShapeDtypeStruct((B, L, width), BF16)]
    if want_state:
        out_specs += [sspec, vspec, vspec]
        out_shape += [jax.ShapeDtypeStruct((B, 2, ML_HEADS, ML_HD, ML_HD), F32),
                      jax.ShapeDtypeStruct((B, 2, ML_HEADS, ML_HD), F32),
                      jax.ShapeDtypeStruct((B, 2, ML_HEADS, ML_HD), F32)]
    outs = pl.pallas_call(
        functools.partial(_mlstm_kernel, has_state=state is not None, want_state=want_state),
        grid=(B,),
        in_specs=in_specs, out_specs=out_specs, out_shape=out_shape,
        compiler_params=_cparams("parallel"),
        name=f"mlstm_{L}",
    )(*args)
    y = outs[0].reshape(B * L, width)
    if not want_state:
        return y
    _, C, n, m = outs
    return y, C, n, m[..., 0]


def _proj_res_kernel(*refs, n_in, n_x, n_p):
    a_refs = refs[:2 * n_in]
    w_ref = refs[2 * n_in]
    x_refs = refs[2 * n_in + 1:2 * n_in + 1 + n_x]
    gate_ref, o_ref = refs[2 * n_in + 1 + n_x:]
    is_prompt = pl.program_id(0) < n_p
    k0 = 0
    acc = None
    for j in range(n_in):
        a = _pick(is_prompt, a_refs[2 * j], a_refs[2 * j + 1])
        kw = a.shape[1]
        part = _bdot(a, w_ref[k0:k0 + kw, :])
        acc = part if acc is None else acc + part
        k0 += kw
    x = _pick(is_prompt, *x_refs) if n_x == 2 else x_refs[0][...]
    o_ref[...] = x + gate_ref[...] * acc


def proj_residual(acts, w, x, mods, gate_idx, tm=1024):
    xs = tuple(x) if isinstance(x, (tuple, list)) else (x,)
    T = sum(a.shape[0] for a in acts[0])
    n_p = acts[0][0].shape[0] // tm
    in_specs = []
    for pair in acts:
        in_specs += _part_specs((tm, pair[0].shape[1]), n_p)
    in_specs.append(pl.BlockSpec(w.shape, lambda i: (0, 0)))
    in_specs += _part_specs((tm, D), n_p) if len(xs) == 2 else [pl.BlockSpec((tm, D), lambda i: (i, 0))]
    in_specs.append(_mod_spec(gate_idx, tm))
    return pl.pallas_call(
        functools.partial(_proj_res_kernel, n_in=len(acts), n_x=len(xs), n_p=n_p),
        grid=(T // tm,),
        in_specs=in_specs,
        out_specs=pl.BlockSpec((tm, D), lambda i: (i, 0)),
        out_shape=jax.ShapeDtypeStruct((T, D), F32),
        compiler_params=_cparams("parallel"),
        name="proj_residual",
    )(*[a for pair in acts for a in pair], w, *xs, mods)


SWIGLU_ROWS = 512


def _swiglu_accumulate(h_scr, acc_scr, w1_ref, w3_ref, w2_ref, rows):
    w1, w3, w2 = w1_ref[...].astype(BF16), w3_ref[...].astype(BF16), w2_ref[...].astype(BF16)
    groups = [slice(r, r + SWIGLU_ROWS) for r in range(0, rows, SWIGLU_ROWS)]
    ups = []
    for sl in groups:
        h = h_scr[sl, :]
        ups.append((jnp.dot(h, w1, preferred_element_type=F32), jnp.dot(h, w3, preferred_element_type=F32)))
    for sl, (a, b) in zip(groups, ups):
        mid = (_silu(a) * b).astype(BF16)
        acc_scr[sl, :] += jnp.dot(mid, w2, preferred_element_type=F32)


def _ffn_kernel(x_ref, g_ref, sh_ref, sc_ref, gate_ref, w1_ref, w3_ref, w2_ref, o_ref, h_scr, acc_scr):
    c = pl.program_id(1)

    @pl.when(c == 0)
    def _():
        h_scr[...] = _norm_mod(x_ref[...], g_ref[...], sh_ref[...], sc_ref[...]).astype(BF16)
        acc_scr[...] = jnp.zeros_like(acc_scr)

    _swiglu_accumulate(h_scr, acc_scr, w1_ref, w3_ref, w2_ref, h_scr.shape[0])

    @pl.when(c == pl.num_programs(1) - 1)
    def _():
        o_ref[...] = x_ref[...] + gate_ref[...] * acc_scr[...]


def ffn_residual(x, g, mods, w1, w3, w2, tm=1024, tf=256):
    T = x.shape[0]
    return pl.pallas_call(
        _ffn_kernel,
        grid=(T // tm, D_FF // tf),
        in_specs=[pl.BlockSpec((tm, D), lambda i, c: (i, 0)),
                  pl.BlockSpec((1, D), lambda i, c: (0, 0)),
                  _mod_spec(3, tm), _mod_spec(4, tm), _mod_spec(5, tm),
                  pl.BlockSpec((D, tf), lambda i, c: (0, c)),
                  pl.BlockSpec((D, tf), lambda i, c: (0, c)),
                  pl.BlockSpec((tf, D), lambda i, c: (c, 0))],
        out_specs=pl.BlockSpec((tm, D), lambda i, c: (i, 0)),
        out_shape=jax.ShapeDtypeStruct((T, D), F32),
        scratch_shapes=[pltpu.VMEM((tm, D), BF16), pltpu.VMEM((tm, D), F32)],
        compiler_params=_cparams("parallel", "arbitrary"),
        name="ffn",
    )(x, g, mods, mods, mods, w1, w3, w2)


def _qkv_kernel(x_ref, g_ref, sh_ref, sc_ref, w_ref, q_ref, kv_ref, h_scr):
    j = pl.program_id(1)

    @pl.when(j == 0)
    def _():
        h_scr[...] = _norm_mod(x_ref[...], g_ref[...], sh_ref[...], sc_ref[...]).astype(BF16)

    @pl.when(j < 2)
    def _():
        q_ref[...] = _bdot(h_scr[...], w_ref[...]).astype(q_ref.dtype)

    @pl.when(j == 2)
    def _():
        kv_ref[...] = _bdot(h_scr[...], w_ref[...])


def qkv_proj(x, g, mods, w_qkv, tm=1024):
    T = x.shape[0]
    tn = 512
    return pl.pallas_call(
        _qkv_kernel,
        grid=(T // tm, 3),
        in_specs=[pl.BlockSpec((tm, D), lambda i, j: (i, 0)),
                  pl.BlockSpec((1, D), lambda i, j: (0, 0)),
                  _mod_spec(0, tm), _mod_spec(1, tm),
                  pl.BlockSpec((D, tn), lambda i, j: (0, j))],
        out_specs=[pl.BlockSpec((tm, tn), lambda i, j: (i, jnp.minimum(j, 1))),
                   pl.BlockSpec((tm, tn), lambda i, j: (i, 0))],
        out_shape=[jax.ShapeDtypeStruct((T, ATT_HEADS * ATT_HD), BF16),
                   jax.ShapeDtypeStruct((T, 2 * ATT_KV * ATT_HD), F32)],
        scratch_shapes=[pltpu.VMEM((tm, D), BF16)],
        compiler_params=_cparams("parallel", "arbitrary"),
        name="qkv_proj",
    )(x, g, mods, mods, w_qkv)


def _rope_tables(L):
    half = ATT_HD // 2
    pos_r = (np.arange(L) // GRID_W).astype(np.float32)
    pos_c = (np.arange(L) % GRID_W).astype(np.float32)
    inv = (ROPE_BASE ** (-np.arange(0, half, 2, dtype=np.float32) / half)).astype(np.float32)
    cos = np.zeros((L, ATT_HD), np.float64)
    sin = np.zeros((L, ATT_HD), np.float64)
    for base, pos in ((0, pos_r), (half, pos_c)):
        ang = (pos[:, None] * inv[None, :]).astype(np.float32).astype(np.float64)
        cos[:, base:base + half] = np.concatenate([np.cos(ang), np.cos(ang)], axis=1)
        sin[:, base:base + half] = np.concatenate([-np.sin(ang), np.sin(ang)], axis=1)
    return (np.tile(cos, (1, 4)).astype(np.float32), np.tile(sin, (1, 4)).astype(np.float32))


def _seg_rms(x):
    w = x.shape[1]
    ri = lax.broadcasted_iota(jnp.int32, (w, w), 0) // ATT_HD
    ci = lax.broadcasted_iota(jnp.int32, (w, w), 1) // ATT_HD
    ss = _bdot(x * x, (ri == ci).astype(F32))
    return x * lax.rsqrt(ss * (1.0 / ATT_HD) + EPS)


LOG2E = 1.4426950408889634


def _exp2_bf16(x):
    return jnp.exp2(x.astype(BF16))


def _both_halves(tile, low):
    lane = lax.broadcasted_iota(jnp.int32, tile.shape, 1)
    other = pltpu.roll(tile, ATT_HD, 1)
    return jnp.where((lane < ATT_HD) == low, tile, other)


def _swap16(x):
    w = x.shape[1]
    lane = lax.broadcasted_iota(jnp.int32, x.shape, 1)
    return jnp.where(lane % 32 < 16, pltpu.roll(x, w - 16, 1), pltpu.roll(x, 16, 1))


def _attn_kernel(*refs, latent, tq):
    if latent:
        (q_ref, kv_ref, ck_ref, cv_ref, qg_ref, kg_ref, sink_ref, cosq_ref, sinq_ref, cosk_ref, sink_t_ref,
         o_ref, kk_scr, vt_scr, ckk_scr, cvt_scr) = refs
    else:
        q_ref, kv_ref, qg_ref, kg_ref, sink_ref, o_ref, ko_ref, vo_ref, kk_scr, vt_scr = refs
    L = kv_ref.shape[0]
    gw = ATT_KV * ATT_HD
    pw = 2 * ATT_HD
    qb = pl.program_id(1)

    vrows = vt_scr.shape[2]
    nblk = L // pw

    def vt_aug(tile, low):
        vt = tile.T[0:ATT_HD, :] if low else tile.T[ATT_HD:, :]
        return jnp.concatenate([vt, jnp.ones((vrows - ATT_HD, tile.shape[0]), F32)], axis=0).astype(BF16)

    @pl.when(qb == 0)
    def _():
        kn = _seg_rms(kv_ref[:, :gw]) * kg_ref[...]
        v = kv_ref[:, gw:]
        if latent:
            kn = kn * cosk_ref[...] + _swap16(kn) * sink_t_ref[...]
        else:
            for c in range(ATT_KV):
                ko_ref[c] = kn[:, c * ATT_HD:(c + 1) * ATT_HD]
                vo_ref[c] = v[:, c * ATT_HD:(c + 1) * ATT_HD]
        for c in range(ATT_KV):
            tile, low = slice((c // 2) * pw, (c // 2 + 1) * pw), c % 2 == 0
            kk_scr[c] = _both_halves(kn[:, tile], low).astype(BF16)
            for j in range(nblk):
                vt_scr[c, j] = vt_aug(v[j * pw:(j + 1) * pw, tile], low)
            if latent:
                ck, cv = ck_ref[c], cv_ref[c]
                ckk_scr[c] = jnp.concatenate([ck, ck], axis=1).astype(BF16)
                cvt_scr[c] = vt_aug(jnp.concatenate([cv, cv], axis=1), True)

    if latent:
        span = tq + 2 * WINDOW
        start = pl.multiple_of(jnp.clip(qb * tq - WINDOW, 0, L - span), WINDOW)
        blk0 = start // pw
        s_pos = start + lax.broadcasted_iota(jnp.int32, (span, tq), 0)
        t_pos = qb * tq + lax.broadcasted_iota(jnp.int32, (span, tq), 1)
        win_bias = jnp.where(jnp.abs(t_pos - s_pos) <= WINDOW, 0.0, NEG)
    else:
        span, blk0 = L, 0

    nt = (((1,), (1,)), ((), ()))
    low_q = lax.broadcasted_iota(jnp.int32, (tq, pw), 1) < ATT_HD
    def group_scores(c):
        qc = _seg_rms(q_ref[:, c * gw:(c + 1) * gw].astype(F32)) * qg_ref[...]
        if latent:
            qc = qc * cosq_ref[...] + _swap16(qc) * sinq_ref[...]
            kw = kk_scr[c, pl.ds(start, span), :]
        else:
            kw = kk_scr[c]
        qc = qc * (LOG2E / math.sqrt(ATT_HD))
        scores = []
        for g in range(ATT_KV):
            qt = qc[:, (g // 2) * pw:(g // 2 + 1) * pw]
            qm = jnp.where(low_q if g % 2 == 0 else ~low_q, qt, 0.0).astype(BF16)
            lw = lax.dot_general(kw, qm, nt, preferred_element_type=F32)
            lc = lax.dot_general(ckk_scr[c], qm, nt, preferred_element_type=F32) if latent else None
            scores.append((lw, lc))
        return scores

    def group_outputs(c, scores):
        vw = jnp.concatenate([vt_scr[c, blk0 + j] for j in range(span // pw)], axis=1)
        outs = []
        for g in range(ATT_KV):
            head = c * ATT_KV + g
            sink = sink_ref[:, head:head + 1] * LOG2E
            lw, lc = scores[g]
            if latent:
                lw = lw + win_bias
                mx = jnp.maximum(jnp.maximum(jnp.max(lw, axis=0, keepdims=True),
                                             jnp.max(lc, axis=0, keepdims=True)), sink)
                r = jnp.dot(vw, _exp2_bf16(lw - mx), preferred_element_type=F32) + jnp.dot(
                    cvt_scr[c], _exp2_bf16(lc - mx), preferred_element_type=F32)
            else:
                mx = jnp.maximum(jnp.max(lw, axis=0, keepdims=True), sink)
                r = jnp.dot(vw, _exp2_bf16(lw - mx), preferred_element_type=F32)
            den = r[ATT_HD:ATT_HD + 1, :] + jnp.exp2(sink - mx)
            outs.append(r[0:ATT_HD, :] / den)
        for t in range(2):
            o_ref[:, c * gw + t * pw:c * gw + (t + 1) * pw] = (
                jnp.concatenate(outs[2 * t:2 * t + 2], axis=0).T.astype(o_ref.dtype))

    scores = group_scores(0)
    for c in range(ATT_KV):
        nxt = group_scores(c + 1) if c + 1 < ATT_KV else None
        group_outputs(c, scores)
        scores = nxt


def attention(q, kv, row0, q_g, k_g, sink, B, L, cache=None, tq=256):
    latent = cache is not None
    gw = ATT_KV * ATT_HD
    qg = jnp.tile(q_g, ATT_KV)[None]
    kg = jnp.tile(k_g, ATT_KV)[None]
    nq = L // tq
    const = lambda a: pl.BlockSpec(a.shape, lambda b, i: (0,) * a.ndim)
    in_specs = [pl.BlockSpec((tq, ATT_HEADS * ATT_HD), lambda b, i: (row0 // tq + b * nq + i, 0)),
                pl.BlockSpec((L, 2 * gw), lambda b, i: (row0 // L + b, 0))]
    args = [q, kv]
    vrows = ATT_HD + 16
    scratch = [pltpu.VMEM((ATT_KV, L, 2 * ATT_HD), BF16),
               pltpu.VMEM((ATT_KV, L // (2 * ATT_HD), vrows, 2 * ATT_HD), BF16)]
    out_specs = [pl.BlockSpec((tq, ATT_HEADS * ATT_HD), lambda b, i: (b * nq + i, 0))]
    out_shape = [jax.ShapeDtypeStruct((B * L, ATT_HEADS * ATT_HD), BF16)]
    if latent:
        ck, cv = cache
        P = ck.shape[2]
        cos, sin = (jnp.asarray(t) for t in _rope_tables(L))
        in_specs += [pl.BlockSpec((None, ATT_KV, P, ATT_HD), lambda b, i: (b, 0, 0, 0))] * 2
        args += [ck, cv]
        in_specs += [const(qg), const(kg), pl.BlockSpec((1, ATT_HEADS), lambda b, i: (0, 0)),
                     pl.BlockSpec((tq, gw), lambda b, i: (i, 0)), pl.BlockSpec((tq, gw), lambda b, i: (i, 0)),
                     const(cos), const(sin)]
        args += [qg, kg, sink[None], cos, sin, cos, sin]
        scratch += [pltpu.VMEM((ATT_KV, P, 2 * ATT_HD), BF16), pltpu.VMEM((ATT_KV, vrows, P), BF16)]
    else:
        in_specs += [const(qg), const(kg), pl.BlockSpec((1, ATT_HEADS), lambda b, i: (0, 0))]
        args += [qg, kg, sink[None]]
        cache_spec = pl.BlockSpec((None, ATT_KV, L, ATT_HD), lambda b, i: (b, 0, 0, 0))
        out_specs += [cache_spec, cache_spec]
        out_shape += [jax.ShapeDtypeStruct((B, ATT_KV, L, ATT_HD), F32)] * 2
    outs = pl.pallas_call(
        functools.partial(_attn_kernel, latent=latent, tq=tq),
        grid=(B, nq),
        in_specs=in_specs, out_specs=out_specs, out_shape=out_shape,
        scratch_shapes=scratch,
        compiler_params=_cparams("parallel", "arbitrary"),
        name="attn_latent" if latent else "attn_context",
    )(*args)
    return outs[0] if latent else outs


MOE_TM = 1024
MOE_TOK = 1024
RUN_ALIGN = 8
MOE_LOCAL = 2 * MOE_TOK + N_EXPERTS * RUN_ALIGN
MOE_MAX_TILES = (2 * 16384 + (16384 // MOE_TOK) * N_EXPERTS * (RUN_ALIGN - 1)) // MOE_TM + N_EXPERTS + 1
RUN_SIZES = tuple(RUN_ALIGN << b for b in range(8, -1, -1))


def _router_kernel(x_ref, g_ref, sh_ref, sc_ref, wr_ref, br_ref, tri_ref, lp_ref, wts_ref, runs_ref, cnt_ref):
    @pl.when(pl.program_id(0) == 0)
    def _():
        cnt_ref[...] = jnp.zeros_like(cnt_ref)

    h = _norm_mod(x_ref[...], g_ref[...], sh_ref[...], sc_ref[...])
    lg = lax.dot_general(wr_ref[...], h, (((1,), (1,)), ((), ())), precision=HIGHEST,
                         preferred_element_type=F32) + br_ref[...]
    row = lax.broadcasted_iota(jnp.int32, lg.shape, 0)
    m1 = jnp.max(lg, axis=0, keepdims=True)
    i1 = jnp.min(jnp.where(lg == m1, row, N_EXPERTS), axis=0, keepdims=True)
    l2 = jnp.where(row == i1, -jnp.inf, lg)
    m2 = jnp.max(l2, axis=0, keepdims=True)
    i2 = jnp.min(jnp.where(l2 == m2, row, N_EXPERTS), axis=0, keepdims=True)
    e2 = jnp.exp(m2 - m1)
    w1 = 1.0 / (1.0 + e2)
    wts_ref[...] = jnp.concatenate([w1, e2 * w1], axis=0)
    oh1 = (row == i1).astype(F32)
    oh2 = (row == i2).astype(F32)
    cs1 = _bdot(oh1, tri_ref[...])
    cs2 = _bdot(oh2, tri_ref[...])
    tot1 = jnp.sum(oh1, axis=1, keepdims=True)
    run = jnp.ceil((tot1 + jnp.sum(oh2, axis=1, keepdims=True)) * (1.0 / RUN_ALIGN)) * RUN_ALIGN
    run_b = jnp.broadcast_to(run, (N_EXPERTS, 128))
    er = lax.broadcasted_iota(jnp.int32, (N_EXPERTS, N_EXPERTS), 0)
    ec = lax.broadcasted_iota(jnp.int32, (N_EXPERTS, N_EXPERTS), 1)
    start = jnp.dot((ec < er).astype(F32), run_b, precision=HIGHEST, preferred_element_type=F32)
    last = lax.broadcasted_iota(jnp.int32, (N_EXPERTS, 128), 0) == N_EXPERTS - 1
    run_b = jnp.where(last, MOE_LOCAL - start, run_b)
    st = start[:, 0:1]
    p1 = jnp.sum(oh1 * (st + cs1), axis=0, keepdims=True)
    p2 = jnp.sum(oh2 * (st + tot1 + cs2), axis=0, keepdims=True)
    lp_ref[...] = jnp.concatenate([p1, p2], axis=0).astype(jnp.int32)
    lane = lax.broadcasted_iota(jnp.int32, (N_EXPERTS, 128), 1)
    runs_ref[...] = jnp.where(lane == 0, run_b, jnp.where(lane == 1, start, cnt_ref[...]))
    cnt_ref[...] = cnt_ref[...] + run_b


def moe_router(x, g, mods, w_router, b_router, tm=MOE_TOK):
    T = x.shape[0]
    tri = jnp.asarray(np.triu(np.ones((tm, tm), np.float32), k=1)).astype(BF16)
    tok2 = lambda dt: jax.ShapeDtypeStruct((2, T), dt)
    return pl.pallas_call(
        _router_kernel,
        grid=(T // tm,),
        in_specs=[pl.BlockSpec((tm, D), lambda i: (i, 0)),
                  pl.BlockSpec((1, D), lambda i: (0, 0)),
                  _mod_spec(3, tm), _mod_spec(4, tm),
                  pl.BlockSpec((N_EXPERTS, D), lambda i: (0, 0)),
                  pl.BlockSpec((N_EXPERTS, 1), lambda i: (0, 0)),
                  _const_spec(tri, 1)],
        out_specs=[pl.BlockSpec((2, tm), lambda i: (0, i)),
                   pl.BlockSpec((2, tm), lambda i: (0, i)),
                   pl.BlockSpec((None, N_EXPERTS, 128), lambda i: (i, 0, 0)),
                   pl.BlockSpec((N_EXPERTS, 128), lambda i: (0, 0))],
        out_shape=[tok2(jnp.int32), tok2(F32),
                   jax.ShapeDtypeStruct((T // tm, N_EXPERTS, 128), F32),
                   jax.ShapeDtypeStruct((N_EXPERTS, 128), F32)],
        compiler_params=_cparams("arbitrary"),
        name="moe_router",
    )(x, g, mods, mods, w_router.T, b_router[:, None], tri)


def moe_layout(runs, totals):
    rows = totals[:, 0].astype(jnp.int32)
    tiles = (rows + MOE_TM - 1) // MOE_TM
    tile_end = jnp.cumsum(tiles)
    group = (tile_end - tiles) * MOE_TM
    run_len = runs[:, :, 0].astype(jnp.int32)
    run_src = runs[:, :, 1].astype(jnp.int32)
    run_dst = group[None, :] + runs[:, :, 2].astype(jnp.int32)
    tail = jnp.stack([group + rows, tiles * MOE_TM - rows]).astype(jnp.int32)
    n_tiles = tile_end[-1]
    t = jnp.arange(MOE_MAX_TILES, dtype=jnp.int32)
    tile_e = jnp.sum(t[:, None] >= tile_end[None, :], axis=1).astype(jnp.int32)
    last_e = jnp.sum((n_tiles - 1) >= tile_end).astype(jnp.int32)
    tile_e = jnp.where(t < n_tiles, tile_e, last_e)
    first = jnp.sum(jnp.where(tile_e[:, None] == jnp.arange(N_EXPERTS), (tile_end - tiles)[None, :], 0), axis=1)
    e_rows = jnp.sum(jnp.where(tile_e[:, None] == jnp.arange(N_EXPERTS), rows[None, :], 0), axis=1)
    tile_rows = jnp.where(t < n_tiles, jnp.clip(e_rows - (t - first) * MOE_TM, 0, MOE_TM), 0).astype(jnp.int32)
    run_tab = jnp.stack([run_len, run_src, run_dst]).reshape(3, -1)
    return run_tab, tail, tile_e, n_tiles.astype(jnp.int32).reshape(1), tile_rows


def _run_copies(tab_ref, i, local_ref, global_ref, sem, to_global):
    out = []
    for e in range(N_EXPERTS):
        k = i * N_EXPERTS + e
        n, src, dst = tab_ref[0, k], tab_ref[1, k], tab_ref[2, k]
        for size in RUN_SIZES:
            done = (n // (2 * size)) * (2 * size)
            loc = local_ref.at[pl.ds(pl.multiple_of(src + done, RUN_ALIGN), size), :]
            glo = global_ref.at[pl.ds(pl.multiple_of(dst + done, RUN_ALIGN), size), :]
            copy = pltpu.make_async_copy(loc, glo, sem) if to_global else pltpu.make_async_copy(glo, loc, sem)
            out.append(((n & size) != 0, copy))
    return out


def _start(copies, live=True):
    for pred, copy in copies:
        pl.when(pred & live)(copy.start)


def _wait(copies, live=True):
    for pred, copy in copies:
        pl.when(pred & live)(copy.wait)


def _start_then_wait(copies):
    _start(copies)
    _wait(copies)


def _dispatch_kernel(tab_ref, tail_ref, nt_ref, lp_ref, x_ref, g_ref, sh_ref, sc_ref, xs_ref, hs_scr, z_scr, sem):
    i = pl.program_id(0)
    tm = x_ref.shape[0]
    buf = i % 2
    h = _norm_mod(x_ref[...], g_ref[...], sh_ref[...], sc_ref[...]).astype(BF16)
    slot = lax.broadcasted_iota(jnp.int32, (MOE_LOCAL, tm), 0)
    perm = jnp.where((slot == lp_ref[0:1, :]) | (slot == lp_ref[1:2, :]), 1.0, 0.0).astype(BF16)
    hs_scr[buf] = jnp.dot(perm, h, preferred_element_type=F32)
    copies = _run_copies(tab_ref, i, hs_scr.at[buf], xs_ref, sem.at[buf], to_global=True)
    _start(copies)
    _wait(_run_copies(tab_ref, jnp.maximum(i - 1, 0), hs_scr.at[1 - buf], xs_ref, sem.at[1 - buf], to_global=True),
          live=i > 0)

    @pl.when(i == 0)
    def _():
        z_scr[...] = jnp.zeros_like(z_scr)
        zrows = z_scr.shape[0]

        def zero_tile(t, carry):
            for part in range(MOE_TM // zrows):
                dst = xs_ref.at[pl.ds(pl.multiple_of(t * MOE_TM + part * zrows, zrows), zrows), :]
                copy = pltpu.make_async_copy(z_scr, dst, sem.at[2])
                copy.start()
                copy.wait()
            return carry

        lax.fori_loop(nt_ref[0], MOE_MAX_TILES, zero_tile, 0)
        tails = []
        for e in range(N_EXPERTS):
            start, n = tail_ref[0, e], tail_ref[1, e]
            for size in RUN_SIZES:
                if size >= MOE_TM:
                    continue
                done = (n // (2 * size)) * (2 * size)
                dst = xs_ref.at[pl.ds(pl.multiple_of(start + done, RUN_ALIGN), size), :]
                tails.append(((n & size) != 0, pltpu.make_async_copy(z_scr.at[pl.ds(0, size), :], dst, sem.at[2])))
        _start_then_wait(tails)

    _wait(copies, live=i == pl.num_programs(0) - 1)


def moe_dispatch(x, g, mods, lp, run_tab, tail, n_tiles, tm=MOE_TOK):
    T = x.shape[0]
    n_rows = MOE_MAX_TILES * MOE_TM
    return pl.pallas_call(
        _dispatch_kernel,
        grid_spec=pltpu.PrefetchScalarGridSpec(
            num_scalar_prefetch=3,
            grid=(T // tm,),
            in_specs=[pl.BlockSpec((2, tm), lambda i, *_: (0, i)),
                      pl.BlockSpec((tm, D), lambda i, *_: (i, 0)),
                      pl.BlockSpec((1, D), lambda i, *_: (0, 0)),
                      _mod_spec(3, tm), _mod_spec(4, tm)],
            out_specs=pl.BlockSpec(memory_space=pl.ANY),
            scratch_shapes=[pltpu.VMEM((2, MOE_LOCAL, D), F32), pltpu.VMEM((MOE_TM // 2, D), F32),
                            pltpu.SemaphoreType.DMA((3,))]),
        out_shape=jax.ShapeDtypeStruct((n_rows, D), F32),
        compiler_params=_cparams("arbitrary"),
        name="moe_dispatch",
    )(run_tab, tail, n_tiles, lp, x, g, mods, mods)


def _moe_group_kernel(te_ref, nt_ref, tr_ref, x_ref, w1_ref, w3_ref, w2_ref, o_ref, h_scr, acc_scr):
    t, c = pl.program_id(0), pl.program_id(1)
    rows = tr_ref[t]
    half = MOE_TM // 2

    @pl.when(t < nt_ref[0])
    def _():
        @pl.when(c == 0)
        def _():
            h_scr[...] = x_ref[...].astype(BF16)
            acc_scr[...] = jnp.zeros_like(acc_scr)

        @pl.when(rows > half)
        def _():
            _swiglu_accumulate(h_scr, acc_scr, w1_ref, w3_ref, w2_ref, MOE_TM)

        @pl.when(rows <= half)
        def _():
            _swiglu_accumulate(h_scr, acc_scr, w1_ref, w3_ref, w2_ref, half)

        @pl.when(c == pl.num_programs(1) - 1)
        def _():
            o_ref[...] = acc_scr[...]

    @pl.when((t >= nt_ref[0]) & (c == pl.num_programs(1) - 1))
    def _():
        o_ref[...] = jnp.zeros_like(o_ref)


def moe_grouped_swiglu(xs, tile_e, n_tiles, tile_rows, w1, w3, w2, tf=256):
    nc = D_FF // tf
    live = lambda t, nt: t < nt[0]
    row = lambda t, c, te, nt, tr: (jnp.where(live(t, nt), t, nt[0] - 1), 0)
    wcol = lambda t, c, te, nt, tr: (te[t], 0, jnp.where(live(t, nt), c, nc - 1))
    wrow = lambda t, c, te, nt, tr: (te[t], jnp.where(live(t, nt), c, nc - 1), 0)
    return pl.pallas_call(
        _moe_group_kernel,
        grid_spec=pltpu.PrefetchScalarGridSpec(
            num_scalar_prefetch=3,
            grid=(MOE_MAX_TILES, nc),
            in_specs=[pl.BlockSpec((MOE_TM, D), row),
                      pl.BlockSpec((None, D, tf), wcol),
                      pl.BlockSpec((None, D, tf), wcol),
                      pl.BlockSpec((None, tf, D), wrow)],
            out_specs=pl.BlockSpec((MOE_TM, D), lambda t, c, te, nt, tr: (t, 0)),
            scratch_shapes=[pltpu.VMEM((MOE_TM, D), BF16), pltpu.VMEM((MOE_TM, D), F32)]),
        out_shape=jax.ShapeDtypeStruct(xs.shape, F32),
        compiler_params=_cparams("arbitrary", "arbitrary"),
        name="moe_grouped",
    )(tile_e, n_tiles, tile_rows, xs, w1, w3, w2)


def _combine_kernel(tab_ref, lp_ref, wt_ref, x_ref, gate_ref, ys_ref, op_ref, os_ref, yl_scr, sem, *, n_p):
    i = pl.program_id(0)
    tm = x_ref.shape[0]
    buf = i % 2
    last = pl.num_programs(0) - 1
    gather = lambda t, b: _run_copies(tab_ref, t, yl_scr.at[b], ys_ref, sem.at[b], to_global=False)
    _start(gather(i, buf), live=i == 0)
    _start(gather(jnp.minimum(i + 1, last), 1 - buf), live=i < last)
    slot = lax.broadcasted_iota(jnp.int32, (tm, MOE_LOCAL), 1)
    mix = (jnp.where(slot == lp_ref[:, 0:1], wt_ref[:, 0:1], 0.0)
           + jnp.where(slot == lp_ref[:, 1:2], wt_ref[:, 1:2], 0.0)).astype(BF16)
    _wait(gather(i, buf))
    moe = jnp.dot(mix, yl_scr[buf].astype(BF16), preferred_element_type=F32)
    out = x_ref[...] + gate_ref[...] * moe

    @pl.when(pl.program_id(0) < n_p)
    def _():
        op_ref[...] = out

    @pl.when(pl.program_id(0) >= n_p)
    def _():
        os_ref[...] = out


def moe_combine(x, mods, lp, wts, run_tab, ys, t_prompt, tm=MOE_TOK):
    T = x.shape[0]
    n_p = t_prompt // tm
    return pl.pallas_call(
        functools.partial(_combine_kernel, n_p=n_p),
        grid_spec=pltpu.PrefetchScalarGridSpec(
            num_scalar_prefetch=1,
            grid=(T // tm,),
            in_specs=[pl.BlockSpec((tm, 2), lambda i, *_: (i, 0)),
                      pl.BlockSpec((tm, 2), lambda i, *_: (i, 0)),
                      pl.BlockSpec((tm, D), lambda i, *_: (i, 0)),
                      _mod_spec(5, tm),
                      pl.BlockSpec(memory_space=pl.ANY)],
            out_specs=_part_specs((tm, D), n_p),
            scratch_shapes=[pltpu.VMEM((2, MOE_LOCAL, D), F32), pltpu.SemaphoreType.DMA((2,))]),
        out_shape=[jax.ShapeDtypeStruct((t_prompt, D), F32), jax.ShapeDtypeStruct((T - t_prompt, D), F32)],
        compiler_params=_cparams("arbitrary"),
        name="moe_combine",
    )(run_tab, lp.T, wts.T, x, mods, ys)


def kernel(x_prompt, x_sample, state_C, state_n, state_m, cache_k, cache_v, c, c_ctx, norm1_g, norm2_g, w_ada, b_ada, ev_w_in, ev_conv, hy_w1, hy_b1, hy_w2, hy_b2, hy_w3, hy_freq, hy_d, ml_b_gate, ml_norm_g, ev_w_out, ff_w1, ff_w3, ff_w2, at_w_qkv, at_q_g, at_k_g, at_sink, at_w_out, moe_w_router, moe_b_router, moe_w1, moe_w3, moe_w2):
    BP, LP, _ = x_prompt.shape
    BS, LS, _ = x_sample.shape
    TP = BP * LP
    assert TP % GROUP == 0 and TP // GROUP == N_PROMPT_GROUPS and LS == GROUP and BS == 8

    xp, xs = x_prompt.reshape(TP, D), x_sample.reshape(BS * LS, D)
    cond = jnp.concatenate([c_ctx[None], c, jnp.zeros((16 - 1 - BS, D), F32)], axis=0)
    mods = adaln_table(cond, w_ada, b_ada)

    u, gates = even_in_proj(xp, xs, norm1_g[0:1], mods[0], ev_w_in[0], ml_b_gate[0].reshape(1, N_GATES))
    hy = []
    for seq0, B, L, nb in ((0, BP, LP, 4), (TP // LS, BS, LS, 1)):
        fwd, inv = (jnp.asarray(t).astype(BF16) for t in _dft_tables(L))
        ka, kb = hyena_filter_spectra(L, hy_w1[0], hy_b1[0], hy_w2[0], hy_b2[0], hy_w3[0], hy_freq[0], fwd)
        hy.append(hyena_mix(u, seq0, B, L, ev_conv[0], hy_d[0], fwd, inv, ka, kb, nb))
    ml_p, new_C, new_n, new_m = mlstm_mix(u, gates, 0, BP, LP, ml_norm_g[0], want_state=True)
    ml_s = mlstm_mix(u, gates, TP // LS, BS, LS, ml_norm_g[0],
                     state=(state_C[:, 0], state_n[:, 0], state_m[:, 0]))
    x = proj_residual([hy, (ml_p, ml_s)], ev_w_out[0], (xp, xs), mods[0], 2)
    x = ffn_residual(x, norm2_g[0:1], mods[0], ff_w1[0], ff_w3[0], ff_w2[0])

    q, kv = qkv_proj(x, norm1_g[1:2], mods[1], at_w_qkv[0])
    o_p, new_k, new_v = attention(q, kv, 0, at_q_g[0], at_k_g[0], at_sink[0], BP, LP)
    o_s = attention(q, kv, TP, at_q_g[0], at_k_g[0], at_sink[0], BS, LS, cache=(cache_k[:, 0], cache_v[:, 0]))
    x = proj_residual([(o_p, o_s)], at_w_out[0], x, mods[1], 2)
    lp, wts, runs, totals = moe_router(x, norm2_g[1:2], mods[1], moe_w_router[0], moe_b_router[0])
    run_tab, tail, tile_e, n_tiles, tile_rows = moe_layout(runs, totals)
    xsort = moe_dispatch(x, norm2_g[1:2], mods[1], lp, run_tab, tail, n_tiles)
    ysort = moe_grouped_swiglu(xsort, tile_e, n_tiles, tile_rows, moe_w1[0], moe_w3[0], moe_w2[0])
    yp, ys = moe_combine(x, mods[1], lp, wts, run_tab, ysort, TP)

    return (yp.reshape(BP, LP, D), ys.reshape(BS, LS, D),
            new_C[:, None], new_n[:, None], new_m[:, None], new_k[:, None], new_v[:, None])
```

```python
import functools
import math

import numpy as np
import jax
import jax.numpy as jnp
from jax import lax
from jax.experimental import pallas as pl
from jax.experimental.pallas import tpu as pltpu

F32 = jnp.float32
BF16 = jnp.bfloat16
HIGHEST = lax.Precision.HIGHEST

D = 1024
GROUP = 1024
N_PROMPT_GROUPS = 8
HY_W = 512
ML_HEADS = 4
ML_HD = 128
ML_CHUNK = 256
EVEN_MAIN = 3 * HY_W + 4 * 512
N_GATES = 16
ATT_HD = 64
ATT_HEADS = 16
ATT_KV = 4
WINDOW = 128
GRID_W = 64
ROPE_BASE = 10000.0
D_FF = 2816
N_EXPERTS = 8
EPS = 1e-6
NEG = -1e30
VMEM_LIMIT = 56 * 1024 * 1024


def _cparams(*sem, flags=None):
    return pltpu.CompilerParams(dimension_semantics=sem, vmem_limit_bytes=VMEM_LIMIT, flags=flags)


def _mod_row(i, tm):
    return jnp.maximum(i * tm // GROUP - (N_PROMPT_GROUPS - 1), 0)


def _silu(x):
    return x * jax.nn.sigmoid(x)


def _bdot(a, b):
    return jnp.dot(a.astype(BF16), b.astype(BF16), preferred_element_type=F32)


def _norm_mod(x, g, sh, sc):
    y = x * lax.rsqrt(jnp.mean(x * x, axis=-1, keepdims=True) + EPS) * g
    return y * (1.0 + sc) + sh


def _adaln_kernel(c_ref, w_ref, b_ref, o_ref):
    s = _silu(c_ref[...])
    o_ref[...] = jnp.dot(s, w_ref[...], precision=HIGHEST, preferred_element_type=F32) + b_ref[...]


def adaln_table(cond, w_ada, b_ada):
    depth = w_ada.shape[0]
    tn = 1536
    out = pl.pallas_call(
        _adaln_kernel,
        grid=(depth, 6 * D // tn),
        in_specs=[pl.BlockSpec((16, D), lambda l, j: (0, 0)),
                  pl.BlockSpec((None, D, tn), lambda l, j: (l, 0, j)),
                  pl.BlockSpec((None, 1, tn), lambda l, j: (l, 0, j))],
        out_specs=pl.BlockSpec((None, 16, tn), lambda l, j: (l, 0, j)),
        out_shape=jax.ShapeDtypeStruct((depth, 16, 6 * D), F32),
        compiler_params=_cparams("parallel", "parallel"),
        name="adaln",
    )(cond, w_ada, b_ada.reshape(depth, 1, 6 * D))
    return out.reshape(depth, 16, 1, 6 * D)


def _part_specs(block, n_p):
    return [pl.BlockSpec(block, lambda i, *_: (jnp.minimum(i, n_p - 1), 0)),
            pl.BlockSpec(block, lambda i, *_: (jnp.maximum(i - n_p, 0), 0))]


def _pick(is_prompt, p_ref, s_ref):
    return jnp.where(is_prompt, p_ref[...], s_ref[...])


def _mod_spec(k, tm):
    return pl.BlockSpec((None, 1, D), lambda i, *_: (_mod_row(i, tm), 0, k))


def _log_sigmoid(x):
    return jnp.minimum(x, 0.0) - jnp.log(1.0 + jnp.exp(-jnp.abs(x)))


def _split3(x):
    hi = x.astype(BF16)
    r = x - hi.astype(F32)
    mid = r.astype(BF16)
    return hi, mid, (r - mid.astype(F32)).astype(BF16)


def _even_in_kernel(xp_ref, xs_ref, g_ref, sh_ref, sc_ref, w_ref, wg_ref, bg_ref, lo_ref, up_ref, u_ref, gate_ref,
                    h_scr, *, n_p):
    is_prompt = pl.program_id(0) < n_p

    @pl.when(pl.program_id(1) == 0)
    def _():
        h = _norm_mod(_pick(is_prompt, xp_ref, xs_ref), g_ref[...], sh_ref[...], sc_ref[...]).astype(BF16)
        h_scr[...] = h
        gates = _bdot(h, wg_ref[...]) + bg_ref[...]
        lf = _log_sigmoid(gates)
        col = lax.broadcasted_iota(jnp.int32, (1, N_GATES), 1)
        is_forget = (col // ML_HEADS) % 2 == 1
        is_rev = col >= N_GATES // 2
        for ch in range(h.shape[0] // ML_CHUNK):
            sl = slice(ch * ML_CHUNK, (ch + 1) * ML_CHUNK)
            parts = _split3(lf[sl])
            cf = sum(jnp.dot(lo_ref[...], p, preferred_element_type=F32) for p in parts)
            cr = sum(jnp.dot(up_ref[...], p, preferred_element_type=F32) for p in parts)
            gate_ref[sl, :] = jnp.where(is_forget, jnp.where(is_rev, cr, cf), gates[sl])

    u_ref[...] = _bdot(h_scr[...], w_ref[...]).astype(u_ref.dtype)


def even_in_proj(xp, xs, g, mods, w_in, b_gate, tm=1024, tn=512):
    T = xp.shape[0] + xs.shape[0]
    tri = np.tril(np.ones((ML_CHUNK, ML_CHUNK), np.float32))
    lo, up = jnp.asarray(tri).astype(BF16), jnp.asarray(tri.T).astype(BF16)
    return pl.pallas_call(
        functools.partial(_even_in_kernel, n_p=xp.shape[0] // tm),
        grid=(T // tm, EVEN_MAIN // tn),
        in_specs=_part_specs((tm, D), xp.shape[0] // tm) + [
                  pl.BlockSpec((1, D), lambda i, j: (0, 0)),
                  _mod_spec(0, tm), _mod_spec(1, tm),
                  pl.BlockSpec((D, tn), lambda i, j: (0, j)),
                  pl.BlockSpec((D, N_GATES), lambda i, j: (0, 0)),
                  pl.BlockSpec((1, N_GATES), lambda i, j: (0, 0)),
                  pl.BlockSpec((ML_CHUNK, ML_CHUNK), lambda i, j: (0, 0)),
                  pl.BlockSpec((ML_CHUNK, ML_CHUNK), lambda i, j: (0, 0))],
        out_specs=[pl.BlockSpec((tm, tn), lambda i, j: (i, j)),
                   pl.BlockSpec((tm, N_GATES), lambda i, j: (i, 0))],
        out_shape=[jax.ShapeDtypeStruct((T, EVEN_MAIN), BF16),
                   jax.ShapeDtypeStruct((T, N_GATES), F32)],
        scratch_shapes=[pltpu.VMEM((tm, D), BF16)],
        compiler_params=_cparams("parallel", "arbitrary"),
        name="even_in_proj",
    )(xp, xs, g, mods, mods, w_in, w_in[:, EVEN_MAIN:], b_gate, lo, up)


def _dft_tables(L):
    n = 2 * L
    f = np.arange(L, dtype=np.int64)[:, None]
    s = np.arange(L, dtype=np.int64)[None, :]
    ang = 2.0 * np.pi * ((f * s) % n).astype(np.float64) / n
    fwd = np.concatenate([np.cos(ang), -np.sin(ang)], axis=0)
    fwd[L, :] = np.where(np.arange(L) % 2 == 0, 1.0, -1.0)
    t = np.arange(L, dtype=np.int64)[:, None]
    ff = np.arange(L, dtype=np.int64)[None, :]
    ang = 2.0 * np.pi * ((t * ff) % n).astype(np.float64) / n
    inv_re = 2.0 * np.cos(ang) / n
    inv_re[:, 0] = 1.0 / n
    inv_im = -2.0 * np.sin(ang) / n
    inv_im[:, 0] = np.where(np.arange(L) % 2 == 0, 1.0, -1.0) / n
    inv = np.concatenate([inv_re, inv_im], axis=1)
    return fwd.astype(np.float32), inv.astype(np.float32)


def _filter_tables(L):
    t = np.linspace(0.0, 1.0, L, dtype=np.float32).astype(np.float64)[:, None]
    w = 2.0 * math.pi * np.arange(L, dtype=np.float64)[:, None] / L
    bands = np.linspace(1e-4, 16 - 1, 16, dtype=np.float32).astype(np.float64)[None, :]
    z = np.concatenate([t, np.cos(bands * w), -np.sin(bands * w)], axis=-1)
    zp = np.zeros((L, 128), np.float64)
    zp[:, :z.shape[1]] = z
    max_decay = math.log(1e-2) / 0.3
    min_decay = math.log(1e-2) / 1.5
    deltas = np.linspace(min_decay, max_decay, HY_W, dtype=np.float32).astype(np.float64)
    decay = np.exp(-t * np.abs(deltas))
    return zp.astype(np.float32), decay.astype(np.float32)


def _hy_filter_kernel(z_ref, dec_ref, w1_ref, b1_ref, w2_ref, b2_ref, w3_ref, fr_ref, fwd_ref,
                      ka_ref, kb_ref):
    L = z_ref.shape[0]
    hdot = functools.partial(jnp.dot, precision=HIGHEST, preferred_element_type=F32)
    h = jnp.sin(fr_ref[0:1, :] * (hdot(z_ref[...], w1_ref[...]) + b1_ref[...]))
    h = jnp.sin(fr_ref[1:2, :] * (hdot(h, w2_ref[...]) + b2_ref[...]))
    h = hdot(h, w3_ref[...])
    row0 = lax.broadcasted_iota(jnp.int32, (L, 1), 0) == 0
    h0 = h[:, :HY_W] * dec_ref[...]
    h1 = h[:, HY_W:] * dec_ref[...]
    l1 = jnp.sum(jnp.abs(h0), axis=0, keepdims=True) + jnp.sum(jnp.abs(h1), axis=0, keepdims=True)
    inv = 1.0 / l1
    h0 = h0 * inv
    h1 = jnp.where(row0, 0.0, h1 * inv)
    f0 = _bdot(fwd_ref[...], h0)
    f1 = _bdot(fwd_ref[...], h1)
    ka_ref[...] = f0[:L] + f1[:L]
    kb_ref[...] = jnp.where(row0, f0[L:] + f1[L:], f0[L:] - f1[L:])


def _const_spec(a, n_grid):
    return pl.BlockSpec(a.shape, lambda *_: (0,) * a.ndim, pipeline_mode=pl.Buffered(1))


def hyena_filter_spectra(L, w1, b1, w2, b2, w3, freq, fwd):
    z, dec = _filter_tables(L)
    pad2 = lambda a, r, c: jnp.pad(a, ((0, r - a.shape[0]), (0, c - a.shape[1])))
    args = (jnp.asarray(z), jnp.asarray(dec), pad2(w1, 128, 128), pad2(b1[None], 1, 128),
            pad2(w2, 128, 128), pad2(b2[None], 1, 128), pad2(w3, 128, 4 * HY_W), pad2(freq, 2, 128), fwd)
    in_specs = [_const_spec(a, 1) for a in args]
    in_specs[6] = pl.BlockSpec((128, 2 * HY_W), lambda o: (0, o))
    shp = jax.ShapeDtypeStruct((2, L, HY_W), F32)
    out_spec = pl.BlockSpec((None, L, HY_W), lambda o: (o, 0, 0))
    return pl.pallas_call(
        _hy_filter_kernel,
        grid=(2,),
        in_specs=in_specs,
        out_specs=[out_spec, out_spec],
        out_shape=[shp, shp],
        compiler_params=_cparams("arbitrary"),
        name=f"hyena_filter_{L}",
    )(*args)


def _hyena_kernel(u_ref, cw_ref, d_ref, fwd_ref, inv_ref, ka_ref, kb_ref, o_ref):
    nb, L = u_ref.shape[0], u_ref.shape[1]
    row = lax.broadcasted_iota(jnp.int32, (L, 1), 0)
    first, last = row == 0, row == L - 1
    fwd = fwd_ref[...].astype(BF16)
    inv = inv_ref[...].astype(BF16)

    def long_conv(z, o):
        zf = jnp.dot(fwd, z.astype(BF16), preferred_element_type=F32)
        a, b = zf[:L], zf[L:]
        ka, kb = ka_ref[o], kb_ref[o]
        yr = a * ka - jnp.where(first, 0.0, b * kb)
        yi = jnp.where(first, b * kb, a * kb + b * ka)
        return (jnp.dot(inv[:, :L], yr.astype(BF16), preferred_element_type=F32)
                + jnp.dot(inv[:, L:], yi.astype(BF16), preferred_element_type=F32))

    for bi in range(nb):
        u = u_ref[bi].astype(F32)
        prev = jnp.where(first, 0.0, pltpu.roll(u, 1, 0))
        nxt = jnp.where(last, 0.0, pltpu.roll(u, L - 1, 0))
        u = prev * cw_ref[0:1, :] + u * cw_ref[1:2, :] + nxt * cw_ref[2:3, :]
        v, x1, x2 = u[:, :HY_W], u[:, HY_W:2 * HY_W], u[:, 2 * HY_W:]
        z = x1 * (long_conv(v, 0) + d_ref[0:1, :] * v)
        z = x2 * (long_conv(z, 1) + d_ref[1:2, :] * z)
        o_ref[bi] = z.astype(o_ref.dtype)


def hyena_mix(u, seq0, B, L, conv_w, d_skip, fwd, inv, ka, kb, nb):
    u3 = u.reshape(-1, L, EVEN_MAIN)
    full = lambda a: _const_spec(a, 1)
    out = pl.pallas_call(
        _hyena_kernel,
        grid=(B // nb,),
        in_specs=[pl.BlockSpec((nb, L, 3 * HY_W), lambda b: (b + seq0 // nb, 0, 0)),
                  full(conv_w), full(d_skip), full(fwd), full(inv), full(ka), full(kb)],
        out_specs=pl.BlockSpec((nb, L, HY_W), lambda b: (b, 0, 0)),
        out_shape=jax.ShapeDtypeStruct((B, L, HY_W), BF16),
        compiler_params=_cparams("parallel"),
        name=f"hyena_{L}",
    )(u3, conv_w, d_skip, fwd, inv, ka, kb)
    return out.reshape(B * L, HY_W)


def _mlstm_kernel(*refs, has_state, want_state):
    q_ref, k_ref, v_ref, o_ref, gc_ref, gr_ref, ng_ref = refs[:7]
    refs = refs[7:]
    if has_state:
        c0t_ref, n0b_ref, m0_ref = refs[:3]
        refs = refs[3:]
    y_ref = refs[0]
    if want_state:
        c_out, n_out, m_out = refs[1:4]
    L, d = q_ref.shape[0], ML_HD
    T = min(ML_CHUNK, L)
    nc = L // T
    scale = 1.0 / math.sqrt(d)
    nt = (((1,), (1,)), ((), ()))
    si = lax.broadcasted_iota(jnp.int32, (T, T), 0)
    ti = lax.broadcasted_iota(jnp.int32, (T, T), 1)
    allowed = (si <= ti, si >= ti)
    chains = [(dr, h) for dr in range(2) for h in range(ML_HEADS)]
    gcol = lambda dr, gi, h: dr * 2 * ML_HEADS + gi * ML_HEADS + h

    caug_t, m = {}, {}
    for ch in chains:
        dr, h = ch
        if has_state:
            caug_t[ch] = jnp.concatenate([c0t_ref[dr, h], n0b_ref[dr, h]], axis=0)
            m[ch] = m0_ref[dr, h:h + 1, 0:1]
        else:
            caug_t[ch], m[ch] = jnp.zeros((2 * d, d), F32), jnp.zeros((1, 1), F32)

    chunk_cache = {}

    def chunk_data(h, j):
        if (h, j) not in chunk_cache:
            sl, hl = slice(j * T, (j + 1) * T), slice(h * d, (h + 1) * d)
            q = q_ref[sl, hl]
            ks = (k_ref[sl, hl].astype(F32) * scale).astype(BF16)
            v_t = v_ref[sl, hl].astype(F32).T
            vaug_t = jnp.concatenate([v_t, jnp.ones((d, T), F32)], axis=0).astype(BF16)
            s_raw = lax.dot_general(ks, q, nt, preferred_element_type=F32)
            chunk_cache[(h, j)] = (q, ks, v_t, vaug_t, s_raw)
        return chunk_cache[(h, j)]

    h_sum = {}
    for it in range(nc):
        step = {ch: (it if ch[0] == 0 else nc - 1 - it) for ch in chains}
        data = {ch: chunk_data(ch[1], step[ch]) for ch in chains}
        inter_t = {ch: lax.dot_general(caug_t[ch].astype(BF16), data[ch][0], nt, preferred_element_type=F32)
                   for ch in chains}
        gate = {}
        for ch in chains:
            dr, h = ch
            sl = slice(step[ch] * T, (step[ch] + 1) * T)
            li_r, b_r = gr_ref[gcol(dr, 0, h):gcol(dr, 0, h) + 1, sl], gr_ref[gcol(dr, 1, h):gcol(dr, 1, h) + 1, sl]
            src = gc_ref[sl, gcol(dr, 0, h):gcol(dr, 0, h) + 1] - gc_ref[sl, gcol(dr, 1, h):gcol(dr, 1, h) + 1]
            dm = jnp.where(allowed[dr], src + b_r, NEG)
            inter = b_r + m[ch]
            m_t = jnp.maximum(inter, jnp.max(dm, axis=0, keepdims=True))
            b_end = b_r[:, T - 1:T] if dr == 0 else b_r[:, 0:1]
            g_r = b_end - b_r + li_r
            m_new = jnp.maximum(b_end + m[ch], jnp.max(g_r, axis=1, keepdims=True))
            gate[ch] = (jnp.exp(dm - m_t), jnp.exp(inter - m_t), jnp.exp(-m_t), jnp.exp(g_r - m_new),
                        jnp.exp(b_end + m[ch] - m_new), m_new)
        for ch in chains:
            q, ks, v_t, vaug_t, s_raw = data[ch]
            w_intra, w_inter, floor, w_tok, decay, m_new = gate[ch]
            acc = jnp.dot(vaug_t, (s_raw * w_intra).astype(BF16), preferred_element_type=F32) + w_inter * inter_t[ch]
            h_t = acc[:d] / jnp.maximum(jnp.abs(acc[d:]), floor)
            key = (ch[1], step[ch])
            h_sum[key] = h_t if key not in h_sum else h_sum[key] + h_t
            vw_t = jnp.concatenate([v_t * w_tok, jnp.broadcast_to(w_tok, (d, T))], axis=0).astype(BF16)
            caug_t[ch] = decay * caug_t[ch] + jnp.dot(vw_t, ks, preferred_element_type=F32)
            m[ch] = m_new

    if want_state:
        for ch in chains:
            dr, h = ch
            c_out[dr, h] = caug_t[ch][:d].T
            n_out[dr, h:h + 1, :] = caug_t[ch][d:d + 1, :]
            m_out[dr, h:h + 1, :] = jnp.broadcast_to(m[ch], (1, d))
    for h in range(ML_HEADS):
        for j in range(nc):
            sl, hl = slice(j * T, (j + 1) * T), slice(h * d, (h + 1) * d)
            hv = h_sum[(h, j)].T
            y = hv * lax.rsqrt(jnp.mean(hv * hv, axis=-1, keepdims=True) + EPS) * ng_ref[:, hl]
            y_ref[sl, hl] = (y * jax.nn.sigmoid(o_ref[sl, hl].astype(F32))).astype(y_ref.dtype)


def mlstm_mix(u, gates, seq0, B, L, norm_g, state=None, want_state=False):
    u3 = u.reshape(-1, L, EVEN_MAIN)
    gc = gates.reshape(-1, L, N_GATES)[seq0:seq0 + B]
    gr = gc.transpose(0, 2, 1)
    width = ML_HEADS * ML_HD
    col = lambda i: pl.BlockSpec((None, L, width), lambda b: (b + seq0, 0, (3 * HY_W + i * width) // width))
    in_specs = [col(0), col(1), col(2), col(3),
                pl.BlockSpec((None, L, N_GATES), lambda b: (b, 0, 0)),
                pl.BlockSpec((None, N_GATES, L), lambda b: (b, 0, 0)),
                pl.BlockSpec((1, width), lambda b: (0, 0))]
    args = [u3, u3, u3, u3, gc, gr, norm_g.reshape(1, width)]
    sspec = pl.BlockSpec((None, 2, ML_HEADS, ML_HD, ML_HD), lambda b: (b, 0, 0, 0, 0))
    vspec = pl.BlockSpec((None, 2, ML_HEADS, ML_HD), lambda b: (b, 0, 0, 0))
    if state is not None:
        C0, n0, m0 = state
        in_specs += [sspec, sspec, vspec]
        args += [C0.swapaxes(-1, -2), jnp.broadcast_to(n0[..., None, :], C0.shape),
                 jnp.broadcast_to(m0[..., None], n0.shape)]
    out_specs = [pl.BlockSpec((None, L, width), lambda b: (b, 0, 0))]
    out_shape = [jax.ShapeDtypeStruct((B, L, width), BF16)]
    if want_state:
        out_specs += [sspec, vspec, vspec]
        out_shape += [jax.ShapeDtypeStruct((B, 2, ML_HEADS, ML_HD, ML_HD), F32),
                      jax.ShapeDtypeStruct((B, 2, ML_HEADS, ML_HD), F32),
                      jax.ShapeDtypeStruct((B, 2, ML_HEADS, ML_HD), F32)]
    outs = pl.pallas_call(
        functools.partial(_mlstm_kernel, has_state=state is not None, want_state=want_state),
        grid=(B,),
        in_specs=in_specs, out_specs=out_specs, out_shape=out_shape,
        compiler_params=_cparams("parallel"),
        name=f"mlstm_{L}",
    )(*args)
    y = outs[0].reshape(B * L, width)
    if not want_state:
        return y
    _, C, n, m = outs
    return y, C, n, m[..., 0]


def _proj_inputs(acts, w, x, mods, tm):
    xs = tuple(x) if isinstance(x, (tuple, list)) else (x,)
    n_p = acts[0][0].shape[0] // tm
    specs = []
    for pair in acts:
        specs += _part_specs((tm, pair[0].shape[1]), n_p)
    specs.append(pl.BlockSpec(w.shape, lambda *_: (0, 0), pipeline_mode=pl.Buffered(1)))
    specs += _part_specs((tm, D), n_p) if len(xs) == 2 else [pl.BlockSpec((tm, D), lambda i, *_: (i, 0))]
    specs.append(_mod_spec(2, tm))
    args = [a for pair in acts for a in pair] + [w, *xs, mods]
    return specs, args, dict(n_in=len(acts), n_x=len(xs), n_p=n_p)


def _proj_value(refs, is_prompt, n_in, n_x, n_p):
    a_refs = refs[:2 * n_in]
    w_ref = refs[2 * n_in]
    x_refs = refs[2 * n_in + 1:2 * n_in + 1 + n_x]
    gate_ref = refs[2 * n_in + 1 + n_x]
    k0 = 0
    acc = None
    for j in range(n_in):
        a = _pick(is_prompt, a_refs[2 * j], a_refs[2 * j + 1])
        kw = a.shape[1]
        part = _bdot(a, w_ref[k0:k0 + kw, :])
        acc = part if acc is None else acc + part
        k0 += kw
    x = _pick(is_prompt, *x_refs) if n_x == 2 else x_refs[0][...]
    return x + gate_ref[...] * acc


def _n_proj_refs(n_in, n_x, n_p):
    return 2 * n_in + 1 + n_x + 1


SWIGLU_ROWS = 512


def _swiglu_accumulate(h_scr, acc_scr, w1_ref, w3_ref, w2_ref, rows, scale=None):
    w1, w3, w2 = w1_ref[...].astype(BF16), w3_ref[...].astype(BF16), w2_ref[...].astype(BF16)
    groups = [slice(r, r + SWIGLU_ROWS) for r in range(0, rows, SWIGLU_ROWS)]
    ups = []
    for sl in groups:
        h = h_scr[sl, :]
        ups.append((jnp.dot(h, w1, preferred_element_type=F32), jnp.dot(h, w3, preferred_element_type=F32)))
    for sl, (a, b) in zip(groups, ups):
        mid = (_silu(a) * b).astype(BF16)
        down = jnp.dot(mid, w2, preferred_element_type=F32)
        acc_scr[sl, :] += down if scale is None else scale * down


def _ffn_kernel(*refs, proj):
    n = _n_proj_refs(**proj)
    g_ref, sh_ref, sc_ref, gate_ref, w1_ref, w3_ref, w2_ref, o_ref, h_scr = refs[n:]
    c = pl.program_id(1)
    is_prompt = pl.program_id(0) < proj["n_p"]

    @pl.when(c == 0)
    def _():
        x = _proj_value(refs[:n], is_prompt, **proj)
        o_ref[...] = x
        h_scr[...] = _norm_mod(x, g_ref[...], sh_ref[...], sc_ref[...]).astype(BF16)

    _swiglu_accumulate(h_scr, o_ref, w1_ref, w3_ref, w2_ref, h_scr.shape[0], scale=gate_ref[...])


def ffn_residual(acts, w_out, x, g, mods, w1, w3, w2, tm=1024, tf=256):
    T = sum(a.shape[0] for a in acts[0])
    p_specs, p_args, proj = _proj_inputs(acts, w_out, x, mods, tm)
    return pl.pallas_call(
        functools.partial(_ffn_kernel, proj=proj),
        grid=(T // tm, D_FF // tf),
        in_specs=p_specs + [
                  pl.BlockSpec((1, D), lambda i, c: (0, 0)),
                  _mod_spec(3, tm), _mod_spec(4, tm), _mod_spec(5, tm),
                  pl.BlockSpec((D, tf), lambda i, c: (0, c)),
                  pl.BlockSpec((D, tf), lambda i, c: (0, c)),
                  pl.BlockSpec((tf, D), lambda i, c: (c, 0))],
        out_specs=pl.BlockSpec((tm, D), lambda i, c: (i, 0)),
        out_shape=jax.ShapeDtypeStruct((T, D), F32),
        scratch_shapes=[pltpu.VMEM((tm, D), BF16)],
        compiler_params=_cparams("parallel", "arbitrary"),
        name="ffn",
    )(*p_args, g, mods, mods, mods, w1, w3, w2)


def _qkv_kernel(x_ref, g_ref, sh_ref, sc_ref, w_ref, q_ref, kv_ref, h_scr):
    j = pl.program_id(1)

    @pl.when(j == 0)
    def _():
        h_scr[...] = _norm_mod(x_ref[...], g_ref[...], sh_ref[...], sc_ref[...]).astype(BF16)

    @pl.when(j < 2)
    def _():
        q_ref[...] = _bdot(h_scr[...], w_ref[...]).astype(q_ref.dtype)

    @pl.when(j == 2)
    def _():
        kv_ref[...] = _bdot(h_scr[...], w_ref[...])


def qkv_proj(x, g, mods, w_qkv, tm=1024):
    T = x.shape[0]
    tn = 512
    return pl.pallas_call(
        _qkv_kernel,
        grid=(T // tm, 3),
        in_specs=[pl.BlockSpec((tm, D), lambda i, j: (i, 0)),
                  pl.BlockSpec((1, D), lambda i, j: (0, 0)),
                  _mod_spec(0, tm), _mod_spec(1, tm),
                  pl.BlockSpec((D, tn), lambda i, j: (0, j))],
        out_specs=[pl.BlockSpec((tm, tn), lambda i, j: (i, jnp.minimum(j, 1))),
                   pl.BlockSpec((tm, tn), lambda i, j: (i, 0))],
        out_shape=[jax.ShapeDtypeStruct((T, ATT_HEADS * ATT_HD), BF16),
                   jax.ShapeDtypeStruct((T, 2 * ATT_KV * ATT_HD), F32)],
        scratch_shapes=[pltpu.VMEM((tm, D), BF16)],
        compiler_params=_cparams("parallel", "arbitrary"),
        name="qkv_proj",
    )(x, g, mods, mods, w_qkv)


def _rope_tables(L):
    half = ATT_HD // 2
    pos_r = (np.arange(L) // GRID_W).astype(np.float32)
    pos_c = (np.arange(L) % GRID_W).astype(np.float32)
    inv = (ROPE_BASE ** (-np.arange(0, half, 2, dtype=np.float32) / half)).astype(np.float32)
    cos = np.zeros((L, ATT_HD), np.float64)
    sin = np.zeros((L, ATT_HD), np.float64)
    for base, pos in ((0, pos_r), (half, pos_c)):
        ang = (pos[:, None] * inv[None, :]).astype(np.float32).astype(np.float64)
        cos[:, base:base + half] = np.concatenate([np.cos(ang), np.cos(ang)], axis=1)
        sin[:, base:base + half] = np.concatenate([-np.sin(ang), np.sin(ang)], axis=1)
    return (np.tile(cos, (1, 4)).astype(np.float32), np.tile(sin, (1, 4)).astype(np.float32))


def _seg_rms(x):
    w = x.shape[1]
    ri = lax.broadcasted_iota(jnp.int32, (w, w), 0) // ATT_HD
    ci = lax.broadcasted_iota(jnp.int32, (w, w), 1) // ATT_HD
    ss = _bdot(x * x, (ri == ci).astype(F32))
    return x * lax.rsqrt(ss * (1.0 / ATT_HD) + EPS)


LOG2E = 1.4426950408889634


def _exp2_bf16(x):
    return jnp.exp2(x.astype(BF16))


def _both_halves(tile, low):
    lane = lax.broadcasted_iota(jnp.int32, tile.shape, 1)
    other = pltpu.roll(tile, ATT_HD, 1)
    return jnp.where((lane < ATT_HD) == low, tile, other)


def _swap16(x):
    w = x.shape[1]
    lane = lax.broadcasted_iota(jnp.int32, x.shape, 1)
    return jnp.where(lane % 32 < 16, pltpu.roll(x, w - 16, 1), pltpu.roll(x, 16, 1))


def _attn_kernel(*refs, latent, tq):
    if latent:
        (q_ref, kv_ref, ck_ref, cv_ref, qg_ref, kg_ref, sink_ref, cosq_ref, sinq_ref, cosk_ref, sink_t_ref,
         o_ref, kk_scr, vt_scr, ckk_scr, cvt_scr) = refs
    else:
        q_ref, kv_ref, qg_ref, kg_ref, sink_ref, o_ref, ko_ref, vo_ref, kk_scr, vt_scr = refs
    L = kv_ref.shape[0]
    gw = ATT_KV * ATT_HD
    pw = 2 * ATT_HD
    qb = pl.program_id(1)

    vrows = vt_scr.shape[2]
    nblk = L // pw

    def vt_aug(tile, low):
        vt = tile.T[0:ATT_HD, :] if low else tile.T[ATT_HD:, :]
        return jnp.concatenate([vt, jnp.ones((vrows - ATT_HD, tile.shape[0]), F32)], axis=0).astype(BF16)

    @pl.when(qb == 0)
    def _():
        kn = _seg_rms(kv_ref[:, :gw]) * kg_ref[...]
        v = kv_ref[:, gw:]
        if latent:
            kn = kn * cosk_ref[...] + _swap16(kn) * sink_t_ref[...]
        else:
            for c in range(ATT_KV):
                ko_ref[c] = kn[:, c * ATT_HD:(c + 1) * ATT_HD]
                vo_ref[c] = v[:, c * ATT_HD:(c + 1) * ATT_HD]
        for c in range(ATT_KV):
            tile, low = slice((c // 2) * pw, (c // 2 + 1) * pw), c % 2 == 0
            kk_scr[c] = _both_halves(kn[:, tile], low).astype(BF16)
            for j in range(nblk):
                vt_scr[c, j] = vt_aug(v[j * pw:(j + 1) * pw, tile], low)
            if latent:
                ck, cv = ck_ref[c], cv_ref[c]
                ckk_scr[c] = jnp.concatenate([ck, ck], axis=1).astype(BF16)
                cvt_scr[c] = vt_aug(jnp.concatenate([cv, cv], axis=1), True)

    if latent:
        span = tq + 2 * WINDOW
        start = pl.multiple_of(jnp.clip(qb * tq - WINDOW, 0, L - span), WINDOW)
        blk0 = start // pw
        s_pos = start + lax.broadcasted_iota(jnp.int32, (span, tq), 0)
        t_pos = qb * tq + lax.broadcasted_iota(jnp.int32, (span, tq), 1)
        win_bias = jnp.where(jnp.abs(t_pos - s_pos) <= WINDOW, 0.0, NEG)
    else:
        span, blk0 = L, 0

    nt = (((1,), (1,)), ((), ()))
    low_q = lax.broadcasted_iota(jnp.int32, (tq, pw), 1) < ATT_HD
    def group_scores(c):
        qc = _seg_rms(q_ref[:, c * gw:(c + 1) * gw].astype(F32)) * qg_ref[...]
        if latent:
            qc = qc * cosq_ref[...] + _swap16(qc) * sinq_ref[...]
            kw = kk_scr[c, pl.ds(start, span), :]
        else:
            kw = kk_scr[c]
        qc = qc * (LOG2E / math.sqrt(ATT_HD))
        scores = []
        for g in range(ATT_KV):
            qt = qc[:, (g // 2) * pw:(g // 2 + 1) * pw]
            qm = jnp.where(low_q if g % 2 == 0 else ~low_q, qt, 0.0).astype(BF16)
            lw = lax.dot_general(kw, qm, nt, preferred_element_type=F32)
            lc = lax.dot_general(ckk_scr[c], qm, nt, preferred_element_type=F32) if latent else None
            scores.append((lw, lc))
        return scores

    def group_outputs(c, scores):
        vw = jnp.concatenate([vt_scr[c, blk0 + j] for j in range(span // pw)], axis=1)
        outs = []
        for g in range(ATT_KV):
            head = c * ATT_KV + g
            sink = sink_ref[:, head:head + 1] * LOG2E
            lw, lc = scores[g]
            if latent:
                lw = lw + win_bias
                mx = jnp.maximum(jnp.maximum(jnp.max(lw, axis=0, keepdims=True),
                                             jnp.max(lc, axis=0, keepdims=True)), sink)
                r = jnp.dot(vw, _exp2_bf16(lw - mx), preferred_element_type=F32) + jnp.dot(
                    cvt_scr[c], _exp2_bf16(lc - mx), preferred_element_type=F32)
            else:
                mx = jnp.maximum(jnp.max(lw, axis=0, keepdims=True), sink)
                r = jnp.dot(vw, _exp2_bf16(lw - mx), preferred_element_type=F32)
            den = r[ATT_HD:ATT_HD + 1, :] + jnp.exp2(sink - mx)
            outs.append(r[0:ATT_HD, :] / den)
        for t in range(2):
            o_ref[:, c * gw + t * pw:c * gw + (t + 1) * pw] = (
                jnp.concatenate(outs[2 * t:2 * t + 2], axis=0).T.astype(o_ref.dtype))

    scores = group_scores(0)
    for c in range(ATT_KV):
        nxt = group_scores(c + 1) if c + 1 < ATT_KV else None
        group_outputs(c, scores)
        scores = nxt


def attention(q, kv, row0, q_g, k_g, sink, B, L, cache=None, tq=256):
    latent = cache is not None
    gw = ATT_KV * ATT_HD
    qg = jnp.tile(q_g, ATT_KV)[None]
    kg = jnp.tile(k_g, ATT_KV)[None]
    nq = L // tq
    const = lambda a: pl.BlockSpec(a.shape, lambda b, i: (0,) * a.ndim)
    in_specs = [pl.BlockSpec((tq, ATT_HEADS * ATT_HD), lambda b, i: (row0 // tq + b * nq + i, 0)),
                pl.BlockSpec((L, 2 * gw), lambda b, i: (row0 // L + b, 0))]
    args = [q, kv]
    vrows = ATT_HD + 16
    scratch = [pltpu.VMEM((ATT_KV, L, 2 * ATT_HD), BF16),
               pltpu.VMEM((ATT_KV, L // (2 * ATT_HD), vrows, 2 * ATT_HD), BF16)]
    out_specs = [pl.BlockSpec((tq, ATT_HEADS * ATT_HD), lambda b, i: (b * nq + i, 0))]
    out_shape = [jax.ShapeDtypeStruct((B * L, ATT_HEADS * ATT_HD), BF16)]
    if latent:
        ck, cv = cache
        P = ck.shape[2]
        cos, sin = (jnp.asarray(t) for t in _rope_tables(L))
        in_specs += [pl.BlockSpec((None, ATT_KV, P, ATT_HD), lambda b, i: (b, 0, 0, 0))] * 2
        args += [ck, cv]
        in_specs += [const(qg), const(kg), pl.BlockSpec((1, ATT_HEADS), lambda b, i: (0, 0)),
                     pl.BlockSpec((tq, gw), lambda b, i: (i, 0)), pl.BlockSpec((tq, gw), lambda b, i: (i, 0)),
                     const(cos), const(sin)]
        args += [qg, kg, sink[None], cos, sin, cos, sin]
        scratch += [pltpu.VMEM((ATT_KV, P, 2 * ATT_HD), BF16), pltpu.VMEM((ATT_KV, vrows, P), BF16)]
    else:
        in_specs += [const(qg), const(kg), pl.BlockSpec((1, ATT_HEADS), lambda b, i: (0, 0))]
        args += [qg, kg, sink[None]]
        cache_spec = pl.BlockSpec((None, ATT_KV, L, ATT_HD), lambda b, i: (b, 0, 0, 0))
        out_specs += [cache_spec, cache_spec]
        out_shape += [jax.ShapeDtypeStruct((B, ATT_KV, L, ATT_HD), F32)] * 2
    outs = pl.pallas_call(
        functools.partial(_attn_kernel, latent=latent, tq=tq),
        grid=(B, nq),
        in_specs=in_specs, out_specs=out_specs, out_shape=out_shape,
        scratch_shapes=scratch,
        compiler_params=_cparams("parallel", "arbitrary"),
        name="attn_latent" if latent else "attn_context",
    )(*args)
    return outs[0] if latent else outs


MOE_TM = 1024
MOE_TOK = 1024
RUN_ALIGN = 16
MOE_LOCAL = 2 * MOE_TOK + N_EXPERTS * RUN_ALIGN
MOE_MAX_TILES = (2 * 16384 + (16384 // MOE_TOK) * N_EXPERTS * (RUN_ALIGN - 1)) // MOE_TM + N_EXPERTS + 1
RUN_SIZES = tuple(RUN_ALIGN << b for b in range(7, -1, -1))


def _router_kernel(*refs, proj):
    n = _n_proj_refs(**proj)
    g_ref, sh_ref, sc_ref, wr_ref, br_ref, tri_ref, x_ref, lp_ref, wts_ref, runs_ref, cnt_ref = refs[n:]

    @pl.when(pl.program_id(0) == 0)
    def _():
        cnt_ref[...] = jnp.zeros_like(cnt_ref)

    x = _proj_value(refs[:n], pl.program_id(0) < proj["n_p"], **proj)
    x_ref[...] = x
    h = _norm_mod(x, g_ref[...], sh_ref[...], sc_ref[...])
    lg = lax.dot_general(wr_ref[...], h, (((1,), (1,)), ((), ())), precision=HIGHEST,
                         preferred_element_type=F32) + br_ref[...]
    row = lax.broadcasted_iota(jnp.int32, lg.shape, 0)
    m1 = jnp.max(lg, axis=0, keepdims=True)
    i1 = jnp.min(jnp.where(lg == m1, row, N_EXPERTS), axis=0, keepdims=True)
    l2 = jnp.where(row == i1, -jnp.inf, lg)
    m2 = jnp.max(l2, axis=0, keepdims=True)
    i2 = jnp.min(jnp.where(l2 == m2, row, N_EXPERTS), axis=0, keepdims=True)
    e2 = jnp.exp(m2 - m1)
    w1 = 1.0 / (1.0 + e2)
    wts_ref[...] = jnp.concatenate([w1, e2 * w1], axis=0)
    oh1 = (row == i1).astype(F32)
    oh2 = (row == i2).astype(F32)
    cs1 = _bdot(oh1, tri_ref[...])
    cs2 = _bdot(oh2, tri_ref[...])
    tot1 = jnp.sum(oh1, axis=1, keepdims=True)
    run = jnp.ceil((tot1 + jnp.sum(oh2, axis=1, keepdims=True)) * (1.0 / RUN_ALIGN)) * RUN_ALIGN
    run_b = jnp.broadcast_to(run, (N_EXPERTS, 128))
    er = lax.broadcasted_iota(jnp.int32, (N_EXPERTS, N_EXPERTS), 0)
    ec = lax.broadcasted_iota(jnp.int32, (N_EXPERTS, N_EXPERTS), 1)
    start = jnp.dot((ec < er).astype(F32), run_b, precision=HIGHEST, preferred_element_type=F32)
    last = lax.broadcasted_iota(jnp.int32, (N_EXPERTS, 128), 0) == N_EXPERTS - 1
    run_b = jnp.where(last, MOE_LOCAL - start, run_b)
    st = start[:, 0:1]
    p1 = jnp.sum(oh1 * (st + cs1), axis=0, keepdims=True)
    p2 = jnp.sum(oh2 * (st + tot1 + cs2), axis=0, keepdims=True)
    lp_ref[...] = jnp.concatenate([p1, p2], axis=0).astype(jnp.int32)
    lane = lax.broadcasted_iota(jnp.int32, (N_EXPERTS, 128), 1)
    runs_ref[...] = jnp.where(lane == 0, run_b, jnp.where(lane == 1, start, cnt_ref[...]))
    cnt_ref[...] = cnt_ref[...] + run_b


def moe_router(acts, w_out, x, g, mods, w_router, b_router, tm=MOE_TOK):
    T = x.shape[0]
    tri = jnp.asarray(np.triu(np.ones((tm, tm), np.float32), k=1)).astype(BF16)
    tok2 = lambda dt: jax.ShapeDtypeStruct((2, T), dt)
    p_specs, p_args, proj = _proj_inputs(acts, w_out, x, mods, tm)
    return pl.pallas_call(
        functools.partial(_router_kernel, proj=proj),
        grid=(T // tm,),
        in_specs=p_specs + [
                  pl.BlockSpec((1, D), lambda i: (0, 0)),
                  _mod_spec(3, tm), _mod_spec(4, tm),
                  pl.BlockSpec((N_EXPERTS, D), lambda i: (0, 0)),
                  pl.BlockSpec((N_EXPERTS, 1), lambda i: (0, 0)),
                  _const_spec(tri, 1)],
        out_specs=[pl.BlockSpec((tm, D), lambda i: (i, 0)),
                   pl.BlockSpec((2, tm), lambda i: (0, i)),
                   pl.BlockSpec((2, tm), lambda i: (0, i)),
                   pl.BlockSpec((None, N_EXPERTS, 128), lambda i: (i, 0, 0)),
                   pl.BlockSpec((N_EXPERTS, 128), lambda i: (0, 0))],
        out_shape=[jax.ShapeDtypeStruct((T, D), F32), tok2(jnp.int32), tok2(F32),
                   jax.ShapeDtypeStruct((T // tm, N_EXPERTS, 128), F32),
                   jax.ShapeDtypeStruct((N_EXPERTS, 128), F32)],
        compiler_params=_cparams("arbitrary"),
        name="moe_router",
    )(*p_args, g, mods, mods, w_router.T, b_router[:, None], tri)


def moe_layout(runs, totals):
    rows = totals[:, 0].astype(jnp.int32)
    tiles = (rows + MOE_TM - 1) // MOE_TM
    tile_end = jnp.cumsum(tiles)
    group = (tile_end - tiles) * MOE_TM
    run_len = runs[:, :, 0].astype(jnp.int32)
    run_src = runs[:, :, 1].astype(jnp.int32)
    run_dst = group[None, :] + runs[:, :, 2].astype(jnp.int32)
    tail = jnp.stack([group + rows, tiles * MOE_TM - rows]).astype(jnp.int32)
    n_tiles = tile_end[-1]
    t = jnp.arange(MOE_MAX_TILES, dtype=jnp.int32)
    tile_e = jnp.sum(t[:, None] >= tile_end[None, :], axis=1).astype(jnp.int32)
    last_e = jnp.sum((n_tiles - 1) >= tile_end).astype(jnp.int32)
    tile_e = jnp.where(t < n_tiles, tile_e, last_e)
    first = jnp.sum(jnp.where(tile_e[:, None] == jnp.arange(N_EXPERTS), (tile_end - tiles)[None, :], 0), axis=1)
    e_rows = jnp.sum(jnp.where(tile_e[:, None] == jnp.arange(N_EXPERTS), rows[None, :], 0), axis=1)
    tile_rows = jnp.where(t < n_tiles, jnp.clip(e_rows - (t - first) * MOE_TM, 0, MOE_TM), 0).astype(jnp.int32)
    run_tab = jnp.stack([run_len, run_src, run_dst]).reshape(3, -1)
    return run_tab, tail, tile_e, n_tiles.astype(jnp.int32).reshape(1), tile_rows


def _run_copies(tab_ref, i, local_ref, global_ref, sem, to_global):
    out = []
    for e in range(N_EXPERTS):
        k = i * N_EXPERTS + e
        n, src, dst = tab_ref[0, k], tab_ref[1, k], tab_ref[2, k]
        for size in RUN_SIZES:
            done = (n // (2 * size)) * (2 * size)
            loc = local_ref.at[pl.ds(pl.multiple_of(src + done, RUN_ALIGN), size), :]
            glo = global_ref.at[pl.ds(pl.multiple_of(dst + done, RUN_ALIGN), size), :]
            copy = pltpu.make_async_copy(loc, glo, sem) if to_global else pltpu.make_async_copy(glo, loc, sem)
            out.append(((n & size) != 0, copy))
    return out


def _start(copies, live=True):
    for pred, copy in copies:
        pl.when(pred & live)(copy.start)


def _wait(copies, live=True):
    for pred, copy in copies:
        pl.when(pred & live)(copy.wait)


def _start_then_wait(copies):
    _start(copies)
    _wait(copies)


def _dispatch_kernel(tab_ref, tail_ref, nt_ref, lp_ref, x_ref, g_ref, sh_ref, sc_ref, xs_ref, hs_scr, z_scr, sem):
    i = pl.program_id(0)
    tm = x_ref.shape[0]
    buf = i % 2
    h = _norm_mod(x_ref[...], g_ref[...], sh_ref[...], sc_ref[...]).astype(BF16)
    slot = lax.broadcasted_iota(jnp.int32, (MOE_LOCAL, tm), 0)
    perm = jnp.where((slot == lp_ref[0:1, :]) | (slot == lp_ref[1:2, :]), 1.0, 0.0).astype(BF16)
    hs_scr[buf] = jnp.dot(perm, h, preferred_element_type=F32).astype(BF16)
    copies = _run_copies(tab_ref, i, hs_scr.at[buf], xs_ref, sem.at[buf], to_global=True)
    _start(copies)
    _wait(_run_copies(tab_ref, jnp.maximum(i - 1, 0), hs_scr.at[1 - buf], xs_ref, sem.at[1 - buf], to_global=True),
          live=i > 0)

    @pl.when(i == 0)
    def _():
        z_scr[...] = jnp.zeros_like(z_scr)
        zrows = z_scr.shape[0]

        def zero_tile(t, carry):
            for part in range(MOE_TM // zrows):
                dst = xs_ref.at[pl.ds(pl.multiple_of(t * MOE_TM + part * zrows, zrows), zrows), :]
                copy = pltpu.make_async_copy(z_scr, dst, sem.at[2])
                copy.start()
                copy.wait()
            return carry

        lax.fori_loop(nt_ref[0], MOE_MAX_TILES, zero_tile, 0)
        tails = []
        for e in range(N_EXPERTS):
            start, n = tail_ref[0, e], tail_ref[1, e]
            for size in RUN_SIZES:
                if size >= MOE_TM:
                    continue
                done = (n // (2 * size)) * (2 * size)
                dst = xs_ref.at[pl.ds(pl.multiple_of(start + done, RUN_ALIGN), size), :]
                tails.append(((n & size) != 0, pltpu.make_async_copy(z_scr.at[pl.ds(0, size), :], dst, sem.at[2])))
        _start_then_wait(tails)

    _wait(copies, live=i == pl.num_programs(0) - 1)


def moe_dispatch(x, g, mods, lp, run_tab, tail, n_tiles, tm=MOE_TOK):
    T = x.shape[0]
    n_rows = MOE_MAX_TILES * MOE_TM
    return pl.pallas_call(
        _dispatch_kernel,
        grid_spec=pltpu.PrefetchScalarGridSpec(
            num_scalar_prefetch=3,
            grid=(T // tm,),
            in_specs=[pl.BlockSpec((2, tm), lambda i, *_: (0, i)),
                      pl.BlockSpec((tm, D), lambda i, *_: (i, 0)),
                      pl.BlockSpec((1, D), lambda i, *_: (0, 0)),
                      _mod_spec(3, tm), _mod_spec(4, tm)],
            out_specs=pl.BlockSpec(memory_space=pl.ANY),
            scratch_shapes=[pltpu.VMEM((2, MOE_LOCAL, D), BF16), pltpu.VMEM((MOE_TM // 2, D), BF16),
                            pltpu.SemaphoreType.DMA((3,))]),
        out_shape=jax.ShapeDtypeStruct((n_rows, D), BF16),
        compiler_params=_cparams("arbitrary"),
        name="moe_dispatch",
    )(run_tab, tail, n_tiles, lp, x, g, mods, mods)


def _moe_group_kernel(te_ref, nt_ref, tr_ref, x_ref, w1_ref, w3_ref, w2_ref, o_ref, acc_scr):
    t, c = pl.program_id(0), pl.program_id(1)
    rows = tr_ref[t]
    half = MOE_TM // 2

    @pl.when(t < nt_ref[0])
    def _():
        @pl.when(c == 0)
        def _():
            acc_scr[...] = jnp.zeros_like(acc_scr)

        @pl.when(rows > half)
        def _():
            _swiglu_accumulate(x_ref, acc_scr, w1_ref, w3_ref, w2_ref, MOE_TM)

        @pl.when(rows <= half)
        def _():
            _swiglu_accumulate(x_ref, acc_scr, w1_ref, w3_ref, w2_ref, half)

        @pl.when(c == pl.num_programs(1) - 1)
        def _():
            o_ref[...] = acc_scr[...].astype(o_ref.dtype)

    @pl.when((t >= nt_ref[0]) & (c == pl.num_programs(1) - 1))
    def _():
        o_ref[...] = jnp.zeros_like(o_ref)


def moe_grouped_swiglu(xs, tile_e, n_tiles, tile_rows, w1, w3, w2, tf=256):
    nc = D_FF // tf
    live = lambda t, nt: t < nt[0]
    row = lambda t, c, te, nt, tr: (jnp.where(live(t, nt), t, jnp.maximum(nt[0] - 1, 0)), 0)
    wcol = lambda t, c, te, nt, tr: (te[t], 0, jnp.where(live(t, nt), c, nc - 1))
    wrow = lambda t, c, te, nt, tr: (te[t], jnp.where(live(t, nt), c, nc - 1), 0)
    return pl.pallas_call(
        _moe_group_kernel,
        grid_spec=pltpu.PrefetchScalarGridSpec(
            num_scalar_prefetch=3,
            grid=(MOE_MAX_TILES, nc),
            in_specs=[pl.BlockSpec((MOE_TM, D), row),
                      pl.BlockSpec((None, D, tf), wcol),
                      pl.BlockSpec((None, D, tf), wcol),
                      pl.BlockSpec((None, tf, D), wrow)],
            out_specs=pl.BlockSpec((MOE_TM, D), lambda t, c, te, nt, tr: (t, 0)),
            scratch_shapes=[pltpu.VMEM((MOE_TM, D), F32)]),
        out_shape=jax.ShapeDtypeStruct(xs.shape, BF16),
        compiler_params=_cparams("arbitrary", "arbitrary"),
        name="moe_grouped",
    )(tile_e, n_tiles, tile_rows, xs, w1, w3, w2)


def _combine_kernel(tab_ref, lp_ref, wt_ref, x_ref, gate_ref, ys_ref, op_ref, os_ref, yl_scr, sem, *, n_p):
    i = pl.program_id(0)
    tm = x_ref.shape[0]
    buf = i % 2
    last = pl.num_programs(0) - 1
    gather = lambda t, b: _run_copies(tab_ref, t, yl_scr.at[b], ys_ref, sem.at[b], to_global=False)
    _start(gather(i, buf), live=i == 0)
    _start(gather(jnp.minimum(i + 1, last), 1 - buf), live=i < last)
    slot = lax.broadcasted_iota(jnp.int32, (tm, MOE_LOCAL), 1)
    mix = (jnp.where(slot == lp_ref[:, 0:1], wt_ref[:, 0:1], 0.0)
           + jnp.where(slot == lp_ref[:, 1:2], wt_ref[:, 1:2], 0.0)).astype(BF16)
    _wait(gather(i, buf))
    moe = jnp.dot(mix, yl_scr[buf], preferred_element_type=F32)
    out = x_ref[...] + gate_ref[...] * moe

    @pl.when(pl.program_id(0) < n_p)
    def _():
        op_ref[...] = out

    @pl.when(pl.program_id(0) >= n_p)
    def _():
        os_ref[...] = out


def moe_combine(x, mods, lp, wts, run_tab, ys, t_prompt, tm=MOE_TOK):
    T = x.shape[0]
    n_p = t_prompt // tm
    return pl.pallas_call(
        functools.partial(_combine_kernel, n_p=n_p),
        grid_spec=pltpu.PrefetchScalarGridSpec(
            num_scalar_prefetch=1,
            grid=(T // tm,),
            in_specs=[pl.BlockSpec((tm, 2), lambda i, *_: (i, 0)),
                      pl.BlockSpec((tm, 2), lambda i, *_: (i, 0)),
                      pl.BlockSpec((tm, D), lambda i, *_: (i, 0)),
                      _mod_spec(5, tm),
                      pl.BlockSpec(memory_space=pl.ANY)],
            out_specs=_part_specs((tm, D), n_p),
            scratch_shapes=[pltpu.VMEM((2, MOE_LOCAL, D), BF16), pltpu.SemaphoreType.DMA((2,))]),
        out_shape=[jax.ShapeDtypeStruct((t_prompt, D), F32), jax.ShapeDtypeStruct((T - t_prompt, D), F32)],
        compiler_params=_cparams("arbitrary"),
        name="moe_combine",
    )(run_tab, lp.T, wts.T, x, mods, ys)


def kernel(x_prompt, x_sample, state_C, state_n, state_m, cache_k, cache_v, c, c_ctx, norm1_g, norm2_g, w_ada, b_ada, ev_w_in, ev_conv, hy_w1, hy_b1, hy_w2, hy_b2, hy_w3, hy_freq, hy_d, ml_b_gate, ml_norm_g, ev_w_out, ff_w1, ff_w3, ff_w2, at_w_qkv, at_q_g, at_k_g, at_sink, at_w_out, moe_w_router, moe_b_router, moe_w1, moe_w3, moe_w2):
    BP, LP, _ = x_prompt.shape
    BS, LS, _ = x_sample.shape
    TP = BP * LP
    assert TP % GROUP == 0 and TP // GROUP == N_PROMPT_GROUPS and LS == GROUP and BS == 8

    xp, xs = x_prompt.reshape(TP, D), x_sample.reshape(BS * LS, D)
    cond = jnp.concatenate([c_ctx[None], c, jnp.zeros((16 - 1 - BS, D), F32)], axis=0)
    mods = adaln_table(cond, w_ada, b_ada)

    u, gates = even_in_proj(xp, xs, norm1_g[0:1], mods[0], ev_w_in[0], ml_b_gate[0].reshape(1, N_GATES))
    hy = []
    for seq0, B, L, nb in ((0, BP, LP, 4), (TP // LS, BS, LS, 1)):
        fwd, inv = (jnp.asarray(t).astype(BF16) for t in _dft_tables(L))
        ka, kb = hyena_filter_spectra(L, hy_w1[0], hy_b1[0], hy_w2[0], hy_b2[0], hy_w3[0], hy_freq[0], fwd)
        hy.append(hyena_mix(u, seq0, B, L, ev_conv[0], hy_d[0], fwd, inv, ka, kb, nb))
    ml_p, new_C, new_n, new_m = mlstm_mix(u, gates, 0, BP, LP, ml_norm_g[0], want_state=True)
    ml_s = mlstm_mix(u, gates, TP // LS, BS, LS, ml_norm_g[0],
                     state=(state_C[:, 0], state_n[:, 0], state_m[:, 0]))
    x = ffn_residual([hy, (ml_p, ml_s)], ev_w_out[0], (xp, xs), norm2_g[0:1], mods[0], ff_w1[0], ff_w3[0], ff_w2[0])

    q, kv = qkv_proj(x, norm1_g[1:2], mods[1], at_w_qkv[0])
    o_p, new_k, new_v = attention(q, kv, 0, at_q_g[0], at_k_g[0], at_sink[0], BP, LP)
    o_s = attention(q, kv, TP, at_q_g[0], at_k_g[0], at_sink[0], BS, LS, cache=(cache_k[:, 0], cache_v[:, 0]))
    x, lp, wts, runs, totals = moe_router([(o_p, o_s)], at_w_out[0], x, norm2_g[1:2], mods[1],
                                          moe_w_router[0], moe_b_router[0])
    run_tab, tail, tile_e, n_tiles, tile_rows = moe_layout(runs, totals)
    xsort = moe_dispatch(x, norm2_g[1:2], mods[1], lp, run_tab, tail, n_tiles)
    ysort = moe_grouped_swiglu(xsort, tile_e, n_tiles, tile_rows, moe_w1[0], moe_w3[0], moe_w2[0])
    yp, ys = moe_combine(x, mods[1], lp, wts, run_tab, ysort, TP)

    return (yp.reshape(BP, LP, D), ys.reshape(BS, LS, D),
            new_C[:, None], new_n[:, None], new_m[:, None], new_k[:, None], new_v[:, None])
```

```python
import functools
import math

import numpy as np
import jax
import jax.numpy as jnp
from jax import lax
from jax.experimental import pallas as pl
from jax.experimental.pallas import tpu as pltpu

F32 = jnp.float32
BF16 = jnp.bfloat16
HIGHEST = lax.Precision.HIGHEST

D = 1024
GROUP = 1024
N_PROMPT_GROUPS = 8
HY_W = 512
ML_HEADS = 4
ML_HD = 128
ML_CHUNK = 256
EVEN_MAIN = 3 * HY_W + 4 * 512
N_GATES = 16
ATT_HD = 64
ATT_HEADS = 16
ATT_KV = 4
WINDOW = 128
GRID_W = 64
ROPE_BASE = 10000.0
D_FF = 2816
N_EXPERTS = 8
EPS = 1e-6
NEG = -1e30
VMEM_LIMIT = 56 * 1024 * 1024


def _cparams(*sem, flags=None):
    return pltpu.CompilerParams(dimension_semantics=sem, vmem_limit_bytes=VMEM_LIMIT, flags=flags)


def _mod_row(i, tm):
    return jnp.maximum(i * tm // GROUP - (N_PROMPT_GROUPS - 1), 0)


def _silu(x):
    return x * jax.nn.sigmoid(x)


def _bdot(a, b):
    return jnp.dot(a.astype(BF16), b.astype(BF16), preferred_element_type=F32)


def _norm_mod(x, g, sh, sc):
    y = x * lax.rsqrt(jnp.mean(x * x, axis=-1, keepdims=True) + EPS) * g
    return y * (1.0 + sc) + sh


def _adaln_kernel(c_ref, w_ref, b_ref, o_ref):
    s = _silu(c_ref[...])
    o_ref[...] = jnp.dot(s, w_ref[...], precision=HIGHEST, preferred_element_type=F32) + b_ref[...]


def adaln_table(cond, w_ada, b_ada):
    depth = w_ada.shape[0]
    tn = 1536
    out = pl.pallas_call(
        _adaln_kernel,
        grid=(depth, 6 * D // tn),
        in_specs=[pl.BlockSpec((16, D), lambda l, j: (0, 0)),
                  pl.BlockSpec((None, D, tn), lambda l, j: (l, 0, j)),
                  pl.BlockSpec((None, 1, tn), lambda l, j: (l, 0, j))],
        out_specs=pl.BlockSpec((None, 16, tn), lambda l, j: (l, 0, j)),
        out_shape=jax.ShapeDtypeStruct((depth, 16, 6 * D), F32),
        compiler_params=_cparams("parallel", "parallel"),
        name="adaln",
    )(cond, w_ada, b_ada.reshape(depth, 1, 6 * D))
    return out.reshape(depth, 16, 1, 6 * D)


def _part_specs(block, n_p):
    return [pl.BlockSpec(block, lambda i, *_: (jnp.minimum(i, n_p - 1), 0)),
            pl.BlockSpec(block, lambda i, *_: (jnp.maximum(i - n_p, 0), 0))]


def _pick(is_prompt, p_ref, s_ref):
    return jnp.where(is_prompt, p_ref[...], s_ref[...])


def _mod_spec(k, tm):
    return pl.BlockSpec((None, 1, D), lambda i, *_: (_mod_row(i, tm), 0, k))


def _log_sigmoid(x):
    return jnp.minimum(x, 0.0) - jnp.log(1.0 + jnp.exp(-jnp.abs(x)))


def _split3(x):
    hi = x.astype(BF16)
    r = x - hi.astype(F32)
    mid = r.astype(BF16)
    return hi, mid, (r - mid.astype(F32)).astype(BF16)


def _even_in_kernel(xp_ref, xs_ref, g_ref, sh_ref, sc_ref, w_ref, bg_ref, lo_ref, up_ref, u_ref, gate_ref, *, n_p, tn):
    is_prompt = pl.program_id(0) < n_p
    h = _norm_mod(_pick(is_prompt, xp_ref, xs_ref), g_ref[...], sh_ref[...], sc_ref[...]).astype(BF16)
    for j in range(EVEN_MAIN // tn):
        u_ref[:, j * tn:(j + 1) * tn] = _bdot(h, w_ref[:, j * tn:(j + 1) * tn]).astype(u_ref.dtype)
    gates = _bdot(h, w_ref[:, EVEN_MAIN:]) + bg_ref[...]
    lf = _log_sigmoid(gates)
    col = lax.broadcasted_iota(jnp.int32, (1, N_GATES), 1)
    is_forget = (col // ML_HEADS) % 2 == 1
    is_rev = col >= N_GATES // 2
    for ch in range(h.shape[0] // ML_CHUNK):
        sl = slice(ch * ML_CHUNK, (ch + 1) * ML_CHUNK)
        parts = _split3(lf[sl])
        cf = sum(jnp.dot(lo_ref[...], p, preferred_element_type=F32) for p in parts)
        cr = sum(jnp.dot(up_ref[...], p, preferred_element_type=F32) for p in parts)
        gate_ref[sl, :] = jnp.where(is_forget, jnp.where(is_rev, cr, cf), gates[sl])


def even_in_proj(xp, xs, g, mods, w_in, b_gate, tm=1024, tn=512):
    T = xp.shape[0] + xs.shape[0]
    tri = np.tril(np.ones((ML_CHUNK, ML_CHUNK), np.float32))
    lo, up = jnp.asarray(tri).astype(BF16), jnp.asarray(tri.T).astype(BF16)
    return pl.pallas_call(
        functools.partial(_even_in_kernel, n_p=xp.shape[0] // tm, tn=tn),
        grid=(T // tm,),
        in_specs=_part_specs((tm, D), xp.shape[0] // tm) + [
                  pl.BlockSpec((1, D), lambda i: (0, 0)),
                  _mod_spec(0, tm), _mod_spec(1, tm),
                  _const_spec(w_in, 1),
                  pl.BlockSpec((1, N_GATES), lambda i: (0, 0)),
                  _const_spec(lo, 1), _const_spec(up, 1)],
        out_specs=[pl.BlockSpec((tm, EVEN_MAIN), lambda i: (i, 0)),
                   pl.BlockSpec((tm, N_GATES), lambda i: (i, 0))],
        out_shape=[jax.ShapeDtypeStruct((T, EVEN_MAIN), BF16),
                   jax.ShapeDtypeStruct((T, N_GATES), F32)],
        compiler_params=_cparams("parallel"),
        name="even_in_proj",
    )(xp, xs, g, mods, mods, w_in, b_gate, lo, up)


def _dft_tables(L):
    n = 2 * L
    f = np.arange(L, dtype=np.int64)[:, None]
    s = np.arange(L, dtype=np.int64)[None, :]
    ang = 2.0 * np.pi * ((f * s) % n).astype(np.float64) / n
    fwd = np.concatenate([np.cos(ang), -np.sin(ang)], axis=0)
    fwd[L, :] = np.where(np.arange(L) % 2 == 0, 1.0, -1.0)
    t = np.arange(L, dtype=np.int64)[:, None]
    ff = np.arange(L, dtype=np.int64)[None, :]
    ang = 2.0 * np.pi * ((t * ff) % n).astype(np.float64) / n
    inv_re = 2.0 * np.cos(ang) / n
    inv_re[:, 0] = 1.0 / n
    inv_im = -2.0 * np.sin(ang) / n
    inv_im[:, 0] = np.where(np.arange(L) % 2 == 0, 1.0, -1.0) / n
    inv = np.concatenate([inv_re, inv_im], axis=1)
    return fwd.astype(np.float32), inv.astype(np.float32)


def _filter_tables(L):
    t = np.linspace(0.0, 1.0, L, dtype=np.float32).astype(np.float64)[:, None]
    w = 2.0 * math.pi * np.arange(L, dtype=np.float64)[:, None] / L
    bands = np.linspace(1e-4, 16 - 1, 16, dtype=np.float32).astype(np.float64)[None, :]
    z = np.concatenate([t, np.cos(bands * w), -np.sin(bands * w)], axis=-1)
    zp = np.zeros((L, 128), np.float64)
    zp[:, :z.shape[1]] = z
    max_decay = math.log(1e-2) / 0.3
    min_decay = math.log(1e-2) / 1.5
    deltas = np.linspace(min_decay, max_decay, HY_W, dtype=np.float32).astype(np.float64)
    decay = np.exp(-t * np.abs(deltas))
    return zp.astype(np.float32), decay.astype(np.float32)


def _hy_filter_kernel(z_ref, dec_ref, w1_ref, b1_ref, w2_ref, b2_ref, w3_ref, fr_ref, fwd_ref,
                      ka_ref, kb_ref):
    L = z_ref.shape[0]
    hdot = functools.partial(jnp.dot, precision=HIGHEST, preferred_element_type=F32)
    h = jnp.sin(fr_ref[0:1, :] * (hdot(z_ref[...], w1_ref[...]) + b1_ref[...]))
    h = jnp.sin(fr_ref[1:2, :] * (hdot(h, w2_ref[...]) + b2_ref[...]))
    h = hdot(h, w3_ref[...])
    row0 = lax.broadcasted_iota(jnp.int32, (L, 1), 0) == 0
    h0 = h[:, :HY_W] * dec_ref[...]
    h1 = h[:, HY_W:] * dec_ref[...]
    l1 = jnp.sum(jnp.abs(h0), axis=0, keepdims=True) + jnp.sum(jnp.abs(h1), axis=0, keepdims=True)
    inv = 1.0 / l1
    h0 = h0 * inv
    h1 = jnp.where(row0, 0.0, h1 * inv)
    f0 = _bdot(fwd_ref[...], h0)
    f1 = _bdot(fwd_ref[...], h1)
    ka_ref[...] = f0[:L] + f1[:L]
    kb_ref[...] = jnp.where(row0, f0[L:] + f1[L:], f0[L:] - f1[L:])


def _const_spec(a, n_grid):
    return pl.BlockSpec(a.shape, lambda *_: (0,) * a.ndim, pipeline_mode=pl.Buffered(1))


def hyena_filter_spectra(L, w1, b1, w2, b2, w3, freq, fwd):
    z, dec = _filter_tables(L)
    pad2 = lambda a, r, c: jnp.pad(a, ((0, r - a.shape[0]), (0, c - a.shape[1])))
    args = (jnp.asarray(z), jnp.asarray(dec), pad2(w1, 128, 128), pad2(b1[None], 1, 128),
            pad2(w2, 128, 128), pad2(b2[None], 1, 128), pad2(w3, 128, 4 * HY_W), pad2(freq, 2, 128), fwd)
    in_specs = [_const_spec(a, 1) for a in args]
    in_specs[6] = pl.BlockSpec((128, 2 * HY_W), lambda o: (0, o))
    shp = jax.ShapeDtypeStruct((2, L, HY_W), F32)
    out_spec = pl.BlockSpec((None, L, HY_W), lambda o: (o, 0, 0))
    return pl.pallas_call(
        _hy_filter_kernel,
        grid=(2,),
        in_specs=in_specs,
        out_specs=[out_spec, out_spec],
        out_shape=[shp, shp],
        compiler_params=_cparams("arbitrary"),
        name=f"hyena_filter_{L}",
    )(*args)


def _hyena_kernel(u_ref, cw_ref, d_ref, fwd_ref, inv_ref, ka_ref, kb_ref, o_ref):
    nb, L = u_ref.shape[0], u_ref.shape[1]
    row = lax.broadcasted_iota(jnp.int32, (L, 1), 0)
    first, last = row == 0, row == L - 1
    fwd = fwd_ref[...].astype(BF16)
    inv = inv_ref[...].astype(BF16)

    def long_conv(z, o):
        zf = jnp.dot(fwd, z.astype(BF16), preferred_element_type=F32)
        a, b = zf[:L], zf[L:]
        ka, kb = ka_ref[o], kb_ref[o]
        yr = a * ka - jnp.where(first, 0.0, b * kb)
        yi = jnp.where(first, b * kb, a * kb + b * ka)
        return (jnp.dot(inv[:, :L], yr.astype(BF16), preferred_element_type=F32)
                + jnp.dot(inv[:, L:], yi.astype(BF16), preferred_element_type=F32))

    for bi in range(nb):
        u = u_ref[bi].astype(F32)
        prev = jnp.where(first, 0.0, pltpu.roll(u, 1, 0))
        nxt = jnp.where(last, 0.0, pltpu.roll(u, L - 1, 0))
        u = prev * cw_ref[0:1, :] + u * cw_ref[1:2, :] + nxt * cw_ref[2:3, :]
        v, x1, x2 = u[:, :HY_W], u[:, HY_W:2 * HY_W], u[:, 2 * HY_W:]
        z = x1 * (long_conv(v, 0) + d_ref[0:1, :] * v)
        z = x2 * (long_conv(z, 1) + d_ref[1:2, :] * z)
        o_ref[bi] = z.astype(o_ref.dtype)


def hyena_mix(u, seq0, B, L, conv_w, d_skip, fwd, inv, ka, kb, nb):
    u3 = u.reshape(-1, L, EVEN_MAIN)
    full = lambda a: _const_spec(a, 1)
    out = pl.pallas_call(
        _hyena_kernel,
        grid=(B // nb,),
        in_specs=[pl.BlockSpec((nb, L, 3 * HY_W), lambda b: (b + seq0 // nb, 0, 0)),
                  full(conv_w), full(d_skip), full(fwd), full(inv), full(ka), full(kb)],
        out_specs=pl.BlockSpec((nb, L, HY_W), lambda b: (b, 0, 0)),
        out_shape=jax.ShapeDtypeStruct((B, L, HY_W), BF16),
        compiler_params=_cparams("parallel"),
        name=f"hyena_{L}",
    )(u3, conv_w, d_skip, fwd, inv, ka, kb)
    return out.reshape(B * L, HY_W)


def _mlstm_kernel(*refs, has_state, want_state):
    q_ref, k_ref, v_ref, o_ref, gc_ref, gr_ref, ng_ref = refs[:7]
    refs = refs[7:]
    if has_state:
        c0t_ref, n0b_ref, m0_ref = refs[:3]
        refs = refs[3:]
    y_ref = refs[0]
    if want_state:
        c_out, n_out, m_out = refs[1:4]
    L, d = q_ref.shape[0], ML_HD
    T = min(ML_CHUNK, L)
    nc = L // T
    scale = 1.0 / math.sqrt(d)
    nt = (((1,), (1,)), ((), ()))
    si = lax.broadcasted_iota(jnp.int32, (T, T), 0)
    ti = lax.broadcasted_iota(jnp.int32, (T, T), 1)
    allowed = (si <= ti, si >= ti)
    chains = [(dr, h) for dr in range(2) for h in range(ML_HEADS)]
    gcol = lambda dr, gi, h: dr * 2 * ML_HEADS + gi * ML_HEADS + h

    caug_t, m = {}, {}
    for ch in chains:
        dr, h = ch
        if has_state:
            caug_t[ch] = jnp.concatenate([c0t_ref[dr, h], n0b_ref[dr, h]], axis=0)
            m[ch] = m0_ref[dr, h:h + 1, 0:1]
        else:
            caug_t[ch], m[ch] = jnp.zeros((2 * d, d), F32), jnp.zeros((1, 1), F32)

    chunk_cache = {}

    def chunk_data(h, j):
        if (h, j) not in chunk_cache:
            sl, hl = slice(j * T, (j + 1) * T), slice(h * d, (h + 1) * d)
            q = q_ref[sl, hl]
            ks = (k_ref[sl, hl].astype(F32) * scale).astype(BF16)
            v_t = v_ref[sl, hl].astype(F32).T
            vaug_t = jnp.concatenate([v_t, jnp.ones((d, T), F32)], axis=0).astype(BF16)
            s_raw = lax.dot_general(ks, q, nt, preferred_element_type=F32)
            chunk_cache[(h, j)] = (q, ks, v_t, vaug_t, s_raw)
        return chunk_cache[(h, j)]

    h_sum = {}
    for it in range(nc):
        step = {ch: (it if ch[0] == 0 else nc - 1 - it) for ch in chains}
        data = {ch: chunk_data(ch[1], step[ch]) for ch in chains}
        inter_t = {ch: lax.dot_general(caug_t[ch].astype(BF16), data[ch][0], nt, preferred_element_type=F32)
                   for ch in chains}
        gate = {}
        for ch in chains:
            dr, h = ch
            sl = slice(step[ch] * T, (step[ch] + 1) * T)
            li_r, b_r = gr_ref[gcol(dr, 0, h):gcol(dr, 0, h) + 1, sl], gr_ref[gcol(dr, 1, h):gcol(dr, 1, h) + 1, sl]
            src = gc_ref[sl, gcol(dr, 0, h):gcol(dr, 0, h) + 1] - gc_ref[sl, gcol(dr, 1, h):gcol(dr, 1, h) + 1]
            dm = jnp.where(allowed[dr], src + b_r, NEG)
            inter = b_r + m[ch]
            m_t = jnp.maximum(inter, jnp.max(dm, axis=0, keepdims=True))
            b_end = b_r[:, T - 1:T] if dr == 0 else b_r[:, 0:1]
            g_r = b_end - b_r + li_r
            m_new = jnp.maximum(b_end + m[ch], jnp.max(g_r, axis=1, keepdims=True))
            gate[ch] = (jnp.exp(dm - m_t), jnp.exp(inter - m_t), jnp.exp(-m_t), jnp.exp(g_r - m_new),
                        jnp.exp(b_end + m[ch] - m_new), m_new)
        for ch in chains:
            q, ks, v_t, vaug_t, s_raw = data[ch]
            w_intra, w_inter, floor, w_tok, decay, m_new = gate[ch]
            acc = jnp.dot(vaug_t, (s_raw * w_intra).astype(BF16), preferred_element_type=F32) + w_inter * inter_t[ch]
            h_t = acc[:d] / jnp.maximum(jnp.abs(acc[d:]), floor)
            key = (ch[1], step[ch])
            h_sum[key] = h_t if key not in h_sum else h_sum[key] + h_t
            vw_t = jnp.concatenate([v_t * w_tok, jnp.broadcast_to(w_tok, (d, T))], axis=0).astype(BF16)
            caug_t[ch] = decay * caug_t[ch] + jnp.dot(vw_t, ks, preferred_element_type=F32)
            m[ch] = m_new

    if want_state:
        for ch in chains:
            dr, h = ch
            c_out[dr, h] = caug_t[ch][:d].T
            n_out[dr, h:h + 1, :] = caug_t[ch][d:d + 1, :]
            m_out[dr, h:h + 1, :] = jnp.broadcast_to(m[ch], (1, d))
    for h in range(ML_HEADS):
        for j in range(nc):
            sl, hl = slice(j * T, (j + 1) * T), slice(h * d, (h + 1) * d)
            hv = h_sum[(h, j)].T
            y = hv * lax.rsqrt(jnp.mean(hv * hv, axis=-1, keepdims=True) + EPS) * ng_ref[:, hl]
            y_ref[sl, hl] = (y * jax.nn.sigmoid(o_ref[sl, hl].astype(F32))).astype(y_ref.dtype)


def mlstm_mix(u, gates, seq0, B, L, norm_g, state=None, want_state=False):
    u3 = u.reshape(-1, L, EVEN_MAIN)
    gc = gates.reshape(-1, L, N_GATES)[seq0:seq0 + B]
    gr = gc.transpose(0, 2, 1)
    width = ML_HEADS * ML_HD
    col = lambda i: pl.BlockSpec((None, L, width), lambda b: (b + seq0, 0, (3 * HY_W + i * width) // width))
    in_specs = [col(0), col(1), col(2), col(3),
                pl.BlockSpec((None, L, N_GATES), lambda b: (b, 0, 0)),
                pl.BlockSpec((None, N_GATES, L), lambda b: (b, 0, 0)),
                pl.BlockSpec((1, width), lambda b: (0, 0))]
    args = [u3, u3, u3, u3, gc, gr, norm_g.reshape(1, width)]
    sspec = pl.BlockSpec((None, 2, ML_HEADS, ML_HD, ML_HD), lambda b: (b, 0, 0, 0, 0))
    vspec = pl.BlockSpec((None, 2, ML_HEADS, ML_HD), lambda b: (b, 0, 0, 0))
    if state is not None:
        C0, n0, m0 = state
        in_specs += [sspec, sspec, vspec]
        args += [C0.swapaxes(-1, -2), jnp.broadcast_to(n0[..., None, :], C0.shape),
                 jnp.broadcast_to(m0[..., None], n0.shape)]
    out_specs = [pl.BlockSpec((None, L, width), lambda b: (b, 0, 0))]
    out_shape = [jax.ShapeDtypeStruct((B, L, width), BF16)]
    if want_state:
        out_specs += [sspec, vspec, vspec]
        out_shape += [jax.ShapeDtypeStruct((B, 2, ML_HEADS, ML_HD, ML_HD), F32),
                      jax.ShapeDtypeStruct((B, 2, ML_HEADS, ML_HD), F32),
                      jax.ShapeDtypeStruct((B, 2, ML_HEADS, ML_HD), F32)]
    outs = pl.pallas_call(
        functools.partial(_mlstm_kernel, has_state=state is not None, want_state=want_state),
        grid=(B,),
        in_specs=in_specs, out_specs=out_specs, out_shape=out_shape,
        compiler_params=_cparams("parallel"),
        name=f"mlstm_{L}",
    )(*args)
    y = outs[0].reshape(B * L, width)
    if not want_state:
        return y
    _, C, n, m = outs
    return y, C, n, m[..., 0]


def _proj_inputs(acts, w, x, mods, tm):
    xs = tuple(x) if isinstance(x, (tuple, list)) else (x,)
    n_p = acts[0][0].shape[0] // tm
    specs = []
    for pair in acts:
        specs += _part_specs((tm, pair[0].shape[1]), n_p)
    specs.append(pl.BlockSpec(w.shape, lambda *_: (0, 0), pipeline_mode=pl.Buffered(1)))
    specs += _part_specs((tm, D), n_p) if len(xs) == 2 else [pl.BlockSpec((tm, D), lambda i, *_: (i, 0))]
    specs.append(_mod_spec(2, tm))
    args = [a for pair in acts for a in pair] + [w, *xs, mods]
    return specs, args, dict(n_in=len(acts), n_x=len(xs), n_p=n_p)


def _proj_value(refs, is_prompt, n_in, n_x, n_p):
    a_refs = refs[:2 * n_in]
    w_ref = refs[2 * n_in]
    x_refs = refs[2 * n_in + 1:2 * n_in + 1 + n_x]
    gate_ref = refs[2 * n_in + 1 + n_x]
    k0 = 0
    acc = None
    for j in range(n_in):
        a = _pick(is_prompt, a_refs[2 * j], a_refs[2 * j + 1])
        kw = a.shape[1]
        part = _bdot(a, w_ref[k0:k0 + kw, :])
        acc = part if acc is None else acc + part
        k0 += kw
    x = _pick(is_prompt, *x_refs) if n_x == 2 else x_refs[0][...]
    return x + gate_ref[...] * acc


def _n_proj_refs(n_in, n_x, n_p):
    return 2 * n_in + 1 + n_x + 1


SWIGLU_ROWS = 512


def _swiglu_accumulate(h_scr, acc_scr, w1_ref, w3_ref, w2_ref, rows, scale=None):
    w1, w3, w2 = w1_ref[...].astype(BF16), w3_ref[...].astype(BF16), w2_ref[...].astype(BF16)
    groups = [slice(r, r + SWIGLU_ROWS) for r in range(0, rows, SWIGLU_ROWS)]
    ups = []
    for sl in groups:
        h = h_scr[sl, :]
        ups.append((jnp.dot(h, w1, preferred_element_type=F32), jnp.dot(h, w3, preferred_element_type=F32)))
    for sl, (a, b) in zip(groups, ups):
        mid = (_silu(a) * b).astype(BF16)
        down = jnp.dot(mid, w2, preferred_element_type=F32)
        acc_scr[sl, :] += down if scale is None else scale * down


def _ffn_kernel(*refs, proj):
    n = _n_proj_refs(**proj)
    g_ref, sh_ref, sc_ref, gate_ref, w1_ref, w3_ref, w2_ref, o_ref, h_scr = refs[n:]
    c = pl.program_id(1)
    is_prompt = pl.program_id(0) < proj["n_p"]

    @pl.when(c == 0)
    def _():
        x = _proj_value(refs[:n], is_prompt, **proj)
        o_ref[...] = x
        h_scr[...] = _norm_mod(x, g_ref[...], sh_ref[...], sc_ref[...]).astype(BF16)

    _swiglu_accumulate(h_scr, o_ref, w1_ref, w3_ref, w2_ref, h_scr.shape[0], scale=gate_ref[...])


def ffn_residual(acts, w_out, x, g, mods, w1, w3, w2, tm=1024, tf=256):
    T = sum(a.shape[0] for a in acts[0])
    p_specs, p_args, proj = _proj_inputs(acts, w_out, x, mods, tm)
    return pl.pallas_call(
        functools.partial(_ffn_kernel, proj=proj),
        grid=(T // tm, D_FF // tf),
        in_specs=p_specs + [
                  pl.BlockSpec((1, D), lambda i, c: (0, 0)),
                  _mod_spec(3, tm), _mod_spec(4, tm), _mod_spec(5, tm),
                  pl.BlockSpec((D, tf), lambda i, c: (0, c)),
                  pl.BlockSpec((D, tf), lambda i, c: (0, c)),
                  pl.BlockSpec((tf, D), lambda i, c: (c, 0))],
        out_specs=pl.BlockSpec((tm, D), lambda i, c: (i, 0)),
        out_shape=jax.ShapeDtypeStruct((T, D), F32),
        scratch_shapes=[pltpu.VMEM((tm, D), BF16)],
        compiler_params=_cparams("parallel", "arbitrary"),
        name="ffn",
    )(*p_args, g, mods, mods, mods, w1, w3, w2)


def _qkv_kernel(x_ref, g_ref, sh_ref, sc_ref, w_ref, q_ref, kv_ref):
    h = _norm_mod(x_ref[...], g_ref[...], sh_ref[...], sc_ref[...]).astype(BF16)
    nq = q_ref.shape[1]
    q_ref[...] = _bdot(h, w_ref[:, :nq]).astype(q_ref.dtype)
    kv_ref[...] = _bdot(h, w_ref[:, nq:])


def qkv_proj(x, g, mods, w_qkv, tm=1024):
    T = x.shape[0]
    nq, nkv = ATT_HEADS * ATT_HD, 2 * ATT_KV * ATT_HD
    return pl.pallas_call(
        _qkv_kernel,
        grid=(T // tm,),
        in_specs=[pl.BlockSpec((tm, D), lambda i: (i, 0)),
                  pl.BlockSpec((1, D), lambda i: (0, 0)),
                  _mod_spec(0, tm), _mod_spec(1, tm),
                  _const_spec(w_qkv, 1)],
        out_specs=[pl.BlockSpec((tm, nq), lambda i: (i, 0)),
                   pl.BlockSpec((tm, nkv), lambda i: (i, 0))],
        out_shape=[jax.ShapeDtypeStruct((T, nq), BF16), jax.ShapeDtypeStruct((T, nkv), F32)],
        compiler_params=_cparams("parallel"),
        name="qkv_proj",
    )(x, g, mods, mods, w_qkv)


def _rope_tables(L):
    half = ATT_HD // 2
    pos_r = (np.arange(L) // GRID_W).astype(np.float32)
    pos_c = (np.arange(L) % GRID_W).astype(np.float32)
    inv = (ROPE_BASE ** (-np.arange(0, half, 2, dtype=np.float32) / half)).astype(np.float32)
    cos = np.zeros((L, ATT_HD), np.float64)
    sin = np.zeros((L, ATT_HD), np.float64)
    for base, pos in ((0, pos_r), (half, pos_c)):
        ang = (pos[:, None] * inv[None, :]).astype(np.float32).astype(np.float64)
        cos[:, base:base + half] = np.concatenate([np.cos(ang), np.cos(ang)], axis=1)
        sin[:, base:base + half] = np.concatenate([-np.sin(ang), np.sin(ang)], axis=1)
    return (np.tile(cos, (1, 4)).astype(np.float32), np.tile(sin, (1, 4)).astype(np.float32))


def _seg_rms(x):
    w = x.shape[1]
    ri = lax.broadcasted_iota(jnp.int32, (w, w), 0) // ATT_HD
    ci = lax.broadcasted_iota(jnp.int32, (w, w), 1) // ATT_HD
    ss = _bdot(x * x, (ri == ci).astype(F32))
    return x * lax.rsqrt(ss * (1.0 / ATT_HD) + EPS)


LOG2E = 1.4426950408889634


def _exp2_bf16(x):
    return jnp.exp2(x.astype(BF16))


def _both_halves(tile, low):
    lane = lax.broadcasted_iota(jnp.int32, tile.shape, 1)
    other = pltpu.roll(tile, ATT_HD, 1)
    return jnp.where((lane < ATT_HD) == low, tile, other)


def _swap16(x):
    w = x.shape[1]
    lane = lax.broadcasted_iota(jnp.int32, x.shape, 1)
    return jnp.where(lane % 32 < 16, pltpu.roll(x, w - 16, 1), pltpu.roll(x, 16, 1))


def _attn_kernel(*refs, latent, tq):
    if latent:
        (q_ref, kv_ref, ck_ref, cv_ref, qg_ref, kg_ref, sink_ref, cosq_ref, sinq_ref, cosk_ref, sink_t_ref,
         o_ref, kk_scr, vt_scr, ckk_scr, cvt_scr) = refs
    else:
        q_ref, kv_ref, qg_ref, kg_ref, sink_ref, o_ref, ko_ref, vo_ref, kk_scr, vt_scr = refs
    L = kv_ref.shape[0]
    gw = ATT_KV * ATT_HD
    pw = 2 * ATT_HD
    qb = pl.program_id(1)

    vrows = vt_scr.shape[2]
    nblk = L // pw

    def vt_aug(tile, low):
        vt = tile.T[0:ATT_HD, :] if low else tile.T[ATT_HD:, :]
        return jnp.concatenate([vt, jnp.ones((vrows - ATT_HD, tile.shape[0]), F32)], axis=0).astype(BF16)

    @pl.when(qb == 0)
    def _():
        kn = _seg_rms(kv_ref[:, :gw]) * kg_ref[...]
        v = kv_ref[:, gw:]
        if latent:
            kn = kn * cosk_ref[...] + _swap16(kn) * sink_t_ref[...]
        else:
            for c in range(ATT_KV):
                ko_ref[c] = kn[:, c * ATT_HD:(c + 1) * ATT_HD]
                vo_ref[c] = v[:, c * ATT_HD:(c + 1) * ATT_HD]
        for c in range(ATT_KV):
            tile, low = slice((c // 2) * pw, (c // 2 + 1) * pw), c % 2 == 0
            kk_scr[c] = _both_halves(kn[:, tile], low).astype(BF16)
            for j in range(nblk):
                vt_scr[c, j] = vt_aug(v[j * pw:(j + 1) * pw, tile], low)
            if latent:
                ck, cv = ck_ref[c], cv_ref[c]
                ckk_scr[c] = jnp.concatenate([ck, ck], axis=1).astype(BF16)
                cvt_scr[c] = vt_aug(jnp.concatenate([cv, cv], axis=1), True)

    if latent:
        span = tq + 2 * WINDOW
        start = pl.multiple_of(jnp.clip(qb * tq - WINDOW, 0, L - span), WINDOW)
        blk0 = start // pw
        s_pos = start + lax.broadcasted_iota(jnp.int32, (span, tq), 0)
        t_pos = qb * tq + lax.broadcasted_iota(jnp.int32, (span, tq), 1)
        win_bias = jnp.where(jnp.abs(t_pos - s_pos) <= WINDOW, 0.0, NEG)
    else:
        span, blk0 = L, 0

    nt = (((1,), (1,)), ((), ()))
    low_q = lax.broadcasted_iota(jnp.int32, (tq, pw), 1) < ATT_HD
    def group_scores(c):
        qc = _seg_rms(q_ref[:, c * gw:(c + 1) * gw].astype(F32)) * qg_ref[...]
        if latent:
            qc = qc * cosq_ref[...] + _swap16(qc) * sinq_ref[...]
            kw = kk_scr[c, pl.ds(start, span), :]
        else:
            kw = kk_scr[c]
        qc = qc * (LOG2E / math.sqrt(ATT_HD))
        scores = []
        for g in range(ATT_KV):
            qt = qc[:, (g // 2) * pw:(g // 2 + 1) * pw]
            qm = jnp.where(low_q if g % 2 == 0 else ~low_q, qt, 0.0).astype(BF16)
            lw = lax.dot_general(kw, qm, nt, preferred_element_type=F32)
            lc = lax.dot_general(ckk_scr[c], qm, nt, preferred_element_type=F32) if latent else None
            scores.append((lw, lc))
        return scores

    def group_outputs(c, scores):
        vw = jnp.concatenate([vt_scr[c, blk0 + j] for j in range(span // pw)], axis=1)
        outs = []
        for g in range(ATT_KV):
            head = c * ATT_KV + g
            sink = sink_ref[:, head:head + 1] * LOG2E
            lw, lc = scores[g]
            if latent:
                lw = lw + win_bias
                mx = jnp.maximum(jnp.maximum(jnp.max(lw, axis=0, keepdims=True),
                                             jnp.max(lc, axis=0, keepdims=True)), sink)
                r = jnp.dot(vw, _exp2_bf16(lw - mx), preferred_element_type=F32) + jnp.dot(
                    cvt_scr[c], _exp2_bf16(lc - mx), preferred_element_type=F32)
            else:
                mx = jnp.maximum(jnp.max(lw, axis=0, keepdims=True), sink)
                r = jnp.dot(vw, _exp2_bf16(lw - mx), preferred_element_type=F32)
            den = r[ATT_HD:ATT_HD + 1, :] + jnp.exp2(sink - mx)
            outs.append(r[0:ATT_HD, :] / den)
        for t in range(2):
            o_ref[:, c * gw + t * pw:c * gw + (t + 1) * pw] = (
                jnp.concatenate(outs[2 * t:2 * t + 2], axis=0).T.astype(o_ref.dtype))

    scores = group_scores(0)
    for c in range(ATT_KV):
        nxt = group_scores(c + 1) if c + 1 < ATT_KV else None
        group_outputs(c, scores)
        scores = nxt


def attention(q, kv, row0, q_g, k_g, sink, B, L, cache=None, tq=256):
    latent = cache is not None
    gw = ATT_KV * ATT_HD
    qg = jnp.tile(q_g, ATT_KV)[None]
    kg = jnp.tile(k_g, ATT_KV)[None]
    nq = L // tq
    const = lambda a: pl.BlockSpec(a.shape, lambda b, i: (0,) * a.ndim)
    in_specs = [pl.BlockSpec((tq, ATT_HEADS * ATT_HD), lambda b, i: (row0 // tq + b * nq + i, 0)),
                pl.BlockSpec((L, 2 * gw), lambda b, i: (row0 // L + b, 0))]
    args = [q, kv]
    vrows = ATT_HD + 16
    scratch = [pltpu.VMEM((ATT_KV, L, 2 * ATT_HD), BF16),
               pltpu.VMEM((ATT_KV, L // (2 * ATT_HD), vrows, 2 * ATT_HD), BF16)]
    out_specs = [pl.BlockSpec((tq, ATT_HEADS * ATT_HD), lambda b, i: (b * nq + i, 0))]
    out_shape = [jax.ShapeDtypeStruct((B * L, ATT_HEADS * ATT_HD), BF16)]
    if latent:
        ck, cv = cache
        P = ck.shape[2]
        cos, sin = (jnp.asarray(t) for t in _rope_tables(L))
        in_specs += [pl.BlockSpec((None, ATT_KV, P, ATT_HD), lambda b, i: (b, 0, 0, 0))] * 2
        args += [ck, cv]
        in_specs += [const(qg), const(kg), pl.BlockSpec((1, ATT_HEADS), lambda b, i: (0, 0)),
                     pl.BlockSpec((tq, gw), lambda b, i: (i, 0)), pl.BlockSpec((tq, gw), lambda b, i: (i, 0)),
                     const(cos), const(sin)]
        args += [qg, kg, sink[None], cos, sin, cos, sin]
        scratch += [pltpu.VMEM((ATT_KV, P, 2 * ATT_HD), BF16), pltpu.VMEM((ATT_KV, vrows, P), BF16)]
    else:
        in_specs += [const(qg), const(kg), pl.BlockSpec((1, ATT_HEADS), lambda b, i: (0, 0))]
        args += [qg, kg, sink[None]]
        cache_spec = pl.BlockSpec((None, ATT_KV, L, ATT_HD), lambda b, i: (b, 0, 0, 0))
        out_specs += [cache_spec, cache_spec]
        out_shape += [jax.ShapeDtypeStruct((B, ATT_KV, L, ATT_HD), F32)] * 2
    outs = pl.pallas_call(
        functools.partial(_attn_kernel, latent=latent, tq=tq),
        grid=(B, nq),
        in_specs=in_specs, out_specs=out_specs, out_shape=out_shape,
        scratch_shapes=scratch,
        compiler_params=_cparams("parallel", "arbitrary"),
        name="attn_latent" if latent else "attn_context",
    )(*args)
    return outs[0] if latent else outs


MOE_TM = 1024
MOE_TOK = 1024
RUN_ALIGN = 16
MOE_LOCAL = 2 * MOE_TOK + N_EXPERTS * RUN_ALIGN
MOE_MAX_TILES = (2 * 16384 + (16384 // MOE_TOK) * N_EXPERTS * (RUN_ALIGN - 1)) // MOE_TM + N_EXPERTS + 1
RUN_SIZES = tuple(RUN_ALIGN << b for b in range(7, -1, -1))


def _router_kernel(*refs, proj):
    n = _n_proj_refs(**proj)
    g_ref, sh_ref, sc_ref, wr_ref, br_ref, tri_ref, x_ref, lp_ref, wts_ref, runs_ref, cnt_ref = refs[n:]

    @pl.when(pl.program_id(0) == 0)
    def _():
        cnt_ref[...] = jnp.zeros_like(cnt_ref)

    x = _proj_value(refs[:n], pl.program_id(0) < proj["n_p"], **proj)
    x_ref[...] = x
    h = _norm_mod(x, g_ref[...], sh_ref[...], sc_ref[...])
    lg = lax.dot_general(wr_ref[...], h, (((1,), (1,)), ((), ())), precision=HIGHEST,
                         preferred_element_type=F32) + br_ref[...]
    row = lax.broadcasted_iota(jnp.int32, lg.shape, 0)
    m1 = jnp.max(lg, axis=0, keepdims=True)
    i1 = jnp.min(jnp.where(lg == m1, row, N_EXPERTS), axis=0, keepdims=True)
    l2 = jnp.where(row == i1, -jnp.inf, lg)
    m2 = jnp.max(l2, axis=0, keepdims=True)
    i2 = jnp.min(jnp.where(l2 == m2, row, N_EXPERTS), axis=0, keepdims=True)
    e2 = jnp.exp(m2 - m1)
    w1 = 1.0 / (1.0 + e2)
    wts_ref[...] = jnp.concatenate([w1, e2 * w1], axis=0)
    oh1 = (row == i1).astype(F32)
    oh2 = (row == i2).astype(F32)
    cs1 = _bdot(oh1, tri_ref[...])
    cs2 = _bdot(oh2, tri_ref[...])
    tot1 = jnp.sum(oh1, axis=1, keepdims=True)
    run = jnp.ceil((tot1 + jnp.sum(oh2, axis=1, keepdims=True)) * (1.0 / RUN_ALIGN)) * RUN_ALIGN
    run_b = jnp.broadcast_to(run, (N_EXPERTS, 128))
    er = lax.broadcasted_iota(jnp.int32, (N_EXPERTS, N_EXPERTS), 0)
    ec = lax.broadcasted_iota(jnp.int32, (N_EXPERTS, N_EXPERTS), 1)
    start = jnp.dot((ec < er).astype(F32), run_b, precision=HIGHEST, preferred_element_type=F32)
    last = lax.broadcasted_iota(jnp.int32, (N_EXPERTS, 128), 0) == N_EXPERTS - 1
    run_b = jnp.where(last, MOE_LOCAL - start, run_b)
    st = start[:, 0:1]
    p1 = jnp.sum(oh1 * (st + cs1), axis=0, keepdims=True)
    p2 = jnp.sum(oh2 * (st + tot1 + cs2), axis=0, keepdims=True)
    lp_ref[...] = jnp.concatenate([p1, p2], axis=0).astype(jnp.int32)
    lane = lax.broadcasted_iota(jnp.int32, (N_EXPERTS, 128), 1)
    runs_ref[...] = jnp.where(lane == 0, run_b, jnp.where(lane == 1, start, cnt_ref[...]))
    cnt_ref[...] = cnt_ref[...] + run_b


def moe_router(acts, w_out, x, g, mods, w_router, b_router, tm=MOE_TOK):
    T = x.shape[0]
    tri = jnp.asarray(np.triu(np.ones((tm, tm), np.float32), k=1)).astype(BF16)
    tok2 = lambda dt: jax.ShapeDtypeStruct((2, T), dt)
    p_specs, p_args, proj = _proj_inputs(acts, w_out, x, mods, tm)
    return pl.pallas_call(
        functools.partial(_router_kernel, proj=proj),
        grid=(T // tm,),
        in_specs=p_specs + [
                  pl.BlockSpec((1, D), lambda i: (0, 0)),
                  _mod_spec(3, tm), _mod_spec(4, tm),
                  pl.BlockSpec((N_EXPERTS, D), lambda i: (0, 0)),
                  pl.BlockSpec((N_EXPERTS, 1), lambda i: (0, 0)),
                  _const_spec(tri, 1)],
        out_specs=[pl.BlockSpec((tm, D), lambda i: (i, 0)),
                   pl.BlockSpec((2, tm), lambda i: (0, i)),
                   pl.BlockSpec((2, tm), lambda i: (0, i)),
                   pl.BlockSpec((None, N_EXPERTS, 128), lambda i: (i, 0, 0)),
                   pl.BlockSpec((N_EXPERTS, 128), lambda i: (0, 0))],
        out_shape=[jax.ShapeDtypeStruct((T, D), F32), tok2(jnp.int32), tok2(F32),
                   jax.ShapeDtypeStruct((T // tm, N_EXPERTS, 128), F32),
                   jax.ShapeDtypeStruct((N_EXPERTS, 128), F32)],
        compiler_params=_cparams("arbitrary"),
        name="moe_router",
    )(*p_args, g, mods, mods, w_router.T, b_router[:, None], tri)


def moe_layout(runs, totals):
    rows = totals[:, 0].astype(jnp.int32)
    tiles = (rows + MOE_TM - 1) // MOE_TM
    tile_end = jnp.cumsum(tiles)
    group = (tile_end - tiles) * MOE_TM
    run_len = runs[:, :, 0].astype(jnp.int32)
    run_src = runs[:, :, 1].astype(jnp.int32)
    run_dst = group[None, :] + runs[:, :, 2].astype(jnp.int32)
    tail = jnp.stack([group + rows, tiles * MOE_TM - rows]).astype(jnp.int32)
    n_tiles = tile_end[-1]
    t = jnp.arange(MOE_MAX_TILES, dtype=jnp.int32)
    tile_e = jnp.sum(t[:, None] >= tile_end[None, :], axis=1).astype(jnp.int32)
    last_e = jnp.sum((n_tiles - 1) >= tile_end).astype(jnp.int32)
    tile_e = jnp.where(t < n_tiles, tile_e, last_e)
    first = jnp.sum(jnp.where(tile_e[:, None] == jnp.arange(N_EXPERTS), (tile_end - tiles)[None, :], 0), axis=1)
    e_rows = jnp.sum(jnp.where(tile_e[:, None] == jnp.arange(N_EXPERTS), rows[None, :], 0), axis=1)
    tile_rows = jnp.where(t < n_tiles, jnp.clip(e_rows - (t - first) * MOE_TM, 0, MOE_TM), 0).astype(jnp.int32)
    run_tab = jnp.stack([run_len, run_src, run_dst]).reshape(3, -1)
    tile_info = jnp.stack([tile_rows, ((t == first) & (t < n_tiles)).astype(jnp.int32)])
    return run_tab, tail, tile_e, n_tiles.astype(jnp.int32).reshape(1), tile_info


def _run_copies(tab_ref, i, local_ref, global_ref, sem, to_global):
    out = []
    for e in range(N_EXPERTS):
        k = i * N_EXPERTS + e
        n, src, dst = tab_ref[0, k], tab_ref[1, k], tab_ref[2, k]
        for size in RUN_SIZES:
            done = (n // (2 * size)) * (2 * size)
            loc = local_ref.at[pl.ds(pl.multiple_of(src + done, RUN_ALIGN), size), :]
            glo = global_ref.at[pl.ds(pl.multiple_of(dst + done, RUN_ALIGN), size), :]
            copy = pltpu.make_async_copy(loc, glo, sem) if to_global else pltpu.make_async_copy(glo, loc, sem)
            out.append(((n & size) != 0, copy))
    return out


def _start(copies, live=True):
    for pred, copy in copies:
        pl.when(pred & live)(copy.start)


def _wait(copies, live=True):
    for pred, copy in copies:
        pl.when(pred & live)(copy.wait)


def _start_then_wait(copies):
    _start(copies)
    _wait(copies)


def _dispatch_kernel(tab_ref, tail_ref, nt_ref, lp_ref, x_ref, g_ref, sh_ref, sc_ref, xs_ref, hs_scr, z_scr, sem):
    i = pl.program_id(0)
    tm = x_ref.shape[0]
    buf = i % 2
    h = _norm_mod(x_ref[...], g_ref[...], sh_ref[...], sc_ref[...]).astype(BF16)
    slot = lax.broadcasted_iota(jnp.int32, (MOE_LOCAL, tm), 0)
    perm = jnp.where((slot == lp_ref[0:1, :]) | (slot == lp_ref[1:2, :]), 1.0, 0.0).astype(BF16)
    hs_scr[buf] = jnp.dot(perm, h, preferred_element_type=F32).astype(BF16)
    copies = _run_copies(tab_ref, i, hs_scr.at[buf], xs_ref, sem.at[buf], to_global=True)
    _start(copies)
    _wait(_run_copies(tab_ref, jnp.maximum(i - 1, 0), hs_scr.at[1 - buf], xs_ref, sem.at[1 - buf], to_global=True),
          live=i > 0)

    @pl.when(i == 0)
    def _():
        z_scr[...] = jnp.zeros_like(z_scr)
        zrows = z_scr.shape[0]

        def zero_tile(t, carry):
            for part in range(MOE_TM // zrows):
                dst = xs_ref.at[pl.ds(pl.multiple_of(t * MOE_TM + part * zrows, zrows), zrows), :]
                copy = pltpu.make_async_copy(z_scr, dst, sem.at[2])
                copy.start()
                copy.wait()
            return carry

        lax.fori_loop(nt_ref[0], MOE_MAX_TILES, zero_tile, 0)
        tails = []
        for e in range(N_EXPERTS):
            start, n = tail_ref[0, e], tail_ref[1, e]
            for size in RUN_SIZES:
                if size >= MOE_TM:
                    continue
                done = (n // (2 * size)) * (2 * size)
                dst = xs_ref.at[pl.ds(pl.multiple_of(start + done, RUN_ALIGN), size), :]
                tails.append(((n & size) != 0, pltpu.make_async_copy(z_scr.at[pl.ds(0, size), :], dst, sem.at[2])))
        _start_then_wait(tails)

    _wait(copies, live=i == pl.num_programs(0) - 1)


def moe_dispatch(x, g, mods, lp, run_tab, tail, n_tiles, tm=MOE_TOK):
    T = x.shape[0]
    n_rows = MOE_MAX_TILES * MOE_TM
    return pl.pallas_call(
        _dispatch_kernel,
        grid_spec=pltpu.PrefetchScalarGridSpec(
            num_scalar_prefetch=3,
            grid=(T // tm,),
            in_specs=[pl.BlockSpec((2, tm), lambda i, *_: (0, i)),
                      pl.BlockSpec((tm, D), lambda i, *_: (i, 0)),
                      pl.BlockSpec((1, D), lambda i, *_: (0, 0)),
                      _mod_spec(3, tm), _mod_spec(4, tm)],
            out_specs=pl.BlockSpec(memory_space=pl.ANY),
            scratch_shapes=[pltpu.VMEM((2, MOE_LOCAL, D), BF16), pltpu.VMEM((MOE_TM // 2, D), BF16),
                            pltpu.SemaphoreType.DMA((3,))]),
        out_shape=jax.ShapeDtypeStruct((n_rows, D), BF16),
        compiler_params=_cparams("arbitrary"),
        name="moe_dispatch",
    )(run_tab, tail, n_tiles, lp, x, g, mods, mods)


def _moe_group_kernel(te_ref, nt_ref, ti_ref, x_ref, w1_ref, w3_ref, w2_ref, o_ref, acc_scr, b1_scr, b3_scr, b2_scr):
    t, c = pl.program_id(0), pl.program_id(1)
    rows = ti_ref[0, t]
    half = MOE_TM // 2

    @pl.when(t < nt_ref[0])
    def _():
        @pl.when(c == 0)
        def _():
            acc_scr[...] = jnp.zeros_like(acc_scr)

        @pl.when(ti_ref[1, t] == 1)
        def _():
            b1_scr[c] = w1_ref[...].astype(BF16)
            b3_scr[c] = w3_ref[...].astype(BF16)
            b2_scr[c] = w2_ref[...].astype(BF16)

        @pl.when(rows > half)
        def _():
            _swiglu_accumulate(x_ref, acc_scr, b1_scr.at[c], b3_scr.at[c], b2_scr.at[c], MOE_TM)

        @pl.when(rows <= half)
        def _():
            _swiglu_accumulate(x_ref, acc_scr, b1_scr.at[c], b3_scr.at[c], b2_scr.at[c], half)

        @pl.when(c == pl.num_programs(1) - 1)
        def _():
            o_ref[...] = acc_scr[...].astype(o_ref.dtype)

    @pl.when((t >= nt_ref[0]) & (c == pl.num_programs(1) - 1))
    def _():
        o_ref[...] = jnp.zeros_like(o_ref)


def moe_grouped_swiglu(xs, tile_e, n_tiles, tile_info, w1, w3, w2, tf=256):
    nc = D_FF // tf
    live = lambda t, nt: t < nt[0]
    row = lambda t, c, te, nt, ti: (jnp.where(live(t, nt), t, jnp.maximum(nt[0] - 1, 0)), 0)
    wcol = lambda t, c, te, nt, ti: (te[t], 0, jnp.where(ti[1, t] == 1, c, nc - 1))
    wrow = lambda t, c, te, nt, ti: (te[t], jnp.where(ti[1, t] == 1, c, nc - 1), 0)
    return pl.pallas_call(
        _moe_group_kernel,
        grid_spec=pltpu.PrefetchScalarGridSpec(
            num_scalar_prefetch=3,
            grid=(MOE_MAX_TILES, nc),
            in_specs=[pl.BlockSpec((MOE_TM, D), row),
                      pl.BlockSpec((None, D, tf), wcol),
                      pl.BlockSpec((None, D, tf), wcol),
                      pl.BlockSpec((None, tf, D), wrow)],
            out_specs=pl.BlockSpec((MOE_TM, D), lambda t, c, te, nt, ti: (t, 0)),
            scratch_shapes=[pltpu.VMEM((MOE_TM, D), F32), pltpu.VMEM((nc, D, tf), BF16),
                            pltpu.VMEM((nc, D, tf), BF16), pltpu.VMEM((nc, tf, D), BF16)]),
        out_shape=jax.ShapeDtypeStruct(xs.shape, BF16),
        compiler_params=_cparams("arbitrary", "arbitrary"),
        name="moe_grouped",
    )(tile_e, n_tiles, tile_info, xs, w1, w3, w2)


def _combine_kernel(tab_ref, lp_ref, wt_ref, x_ref, gate_ref, ys_ref, op_ref, os_ref, yl_scr, sem, *, n_p):
    i = pl.program_id(0)
    tm = x_ref.shape[0]
    buf = i % 2
    last = pl.num_programs(0) - 1
    gather = lambda t, b: _run_copies(tab_ref, t, yl_scr.at[b], ys_ref, sem.at[b], to_global=False)
    _start(gather(i, buf), live=i == 0)
    _start(gather(jnp.minimum(i + 1, last), 1 - buf), live=i < last)
    slot = lax.broadcasted_iota(jnp.int32, (tm, MOE_LOCAL), 1)
    mix = (jnp.where(slot == lp_ref[:, 0:1], wt_ref[:, 0:1], 0.0)
           + jnp.where(slot == lp_ref[:, 1:2], wt_ref[:, 1:2], 0.0)).astype(BF16)
    _wait(gather(i, buf))
    moe = jnp.dot(mix, yl_scr[buf], preferred_element_type=F32)
    out = x_ref[...] + gate_ref[...] * moe

    @pl.when(pl.program_id(0) < n_p)
    def _():
        op_ref[...] = out

    @pl.when(pl.program_id(0) >= n_p)
    def _():
        os_ref[...] = out


def moe_combine(x, mods, lp, wts, run_tab, ys, t_prompt, tm=MOE_TOK):
    T = x.shape[0]
    n_p = t_prompt // tm
    return pl.pallas_call(
        functools.partial(_combine_kernel, n_p=n_p),
        grid_spec=pltpu.PrefetchScalarGridSpec(
            num_scalar_prefetch=1,
            grid=(T // tm,),
            in_specs=[pl.BlockSpec((tm, 2), lambda i, *_: (i, 0)),
                      pl.BlockSpec((tm, 2), lambda i, *_: (i, 0)),
                      pl.BlockSpec((tm, D), lambda i, *_: (i, 0)),
                      _mod_spec(5, tm),
                      pl.BlockSpec(memory_space=pl.ANY)],
            out_specs=_part_specs((tm, D), n_p),
            scratch_shapes=[pltpu.VMEM((2, MOE_LOCAL, D), BF16), pltpu.SemaphoreType.DMA((2,))]),
        out_shape=[jax.ShapeDtypeStruct((t_prompt, D), F32), jax.ShapeDtypeStruct((T - t_prompt, D), F32)],
        compiler_params=_cparams("arbitrary"),
        name="moe_combine",
    )(run_tab, lp.T, wts.T, x, mods, ys)


def kernel(x_prompt, x_sample, state_C, state_n, state_m, cache_k, cache_v, c, c_ctx, norm1_g, norm2_g, w_ada, b_ada, ev_w_in, ev_conv, hy_w1, hy_b1, hy_w2, hy_b2, hy_w3, hy_freq, hy_d, ml_b_gate, ml_norm_g, ev_w_out, ff_w1, ff_w3, ff_w2, at_w_qkv, at_q_g, at_k_g, at_sink, at_w_out, moe_w_router, moe_b_router, moe_w1, moe_w3, moe_w2):
    BP, LP, _ = x_prompt.shape
    BS, LS, _ = x_sample.shape
    TP = BP * LP
    assert TP % GROUP == 0 and TP // GROUP == N_PROMPT_GROUPS and LS == GROUP and BS == 8

    xp, xs = x_prompt.reshape(TP, D), x_sample.reshape(BS * LS, D)
    cond = jnp.concatenate([c_ctx[None], c, jnp.zeros((16 - 1 - BS, D), F32)], axis=0)
    mods = adaln_table(cond, w_ada, b_ada)

    u, gates = even_in_proj(xp, xs, norm1_g[0:1], mods[0], ev_w_in[0], ml_b_gate[0].reshape(1, N_GATES))
    hy = []
    for seq0, B, L, nb in ((0, BP, LP, 4), (TP // LS, BS, LS, 1)):
        fwd, inv = (jnp.asarray(t).astype(BF16) for t in _dft_tables(L))
        ka, kb = hyena_filter_spectra(L, hy_w1[0], hy_b1[0], hy_w2[0], hy_b2[0], hy_w3[0], hy_freq[0], fwd)
        hy.append(hyena_mix(u, seq0, B, L, ev_conv[0], hy_d[0], fwd, inv, ka, kb, nb))
    ml_p, new_C, new_n, new_m = mlstm_mix(u, gates, 0, BP, LP, ml_norm_g[0], want_state=True)
    ml_s = mlstm_mix(u, gates, TP // LS, BS, LS, ml_norm_g[0],
                     state=(state_C[:, 0], state_n[:, 0], state_m[:, 0]))
    x = ffn_residual([hy, (ml_p, ml_s)], ev_w_out[0], (xp, xs), norm2_g[0:1], mods[0], ff_w1[0], ff_w3[0], ff_w2[0])

    q, kv = qkv_proj(x, norm1_g[1:2], mods[1], at_w_qkv[0])
    o_p, new_k, new_v = attention(q, kv, 0, at_q_g[0], at_k_g[0], at_sink[0], BP, LP)
    o_s = attention(q, kv, TP, at_q_g[0], at_k_g[0], at_sink[0], BS, LS, cache=(cache_k[:, 0], cache_v[:, 0]))
    x, lp, wts, runs, totals = moe_router([(o_p, o_s)], at_w_out[0], x, norm2_g[1:2], mods[1],
                                          moe_w_router[0], moe_b_router[0])
    run_tab, tail, tile_e, n_tiles, tile_info = moe_layout(runs, totals)
    xsort = moe_dispatch(x, norm2_g[1:2], mods[1], lp, run_tab, tail, n_tiles)
    ysort = moe_grouped_swiglu(xsort, tile_e, n_tiles, tile_info, moe_w1[0], moe_w3[0], moe_w2[0])
    yp, ys = moe_combine(x, mods[1], lp, wts, run_tab, ysort, TP)

    return (yp.reshape(BP, LP, D), ys.reshape(BS, LS, D),
            new_C[:, None], new_n[:, None], new_m[:, None], new_k[:, None], new_v[:, None])
```

```python
import functools
import math

import numpy as np
import jax
import jax.numpy as jnp
from jax import lax
from jax.experimental import pallas as pl
from jax.experimental.pallas import tpu as pltpu

F32 = jnp.float32
BF16 = jnp.bfloat16
HIGHEST = lax.Precision.HIGHEST

D = 1024
GROUP = 1024
N_PROMPT_GROUPS = 8
HY_W = 512
ML_HEADS = 4
ML_HD = 128
ML_CHUNK = 256
EVEN_MAIN = 3 * HY_W + 4 * 512
N_GATES = 16
ATT_HD = 64
ATT_HEADS = 16
ATT_KV = 4
WINDOW = 128
GRID_W = 64
ROPE_BASE = 10000.0
D_FF = 2816
N_EXPERTS = 8
EPS = 1e-6
NEG = -1e30
VMEM_LIMIT = 56 * 1024 * 1024


def _cparams(*sem, flags=None):
    return pltpu.CompilerParams(dimension_semantics=sem, vmem_limit_bytes=VMEM_LIMIT, flags=flags)


def _mod_row(i, tm):
    return jnp.maximum(i * tm // GROUP - (N_PROMPT_GROUPS - 1), 0)


def _silu(x):
    return x * jax.nn.sigmoid(x)


def _bdot(a, b):
    return jnp.dot(a.astype(BF16), b.astype(BF16), preferred_element_type=F32)


def _norm_mod(x, g, sh, sc):
    y = x * lax.rsqrt(jnp.mean(x * x, axis=-1, keepdims=True) + EPS) * g
    return y * (1.0 + sc) + sh


def _adaln_kernel(c_ref, w_ref, b_ref, o_ref):
    s = _silu(c_ref[...])
    o_ref[...] = jnp.dot(s, w_ref[...], precision=HIGHEST, preferred_element_type=F32) + b_ref[...]


def adaln_table(cond, w_ada, b_ada):
    depth = w_ada.shape[0]
    tn = 1536
    out = pl.pallas_call(
        _adaln_kernel,
        grid=(depth, 6 * D // tn),
        in_specs=[pl.BlockSpec((16, D), lambda l, j: (0, 0)),
                  pl.BlockSpec((None, D, tn), lambda l, j: (l, 0, j)),
                  pl.BlockSpec((None, 1, tn), lambda l, j: (l, 0, j))],
        out_specs=pl.BlockSpec((None, 16, tn), lambda l, j: (l, 0, j)),
        out_shape=jax.ShapeDtypeStruct((depth, 16, 6 * D), F32),
        compiler_params=_cparams("parallel", "parallel"),
        name="adaln",
    )(cond, w_ada, b_ada.reshape(depth, 1, 6 * D))
    return out.reshape(depth, 16, 1, 6 * D)


def _same_tile(i):
    return i


def _part_specs(block, n_p, tile_of=_same_tile):
    return [pl.BlockSpec(block, lambda i, *_: (jnp.minimum(tile_of(i), n_p - 1), 0)),
            pl.BlockSpec(block, lambda i, *_: (jnp.maximum(tile_of(i) - n_p, 0), 0))]


def _pick(is_prompt, p_ref, s_ref):
    return jnp.where(is_prompt, p_ref[...], s_ref[...])


def _mod_spec(k, tm, tile_of=_same_tile):
    return pl.BlockSpec((None, 1, D), lambda i, *_: (_mod_row(tile_of(i), tm), 0, k))


def _log_sigmoid(x):
    return jnp.minimum(x, 0.0) - jnp.log(1.0 + jnp.exp(-jnp.abs(x)))


def _split3(x):
    hi = x.astype(BF16)
    r = x - hi.astype(F32)
    mid = r.astype(BF16)
    return hi, mid, (r - mid.astype(F32)).astype(BF16)


def _even_in_kernel(xp_ref, xs_ref, g_ref, sh_ref, sc_ref, w_ref, bg_ref, lo_ref, up_ref, u_ref, gate_ref, *, n_p, tn):
    is_prompt = pl.program_id(0) < n_p
    h = _norm_mod(_pick(is_prompt, xp_ref, xs_ref), g_ref[...], sh_ref[...], sc_ref[...]).astype(BF16)
    for j in range(EVEN_MAIN // tn):
        u_ref[:, j * tn:(j + 1) * tn] = _bdot(h, w_ref[:, j * tn:(j + 1) * tn]).astype(u_ref.dtype)
    gates = _bdot(h, w_ref[:, EVEN_MAIN:]) + bg_ref[...]
    lf = _log_sigmoid(gates)
    col = lax.broadcasted_iota(jnp.int32, (1, N_GATES), 1)
    is_forget = (col // ML_HEADS) % 2 == 1
    is_rev = col >= N_GATES // 2
    for ch in range(h.shape[0] // ML_CHUNK):
        sl = slice(ch * ML_CHUNK, (ch + 1) * ML_CHUNK)
        parts = _split3(lf[sl])
        cf = sum(jnp.dot(lo_ref[...], p, preferred_element_type=F32) for p in parts)
        cr = sum(jnp.dot(up_ref[...], p, preferred_element_type=F32) for p in parts)
        gate_ref[sl, :] = jnp.where(is_forget, jnp.where(is_rev, cr, cf), gates[sl])


def even_in_proj(xp, xs, g, mods, w_in, b_gate, tm=1024, tn=512):
    T = xp.shape[0] + xs.shape[0]
    tri = np.tril(np.ones((ML_CHUNK, ML_CHUNK), np.float32))
    lo, up = jnp.asarray(tri).astype(BF16), jnp.asarray(tri.T).astype(BF16)
    return pl.pallas_call(
        functools.partial(_even_in_kernel, n_p=xp.shape[0] // tm, tn=tn),
        grid=(T // tm,),
        in_specs=_part_specs((tm, D), xp.shape[0] // tm) + [
                  pl.BlockSpec((1, D), lambda i: (0, 0)),
                  _mod_spec(0, tm), _mod_spec(1, tm),
                  _const_spec(w_in, 1),
                  pl.BlockSpec((1, N_GATES), lambda i: (0, 0)),
                  _const_spec(lo, 1), _const_spec(up, 1)],
        out_specs=[pl.BlockSpec((tm, EVEN_MAIN), lambda i: (i, 0)),
                   pl.BlockSpec((tm, N_GATES), lambda i: (i, 0))],
        out_shape=[jax.ShapeDtypeStruct((T, EVEN_MAIN), BF16),
                   jax.ShapeDtypeStruct((T, N_GATES), F32)],
        compiler_params=_cparams("parallel"),
        name="even_in_proj",
    )(xp, xs, g, mods, mods, w_in, b_gate, lo, up)


def _dft_tables(L):
    n = 2 * L
    f = np.arange(L, dtype=np.int64)[:, None]
    s = np.arange(L, dtype=np.int64)[None, :]
    ang = 2.0 * np.pi * ((f * s) % n).astype(np.float64) / n
    fwd = np.concatenate([np.cos(ang), -np.sin(ang)], axis=0)
    fwd[L, :] = np.where(np.arange(L) % 2 == 0, 1.0, -1.0)
    t = np.arange(L, dtype=np.int64)[:, None]
    ff = np.arange(L, dtype=np.int64)[None, :]
    ang = 2.0 * np.pi * ((t * ff) % n).astype(np.float64) / n
    inv_re = 2.0 * np.cos(ang) / n
    inv_re[:, 0] = 1.0 / n
    inv_im = -2.0 * np.sin(ang) / n
    inv_im[:, 0] = np.where(np.arange(L) % 2 == 0, 1.0, -1.0) / n
    inv = np.concatenate([inv_re, inv_im], axis=1)
    return fwd.astype(np.float32), inv.astype(np.float32)


def _filter_tables(L):
    t = np.linspace(0.0, 1.0, L, dtype=np.float32).astype(np.float64)[:, None]
    w = 2.0 * math.pi * np.arange(L, dtype=np.float64)[:, None] / L
    bands = np.linspace(1e-4, 16 - 1, 16, dtype=np.float32).astype(np.float64)[None, :]
    z = np.concatenate([t, np.cos(bands * w), -np.sin(bands * w)], axis=-1)
    zp = np.zeros((L, 128), np.float64)
    zp[:, :z.shape[1]] = z
    max_decay = math.log(1e-2) / 0.3
    min_decay = math.log(1e-2) / 1.5
    deltas = np.linspace(min_decay, max_decay, HY_W, dtype=np.float32).astype(np.float64)
    decay = np.exp(-t * np.abs(deltas))
    return zp.astype(np.float32), decay.astype(np.float32)


def _hy_filter_kernel(z_ref, dec_ref, w1_ref, b1_ref, w2_ref, b2_ref, w3_ref, fr_ref, fwd_ref,
                      ka_ref, kb_ref):
    L = z_ref.shape[0]
    hdot = functools.partial(jnp.dot, precision=HIGHEST, preferred_element_type=F32)
    h = jnp.sin(fr_ref[0:1, :] * (hdot(z_ref[...], w1_ref[...]) + b1_ref[...]))
    h = jnp.sin(fr_ref[1:2, :] * (hdot(h, w2_ref[...]) + b2_ref[...]))
    h = hdot(h, w3_ref[...])
    row0 = lax.broadcasted_iota(jnp.int32, (L, 1), 0) == 0
    h0 = h[:, :HY_W] * dec_ref[...]
    h1 = h[:, HY_W:] * dec_ref[...]
    l1 = jnp.sum(jnp.abs(h0), axis=0, keepdims=True) + jnp.sum(jnp.abs(h1), axis=0, keepdims=True)
    inv = 1.0 / l1
    h0 = h0 * inv
    h1 = jnp.where(row0, 0.0, h1 * inv)
    f0 = _bdot(fwd_ref[...], h0)
    f1 = _bdot(fwd_ref[...], h1)
    ka_ref[...] = f0[:L] + f1[:L]
    kb_ref[...] = jnp.where(row0, f0[L:] + f1[L:], f0[L:] - f1[L:])


def _const_spec(a, n_grid):
    return pl.BlockSpec(a.shape, lambda *_: (0,) * a.ndim, pipeline_mode=pl.Buffered(1))


def hyena_filter_spectra(L, w1, b1, w2, b2, w3, freq, fwd):
    z, dec = _filter_tables(L)
    pad2 = lambda a, r, c: jnp.pad(a, ((0, r - a.shape[0]), (0, c - a.shape[1])))
    args = (jnp.asarray(z), jnp.asarray(dec), pad2(w1, 128, 128), pad2(b1[None], 1, 128),
            pad2(w2, 128, 128), pad2(b2[None], 1, 128), pad2(w3, 128, 4 * HY_W), pad2(freq, 2, 128), fwd)
    in_specs = [_const_spec(a, 1) for a in args]
    in_specs[6] = pl.BlockSpec((128, 2 * HY_W), lambda o: (0, o))
    shp = jax.ShapeDtypeStruct((2, L, HY_W), F32)
    out_spec = pl.BlockSpec((None, L, HY_W), lambda o: (o, 0, 0))
    return pl.pallas_call(
        _hy_filter_kernel,
        grid=(2,),
        in_specs=in_specs,
        out_specs=[out_spec, out_spec],
        out_shape=[shp, shp],
        compiler_params=_cparams("arbitrary"),
        name=f"hyena_filter_{L}",
    )(*args)


def _hyena_kernel(u_ref, cw_ref, d_ref, fwd_ref, inv_ref, ka_ref, kb_ref, o_ref):
    nb, L = u_ref.shape[0], u_ref.shape[1]
    row = lax.broadcasted_iota(jnp.int32, (L, 1), 0)
    first, last = row == 0, row == L - 1
    fwd = fwd_ref[...].astype(BF16)
    inv = inv_ref[...].astype(BF16)

    def long_conv(z, o):
        zf = jnp.dot(fwd, z.astype(BF16), preferred_element_type=F32)
        a, b = zf[:L], zf[L:]
        ka, kb = ka_ref[o], kb_ref[o]
        yr = a * ka - jnp.where(first, 0.0, b * kb)
        yi = jnp.where(first, b * kb, a * kb + b * ka)
        return (jnp.dot(inv[:, :L], yr.astype(BF16), preferred_element_type=F32)
                + jnp.dot(inv[:, L:], yi.astype(BF16), preferred_element_type=F32))

    for bi in range(nb):
        u = u_ref[bi].astype(F32)
        prev = jnp.where(first, 0.0, pltpu.roll(u, 1, 0))
        nxt = jnp.where(last, 0.0, pltpu.roll(u, L - 1, 0))
        u = prev * cw_ref[0:1, :] + u * cw_ref[1:2, :] + nxt * cw_ref[2:3, :]
        v, x1, x2 = u[:, :HY_W], u[:, HY_W:2 * HY_W], u[:, 2 * HY_W:]
        z = x1 * (long_conv(v, 0) + d_ref[0:1, :] * v)
        z = x2 * (long_conv(z, 1) + d_ref[1:2, :] * z)
        o_ref[bi] = z.astype(o_ref.dtype)


def hyena_mix(u, seq0, B, L, conv_w, d_skip, fwd, inv, ka, kb, nb):
    u3 = u.reshape(-1, L, EVEN_MAIN)
    full = lambda a: _const_spec(a, 1)
    out = pl.pallas_call(
        _hyena_kernel,
        grid=(B // nb,),
        in_specs=[pl.BlockSpec((nb, L, 3 * HY_W), lambda b: (b + seq0 // nb, 0, 0)),
                  full(conv_w), full(d_skip), full(fwd), full(inv), full(ka), full(kb)],
        out_specs=pl.BlockSpec((nb, L, HY_W), lambda b: (b, 0, 0)),
        out_shape=jax.ShapeDtypeStruct((B, L, HY_W), BF16),
        compiler_params=_cparams("parallel"),
        name=f"hyena_{L}",
    )(u3, conv_w, d_skip, fwd, inv, ka, kb)
    return out.reshape(B * L, HY_W)


def _mlstm_kernel(*refs, has_state, want_state):
    q_ref, k_ref, v_ref, o_ref, gc_ref, gr_ref, ng_ref = refs[:7]
    refs = refs[7:]
    if has_state:
        c0t_ref, n0b_ref, m0_ref = refs[:3]
        refs = refs[3:]
    y_ref = refs[0]
    if want_state:
        c_out, n_out, m_out = refs[1:4]
    L, d = q_ref.shape[0], ML_HD
    T = min(ML_CHUNK, L)
    nc = L // T
    scale = 1.0 / math.sqrt(d)
    nt = (((1,), (1,)), ((), ()))
    si = lax.broadcasted_iota(jnp.int32, (T, T), 0)
    ti = lax.broadcasted_iota(jnp.int32, (T, T), 1)
    allowed = (si <= ti, si >= ti)
    chains = [(dr, h) for dr in range(2) for h in range(ML_HEADS)]
    gcol = lambda dr, gi, h: dr * 2 * ML_HEADS + gi * ML_HEADS + h

    caug_t, m = {}, {}
    for ch in chains:
        dr, h = ch
        if has_state:
            caug_t[ch] = jnp.concatenate([c0t_ref[dr, h], n0b_ref[dr, h]], axis=0)
            m[ch] = m0_ref[dr, h:h + 1, 0:1]
        else:
            caug_t[ch], m[ch] = jnp.zeros((2 * d, d), F32), jnp.zeros((1, 1), F32)

    chunk_cache = {}

    def chunk_data(h, j):
        if (h, j) not in chunk_cache:
            sl, hl = slice(j * T, (j + 1) * T), slice(h * d, (h + 1) * d)
            q = q_ref[sl, hl]
            ks = (k_ref[sl, hl].astype(F32) * scale).astype(BF16)
            v_t = v_ref[sl, hl].astype(F32).T
            vaug_t = jnp.concatenate([v_t, jnp.ones((d, T), F32)], axis=0).astype(BF16)
            s_raw = lax.dot_general(ks, q, nt, preferred_element_type=F32)
            chunk_cache[(h, j)] = (q, ks, v_t, vaug_t, s_raw)
        return chunk_cache[(h, j)]

    h_sum = {}
    for it in range(nc):
        step = {ch: (it if ch[0] == 0 else nc - 1 - it) for ch in chains}
        data = {ch: chunk_data(ch[1], step[ch]) for ch in chains}
        inter_t = {ch: lax.dot_general(caug_t[ch].astype(BF16), data[ch][0], nt, preferred_element_type=F32)
                   for ch in chains}
        gate = {}
        for ch in chains:
            dr, h = ch
            sl = slice(step[ch] * T, (step[ch] + 1) * T)
            li_r, b_r = gr_ref[gcol(dr, 0, h):gcol(dr, 0, h) + 1, sl], gr_ref[gcol(dr, 1, h):gcol(dr, 1, h) + 1, sl]
            src = gc_ref[sl, gcol(dr, 0, h):gcol(dr, 0, h) + 1] - gc_ref[sl, gcol(dr, 1, h):gcol(dr, 1, h) + 1]
            dm = jnp.where(allowed[dr], src + b_r, NEG)
            inter = b_r + m[ch]
            m_t = jnp.maximum(inter, jnp.max(dm, axis=0, keepdims=True))
            b_end = b_r[:, T - 1:T] if dr == 0 else b_r[:, 0:1]
            g_r = b_end - b_r + li_r
            m_new = jnp.maximum(b_end + m[ch], jnp.max(g_r, axis=1, keepdims=True))
            gate[ch] = (jnp.exp(dm - m_t), jnp.exp(inter - m_t), jnp.exp(-m_t), jnp.exp(g_r - m_new),
                        jnp.exp(b_end + m[ch] - m_new), m_new)
        for ch in chains:
            q, ks, v_t, vaug_t, s_raw = data[ch]
            w_intra, w_inter, floor, w_tok, decay, m_new = gate[ch]
            acc = jnp.dot(vaug_t, (s_raw * w_intra).astype(BF16), preferred_element_type=F32) + w_inter * inter_t[ch]
            h_t = acc[:d] / jnp.maximum(jnp.abs(acc[d:]), floor)
            key = (ch[1], step[ch])
            h_sum[key] = h_t if key not in h_sum else h_sum[key] + h_t
            vw_t = jnp.concatenate([v_t * w_tok, jnp.broadcast_to(w_tok, (d, T))], axis=0).astype(BF16)
            caug_t[ch] = decay * caug_t[ch] + jnp.dot(vw_t, ks, preferred_element_type=F32)
            m[ch] = m_new

    if want_state:
        for ch in chains:
            dr, h = ch
            c_out[dr, h] = caug_t[ch][:d].T
            n_out[dr, h:h + 1, :] = caug_t[ch][d:d + 1, :]
            m_out[dr, h:h + 1, :] = jnp.broadcast_to(m[ch], (1, d))
    for h in range(ML_HEADS):
        for j in range(nc):
            sl, hl = slice(j * T, (j + 1) * T), slice(h * d, (h + 1) * d)
            hv = h_sum[(h, j)].T
            y = hv * lax.rsqrt(jnp.mean(hv * hv, axis=-1, keepdims=True) + EPS) * ng_ref[:, hl]
            y_ref[sl, hl] = (y * jax.nn.sigmoid(o_ref[sl, hl].astype(F32))).astype(y_ref.dtype)


def mlstm_mix(u, gates, seq0, B, L, norm_g, state=None, want_state=False):
    u3 = u.reshape(-1, L, EVEN_MAIN)
    gc = gates.reshape(-1, L, N_GATES)[seq0:seq0 + B]
    gr = gc.transpose(0, 2, 1)
    width = ML_HEADS * ML_HD
    col = lambda i: pl.BlockSpec((None, L, width), lambda b: (b + seq0, 0, (3 * HY_W + i * width) // width))
    in_specs = [col(0), col(1), col(2), col(3),
                pl.BlockSpec((None, L, N_GATES), lambda b: (b, 0, 0)),
                pl.BlockSpec((None, N_GATES, L), lambda b: (b, 0, 0)),
                pl.BlockSpec((1, width), lambda b: (0, 0))]
    args = [u3, u3, u3, u3, gc, gr, norm_g.reshape(1, width)]
    sspec = pl.BlockSpec((None, 2, ML_HEADS, ML_HD, ML_HD), lambda b: (b, 0, 0, 0, 0))
    vspec = pl.BlockSpec((None, 2, ML_HEADS, ML_HD), lambda b: (b, 0, 0, 0))
    if state is not None:
        C0, n0, m0 = state
        in_specs += [sspec, sspec, vspec]
        args += [C0.swapaxes(-1, -2), jnp.broadcast_to(n0[..., None, :], C0.shape),
                 jnp.broadcast_to(m0[..., None], n0.shape)]
    out_specs = [pl.BlockSpec((None, L, width), lambda b: (b, 0, 0))]
    out_shape = [jax.ShapeDtypeStruct((B, L, width), BF16)]
    if want_state:
        out_specs += [sspec, vspec, vspec]
        out_shape += [jax.ShapeDtypeStruct((B, 2, ML_HEADS, ML_HD, ML_HD), F32),
                      jax.ShapeDtypeStruct((B, 2, ML_HEADS, ML_HD), F32),
                      jax.ShapeDtypeStruct((B, 2, ML_HEADS, ML_HD), F32)]
    outs = pl.pallas_call(
        functools.partial(_mlstm_kernel, has_state=state is not None, want_state=want_state),
        grid=(B,),
        in_specs=in_specs, out_specs=out_specs, out_shape=out_shape,
        compiler_params=_cparams("parallel"),
        name=f"mlstm_{L}",
    )(*args)
    y = outs[0].reshape(B * L, width)
    if not want_state:
        return y
    _, C, n, m = outs
    return y, C, n, m[..., 0]


def _proj_inputs(acts, w, x, mods, tm, tile_of=_same_tile):
    xs = tuple(x) if isinstance(x, (tuple, list)) else (x,)
    n_p = acts[0][0].shape[0] // tm
    specs = []
    for pair in acts:
        specs += _part_specs((tm, pair[0].shape[1]), n_p, tile_of)
    specs.append(pl.BlockSpec(w.shape, lambda *_: (0, 0), pipeline_mode=pl.Buffered(1)))
    specs += (_part_specs((tm, D), n_p, tile_of) if len(xs) == 2
              else [pl.BlockSpec((tm, D), lambda i, *_: (tile_of(i), 0))])
    specs.append(_mod_spec(2, tm, tile_of))
    args = [a for pair in acts for a in pair] + [w, *xs, mods]
    return specs, args, dict(n_in=len(acts), n_x=len(xs), n_p=n_p)


def _proj_value(refs, is_prompt, n_in, n_x, n_p):
    a_refs = refs[:2 * n_in]
    w_ref = refs[2 * n_in]
    x_refs = refs[2 * n_in + 1:2 * n_in + 1 + n_x]
    gate_ref = refs[2 * n_in + 1 + n_x]
    k0 = 0
    acc = None
    for j in range(n_in):
        a = _pick(is_prompt, a_refs[2 * j], a_refs[2 * j + 1])
        kw = a.shape[1]
        part = _bdot(a, w_ref[k0:k0 + kw, :])
        acc = part if acc is None else acc + part
        k0 += kw
    x = _pick(is_prompt, *x_refs) if n_x == 2 else x_refs[0][...]
    return x + gate_ref[...] * acc


def _n_proj_refs(n_in, n_x, n_p):
    return 2 * n_in + 1 + n_x + 1


SWIGLU_ROWS = 512


def _swiglu_accumulate(h_scr, acc_scr, w1_ref, w3_ref, w2_ref, rows, scale=None):
    w1, w3, w2 = w1_ref[...].astype(BF16), w3_ref[...].astype(BF16), w2_ref[...].astype(BF16)
    groups = [slice(r, r + SWIGLU_ROWS) for r in range(0, rows, SWIGLU_ROWS)]
    ups = []
    for sl in groups:
        h = h_scr[sl, :]
        ups.append((jnp.dot(h, w1, preferred_element_type=F32), jnp.dot(h, w3, preferred_element_type=F32)))
    for sl, (a, b) in zip(groups, ups):
        mid = (_silu(a) * b).astype(BF16)
        down = jnp.dot(mid, w2, preferred_element_type=F32)
        acc_scr[sl, :] += down if scale is None else scale * down


def _ffn_kernel(*refs, proj, nc):
    n = _n_proj_refs(**proj)
    g_ref, sh_ref, sc_ref, gate_ref, w1_ref, w3_ref, w2_ref, o_ref, h_scr, b1_scr, b3_scr, b2_scr = refs[n:]
    i = pl.program_id(0)
    tile = jnp.maximum(i - (nc - 1), 0)
    is_prompt = tile < proj["n_p"]
    rows = h_scr.shape[0]

    @pl.when((i == 0) | (i >= nc))
    def _():
        x = _proj_value(refs[:n], is_prompt, **proj)
        o_ref[...] = x
        h_scr[...] = _norm_mod(x, g_ref[...], sh_ref[...], sc_ref[...]).astype(BF16)

    def chunk(c):
        _swiglu_accumulate(h_scr, o_ref, b1_scr.at[c], b3_scr.at[c], b2_scr.at[c], rows, scale=gate_ref[...])

    @pl.when(i < nc)
    def _():
        b1_scr[i] = w1_ref[...].astype(BF16)
        b3_scr[i] = w3_ref[...].astype(BF16)
        b2_scr[i] = w2_ref[...].astype(BF16)
        chunk(i)

    @pl.when(i >= nc)
    def _():
        def body(c, carry):
            chunk(c)
            return carry
        lax.fori_loop(0, nc, body, 0)


def ffn_residual(acts, w_out, x, g, mods, w1, w3, w2, tm=512, tf=256):
    T = sum(a.shape[0] for a in acts[0])
    nc = D_FF // tf
    tile_of = lambda i: jnp.maximum(i - (nc - 1), 0)
    chunk_of = lambda i: jnp.minimum(i, nc - 1)
    p_specs, p_args, proj = _proj_inputs(acts, w_out, x, mods, tm, tile_of)
    return pl.pallas_call(
        functools.partial(_ffn_kernel, proj=proj, nc=nc),
        grid=(T // tm + nc - 1,),
        in_specs=p_specs + [
                  pl.BlockSpec((1, D), lambda i: (0, 0)),
                  _mod_spec(3, tm, tile_of), _mod_spec(4, tm, tile_of), _mod_spec(5, tm, tile_of),
                  pl.BlockSpec((D, tf), lambda i: (0, chunk_of(i))),
                  pl.BlockSpec((D, tf), lambda i: (0, chunk_of(i))),
                  pl.BlockSpec((tf, D), lambda i: (chunk_of(i), 0))],
        out_specs=pl.BlockSpec((tm, D), lambda i: (tile_of(i), 0)),
        out_shape=jax.ShapeDtypeStruct((T, D), F32),
        scratch_shapes=[pltpu.VMEM((tm, D), BF16), pltpu.VMEM((nc, D, tf), BF16),
                        pltpu.VMEM((nc, D, tf), BF16), pltpu.VMEM((nc, tf, D), BF16)],
        compiler_params=_cparams("arbitrary"),
        name="ffn",
    )(*p_args, g, mods, mods, mods, w1, w3, w2)


def _qkv_kernel(x_ref, g_ref, sh_ref, sc_ref, w_ref, q_ref, kv_ref):
    h = _norm_mod(x_ref[...], g_ref[...], sh_ref[...], sc_ref[...]).astype(BF16)
    nq = q_ref.shape[1]
    q_ref[...] = _bdot(h, w_ref[:, :nq]).astype(q_ref.dtype)
    kv_ref[...] = _bdot(h, w_ref[:, nq:])


def qkv_proj(x, g, mods, w_qkv, tm=1024):
    T = x.shape[0]
    nq, nkv = ATT_HEADS * ATT_HD, 2 * ATT_KV * ATT_HD
    return pl.pallas_call(
        _qkv_kernel,
        grid=(T // tm,),
        in_specs=[pl.BlockSpec((tm, D), lambda i: (i, 0)),
                  pl.BlockSpec((1, D), lambda i: (0, 0)),
                  _mod_spec(0, tm), _mod_spec(1, tm),
                  _const_spec(w_qkv, 1)],
        out_specs=[pl.BlockSpec((tm, nq), lambda i: (i, 0)),
                   pl.BlockSpec((tm, nkv), lambda i: (i, 0))],
        out_shape=[jax.ShapeDtypeStruct((T, nq), BF16), jax.ShapeDtypeStruct((T, nkv), F32)],
        compiler_params=_cparams("parallel"),
        name="qkv_proj",
    )(x, g, mods, mods, w_qkv)


def _rope_tables(L):
    half = ATT_HD // 2
    pos_r = (np.arange(L) // GRID_W).astype(np.float32)
    pos_c = (np.arange(L) % GRID_W).astype(np.float32)
    inv = (ROPE_BASE ** (-np.arange(0, half, 2, dtype=np.float32) / half)).astype(np.float32)
    cos = np.zeros((L, ATT_HD), np.float64)
    sin = np.zeros((L, ATT_HD), np.float64)
    for base, pos in ((0, pos_r), (half, pos_c)):
        ang = (pos[:, None] * inv[None, :]).astype(np.float32).astype(np.float64)
        cos[:, base:base + half] = np.concatenate([np.cos(ang), np.cos(ang)], axis=1)
        sin[:, base:base + half] = np.concatenate([-np.sin(ang), np.sin(ang)], axis=1)
    return (np.tile(cos, (1, 4)).astype(np.float32), np.tile(sin, (1, 4)).astype(np.float32))


def _seg_rms(x):
    w = x.shape[1]
    ri = lax.broadcasted_iota(jnp.int32, (w, w), 0) // ATT_HD
    ci = lax.broadcasted_iota(jnp.int32, (w, w), 1) // ATT_HD
    ss = _bdot(x * x, (ri == ci).astype(F32))
    return x * lax.rsqrt(ss * (1.0 / ATT_HD) + EPS)


LOG2E = 1.4426950408889634


def _exp2_bf16(x):
    return jnp.exp2(x.astype(BF16))


def _both_halves(tile, low):
    lane = lax.broadcasted_iota(jnp.int32, tile.shape, 1)
    other = pltpu.roll(tile, ATT_HD, 1)
    return jnp.where((lane < ATT_HD) == low, tile, other)


def _swap16(x):
    w = x.shape[1]
    lane = lax.broadcasted_iota(jnp.int32, x.shape, 1)
    return jnp.where(lane % 32 < 16, pltpu.roll(x, w - 16, 1), pltpu.roll(x, 16, 1))


def _attn_kernel(*refs, latent, tq):
    if latent:
        (q_ref, kv_ref, ck_ref, cv_ref, qg_ref, kg_ref, sink_ref, cosq_ref, sinq_ref, cosk_ref, sink_t_ref,
         o_ref, kk_scr, vt_scr, ckk_scr, cvt_scr) = refs
    else:
        q_ref, kv_ref, qg_ref, kg_ref, sink_ref, o_ref, ko_ref, vo_ref, kk_scr, vt_scr = refs
    L = kv_ref.shape[0]
    gw = ATT_KV * ATT_HD
    pw = 2 * ATT_HD
    qb = pl.program_id(1)

    vrows = vt_scr.shape[2]
    nblk = L // pw

    def vt_aug(tile, low):
        vt = tile.T[0:ATT_HD, :] if low else tile.T[ATT_HD:, :]
        return jnp.concatenate([vt, jnp.ones((vrows - ATT_HD, tile.shape[0]), F32)], axis=0).astype(BF16)

    @pl.when(qb == 0)
    def _():
        kn = _seg_rms(kv_ref[:, :gw]) * kg_ref[...]
        v = kv_ref[:, gw:]
        if latent:
            kn = kn * cosk_ref[...] + _swap16(kn) * sink_t_ref[...]
        else:
            for c in range(ATT_KV):
                ko_ref[c] = kn[:, c * ATT_HD:(c + 1) * ATT_HD]
                vo_ref[c] = v[:, c * ATT_HD:(c + 1) * ATT_HD]
        for c in range(ATT_KV):
            tile, low = slice((c // 2) * pw, (c // 2 + 1) * pw), c % 2 == 0
            kk_scr[c] = _both_halves(kn[:, tile], low).astype(BF16)
            for j in range(nblk):
                vt_scr[c, j] = vt_aug(v[j * pw:(j + 1) * pw, tile], low)
            if latent:
                ck, cv = ck_ref[c], cv_ref[c]
                ckk_scr[c] = jnp.concatenate([ck, ck], axis=1).astype(BF16)
                cvt_scr[c] = vt_aug(jnp.concatenate([cv, cv], axis=1), True)

    if latent:
        span = tq + 2 * WINDOW
        start = pl.multiple_of(jnp.clip(qb * tq - WINDOW, 0, L - span), WINDOW)
        blk0 = start // pw
        s_pos = start + lax.broadcasted_iota(jnp.int32, (span, tq), 0)
        t_pos = qb * tq + lax.broadcasted_iota(jnp.int32, (span, tq), 1)
        win_bias = jnp.where(jnp.abs(t_pos - s_pos) <= WINDOW, 0.0, NEG)
    else:
        span, blk0 = L, 0

    nt = (((1,), (1,)), ((), ()))
    low_q = lax.broadcasted_iota(jnp.int32, (tq, pw), 1) < ATT_HD
    def group_scores(c):
        qc = _seg_rms(q_ref[:, c * gw:(c + 1) * gw].astype(F32)) * qg_ref[...]
        if latent:
            qc = qc * cosq_ref[...] + _swap16(qc) * sinq_ref[...]
            kw = kk_scr[c, pl.ds(start, span), :]
        else:
            kw = kk_scr[c]
        qc = qc * (LOG2E / math.sqrt(ATT_HD))
        scores = []
        for g in range(ATT_KV):
            qt = qc[:, (g // 2) * pw:(g // 2 + 1) * pw]
            qm = jnp.where(low_q if g % 2 == 0 else ~low_q, qt, 0.0).astype(BF16)
            lw = lax.dot_general(kw, qm, nt, preferred_element_type=F32)
            lc = lax.dot_general(ckk_scr[c], qm, nt, preferred_element_type=F32) if latent else None
            scores.append((lw, lc))
        return scores

    def group_outputs(c, scores):
        vw = jnp.concatenate([vt_scr[c, blk0 + j] for j in range(span // pw)], axis=1)
        outs = []
        for g in range(ATT_KV):
            head = c * ATT_KV + g
            sink = sink_ref[:, head:head + 1] * LOG2E
            lw, lc = scores[g]
            if latent:
                lw = lw + win_bias
                mx = jnp.maximum(jnp.maximum(jnp.max(lw, axis=0, keepdims=True),
                                             jnp.max(lc, axis=0, keepdims=True)), sink)
                r = jnp.dot(vw, _exp2_bf16(lw - mx), preferred_element_type=F32) + jnp.dot(
                    cvt_scr[c], _exp2_bf16(lc - mx), preferred_element_type=F32)
            else:
                mx = jnp.maximum(jnp.max(lw, axis=0, keepdims=True), sink)
                r = jnp.dot(vw, _exp2_bf16(lw - mx), preferred_element_type=F32)
            den = r[ATT_HD:ATT_HD + 1, :] + jnp.exp2(sink - mx)
            outs.append(r[0:ATT_HD, :] / den)
        for t in range(2):
            o_ref[:, c * gw + t * pw:c * gw + (t + 1) * pw] = (
                jnp.concatenate(outs[2 * t:2 * t + 2], axis=0).T.astype(o_ref.dtype))

    scores = group_scores(0)
    for c in range(ATT_KV):
        nxt = group_scores(c + 1) if c + 1 < ATT_KV else None
        group_outputs(c, scores)
        scores = nxt


def attention(q, kv, row0, q_g, k_g, sink, B, L, cache=None, tq=256):
    latent = cache is not None
    gw = ATT_KV * ATT_HD
    qg = jnp.tile(q_g, ATT_KV)[None]
    kg = jnp.tile(k_g, ATT_KV)[None]
    nq = L // tq
    const = lambda a: pl.BlockSpec(a.shape, lambda b, i: (0,) * a.ndim)
    in_specs = [pl.BlockSpec((tq, ATT_HEADS * ATT_HD), lambda b, i: (row0 // tq + b * nq + i, 0)),
                pl.BlockSpec((L, 2 * gw), lambda b, i: (row0 // L + b, 0))]
    args = [q, kv]
    vrows = ATT_HD + 16
    scratch = [pltpu.VMEM((ATT_KV, L, 2 * ATT_HD), BF16),
               pltpu.VMEM((ATT_KV, L // (2 * ATT_HD), vrows, 2 * ATT_HD), BF16)]
    out_specs = [pl.BlockSpec((tq, ATT_HEADS * ATT_HD), lambda b, i: (b * nq + i, 0))]
    out_shape = [jax.ShapeDtypeStruct((B * L, ATT_HEADS * ATT_HD), BF16)]
    if latent:
        ck, cv = cache
        P = ck.shape[2]
        cos, sin = (jnp.asarray(t) for t in _rope_tables(L))
        in_specs += [pl.BlockSpec((None, ATT_KV, P, ATT_HD), lambda b, i: (b, 0, 0, 0))] * 2
        args += [ck, cv]
        in_specs += [const(qg), const(kg), pl.BlockSpec((1, ATT_HEADS), lambda b, i: (0, 0)),
                     pl.BlockSpec((tq, gw), lambda b, i: (i, 0)), pl.BlockSpec((tq, gw), lambda b, i: (i, 0)),
                     const(cos), const(sin)]
        args += [qg, kg, sink[None], cos, sin, cos, sin]
        scratch += [pltpu.VMEM((ATT_KV, P, 2 * ATT_HD), BF16), pltpu.VMEM((ATT_KV, vrows, P), BF16)]
    else:
        in_specs += [const(qg), const(kg), pl.BlockSpec((1, ATT_HEADS), lambda b, i: (0, 0))]
        args += [qg, kg, sink[None]]
        cache_spec = pl.BlockSpec((None, ATT_KV, L, ATT_HD), lambda b, i: (b, 0, 0, 0))
        out_specs += [cache_spec, cache_spec]
        out_shape += [jax.ShapeDtypeStruct((B, ATT_KV, L, ATT_HD), F32)] * 2
    outs = pl.pallas_call(
        functools.partial(_attn_kernel, latent=latent, tq=tq),
        grid=(B, nq),
        in_specs=in_specs, out_specs=out_specs, out_shape=out_shape,
        scratch_shapes=scratch,
        compiler_params=_cparams("parallel", "arbitrary"),
        name="attn_latent" if latent else "attn_context",
    )(*args)
    return outs[0] if latent else outs


MOE_TM = 1024
MOE_TOK = 1024
RUN_ALIGN = 16
MOE_LOCAL = 2 * MOE_TOK + N_EXPERTS * RUN_ALIGN
MOE_MAX_TILES = (2 * 16384 + (16384 // MOE_TOK) * N_EXPERTS * (RUN_ALIGN - 1)) // MOE_TM + N_EXPERTS + 1
RUN_SIZES = tuple(RUN_ALIGN << b for b in range(7, -1, -1))
MOE_CHUNKS = 11
MOE_MAX_ITEMS = N_EXPERTS * MOE_CHUNKS + MOE_MAX_TILES - N_EXPERTS
ITEM_FULL, ITEM_DEAD, ITEM_NONE = -1, -2, -3


def _router_kernel(*refs, proj):
    n = _n_proj_refs(**proj)
    g_ref, sh_ref, sc_ref, wr_ref, br_ref, tri_ref, x_ref, lp_ref, wts_ref, runs_ref, cnt_ref = refs[n:]

    @pl.when(pl.program_id(0) == 0)
    def _():
        cnt_ref[...] = jnp.zeros_like(cnt_ref)

    x = _proj_value(refs[:n], pl.program_id(0) < proj["n_p"], **proj)
    x_ref[...] = x
    h = _norm_mod(x, g_ref[...], sh_ref[...], sc_ref[...])
    lg = lax.dot_general(wr_ref[...], h, (((1,), (1,)), ((), ())), precision=HIGHEST,
                         preferred_element_type=F32) + br_ref[...]
    row = lax.broadcasted_iota(jnp.int32, lg.shape, 0)
    m1 = jnp.max(lg, axis=0, keepdims=True)
    i1 = jnp.min(jnp.where(lg == m1, row, N_EXPERTS), axis=0, keepdims=True)
    l2 = jnp.where(row == i1, -jnp.inf, lg)
    m2 = jnp.max(l2, axis=0, keepdims=True)
    i2 = jnp.min(jnp.where(l2 == m2, row, N_EXPERTS), axis=0, keepdims=True)
    e2 = jnp.exp(m2 - m1)
    w1 = 1.0 / (1.0 + e2)
    wts_ref[...] = jnp.concatenate([w1, e2 * w1], axis=0)
    oh1 = (row == i1).astype(F32)
    oh2 = (row == i2).astype(F32)
    cs1 = _bdot(oh1, tri_ref[...])
    cs2 = _bdot(oh2, tri_ref[...])
    tot1 = jnp.sum(oh1, axis=1, keepdims=True)
    run = jnp.ceil((tot1 + jnp.sum(oh2, axis=1, keepdims=True)) * (1.0 / RUN_ALIGN)) * RUN_ALIGN
    run_b = jnp.broadcast_to(run, (N_EXPERTS, 128))
    er = lax.broadcasted_iota(jnp.int32, (N_EXPERTS, N_EXPERTS), 0)
    ec = lax.broadcasted_iota(jnp.int32, (N_EXPERTS, N_EXPERTS), 1)
    start = jnp.dot((ec < er).astype(F32), run_b, precision=HIGHEST, preferred_element_type=F32)
    last = lax.broadcasted_iota(jnp.int32, (N_EXPERTS, 128), 0) == N_EXPERTS - 1
    run_b = jnp.where(last, MOE_LOCAL - start, run_b)
    st = start[:, 0:1]
    p1 = jnp.sum(oh1 * (st + cs1), axis=0, keepdims=True)
    p2 = jnp.sum(oh2 * (st + tot1 + cs2), axis=0, keepdims=True)
    lp_ref[...] = jnp.concatenate([p1, p2], axis=0).astype(jnp.int32)
    lane = lax.broadcasted_iota(jnp.int32, (N_EXPERTS, 128), 1)
    runs_ref[...] = jnp.where(lane == 0, run_b, jnp.where(lane == 1, start, cnt_ref[...]))
    cnt_ref[...] = cnt_ref[...] + run_b


def moe_router(acts, w_out, x, g, mods, w_router, b_router, tm=MOE_TOK):
    T = x.shape[0]
    tri = jnp.asarray(np.triu(np.ones((tm, tm), np.float32), k=1)).astype(BF16)
    tok2 = lambda dt: jax.ShapeDtypeStruct((2, T), dt)
    p_specs, p_args, proj = _proj_inputs(acts, w_out, x, mods, tm)
    return pl.pallas_call(
        functools.partial(_router_kernel, proj=proj),
        grid=(T // tm,),
        in_specs=p_specs + [
                  pl.BlockSpec((1, D), lambda i: (0, 0)),
                  _mod_spec(3, tm), _mod_spec(4, tm),
                  pl.BlockSpec((N_EXPERTS, D), lambda i: (0, 0)),
                  pl.BlockSpec((N_EXPERTS, 1), lambda i: (0, 0)),
                  _const_spec(tri, 1)],
        out_specs=[pl.BlockSpec((tm, D), lambda i: (i, 0)),
                   pl.BlockSpec((2, tm), lambda i: (0, i)),
                   pl.BlockSpec((2, tm), lambda i: (0, i)),
                   pl.BlockSpec((None, N_EXPERTS, 128), lambda i: (i, 0, 0)),
                   pl.BlockSpec((N_EXPERTS, 128), lambda i: (0, 0))],
        out_shape=[jax.ShapeDtypeStruct((T, D), F32), tok2(jnp.int32), tok2(F32),
                   jax.ShapeDtypeStruct((T // tm, N_EXPERTS, 128), F32),
                   jax.ShapeDtypeStruct((N_EXPERTS, 128), F32)],
        compiler_params=_cparams("arbitrary"),
        name="moe_router",
    )(*p_args, g, mods, mods, w_router.T, b_router[:, None], tri)


def moe_layout(runs, totals):
    rows = totals[:, 0].astype(jnp.int32)
    tiles = (rows + MOE_TM - 1) // MOE_TM
    tile_end = jnp.cumsum(tiles)
    group = (tile_end - tiles) * MOE_TM
    run_len = runs[:, :, 0].astype(jnp.int32)
    run_src = runs[:, :, 1].astype(jnp.int32)
    run_dst = group[None, :] + runs[:, :, 2].astype(jnp.int32)
    tail = jnp.stack([group + rows, tiles * MOE_TM - rows]).astype(jnp.int32)
    n_tiles = tile_end[-1]
    t = jnp.arange(MOE_MAX_TILES, dtype=jnp.int32)
    tile_e = jnp.sum(t[:, None] >= tile_end[None, :], axis=1).astype(jnp.int32)
    last_e = jnp.sum((n_tiles - 1) >= tile_end).astype(jnp.int32)
    tile_e = jnp.where(t < n_tiles, tile_e, last_e)
    first = jnp.sum(jnp.where(tile_e[:, None] == jnp.arange(N_EXPERTS), (tile_end - tiles)[None, :], 0), axis=1)
    e_rows = jnp.sum(jnp.where(tile_e[:, None] == jnp.arange(N_EXPERTS), rows[None, :], 0), axis=1)
    tile_rows = jnp.where(t < n_tiles, jnp.clip(e_rows - (t - first) * MOE_TM, 0, MOE_TM), 0).astype(jnp.int32)
    run_tab = jnp.stack([run_len, run_src, run_dst]).reshape(3, -1)
    live = t < n_tiles
    is_first = (t == first) & live
    n_items = jnp.where(is_first, MOE_CHUNKS, 1)
    item_end = jnp.cumsum(n_items)
    j = jnp.arange(MOE_MAX_ITEMS, dtype=jnp.int32)
    it_tile = jnp.minimum(jnp.sum(j[:, None] >= item_end[None, :], axis=1), MOE_MAX_TILES - 1).astype(jnp.int32)
    chunk = j - jnp.take(item_end - n_items, it_tile)
    kind = jnp.where(jnp.take(is_first, it_tile), chunk, jnp.where(jnp.take(live, it_tile), ITEM_FULL, ITEM_DEAD))
    kind = jnp.where(j >= item_end[-1], ITEM_NONE, kind)
    items = jnp.stack([it_tile, kind.astype(jnp.int32)])
    return run_tab, tail, tile_e, n_tiles.astype(jnp.int32).reshape(1), tile_rows, items


def _run_copies(tab_ref, i, local_ref, global_ref, sem, to_global):
    out = []
    for e in range(N_EXPERTS):
        k = i * N_EXPERTS + e
        n, src, dst = tab_ref[0, k], tab_ref[1, k], tab_ref[2, k]
        for size in RUN_SIZES:
            done = (n // (2 * size)) * (2 * size)
            loc = local_ref.at[pl.ds(pl.multiple_of(src + done, RUN_ALIGN), size), :]
            glo = global_ref.at[pl.ds(pl.multiple_of(dst + done, RUN_ALIGN), size), :]
            copy = pltpu.make_async_copy(loc, glo, sem) if to_global else pltpu.make_async_copy(glo, loc, sem)
            out.append(((n & size) != 0, copy))
    return out


def _start(copies, live=True):
    for pred, copy in copies:
        pl.when(pred & live)(copy.start)


def _wait(copies, live=True):
    for pred, copy in copies:
        pl.when(pred & live)(copy.wait)


def _start_then_wait(copies):
    _start(copies)
    _wait(copies)


def _dispatch_kernel(tab_ref, tail_ref, nt_ref, lp_ref, x_ref, g_ref, sh_ref, sc_ref, xs_ref, hs_scr, z_scr, sem):
    i = pl.program_id(0)
    tm = x_ref.shape[0]
    buf = i % 2
    h = _norm_mod(x_ref[...], g_ref[...], sh_ref[...], sc_ref[...]).astype(BF16)
    slot = lax.broadcasted_iota(jnp.int32, (MOE_LOCAL, tm), 0)
    perm = jnp.where((slot == lp_ref[0:1, :]) | (slot == lp_ref[1:2, :]), 1.0, 0.0).astype(BF16)
    hs_scr[buf] = jnp.dot(perm, h, preferred_element_type=F32).astype(BF16)
    copies = _run_copies(tab_ref, i, hs_scr.at[buf], xs_ref, sem.at[buf], to_global=True)
    _start(copies)
    _wait(_run_copies(tab_ref, jnp.maximum(i - 1, 0), hs_scr.at[1 - buf], xs_ref, sem.at[1 - buf], to_global=True),
          live=i > 0)

    @pl.when(i == 0)
    def _():
        z_scr[...] = jnp.zeros_like(z_scr)
        zrows = z_scr.shape[0]

        def zero_tile(t, carry):
            for part in range(MOE_TM // zrows):
                dst = xs_ref.at[pl.ds(pl.multiple_of(t * MOE_TM + part * zrows, zrows), zrows), :]
                copy = pltpu.make_async_copy(z_scr, dst, sem.at[2])
                copy.start()
                copy.wait()
            return carry

        lax.fori_loop(nt_ref[0], MOE_MAX_TILES, zero_tile, 0)
        tails = []
        for e in range(N_EXPERTS):
            start, n = tail_ref[0, e], tail_ref[1, e]
            for size in RUN_SIZES:
                if size >= MOE_TM:
                    continue
                done = (n // (2 * size)) * (2 * size)
                dst = xs_ref.at[pl.ds(pl.multiple_of(start + done, RUN_ALIGN), size), :]
                tails.append(((n & size) != 0, pltpu.make_async_copy(z_scr.at[pl.ds(0, size), :], dst, sem.at[2])))
        _start_then_wait(tails)

    _wait(copies, live=i == pl.num_programs(0) - 1)


def moe_dispatch(x, g, mods, lp, run_tab, tail, n_tiles, tm=MOE_TOK):
    T = x.shape[0]
    n_rows = MOE_MAX_TILES * MOE_TM
    return pl.pallas_call(
        _dispatch_kernel,
        grid_spec=pltpu.PrefetchScalarGridSpec(
            num_scalar_prefetch=3,
            grid=(T // tm,),
            in_specs=[pl.BlockSpec((2, tm), lambda i, *_: (0, i)),
                      pl.BlockSpec((tm, D), lambda i, *_: (i, 0)),
                      pl.BlockSpec((1, D), lambda i, *_: (0, 0)),
                      _mod_spec(3, tm), _mod_spec(4, tm)],
            out_specs=pl.BlockSpec(memory_space=pl.ANY),
            scratch_shapes=[pltpu.VMEM((2, MOE_LOCAL, D), BF16), pltpu.VMEM((MOE_TM // 2, D), BF16),
                            pltpu.SemaphoreType.DMA((3,))]),
        out_shape=jax.ShapeDtypeStruct((n_rows, D), BF16),
        compiler_params=_cparams("arbitrary"),
        name="moe_dispatch",
    )(run_tab, tail, n_tiles, lp, x, g, mods, mods)


def _moe_group_kernel(it_ref, te_ref, nt_ref, tr_ref, x_ref, w1_ref, w3_ref, w2_ref, o_ref, acc_scr, b1_scr, b3_scr,
                      b2_scr):
    j = pl.program_id(0)
    kind = it_ref[1, j]
    rows = tr_ref[it_ref[0, j]]
    half = MOE_TM // 2

    def chunk(c):
        @pl.when(rows > half)
        def _():
            _swiglu_accumulate(x_ref, acc_scr, b1_scr.at[c], b3_scr.at[c], b2_scr.at[c], MOE_TM)

        @pl.when(rows <= half)
        def _():
            _swiglu_accumulate(x_ref, acc_scr, b1_scr.at[c], b3_scr.at[c], b2_scr.at[c], half)

    @pl.when((kind == 0) | (kind == ITEM_FULL))
    def _():
        acc_scr[...] = jnp.zeros_like(acc_scr)

    @pl.when(kind >= 0)
    def _():
        b1_scr[kind] = w1_ref[...].astype(BF16)
        b3_scr[kind] = w3_ref[...].astype(BF16)
        b2_scr[kind] = w2_ref[...].astype(BF16)
        chunk(kind)

    @pl.when(kind == ITEM_FULL)
    def _():
        def body(c, carry):
            chunk(c)
            return carry
        lax.fori_loop(0, MOE_CHUNKS, body, 0)

    @pl.when((kind == MOE_CHUNKS - 1) | (kind == ITEM_FULL))
    def _():
        o_ref[...] = acc_scr[...].astype(o_ref.dtype)

    @pl.when(kind == ITEM_DEAD)
    def _():
        o_ref[...] = jnp.zeros_like(o_ref)


def moe_grouped_swiglu(xs, items, tile_e, n_tiles, tile_rows, w1, w3, w2):
    tf = D_FF // MOE_CHUNKS
    tile = lambda j, it, te, nt, tr: it[0, j]
    row_in = lambda j, it, te, nt, tr: (jnp.minimum(tile(j, it, te, nt, tr), jnp.maximum(nt[0] - 1, 0)), 0)
    wchunk = lambda j, it: jnp.where(it[1, j] >= 0, it[1, j], MOE_CHUNKS - 1)
    wcol = lambda j, it, te, nt, tr: (te[it[0, j]], 0, wchunk(j, it))
    wrow = lambda j, it, te, nt, tr: (te[it[0, j]], wchunk(j, it), 0)
    return pl.pallas_call(
        _moe_group_kernel,
        grid_spec=pltpu.PrefetchScalarGridSpec(
            num_scalar_prefetch=4,
            grid=(MOE_MAX_ITEMS,),
            in_specs=[pl.BlockSpec((MOE_TM, D), row_in),
                      pl.BlockSpec((None, D, tf), wcol),
                      pl.BlockSpec((None, D, tf), wcol),
                      pl.BlockSpec((None, tf, D), wrow)],
            out_specs=pl.BlockSpec((MOE_TM, D), lambda j, it, te, nt, tr: (it[0, j], 0)),
            scratch_shapes=[pltpu.VMEM((MOE_TM, D), F32), pltpu.VMEM((MOE_CHUNKS, D, tf), BF16),
                            pltpu.VMEM((MOE_CHUNKS, D, tf), BF16), pltpu.VMEM((MOE_CHUNKS, tf, D), BF16)]),
        out_shape=jax.ShapeDtypeStruct(xs.shape, BF16),
        compiler_params=_cparams("arbitrary"),
        name="moe_grouped",
    )(items, tile_e, n_tiles, tile_rows, xs, w1, w3, w2)


def _combine_kernel(tab_ref, lp_ref, wt_ref, x_ref, gate_ref, ys_ref, op_ref, os_ref, yl_scr, sem, *, n_p):
    i = pl.program_id(0)
    tm = x_ref.shape[0]
    buf = i % 2
    last = pl.num_programs(0) - 1
    gather = lambda t, b: _run_copies(tab_ref, t, yl_scr.at[b], ys_ref, sem.at[b], to_global=False)
    _start(gather(i, buf), live=i == 0)
    _start(gather(jnp.minimum(i + 1, last), 1 - buf), live=i < last)
    slot = lax.broadcasted_iota(jnp.int32, (tm, MOE_LOCAL), 1)
    mix = (jnp.where(slot == lp_ref[:, 0:1], wt_ref[:, 0:1], 0.0)
           + jnp.where(slot == lp_ref[:, 1:2], wt_ref[:, 1:2], 0.0)).astype(BF16)
    _wait(gather(i, buf))
    moe = jnp.dot(mix, yl_scr[buf], preferred_element_type=F32)
    out = x_ref[...] + gate_ref[...] * moe

    @pl.when(pl.program_id(0) < n_p)
    def _():
        op_ref[...] = out

    @pl.when(pl.program_id(0) >= n_p)
    def _():
        os_ref[...] = out


def moe_combine(x, mods, lp, wts, run_tab, ys, t_prompt, tm=MOE_TOK):
    T = x.shape[0]
    n_p = t_prompt // tm
    return pl.pallas_call(
        functools.partial(_combine_kernel, n_p=n_p),
        grid_spec=pltpu.PrefetchScalarGridSpec(
            num_scalar_prefetch=1,
            grid=(T // tm,),
            in_specs=[pl.BlockSpec((tm, 2), lambda i, *_: (i, 0)),
                      pl.BlockSpec((tm, 2), lambda i, *_: (i, 0)),
                      pl.BlockSpec((tm, D), lambda i, *_: (i, 0)),
                      _mod_spec(5, tm),
                      pl.BlockSpec(memory_space=pl.ANY)],
            out_specs=_part_specs((tm, D), n_p),
            scratch_shapes=[pltpu.VMEM((2, MOE_LOCAL, D), BF16), pltpu.SemaphoreType.DMA((2,))]),
        out_shape=[jax.ShapeDtypeStruct((t_prompt, D), F32), jax.ShapeDtypeStruct((T - t_prompt, D), F32)],
        compiler_params=_cparams("arbitrary"),
        name="moe_combine",
    )(run_tab, lp.T, wts.T, x, mods, ys)


def kernel(x_prompt, x_sample, state_C, state_n, state_m, cache_k, cache_v, c, c_ctx, norm1_g, norm2_g, w_ada, b_ada, ev_w_in, ev_conv, hy_w1, hy_b1, hy_w2, hy_b2, hy_w3, hy_freq, hy_d, ml_b_gate, ml_norm_g, ev_w_out, ff_w1, ff_w3, ff_w2, at_w_qkv, at_q_g, at_k_g, at_sink, at_w_out, moe_w_router, moe_b_router, moe_w1, moe_w3, moe_w2):
    BP, LP, _ = x_prompt.shape
    BS, LS, _ = x_sample.shape
    TP = BP * LP
    assert TP % GROUP == 0 and TP // GROUP == N_PROMPT_GROUPS and LS == GROUP and BS == 8

    xp, xs = x_prompt.reshape(TP, D), x_sample.reshape(BS * LS, D)
    cond = jnp.concatenate([c_ctx[None], c, jnp.zeros((16 - 1 - BS, D), F32)], axis=0)
    mods = adaln_table(cond, w_ada, b_ada)

    u, gates = even_in_proj(xp, xs, norm1_g[0:1], mods[0], ev_w_in[0], ml_b_gate[0].reshape(1, N_GATES))
    hy = []
    for seq0, B, L, nb in ((0, BP, LP, 4), (TP // LS, BS, LS, 1)):
        fwd, inv = (jnp.asarray(t).astype(BF16) for t in _dft_tables(L))
        ka, kb = hyena_filter_spectra(L, hy_w1[0], hy_b1[0], hy_w2[0], hy_b2[0], hy_w3[0], hy_freq[0], fwd)
        hy.append(hyena_mix(u, seq0, B, L, ev_conv[0], hy_d[0], fwd, inv, ka, kb, nb))
    ml_p, new_C, new_n, new_m = mlstm_mix(u, gates, 0, BP, LP, ml_norm_g[0], want_state=True)
    ml_s = mlstm_mix(u, gates, TP // LS, BS, LS, ml_norm_g[0],
                     state=(state_C[:, 0], state_n[:, 0], state_m[:, 0]))
    x = ffn_residual([hy, (ml_p, ml_s)], ev_w_out[0], (xp, xs), norm2_g[0:1], mods[0], ff_w1[0], ff_w3[0], ff_w2[0])

    q, kv = qkv_proj(x, norm1_g[1:2], mods[1], at_w_qkv[0])
    o_p, new_k, new_v = attention(q, kv, 0, at_q_g[0], at_k_g[0], at_sink[0], BP, LP)
    o_s = attention(q, kv, TP, at_q_g[0], at_k_g[0], at_sink[0], BS, LS, cache=(cache_k[:, 0], cache_v[:, 0]))
    x, lp, wts, runs, totals = moe_router([(o_p, o_s)], at_w_out[0], x, norm2_g[1:2], mods[1],
                                          moe_w_router[0], moe_b_router[0])
    run_tab, tail, tile_e, n_tiles, tile_rows, items = moe_layout(runs, totals)
    xsort = moe_dispatch(x, norm2_g[1:2], mods[1], lp, run_tab, tail, n_tiles)
    ysort = moe_grouped_swiglu(xsort, items, tile_e, n_tiles, tile_rows, moe_w1[0], moe_w3[0], moe_w2[0])
    yp, ys = moe_combine(x, mods[1], lp, wts, run_tab, ysort, TP)

    return (yp.reshape(BP, LP, D), ys.reshape(BS, LS, D),
            new_C[:, None], new_n[:, None], new_m[:, None], new_k[:, None], new_v[:, None])
```

```python
import functools
import math

import numpy as np
import jax
import jax.numpy as jnp
from jax import lax
from jax.experimental import pallas as pl
from jax.experimental.pallas import tpu as pltpu

F32 = jnp.float32
BF16 = jnp.bfloat16
HIGHEST = lax.Precision.HIGHEST

D = 1024
GROUP = 1024
N_PROMPT_GROUPS = 8
HY_W = 512
ML_HEADS = 4
ML_HD = 128
ML_CHUNK = 256
EVEN_MAIN = 3 * HY_W + 4 * 512
N_GATES = 16
ATT_HD = 64
ATT_HEADS = 16
ATT_KV = 4
WINDOW = 128
GRID_W = 64
ROPE_BASE = 10000.0
D_FF = 2816
N_EXPERTS = 8
EPS = 1e-6
NEG = -1e30
VMEM_LIMIT = 56 * 1024 * 1024


def _cparams(*sem, flags=None):
    return pltpu.CompilerParams(dimension_semantics=sem, vmem_limit_bytes=VMEM_LIMIT, flags=flags)


def _mod_row(i, tm):
    return jnp.maximum(i * tm // GROUP - (N_PROMPT_GROUPS - 1), 0)


def _silu(x):
    return x * jax.nn.sigmoid(x)


def _bdot(a, b):
    return jnp.dot(a.astype(BF16), b.astype(BF16), preferred_element_type=F32)


def _norm_mod(x, g, sh, sc):
    y = x * lax.rsqrt(jnp.mean(x * x, axis=-1, keepdims=True) + EPS) * g
    return y * (1.0 + sc) + sh


def _adaln_kernel(c_ref, w_ref, b_ref, o_ref):
    s = _silu(c_ref[...])
    o_ref[...] = jnp.dot(s, w_ref[...], precision=HIGHEST, preferred_element_type=F32) + b_ref[...]


def adaln_table(cond, w_ada, b_ada):
    depth = w_ada.shape[0]
    tn = 1536
    out = pl.pallas_call(
        _adaln_kernel,
        grid=(depth, 6 * D // tn),
        in_specs=[pl.BlockSpec((16, D), lambda l, j: (0, 0)),
                  pl.BlockSpec((None, D, tn), lambda l, j: (l, 0, j)),
                  pl.BlockSpec((None, 1, tn), lambda l, j: (l, 0, j))],
        out_specs=pl.BlockSpec((None, 16, tn), lambda l, j: (l, 0, j)),
        out_shape=jax.ShapeDtypeStruct((depth, 16, 6 * D), F32),
        compiler_params=_cparams("parallel", "parallel"),
        name="adaln",
    )(cond, w_ada, b_ada.reshape(depth, 1, 6 * D))
    return out.reshape(depth, 16, 1, 6 * D)


def _same_tile(i):
    return i


def _part_specs(block, n_p, tile_of=_same_tile):
    return [pl.BlockSpec(block, lambda i, *_: (jnp.minimum(tile_of(i), n_p - 1), 0)),
            pl.BlockSpec(block, lambda i, *_: (jnp.maximum(tile_of(i) - n_p, 0), 0))]


def _pick(is_prompt, p_ref, s_ref):
    return jnp.where(is_prompt, p_ref[...], s_ref[...])


def _mod_spec(k, tm, tile_of=_same_tile):
    return pl.BlockSpec((None, 1, D), lambda i, *_: (_mod_row(tile_of(i), tm), 0, k))


def _log_sigmoid(x):
    return jnp.minimum(x, 0.0) - jnp.log(1.0 + jnp.exp(-jnp.abs(x)))


def _split3(x):
    hi = x.astype(BF16)
    r = x - hi.astype(F32)
    mid = r.astype(BF16)
    return hi, mid, (r - mid.astype(F32)).astype(BF16)


def _even_in_kernel(xp_ref, xs_ref, g_ref, sh_ref, sc_ref, w_ref, bg_ref, lo_ref, up_ref, u_ref, gate_ref, *, n_p, tn):
    is_prompt = pl.program_id(0) < n_p
    h = _norm_mod(_pick(is_prompt, xp_ref, xs_ref), g_ref[...], sh_ref[...], sc_ref[...]).astype(BF16)
    for j in range(EVEN_MAIN // tn):
        u_ref[:, j * tn:(j + 1) * tn] = _bdot(h, w_ref[:, j * tn:(j + 1) * tn]).astype(u_ref.dtype)
    gates = _bdot(h, w_ref[:, EVEN_MAIN:]) + bg_ref[...]
    lf = _log_sigmoid(gates)
    col = lax.broadcasted_iota(jnp.int32, (1, N_GATES), 1)
    is_forget = (col // ML_HEADS) % 2 == 1
    is_rev = col >= N_GATES // 2
    for ch in range(h.shape[0] // ML_CHUNK):
        sl = slice(ch * ML_CHUNK, (ch + 1) * ML_CHUNK)
        parts = _split3(lf[sl])
        cf = sum(jnp.dot(lo_ref[...], p, preferred_element_type=F32) for p in parts)
        cr = sum(jnp.dot(up_ref[...], p, preferred_element_type=F32) for p in parts)
        gate_ref[sl, :] = jnp.where(is_forget, jnp.where(is_rev, cr, cf), gates[sl])


def even_in_proj(xp, xs, g, mods, w_in, b_gate, tm=1024, tn=512):
    T = xp.shape[0] + xs.shape[0]
    tri = np.tril(np.ones((ML_CHUNK, ML_CHUNK), np.float32))
    lo, up = jnp.asarray(tri).astype(BF16), jnp.asarray(tri.T).astype(BF16)
    return pl.pallas_call(
        functools.partial(_even_in_kernel, n_p=xp.shape[0] // tm, tn=tn),
        grid=(T // tm,),
        in_specs=_part_specs((tm, D), xp.shape[0] // tm) + [
                  pl.BlockSpec((1, D), lambda i: (0, 0)),
                  _mod_spec(0, tm), _mod_spec(1, tm),
                  _const_spec(w_in, 1),
                  pl.BlockSpec((1, N_GATES), lambda i: (0, 0)),
                  _const_spec(lo, 1), _const_spec(up, 1)],
        out_specs=[pl.BlockSpec((tm, EVEN_MAIN), lambda i: (i, 0)),
                   pl.BlockSpec((tm, N_GATES), lambda i: (i, 0))],
        out_shape=[jax.ShapeDtypeStruct((T, EVEN_MAIN), BF16),
                   jax.ShapeDtypeStruct((T, N_GATES), F32)],
        compiler_params=_cparams("parallel"),
        name="even_in_proj",
    )(xp, xs, g, mods, mods, w_in, b_gate, lo, up)


def _dft_tables(L):
    n = 2 * L
    f = np.arange(L, dtype=np.int64)[:, None]
    s = np.arange(L, dtype=np.int64)[None, :]
    ang = 2.0 * np.pi * ((f * s) % n).astype(np.float64) / n
    fwd = np.concatenate([np.cos(ang), -np.sin(ang)], axis=0)
    fwd[L, :] = np.where(np.arange(L) % 2 == 0, 1.0, -1.0)
    t = np.arange(L, dtype=np.int64)[:, None]
    ff = np.arange(L, dtype=np.int64)[None, :]
    ang = 2.0 * np.pi * ((t * ff) % n).astype(np.float64) / n
    inv_re = 2.0 * np.cos(ang) / n
    inv_re[:, 0] = 1.0 / n
    inv_im = -2.0 * np.sin(ang) / n
    inv_im[:, 0] = np.where(np.arange(L) % 2 == 0, 1.0, -1.0) / n
    inv = np.concatenate([inv_re, inv_im], axis=1)
    return fwd.astype(np.float32), inv.astype(np.float32)


def _filter_tables(L):
    t = np.linspace(0.0, 1.0, L, dtype=np.float32).astype(np.float64)[:, None]
    w = 2.0 * math.pi * np.arange(L, dtype=np.float64)[:, None] / L
    bands = np.linspace(1e-4, 16 - 1, 16, dtype=np.float32).astype(np.float64)[None, :]
    z = np.concatenate([t, np.cos(bands * w), -np.sin(bands * w)], axis=-1)
    zp = np.zeros((L, 128), np.float64)
    zp[:, :z.shape[1]] = z
    max_decay = math.log(1e-2) / 0.3
    min_decay = math.log(1e-2) / 1.5
    deltas = np.linspace(min_decay, max_decay, HY_W, dtype=np.float32).astype(np.float64)
    decay = np.exp(-t * np.abs(deltas))
    return zp.astype(np.float32), decay.astype(np.float32)


def _hy_filter_kernel(z_ref, dec_ref, w1_ref, b1_ref, w2_ref, b2_ref, w3_ref, fr_ref, fwd_ref,
                      ka_ref, kb_ref):
    L = z_ref.shape[0]
    hdot = functools.partial(jnp.dot, precision=HIGHEST, preferred_element_type=F32)
    h = jnp.sin(fr_ref[0:1, :] * (hdot(z_ref[...], w1_ref[...]) + b1_ref[...]))
    h = jnp.sin(fr_ref[1:2, :] * (hdot(h, w2_ref[...]) + b2_ref[...]))
    h = hdot(h, w3_ref[...])
    row0 = lax.broadcasted_iota(jnp.int32, (L, 1), 0) == 0
    h0 = h[:, :HY_W] * dec_ref[...]
    h1 = h[:, HY_W:] * dec_ref[...]
    l1 = jnp.sum(jnp.abs(h0), axis=0, keepdims=True) + jnp.sum(jnp.abs(h1), axis=0, keepdims=True)
    inv = 1.0 / l1
    h0 = h0 * inv
    h1 = jnp.where(row0, 0.0, h1 * inv)
    f0 = _bdot(fwd_ref[...], h0)
    f1 = _bdot(fwd_ref[...], h1)
    ka_ref[...] = f0[:L] + f1[:L]
    kb_ref[...] = jnp.where(row0, f0[L:] + f1[L:], f0[L:] - f1[L:])


def _const_spec(a, n_grid):
    return pl.BlockSpec(a.shape, lambda *_: (0,) * a.ndim, pipeline_mode=pl.Buffered(1))


def hyena_filter_spectra(L, w1, b1, w2, b2, w3, freq, fwd):
    z, dec = _filter_tables(L)
    pad2 = lambda a, r, c: jnp.pad(a, ((0, r - a.shape[0]), (0, c - a.shape[1])))
    args = (jnp.asarray(z), jnp.asarray(dec), pad2(w1, 128, 128), pad2(b1[None], 1, 128),
            pad2(w2, 128, 128), pad2(b2[None], 1, 128), pad2(w3, 128, 4 * HY_W), pad2(freq, 2, 128), fwd)
    in_specs = [_const_spec(a, 1) for a in args]
    in_specs[6] = pl.BlockSpec((128, 2 * HY_W), lambda o: (0, o))
    shp = jax.ShapeDtypeStruct((2, L, HY_W), F32)
    out_spec = pl.BlockSpec((None, L, HY_W), lambda o: (o, 0, 0))
    return pl.pallas_call(
        _hy_filter_kernel,
        grid=(2,),
        in_specs=in_specs,
        out_specs=[out_spec, out_spec],
        out_shape=[shp, shp],
        compiler_params=_cparams("arbitrary"),
        name=f"hyena_filter_{L}",
    )(*args)


def _hyena_kernel(u_ref, cw_ref, d_ref, fwd_ref, inv_ref, ka_ref, kb_ref, o_ref):
    nb, L = u_ref.shape[0], u_ref.shape[1]
    row = lax.broadcasted_iota(jnp.int32, (L, 1), 0)
    first, last = row == 0, row == L - 1
    fwd = fwd_ref[...].astype(BF16)
    inv = inv_ref[...].astype(BF16)

    def long_conv(z, o):
        zf = jnp.dot(fwd, z.astype(BF16), preferred_element_type=F32)
        a, b = zf[:L], zf[L:]
        ka, kb = ka_ref[o], kb_ref[o]
        yr = a * ka - jnp.where(first, 0.0, b * kb)
        yi = jnp.where(first, b * kb, a * kb + b * ka)
        return (jnp.dot(inv[:, :L], yr.astype(BF16), preferred_element_type=F32)
                + jnp.dot(inv[:, L:], yi.astype(BF16), preferred_element_type=F32))

    for bi in range(nb):
        u = u_ref[bi].astype(F32)
        prev = jnp.where(first, 0.0, pltpu.roll(u, 1, 0))
        nxt = jnp.where(last, 0.0, pltpu.roll(u, L - 1, 0))
        u = prev * cw_ref[0:1, :] + u * cw_ref[1:2, :] + nxt * cw_ref[2:3, :]
        v, x1, x2 = u[:, :HY_W], u[:, HY_W:2 * HY_W], u[:, 2 * HY_W:]
        z = x1 * (long_conv(v, 0) + d_ref[0:1, :] * v)
        z = x2 * (long_conv(z, 1) + d_ref[1:2, :] * z)
        o_ref[bi] = z.astype(o_ref.dtype)


def hyena_mix(u, seq0, B, L, conv_w, d_skip, fwd, inv, ka, kb, nb):
    u3 = u.reshape(-1, L, EVEN_MAIN)
    full = lambda a: _const_spec(a, 1)
    out = pl.pallas_call(
        _hyena_kernel,
        grid=(B // nb,),
        in_specs=[pl.BlockSpec((nb, L, 3 * HY_W), lambda b: (b + seq0 // nb, 0, 0)),
                  full(conv_w), full(d_skip), full(fwd), full(inv), full(ka), full(kb)],
        out_specs=pl.BlockSpec((nb, L, HY_W), lambda b: (b, 0, 0)),
        out_shape=jax.ShapeDtypeStruct((B, L, HY_W), BF16),
        compiler_params=_cparams("parallel"),
        name=f"hyena_{L}",
    )(u3, conv_w, d_skip, fwd, inv, ka, kb)
    return out.reshape(B * L, HY_W)


def _mlstm_kernel(*refs, has_state, want_state):
    q_ref, k_ref, v_ref, o_ref, gc_ref, gr_ref, ng_ref = refs[:7]
    refs = refs[7:]
    if has_state:
        c0t_ref, n0b_ref, m0_ref = refs[:3]
        refs = refs[3:]
    y_ref = refs[0]
    if want_state:
        c_out, n_out, m_out = refs[1:4]
    L, d = q_ref.shape[0], ML_HD
    T = min(ML_CHUNK, L)
    nc = L // T
    scale = 1.0 / math.sqrt(d)
    nt = (((1,), (1,)), ((), ()))
    si = lax.broadcasted_iota(jnp.int32, (T, T), 0)
    ti = lax.broadcasted_iota(jnp.int32, (T, T), 1)
    allowed = (si <= ti, si >= ti)
    chains = [(dr, h) for dr in range(2) for h in range(ML_HEADS)]
    gcol = lambda dr, gi, h: dr * 2 * ML_HEADS + gi * ML_HEADS + h

    caug_t, m = {}, {}
    for ch in chains:
        dr, h = ch
        if has_state:
            caug_t[ch] = jnp.concatenate([c0t_ref[dr, h], n0b_ref[dr, h]], axis=0)
            m[ch] = m0_ref[dr, h:h + 1, 0:1]
        else:
            caug_t[ch], m[ch] = jnp.zeros((2 * d, d), F32), jnp.zeros((1, 1), F32)

    chunk_cache = {}

    def chunk_data(h, j):
        if (h, j) not in chunk_cache:
            sl, hl = slice(j * T, (j + 1) * T), slice(h * d, (h + 1) * d)
            q = q_ref[sl, hl]
            ks = (k_ref[sl, hl].astype(F32) * scale).astype(BF16)
            v_t = v_ref[sl, hl].astype(F32).T
            vaug_t = jnp.concatenate([v_t, jnp.ones((d, T), F32)], axis=0).astype(BF16)
            s_raw = lax.dot_general(ks, q, nt, preferred_element_type=F32)
            chunk_cache[(h, j)] = (q, ks, v_t, vaug_t, s_raw)
        return chunk_cache[(h, j)]

    h_sum = {}
    for it in range(nc):
        step = {ch: (it if ch[0] == 0 else nc - 1 - it) for ch in chains}
        data = {ch: chunk_data(ch[1], step[ch]) for ch in chains}
        inter_t = {ch: lax.dot_general(caug_t[ch].astype(BF16), data[ch][0], nt, preferred_element_type=F32)
                   for ch in chains}
        gate = {}
        for ch in chains:
            dr, h = ch
            sl = slice(step[ch] * T, (step[ch] + 1) * T)
            li_r, b_r = gr_ref[gcol(dr, 0, h):gcol(dr, 0, h) + 1, sl], gr_ref[gcol(dr, 1, h):gcol(dr, 1, h) + 1, sl]
            src = gc_ref[sl, gcol(dr, 0, h):gcol(dr, 0, h) + 1] - gc_ref[sl, gcol(dr, 1, h):gcol(dr, 1, h) + 1]
            dm = jnp.where(allowed[dr], src + b_r, NEG)
            inter = b_r + m[ch]
            m_t = jnp.maximum(inter, jnp.max(dm, axis=0, keepdims=True))
            b_end = b_r[:, T - 1:T] if dr == 0 else b_r[:, 0:1]
            g_r = b_end - b_r + li_r
            m_new = jnp.maximum(b_end + m[ch], jnp.max(g_r, axis=1, keepdims=True))
            gate[ch] = (jnp.exp(dm - m_t), jnp.exp(inter - m_t), jnp.exp(-m_t), jnp.exp(g_r - m_new),
                        jnp.exp(b_end + m[ch] - m_new), m_new)
        for ch in chains:
            q, ks, v_t, vaug_t, s_raw = data[ch]
            w_intra, w_inter, floor, w_tok, decay, m_new = gate[ch]
            acc = jnp.dot(vaug_t, (s_raw * w_intra).astype(BF16), preferred_element_type=F32) + w_inter * inter_t[ch]
            h_t = acc[:d] / jnp.maximum(jnp.abs(acc[d:]), floor)
            key = (ch[1], step[ch])
            h_sum[key] = h_t if key not in h_sum else h_sum[key] + h_t
            vw_t = jnp.concatenate([v_t * w_tok, jnp.broadcast_to(w_tok, (d, T))], axis=0).astype(BF16)
            caug_t[ch] = decay * caug_t[ch] + jnp.dot(vw_t, ks, preferred_element_type=F32)
            m[ch] = m_new

    if want_state:
        for ch in chains:
            dr, h = ch
            c_out[dr, h] = caug_t[ch][:d].T
            n_out[dr, h:h + 1, :] = caug_t[ch][d:d + 1, :]
            m_out[dr, h:h + 1, :] = jnp.broadcast_to(m[ch], (1, d))
    for h in range(ML_HEADS):
        for j in range(nc):
            sl, hl = slice(j * T, (j + 1) * T), slice(h * d, (h + 1) * d)
            hv = h_sum[(h, j)].T
            y = hv * lax.rsqrt(jnp.mean(hv * hv, axis=-1, keepdims=True) + EPS) * ng_ref[:, hl]
            y_ref[sl, hl] = (y * jax.nn.sigmoid(o_ref[sl, hl].astype(F32))).astype(y_ref.dtype)


def mlstm_mix(u, gates, seq0, B, L, norm_g, state=None, want_state=False):
    u3 = u.reshape(-1, L, EVEN_MAIN)
    gc = gates.reshape(-1, L, N_GATES)[seq0:seq0 + B]
    gr = gc.transpose(0, 2, 1)
    width = ML_HEADS * ML_HD
    col = lambda i: pl.BlockSpec((None, L, width), lambda b: (b + seq0, 0, (3 * HY_W + i * width) // width))
    in_specs = [col(0), col(1), col(2), col(3),
                pl.BlockSpec((None, L, N_GATES), lambda b: (b, 0, 0)),
                pl.BlockSpec((None, N_GATES, L), lambda b: (b, 0, 0)),
                pl.BlockSpec((1, width), lambda b: (0, 0))]
    args = [u3, u3, u3, u3, gc, gr, norm_g.reshape(1, width)]
    sspec = pl.BlockSpec((None, 2, ML_HEADS, ML_HD, ML_HD), lambda b: (b, 0, 0, 0, 0))
    vspec = pl.BlockSpec((None, 2, ML_HEADS, ML_HD), lambda b: (b, 0, 0, 0))
    if state is not None:
        C0, n0, m0 = state
        in_specs += [sspec, sspec, vspec]
        args += [C0.swapaxes(-1, -2), jnp.broadcast_to(n0[..., None, :], C0.shape),
                 jnp.broadcast_to(m0[..., None], n0.shape)]
    out_specs = [pl.BlockSpec((None, L, width), lambda b: (b, 0, 0))]
    out_shape = [jax.ShapeDtypeStruct((B, L, width), BF16)]
    if want_state:
        out_specs += [sspec, vspec, vspec]
        out_shape += [jax.ShapeDtypeStruct((B, 2, ML_HEADS, ML_HD, ML_HD), F32),
                      jax.ShapeDtypeStruct((B, 2, ML_HEADS, ML_HD), F32),
                      jax.ShapeDtypeStruct((B, 2, ML_HEADS, ML_HD), F32)]
    outs = pl.pallas_call(
        functools.partial(_mlstm_kernel, has_state=state is not None, want_state=want_state),
        grid=(B,),
        in_specs=in_specs, out_specs=out_specs, out_shape=out_shape,
        compiler_params=_cparams("parallel"),
        name=f"mlstm_{L}",
    )(*args)
    y = outs[0].reshape(B * L, width)
    if not want_state:
        return y
    _, C, n, m = outs
    return y, C, n, m[..., 0]


def _proj_inputs(acts, w, x, mods, tm, tile_of=_same_tile):
    xs = tuple(x) if isinstance(x, (tuple, list)) else (x,)
    n_p = acts[0][0].shape[0] // tm
    specs = []
    for pair in acts:
        specs += _part_specs((tm, pair[0].shape[1]), n_p, tile_of)
    specs.append(pl.BlockSpec(w.shape, lambda *_: (0, 0), pipeline_mode=pl.Buffered(1)))
    specs += (_part_specs((tm, D), n_p, tile_of) if len(xs) == 2
              else [pl.BlockSpec((tm, D), lambda i, *_: (tile_of(i), 0))])
    specs.append(_mod_spec(2, tm, tile_of))
    args = [a for pair in acts for a in pair] + [w, *xs, mods]
    return specs, args, dict(n_in=len(acts), n_x=len(xs), n_p=n_p)


def _proj_value(refs, is_prompt, n_in, n_x, n_p):
    a_refs = refs[:2 * n_in]
    w_ref = refs[2 * n_in]
    x_refs = refs[2 * n_in + 1:2 * n_in + 1 + n_x]
    gate_ref = refs[2 * n_in + 1 + n_x]
    k0 = 0
    acc = None
    for j in range(n_in):
        a = _pick(is_prompt, a_refs[2 * j], a_refs[2 * j + 1])
        kw = a.shape[1]
        part = _bdot(a, w_ref[k0:k0 + kw, :])
        acc = part if acc is None else acc + part
        k0 += kw
    x = _pick(is_prompt, *x_refs) if n_x == 2 else x_refs[0][...]
    return x + gate_ref[...] * acc


def _n_proj_refs(n_in, n_x, n_p):
    return 2 * n_in + 1 + n_x + 1


SWIGLU_ROWS = 512


def _swiglu_accumulate(h_scr, acc_scr, w1_ref, w3_ref, w2_ref, rows, scale=None):
    w1, w3, w2 = w1_ref[...].astype(BF16), w3_ref[...].astype(BF16), w2_ref[...].astype(BF16)
    groups = [slice(r, r + SWIGLU_ROWS) for r in range(0, rows, SWIGLU_ROWS)]
    ups = []
    for sl in groups:
        h = h_scr[sl, :]
        ups.append((jnp.dot(h, w1, preferred_element_type=F32), jnp.dot(h, w3, preferred_element_type=F32)))
    for sl, (a, b) in zip(groups, ups):
        mid = (_silu(a) * b).astype(BF16)
        down = jnp.dot(mid, w2, preferred_element_type=F32)
        acc_scr[sl, :] += down if scale is None else scale * down


def _ffn_kernel(*refs, proj, nc):
    n = _n_proj_refs(**proj)
    g_ref, sh_ref, sc_ref, gate_ref, w1_ref, w3_ref, w2_ref, o_ref, h_scr, b1_scr, b3_scr, b2_scr = refs[n:]
    i = pl.program_id(0)
    tile = jnp.maximum(i - (nc - 1), 0)
    is_prompt = tile < proj["n_p"]
    rows = h_scr.shape[0]

    @pl.when((i == 0) | (i >= nc))
    def _():
        x = _proj_value(refs[:n], is_prompt, **proj)
        o_ref[...] = x
        h_scr[...] = _norm_mod(x, g_ref[...], sh_ref[...], sc_ref[...]).astype(BF16)

    def chunk(c):
        _swiglu_accumulate(h_scr, o_ref, b1_scr.at[c], b3_scr.at[c], b2_scr.at[c], rows, scale=gate_ref[...])

    @pl.when(i < nc)
    def _():
        b1_scr[i] = w1_ref[...].astype(BF16)
        b3_scr[i] = w3_ref[...].astype(BF16)
        b2_scr[i] = w2_ref[...].astype(BF16)
        chunk(i)

    @pl.when(i >= nc)
    def _():
        for c in range(nc):
            chunk(c)


def ffn_residual(acts, w_out, x, g, mods, w1, w3, w2, tm=512, tf=256):
    T = sum(a.shape[0] for a in acts[0])
    nc = D_FF // tf
    tile_of = lambda i: jnp.maximum(i - (nc - 1), 0)
    chunk_of = lambda i: jnp.minimum(i, nc - 1)
    p_specs, p_args, proj = _proj_inputs(acts, w_out, x, mods, tm, tile_of)
    return pl.pallas_call(
        functools.partial(_ffn_kernel, proj=proj, nc=nc),
        grid=(T // tm + nc - 1,),
        in_specs=p_specs + [
                  pl.BlockSpec((1, D), lambda i: (0, 0)),
                  _mod_spec(3, tm, tile_of), _mod_spec(4, tm, tile_of), _mod_spec(5, tm, tile_of),
                  pl.BlockSpec((D, tf), lambda i: (0, chunk_of(i))),
                  pl.BlockSpec((D, tf), lambda i: (0, chunk_of(i))),
                  pl.BlockSpec((tf, D), lambda i: (chunk_of(i), 0))],
        out_specs=pl.BlockSpec((tm, D), lambda i: (tile_of(i), 0)),
        out_shape=jax.ShapeDtypeStruct((T, D), F32),
        scratch_shapes=[pltpu.VMEM((tm, D), BF16), pltpu.VMEM((nc, D, tf), BF16),
                        pltpu.VMEM((nc, D, tf), BF16), pltpu.VMEM((nc, tf, D), BF16)],
        compiler_params=_cparams("arbitrary"),
        name="ffn",
    )(*p_args, g, mods, mods, mods, w1, w3, w2)


def _qkv_kernel(x_ref, g_ref, sh_ref, sc_ref, w_ref, q_ref, kv_ref):
    h = _norm_mod(x_ref[...], g_ref[...], sh_ref[...], sc_ref[...]).astype(BF16)
    nq = q_ref.shape[1]
    q_ref[...] = _bdot(h, w_ref[:, :nq]).astype(q_ref.dtype)
    kv_ref[...] = _bdot(h, w_ref[:, nq:])


def qkv_proj(x, g, mods, w_qkv, tm=1024):
    T = x.shape[0]
    nq, nkv = ATT_HEADS * ATT_HD, 2 * ATT_KV * ATT_HD
    return pl.pallas_call(
        _qkv_kernel,
        grid=(T // tm,),
        in_specs=[pl.BlockSpec((tm, D), lambda i: (i, 0)),
                  pl.BlockSpec((1, D), lambda i: (0, 0)),
                  _mod_spec(0, tm), _mod_spec(1, tm),
                  _const_spec(w_qkv, 1)],
        out_specs=[pl.BlockSpec((tm, nq), lambda i: (i, 0)),
                   pl.BlockSpec((tm, nkv), lambda i: (i, 0))],
        out_shape=[jax.ShapeDtypeStruct((T, nq), BF16), jax.ShapeDtypeStruct((T, nkv), F32)],
        compiler_params=_cparams("parallel"),
        name="qkv_proj",
    )(x, g, mods, mods, w_qkv)


def _rope_tables(L):
    half = ATT_HD // 2
    pos_r = (np.arange(L) // GRID_W).astype(np.float32)
    pos_c = (np.arange(L) % GRID_W).astype(np.float32)
    inv = (ROPE_BASE ** (-np.arange(0, half, 2, dtype=np.float32) / half)).astype(np.float32)
    cos = np.zeros((L, ATT_HD), np.float64)
    sin = np.zeros((L, ATT_HD), np.float64)
    for base, pos in ((0, pos_r), (half, pos_c)):
        ang = (pos[:, None] * inv[None, :]).astype(np.float32).astype(np.float64)
        cos[:, base:base + half] = np.concatenate([np.cos(ang), np.cos(ang)], axis=1)
        sin[:, base:base + half] = np.concatenate([-np.sin(ang), np.sin(ang)], axis=1)
    return (np.tile(cos, (1, 4)).astype(np.float32), np.tile(sin, (1, 4)).astype(np.float32))


def _seg_rms(x):
    w = x.shape[1]
    ri = lax.broadcasted_iota(jnp.int32, (w, w), 0) // ATT_HD
    ci = lax.broadcasted_iota(jnp.int32, (w, w), 1) // ATT_HD
    ss = _bdot(x * x, (ri == ci).astype(F32))
    return x * lax.rsqrt(ss * (1.0 / ATT_HD) + EPS)


LOG2E = 1.4426950408889634


def _exp2_bf16(x):
    return jnp.exp2(x.astype(BF16))


def _both_halves(tile, low):
    lane = lax.broadcasted_iota(jnp.int32, tile.shape, 1)
    other = pltpu.roll(tile, ATT_HD, 1)
    return jnp.where((lane < ATT_HD) == low, tile, other)


def _swap16(x):
    w = x.shape[1]
    lane = lax.broadcasted_iota(jnp.int32, x.shape, 1)
    return jnp.where(lane % 32 < 16, pltpu.roll(x, w - 16, 1), pltpu.roll(x, 16, 1))


def _attn_kernel(*refs, latent, tq):
    if latent:
        (q_ref, kv_ref, ck_ref, cv_ref, qg_ref, kg_ref, sink_ref, cosq_ref, sinq_ref, cosk_ref, sink_t_ref,
         o_ref, kk_scr, vt_scr, ckk_scr, cvt_scr) = refs
    else:
        q_ref, kv_ref, qg_ref, kg_ref, sink_ref, o_ref, ko_ref, vo_ref, kk_scr, vt_scr = refs
    L = kv_ref.shape[0]
    gw = ATT_KV * ATT_HD
    pw = 2 * ATT_HD
    qb = pl.program_id(1)

    vrows = vt_scr.shape[2]
    nblk = L // pw

    def vt_aug(tile, low):
        vt = tile.T[0:ATT_HD, :] if low else tile.T[ATT_HD:, :]
        return jnp.concatenate([vt, jnp.ones((vrows - ATT_HD, tile.shape[0]), F32)], axis=0).astype(BF16)

    @pl.when(qb == 0)
    def _():
        kn = _seg_rms(kv_ref[:, :gw]) * kg_ref[...]
        v = kv_ref[:, gw:]
        if latent:
            kn = kn * cosk_ref[...] + _swap16(kn) * sink_t_ref[...]
        else:
            for c in range(ATT_KV):
                ko_ref[c] = kn[:, c * ATT_HD:(c + 1) * ATT_HD]
                vo_ref[c] = v[:, c * ATT_HD:(c + 1) * ATT_HD]
        for c in range(ATT_KV):
            tile, low = slice((c // 2) * pw, (c // 2 + 1) * pw), c % 2 == 0
            kk_scr[c] = _both_halves(kn[:, tile], low).astype(BF16)
            for j in range(nblk):
                vt_scr[c, j] = vt_aug(v[j * pw:(j + 1) * pw, tile], low)
            if latent:
                ck, cv = ck_ref[c], cv_ref[c]
                ckk_scr[c] = jnp.concatenate([ck, ck], axis=1).astype(BF16)
                cvt_scr[c] = vt_aug(jnp.concatenate([cv, cv], axis=1), True)

    if latent:
        span = tq + 2 * WINDOW
        start = pl.multiple_of(jnp.clip(qb * tq - WINDOW, 0, L - span), WINDOW)
        blk0 = start // pw
        s_pos = start + lax.broadcasted_iota(jnp.int32, (span, tq), 0)
        t_pos = qb * tq + lax.broadcasted_iota(jnp.int32, (span, tq), 1)
        win_bias = jnp.where(jnp.abs(t_pos - s_pos) <= WINDOW, 0.0, NEG)
    else:
        span, blk0 = L, 0

    nt = (((1,), (1,)), ((), ()))
    low_q = lax.broadcasted_iota(jnp.int32, (tq, pw), 1) < ATT_HD
    def group_scores(c):
        qc = _seg_rms(q_ref[:, c * gw:(c + 1) * gw].astype(F32)) * qg_ref[...]
        if latent:
            qc = qc * cosq_ref[...] + _swap16(qc) * sinq_ref[...]
            kw = kk_scr[c, pl.ds(start, span), :]
        else:
            kw = kk_scr[c]
        qc = qc * (LOG2E / math.sqrt(ATT_HD))
        scores = []
        for g in range(ATT_KV):
            qt = qc[:, (g // 2) * pw:(g // 2 + 1) * pw]
            qm = jnp.where(low_q if g % 2 == 0 else ~low_q, qt, 0.0).astype(BF16)
            lw = lax.dot_general(kw, qm, nt, preferred_element_type=F32)
            lc = lax.dot_general(ckk_scr[c], qm, nt, preferred_element_type=F32) if latent else None
            scores.append((lw, lc))
        return scores

    def group_outputs(c, scores):
        vw = jnp.concatenate([vt_scr[c, blk0 + j] for j in range(span // pw)], axis=1)
        outs = []
        for g in range(ATT_KV):
            head = c * ATT_KV + g
            sink = sink_ref[:, head:head + 1] * LOG2E
            lw, lc = scores[g]
            if latent:
                lw = lw + win_bias
                mx = jnp.maximum(jnp.maximum(jnp.max(lw, axis=0, keepdims=True),
                                             jnp.max(lc, axis=0, keepdims=True)), sink)
                r = jnp.dot(vw, _exp2_bf16(lw - mx), preferred_element_type=F32) + jnp.dot(
                    cvt_scr[c], _exp2_bf16(lc - mx), preferred_element_type=F32)
            else:
                mx = jnp.maximum(jnp.max(lw, axis=0, keepdims=True), sink)
                r = jnp.dot(vw, _exp2_bf16(lw - mx), preferred_element_type=F32)
            den = r[ATT_HD:ATT_HD + 1, :] + jnp.exp2(sink - mx)
            outs.append(r[0:ATT_HD, :] / den)
        for t in range(2):
            o_ref[:, c * gw + t * pw:c * gw + (t + 1) * pw] = (
                jnp.concatenate(outs[2 * t:2 * t + 2], axis=0).T.astype(o_ref.dtype))

    scores = group_scores(0)
    for c in range(ATT_KV):
        nxt = group_scores(c + 1) if c + 1 < ATT_KV else None
        group_outputs(c, scores)
        scores = nxt


def attention(q, kv, row0, q_g, k_g, sink, B, L, cache=None, tq=256):
    latent = cache is not None
    gw = ATT_KV * ATT_HD
    qg = jnp.tile(q_g, ATT_KV)[None]
    kg = jnp.tile(k_g, ATT_KV)[None]
    nq = L // tq
    const = lambda a: pl.BlockSpec(a.shape, lambda b, i: (0,) * a.ndim)
    in_specs = [pl.BlockSpec((tq, ATT_HEADS * ATT_HD), lambda b, i: (row0 // tq + b * nq + i, 0)),
                pl.BlockSpec((L, 2 * gw), lambda b, i: (row0 // L + b, 0))]
    args = [q, kv]
    vrows = ATT_HD + 16
    scratch = [pltpu.VMEM((ATT_KV, L, 2 * ATT_HD), BF16),
               pltpu.VMEM((ATT_KV, L // (2 * ATT_HD), vrows, 2 * ATT_HD), BF16)]
    out_specs = [pl.BlockSpec((tq, ATT_HEADS * ATT_HD), lambda b, i: (b * nq + i, 0))]
    out_shape = [jax.ShapeDtypeStruct((B * L, ATT_HEADS * ATT_HD), BF16)]
    if latent:
        ck, cv = cache
        P = ck.shape[2]
        cos, sin = (jnp.asarray(t) for t in _rope_tables(L))
        in_specs += [pl.BlockSpec((None, ATT_KV, P, ATT_HD), lambda b, i: (b, 0, 0, 0))] * 2
        args += [ck, cv]
        in_specs += [const(qg), const(kg), pl.BlockSpec((1, ATT_HEADS), lambda b, i: (0, 0)),
                     pl.BlockSpec((tq, gw), lambda b, i: (i, 0)), pl.BlockSpec((tq, gw), lambda b, i: (i, 0)),
                     const(cos), const(sin)]
        args += [qg, kg, sink[None], cos, sin, cos, sin]
        scratch += [pltpu.VMEM((ATT_KV, P, 2 * ATT_HD), BF16), pltpu.VMEM((ATT_KV, vrows, P), BF16)]
    else:
        in_specs += [const(qg), const(kg), pl.BlockSpec((1, ATT_HEADS), lambda b, i: (0, 0))]
        args += [qg, kg, sink[None]]
        cache_spec = pl.BlockSpec((None, ATT_KV, L, ATT_HD), lambda b, i: (b, 0, 0, 0))
        out_specs += [cache_spec, cache_spec]
        out_shape += [jax.ShapeDtypeStruct((B, ATT_KV, L, ATT_HD), F32)] * 2
    outs = pl.pallas_call(
        functools.partial(_attn_kernel, latent=latent, tq=tq),
        grid=(B, nq),
        in_specs=in_specs, out_specs=out_specs, out_shape=out_shape,
        scratch_shapes=scratch,
        compiler_params=_cparams("parallel", "arbitrary"),
        name="attn_latent" if latent else "attn_context",
    )(*args)
    return outs[0] if latent else outs


MOE_TM = 1024
MOE_TOK = 1024
RUN_ALIGN = 16
MOE_LOCAL = 2 * MOE_TOK + N_EXPERTS * RUN_ALIGN
MOE_MAX_TILES = (2 * 16384 + (16384 // MOE_TOK) * N_EXPERTS * (RUN_ALIGN - 1)) // MOE_TM + N_EXPERTS + 1
RUN_SIZES = tuple(RUN_ALIGN << b for b in range(7, -1, -1))
MOE_CHUNKS = 11
MOE_MAX_ITEMS = N_EXPERTS * MOE_CHUNKS + MOE_MAX_TILES - N_EXPERTS
ITEM_FULL, ITEM_DEAD, ITEM_NONE = -1, -2, -3


def _router_kernel(*refs, proj):
    n = _n_proj_refs(**proj)
    g_ref, sh_ref, sc_ref, wr_ref, br_ref, tri_ref, x_ref, lp_ref, wts_ref, runs_ref, cnt_ref = refs[n:]

    @pl.when(pl.program_id(0) == 0)
    def _():
        cnt_ref[...] = jnp.zeros_like(cnt_ref)

    x = _proj_value(refs[:n], pl.program_id(0) < proj["n_p"], **proj)
    x_ref[...] = x
    h = _norm_mod(x, g_ref[...], sh_ref[...], sc_ref[...])
    lg = lax.dot_general(wr_ref[...], h, (((1,), (1,)), ((), ())), precision=HIGHEST,
                         preferred_element_type=F32) + br_ref[...]
    row = lax.broadcasted_iota(jnp.int32, lg.shape, 0)
    m1 = jnp.max(lg, axis=0, keepdims=True)
    i1 = jnp.min(jnp.where(lg == m1, row, N_EXPERTS), axis=0, keepdims=True)
    l2 = jnp.where(row == i1, -jnp.inf, lg)
    m2 = jnp.max(l2, axis=0, keepdims=True)
    i2 = jnp.min(jnp.where(l2 == m2, row, N_EXPERTS), axis=0, keepdims=True)
    e2 = jnp.exp(m2 - m1)
    w1 = 1.0 / (1.0 + e2)
    wts_ref[...] = jnp.concatenate([w1, e2 * w1], axis=0)
    oh1 = (row == i1).astype(F32)
    oh2 = (row == i2).astype(F32)
    cs1 = _bdot(oh1, tri_ref[...])
    cs2 = _bdot(oh2, tri_ref[...])
    tot1 = jnp.sum(oh1, axis=1, keepdims=True)
    run = jnp.ceil((tot1 + jnp.sum(oh2, axis=1, keepdims=True)) * (1.0 / RUN_ALIGN)) * RUN_ALIGN
    run_b = jnp.broadcast_to(run, (N_EXPERTS, 128))
    er = lax.broadcasted_iota(jnp.int32, (N_EXPERTS, N_EXPERTS), 0)
    ec = lax.broadcasted_iota(jnp.int32, (N_EXPERTS, N_EXPERTS), 1)
    start = jnp.dot((ec < er).astype(F32), run_b, precision=HIGHEST, preferred_element_type=F32)
    last = lax.broadcasted_iota(jnp.int32, (N_EXPERTS, 128), 0) == N_EXPERTS - 1
    run_b = jnp.where(last, MOE_LOCAL - start, run_b)
    st = start[:, 0:1]
    p1 = jnp.sum(oh1 * (st + cs1), axis=0, keepdims=True)
    p2 = jnp.sum(oh2 * (st + tot1 + cs2), axis=0, keepdims=True)
    lp_ref[...] = jnp.concatenate([p1, p2], axis=0).astype(jnp.int32)
    lane = lax.broadcasted_iota(jnp.int32, (N_EXPERTS, 128), 1)
    runs_ref[...] = jnp.where(lane == 0, run_b, jnp.where(lane == 1, start, cnt_ref[...]))
    cnt_ref[...] = cnt_ref[...] + run_b


def moe_router(acts, w_out, x, g, mods, w_router, b_router, tm=MOE_TOK):
    T = x.shape[0]
    tri = jnp.asarray(np.triu(np.ones((tm, tm), np.float32), k=1)).astype(BF16)
    tok2 = lambda dt: jax.ShapeDtypeStruct((2, T), dt)
    p_specs, p_args, proj = _proj_inputs(acts, w_out, x, mods, tm)
    return pl.pallas_call(
        functools.partial(_router_kernel, proj=proj),
        grid=(T // tm,),
        in_specs=p_specs + [
                  pl.BlockSpec((1, D), lambda i: (0, 0)),
                  _mod_spec(3, tm), _mod_spec(4, tm),
                  pl.BlockSpec((N_EXPERTS, D), lambda i: (0, 0)),
                  pl.BlockSpec((N_EXPERTS, 1), lambda i: (0, 0)),
                  _const_spec(tri, 1)],
        out_specs=[pl.BlockSpec((tm, D), lambda i: (i, 0)),
                   pl.BlockSpec((2, tm), lambda i: (0, i)),
                   pl.BlockSpec((2, tm), lambda i: (0, i)),
                   pl.BlockSpec((None, N_EXPERTS, 128), lambda i: (i, 0, 0)),
                   pl.BlockSpec((N_EXPERTS, 128), lambda i: (0, 0))],
        out_shape=[jax.ShapeDtypeStruct((T, D), F32), tok2(jnp.int32), tok2(F32),
                   jax.ShapeDtypeStruct((T // tm, N_EXPERTS, 128), F32),
                   jax.ShapeDtypeStruct((N_EXPERTS, 128), F32)],
        compiler_params=_cparams("arbitrary"),
        name="moe_router",
    )(*p_args, g, mods, mods, w_router.T, b_router[:, None], tri)


def moe_layout(runs, totals):
    rows = totals[:, 0].astype(jnp.int32)
    tiles = (rows + MOE_TM - 1) // MOE_TM
    tile_end = jnp.cumsum(tiles)
    group = (tile_end - tiles) * MOE_TM
    run_len = runs[:, :, 0].astype(jnp.int32)
    run_src = runs[:, :, 1].astype(jnp.int32)
    run_dst = group[None, :] + runs[:, :, 2].astype(jnp.int32)
    tail = jnp.stack([group + rows, tiles * MOE_TM - rows]).astype(jnp.int32)
    n_tiles = tile_end[-1]
    t = jnp.arange(MOE_MAX_TILES, dtype=jnp.int32)
    tile_e = jnp.sum(t[:, None] >= tile_end[None, :], axis=1).astype(jnp.int32)
    last_e = jnp.sum((n_tiles - 1) >= tile_end).astype(jnp.int32)
    tile_e = jnp.where(t < n_tiles, tile_e, last_e)
    first = jnp.sum(jnp.where(tile_e[:, None] == jnp.arange(N_EXPERTS), (tile_end - tiles)[None, :], 0), axis=1)
    e_rows = jnp.sum(jnp.where(tile_e[:, None] == jnp.arange(N_EXPERTS), rows[None, :], 0), axis=1)
    tile_rows = jnp.where(t < n_tiles, jnp.clip(e_rows - (t - first) * MOE_TM, 0, MOE_TM), 0).astype(jnp.int32)
    run_tab = jnp.stack([run_len, run_src, run_dst]).reshape(3, -1)
    live = t < n_tiles
    is_first = (t == first) & live
    n_items = jnp.where(is_first, MOE_CHUNKS, 1)
    item_end = jnp.cumsum(n_items)
    j = jnp.arange(MOE_MAX_ITEMS, dtype=jnp.int32)
    it_tile = jnp.minimum(jnp.sum(j[:, None] >= item_end[None, :], axis=1), MOE_MAX_TILES - 1).astype(jnp.int32)
    chunk = j - jnp.take(item_end - n_items, it_tile)
    kind = jnp.where(jnp.take(is_first, it_tile), chunk, jnp.where(jnp.take(live, it_tile), ITEM_FULL, ITEM_DEAD))
    kind = jnp.where(j >= item_end[-1], ITEM_NONE, kind)
    items = jnp.stack([it_tile, kind.astype(jnp.int32)])
    return run_tab, tail, tile_e, n_tiles.astype(jnp.int32).reshape(1), tile_rows, items


def _run_copies(tab_ref, i, local_ref, global_ref, sem, to_global):
    out = []
    for e in range(N_EXPERTS):
        k = i * N_EXPERTS + e
        n, src, dst = tab_ref[0, k], tab_ref[1, k], tab_ref[2, k]
        for size in RUN_SIZES:
            done = (n // (2 * size)) * (2 * size)
            loc = local_ref.at[pl.ds(pl.multiple_of(src + done, RUN_ALIGN), size), :]
            glo = global_ref.at[pl.ds(pl.multiple_of(dst + done, RUN_ALIGN), size), :]
            copy = pltpu.make_async_copy(loc, glo, sem) if to_global else pltpu.make_async_copy(glo, loc, sem)
            out.append(((n & size) != 0, copy))
    return out


def _start(copies, live=True):
    for pred, copy in copies:
        pl.when(pred & live)(copy.start)


def _wait(copies, live=True):
    for pred, copy in copies:
        pl.when(pred & live)(copy.wait)


def _start_then_wait(copies):
    _start(copies)
    _wait(copies)


def _dispatch_kernel(tab_ref, tail_ref, nt_ref, lp_ref, x_ref, g_ref, sh_ref, sc_ref, xs_ref, hs_scr, z_scr, sem):
    i = pl.program_id(0)
    tm = x_ref.shape[0]
    buf = i % 2
    h = _norm_mod(x_ref[...], g_ref[...], sh_ref[...], sc_ref[...]).astype(BF16)
    slot = lax.broadcasted_iota(jnp.int32, (MOE_LOCAL, tm), 0)
    perm = jnp.where((slot == lp_ref[0:1, :]) | (slot == lp_ref[1:2, :]), 1.0, 0.0).astype(BF16)
    hs_scr[buf] = jnp.dot(perm, h, preferred_element_type=F32).astype(BF16)
    copies = _run_copies(tab_ref, i, hs_scr.at[buf], xs_ref, sem.at[buf], to_global=True)
    _start(copies)
    _wait(_run_copies(tab_ref, jnp.maximum(i - 1, 0), hs_scr.at[1 - buf], xs_ref, sem.at[1 - buf], to_global=True),
          live=i > 0)

    @pl.when(i == 0)
    def _():
        z_scr[...] = jnp.zeros_like(z_scr)
        zrows = z_scr.shape[0]

        def zero_tile(t, carry):
            for part in range(MOE_TM // zrows):
                dst = xs_ref.at[pl.ds(pl.multiple_of(t * MOE_TM + part * zrows, zrows), zrows), :]
                copy = pltpu.make_async_copy(z_scr, dst, sem.at[2])
                copy.start()
                copy.wait()
            return carry

        lax.fori_loop(nt_ref[0], MOE_MAX_TILES, zero_tile, 0)
        tails = []
        for e in range(N_EXPERTS):
            start, n = tail_ref[0, e], tail_ref[1, e]
            for size in RUN_SIZES:
                if size >= MOE_TM:
                    continue
                done = (n // (2 * size)) * (2 * size)
                dst = xs_ref.at[pl.ds(pl.multiple_of(start + done, RUN_ALIGN), size), :]
                tails.append(((n & size) != 0, pltpu.make_async_copy(z_scr.at[pl.ds(0, size), :], dst, sem.at[2])))
        _start_then_wait(tails)

    _wait(copies, live=i == pl.num_programs(0) - 1)


def moe_dispatch(x, g, mods, lp, run_tab, tail, n_tiles, tm=MOE_TOK):
    T = x.shape[0]
    n_rows = MOE_MAX_TILES * MOE_TM
    return pl.pallas_call(
        _dispatch_kernel,
        grid_spec=pltpu.PrefetchScalarGridSpec(
            num_scalar_prefetch=3,
            grid=(T // tm,),
            in_specs=[pl.BlockSpec((2, tm), lambda i, *_: (0, i)),
                      pl.BlockSpec((tm, D), lambda i, *_: (i, 0)),
                      pl.BlockSpec((1, D), lambda i, *_: (0, 0)),
                      _mod_spec(3, tm), _mod_spec(4, tm)],
            out_specs=pl.BlockSpec(memory_space=pl.ANY),
            scratch_shapes=[pltpu.VMEM((2, MOE_LOCAL, D), BF16), pltpu.VMEM((MOE_TM // 2, D), BF16),
                            pltpu.SemaphoreType.DMA((3,))]),
        out_shape=jax.ShapeDtypeStruct((n_rows, D), BF16),
        compiler_params=_cparams("arbitrary"),
        name="moe_dispatch",
    )(run_tab, tail, n_tiles, lp, x, g, mods, mods)


def _moe_group_kernel(it_ref, te_ref, nt_ref, tr_ref, x_ref, w1_ref, w3_ref, w2_ref, o_ref, acc_scr, b1_scr, b3_scr,
                      b2_scr):
    j = pl.program_id(0)
    kind = it_ref[1, j]
    rows = tr_ref[it_ref[0, j]]
    half = MOE_TM // 2

    def chunk(c):
        @pl.when(rows > half)
        def _():
            _swiglu_accumulate(x_ref, acc_scr, b1_scr.at[c], b3_scr.at[c], b2_scr.at[c], MOE_TM)

        @pl.when(rows <= half)
        def _():
            _swiglu_accumulate(x_ref, acc_scr, b1_scr.at[c], b3_scr.at[c], b2_scr.at[c], half)

    @pl.when((kind == 0) | (kind == ITEM_FULL))
    def _():
        acc_scr[...] = jnp.zeros_like(acc_scr)

    @pl.when(kind >= 0)
    def _():
        b1_scr[kind] = w1_ref[...].astype(BF16)
        b3_scr[kind] = w3_ref[...].astype(BF16)
        b2_scr[kind] = w2_ref[...].astype(BF16)
        chunk(kind)

    for n_rows, pred in ((MOE_TM, rows > half), (half, rows <= half)):
        @pl.when((kind == ITEM_FULL) & pred)
        def _():
            for c in range(MOE_CHUNKS):
                _swiglu_accumulate(x_ref, acc_scr, b1_scr.at[c], b3_scr.at[c], b2_scr.at[c], n_rows)

    @pl.when((kind == MOE_CHUNKS - 1) | (kind == ITEM_FULL))
    def _():
        o_ref[...] = acc_scr[...].astype(o_ref.dtype)

    @pl.when(kind == ITEM_DEAD)
    def _():
        o_ref[...] = jnp.zeros_like(o_ref)


def moe_grouped_swiglu(xs, items, tile_e, n_tiles, tile_rows, w1, w3, w2):
    tf = D_FF // MOE_CHUNKS
    tile = lambda j, it, te, nt, tr: it[0, j]
    row_in = lambda j, it, te, nt, tr: (jnp.minimum(tile(j, it, te, nt, tr), jnp.maximum(nt[0] - 1, 0)), 0)
    wchunk = lambda j, it: jnp.where(it[1, j] >= 0, it[1, j], MOE_CHUNKS - 1)
    wcol = lambda j, it, te, nt, tr: (te[it[0, j]], 0, wchunk(j, it))
    wrow = lambda j, it, te, nt, tr: (te[it[0, j]], wchunk(j, it), 0)
    return pl.pallas_call(
        _moe_group_kernel,
        grid_spec=pltpu.PrefetchScalarGridSpec(
            num_scalar_prefetch=4,
            grid=(MOE_MAX_ITEMS,),
            in_specs=[pl.BlockSpec((MOE_TM, D), row_in),
                      pl.BlockSpec((None, D, tf), wcol),
                      pl.BlockSpec((None, D, tf), wcol),
                      pl.BlockSpec((None, tf, D), wrow)],
            out_specs=pl.BlockSpec((MOE_TM, D), lambda j, it, te, nt, tr: (it[0, j], 0)),
            scratch_shapes=[pltpu.VMEM((MOE_TM, D), F32), pltpu.VMEM((MOE_CHUNKS, D, tf), BF16),
                            pltpu.VMEM((MOE_CHUNKS, D, tf), BF16), pltpu.VMEM((MOE_CHUNKS, tf, D), BF16)]),
        out_shape=jax.ShapeDtypeStruct(xs.shape, BF16),
        compiler_params=_cparams("arbitrary"),
        name="moe_grouped",
    )(items, tile_e, n_tiles, tile_rows, xs, w1, w3, w2)


def _combine_kernel(tab_ref, lp_ref, wt_ref, x_ref, gate_ref, ys_ref, op_ref, os_ref, yl_scr, sem, *, n_p):
    i = pl.program_id(0)
    tm = x_ref.shape[0]
    buf = i % 2
    last = pl.num_programs(0) - 1
    gather = lambda t, b: _run_copies(tab_ref, t, yl_scr.at[b], ys_ref, sem.at[b], to_global=False)
    _start(gather(i, buf), live=i == 0)
    _start(gather(jnp.minimum(i + 1, last), 1 - buf), live=i < last)
    slot = lax.broadcasted_iota(jnp.int32, (tm, MOE_LOCAL), 1)
    mix = (jnp.where(slot == lp_ref[:, 0:1], wt_ref[:, 0:1], 0.0)
           + jnp.where(slot == lp_ref[:, 1:2], wt_ref[:, 1:2], 0.0)).astype(BF16)
    _wait(gather(i, buf))
    moe = jnp.dot(mix, yl_scr[buf], preferred_element_type=F32)
    out = x_ref[...] + gate_ref[...] * moe

    @pl.when(pl.program_id(0) < n_p)
    def _():
        op_ref[...] = out

    @pl.when(pl.program_id(0) >= n_p)
    def _():
        os_ref[...] = out


def moe_combine(x, mods, lp, wts, run_tab, ys, t_prompt, tm=MOE_TOK):
    T = x.shape[0]
    n_p = t_prompt // tm
    return pl.pallas_call(
        functools.partial(_combine_kernel, n_p=n_p),
        grid_spec=pltpu.PrefetchScalarGridSpec(
            num_scalar_prefetch=1,
            grid=(T // tm,),
            in_specs=[pl.BlockSpec((tm, 2), lambda i, *_: (i, 0)),
                      pl.BlockSpec((tm, 2), lambda i, *_: (i, 0)),
                      pl.BlockSpec((tm, D), lambda i, *_: (i, 0)),
                      _mod_spec(5, tm),
                      pl.BlockSpec(memory_space=pl.ANY)],
            out_specs=_part_specs((tm, D), n_p),
            scratch_shapes=[pltpu.VMEM((2, MOE_LOCAL, D), BF16), pltpu.SemaphoreType.DMA((2,))]),
        out_shape=[jax.ShapeDtypeStruct((t_prompt, D), F32), jax.ShapeDtypeStruct((T - t_prompt, D), F32)],
        compiler_params=_cparams("arbitrary"),
        name="moe_combine",
    )(run_tab, lp.T, wts.T, x, mods, ys)


def kernel(x_prompt, x_sample, state_C, state_n, state_m, cache_k, cache_v, c, c_ctx, norm1_g, norm2_g, w_ada, b_ada, ev_w_in, ev_conv, hy_w1, hy_b1, hy_w2, hy_b2, hy_w3, hy_freq, hy_d, ml_b_gate, ml_norm_g, ev_w_out, ff_w1, ff_w3, ff_w2, at_w_qkv, at_q_g, at_k_g, at_sink, at_w_out, moe_w_router, moe_b_router, moe_w1, moe_w3, moe_w2):
    BP, LP, _ = x_prompt.shape
    BS, LS, _ = x_sample.shape
    TP = BP * LP
    assert TP % GROUP == 0 and TP // GROUP == N_PROMPT_GROUPS and LS == GROUP and BS == 8

    xp, xs = x_prompt.reshape(TP, D), x_sample.reshape(BS * LS, D)
    cond = jnp.concatenate([c_ctx[None], c, jnp.zeros((16 - 1 - BS, D), F32)], axis=0)
    mods = adaln_table(cond, w_ada, b_ada)

    u, gates = even_in_proj(xp, xs, norm1_g[0:1], mods[0], ev_w_in[0], ml_b_gate[0].reshape(1, N_GATES))
    hy = []
    for seq0, B, L, nb in ((0, BP, LP, 4), (TP // LS, BS, LS, 1)):
        fwd, inv = (jnp.asarray(t).astype(BF16) for t in _dft_tables(L))
        ka, kb = hyena_filter_spectra(L, hy_w1[0], hy_b1[0], hy_w2[0], hy_b2[0], hy_w3[0], hy_freq[0], fwd)
        hy.append(hyena_mix(u, seq0, B, L, ev_conv[0], hy_d[0], fwd, inv, ka, kb, nb))
    ml_p, new_C, new_n, new_m = mlstm_mix(u, gates, 0, BP, LP, ml_norm_g[0], want_state=True)
    ml_s = mlstm_mix(u, gates, TP // LS, BS, LS, ml_norm_g[0],
                     state=(state_C[:, 0], state_n[:, 0], state_m[:, 0]))
    x = ffn_residual([hy, (ml_p, ml_s)], ev_w_out[0], (xp, xs), norm2_g[0:1], mods[0], ff_w1[0], ff_w3[0], ff_w2[0])

    q, kv = qkv_proj(x, norm1_g[1:2], mods[1], at_w_qkv[0])
    o_p, new_k, new_v = attention(q, kv, 0, at_q_g[0], at_k_g[0], at_sink[0], BP, LP)
    o_s = attention(q, kv, TP, at_q_g[0], at_k_g[0], at_sink[0], BS, LS, cache=(cache_k[:, 0], cache_v[:, 0]))
    x, lp, wts, runs, totals = moe_router([(o_p, o_s)], at_w_out[0], x, norm2_g[1:2], mods[1],
                                          moe_w_router[0], moe_b_router[0])
    run_tab, tail, tile_e, n_tiles, tile_rows, items = moe_layout(runs, totals)
    xsort = moe_dispatch(x, norm2_g[1:2], mods[1], lp, run_tab, tail, n_tiles)
    ysort = moe_grouped_swiglu(xsort, items, tile_e, n_tiles, tile_rows, moe_w1[0], moe_w3[0], moe_w2[0])
    yp, ys = moe_combine(x, mods[1], lp, wts, run_tab, ysort, TP)

    return (yp.reshape(BP, LP, D), ys.reshape(BS, LS, D),
            new_C[:, None], new_n[:, None], new_m[:, None], new_k[:, None], new_v[:, None])
```

```python
import functools
import math

import numpy as np
import jax
import jax.numpy as jnp
from jax import lax
from jax.experimental import pallas as pl
from jax.experimental.pallas import tpu as pltpu

F32 = jnp.float32
BF16 = jnp.bfloat16
HIGHEST = lax.Precision.HIGHEST

D = 1024
GROUP = 1024
N_PROMPT_GROUPS = 8
HY_W = 512
ML_HEADS = 4
ML_HD = 128
ML_CHUNK = 256
EVEN_MAIN = 3 * HY_W + 4 * 512
N_GATES = 16
ATT_HD = 64
ATT_HEADS = 16
ATT_KV = 4
WINDOW = 128
GRID_W = 64
ROPE_BASE = 10000.0
D_FF = 2816
N_EXPERTS = 8
EPS = 1e-6
NEG = -1e30
VMEM_LIMIT = 56 * 1024 * 1024


def _cparams(*sem, flags=None):
    return pltpu.CompilerParams(dimension_semantics=sem, vmem_limit_bytes=VMEM_LIMIT, flags=flags)


def _mod_row(i, tm):
    return jnp.maximum(i * tm // GROUP - (N_PROMPT_GROUPS - 1), 0)


def _silu(x):
    return x * jax.nn.sigmoid(x)


def _bdot(a, b):
    return jnp.dot(a.astype(BF16), b.astype(BF16), preferred_element_type=F32)


def _norm_mod(x, g, sh, sc):
    y = x * lax.rsqrt(jnp.mean(x * x, axis=-1, keepdims=True) + EPS) * g
    return y * (1.0 + sc) + sh


def _adaln_kernel(c_ref, w_ref, b_ref, o_ref):
    s = _silu(c_ref[...])
    o_ref[...] = jnp.dot(s, w_ref[...], precision=HIGHEST, preferred_element_type=F32) + b_ref[...]


def adaln_table(cond, w_ada, b_ada):
    depth = w_ada.shape[0]
    tn = 1536
    out = pl.pallas_call(
        _adaln_kernel,
        grid=(depth, 6 * D // tn),
        in_specs=[pl.BlockSpec((16, D), lambda l, j: (0, 0)),
                  pl.BlockSpec((None, D, tn), lambda l, j: (l, 0, j)),
                  pl.BlockSpec((None, 1, tn), lambda l, j: (l, 0, j))],
        out_specs=pl.BlockSpec((None, 16, tn), lambda l, j: (l, 0, j)),
        out_shape=jax.ShapeDtypeStruct((depth, 16, 6 * D), F32),
        compiler_params=_cparams("parallel", "parallel"),
        name="adaln",
    )(cond, w_ada, b_ada.reshape(depth, 1, 6 * D))
    return out.reshape(depth, 16, 1, 6 * D)


def _same_tile(i):
    return i


def _part_specs(block, n_p, tile_of=_same_tile):
    return [pl.BlockSpec(block, lambda i, *_: (jnp.minimum(tile_of(i), n_p - 1), 0)),
            pl.BlockSpec(block, lambda i, *_: (jnp.maximum(tile_of(i) - n_p, 0), 0))]


def _pick(is_prompt, p_ref, s_ref):
    return jnp.where(is_prompt, p_ref[...], s_ref[...])


def _mod_spec(k, tm, tile_of=_same_tile):
    return pl.BlockSpec((None, 1, D), lambda i, *_: (_mod_row(tile_of(i), tm), 0, k))


def _log_sigmoid(x):
    return jnp.minimum(x, 0.0) - jnp.log(1.0 + jnp.exp(-jnp.abs(x)))


def _split3(x):
    hi = x.astype(BF16)
    r = x - hi.astype(F32)
    mid = r.astype(BF16)
    return hi, mid, (r - mid.astype(F32)).astype(BF16)


def _even_in_kernel(xp_ref, xs_ref, g_ref, sh_ref, sc_ref, w_ref, bg_ref, lo_ref, up_ref, u_ref, gate_ref, *, n_p, tn):
    is_prompt = pl.program_id(0) < n_p
    h = _norm_mod(_pick(is_prompt, xp_ref, xs_ref), g_ref[...], sh_ref[...], sc_ref[...]).astype(BF16)
    for j in range(EVEN_MAIN // tn):
        u_ref[:, j * tn:(j + 1) * tn] = _bdot(h, w_ref[:, j * tn:(j + 1) * tn]).astype(u_ref.dtype)
    gates = _bdot(h, w_ref[:, EVEN_MAIN:]) + bg_ref[...]
    lf = _log_sigmoid(gates)
    col = lax.broadcasted_iota(jnp.int32, (1, N_GATES), 1)
    is_forget = (col // ML_HEADS) % 2 == 1
    is_rev = col >= N_GATES // 2
    for ch in range(h.shape[0] // ML_CHUNK):
        sl = slice(ch * ML_CHUNK, (ch + 1) * ML_CHUNK)
        parts = _split3(lf[sl])
        cf = sum(jnp.dot(lo_ref[...], p, preferred_element_type=F32) for p in parts)
        cr = sum(jnp.dot(up_ref[...], p, preferred_element_type=F32) for p in parts)
        gate_ref[sl, :] = jnp.where(is_forget, jnp.where(is_rev, cr, cf), gates[sl])


def even_in_proj(xp, xs, g, mods, w_in, b_gate, tm=1024, tn=512):
    T = xp.shape[0] + xs.shape[0]
    tri = np.tril(np.ones((ML_CHUNK, ML_CHUNK), np.float32))
    lo, up = jnp.asarray(tri).astype(BF16), jnp.asarray(tri.T).astype(BF16)
    return pl.pallas_call(
        functools.partial(_even_in_kernel, n_p=xp.shape[0] // tm, tn=tn),
        grid=(T // tm,),
        in_specs=_part_specs((tm, D), xp.shape[0] // tm) + [
                  pl.BlockSpec((1, D), lambda i: (0, 0)),
                  _mod_spec(0, tm), _mod_spec(1, tm),
                  _const_spec(w_in, 1),
                  pl.BlockSpec((1, N_GATES), lambda i: (0, 0)),
                  _const_spec(lo, 1), _const_spec(up, 1)],
        out_specs=[pl.BlockSpec((tm, EVEN_MAIN), lambda i: (i, 0)),
                   pl.BlockSpec((tm, N_GATES), lambda i: (i, 0))],
        out_shape=[jax.ShapeDtypeStruct((T, EVEN_MAIN), BF16),
                   jax.ShapeDtypeStruct((T, N_GATES), F32)],
        compiler_params=_cparams("parallel"),
        name="even_in_proj",
    )(xp, xs, g, mods, mods, w_in, b_gate, lo, up)


def _dft_tables(L):
    n = 2 * L
    f = np.arange(L, dtype=np.int64)[:, None]
    s = np.arange(L, dtype=np.int64)[None, :]
    ang = 2.0 * np.pi * ((f * s) % n).astype(np.float64) / n
    fwd = np.concatenate([np.cos(ang), -np.sin(ang)], axis=0)
    fwd[L, :] = np.where(np.arange(L) % 2 == 0, 1.0, -1.0)
    t = np.arange(L, dtype=np.int64)[:, None]
    ff = np.arange(L, dtype=np.int64)[None, :]
    ang = 2.0 * np.pi * ((t * ff) % n).astype(np.float64) / n
    inv_re = 2.0 * np.cos(ang) / n
    inv_re[:, 0] = 1.0 / n
    inv_im = -2.0 * np.sin(ang) / n
    inv_im[:, 0] = np.where(np.arange(L) % 2 == 0, 1.0, -1.0) / n
    inv = np.concatenate([inv_re, inv_im], axis=1)
    return fwd.astype(np.float32), inv.astype(np.float32)


def _filter_tables(L):
    t = np.linspace(0.0, 1.0, L, dtype=np.float32).astype(np.float64)[:, None]
    w = 2.0 * math.pi * np.arange(L, dtype=np.float64)[:, None] / L
    bands = np.linspace(1e-4, 16 - 1, 16, dtype=np.float32).astype(np.float64)[None, :]
    z = np.concatenate([t, np.cos(bands * w), -np.sin(bands * w)], axis=-1)
    zp = np.zeros((L, 128), np.float64)
    zp[:, :z.shape[1]] = z
    max_decay = math.log(1e-2) / 0.3
    min_decay = math.log(1e-2) / 1.5
    deltas = np.linspace(min_decay, max_decay, HY_W, dtype=np.float32).astype(np.float64)
    decay = np.exp(-t * np.abs(deltas))
    return zp.astype(np.float32), decay.astype(np.float32)


def _hy_filter_kernel(z_ref, dec_ref, w1_ref, b1_ref, w2_ref, b2_ref, w3_ref, fr_ref, fwd_ref,
                      ka_ref, kb_ref):
    L = z_ref.shape[0]
    hdot = functools.partial(jnp.dot, precision=HIGHEST, preferred_element_type=F32)
    h = jnp.sin(fr_ref[0:1, :] * (hdot(z_ref[...], w1_ref[...]) + b1_ref[...]))
    h = jnp.sin(fr_ref[1:2, :] * (hdot(h, w2_ref[...]) + b2_ref[...]))
    h = hdot(h, w3_ref[...])
    row0 = lax.broadcasted_iota(jnp.int32, (L, 1), 0) == 0
    h0 = h[:, :HY_W] * dec_ref[...]
    h1 = h[:, HY_W:] * dec_ref[...]
    l1 = jnp.sum(jnp.abs(h0), axis=0, keepdims=True) + jnp.sum(jnp.abs(h1), axis=0, keepdims=True)
    inv = 1.0 / l1
    h0 = h0 * inv
    h1 = jnp.where(row0, 0.0, h1 * inv)
    f0 = _bdot(fwd_ref[...], h0)
    f1 = _bdot(fwd_ref[...], h1)
    ka_ref[...] = f0[:L] + f1[:L]
    kb_ref[...] = jnp.where(row0, f0[L:] + f1[L:], f0[L:] - f1[L:])


def _const_spec(a, n_grid):
    return pl.BlockSpec(a.shape, lambda *_: (0,) * a.ndim, pipeline_mode=pl.Buffered(1))


def hyena_filter_spectra(L, w1, b1, w2, b2, w3, freq, fwd):
    z, dec = _filter_tables(L)
    pad2 = lambda a, r, c: jnp.pad(a, ((0, r - a.shape[0]), (0, c - a.shape[1])))
    args = (jnp.asarray(z), jnp.asarray(dec), pad2(w1, 128, 128), pad2(b1[None], 1, 128),
            pad2(w2, 128, 128), pad2(b2[None], 1, 128), pad2(w3, 128, 4 * HY_W), pad2(freq, 2, 128), fwd)
    in_specs = [_const_spec(a, 1) for a in args]
    in_specs[6] = pl.BlockSpec((128, 2 * HY_W), lambda o: (0, o))
    shp = jax.ShapeDtypeStruct((2, L, HY_W), F32)
    out_spec = pl.BlockSpec((None, L, HY_W), lambda o: (o, 0, 0))
    return pl.pallas_call(
        _hy_filter_kernel,
        grid=(2,),
        in_specs=in_specs,
        out_specs=[out_spec, out_spec],
        out_shape=[shp, shp],
        compiler_params=_cparams("arbitrary"),
        name=f"hyena_filter_{L}",
    )(*args)


def _hyena_kernel(u_ref, cw_ref, d_ref, fwd_ref, inv_ref, ka_ref, kb_ref, o_ref):
    nb, L = u_ref.shape[0], u_ref.shape[1]
    row = lax.broadcasted_iota(jnp.int32, (L, 1), 0)
    first, last = row == 0, row == L - 1
    fwd = fwd_ref[...].astype(BF16)
    inv = inv_ref[...].astype(BF16)

    def long_conv(z, o):
        zf = jnp.dot(fwd, z.astype(BF16), preferred_element_type=F32)
        a, b = zf[:L], zf[L:]
        ka, kb = ka_ref[o], kb_ref[o]
        yr = a * ka - jnp.where(first, 0.0, b * kb)
        yi = jnp.where(first, b * kb, a * kb + b * ka)
        return (jnp.dot(inv[:, :L], yr.astype(BF16), preferred_element_type=F32)
                + jnp.dot(inv[:, L:], yi.astype(BF16), preferred_element_type=F32))

    for bi in range(nb):
        u = u_ref[bi].astype(F32)
        prev = jnp.where(first, 0.0, pltpu.roll(u, 1, 0))
        nxt = jnp.where(last, 0.0, pltpu.roll(u, L - 1, 0))
        u = prev * cw_ref[0:1, :] + u * cw_ref[1:2, :] + nxt * cw_ref[2:3, :]
        v, x1, x2 = u[:, :HY_W], u[:, HY_W:2 * HY_W], u[:, 2 * HY_W:]
        z = x1 * (long_conv(v, 0) + d_ref[0:1, :] * v)
        z = x2 * (long_conv(z, 1) + d_ref[1:2, :] * z)
        o_ref[bi] = z.astype(o_ref.dtype)


def hyena_mix(u, seq0, B, L, conv_w, d_skip, fwd, inv, ka, kb, nb):
    u3 = u.reshape(-1, L, EVEN_MAIN)
    full = lambda a: _const_spec(a, 1)
    out = pl.pallas_call(
        _hyena_kernel,
        grid=(B // nb,),
        in_specs=[pl.BlockSpec((nb, L, 3 * HY_W), lambda b: (b + seq0 // nb, 0, 0)),
                  full(conv_w), full(d_skip), full(fwd), full(inv), full(ka), full(kb)],
        out_specs=pl.BlockSpec((nb, L, HY_W), lambda b: (b, 0, 0)),
        out_shape=jax.ShapeDtypeStruct((B, L, HY_W), BF16),
        compiler_params=_cparams("parallel"),
        name=f"hyena_{L}",
    )(u3, conv_w, d_skip, fwd, inv, ka, kb)
    return out.reshape(B * L, HY_W)


def _mlstm_kernel(*refs, has_state, want_state):
    q_ref, k_ref, v_ref, o_ref, gc_ref, gr_ref, ng_ref = refs[:7]
    refs = refs[7:]
    if has_state:
        c0t_ref, n0b_ref, m0_ref = refs[:3]
        refs = refs[3:]
    y_ref = refs[0]
    if want_state:
        c_out, n_out, m_out = refs[1:4]
    L, d = q_ref.shape[0], ML_HD
    T = min(ML_CHUNK, L)
    nc = L // T
    scale = 1.0 / math.sqrt(d)
    nt = (((1,), (1,)), ((), ()))
    si = lax.broadcasted_iota(jnp.int32, (T, T), 0)
    ti = lax.broadcasted_iota(jnp.int32, (T, T), 1)
    allowed = (si <= ti, si >= ti)
    chains = [(dr, h) for dr in range(2) for h in range(ML_HEADS)]
    gcol = lambda dr, gi, h: dr * 2 * ML_HEADS + gi * ML_HEADS + h

    caug_t, m = {}, {}
    for ch in chains:
        dr, h = ch
        if has_state:
            caug_t[ch] = jnp.concatenate([c0t_ref[dr, h], n0b_ref[dr, h]], axis=0)
            m[ch] = m0_ref[dr, h:h + 1, 0:1]
        else:
            caug_t[ch], m[ch] = jnp.zeros((2 * d, d), F32), jnp.zeros((1, 1), F32)

    chunk_cache = {}

    def chunk_data(h, j):
        if (h, j) not in chunk_cache:
            sl, hl = slice(j * T, (j + 1) * T), slice(h * d, (h + 1) * d)
            q = q_ref[sl, hl]
            ks = (k_ref[sl, hl].astype(F32) * scale).astype(BF16)
            v_t = v_ref[sl, hl].astype(F32).T
            vaug_t = jnp.concatenate([v_t, jnp.ones((d, T), F32)], axis=0).astype(BF16)
            s_raw = lax.dot_general(ks, q, nt, preferred_element_type=F32)
            chunk_cache[(h, j)] = (q, ks, v_t, vaug_t, s_raw)
        return chunk_cache[(h, j)]

    h_sum = {}
    for it in range(nc):
        step = {ch: (it if ch[0] == 0 else nc - 1 - it) for ch in chains}
        data = {ch: chunk_data(ch[1], step[ch]) for ch in chains}
        inter_t = {ch: lax.dot_general(caug_t[ch].astype(BF16), data[ch][0], nt, preferred_element_type=F32)
                   for ch in chains}
        gate = {}
        for ch in chains:
            dr, h = ch
            sl = slice(step[ch] * T, (step[ch] + 1) * T)
            li_r, b_r = gr_ref[gcol(dr, 0, h):gcol(dr, 0, h) + 1, sl], gr_ref[gcol(dr, 1, h):gcol(dr, 1, h) + 1, sl]
            src = gc_ref[sl, gcol(dr, 0, h):gcol(dr, 0, h) + 1] - gc_ref[sl, gcol(dr, 1, h):gcol(dr, 1, h) + 1]
            dm = jnp.where(allowed[dr], src + b_r, NEG)
            inter = b_r + m[ch]
            m_t = jnp.maximum(inter, jnp.max(dm, axis=0, keepdims=True))
            b_end = b_r[:, T - 1:T] if dr == 0 else b_r[:, 0:1]
            g_r = b_end - b_r + li_r
            m_new = jnp.maximum(b_end + m[ch], jnp.max(g_r, axis=1, keepdims=True))
            gate[ch] = (jnp.exp(dm - m_t), jnp.exp(inter - m_t), jnp.exp(-m_t), jnp.exp(g_r - m_new),
                        jnp.exp(b_end + m[ch] - m_new), m_new)
        for ch in chains:
            q, ks, v_t, vaug_t, s_raw = data[ch]
            w_intra, w_inter, floor, w_tok, decay, m_new = gate[ch]
            acc = jnp.dot(vaug_t, (s_raw * w_intra).astype(BF16), preferred_element_type=F32) + w_inter * inter_t[ch]
            h_t = acc[:d] / jnp.maximum(jnp.abs(acc[d:]), floor)
            key = (ch[1], step[ch])
            h_sum[key] = h_t if key not in h_sum else h_sum[key] + h_t
            vw_t = jnp.concatenate([v_t * w_tok, jnp.broadcast_to(w_tok, (d, T))], axis=0).astype(BF16)
            caug_t[ch] = decay * caug_t[ch] + jnp.dot(vw_t, ks, preferred_element_type=F32)
            m[ch] = m_new

    if want_state:
        for ch in chains:
            dr, h = ch
            c_out[dr, h] = caug_t[ch][:d].T
            n_out[dr, h:h + 1, :] = caug_t[ch][d:d + 1, :]
            m_out[dr, h:h + 1, :] = jnp.broadcast_to(m[ch], (1, d))
    for h in range(ML_HEADS):
        for j in range(nc):
            sl, hl = slice(j * T, (j + 1) * T), slice(h * d, (h + 1) * d)
            hv = h_sum[(h, j)].T
            y = hv * lax.rsqrt(jnp.mean(hv * hv, axis=-1, keepdims=True) + EPS) * ng_ref[:, hl]
            y_ref[sl, hl] = (y * jax.nn.sigmoid(o_ref[sl, hl].astype(F32))).astype(y_ref.dtype)


def mlstm_mix(u, gates, seq0, B, L, norm_g, state=None, want_state=False):
    u3 = u.reshape(-1, L, EVEN_MAIN)
    gc = gates.reshape(-1, L, N_GATES)[seq0:seq0 + B]
    gr = gc.transpose(0, 2, 1)
    width = ML_HEADS * ML_HD
    col = lambda i: pl.BlockSpec((None, L, width), lambda b: (b + seq0, 0, (3 * HY_W + i * width) // width))
    in_specs = [col(0), col(1), col(2), col(3),
                pl.BlockSpec((None, L, N_GATES), lambda b: (b, 0, 0)),
                pl.BlockSpec((None, N_GATES, L), lambda b: (b, 0, 0)),
                pl.BlockSpec((1, width), lambda b: (0, 0))]
    args = [u3, u3, u3, u3, gc, gr, norm_g.reshape(1, width)]
    sspec = pl.BlockSpec((None, 2, ML_HEADS, ML_HD, ML_HD), lambda b: (b, 0, 0, 0, 0))
    vspec = pl.BlockSpec((None, 2, ML_HEADS, ML_HD), lambda b: (b, 0, 0, 0))
    if state is not None:
        C0, n0, m0 = state
        in_specs += [sspec, sspec, vspec]
        args += [C0.swapaxes(-1, -2), jnp.broadcast_to(n0[..., None, :], C0.shape),
                 jnp.broadcast_to(m0[..., None], n0.shape)]
    out_specs = [pl.BlockSpec((None, L, width), lambda b: (b, 0, 0))]
    out_shape = [jax.ShapeDtypeStruct((B, L, width), BF16)]
    if want_state:
        out_specs += [sspec, vspec, vspec]
        out_shape += [jax.ShapeDtypeStruct((B, 2, ML_HEADS, ML_HD, ML_HD), F32),
                      jax.ShapeDtypeStruct((B, 2, ML_HEADS, ML_HD), F32),
                      jax.ShapeDtypeStruct((B, 2, ML_HEADS, ML_HD), F32)]
    outs = pl.pallas_call(
        functools.partial(_mlstm_kernel, has_state=state is not None, want_state=want_state),
        grid=(B,),
        in_specs=in_specs, out_specs=out_specs, out_shape=out_shape,
        compiler_params=_cparams("parallel"),
        name=f"mlstm_{L}",
    )(*args)
    y = outs[0].reshape(B * L, width)
    if not want_state:
        return y
    _, C, n, m = outs
    return y, C, n, m[..., 0]


def _proj_inputs(acts, w, x, mods, tm, tile_of=_same_tile):
    xs = tuple(x) if isinstance(x, (tuple, list)) else (x,)
    n_p = acts[0][0].shape[0] // tm
    specs = []
    for pair in acts:
        specs += _part_specs((tm, pair[0].shape[1]), n_p, tile_of)
    specs.append(pl.BlockSpec(w.shape, lambda *_: (0, 0), pipeline_mode=pl.Buffered(1)))
    specs += (_part_specs((tm, D), n_p, tile_of) if len(xs) == 2
              else [pl.BlockSpec((tm, D), lambda i, *_: (tile_of(i), 0))])
    specs.append(_mod_spec(2, tm, tile_of))
    args = [a for pair in acts for a in pair] + [w, *xs, mods]
    return specs, args, dict(n_in=len(acts), n_x=len(xs), n_p=n_p)


def _proj_value(refs, is_prompt, n_in, n_x, n_p):
    a_refs = refs[:2 * n_in]
    w_ref = refs[2 * n_in]
    x_refs = refs[2 * n_in + 1:2 * n_in + 1 + n_x]
    gate_ref = refs[2 * n_in + 1 + n_x]
    k0 = 0
    acc = None
    for j in range(n_in):
        a = _pick(is_prompt, a_refs[2 * j], a_refs[2 * j + 1])
        kw = a.shape[1]
        part = _bdot(a, w_ref[k0:k0 + kw, :])
        acc = part if acc is None else acc + part
        k0 += kw
    x = _pick(is_prompt, *x_refs) if n_x == 2 else x_refs[0][...]
    return x + gate_ref[...] * acc


def _n_proj_refs(n_in, n_x, n_p):
    return 2 * n_in + 1 + n_x + 1


SWIGLU_ROWS = 512


def _swiglu_accumulate(h_scr, acc_scr, w1_ref, w3_ref, w2_ref, rows, scale=None):
    w1, w3, w2 = w1_ref[...].astype(BF16), w3_ref[...].astype(BF16), w2_ref[...].astype(BF16)
    groups = [slice(r, r + SWIGLU_ROWS) for r in range(0, rows, SWIGLU_ROWS)]
    ups = []
    for sl in groups:
        h = h_scr[sl, :]
        ups.append((jnp.dot(h, w1, preferred_element_type=F32), jnp.dot(h, w3, preferred_element_type=F32)))
    for sl, (a, b) in zip(groups, ups):
        mid = (_silu(a) * b).astype(BF16)
        down = jnp.dot(mid, w2, preferred_element_type=F32)
        acc_scr[sl, :] += down if scale is None else scale * down


def _ffn_kernel(*refs, proj, nc):
    n = _n_proj_refs(**proj)
    g_ref, sh_ref, sc_ref, gate_ref, w1_ref, w3_ref, w2_ref, o_ref, h_scr, b1_scr, b3_scr, b2_scr = refs[n:]
    i = pl.program_id(0)
    tile = jnp.maximum(i - (nc - 1), 0)
    is_prompt = tile < proj["n_p"]
    rows = h_scr.shape[0]

    def start_tile():
        x = _proj_value(refs[:n], is_prompt, **proj)
        o_ref[...] = x
        h_scr[...] = _norm_mod(x, g_ref[...], sh_ref[...], sc_ref[...]).astype(BF16)

    def chunk(c):
        _swiglu_accumulate(h_scr, o_ref, b1_scr.at[c], b3_scr.at[c], b2_scr.at[c], rows, scale=gate_ref[...])

    pl.when(i == 0)(start_tile)

    @pl.when(i < nc)
    def _():
        b1_scr[i] = w1_ref[...].astype(BF16)
        b3_scr[i] = w3_ref[...].astype(BF16)
        b2_scr[i] = w2_ref[...].astype(BF16)
        chunk(i)

    @pl.when(i >= nc)
    def _():
        start_tile()
        for c in range(nc):
            chunk(c)


def ffn_residual(acts, w_out, x, g, mods, w1, w3, w2, tm=512, tf=256):
    T = sum(a.shape[0] for a in acts[0])
    nc = D_FF // tf
    tile_of = lambda i: jnp.maximum(i - (nc - 1), 0)
    chunk_of = lambda i: jnp.minimum(i, nc - 1)
    p_specs, p_args, proj = _proj_inputs(acts, w_out, x, mods, tm, tile_of)
    return pl.pallas_call(
        functools.partial(_ffn_kernel, proj=proj, nc=nc),
        grid=(T // tm + nc - 1,),
        in_specs=p_specs + [
                  pl.BlockSpec((1, D), lambda i: (0, 0)),
                  _mod_spec(3, tm, tile_of), _mod_spec(4, tm, tile_of), _mod_spec(5, tm, tile_of),
                  pl.BlockSpec((D, tf), lambda i: (0, chunk_of(i))),
                  pl.BlockSpec((D, tf), lambda i: (0, chunk_of(i))),
                  pl.BlockSpec((tf, D), lambda i: (chunk_of(i), 0))],
        out_specs=pl.BlockSpec((tm, D), lambda i: (tile_of(i), 0)),
        out_shape=jax.ShapeDtypeStruct((T, D), F32),
        scratch_shapes=[pltpu.VMEM((tm, D), BF16), pltpu.VMEM((nc, D, tf), BF16),
                        pltpu.VMEM((nc, D, tf), BF16), pltpu.VMEM((nc, tf, D), BF16)],
        compiler_params=_cparams("arbitrary"),
        name="ffn",
    )(*p_args, g, mods, mods, mods, w1, w3, w2)


def _qkv_kernel(x_ref, g_ref, sh_ref, sc_ref, w_ref, q_ref, kv_ref):
    h = _norm_mod(x_ref[...], g_ref[...], sh_ref[...], sc_ref[...]).astype(BF16)
    nq = q_ref.shape[1]
    q_ref[...] = _bdot(h, w_ref[:, :nq]).astype(q_ref.dtype)
    kv_ref[...] = _bdot(h, w_ref[:, nq:])


def qkv_proj(x, g, mods, w_qkv, tm=1024):
    T = x.shape[0]
    nq, nkv = ATT_HEADS * ATT_HD, 2 * ATT_KV * ATT_HD
    return pl.pallas_call(
        _qkv_kernel,
        grid=(T // tm,),
        in_specs=[pl.BlockSpec((tm, D), lambda i: (i, 0)),
                  pl.BlockSpec((1, D), lambda i: (0, 0)),
                  _mod_spec(0, tm), _mod_spec(1, tm),
                  _const_spec(w_qkv, 1)],
        out_specs=[pl.BlockSpec((tm, nq), lambda i: (i, 0)),
                   pl.BlockSpec((tm, nkv), lambda i: (i, 0))],
        out_shape=[jax.ShapeDtypeStruct((T, nq), BF16), jax.ShapeDtypeStruct((T, nkv), F32)],
        compiler_params=_cparams("parallel"),
        name="qkv_proj",
    )(x, g, mods, mods, w_qkv)


def _rope_tables(L):
    half = ATT_HD // 2
    pos_r = (np.arange(L) // GRID_W).astype(np.float32)
    pos_c = (np.arange(L) % GRID_W).astype(np.float32)
    inv = (ROPE_BASE ** (-np.arange(0, half, 2, dtype=np.float32) / half)).astype(np.float32)
    cos = np.zeros((L, ATT_HD), np.float64)
    sin = np.zeros((L, ATT_HD), np.float64)
    for base, pos in ((0, pos_r), (half, pos_c)):
        ang = (pos[:, None] * inv[None, :]).astype(np.float32).astype(np.float64)
        cos[:, base:base + half] = np.concatenate([np.cos(ang), np.cos(ang)], axis=1)
        sin[:, base:base + half] = np.concatenate([-np.sin(ang), np.sin(ang)], axis=1)
    return (np.tile(cos, (1, 4)).astype(np.float32), np.tile(sin, (1, 4)).astype(np.float32))


def _seg_rms(x):
    w = x.shape[1]
    ri = lax.broadcasted_iota(jnp.int32, (w, w), 0) // ATT_HD
    ci = lax.broadcasted_iota(jnp.int32, (w, w), 1) // ATT_HD
    ss = _bdot(x * x, (ri == ci).astype(F32))
    return x * lax.rsqrt(ss * (1.0 / ATT_HD) + EPS)


LOG2E = 1.4426950408889634


def _exp2_bf16(x):
    return jnp.exp2(x.astype(BF16))


def _both_halves(tile, low):
    lane = lax.broadcasted_iota(jnp.int32, tile.shape, 1)
    other = pltpu.roll(tile, ATT_HD, 1)
    return jnp.where((lane < ATT_HD) == low, tile, other)


def _swap16(x):
    w = x.shape[1]
    lane = lax.broadcasted_iota(jnp.int32, x.shape, 1)
    return jnp.where(lane % 32 < 16, pltpu.roll(x, w - 16, 1), pltpu.roll(x, 16, 1))


def _attn_kernel(*refs, latent, tq):
    if latent:
        (q_ref, kv_ref, ck_ref, cv_ref, qg_ref, kg_ref, sink_ref, cosq_ref, sinq_ref, cosk_ref, sink_t_ref,
         o_ref, kk_scr, vt_scr, ckk_scr, cvt_scr) = refs
    else:
        q_ref, kv_ref, qg_ref, kg_ref, sink_ref, o_ref, ko_ref, vo_ref, kk_scr, vt_scr = refs
    L = kv_ref.shape[0]
    gw = ATT_KV * ATT_HD
    pw = 2 * ATT_HD
    qb = pl.program_id(1)

    vrows = vt_scr.shape[2]
    nblk = L // pw

    def vt_aug(tile, low):
        vt = tile.T[0:ATT_HD, :] if low else tile.T[ATT_HD:, :]
        return jnp.concatenate([vt, jnp.ones((vrows - ATT_HD, tile.shape[0]), F32)], axis=0).astype(BF16)

    @pl.when(qb == 0)
    def _():
        kn = _seg_rms(kv_ref[:, :gw]) * kg_ref[...]
        v = kv_ref[:, gw:]
        if latent:
            kn = kn * cosk_ref[...] + _swap16(kn) * sink_t_ref[...]
        else:
            for c in range(ATT_KV):
                ko_ref[c] = kn[:, c * ATT_HD:(c + 1) * ATT_HD]
                vo_ref[c] = v[:, c * ATT_HD:(c + 1) * ATT_HD]
        for c in range(ATT_KV):
            tile, low = slice((c // 2) * pw, (c // 2 + 1) * pw), c % 2 == 0
            kk_scr[c] = _both_halves(kn[:, tile], low).astype(BF16)
            for j in range(nblk):
                vt_scr[c, j] = vt_aug(v[j * pw:(j + 1) * pw, tile], low)
            if latent:
                ck, cv = ck_ref[c], cv_ref[c]
                ckk_scr[c] = jnp.concatenate([ck, ck], axis=1).astype(BF16)
                cvt_scr[c] = vt_aug(jnp.concatenate([cv, cv], axis=1), True)

    if latent:
        span = tq + 2 * WINDOW
        start = pl.multiple_of(jnp.clip(qb * tq - WINDOW, 0, L - span), WINDOW)
        blk0 = start // pw
        s_pos = start + lax.broadcasted_iota(jnp.int32, (span, tq), 0)
        t_pos = qb * tq + lax.broadcasted_iota(jnp.int32, (span, tq), 1)
        win_bias = jnp.where(jnp.abs(t_pos - s_pos) <= WINDOW, 0.0, NEG)
    else:
        span, blk0 = L, 0

    nt = (((1,), (1,)), ((), ()))
    low_q = lax.broadcasted_iota(jnp.int32, (tq, pw), 1) < ATT_HD
    def group_scores(c):
        qc = _seg_rms(q_ref[:, c * gw:(c + 1) * gw].astype(F32)) * qg_ref[...]
        if latent:
            qc = qc * cosq_ref[...] + _swap16(qc) * sinq_ref[...]
            kw = kk_scr[c, pl.ds(start, span), :]
        else:
            kw = kk_scr[c]
        qc = qc * (LOG2E / math.sqrt(ATT_HD))
        scores = []
        for g in range(ATT_KV):
            qt = qc[:, (g // 2) * pw:(g // 2 + 1) * pw]
            qm = jnp.where(low_q if g % 2 == 0 else ~low_q, qt, 0.0).astype(BF16)
            lw = lax.dot_general(kw, qm, nt, preferred_element_type=F32)
            lc = lax.dot_general(ckk_scr[c], qm, nt, preferred_element_type=F32) if latent else None
            scores.append((lw, lc))
        return scores

    def group_outputs(c, scores):
        vw = jnp.concatenate([vt_scr[c, blk0 + j] for j in range(span // pw)], axis=1)
        outs = []
        for g in range(ATT_KV):
            head = c * ATT_KV + g
            sink = sink_ref[:, head:head + 1] * LOG2E
            lw, lc = scores[g]
            if latent:
                lw = lw + win_bias
                mx = jnp.maximum(jnp.maximum(jnp.max(lw, axis=0, keepdims=True),
                                             jnp.max(lc, axis=0, keepdims=True)), sink)
                r = jnp.dot(vw, _exp2_bf16(lw - mx), preferred_element_type=F32) + jnp.dot(
                    cvt_scr[c], _exp2_bf16(lc - mx), preferred_element_type=F32)
            else:
                mx = jnp.maximum(jnp.max(lw, axis=0, keepdims=True), sink)
                r = jnp.dot(vw, _exp2_bf16(lw - mx), preferred_element_type=F32)
            den = r[ATT_HD:ATT_HD + 1, :] + jnp.exp2(sink - mx)
            outs.append(r[0:ATT_HD, :] / den)
        for t in range(2):
            o_ref[:, c * gw + t * pw:c * gw + (t + 1) * pw] = (
                jnp.concatenate(outs[2 * t:2 * t + 2], axis=0).T.astype(o_ref.dtype))

    scores = group_scores(0)
    for c in range(ATT_KV):
        nxt = group_scores(c + 1) if c + 1 < ATT_KV else None
        group_outputs(c, scores)
        scores = nxt


def attention(q, kv, row0, q_g, k_g, sink, B, L, cache=None, tq=256):
    latent = cache is not None
    gw = ATT_KV * ATT_HD
    qg = jnp.tile(q_g, ATT_KV)[None]
    kg = jnp.tile(k_g, ATT_KV)[None]
    nq = L // tq
    const = lambda a: pl.BlockSpec(a.shape, lambda b, i: (0,) * a.ndim)
    in_specs = [pl.BlockSpec((tq, ATT_HEADS * ATT_HD), lambda b, i: (row0 // tq + b * nq + i, 0)),
                pl.BlockSpec((L, 2 * gw), lambda b, i: (row0 // L + b, 0))]
    args = [q, kv]
    vrows = ATT_HD + 16
    scratch = [pltpu.VMEM((ATT_KV, L, 2 * ATT_HD), BF16),
               pltpu.VMEM((ATT_KV, L // (2 * ATT_HD), vrows, 2 * ATT_HD), BF16)]
    out_specs = [pl.BlockSpec((tq, ATT_HEADS * ATT_HD), lambda b, i: (b * nq + i, 0))]
    out_shape = [jax.ShapeDtypeStruct((B * L, ATT_HEADS * ATT_HD), BF16)]
    if latent:
        ck, cv = cache
        P = ck.shape[2]
        cos, sin = (jnp.asarray(t) for t in _rope_tables(L))
        in_specs += [pl.BlockSpec((None, ATT_KV, P, ATT_HD), lambda b, i: (b, 0, 0, 0))] * 2
        args += [ck, cv]
        in_specs += [const(qg), const(kg), pl.BlockSpec((1, ATT_HEADS), lambda b, i: (0, 0)),
                     pl.BlockSpec((tq, gw), lambda b, i: (i, 0)), pl.BlockSpec((tq, gw), lambda b, i: (i, 0)),
                     const(cos), const(sin)]
        args += [qg, kg, sink[None], cos, sin, cos, sin]
        scratch += [pltpu.VMEM((ATT_KV, P, 2 * ATT_HD), BF16), pltpu.VMEM((ATT_KV, vrows, P), BF16)]
    else:
        in_specs += [const(qg), const(kg), pl.BlockSpec((1, ATT_HEADS), lambda b, i: (0, 0))]
        args += [qg, kg, sink[None]]
        cache_spec = pl.BlockSpec((None, ATT_KV, L, ATT_HD), lambda b, i: (b, 0, 0, 0))
        out_specs += [cache_spec, cache_spec]
        out_shape += [jax.ShapeDtypeStruct((B, ATT_KV, L, ATT_HD), F32)] * 2
    outs = pl.pallas_call(
        functools.partial(_attn_kernel, latent=latent, tq=tq),
        grid=(B, nq),
        in_specs=in_specs, out_specs=out_specs, out_shape=out_shape,
        scratch_shapes=scratch,
        compiler_params=_cparams("parallel", "arbitrary"),
        name="attn_latent" if latent else "attn_context",
    )(*args)
    return outs[0] if latent else outs


MOE_TM = 1024
MOE_TOK = 1024
RUN_ALIGN = 16
MOE_LOCAL = 2 * MOE_TOK + N_EXPERTS * RUN_ALIGN
MOE_MAX_TILES = (2 * 16384 + (16384 // MOE_TOK) * N_EXPERTS * (RUN_ALIGN - 1)) // MOE_TM + N_EXPERTS + 1
RUN_SIZES = tuple(RUN_ALIGN << b for b in range(7, -1, -1))
MOE_CHUNKS = 11
MOE_MAX_ITEMS = N_EXPERTS * MOE_CHUNKS + MOE_MAX_TILES - N_EXPERTS
ITEM_FULL, ITEM_DEAD, ITEM_NONE = -1, -2, -3


def _router_kernel(*refs, proj):
    n = _n_proj_refs(**proj)
    g_ref, sh_ref, sc_ref, wr_ref, br_ref, tri_ref, x_ref, lp_ref, wts_ref, runs_ref, cnt_ref = refs[n:]

    @pl.when(pl.program_id(0) == 0)
    def _():
        cnt_ref[...] = jnp.zeros_like(cnt_ref)

    x = _proj_value(refs[:n], pl.program_id(0) < proj["n_p"], **proj)
    x_ref[...] = x
    h = _norm_mod(x, g_ref[...], sh_ref[...], sc_ref[...])
    nt = (((1,), (1,)), ((), ()))
    h_hi = h.astype(BF16)
    h_lo = (h - h_hi.astype(F32)).astype(BF16)
    w_hi = wr_ref[...].astype(BF16)
    w_lo = (wr_ref[...] - w_hi.astype(F32)).astype(BF16)
    lg = (lax.dot_general(w_hi, h_hi, nt, preferred_element_type=F32)
          + lax.dot_general(w_hi, h_lo, nt, preferred_element_type=F32)
          + lax.dot_general(w_lo, h_hi, nt, preferred_element_type=F32)) + br_ref[...]
    row = lax.broadcasted_iota(jnp.int32, lg.shape, 0)
    m1 = jnp.max(lg, axis=0, keepdims=True)
    i1 = jnp.min(jnp.where(lg == m1, row, N_EXPERTS), axis=0, keepdims=True)
    l2 = jnp.where(row == i1, -jnp.inf, lg)
    m2 = jnp.max(l2, axis=0, keepdims=True)
    i2 = jnp.min(jnp.where(l2 == m2, row, N_EXPERTS), axis=0, keepdims=True)
    e2 = jnp.exp(m2 - m1)
    w1 = 1.0 / (1.0 + e2)
    wts_ref[...] = jnp.concatenate([w1, e2 * w1], axis=0)
    oh1 = (row == i1).astype(F32)
    oh2 = (row == i2).astype(F32)
    cs1 = _bdot(oh1, tri_ref[...])
    cs2 = _bdot(oh2, tri_ref[...])
    tot1 = jnp.sum(oh1, axis=1, keepdims=True)
    run = jnp.ceil((tot1 + jnp.sum(oh2, axis=1, keepdims=True)) * (1.0 / RUN_ALIGN)) * RUN_ALIGN
    run_b = jnp.broadcast_to(run, (N_EXPERTS, 128))
    er = lax.broadcasted_iota(jnp.int32, (N_EXPERTS, N_EXPERTS), 0)
    ec = lax.broadcasted_iota(jnp.int32, (N_EXPERTS, N_EXPERTS), 1)
    start = jnp.dot((ec < er).astype(F32), run_b, precision=HIGHEST, preferred_element_type=F32)
    last = lax.broadcasted_iota(jnp.int32, (N_EXPERTS, 128), 0) == N_EXPERTS - 1
    run_b = jnp.where(last, MOE_LOCAL - start, run_b)
    st = start[:, 0:1]
    p1 = jnp.sum(oh1 * (st + cs1), axis=0, keepdims=True)
    p2 = jnp.sum(oh2 * (st + tot1 + cs2), axis=0, keepdims=True)
    lp_ref[...] = jnp.concatenate([p1, p2], axis=0).astype(jnp.int32)
    lane = lax.broadcasted_iota(jnp.int32, (N_EXPERTS, 128), 1)
    runs_ref[...] = jnp.where(lane == 0, run_b, jnp.where(lane == 1, start, cnt_ref[...]))
    cnt_ref[...] = cnt_ref[...] + run_b


def moe_router(acts, w_out, x, g, mods, w_router, b_router, tm=MOE_TOK):
    T = x.shape[0]
    tri = jnp.asarray(np.triu(np.ones((tm, tm), np.float32), k=1)).astype(BF16)
    tok2 = lambda dt: jax.ShapeDtypeStruct((2, T), dt)
    p_specs, p_args, proj = _proj_inputs(acts, w_out, x, mods, tm)
    return pl.pallas_call(
        functools.partial(_router_kernel, proj=proj),
        grid=(T // tm,),
        in_specs=p_specs + [
                  pl.BlockSpec((1, D), lambda i: (0, 0)),
                  _mod_spec(3, tm), _mod_spec(4, tm),
                  pl.BlockSpec((N_EXPERTS, D), lambda i: (0, 0)),
                  pl.BlockSpec((N_EXPERTS, 1), lambda i: (0, 0)),
                  _const_spec(tri, 1)],
        out_specs=[pl.BlockSpec((tm, D), lambda i: (i, 0)),
                   pl.BlockSpec((2, tm), lambda i: (0, i)),
                   pl.BlockSpec((2, tm), lambda i: (0, i)),
                   pl.BlockSpec((None, N_EXPERTS, 128), lambda i: (i, 0, 0)),
                   pl.BlockSpec((N_EXPERTS, 128), lambda i: (0, 0))],
        out_shape=[jax.ShapeDtypeStruct((T, D), F32), tok2(jnp.int32), tok2(F32),
                   jax.ShapeDtypeStruct((T // tm, N_EXPERTS, 128), F32),
                   jax.ShapeDtypeStruct((N_EXPERTS, 128), F32)],
        compiler_params=_cparams("arbitrary"),
        name="moe_router",
    )(*p_args, g, mods, mods, w_router.T, b_router[:, None], tri)


def moe_layout(runs, totals):
    rows = totals[:, 0].astype(jnp.int32)
    tiles = (rows + MOE_TM - 1) // MOE_TM
    tile_end = jnp.cumsum(tiles)
    group = (tile_end - tiles) * MOE_TM
    run_len = runs[:, :, 0].astype(jnp.int32)
    run_src = runs[:, :, 1].astype(jnp.int32)
    run_dst = group[None, :] + runs[:, :, 2].astype(jnp.int32)
    tail = jnp.stack([group + rows, tiles * MOE_TM - rows]).astype(jnp.int32)
    n_tiles = tile_end[-1]
    t = jnp.arange(MOE_MAX_TILES, dtype=jnp.int32)
    tile_e = jnp.sum(t[:, None] >= tile_end[None, :], axis=1).astype(jnp.int32)
    last_e = jnp.sum((n_tiles - 1) >= tile_end).astype(jnp.int32)
    tile_e = jnp.where(t < n_tiles, tile_e, last_e)
    first = jnp.sum(jnp.where(tile_e[:, None] == jnp.arange(N_EXPERTS), (tile_end - tiles)[None, :], 0), axis=1)
    e_rows = jnp.sum(jnp.where(tile_e[:, None] == jnp.arange(N_EXPERTS), rows[None, :], 0), axis=1)
    tile_rows = jnp.where(t < n_tiles, jnp.clip(e_rows - (t - first) * MOE_TM, 0, MOE_TM), 0).astype(jnp.int32)
    run_tab = jnp.stack([run_len, run_src, run_dst]).reshape(3, -1)
    live = t < n_tiles
    is_first = (t == first) & live
    n_items = jnp.where(is_first, MOE_CHUNKS, 1)
    item_end = jnp.cumsum(n_items)
    j = jnp.arange(MOE_MAX_ITEMS, dtype=jnp.int32)
    it_tile = jnp.minimum(jnp.sum(j[:, None] >= item_end[None, :], axis=1), MOE_MAX_TILES - 1).astype(jnp.int32)
    chunk = j - jnp.take(item_end - n_items, it_tile)
    kind = jnp.where(jnp.take(is_first, it_tile), chunk, jnp.where(jnp.take(live, it_tile), ITEM_FULL, ITEM_DEAD))
    kind = jnp.where(j >= item_end[-1], ITEM_NONE, kind)
    items = jnp.stack([it_tile, kind.astype(jnp.int32)])
    return run_tab, tail, tile_e, n_tiles.astype(jnp.int32).reshape(1), tile_rows, items


def _run_copies(tab_ref, i, local_ref, global_ref, sem, to_global):
    out = []
    for e in range(N_EXPERTS):
        k = i * N_EXPERTS + e
        n, src, dst = tab_ref[0, k], tab_ref[1, k], tab_ref[2, k]
        for size in RUN_SIZES:
            done = (n // (2 * size)) * (2 * size)
            loc = local_ref.at[pl.ds(pl.multiple_of(src + done, RUN_ALIGN), size), :]
            glo = global_ref.at[pl.ds(pl.multiple_of(dst + done, RUN_ALIGN), size), :]
            copy = pltpu.make_async_copy(loc, glo, sem) if to_global else pltpu.make_async_copy(glo, loc, sem)
            out.append(((n & size) != 0, copy))
    return out


def _start(copies, live=True):
    for pred, copy in copies:
        pl.when(pred & live)(copy.start)


def _wait(copies, live=True):
    for pred, copy in copies:
        pl.when(pred & live)(copy.wait)


def _start_then_wait(copies):
    _start(copies)
    _wait(copies)


def _dispatch_kernel(tab_ref, tail_ref, nt_ref, lp_ref, x_ref, g_ref, sh_ref, sc_ref, xs_ref, hs_scr, z_scr, sem):
    i = pl.program_id(0)
    tm = x_ref.shape[0]
    buf = i % 2
    tg = 256
    slot = lax.broadcasted_iota(jnp.int32, (MOE_LOCAL, tg), 0)
    acc = None
    for k in range(tm // tg):
        sl = slice(k * tg, (k + 1) * tg)
        h = _norm_mod(x_ref[sl, :], g_ref[...], sh_ref[...], sc_ref[...]).astype(BF16)
        perm = jnp.where((slot == lp_ref[0:1, sl]) | (slot == lp_ref[1:2, sl]), 1.0, 0.0).astype(BF16)
        part = jnp.dot(perm, h, preferred_element_type=F32)
        acc = part if acc is None else acc + part
    hs_scr[buf] = acc.astype(BF16)
    copies = _run_copies(tab_ref, i, hs_scr.at[buf], xs_ref, sem.at[buf], to_global=True)
    _start(copies)
    _wait(_run_copies(tab_ref, jnp.maximum(i - 1, 0), hs_scr.at[1 - buf], xs_ref, sem.at[1 - buf], to_global=True),
          live=i > 0)

    @pl.when(i == 0)
    def _():
        z_scr[...] = jnp.zeros_like(z_scr)
        zrows = z_scr.shape[0]

        def zero_tile(t, carry):
            for part in range(MOE_TM // zrows):
                dst = xs_ref.at[pl.ds(pl.multiple_of(t * MOE_TM + part * zrows, zrows), zrows), :]
                copy = pltpu.make_async_copy(z_scr, dst, sem.at[2])
                copy.start()
                copy.wait()
            return carry

        lax.fori_loop(nt_ref[0], MOE_MAX_TILES, zero_tile, 0)
        tails = []
        for e in range(N_EXPERTS):
            start, n = tail_ref[0, e], tail_ref[1, e]
            for size in RUN_SIZES:
                if size >= MOE_TM:
                    continue
                done = (n // (2 * size)) * (2 * size)
                dst = xs_ref.at[pl.ds(pl.multiple_of(start + done, RUN_ALIGN), size), :]
                tails.append(((n & size) != 0, pltpu.make_async_copy(z_scr.at[pl.ds(0, size), :], dst, sem.at[2])))
        _start_then_wait(tails)

    _wait(copies, live=i == pl.num_programs(0) - 1)


def moe_dispatch(x, g, mods, lp, run_tab, tail, n_tiles, tm=MOE_TOK):
    T = x.shape[0]
    n_rows = MOE_MAX_TILES * MOE_TM
    return pl.pallas_call(
        _dispatch_kernel,
        grid_spec=pltpu.PrefetchScalarGridSpec(
            num_scalar_prefetch=3,
            grid=(T // tm,),
            in_specs=[pl.BlockSpec((2, tm), lambda i, *_: (0, i)),
                      pl.BlockSpec((tm, D), lambda i, *_: (i, 0)),
                      pl.BlockSpec((1, D), lambda i, *_: (0, 0)),
                      _mod_spec(3, tm), _mod_spec(4, tm)],
            out_specs=pl.BlockSpec(memory_space=pl.ANY),
            scratch_shapes=[pltpu.VMEM((2, MOE_LOCAL, D), BF16), pltpu.VMEM((MOE_TM // 2, D), BF16),
                            pltpu.SemaphoreType.DMA((3,))]),
        out_shape=jax.ShapeDtypeStruct((n_rows, D), BF16),
        compiler_params=_cparams("arbitrary"),
        name="moe_dispatch",
    )(run_tab, tail, n_tiles, lp, x, g, mods, mods)


def _moe_group_kernel(it_ref, te_ref, nt_ref, tr_ref, x_ref, w1_ref, w3_ref, w2_ref, o_ref, acc_scr, b1_scr, b3_scr,
                      b2_scr):
    j = pl.program_id(0)
    kind = it_ref[1, j]
    rows = tr_ref[it_ref[0, j]]
    half = MOE_TM // 2

    def chunk(c):
        @pl.when(rows > half)
        def _():
            _swiglu_accumulate(x_ref, acc_scr, b1_scr.at[c], b3_scr.at[c], b2_scr.at[c], MOE_TM)

        @pl.when(rows <= half)
        def _():
            _swiglu_accumulate(x_ref, acc_scr, b1_scr.at[c], b3_scr.at[c], b2_scr.at[c], half)

    @pl.when((kind == 0) | (kind == ITEM_FULL))
    def _():
        acc_scr[...] = jnp.zeros_like(acc_scr)

    @pl.when(kind >= 0)
    def _():
        b1_scr[kind] = w1_ref[...].astype(BF16)
        b3_scr[kind] = w3_ref[...].astype(BF16)
        b2_scr[kind] = w2_ref[...].astype(BF16)
        chunk(kind)

    for n_rows, pred in ((MOE_TM, rows > half), (half, rows <= half)):
        @pl.when((kind == ITEM_FULL) & pred)
        def _():
            for c in range(MOE_CHUNKS):
                _swiglu_accumulate(x_ref, acc_scr, b1_scr.at[c], b3_scr.at[c], b2_scr.at[c], n_rows)

    @pl.when((kind == MOE_CHUNKS - 1) | (kind == ITEM_FULL))
    def _():
        o_ref[...] = acc_scr[...].astype(o_ref.dtype)

    @pl.when(kind == ITEM_DEAD)
    def _():
        o_ref[...] = jnp.zeros_like(o_ref)


def moe_grouped_swiglu(xs, items, tile_e, n_tiles, tile_rows, w1, w3, w2):
    tf = D_FF // MOE_CHUNKS
    tile = lambda j, it, te, nt, tr: it[0, j]
    row_in = lambda j, it, te, nt, tr: (jnp.minimum(tile(j, it, te, nt, tr), jnp.maximum(nt[0] - 1, 0)), 0)
    wchunk = lambda j, it: jnp.where(it[1, j] >= 0, it[1, j], MOE_CHUNKS - 1)
    wcol = lambda j, it, te, nt, tr: (te[it[0, j]], 0, wchunk(j, it))
    wrow = lambda j, it, te, nt, tr: (te[it[0, j]], wchunk(j, it), 0)
    return pl.pallas_call(
        _moe_group_kernel,
        grid_spec=pltpu.PrefetchScalarGridSpec(
            num_scalar_prefetch=4,
            grid=(MOE_MAX_ITEMS,),
            in_specs=[pl.BlockSpec((MOE_TM, D), row_in),
                      pl.BlockSpec((None, D, tf), wcol),
                      pl.BlockSpec((None, D, tf), wcol),
                      pl.BlockSpec((None, tf, D), wrow)],
            out_specs=pl.BlockSpec((MOE_TM, D), lambda j, it, te, nt, tr: (it[0, j], 0)),
            scratch_shapes=[pltpu.VMEM((MOE_TM, D), F32), pltpu.VMEM((MOE_CHUNKS, D, tf), BF16),
                            pltpu.VMEM((MOE_CHUNKS, D, tf), BF16), pltpu.VMEM((MOE_CHUNKS, tf, D), BF16)]),
        out_shape=jax.ShapeDtypeStruct(xs.shape, BF16),
        compiler_params=_cparams("arbitrary"),
        name="moe_grouped",
    )(items, tile_e, n_tiles, tile_rows, xs, w1, w3, w2)


def _combine_kernel(tab_ref, lp_ref, wt_ref, x_ref, gate_ref, ys_ref, op_ref, os_ref, yl_scr, sem, *, n_p):
    i = pl.program_id(0)
    tm = x_ref.shape[0]
    buf = i % 2
    last = pl.num_programs(0) - 1
    gather = lambda t, b: _run_copies(tab_ref, t, yl_scr.at[b], ys_ref, sem.at[b], to_global=False)
    _start(gather(i, buf), live=i == 0)
    _start(gather(jnp.minimum(i + 1, last), 1 - buf), live=i < last)
    _wait(gather(i, buf))
    bounds = (0, 512, 1024, 1536, MOE_LOCAL)
    moe = None
    for lo, hi in zip(bounds[:-1], bounds[1:]):
        slot = lo + lax.broadcasted_iota(jnp.int32, (tm, hi - lo), 1)
        mix = (jnp.where(slot == lp_ref[:, 0:1], wt_ref[:, 0:1], 0.0)
               + jnp.where(slot == lp_ref[:, 1:2], wt_ref[:, 1:2], 0.0)).astype(BF16)
        part = jnp.dot(mix, yl_scr[buf, lo:hi, :], preferred_element_type=F32)
        moe = part if moe is None else moe + part
    out = x_ref[...] + gate_ref[...] * moe

    @pl.when(pl.program_id(0) < n_p)
    def _():
        op_ref[...] = out

    @pl.when(pl.program_id(0) >= n_p)
    def _():
        os_ref[...] = out


def moe_combine(x, mods, lp, wts, run_tab, ys, t_prompt, tm=MOE_TOK):
    T = x.shape[0]
    n_p = t_prompt // tm
    return pl.pallas_call(
        functools.partial(_combine_kernel, n_p=n_p),
        grid_spec=pltpu.PrefetchScalarGridSpec(
            num_scalar_prefetch=1,
            grid=(T // tm,),
            in_specs=[pl.BlockSpec((tm, 2), lambda i, *_: (i, 0)),
                      pl.BlockSpec((tm, 2), lambda i, *_: (i, 0)),
                      pl.BlockSpec((tm, D), lambda i, *_: (i, 0)),
                      _mod_spec(5, tm),
                      pl.BlockSpec(memory_space=pl.ANY)],
            out_specs=_part_specs((tm, D), n_p),
            scratch_shapes=[pltpu.VMEM((2, MOE_LOCAL, D), BF16), pltpu.SemaphoreType.DMA((2,))]),
        out_shape=[jax.ShapeDtypeStruct((t_prompt, D), F32), jax.ShapeDtypeStruct((T - t_prompt, D), F32)],
        compiler_params=_cparams("arbitrary"),
        name="moe_combine",
    )(run_tab, lp.T, wts.T, x, mods, ys)


def kernel(x_prompt, x_sample, state_C, state_n, state_m, cache_k, cache_v, c, c_ctx, norm1_g, norm2_g, w_ada, b_ada, ev_w_in, ev_conv, hy_w1, hy_b1, hy_w2, hy_b2, hy_w3, hy_freq, hy_d, ml_b_gate, ml_norm_g, ev_w_out, ff_w1, ff_w3, ff_w2, at_w_qkv, at_q_g, at_k_g, at_sink, at_w_out, moe_w_router, moe_b_router, moe_w1, moe_w3, moe_w2):
    BP, LP, _ = x_prompt.shape
    BS, LS, _ = x_sample.shape
    TP = BP * LP
    assert TP % GROUP == 0 and TP // GROUP == N_PROMPT_GROUPS and LS == GROUP and BS == 8

    xp, xs = x_prompt.reshape(TP, D), x_sample.reshape(BS * LS, D)
    cond = jnp.concatenate([c_ctx[None], c, jnp.zeros((16 - 1 - BS, D), F32)], axis=0)
    mods = adaln_table(cond, w_ada, b_ada)

    u, gates = even_in_proj(xp, xs, norm1_g[0:1], mods[0], ev_w_in[0], ml_b_gate[0].reshape(1, N_GATES))
    hy = []
    for seq0, B, L, nb in ((0, BP, LP, 4), (TP // LS, BS, LS, 1)):
        fwd, inv = (jnp.asarray(t).astype(BF16) for t in _dft_tables(L))
        ka, kb = hyena_filter_spectra(L, hy_w1[0], hy_b1[0], hy_w2[0], hy_b2[0], hy_w3[0], hy_freq[0], fwd)
        hy.append(hyena_mix(u, seq0, B, L, ev_conv[0], hy_d[0], fwd, inv, ka, kb, nb))
    ml_p, new_C, new_n, new_m = mlstm_mix(u, gates, 0, BP, LP, ml_norm_g[0], want_state=True)
    ml_s = mlstm_mix(u, gates, TP // LS, BS, LS, ml_norm_g[0],
                     state=(state_C[:, 0], state_n[:, 0], state_m[:, 0]))
    x = ffn_residual([hy, (ml_p, ml_s)], ev_w_out[0], (xp, xs), norm2_g[0:1], mods[0], ff_w1[0], ff_w3[0], ff_w2[0])

    q, kv = qkv_proj(x, norm1_g[1:2], mods[1], at_w_qkv[0])
    o_p, new_k, new_v = attention(q, kv, 0, at_q_g[0], at_k_g[0], at_sink[0], BP, LP)
    o_s = attention(q, kv, TP, at_q_g[0], at_k_g[0], at_sink[0], BS, LS, cache=(cache_k[:, 0], cache_v[:, 0]))
    x, lp, wts, runs, totals = moe_router([(o_p, o_s)], at_w_out[0], x, norm2_g[1:2], mods[1],
                                          moe_w_router[0], moe_b_router[0])
    run_tab, tail, tile_e, n_tiles, tile_rows, items = moe_layout(runs, totals)
    xsort = moe_dispatch(x, norm2_g[1:2], mods[1], lp, run_tab, tail, n_tiles)
    ysort = moe_grouped_swiglu(xsort, items, tile_e, n_tiles, tile_rows, moe_w1[0], moe_w3[0], moe_w2[0])
    yp, ys = moe_combine(x, mods[1], lp, wts, run_tab, ysort, TP)

    return (yp.reshape(BP, LP, D), ys.reshape(BS, LS, D),
            new_C[:, None], new_n[:, None], new_m[:, None], new_k[:, None], new_v[:, None])
```

```python
import functools
import math

import numpy as np
import jax
import jax.numpy as jnp
from jax import lax
from jax.experimental import pallas as pl
from jax.experimental.pallas import tpu as pltpu

F32 = jnp.float32
BF16 = jnp.bfloat16
HIGHEST = lax.Precision.HIGHEST

D = 1024
GROUP = 1024
N_PROMPT_GROUPS = 8
HY_W = 512
ML_HEADS = 4
ML_HD = 128
ML_CHUNK = 256
EVEN_MAIN = 3 * HY_W + 4 * 512
N_GATES = 16
ATT_HD = 64
ATT_HEADS = 16
ATT_KV = 4
WINDOW = 128
GRID_W = 64
ROPE_BASE = 10000.0
D_FF = 2816
N_EXPERTS = 8
EPS = 1e-6
NEG = -1e30
VMEM_LIMIT = 56 * 1024 * 1024


def _cparams(*sem, flags=None):
    return pltpu.CompilerParams(dimension_semantics=sem, vmem_limit_bytes=VMEM_LIMIT, flags=flags)


def _mod_row(i, tm):
    return jnp.maximum(i * tm // GROUP - (N_PROMPT_GROUPS - 1), 0)


def _silu(x):
    return x * jax.nn.sigmoid(x)


def _bdot(a, b):
    return jnp.dot(a.astype(BF16), b.astype(BF16), preferred_element_type=F32)


def _norm_mod(x, g, sh, sc):
    y = x * lax.rsqrt(jnp.mean(x * x, axis=-1, keepdims=True) + EPS) * g
    return y * (1.0 + sc) + sh


def _adaln_kernel(c_ref, w_ref, b_ref, o_ref):
    s = _silu(c_ref[...])
    s_hi = s.astype(BF16)
    s_lo = (s - s_hi.astype(F32)).astype(BF16)
    w_hi = w_ref[...].astype(BF16)
    w_lo = (w_ref[...] - w_hi.astype(F32)).astype(BF16)
    dot = functools.partial(jnp.dot, preferred_element_type=F32)
    o_ref[...] = dot(s_hi, w_hi) + dot(s_lo, w_hi) + dot(s_hi, w_lo) + b_ref[...]


def adaln_table(cond, w_ada, b_ada):
    depth = w_ada.shape[0]
    tn = 1536
    out = pl.pallas_call(
        _adaln_kernel,
        grid=(depth, 6 * D // tn),
        in_specs=[pl.BlockSpec((16, D), lambda l, j: (0, 0)),
                  pl.BlockSpec((None, D, tn), lambda l, j: (l, 0, j)),
                  pl.BlockSpec((None, 1, tn), lambda l, j: (l, 0, j))],
        out_specs=pl.BlockSpec((None, 16, tn), lambda l, j: (l, 0, j)),
        out_shape=jax.ShapeDtypeStruct((depth, 16, 6 * D), F32),
        compiler_params=_cparams("parallel", "parallel"),
        name="adaln",
    )(cond, w_ada, b_ada.reshape(depth, 1, 6 * D))
    return out.reshape(depth, 16, 1, 6 * D)


def _same_tile(i):
    return i


def _part_specs(block, n_p, tile_of=_same_tile):
    return [pl.BlockSpec(block, lambda i, *_: (jnp.minimum(tile_of(i), n_p - 1), 0)),
            pl.BlockSpec(block, lambda i, *_: (jnp.maximum(tile_of(i) - n_p, 0), 0))]


def _pick(is_prompt, p_ref, s_ref):
    return jnp.where(is_prompt, p_ref[...], s_ref[...])


def _mod_spec(k, tm, tile_of=_same_tile):
    return pl.BlockSpec((None, 1, D), lambda i, *_: (_mod_row(tile_of(i), tm), 0, k))


def _log_sigmoid(x):
    return jnp.minimum(x, 0.0) - jnp.log(1.0 + jnp.exp(-jnp.abs(x)))


def _split3(x):
    hi = x.astype(BF16)
    r = x - hi.astype(F32)
    mid = r.astype(BF16)
    return hi, mid, (r - mid.astype(F32)).astype(BF16)


def _even_in_kernel(xp_ref, xs_ref, g_ref, sh_ref, sc_ref, w_ref, bg_ref, lo_ref, up_ref, u_ref, gate_ref, *, n_p, tn):
    is_prompt = pl.program_id(0) < n_p
    h = _norm_mod(_pick(is_prompt, xp_ref, xs_ref), g_ref[...], sh_ref[...], sc_ref[...]).astype(BF16)

    def main_chunks(j0, j1):
        for j in range(j0, j1):
            u_ref[:, j * tn:(j + 1) * tn] = _bdot(h, w_ref[:, j * tn:(j + 1) * tn]).astype(u_ref.dtype)

    n_main = EVEN_MAIN // tn
    gates = _bdot(h, w_ref[:, EVEN_MAIN:]) + bg_ref[...]
    main_chunks(0, n_main // 2)
    lf = _log_sigmoid(gates)
    col = lax.broadcasted_iota(jnp.int32, (1, N_GATES), 1)
    is_forget = (col // ML_HEADS) % 2 == 1
    is_rev = col >= N_GATES // 2
    for ch in range(h.shape[0] // ML_CHUNK):
        sl = slice(ch * ML_CHUNK, (ch + 1) * ML_CHUNK)
        parts = _split3(lf[sl])
        cf = sum(jnp.dot(lo_ref[...], p, preferred_element_type=F32) for p in parts)
        cr = sum(jnp.dot(up_ref[...], p, preferred_element_type=F32) for p in parts)
        gate_ref[sl, :] = jnp.where(is_forget, jnp.where(is_rev, cr, cf), gates[sl])
    main_chunks(n_main // 2, n_main)


def even_in_proj(xp, xs, g, mods, w_in, b_gate, tm=1024, tn=512):
    T = xp.shape[0] + xs.shape[0]
    tri = np.tril(np.ones((ML_CHUNK, ML_CHUNK), np.float32))
    lo, up = jnp.asarray(tri).astype(BF16), jnp.asarray(tri.T).astype(BF16)
    return pl.pallas_call(
        functools.partial(_even_in_kernel, n_p=xp.shape[0] // tm, tn=tn),
        grid=(T // tm,),
        in_specs=_part_specs((tm, D), xp.shape[0] // tm) + [
                  pl.BlockSpec((1, D), lambda i: (0, 0)),
                  _mod_spec(0, tm), _mod_spec(1, tm),
                  _const_spec(w_in, 1),
                  pl.BlockSpec((1, N_GATES), lambda i: (0, 0)),
                  _const_spec(lo, 1), _const_spec(up, 1)],
        out_specs=[pl.BlockSpec((tm, EVEN_MAIN), lambda i: (i, 0)),
                   pl.BlockSpec((tm, N_GATES), lambda i: (i, 0))],
        out_shape=[jax.ShapeDtypeStruct((T, EVEN_MAIN), BF16),
                   jax.ShapeDtypeStruct((T, N_GATES), F32)],
        compiler_params=_cparams("parallel"),
        name="even_in_proj",
    )(xp, xs, g, mods, mods, w_in, b_gate, lo, up)


def _dft_tables(L):
    n = 2 * L
    f = np.arange(L, dtype=np.int64)[:, None]
    s = np.arange(L, dtype=np.int64)[None, :]
    ang = 2.0 * np.pi * ((f * s) % n).astype(np.float64) / n
    fwd = np.concatenate([np.cos(ang), -np.sin(ang)], axis=0)
    fwd[L, :] = np.where(np.arange(L) % 2 == 0, 1.0, -1.0)
    t = np.arange(L, dtype=np.int64)[:, None]
    ff = np.arange(L, dtype=np.int64)[None, :]
    ang = 2.0 * np.pi * ((t * ff) % n).astype(np.float64) / n
    inv_re = 2.0 * np.cos(ang) / n
    inv_re[:, 0] = 1.0 / n
    inv_im = -2.0 * np.sin(ang) / n
    inv_im[:, 0] = np.where(np.arange(L) % 2 == 0, 1.0, -1.0) / n
    inv = np.concatenate([inv_re, inv_im], axis=1)
    return fwd.astype(np.float32), inv.astype(np.float32)


def _filter_tables(L):
    t = np.linspace(0.0, 1.0, L, dtype=np.float32).astype(np.float64)[:, None]
    w = 2.0 * math.pi * np.arange(L, dtype=np.float64)[:, None] / L
    bands = np.linspace(1e-4, 16 - 1, 16, dtype=np.float32).astype(np.float64)[None, :]
    z = np.concatenate([t, np.cos(bands * w), -np.sin(bands * w)], axis=-1)
    zp = np.zeros((L, 128), np.float64)
    zp[:, :z.shape[1]] = z
    max_decay = math.log(1e-2) / 0.3
    min_decay = math.log(1e-2) / 1.5
    deltas = np.linspace(min_decay, max_decay, HY_W, dtype=np.float32).astype(np.float64)
    decay = np.exp(-t * np.abs(deltas))
    return zp.astype(np.float32), decay.astype(np.float32)


def _hy_filter_kernel(z_ref, dec_ref, w1_ref, b1_ref, w2_ref, b2_ref, w3_ref, fr_ref, fwd_ref,
                      ka_ref, kb_ref):
    L = z_ref.shape[0]
    hdot = functools.partial(jnp.dot, precision=HIGHEST, preferred_element_type=F32)
    h = jnp.sin(fr_ref[0:1, :] * (hdot(z_ref[...], w1_ref[...]) + b1_ref[...]))
    h = jnp.sin(fr_ref[1:2, :] * (hdot(h, w2_ref[...]) + b2_ref[...]))
    h = hdot(h, w3_ref[...])
    row0 = lax.broadcasted_iota(jnp.int32, (L, 1), 0) == 0
    h0 = h[:, :HY_W] * dec_ref[...]
    h1 = h[:, HY_W:] * dec_ref[...]
    l1 = jnp.sum(jnp.abs(h0), axis=0, keepdims=True) + jnp.sum(jnp.abs(h1), axis=0, keepdims=True)
    inv = 1.0 / l1
    h0 = h0 * inv
    h1 = jnp.where(row0, 0.0, h1 * inv)
    f0 = _bdot(fwd_ref[...], h0)
    f1 = _bdot(fwd_ref[...], h1)
    ka_ref[...] = f0[:L] + f1[:L]
    kb_ref[...] = jnp.where(row0, f0[L:] + f1[L:], f0[L:] - f1[L:])


def _const_spec(a, n_grid):
    return pl.BlockSpec(a.shape, lambda *_: (0,) * a.ndim, pipeline_mode=pl.Buffered(1))


def hyena_filter_spectra(L, w1, b1, w2, b2, w3, freq, fwd):
    z, dec = _filter_tables(L)
    pad2 = lambda a, r, c: jnp.pad(a, ((0, r - a.shape[0]), (0, c - a.shape[1])))
    args = (jnp.asarray(z), jnp.asarray(dec), pad2(w1, 128, 128), pad2(b1[None], 1, 128),
            pad2(w2, 128, 128), pad2(b2[None], 1, 128), pad2(w3, 128, 4 * HY_W), pad2(freq, 2, 128), fwd)
    in_specs = [_const_spec(a, 1) for a in args]
    in_specs[6] = pl.BlockSpec((128, 2 * HY_W), lambda o: (0, o))
    shp = jax.ShapeDtypeStruct((2, L, HY_W), F32)
    out_spec = pl.BlockSpec((None, L, HY_W), lambda o: (o, 0, 0))
    return pl.pallas_call(
        _hy_filter_kernel,
        grid=(2,),
        in_specs=in_specs,
        out_specs=[out_spec, out_spec],
        out_shape=[shp, shp],
        compiler_params=_cparams("arbitrary"),
        name=f"hyena_filter_{L}",
    )(*args)


def _hyena_kernel(u_ref, cw_ref, d_ref, fwd_ref, inv_ref, ka_ref, kb_ref, o_ref):
    nb, L = u_ref.shape[0], u_ref.shape[1]
    row = lax.broadcasted_iota(jnp.int32, (L, 1), 0)
    first, last = row == 0, row == L - 1
    fwd = fwd_ref[...].astype(BF16)
    inv = inv_ref[...].astype(BF16)

    def long_conv(z, o):
        zf = jnp.dot(fwd, z.astype(BF16), preferred_element_type=F32)
        a, b = zf[:L], zf[L:]
        ka, kb = ka_ref[o], kb_ref[o]
        yr = a * ka - jnp.where(first, 0.0, b * kb)
        yi = jnp.where(first, b * kb, a * kb + b * ka)
        return (jnp.dot(inv[:, :L], yr.astype(BF16), preferred_element_type=F32)
                + jnp.dot(inv[:, L:], yi.astype(BF16), preferred_element_type=F32))

    for bi in range(nb):
        u = u_ref[bi].astype(F32)
        prev = jnp.where(first, 0.0, pltpu.roll(u, 1, 0))
        nxt = jnp.where(last, 0.0, pltpu.roll(u, L - 1, 0))
        u = prev * cw_ref[0:1, :] + u * cw_ref[1:2, :] + nxt * cw_ref[2:3, :]
        v, x1, x2 = u[:, :HY_W], u[:, HY_W:2 * HY_W], u[:, 2 * HY_W:]
        z = x1 * (long_conv(v, 0) + d_ref[0:1, :] * v)
        z = x2 * (long_conv(z, 1) + d_ref[1:2, :] * z)
        o_ref[bi] = z.astype(o_ref.dtype)


def hyena_mix(u, seq0, B, L, conv_w, d_skip, fwd, inv, ka, kb, nb):
    u3 = u.reshape(-1, L, EVEN_MAIN)
    full = lambda a: _const_spec(a, 1)
    out = pl.pallas_call(
        _hyena_kernel,
        grid=(B // nb,),
        in_specs=[pl.BlockSpec((nb, L, 3 * HY_W), lambda b: (b + seq0 // nb, 0, 0)),
                  full(conv_w), full(d_skip), full(fwd), full(inv), full(ka), full(kb)],
        out_specs=pl.BlockSpec((nb, L, HY_W), lambda b: (b, 0, 0)),
        out_shape=jax.ShapeDtypeStruct((B, L, HY_W), BF16),
        compiler_params=_cparams("parallel"),
        name=f"hyena_{L}",
    )(u3, conv_w, d_skip, fwd, inv, ka, kb)
    return out.reshape(B * L, HY_W)


def _mlstm_kernel(*refs, has_state, want_state):
    q_ref, k_ref, v_ref, o_ref, gc_ref, gr_ref, ng_ref = refs[:7]
    refs = refs[7:]
    if has_state:
        c0t_ref, n0b_ref, m0_ref = refs[:3]
        refs = refs[3:]
    y_ref = refs[0]
    if want_state:
        c_out, n_out, m_out = refs[1:4]
    L, d = q_ref.shape[0], ML_HD
    T = min(ML_CHUNK, L)
    nc = L // T
    scale = 1.0 / math.sqrt(d)
    nt = (((1,), (1,)), ((), ()))
    si = lax.broadcasted_iota(jnp.int32, (T, T), 0)
    ti = lax.broadcasted_iota(jnp.int32, (T, T), 1)
    allowed = (si <= ti, si >= ti)
    chains = [(dr, h) for dr in range(2) for h in range(ML_HEADS)]
    gcol = lambda dr, gi, h: dr * 2 * ML_HEADS + gi * ML_HEADS + h

    caug_t, m = {}, {}
    for ch in chains:
        dr, h = ch
        if has_state:
            caug_t[ch] = jnp.concatenate([c0t_ref[dr, h], n0b_ref[dr, h]], axis=0)
            m[ch] = m0_ref[dr, h:h + 1, 0:1]
        else:
            caug_t[ch], m[ch] = jnp.zeros((2 * d, d), F32), jnp.zeros((1, 1), F32)

    chunk_cache = {}

    def chunk_data(h, j):
        if (h, j) not in chunk_cache:
            sl, hl = slice(j * T, (j + 1) * T), slice(h * d, (h + 1) * d)
            q = q_ref[sl, hl]
            ks = (k_ref[sl, hl].astype(F32) * scale).astype(BF16)
            v_t = v_ref[sl, hl].astype(F32).T
            vaug_t = jnp.concatenate([v_t, jnp.ones((d, T), F32)], axis=0).astype(BF16)
            s_raw = lax.dot_general(ks, q, nt, preferred_element_type=F32)
            chunk_cache[(h, j)] = (q, ks, v_t, vaug_t, s_raw)
        return chunk_cache[(h, j)]

    h_sum = {}
    for it in range(nc):
        step = {ch: (it if ch[0] == 0 else nc - 1 - it) for ch in chains}
        data = {ch: chunk_data(ch[1], step[ch]) for ch in chains}
        inter_t = {ch: lax.dot_general(caug_t[ch].astype(BF16), data[ch][0], nt, preferred_element_type=F32)
                   for ch in chains}
        gate = {}
        for ch in chains:
            dr, h = ch
            sl = slice(step[ch] * T, (step[ch] + 1) * T)
            li_r, b_r = gr_ref[gcol(dr, 0, h):gcol(dr, 0, h) + 1, sl], gr_ref[gcol(dr, 1, h):gcol(dr, 1, h) + 1, sl]
            src = gc_ref[sl, gcol(dr, 0, h):gcol(dr, 0, h) + 1] - gc_ref[sl, gcol(dr, 1, h):gcol(dr, 1, h) + 1]
            dm = jnp.where(allowed[dr], src + b_r, NEG)
            inter = b_r + m[ch]
            m_t = jnp.maximum(inter, jnp.max(dm, axis=0, keepdims=True))
            b_end = b_r[:, T - 1:T] if dr == 0 else b_r[:, 0:1]
            g_r = b_end - b_r + li_r
            m_new = jnp.maximum(b_end + m[ch], jnp.max(g_r, axis=1, keepdims=True))
            gate[ch] = (jnp.exp(dm - m_t), jnp.exp(inter - m_t), jnp.exp(-m_t), jnp.exp(g_r - m_new),
                        jnp.exp(b_end + m[ch] - m_new), m_new)
        for ch in chains:
            q, ks, v_t, vaug_t, s_raw = data[ch]
            w_intra, w_inter, floor, w_tok, decay, m_new = gate[ch]
            acc = jnp.dot(vaug_t, (s_raw * w_intra).astype(BF16), preferred_element_type=F32) + w_inter * inter_t[ch]
            h_t = acc[:d] / jnp.maximum(jnp.abs(acc[d:]), floor)
            key = (ch[1], step[ch])
            h_sum[key] = h_t if key not in h_sum else h_sum[key] + h_t
            vw_t = jnp.concatenate([v_t * w_tok, jnp.broadcast_to(w_tok, (d, T))], axis=0).astype(BF16)
            caug_t[ch] = decay * caug_t[ch] + jnp.dot(vw_t, ks, preferred_element_type=F32)
            m[ch] = m_new

    if want_state:
        for ch in chains:
            dr, h = ch
            c_out[dr, h] = caug_t[ch][:d].T
            n_out[dr, h:h + 1, :] = caug_t[ch][d:d + 1, :]
            m_out[dr, h:h + 1, :] = jnp.broadcast_to(m[ch], (1, d))
    for h in range(ML_HEADS):
        for j in range(nc):
            sl, hl = slice(j * T, (j + 1) * T), slice(h * d, (h + 1) * d)
            hv = h_sum[(h, j)].T
            y = hv * lax.rsqrt(jnp.mean(hv * hv, axis=-1, keepdims=True) + EPS) * ng_ref[:, hl]
            y_ref[sl, hl] = (y * jax.nn.sigmoid(o_ref[sl, hl].astype(F32))).astype(y_ref.dtype)


def mlstm_mix(u, gates, seq0, B, L, norm_g, state=None, want_state=False):
    u3 = u.reshape(-1, L, EVEN_MAIN)
    gc = gates.reshape(-1, L, N_GATES)[seq0:seq0 + B]
    gr = gc.transpose(0, 2, 1)
    width = ML_HEADS * ML_HD
    col = lambda i: pl.BlockSpec((None, L, width), lambda b: (b + seq0, 0, (3 * HY_W + i * width) // width))
    in_specs = [col(0), col(1), col(2), col(3),
                pl.BlockSpec((None, L, N_GATES), lambda b: (b, 0, 0)),
                pl.BlockSpec((None, N_GATES, L), lambda b: (b, 0, 0)),
                pl.BlockSpec((1, width), lambda b: (0, 0))]
    args = [u3, u3, u3, u3, gc, gr, norm_g.reshape(1, width)]
    sspec = pl.BlockSpec((None, 2, ML_HEADS, ML_HD, ML_HD), lambda b: (b, 0, 0, 0, 0))
    vspec = pl.BlockSpec((None, 2, ML_HEADS, ML_HD), lambda b: (b, 0, 0, 0))
    if state is not None:
        C0, n0, m0 = state
        in_specs += [sspec, sspec, vspec]
        args += [C0.swapaxes(-1, -2), jnp.broadcast_to(n0[..., None, :], C0.shape),
                 jnp.broadcast_to(m0[..., None], n0.shape)]
    out_specs = [pl.BlockSpec((None, L, width), lambda b: (b, 0, 0))]
    out_shape = [jax.ShapeDtypeStruct((B, L, width), BF16)]
    if want_state:
        out_specs += [sspec, vspec, vspec]
        out_shape += [jax.ShapeDtypeStruct((B, 2, ML_HEADS, ML_HD, ML_HD), F32),
                      jax.ShapeDtypeStruct((B, 2, ML_HEADS, ML_HD), F32),
                      jax.ShapeDtypeStruct((B, 2, ML_HEADS, ML_HD), F32)]
    outs = pl.pallas_call(
        functools.partial(_mlstm_kernel, has_state=state is not None, want_state=want_state),
        grid=(B,),
        in_specs=in_specs, out_specs=out_specs, out_shape=out_shape,
        compiler_params=_cparams("parallel"),
        name=f"mlstm_{L}",
    )(*args)
    y = outs[0].reshape(B * L, width)
    if not want_state:
        return y
    _, C, n, m = outs
    return y, C, n, m[..., 0]


def _proj_inputs(acts, w, x, mods, tm, tile_of=_same_tile):
    xs = tuple(x) if isinstance(x, (tuple, list)) else (x,)
    n_p = acts[0][0].shape[0] // tm
    specs = []
    for pair in acts:
        specs += _part_specs((tm, pair[0].shape[1]), n_p, tile_of)
    specs.append(pl.BlockSpec(w.shape, lambda *_: (0, 0), pipeline_mode=pl.Buffered(1)))
    specs += (_part_specs((tm, D), n_p, tile_of) if len(xs) == 2
              else [pl.BlockSpec((tm, D), lambda i, *_: (tile_of(i), 0))])
    specs.append(_mod_spec(2, tm, tile_of))
    args = [a for pair in acts for a in pair] + [w, *xs, mods]
    return specs, args, dict(n_in=len(acts), n_x=len(xs), n_p=n_p)


def _proj_value(refs, is_prompt, n_in, n_x, n_p):
    a_refs = refs[:2 * n_in]
    w_ref = refs[2 * n_in]
    x_refs = refs[2 * n_in + 1:2 * n_in + 1 + n_x]
    gate_ref = refs[2 * n_in + 1 + n_x]
    k0 = 0
    acc = None
    for j in range(n_in):
        a = _pick(is_prompt, a_refs[2 * j], a_refs[2 * j + 1])
        kw = a.shape[1]
        part = _bdot(a, w_ref[k0:k0 + kw, :])
        acc = part if acc is None else acc + part
        k0 += kw
    x = _pick(is_prompt, *x_refs) if n_x == 2 else x_refs[0][...]
    return x + gate_ref[...] * acc


def _n_proj_refs(n_in, n_x, n_p):
    return 2 * n_in + 1 + n_x + 1


SWIGLU_ROWS = 512


def _swiglu_accumulate(h_scr, acc_scr, w1_ref, w3_ref, w2_ref, rows, scale=None):
    w1, w3, w2 = w1_ref[...].astype(BF16), w3_ref[...].astype(BF16), w2_ref[...].astype(BF16)
    groups = [slice(r, r + SWIGLU_ROWS) for r in range(0, rows, SWIGLU_ROWS)]
    ups = []
    for sl in groups:
        h = h_scr[sl, :]
        ups.append((jnp.dot(h, w1, preferred_element_type=F32), jnp.dot(h, w3, preferred_element_type=F32)))
    for sl, (a, b) in zip(groups, ups):
        mid = (_silu(a) * b).astype(BF16)
        down = jnp.dot(mid, w2, preferred_element_type=F32)
        acc_scr[sl, :] += down if scale is None else scale * down


def _ffn_kernel(*refs, proj, nc):
    n = _n_proj_refs(**proj)
    g_ref, sh_ref, sc_ref, gate_ref, w1_ref, w3_ref, w2_ref, o_ref, h_scr, b1_scr, b3_scr, b2_scr = refs[n:]
    i = pl.program_id(0)
    tile = jnp.maximum(i - (nc - 1), 0)
    is_prompt = tile < proj["n_p"]
    rows = h_scr.shape[0]

    def start_tile():
        x = _proj_value(refs[:n], is_prompt, **proj)
        o_ref[...] = x
        h_scr[...] = _norm_mod(x, g_ref[...], sh_ref[...], sc_ref[...]).astype(BF16)

    def chunk(c):
        _swiglu_accumulate(h_scr, o_ref, b1_scr.at[c], b3_scr.at[c], b2_scr.at[c], rows, scale=gate_ref[...])

    pl.when(i == 0)(start_tile)

    @pl.when(i < nc)
    def _():
        b1_scr[i] = w1_ref[...].astype(BF16)
        b3_scr[i] = w3_ref[...].astype(BF16)
        b2_scr[i] = w2_ref[...].astype(BF16)
        chunk(i)

    @pl.when(i >= nc)
    def _():
        start_tile()
        for c in range(nc):
            chunk(c)


def ffn_residual(acts, w_out, x, g, mods, w1, w3, w2, tm=512, tf=256):
    T = sum(a.shape[0] for a in acts[0])
    nc = D_FF // tf
    tile_of = lambda i: jnp.maximum(i - (nc - 1), 0)
    chunk_of = lambda i: jnp.minimum(i, nc - 1)
    p_specs, p_args, proj = _proj_inputs(acts, w_out, x, mods, tm, tile_of)
    return pl.pallas_call(
        functools.partial(_ffn_kernel, proj=proj, nc=nc),
        grid=(T // tm + nc - 1,),
        in_specs=p_specs + [
                  pl.BlockSpec((1, D), lambda i: (0, 0)),
                  _mod_spec(3, tm, tile_of), _mod_spec(4, tm, tile_of), _mod_spec(5, tm, tile_of),
                  pl.BlockSpec((D, tf), lambda i: (0, chunk_of(i))),
                  pl.BlockSpec((D, tf), lambda i: (0, chunk_of(i))),
                  pl.BlockSpec((tf, D), lambda i: (chunk_of(i), 0))],
        out_specs=pl.BlockSpec((tm, D), lambda i: (tile_of(i), 0)),
        out_shape=jax.ShapeDtypeStruct((T, D), F32),
        scratch_shapes=[pltpu.VMEM((tm, D), BF16), pltpu.VMEM((nc, D, tf), BF16),
                        pltpu.VMEM((nc, D, tf), BF16), pltpu.VMEM((nc, tf, D), BF16)],
        compiler_params=_cparams("arbitrary"),
        name="ffn",
    )(*p_args, g, mods, mods, mods, w1, w3, w2)


def _qkv_kernel(x_ref, g_ref, sh_ref, sc_ref, w_ref, q_ref, kv_ref):
    h = _norm_mod(x_ref[...], g_ref[...], sh_ref[...], sc_ref[...]).astype(BF16)
    nq = q_ref.shape[1]
    q_ref[...] = _bdot(h, w_ref[:, :nq]).astype(q_ref.dtype)
    kv_ref[...] = _bdot(h, w_ref[:, nq:])


def qkv_proj(x, g, mods, w_qkv, tm=1024):
    T = x.shape[0]
    nq, nkv = ATT_HEADS * ATT_HD, 2 * ATT_KV * ATT_HD
    return pl.pallas_call(
        _qkv_kernel,
        grid=(T // tm,),
        in_specs=[pl.BlockSpec((tm, D), lambda i: (i, 0)),
                  pl.BlockSpec((1, D), lambda i: (0, 0)),
                  _mod_spec(0, tm), _mod_spec(1, tm),
                  _const_spec(w_qkv, 1)],
        out_specs=[pl.BlockSpec((tm, nq), lambda i: (i, 0)),
                   pl.BlockSpec((tm, nkv), lambda i: (i, 0))],
        out_shape=[jax.ShapeDtypeStruct((T, nq), BF16), jax.ShapeDtypeStruct((T, nkv), F32)],
        compiler_params=_cparams("parallel"),
        name="qkv_proj",
    )(x, g, mods, mods, w_qkv)


def _rope_tables(L):
    half = ATT_HD // 2
    pos_r = (np.arange(L) // GRID_W).astype(np.float32)
    pos_c = (np.arange(L) % GRID_W).astype(np.float32)
    inv = (ROPE_BASE ** (-np.arange(0, half, 2, dtype=np.float32) / half)).astype(np.float32)
    cos = np.zeros((L, ATT_HD), np.float64)
    sin = np.zeros((L, ATT_HD), np.float64)
    for base, pos in ((0, pos_r), (half, pos_c)):
        ang = (pos[:, None] * inv[None, :]).astype(np.float32).astype(np.float64)
        cos[:, base:base + half] = np.concatenate([np.cos(ang), np.cos(ang)], axis=1)
        sin[:, base:base + half] = np.concatenate([-np.sin(ang), np.sin(ang)], axis=1)
    return (np.tile(cos, (1, 4)).astype(np.float32), np.tile(sin, (1, 4)).astype(np.float32))


def _seg_rms(x):
    w = x.shape[1]
    ri = lax.broadcasted_iota(jnp.int32, (w, w), 0) // ATT_HD
    ci = lax.broadcasted_iota(jnp.int32, (w, w), 1) // ATT_HD
    ss = _bdot(x * x, (ri == ci).astype(F32))
    return x * lax.rsqrt(ss * (1.0 / ATT_HD) + EPS)


LOG2E = 1.4426950408889634
ATTN_LOOKAHEAD = 1


def _exp2_bf16(x):
    return jnp.exp2(x.astype(BF16))


def _both_halves(tile, low):
    lane = lax.broadcasted_iota(jnp.int32, tile.shape, 1)
    other = pltpu.roll(tile, ATT_HD, 1)
    return jnp.where((lane < ATT_HD) == low, tile, other)


def _swap16(x):
    w = x.shape[1]
    lane = lax.broadcasted_iota(jnp.int32, x.shape, 1)
    return jnp.where(lane % 32 < 16, pltpu.roll(x, w - 16, 1), pltpu.roll(x, 16, 1))


def _attn_kernel(*refs, latent, tq):
    if latent:
        (q_ref, kv_ref, ck_ref, cv_ref, qg_ref, kg_ref, sink_ref, cosq_ref, sinq_ref, cosk_ref, sink_t_ref,
         o_ref, kk_scr, vt_scr, ckk_scr, cvt_scr) = refs
    else:
        q_ref, kv_ref, qg_ref, kg_ref, sink_ref, o_ref, ko_ref, vo_ref, kk_scr, vt_scr = refs
    L = kv_ref.shape[0]
    gw = ATT_KV * ATT_HD
    pw = 2 * ATT_HD
    qb = pl.program_id(1)

    vrows = vt_scr.shape[2]
    nblk = L // pw

    def vt_aug(tile, low):
        vt = tile.T[0:ATT_HD, :] if low else tile.T[ATT_HD:, :]
        return jnp.concatenate([vt, jnp.ones((vrows - ATT_HD, tile.shape[0]), F32)], axis=0).astype(BF16)

    @pl.when(qb == 0)
    def _():
        kn = _seg_rms(kv_ref[:, :gw]) * kg_ref[...]
        v = kv_ref[:, gw:]
        if latent:
            kn = kn * cosk_ref[...] + _swap16(kn) * sink_t_ref[...]
        else:
            for c in range(ATT_KV):
                ko_ref[c] = kn[:, c * ATT_HD:(c + 1) * ATT_HD]
                vo_ref[c] = v[:, c * ATT_HD:(c + 1) * ATT_HD]
        for c in range(ATT_KV):
            tile, low = slice((c // 2) * pw, (c // 2 + 1) * pw), c % 2 == 0
            kk_scr[c] = _both_halves(kn[:, tile], low).astype(BF16)
            for j in range(nblk):
                vt_scr[c, j] = vt_aug(v[j * pw:(j + 1) * pw, tile], low)
            if latent:
                ck, cv = ck_ref[c], cv_ref[c]
                ckk_scr[c] = jnp.concatenate([ck, ck], axis=1).astype(BF16)
                cvt_scr[c] = vt_aug(jnp.concatenate([cv, cv], axis=1), True)

    if latent:
        span = tq + 2 * WINDOW
        start = pl.multiple_of(jnp.clip(qb * tq - WINDOW, 0, L - span), WINDOW)
        blk0 = start // pw
        s_pos = start + lax.broadcasted_iota(jnp.int32, (span, tq), 0)
        t_pos = qb * tq + lax.broadcasted_iota(jnp.int32, (span, tq), 1)
        win_bias = jnp.where(jnp.abs(t_pos - s_pos) <= WINDOW, 0.0, NEG)
    else:
        span, blk0 = L, 0

    nt = (((1,), (1,)), ((), ()))
    low_q = lax.broadcasted_iota(jnp.int32, (tq, pw), 1) < ATT_HD
    def group_scores(c):
        qc = _seg_rms(q_ref[:, c * gw:(c + 1) * gw].astype(F32)) * qg_ref[...]
        if latent:
            qc = qc * cosq_ref[...] + _swap16(qc) * sinq_ref[...]
            kw = kk_scr[c, pl.ds(start, span), :]
        else:
            kw = kk_scr[c]
        qc = qc * (LOG2E / math.sqrt(ATT_HD))
        scores = []
        for g in range(ATT_KV):
            qt = qc[:, (g // 2) * pw:(g // 2 + 1) * pw]
            qm = jnp.where(low_q if g % 2 == 0 else ~low_q, qt, 0.0).astype(BF16)
            lw = lax.dot_general(kw, qm, nt, preferred_element_type=F32)
            lc = lax.dot_general(ckk_scr[c], qm, nt, preferred_element_type=F32) if latent else None
            scores.append((lw, lc))
        return scores

    def group_outputs(c, scores):
        vw = jnp.concatenate([vt_scr[c, blk0 + j] for j in range(span // pw)], axis=1)
        outs = []
        for g in range(ATT_KV):
            head = c * ATT_KV + g
            sink = sink_ref[:, head:head + 1] * LOG2E
            lw, lc = scores[g]
            if latent:
                lw = lw + win_bias
                mx = jnp.maximum(jnp.maximum(jnp.max(lw, axis=0, keepdims=True),
                                             jnp.max(lc, axis=0, keepdims=True)), sink)
                r = jnp.dot(vw, _exp2_bf16(lw - mx), preferred_element_type=F32) + jnp.dot(
                    cvt_scr[c], _exp2_bf16(lc - mx), preferred_element_type=F32)
            else:
                mx = jnp.maximum(jnp.max(lw, axis=0, keepdims=True), sink)
                r = jnp.dot(vw, _exp2_bf16(lw - mx), preferred_element_type=F32)
            den = r[ATT_HD:ATT_HD + 1, :] + jnp.exp2(sink - mx)
            outs.append(r[0:ATT_HD, :] / den)
        for t in range(2):
            o_ref[:, c * gw + t * pw:c * gw + (t + 1) * pw] = (
                jnp.concatenate(outs[2 * t:2 * t + 2], axis=0).T.astype(o_ref.dtype))

    pending = [group_scores(c) for c in range(ATTN_LOOKAHEAD)]
    for c in range(ATT_KV):
        if c + ATTN_LOOKAHEAD < ATT_KV:
            pending.append(group_scores(c + ATTN_LOOKAHEAD))
        group_outputs(c, pending.pop(0))


def attention(q, kv, row0, q_g, k_g, sink, B, L, cache=None, tq=256):
    latent = cache is not None
    gw = ATT_KV * ATT_HD
    qg = jnp.tile(q_g, ATT_KV)[None]
    kg = jnp.tile(k_g, ATT_KV)[None]
    nq = L // tq
    const = lambda a: pl.BlockSpec(a.shape, lambda b, i: (0,) * a.ndim)
    in_specs = [pl.BlockSpec((tq, ATT_HEADS * ATT_HD), lambda b, i: (row0 // tq + b * nq + i, 0)),
                pl.BlockSpec((L, 2 * gw), lambda b, i: (row0 // L + b, 0))]
    args = [q, kv]
    vrows = ATT_HD + 16
    scratch = [pltpu.VMEM((ATT_KV, L, 2 * ATT_HD), BF16),
               pltpu.VMEM((ATT_KV, L // (2 * ATT_HD), vrows, 2 * ATT_HD), BF16)]
    out_specs = [pl.BlockSpec((tq, ATT_HEADS * ATT_HD), lambda b, i: (b * nq + i, 0))]
    out_shape = [jax.ShapeDtypeStruct((B * L, ATT_HEADS * ATT_HD), BF16)]
    if latent:
        ck, cv = cache
        P = ck.shape[2]
        cos, sin = (jnp.asarray(t) for t in _rope_tables(L))
        in_specs += [pl.BlockSpec((None, ATT_KV, P, ATT_HD), lambda b, i: (b, 0, 0, 0))] * 2
        args += [ck, cv]
        in_specs += [const(qg), const(kg), pl.BlockSpec((1, ATT_HEADS), lambda b, i: (0, 0)),
                     pl.BlockSpec((tq, gw), lambda b, i: (i, 0)), pl.BlockSpec((tq, gw), lambda b, i: (i, 0)),
                     const(cos), const(sin)]
        args += [qg, kg, sink[None], cos, sin, cos, sin]
        scratch += [pltpu.VMEM((ATT_KV, P, 2 * ATT_HD), BF16), pltpu.VMEM((ATT_KV, vrows, P), BF16)]
    else:
        in_specs += [const(qg), const(kg), pl.BlockSpec((1, ATT_HEADS), lambda b, i: (0, 0))]
        args += [qg, kg, sink[None]]
        cache_spec = pl.BlockSpec((None, ATT_KV, L, ATT_HD), lambda b, i: (b, 0, 0, 0))
        out_specs += [cache_spec, cache_spec]
        out_shape += [jax.ShapeDtypeStruct((B, ATT_KV, L, ATT_HD), F32)] * 2
    outs = pl.pallas_call(
        functools.partial(_attn_kernel, latent=latent, tq=tq),
        grid=(B, nq),
        in_specs=in_specs, out_specs=out_specs, out_shape=out_shape,
        scratch_shapes=scratch,
        compiler_params=_cparams("parallel", "arbitrary"),
        name="attn_latent" if latent else "attn_context",
    )(*args)
    return outs[0] if latent else outs


MOE_TM = 1024
MOE_TOK = 1024
RUN_ALIGN = 16
MOE_LOCAL = 2 * MOE_TOK + N_EXPERTS * RUN_ALIGN
MOE_MAX_TILES = (2 * 16384 + (16384 // MOE_TOK) * N_EXPERTS * (RUN_ALIGN - 1)) // MOE_TM + N_EXPERTS + 1
RUN_SIZES = tuple(RUN_ALIGN << b for b in range(7, -1, -1))
MOE_CHUNKS = 11
MOE_MAX_ITEMS = N_EXPERTS * MOE_CHUNKS + MOE_MAX_TILES - N_EXPERTS
ITEM_FULL, ITEM_DEAD, ITEM_NONE = -1, -2, -3


def _router_kernel(*refs, proj):
    n = _n_proj_refs(**proj)
    g_ref, sh_ref, sc_ref, wr_ref, br_ref, tri_ref, x_ref, lp_ref, wts_ref, runs_ref, cnt_ref = refs[n:]

    @pl.when(pl.program_id(0) == 0)
    def _():
        cnt_ref[...] = jnp.zeros_like(cnt_ref)

    x = _proj_value(refs[:n], pl.program_id(0) < proj["n_p"], **proj)
    x_ref[...] = x
    h = _norm_mod(x, g_ref[...], sh_ref[...], sc_ref[...])
    nt = (((1,), (1,)), ((), ()))
    h_hi = h.astype(BF16)
    h_lo = (h - h_hi.astype(F32)).astype(BF16)
    w_hi = wr_ref[...].astype(BF16)
    w_lo = (wr_ref[...] - w_hi.astype(F32)).astype(BF16)
    lg = (lax.dot_general(w_hi, h_hi, nt, preferred_element_type=F32)
          + lax.dot_general(w_hi, h_lo, nt, preferred_element_type=F32)
          + lax.dot_general(w_lo, h_hi, nt, preferred_element_type=F32)) + br_ref[...]
    row = lax.broadcasted_iota(jnp.int32, lg.shape, 0)
    m1 = jnp.max(lg, axis=0, keepdims=True)
    i1 = jnp.min(jnp.where(lg == m1, row, N_EXPERTS), axis=0, keepdims=True)
    l2 = jnp.where(row == i1, -jnp.inf, lg)
    m2 = jnp.max(l2, axis=0, keepdims=True)
    i2 = jnp.min(jnp.where(l2 == m2, row, N_EXPERTS), axis=0, keepdims=True)
    e2 = jnp.exp(m2 - m1)
    w1 = 1.0 / (1.0 + e2)
    wts_ref[...] = jnp.concatenate([w1, e2 * w1], axis=0)
    oh1 = (row == i1).astype(F32)
    oh2 = (row == i2).astype(F32)
    cs1 = _bdot(oh1, tri_ref[...])
    cs2 = _bdot(oh2, tri_ref[...])
    tot1 = jnp.sum(oh1, axis=1, keepdims=True)
    run = jnp.ceil((tot1 + jnp.sum(oh2, axis=1, keepdims=True)) * (1.0 / RUN_ALIGN)) * RUN_ALIGN
    run_b = jnp.broadcast_to(run, (N_EXPERTS, 128))
    er = lax.broadcasted_iota(jnp.int32, (N_EXPERTS, N_EXPERTS), 0)
    ec = lax.broadcasted_iota(jnp.int32, (N_EXPERTS, N_EXPERTS), 1)
    start = jnp.dot((ec < er).astype(F32), run_b, precision=HIGHEST, preferred_element_type=F32)
    last = lax.broadcasted_iota(jnp.int32, (N_EXPERTS, 128), 0) == N_EXPERTS - 1
    run_b = jnp.where(last, MOE_LOCAL - start, run_b)
    st = start[:, 0:1]
    p1 = jnp.sum(oh1 * (st + cs1), axis=0, keepdims=True)
    p2 = jnp.sum(oh2 * (st + tot1 + cs2), axis=0, keepdims=True)
    lp_ref[...] = jnp.concatenate([p1, p2], axis=0).astype(jnp.int32)
    lane = lax.broadcasted_iota(jnp.int32, (N_EXPERTS, 128), 1)
    runs_ref[...] = jnp.where(lane == 0, run_b, jnp.where(lane == 1, start, cnt_ref[...]))
    cnt_ref[...] = cnt_ref[...] + run_b


def moe_router(acts, w_out, x, g, mods, w_router, b_router, tm=MOE_TOK):
    T = x.shape[0]
    tri = jnp.asarray(np.triu(np.ones((tm, tm), np.float32), k=1)).astype(BF16)
    tok2 = lambda dt: jax.ShapeDtypeStruct((2, T), dt)
    p_specs, p_args, proj = _proj_inputs(acts, w_out, x, mods, tm)
    return pl.pallas_call(
        functools.partial(_router_kernel, proj=proj),
        grid=(T // tm,),
        in_specs=p_specs + [
                  pl.BlockSpec((1, D), lambda i: (0, 0)),
                  _mod_spec(3, tm), _mod_spec(4, tm),
                  pl.BlockSpec((N_EXPERTS, D), lambda i: (0, 0)),
                  pl.BlockSpec((N_EXPERTS, 1), lambda i: (0, 0)),
                  _const_spec(tri, 1)],
        out_specs=[pl.BlockSpec((tm, D), lambda i: (i, 0)),
                   pl.BlockSpec((2, tm), lambda i: (0, i)),
                   pl.BlockSpec((2, tm), lambda i: (0, i)),
                   pl.BlockSpec((None, N_EXPERTS, 128), lambda i: (i, 0, 0)),
                   pl.BlockSpec((N_EXPERTS, 128), lambda i: (0, 0))],
        out_shape=[jax.ShapeDtypeStruct((T, D), F32), tok2(jnp.int32), tok2(F32),
                   jax.ShapeDtypeStruct((T // tm, N_EXPERTS, 128), F32),
                   jax.ShapeDtypeStruct((N_EXPERTS, 128), F32)],
        compiler_params=_cparams("arbitrary"),
        name="moe_router",
    )(*p_args, g, mods, mods, w_router.T, b_router[:, None], tri)


def moe_layout(runs, totals):
    rows = totals[:, 0].astype(jnp.int32)
    tiles = (rows + MOE_TM - 1) // MOE_TM
    tile_end = jnp.cumsum(tiles)
    group = (tile_end - tiles) * MOE_TM
    run_len = runs[:, :, 0].astype(jnp.int32)
    run_src = runs[:, :, 1].astype(jnp.int32)
    run_dst = group[None, :] + runs[:, :, 2].astype(jnp.int32)
    tail = jnp.stack([group + rows, tiles * MOE_TM - rows]).astype(jnp.int32)
    n_tiles = tile_end[-1]
    t = jnp.arange(MOE_MAX_TILES, dtype=jnp.int32)
    tile_e = jnp.sum(t[:, None] >= tile_end[None, :], axis=1).astype(jnp.int32)
    last_e = jnp.sum((n_tiles - 1) >= tile_end).astype(jnp.int32)
    tile_e = jnp.where(t < n_tiles, tile_e, last_e)
    first = jnp.sum(jnp.where(tile_e[:, None] == jnp.arange(N_EXPERTS), (tile_end - tiles)[None, :], 0), axis=1)
    e_rows = jnp.sum(jnp.where(tile_e[:, None] == jnp.arange(N_EXPERTS), rows[None, :], 0), axis=1)
    tile_rows = jnp.where(t < n_tiles, jnp.clip(e_rows - (t - first) * MOE_TM, 0, MOE_TM), 0).astype(jnp.int32)
    run_tab = jnp.stack([run_len, run_src, run_dst]).reshape(3, -1)
    live = t < n_tiles
    is_first = (t == first) & live
    n_items = jnp.where(is_first, MOE_CHUNKS, 1)
    item_end = jnp.cumsum(n_items)
    j = jnp.arange(MOE_MAX_ITEMS, dtype=jnp.int32)
    it_tile = jnp.minimum(jnp.sum(j[:, None] >= item_end[None, :], axis=1), MOE_MAX_TILES - 1).astype(jnp.int32)
    chunk = j - jnp.take(item_end - n_items, it_tile)
    kind = jnp.where(jnp.take(is_first, it_tile), chunk, jnp.where(jnp.take(live, it_tile), ITEM_FULL, ITEM_DEAD))
    kind = jnp.where(j >= item_end[-1], ITEM_NONE, kind)
    items = jnp.stack([it_tile, kind.astype(jnp.int32)])
    return run_tab, tail, tile_e, n_tiles.astype(jnp.int32).reshape(1), tile_rows, items


def _run_copies(tab_ref, i, local_ref, global_ref, sem, to_global):
    out = []
    for e in range(N_EXPERTS):
        k = i * N_EXPERTS + e
        n, src, dst = tab_ref[0, k], tab_ref[1, k], tab_ref[2, k]
        for size in RUN_SIZES:
            done = (n // (2 * size)) * (2 * size)
            loc = local_ref.at[pl.ds(pl.multiple_of(src + done, RUN_ALIGN), size), :]
            glo = global_ref.at[pl.ds(pl.multiple_of(dst + done, RUN_ALIGN), size), :]
            copy = pltpu.make_async_copy(loc, glo, sem) if to_global else pltpu.make_async_copy(glo, loc, sem)
            out.append(((n & size) != 0, copy))
    return out


def _start(copies, live=True):
    for pred, copy in copies:
        pl.when(pred & live)(copy.start)


def _wait(copies, live=True):
    for pred, copy in copies:
        pl.when(pred & live)(copy.wait)


def _start_then_wait(copies):
    _start(copies)
    _wait(copies)


def _dispatch_kernel(tab_ref, tail_ref, nt_ref, lp_ref, x_ref, g_ref, sh_ref, sc_ref, xs_ref, hs_scr, z_scr, sem):
    i = pl.program_id(0)
    tm = x_ref.shape[0]
    buf = i % 2
    tg = 256
    slot = lax.broadcasted_iota(jnp.int32, (MOE_LOCAL, tg), 0)
    acc = None
    for k in range(tm // tg):
        sl = slice(k * tg, (k + 1) * tg)
        h = _norm_mod(x_ref[sl, :], g_ref[...], sh_ref[...], sc_ref[...]).astype(BF16)
        perm = jnp.where((slot == lp_ref[0:1, sl]) | (slot == lp_ref[1:2, sl]), 1.0, 0.0).astype(BF16)
        part = jnp.dot(perm, h, preferred_element_type=F32)
        acc = part if acc is None else acc + part
    hs_scr[buf] = acc.astype(BF16)
    copies = _run_copies(tab_ref, i, hs_scr.at[buf], xs_ref, sem.at[buf], to_global=True)
    _start(copies)
    _wait(_run_copies(tab_ref, jnp.maximum(i - 1, 0), hs_scr.at[1 - buf], xs_ref, sem.at[1 - buf], to_global=True),
          live=i > 0)

    @pl.when(i == 0)
    def _():
        z_scr[...] = jnp.zeros_like(z_scr)
        zrows = z_scr.shape[0]

        def zero_tile(t, carry):
            for part in range(MOE_TM // zrows):
                dst = xs_ref.at[pl.ds(pl.multiple_of(t * MOE_TM + part * zrows, zrows), zrows), :]
                copy = pltpu.make_async_copy(z_scr, dst, sem.at[2])
                copy.start()
                copy.wait()
            return carry

        lax.fori_loop(nt_ref[0], MOE_MAX_TILES, zero_tile, 0)
        tails = []
        for e in range(N_EXPERTS):
            start, n = tail_ref[0, e], tail_ref[1, e]
            for size in RUN_SIZES:
                if size >= MOE_TM:
                    continue
                done = (n // (2 * size)) * (2 * size)
                dst = xs_ref.at[pl.ds(pl.multiple_of(start + done, RUN_ALIGN), size), :]
                tails.append(((n & size) != 0, pltpu.make_async_copy(z_scr.at[pl.ds(0, size), :], dst, sem.at[2])))
        _start_then_wait(tails)

    _wait(copies, live=i == pl.num_programs(0) - 1)


def moe_dispatch(x, g, mods, lp, run_tab, tail, n_tiles, tm=MOE_TOK):
    T = x.shape[0]
    n_rows = MOE_MAX_TILES * MOE_TM
    return pl.pallas_call(
        _dispatch_kernel,
        grid_spec=pltpu.PrefetchScalarGridSpec(
            num_scalar_prefetch=3,
            grid=(T // tm,),
            in_specs=[pl.BlockSpec((2, tm), lambda i, *_: (0, i)),
                      pl.BlockSpec((tm, D), lambda i, *_: (i, 0)),
                      pl.BlockSpec((1, D), lambda i, *_: (0, 0)),
                      _mod_spec(3, tm), _mod_spec(4, tm)],
            out_specs=pl.BlockSpec(memory_space=pl.ANY),
            scratch_shapes=[pltpu.VMEM((2, MOE_LOCAL, D), BF16), pltpu.VMEM((MOE_TM // 2, D), BF16),
                            pltpu.SemaphoreType.DMA((3,))]),
        out_shape=jax.ShapeDtypeStruct((n_rows, D), BF16),
        compiler_params=_cparams("arbitrary"),
        name="moe_dispatch",
    )(run_tab, tail, n_tiles, lp, x, g, mods, mods)


def _moe_group_kernel(it_ref, te_ref, nt_ref, tr_ref, x_ref, w1_ref, w3_ref, w2_ref, o_ref, acc_scr, b1_scr, b3_scr,
                      b2_scr):
    j = pl.program_id(0)
    kind = it_ref[1, j]
    rows = tr_ref[it_ref[0, j]]
    half = MOE_TM // 2

    def chunk(c):
        @pl.when(rows > half)
        def _():
            _swiglu_accumulate(x_ref, acc_scr, b1_scr.at[c], b3_scr.at[c], b2_scr.at[c], MOE_TM)

        @pl.when(rows <= half)
        def _():
            _swiglu_accumulate(x_ref, acc_scr, b1_scr.at[c], b3_scr.at[c], b2_scr.at[c], half)

    @pl.when((kind == 0) | (kind == ITEM_FULL))
    def _():
        acc_scr[...] = jnp.zeros_like(acc_scr)

    @pl.when(kind >= 0)
    def _():
        b1_scr[kind] = w1_ref[...].astype(BF16)
        b3_scr[kind] = w3_ref[...].astype(BF16)
        b2_scr[kind] = w2_ref[...].astype(BF16)
        chunk(kind)

    for n_rows, pred in ((MOE_TM, rows > half), (half, rows <= half)):
        @pl.when((kind == ITEM_FULL) & pred)
        def _():
            for c in range(MOE_CHUNKS):
                _swiglu_accumulate(x_ref, acc_scr, b1_scr.at[c], b3_scr.at[c], b2_scr.at[c], n_rows)

    @pl.when((kind == MOE_CHUNKS - 1) | (kind == ITEM_FULL))
    def _():
        o_ref[...] = acc_scr[...].astype(o_ref.dtype)

    @pl.when(kind == ITEM_DEAD)
    def _():
        o_ref[...] = jnp.zeros_like(o_ref)


def moe_grouped_swiglu(xs, items, tile_e, n_tiles, tile_rows, w1, w3, w2):
    tf = D_FF // MOE_CHUNKS
    tile = lambda j, it, te, nt, tr: it[0, j]
    row_in = lambda j, it, te, nt, tr: (jnp.minimum(tile(j, it, te, nt, tr), jnp.maximum(nt[0] - 1, 0)), 0)
    wchunk = lambda j, it: jnp.where(it[1, j] >= 0, it[1, j], MOE_CHUNKS - 1)
    wcol = lambda j, it, te, nt, tr: (te[it[0, j]], 0, wchunk(j, it))
    wrow = lambda j, it, te, nt, tr: (te[it[0, j]], wchunk(j, it), 0)
    return pl.pallas_call(
        _moe_group_kernel,
        grid_spec=pltpu.PrefetchScalarGridSpec(
            num_scalar_prefetch=4,
            grid=(MOE_MAX_ITEMS,),
            in_specs=[pl.BlockSpec((MOE_TM, D), row_in),
                      pl.BlockSpec((None, D, tf), wcol),
                      pl.BlockSpec((None, D, tf), wcol),
                      pl.BlockSpec((None, tf, D), wrow)],
            out_specs=pl.BlockSpec((MOE_TM, D), lambda j, it, te, nt, tr: (it[0, j], 0)),
            scratch_shapes=[pltpu.VMEM((MOE_TM, D), F32), pltpu.VMEM((MOE_CHUNKS, D, tf), BF16),
                            pltpu.VMEM((MOE_CHUNKS, D, tf), BF16), pltpu.VMEM((MOE_CHUNKS, tf, D), BF16)]),
        out_shape=jax.ShapeDtypeStruct(xs.shape, BF16),
        compiler_params=_cparams("arbitrary"),
        name="moe_grouped",
    )(items, tile_e, n_tiles, tile_rows, xs, w1, w3, w2)


def _combine_kernel(tab_ref, lp_ref, wt_ref, x_ref, gate_ref, ys_ref, op_ref, os_ref, yl_scr, sem, *, n_p):
    i = pl.program_id(0)
    tm = x_ref.shape[0]
    buf = i % 2
    last = pl.num_programs(0) - 1
    gather = lambda t, b: _run_copies(tab_ref, t, yl_scr.at[b], ys_ref, sem.at[b], to_global=False)
    _start(gather(i, buf), live=i == 0)
    _start(gather(jnp.minimum(i + 1, last), 1 - buf), live=i < last)
    _wait(gather(i, buf))
    bounds = (0, 512, 1024, 1536, MOE_LOCAL)
    moe = None
    for lo, hi in zip(bounds[:-1], bounds[1:]):
        slot = lo + lax.broadcasted_iota(jnp.int32, (tm, hi - lo), 1)
        mix = (jnp.where(slot == lp_ref[:, 0:1], wt_ref[:, 0:1], 0.0)
               + jnp.where(slot == lp_ref[:, 1:2], wt_ref[:, 1:2], 0.0)).astype(BF16)
        part = jnp.dot(mix, yl_scr[buf, lo:hi, :], preferred_element_type=F32)
        moe = part if moe is None else moe + part
    out = x_ref[...] + gate_ref[...] * moe

    @pl.when(pl.program_id(0) < n_p)
    def _():
        op_ref[...] = out

    @pl.when(pl.program_id(0) >= n_p)
    def _():
        os_ref[...] = out


def moe_combine(x, mods, lp, wts, run_tab, ys, t_prompt, tm=MOE_TOK):
    T = x.shape[0]
    n_p = t_prompt // tm
    return pl.pallas_call(
        functools.partial(_combine_kernel, n_p=n_p),
        grid_spec=pltpu.PrefetchScalarGridSpec(
            num_scalar_prefetch=1,
            grid=(T // tm,),
            in_specs=[pl.BlockSpec((tm, 2), lambda i, *_: (i, 0)),
                      pl.BlockSpec((tm, 2), lambda i, *_: (i, 0)),
                      pl.BlockSpec((tm, D), lambda i, *_: (i, 0)),
                      _mod_spec(5, tm),
                      pl.BlockSpec(memory_space=pl.ANY)],
            out_specs=_part_specs((tm, D), n_p),
            scratch_shapes=[pltpu.VMEM((2, MOE_LOCAL, D), BF16), pltpu.SemaphoreType.DMA((2,))]),
        out_shape=[jax.ShapeDtypeStruct((t_prompt, D), F32), jax.ShapeDtypeStruct((T - t_prompt, D), F32)],
        compiler_params=_cparams("arbitrary"),
        name="moe_combine",
    )(run_tab, lp.T, wts.T, x, mods, ys)


def kernel(x_prompt, x_sample, state_C, state_n, state_m, cache_k, cache_v, c, c_ctx, norm1_g, norm2_g, w_ada, b_ada, ev_w_in, ev_conv, hy_w1, hy_b1, hy_w2, hy_b2, hy_w3, hy_freq, hy_d, ml_b_gate, ml_norm_g, ev_w_out, ff_w1, ff_w3, ff_w2, at_w_qkv, at_q_g, at_k_g, at_sink, at_w_out, moe_w_router, moe_b_router, moe_w1, moe_w3, moe_w2):
    BP, LP, _ = x_prompt.shape
    BS, LS, _ = x_sample.shape
    TP = BP * LP
    assert TP % GROUP == 0 and TP // GROUP == N_PROMPT_GROUPS and LS == GROUP and BS == 8

    xp, xs = x_prompt.reshape(TP, D), x_sample.reshape(BS * LS, D)
    cond = jnp.concatenate([c_ctx[None], c, jnp.zeros((16 - 1 - BS, D), F32)], axis=0)
    mods = adaln_table(cond, w_ada, b_ada)

    u, gates = even_in_proj(xp, xs, norm1_g[0:1], mods[0], ev_w_in[0], ml_b_gate[0].reshape(1, N_GATES))
    hy = []
    for seq0, B, L, nb in ((0, BP, LP, 4), (TP // LS, BS, LS, 1)):
        fwd, inv = (jnp.asarray(t).astype(BF16) for t in _dft_tables(L))
        ka, kb = hyena_filter_spectra(L, hy_w1[0], hy_b1[0], hy_w2[0], hy_b2[0], hy_w3[0], hy_freq[0], fwd)
        hy.append(hyena_mix(u, seq0, B, L, ev_conv[0], hy_d[0], fwd, inv, ka, kb, nb))
    ml_p, new_C, new_n, new_m = mlstm_mix(u, gates, 0, BP, LP, ml_norm_g[0], want_state=True)
    ml_s = mlstm_mix(u, gates, TP // LS, BS, LS, ml_norm_g[0],
                     state=(state_C[:, 0], state_n[:, 0], state_m[:, 0]))
    x = ffn_residual([hy, (ml_p, ml_s)], ev_w_out[0], (xp, xs), norm2_g[0:1], mods[0], ff_w1[0], ff_w3[0], ff_w2[0])

    q, kv = qkv_proj(x, norm1_g[1:2], mods[1], at_w_qkv[0])
    o_p, new_k, new_v = attention(q, kv, 0, at_q_g[0], at_k_g[0], at_sink[0], BP, LP)
    o_s = attention(q, kv, TP, at_q_g[0], at_k_g[0], at_sink[0], BS, LS, cache=(cache_k[:, 0], cache_v[:, 0]))
    x, lp, wts, runs, totals = moe_router([(o_p, o_s)], at_w_out[0], x, norm2_g[1:2], mods[1],
                                          moe_w_router[0], moe_b_router[0])
    run_tab, tail, tile_e, n_tiles, tile_rows, items = moe_layout(runs, totals)
    xsort = moe_dispatch(x, norm2_g[1:2], mods[1], lp, run_tab, tail, n_tiles)
    ysort = moe_grouped_swiglu(xsort, items, tile_e, n_tiles, tile_rows, moe_w1[0], moe_w3[0], moe_w2[0])
    yp, ys = moe_combine(x, mods[1], lp, wts, run_tab, ysort, TP)

    return (yp.reshape(BP, LP, D), ys.reshape(BS, LS, D),
            new_C[:, None], new_n[:, None], new_m[:, None], new_k[:, None], new_v[:, None])
```

```python
import functools
import math

import numpy as np
import jax
import jax.numpy as jnp
from jax import lax
from jax.experimental import pallas as pl
from jax.experimental.pallas import tpu as pltpu

F32 = jnp.float32
BF16 = jnp.bfloat16
HIGHEST = lax.Precision.HIGHEST

D = 1024
GROUP = 1024
N_PROMPT_GROUPS = 8
HY_W = 512
ML_HEADS = 4
ML_HD = 128
ML_CHUNK = 256
EVEN_MAIN = 3 * HY_W + 4 * 512
N_GATES = 16
ATT_HD = 64
ATT_HEADS = 16
ATT_KV = 4
WINDOW = 128
GRID_W = 64
ROPE_BASE = 10000.0
D_FF = 2816
N_EXPERTS = 8
EPS = 1e-6
NEG = -1e30
VMEM_LIMIT = 56 * 1024 * 1024


def _cparams(*sem):
    return pltpu.CompilerParams(dimension_semantics=sem, vmem_limit_bytes=VMEM_LIMIT)


def _mod_row(i, tm):
    return jnp.maximum(i * tm // GROUP - (N_PROMPT_GROUPS - 1), 0)


def _silu(x):
    return x * jax.nn.sigmoid(x)


def _bdot(a, b):
    return jnp.dot(a.astype(BF16), b.astype(BF16), preferred_element_type=F32)


def _norm_mod(x, g, sh, sc):
    y = x * lax.rsqrt(jnp.mean(x * x, axis=-1, keepdims=True) + EPS) * g
    return y * (1.0 + sc) + sh


def _adaln_kernel(c_ref, w_ref, b_ref, o_ref):
    s = _silu(c_ref[...])
    s_hi = s.astype(BF16)
    s_lo = (s - s_hi.astype(F32)).astype(BF16)
    w_hi = w_ref[...].astype(BF16)
    w_lo = (w_ref[...] - w_hi.astype(F32)).astype(BF16)
    dot = functools.partial(jnp.dot, preferred_element_type=F32)
    o_ref[...] = dot(s_hi, w_hi) + dot(s_lo, w_hi) + dot(s_hi, w_lo) + b_ref[...]


def adaln_table(cond, w_ada, b_ada):
    depth = w_ada.shape[0]
    tn = 1536
    out = pl.pallas_call(
        _adaln_kernel,
        grid=(depth, 6 * D // tn),
        in_specs=[pl.BlockSpec((16, D), lambda l, j: (0, 0)),
                  pl.BlockSpec((None, D, tn), lambda l, j: (l, 0, j)),
                  pl.BlockSpec((None, 1, tn), lambda l, j: (l, 0, j))],
        out_specs=pl.BlockSpec((None, 16, tn), lambda l, j: (l, 0, j)),
        out_shape=jax.ShapeDtypeStruct((depth, 16, 6 * D), F32),
        compiler_params=_cparams("parallel", "parallel"),
        name="adaln",
    )(cond, w_ada, b_ada.reshape(depth, 1, 6 * D))
    return out.reshape(depth, 16, 1, 6 * D)


def _same_tile(i):
    return i


def _part_specs(block, n_p, tile_of=_same_tile):
    return [pl.BlockSpec(block, lambda i, *_: (jnp.minimum(tile_of(i), n_p - 1), 0)),
            pl.BlockSpec(block, lambda i, *_: (jnp.maximum(tile_of(i) - n_p, 0), 0))]


def _pick(is_prompt, p_ref, s_ref):
    return jnp.where(is_prompt, p_ref[...], s_ref[...])


def _mod_spec(k, tm, tile_of=_same_tile):
    return pl.BlockSpec((None, 1, D), lambda i, *_: (_mod_row(tile_of(i), tm), 0, k))


def _log_sigmoid(x):
    return jnp.minimum(x, 0.0) - jnp.log(1.0 + jnp.exp(-jnp.abs(x)))


def _split3(x):
    hi = x.astype(BF16)
    r = x - hi.astype(F32)
    mid = r.astype(BF16)
    return hi, mid, (r - mid.astype(F32)).astype(BF16)


def _even_in_kernel(xp_ref, xs_ref, g_ref, sh_ref, sc_ref, w_ref, bg_ref, lo_ref, up_ref, u_ref, gate_ref, *, n_p, tn):
    is_prompt = pl.program_id(0) < n_p
    h = _norm_mod(_pick(is_prompt, xp_ref, xs_ref), g_ref[...], sh_ref[...], sc_ref[...]).astype(BF16)

    def main_chunks(j0, j1):
        for j in range(j0, j1):
            u_ref[:, j * tn:(j + 1) * tn] = _bdot(h, w_ref[:, j * tn:(j + 1) * tn]).astype(u_ref.dtype)

    n_main = EVEN_MAIN // tn
    gates = _bdot(h, w_ref[:, EVEN_MAIN:]) + bg_ref[...]
    main_chunks(0, n_main // 2)
    lf = _log_sigmoid(gates)
    col = lax.broadcasted_iota(jnp.int32, (1, N_GATES), 1)
    is_forget = (col // ML_HEADS) % 2 == 1
    is_rev = col >= N_GATES // 2
    for ch in range(h.shape[0] // ML_CHUNK):
        sl = slice(ch * ML_CHUNK, (ch + 1) * ML_CHUNK)
        parts = _split3(lf[sl])
        cf = sum(jnp.dot(lo_ref[...], p, preferred_element_type=F32) for p in parts)
        cr = sum(jnp.dot(up_ref[...], p, preferred_element_type=F32) for p in parts)
        gate_ref[sl, :] = jnp.where(is_forget, jnp.where(is_rev, cr, cf), gates[sl])
    main_chunks(n_main // 2, n_main)


def even_in_proj(xp, xs, g, mods, w_in, b_gate, tm=1024, tn=512):
    T = xp.shape[0] + xs.shape[0]
    tri = np.tril(np.ones((ML_CHUNK, ML_CHUNK), np.float32))
    lo, up = jnp.asarray(tri).astype(BF16), jnp.asarray(tri.T).astype(BF16)
    return pl.pallas_call(
        functools.partial(_even_in_kernel, n_p=xp.shape[0] // tm, tn=tn),
        grid=(T // tm,),
        in_specs=_part_specs((tm, D), xp.shape[0] // tm) + [
                  pl.BlockSpec((1, D), lambda i: (0, 0)),
                  _mod_spec(0, tm), _mod_spec(1, tm),
                  _const_spec(w_in),
                  pl.BlockSpec((1, N_GATES), lambda i: (0, 0)),
                  _const_spec(lo), _const_spec(up)],
        out_specs=[pl.BlockSpec((tm, EVEN_MAIN), lambda i: (i, 0)),
                   pl.BlockSpec((tm, N_GATES), lambda i: (i, 0))],
        out_shape=[jax.ShapeDtypeStruct((T, EVEN_MAIN), BF16),
                   jax.ShapeDtypeStruct((T, N_GATES), F32)],
        compiler_params=_cparams("parallel"),
        name="even_in_proj",
    )(xp, xs, g, mods, mods, w_in, b_gate, lo, up)


def _dft_tables(L):
    n = 2 * L
    f = np.arange(L, dtype=np.int64)[:, None]
    s = np.arange(L, dtype=np.int64)[None, :]
    ang = 2.0 * np.pi * ((f * s) % n).astype(np.float64) / n
    fwd = np.concatenate([np.cos(ang), -np.sin(ang)], axis=0)
    fwd[L, :] = np.where(np.arange(L) % 2 == 0, 1.0, -1.0)
    t = np.arange(L, dtype=np.int64)[:, None]
    ff = np.arange(L, dtype=np.int64)[None, :]
    ang = 2.0 * np.pi * ((t * ff) % n).astype(np.float64) / n
    inv_re = 2.0 * np.cos(ang) / n
    inv_re[:, 0] = 1.0 / n
    inv_im = -2.0 * np.sin(ang) / n
    inv_im[:, 0] = np.where(np.arange(L) % 2 == 0, 1.0, -1.0) / n
    inv = np.concatenate([inv_re, inv_im], axis=1)
    return fwd.astype(np.float32), inv.astype(np.float32)


def _filter_tables(L):
    t = np.linspace(0.0, 1.0, L, dtype=np.float32).astype(np.float64)[:, None]
    w = 2.0 * math.pi * np.arange(L, dtype=np.float64)[:, None] / L
    bands = np.linspace(1e-4, 16 - 1, 16, dtype=np.float32).astype(np.float64)[None, :]
    z = np.concatenate([t, np.cos(bands * w), -np.sin(bands * w)], axis=-1)
    zp = np.zeros((L, 128), np.float64)
    zp[:, :z.shape[1]] = z
    max_decay = math.log(1e-2) / 0.3
    min_decay = math.log(1e-2) / 1.5
    deltas = np.linspace(min_decay, max_decay, HY_W, dtype=np.float32).astype(np.float64)
    decay = np.exp(-t * np.abs(deltas))
    return zp.astype(np.float32), decay.astype(np.float32)


def _hy_filter_kernel(z_ref, dec_ref, w1_ref, b1_ref, w2_ref, b2_ref, w3_ref, fr_ref, fwd_ref,
                      ka_ref, kb_ref):
    L = z_ref.shape[0]
    hdot = functools.partial(jnp.dot, precision=HIGHEST, preferred_element_type=F32)
    h = jnp.sin(fr_ref[0:1, :] * (hdot(z_ref[...], w1_ref[...]) + b1_ref[...]))
    h = jnp.sin(fr_ref[1:2, :] * (hdot(h, w2_ref[...]) + b2_ref[...]))
    h = hdot(h, w3_ref[...])
    row0 = lax.broadcasted_iota(jnp.int32, (L, 1), 0) == 0
    h0 = h[:, :HY_W] * dec_ref[...]
    h1 = h[:, HY_W:] * dec_ref[...]
    l1 = jnp.sum(jnp.abs(h0), axis=0, keepdims=True) + jnp.sum(jnp.abs(h1), axis=0, keepdims=True)
    inv = 1.0 / l1
    h0 = h0 * inv
    h1 = jnp.where(row0, 0.0, h1 * inv)
    f0 = _bdot(fwd_ref[...], h0)
    f1 = _bdot(fwd_ref[...], h1)
    ka_ref[...] = f0[:L] + f1[:L]
    kb_ref[...] = jnp.where(row0, f0[L:] + f1[L:], f0[L:] - f1[L:])


def _const_spec(a):
    return pl.BlockSpec(a.shape, lambda *_: (0,) * a.ndim, pipeline_mode=pl.Buffered(1))


def hyena_filter_spectra(L, w1, b1, w2, b2, w3, freq, fwd):
    z, dec = _filter_tables(L)
    pad2 = lambda a, r, c: jnp.pad(a, ((0, r - a.shape[0]), (0, c - a.shape[1])))
    args = (jnp.asarray(z), jnp.asarray(dec), pad2(w1, 128, 128), pad2(b1[None], 1, 128),
            pad2(w2, 128, 128), pad2(b2[None], 1, 128), pad2(w3, 128, 4 * HY_W), pad2(freq, 2, 128), fwd)
    in_specs = [_const_spec(a) for a in args]
    in_specs[6] = pl.BlockSpec((128, 2 * HY_W), lambda o: (0, o))
    shp = jax.ShapeDtypeStruct((2, L, HY_W), F32)
    out_spec = pl.BlockSpec((None, L, HY_W), lambda o: (o, 0, 0))
    return pl.pallas_call(
        _hy_filter_kernel,
        grid=(2,),
        in_specs=in_specs,
        out_specs=[out_spec, out_spec],
        out_shape=[shp, shp],
        compiler_params=_cparams("arbitrary"),
        name=f"hyena_filter_{L}",
    )(*args)


def _hyena_kernel(u_ref, cw_ref, d_ref, fwd_ref, inv_ref, ka_ref, kb_ref, o_ref):
    nb, L = u_ref.shape[0], u_ref.shape[1]
    row = lax.broadcasted_iota(jnp.int32, (L, 1), 0)
    first, last = row == 0, row == L - 1
    fwd = fwd_ref[...].astype(BF16)
    inv = inv_ref[...].astype(BF16)

    def long_conv(z, o):
        zf = jnp.dot(fwd, z.astype(BF16), preferred_element_type=F32)
        a, b = zf[:L], zf[L:]
        ka, kb = ka_ref[o], kb_ref[o]
        yr = a * ka - jnp.where(first, 0.0, b * kb)
        yi = jnp.where(first, b * kb, a * kb + b * ka)
        return (jnp.dot(inv[:, :L], yr.astype(BF16), preferred_element_type=F32)
                + jnp.dot(inv[:, L:], yi.astype(BF16), preferred_element_type=F32))

    for bi in range(nb):
        u = u_ref[bi].astype(F32)
        prev = jnp.where(first, 0.0, pltpu.roll(u, 1, 0))
        nxt = jnp.where(last, 0.0, pltpu.roll(u, L - 1, 0))
        u = prev * cw_ref[0:1, :] + u * cw_ref[1:2, :] + nxt * cw_ref[2:3, :]
        v, x1, x2 = u[:, :HY_W], u[:, HY_W:2 * HY_W], u[:, 2 * HY_W:]
        z = x1 * (long_conv(v, 0) + d_ref[0:1, :] * v)
        z = x2 * (long_conv(z, 1) + d_ref[1:2, :] * z)
        o_ref[bi] = z.astype(o_ref.dtype)


def hyena_mix(u, seq0, B, L, conv_w, d_skip, fwd, inv, ka, kb, nb):
    u3 = u.reshape(-1, L, EVEN_MAIN)
    full = _const_spec
    out = pl.pallas_call(
        _hyena_kernel,
        grid=(B // nb,),
        in_specs=[pl.BlockSpec((nb, L, 3 * HY_W), lambda b: (b + seq0 // nb, 0, 0)),
                  full(conv_w), full(d_skip), full(fwd), full(inv), full(ka), full(kb)],
        out_specs=pl.BlockSpec((nb, L, HY_W), lambda b: (b, 0, 0)),
        out_shape=jax.ShapeDtypeStruct((B, L, HY_W), BF16),
        compiler_params=_cparams("parallel"),
        name=f"hyena_{L}",
    )(u3, conv_w, d_skip, fwd, inv, ka, kb)
    return out.reshape(B * L, HY_W)


def _mlstm_kernel(*refs, has_state, want_state):
    q_ref, k_ref, v_ref, o_ref, gc_ref, gr_ref, ng_ref = refs[:7]
    refs = refs[7:]
    if has_state:
        c0t_ref, n0b_ref, m0_ref = refs[:3]
        refs = refs[3:]
    y_ref = refs[0]
    if want_state:
        c_out, n_out, m_out = refs[1:4]
    L, d = q_ref.shape[0], ML_HD
    T = min(ML_CHUNK, L)
    nc = L // T
    scale = 1.0 / math.sqrt(d)
    nt = (((1,), (1,)), ((), ()))
    si = lax.broadcasted_iota(jnp.int32, (T, T), 0)
    ti = lax.broadcasted_iota(jnp.int32, (T, T), 1)
    allowed = (si <= ti, si >= ti)
    chains = [(dr, h) for dr in range(2) for h in range(ML_HEADS)]
    gcol = lambda dr, gi, h: dr * 2 * ML_HEADS + gi * ML_HEADS + h

    caug_t, m = {}, {}
    for ch in chains:
        dr, h = ch
        if has_state:
            caug_t[ch] = jnp.concatenate([c0t_ref[dr, h], n0b_ref[dr, h]], axis=0)
            m[ch] = m0_ref[dr, h:h + 1, 0:1]
        else:
            caug_t[ch], m[ch] = jnp.zeros((2 * d, d), F32), jnp.zeros((1, 1), F32)

    chunk_cache = {}

    def chunk_data(h, j):
        if (h, j) not in chunk_cache:
            sl, hl = slice(j * T, (j + 1) * T), slice(h * d, (h + 1) * d)
            q = q_ref[sl, hl]
            ks = (k_ref[sl, hl].astype(F32) * scale).astype(BF16)
            v_t = v_ref[sl, hl].astype(F32).T
            vaug_t = jnp.concatenate([v_t, jnp.ones((d, T), F32)], axis=0).astype(BF16)
            s_raw = lax.dot_general(ks, q, nt, preferred_element_type=F32)
            chunk_cache[(h, j)] = (q, ks, v_t, vaug_t, s_raw)
        return chunk_cache[(h, j)]

    h_sum = {}
    for it in range(nc):
        step = {ch: (it if ch[0] == 0 else nc - 1 - it) for ch in chains}
        data = {ch: chunk_data(ch[1], step[ch]) for ch in chains}
        inter_t = {ch: lax.dot_general(caug_t[ch].astype(BF16), data[ch][0], nt, preferred_element_type=F32)
                   for ch in chains}
        gate = {}
        for ch in chains:
            dr, h = ch
            sl = slice(step[ch] * T, (step[ch] + 1) * T)
            li_r, b_r = gr_ref[gcol(dr, 0, h):gcol(dr, 0, h) + 1, sl], gr_ref[gcol(dr, 1, h):gcol(dr, 1, h) + 1, sl]
            src = gc_ref[sl, gcol(dr, 0, h):gcol(dr, 0, h) + 1] - gc_ref[sl, gcol(dr, 1, h):gcol(dr, 1, h) + 1]
            dm = jnp.where(allowed[dr], src + b_r, NEG)
            inter = b_r + m[ch]
            m_t = jnp.maximum(inter, jnp.max(dm, axis=0, keepdims=True))
            b_end = b_r[:, T - 1:T] if dr == 0 else b_r[:, 0:1]
            g_r = b_end - b_r + li_r
            m_new = jnp.maximum(b_end + m[ch], jnp.max(g_r, axis=1, keepdims=True))
            gate[ch] = (jnp.exp(dm - m_t), jnp.exp(inter - m_t), jnp.exp(-m_t), jnp.exp(g_r - m_new),
                        jnp.exp(b_end + m[ch] - m_new), m_new)
        for ch in chains:
            q, ks, v_t, vaug_t, s_raw = data[ch]
            w_intra, w_inter, floor, w_tok, decay, m_new = gate[ch]
            acc = jnp.dot(vaug_t, (s_raw * w_intra).astype(BF16), preferred_element_type=F32) + w_inter * inter_t[ch]
            h_t = acc[:d] / jnp.maximum(jnp.abs(acc[d:]), floor)
            key = (ch[1], step[ch])
            h_sum[key] = h_t if key not in h_sum else h_sum[key] + h_t
            vw_t = jnp.concatenate([v_t * w_tok, jnp.broadcast_to(w_tok, (d, T))], axis=0).astype(BF16)
            caug_t[ch] = decay * caug_t[ch] + jnp.dot(vw_t, ks, preferred_element_type=F32)
            m[ch] = m_new

    if want_state:
        for ch in chains:
            dr, h = ch
            c_out[dr, h] = caug_t[ch][:d].T
            n_out[dr, h:h + 1, :] = caug_t[ch][d:d + 1, :]
            m_out[dr, h:h + 1, :] = jnp.broadcast_to(m[ch], (1, d))
    for h in range(ML_HEADS):
        for j in range(nc):
            sl, hl = slice(j * T, (j + 1) * T), slice(h * d, (h + 1) * d)
            hv = h_sum[(h, j)].T
            y = hv * lax.rsqrt(jnp.mean(hv * hv, axis=-1, keepdims=True) + EPS) * ng_ref[:, hl]
            y_ref[sl, hl] = (y * jax.nn.sigmoid(o_ref[sl, hl].astype(F32))).astype(y_ref.dtype)


def mlstm_mix(u, gates, seq0, B, L, norm_g, state=None, want_state=False):
    u3 = u.reshape(-1, L, EVEN_MAIN)
    gc = gates.reshape(-1, L, N_GATES)[seq0:seq0 + B]
    gr = gc.transpose(0, 2, 1)
    width = ML_HEADS * ML_HD
    col = lambda i: pl.BlockSpec((None, L, width), lambda b: (b + seq0, 0, (3 * HY_W + i * width) // width))
    in_specs = [col(0), col(1), col(2), col(3),
                pl.BlockSpec((None, L, N_GATES), lambda b: (b, 0, 0)),
                pl.BlockSpec((None, N_GATES, L), lambda b: (b, 0, 0)),
                pl.BlockSpec((1, width), lambda b: (0, 0))]
    args = [u3, u3, u3, u3, gc, gr, norm_g.reshape(1, width)]
    sspec = pl.BlockSpec((None, 2, ML_HEADS, ML_HD, ML_HD), lambda b: (b, 0, 0, 0, 0))
    vspec = pl.BlockSpec((None, 2, ML_HEADS, ML_HD), lambda b: (b, 0, 0, 0))
    if state is not None:
        C0, n0, m0 = state
        in_specs += [sspec, sspec, vspec]
        args += [C0.swapaxes(-1, -2), jnp.broadcast_to(n0[..., None, :], C0.shape),
                 jnp.broadcast_to(m0[..., None], n0.shape)]
    out_specs = [pl.BlockSpec((None, L, width), lambda b: (b, 0, 0))]
    out_shape = [jax.ShapeDtypeStruct((B, L, width), BF16)]
    if want_state:
        out_specs += [sspec, vspec, vspec]
        out_shape += [jax.ShapeDtypeStruct((B, 2, ML_HEADS, ML_HD, ML_HD), F32),
                      jax.ShapeDtypeStruct((B, 2, ML_HEADS, ML_HD), F32),
                      jax.ShapeDtypeStruct((B, 2, ML_HEADS, ML_HD), F32)]
    outs = pl.pallas_call(
        functools.partial(_mlstm_kernel, has_state=state is not None, want_state=want_state),
        grid=(B,),
        in_specs=in_specs, out_specs=out_specs, out_shape=out_shape,
        compiler_params=_cparams("parallel"),
        name=f"mlstm_{L}",
    )(*args)
    y = outs[0].reshape(B * L, width)
    if not want_state:
        return y
    _, C, n, m = outs
    return y, C, n, m[..., 0]


def _proj_inputs(acts, w, x, mods, tm, tile_of=_same_tile):
    xs = tuple(x) if isinstance(x, (tuple, list)) else (x,)
    n_p = acts[0][0].shape[0] // tm
    specs = []
    for pair in acts:
        specs += _part_specs((tm, pair[0].shape[1]), n_p, tile_of)
    specs.append(pl.BlockSpec(w.shape, lambda *_: (0, 0), pipeline_mode=pl.Buffered(1)))
    specs += (_part_specs((tm, D), n_p, tile_of) if len(xs) == 2
              else [pl.BlockSpec((tm, D), lambda i, *_: (tile_of(i), 0))])
    specs.append(_mod_spec(2, tm, tile_of))
    args = [a for pair in acts for a in pair] + [w, *xs, mods]
    return specs, args, dict(n_in=len(acts), n_x=len(xs), n_p=n_p)


def _proj_value(refs, is_prompt, n_in, n_x, n_p):
    a_refs = refs[:2 * n_in]
    w_ref = refs[2 * n_in]
    x_refs = refs[2 * n_in + 1:2 * n_in + 1 + n_x]
    gate_ref = refs[2 * n_in + 1 + n_x]
    k0 = 0
    acc = None
    for j in range(n_in):
        a = _pick(is_prompt, a_refs[2 * j], a_refs[2 * j + 1])
        kw = a.shape[1]
        part = _bdot(a, w_ref[k0:k0 + kw, :])
        acc = part if acc is None else acc + part
        k0 += kw
    x = _pick(is_prompt, *x_refs) if n_x == 2 else x_refs[0][...]
    return x + gate_ref[...] * acc


def _n_proj_refs(n_in, n_x, n_p):
    return 2 * n_in + 1 + n_x + 1


SWIGLU_ROWS = 512


def _swiglu_accumulate(h_scr, acc_scr, w1_ref, w3_ref, w2_ref, rows, scale=None):
    w1, w3, w2 = w1_ref[...].astype(BF16), w3_ref[...].astype(BF16), w2_ref[...].astype(BF16)
    groups = [slice(r, r + SWIGLU_ROWS) for r in range(0, rows, SWIGLU_ROWS)]
    ups = []
    for sl in groups:
        h = h_scr[sl, :]
        ups.append((jnp.dot(h, w1, preferred_element_type=F32), jnp.dot(h, w3, preferred_element_type=F32)))
    for sl, (a, b) in zip(groups, ups):
        mid = (_silu(a) * b).astype(BF16)
        down = jnp.dot(mid, w2, preferred_element_type=F32)
        acc_scr[sl, :] += down if scale is None else scale * down


def _ffn_kernel(*refs, proj, nc):
    n = _n_proj_refs(**proj)
    g_ref, sh_ref, sc_ref, gate_ref, w1_ref, w3_ref, w2_ref, o_ref, h_scr, b1_scr, b3_scr, b2_scr = refs[n:]
    i = pl.program_id(0)
    tile = jnp.maximum(i - (nc - 1), 0)
    is_prompt = tile < proj["n_p"]
    rows = h_scr.shape[0]

    def start_tile():
        x = _proj_value(refs[:n], is_prompt, **proj)
        o_ref[...] = x
        h_scr[...] = _norm_mod(x, g_ref[...], sh_ref[...], sc_ref[...]).astype(BF16)

    def chunk(c):
        _swiglu_accumulate(h_scr, o_ref, b1_scr.at[c], b3_scr.at[c], b2_scr.at[c], rows, scale=gate_ref[...])

    pl.when(i == 0)(start_tile)

    @pl.when(i < nc)
    def _():
        b1_scr[i] = w1_ref[...].astype(BF16)
        b3_scr[i] = w3_ref[...].astype(BF16)
        b2_scr[i] = w2_ref[...].astype(BF16)
        chunk(i)

    @pl.when(i >= nc)
    def _():
        start_tile()
        for c in range(nc):
            chunk(c)


def ffn_residual(acts, w_out, x, g, mods, w1, w3, w2, tm=512, tf=256):
    T = sum(a.shape[0] for a in acts[0])
    nc = D_FF // tf
    tile_of = lambda i: jnp.maximum(i - (nc - 1), 0)
    chunk_of = lambda i: jnp.minimum(i, nc - 1)
    p_specs, p_args, proj = _proj_inputs(acts, w_out, x, mods, tm, tile_of)
    return pl.pallas_call(
        functools.partial(_ffn_kernel, proj=proj, nc=nc),
        grid=(T // tm + nc - 1,),
        in_specs=p_specs + [
                  pl.BlockSpec((1, D), lambda i: (0, 0)),
                  _mod_spec(3, tm, tile_of), _mod_spec(4, tm, tile_of), _mod_spec(5, tm, tile_of),
                  pl.BlockSpec((D, tf), lambda i: (0, chunk_of(i))),
                  pl.BlockSpec((D, tf), lambda i: (0, chunk_of(i))),
                  pl.BlockSpec((tf, D), lambda i: (chunk_of(i), 0))],
        out_specs=pl.BlockSpec((tm, D), lambda i: (tile_of(i), 0)),
        out_shape=jax.ShapeDtypeStruct((T, D), F32),
        scratch_shapes=[pltpu.VMEM((tm, D), BF16), pltpu.VMEM((nc, D, tf), BF16),
                        pltpu.VMEM((nc, D, tf), BF16), pltpu.VMEM((nc, tf, D), BF16)],
        compiler_params=_cparams("arbitrary"),
        name="ffn",
    )(*p_args, g, mods, mods, mods, w1, w3, w2)


def _qkv_kernel(x_ref, g_ref, sh_ref, sc_ref, w_ref, q_ref, kv_ref):
    h = _norm_mod(x_ref[...], g_ref[...], sh_ref[...], sc_ref[...]).astype(BF16)
    nq = q_ref.shape[1]
    q_ref[...] = _bdot(h, w_ref[:, :nq]).astype(q_ref.dtype)
    kv_ref[...] = _bdot(h, w_ref[:, nq:])


def qkv_proj(x, g, mods, w_qkv, tm=1024):
    T = x.shape[0]
    nq, nkv = ATT_HEADS * ATT_HD, 2 * ATT_KV * ATT_HD
    return pl.pallas_call(
        _qkv_kernel,
        grid=(T // tm,),
        in_specs=[pl.BlockSpec((tm, D), lambda i: (i, 0)),
                  pl.BlockSpec((1, D), lambda i: (0, 0)),
                  _mod_spec(0, tm), _mod_spec(1, tm),
                  _const_spec(w_qkv)],
        out_specs=[pl.BlockSpec((tm, nq), lambda i: (i, 0)),
                   pl.BlockSpec((tm, nkv), lambda i: (i, 0))],
        out_shape=[jax.ShapeDtypeStruct((T, nq), BF16), jax.ShapeDtypeStruct((T, nkv), F32)],
        compiler_params=_cparams("parallel"),
        name="qkv_proj",
    )(x, g, mods, mods, w_qkv)


def _rope_tables(L):
    half = ATT_HD // 2
    pos_r = (np.arange(L) // GRID_W).astype(np.float32)
    pos_c = (np.arange(L) % GRID_W).astype(np.float32)
    inv = (ROPE_BASE ** (-np.arange(0, half, 2, dtype=np.float32) / half)).astype(np.float32)
    cos = np.zeros((L, ATT_HD), np.float64)
    sin = np.zeros((L, ATT_HD), np.float64)
    for base, pos in ((0, pos_r), (half, pos_c)):
        ang = (pos[:, None] * inv[None, :]).astype(np.float32).astype(np.float64)
        cos[:, base:base + half] = np.concatenate([np.cos(ang), np.cos(ang)], axis=1)
        sin[:, base:base + half] = np.concatenate([-np.sin(ang), np.sin(ang)], axis=1)
    return (np.tile(cos, (1, 4)).astype(np.float32), np.tile(sin, (1, 4)).astype(np.float32))


def _seg_rms(x):
    w = x.shape[1]
    ri = lax.broadcasted_iota(jnp.int32, (w, w), 0) // ATT_HD
    ci = lax.broadcasted_iota(jnp.int32, (w, w), 1) // ATT_HD
    ss = _bdot(x * x, (ri == ci).astype(F32))
    return x * lax.rsqrt(ss * (1.0 / ATT_HD) + EPS)


LOG2E = 1.4426950408889634
ATTN_LOOKAHEAD = 1


def _exp2_bf16(x):
    return jnp.exp2(x.astype(BF16))


def _both_halves(tile, low):
    lane = lax.broadcasted_iota(jnp.int32, tile.shape, 1)
    other = pltpu.roll(tile, ATT_HD, 1)
    return jnp.where((lane < ATT_HD) == low, tile, other)


def _swap16(x):
    w = x.shape[1]
    lane = lax.broadcasted_iota(jnp.int32, x.shape, 1)
    return jnp.where(lane % 32 < 16, pltpu.roll(x, w - 16, 1), pltpu.roll(x, 16, 1))


def _attn_kernel(*refs, latent, tq):
    if latent:
        (q_ref, kv_ref, ck_ref, cv_ref, qg_ref, kg_ref, sink_ref, cosq_ref, sinq_ref, cosk_ref, sink_t_ref,
         o_ref, kk_scr, vt_scr, ckk_scr, cvt_scr) = refs
    else:
        q_ref, kv_ref, qg_ref, kg_ref, sink_ref, o_ref, ko_ref, vo_ref, kk_scr, vt_scr = refs
    L = kv_ref.shape[0]
    gw = ATT_KV * ATT_HD
    pw = 2 * ATT_HD
    qb = pl.program_id(1)

    vrows = vt_scr.shape[2]
    nblk = L // pw

    def vt_aug(tile, low):
        vt = tile.T[0:ATT_HD, :] if low else tile.T[ATT_HD:, :]
        return jnp.concatenate([vt, jnp.ones((vrows - ATT_HD, tile.shape[0]), F32)], axis=0).astype(BF16)

    @pl.when(qb == 0)
    def _():
        kn = _seg_rms(kv_ref[:, :gw]) * kg_ref[...]
        v = kv_ref[:, gw:]
        if latent:
            kn = kn * cosk_ref[...] + _swap16(kn) * sink_t_ref[...]
        else:
            for c in range(ATT_KV):
                ko_ref[c] = kn[:, c * ATT_HD:(c + 1) * ATT_HD]
                vo_ref[c] = v[:, c * ATT_HD:(c + 1) * ATT_HD]
        for c in range(ATT_KV):
            tile, low = slice((c // 2) * pw, (c // 2 + 1) * pw), c % 2 == 0
            kk_scr[c] = _both_halves(kn[:, tile], low).astype(BF16)
            for j in range(nblk):
                vt_scr[c, j] = vt_aug(v[j * pw:(j + 1) * pw, tile], low)
            if latent:
                ck, cv = ck_ref[c], cv_ref[c]
                ckk_scr[c] = jnp.concatenate([ck, ck], axis=1).astype(BF16)
                cvt_scr[c] = vt_aug(jnp.concatenate([cv, cv], axis=1), True)

    if latent:
        span = tq + 2 * WINDOW
        start = pl.multiple_of(jnp.clip(qb * tq - WINDOW, 0, L - span), WINDOW)
        blk0 = start // pw
        s_pos = start + lax.broadcasted_iota(jnp.int32, (span, tq), 0)
        t_pos = qb * tq + lax.broadcasted_iota(jnp.int32, (span, tq), 1)
        win_bias = jnp.where(jnp.abs(t_pos - s_pos) <= WINDOW, 0.0, NEG)
    else:
        span, blk0 = L, 0

    nt = (((1,), (1,)), ((), ()))
    low_q = lax.broadcasted_iota(jnp.int32, (tq, pw), 1) < ATT_HD
    def group_scores(c):
        qc = _seg_rms(q_ref[:, c * gw:(c + 1) * gw].astype(F32)) * qg_ref[...]
        if latent:
            qc = qc * cosq_ref[...] + _swap16(qc) * sinq_ref[...]
            kw = kk_scr[c, pl.ds(start, span), :]
        else:
            kw = kk_scr[c]
        qc = qc * (LOG2E / math.sqrt(ATT_HD))
        scores = []
        for g in range(ATT_KV):
            qt = qc[:, (g // 2) * pw:(g // 2 + 1) * pw]
            qm = jnp.where(low_q if g % 2 == 0 else ~low_q, qt, 0.0).astype(BF16)
            lw = lax.dot_general(kw, qm, nt, preferred_element_type=F32)
            lc = lax.dot_general(ckk_scr[c], qm, nt, preferred_element_type=F32) if latent else None
            scores.append((lw, lc))
        return scores

    def group_outputs(c, scores):
        vw = jnp.concatenate([vt_scr[c, blk0 + j] for j in range(span // pw)], axis=1)
        outs = []
        for g in range(ATT_KV):
            head = c * ATT_KV + g
            sink = sink_ref[:, head:head + 1] * LOG2E
            lw, lc = scores[g]
            if latent:
                lw = lw + win_bias
                mx = jnp.maximum(jnp.maximum(jnp.max(lw, axis=0, keepdims=True),
                                             jnp.max(lc, axis=0, keepdims=True)), sink)
                r = jnp.dot(vw, _exp2_bf16(lw - mx), preferred_element_type=F32) + jnp.dot(
                    cvt_scr[c], _exp2_bf16(lc - mx), preferred_element_type=F32)
            else:
                mx = jnp.maximum(jnp.max(lw, axis=0, keepdims=True), sink)
                r = jnp.dot(vw, _exp2_bf16(lw - mx), preferred_element_type=F32)
            den = r[ATT_HD:ATT_HD + 1, :] + jnp.exp2(sink - mx)
            outs.append(r[0:ATT_HD, :] / den)
        for t in range(2):
            o_ref[:, c * gw + t * pw:c * gw + (t + 1) * pw] = (
                jnp.concatenate(outs[2 * t:2 * t + 2], axis=0).T.astype(o_ref.dtype))

    pending = [group_scores(c) for c in range(ATTN_LOOKAHEAD)]
    for c in range(ATT_KV):
        if c + ATTN_LOOKAHEAD < ATT_KV:
            pending.append(group_scores(c + ATTN_LOOKAHEAD))
        group_outputs(c, pending.pop(0))


def attention(q, kv, row0, q_g, k_g, sink, B, L, cache=None, tq=256):
    latent = cache is not None
    gw = ATT_KV * ATT_HD
    qg = jnp.tile(q_g, ATT_KV)[None]
    kg = jnp.tile(k_g, ATT_KV)[None]
    nq = L // tq
    const = lambda a: pl.BlockSpec(a.shape, lambda b, i: (0,) * a.ndim)
    in_specs = [pl.BlockSpec((tq, ATT_HEADS * ATT_HD), lambda b, i: (row0 // tq + b * nq + i, 0)),
                pl.BlockSpec((L, 2 * gw), lambda b, i: (row0 // L + b, 0))]
    args = [q, kv]
    vrows = ATT_HD + 16
    scratch = [pltpu.VMEM((ATT_KV, L, 2 * ATT_HD), BF16),
               pltpu.VMEM((ATT_KV, L // (2 * ATT_HD), vrows, 2 * ATT_HD), BF16)]
    out_specs = [pl.BlockSpec((tq, ATT_HEADS * ATT_HD), lambda b, i: (b * nq + i, 0))]
    out_shape = [jax.ShapeDtypeStruct((B * L, ATT_HEADS * ATT_HD), BF16)]
    if latent:
        ck, cv = cache
        P = ck.shape[2]
        cos, sin = (jnp.asarray(t) for t in _rope_tables(L))
        in_specs += [pl.BlockSpec((None, ATT_KV, P, ATT_HD), lambda b, i: (b, 0, 0, 0))] * 2
        args += [ck, cv]
        in_specs += [const(qg), const(kg), pl.BlockSpec((1, ATT_HEADS), lambda b, i: (0, 0)),
                     pl.BlockSpec((tq, gw), lambda b, i: (i, 0)), pl.BlockSpec((tq, gw), lambda b, i: (i, 0)),
                     const(cos), const(sin)]
        args += [qg, kg, sink[None], cos, sin, cos, sin]
        scratch += [pltpu.VMEM((ATT_KV, P, 2 * ATT_HD), BF16), pltpu.VMEM((ATT_KV, vrows, P), BF16)]
    else:
        in_specs += [const(qg), const(kg), pl.BlockSpec((1, ATT_HEADS), lambda b, i: (0, 0))]
        args += [qg, kg, sink[None]]
        cache_spec = pl.BlockSpec((None, ATT_KV, L, ATT_HD), lambda b, i: (b, 0, 0, 0))
        out_specs += [cache_spec, cache_spec]
        out_shape += [jax.ShapeDtypeStruct((B, ATT_KV, L, ATT_HD), F32)] * 2
    outs = pl.pallas_call(
        functools.partial(_attn_kernel, latent=latent, tq=tq),
        grid=(B, nq),
        in_specs=in_specs, out_specs=out_specs, out_shape=out_shape,
        scratch_shapes=scratch,
        compiler_params=_cparams("parallel", "arbitrary"),
        name="attn_latent" if latent else "attn_context",
    )(*args)
    return outs[0] if latent else outs


MOE_TM = 1024
MOE_TOK = 1024
RUN_ALIGN = 16
MOE_LOCAL = 2 * MOE_TOK + N_EXPERTS * RUN_ALIGN
MOE_MAX_TILES = (2 * 16384 + (16384 // MOE_TOK) * N_EXPERTS * (RUN_ALIGN - 1)) // MOE_TM + N_EXPERTS + 1
RUN_SIZES = tuple(RUN_ALIGN << b for b in range(7, -1, -1))
MOE_CHUNKS = 11
MOE_MAX_ITEMS = N_EXPERTS * MOE_CHUNKS + MOE_MAX_TILES - N_EXPERTS
ITEM_FULL, ITEM_DEAD, ITEM_NONE = -1, -2, -3


def _router_kernel(*refs, proj):
    n = _n_proj_refs(**proj)
    g_ref, sh_ref, sc_ref, wr_ref, br_ref, tri_ref, x_ref, lp_ref, wts_ref, runs_ref, cnt_ref = refs[n:]

    @pl.when(pl.program_id(0) == 0)
    def _():
        cnt_ref[...] = jnp.zeros_like(cnt_ref)

    x = _proj_value(refs[:n], pl.program_id(0) < proj["n_p"], **proj)
    x_ref[...] = x
    h = _norm_mod(x, g_ref[...], sh_ref[...], sc_ref[...])
    nt = (((1,), (1,)), ((), ()))
    h_hi = h.astype(BF16)
    h_lo = (h - h_hi.astype(F32)).astype(BF16)
    w_hi = wr_ref[...].astype(BF16)
    w_lo = (wr_ref[...] - w_hi.astype(F32)).astype(BF16)
    lg = (lax.dot_general(w_hi, h_hi, nt, preferred_element_type=F32)
          + lax.dot_general(w_hi, h_lo, nt, preferred_element_type=F32)
          + lax.dot_general(w_lo, h_hi, nt, preferred_element_type=F32)) + br_ref[...]
    row = lax.broadcasted_iota(jnp.int32, lg.shape, 0)
    m1 = jnp.max(lg, axis=0, keepdims=True)
    i1 = jnp.min(jnp.where(lg == m1, row, N_EXPERTS), axis=0, keepdims=True)
    l2 = jnp.where(row == i1, -jnp.inf, lg)
    m2 = jnp.max(l2, axis=0, keepdims=True)
    i2 = jnp.min(jnp.where(l2 == m2, row, N_EXPERTS), axis=0, keepdims=True)
    e2 = jnp.exp(m2 - m1)
    w1 = 1.0 / (1.0 + e2)
    wts_ref[...] = jnp.concatenate([w1, e2 * w1], axis=0)
    oh1 = (row == i1).astype(F32)
    oh2 = (row == i2).astype(F32)
    cs1 = _bdot(oh1, tri_ref[...])
    cs2 = _bdot(oh2, tri_ref[...])
    tot1 = jnp.sum(oh1, axis=1, keepdims=True)
    run = jnp.ceil((tot1 + jnp.sum(oh2, axis=1, keepdims=True)) * (1.0 / RUN_ALIGN)) * RUN_ALIGN
    run_b = jnp.broadcast_to(run, (N_EXPERTS, 128))
    er = lax.broadcasted_iota(jnp.int32, (N_EXPERTS, N_EXPERTS), 0)
    ec = lax.broadcasted_iota(jnp.int32, (N_EXPERTS, N_EXPERTS), 1)
    start = jnp.dot((ec < er).astype(F32), run_b, precision=HIGHEST, preferred_element_type=F32)
    last = lax.broadcasted_iota(jnp.int32, (N_EXPERTS, 128), 0) == N_EXPERTS - 1
    run_b = jnp.where(last, MOE_LOCAL - start, run_b)
    st = start[:, 0:1]
    p1 = jnp.sum(oh1 * (st + cs1), axis=0, keepdims=True)
    p2 = jnp.sum(oh2 * (st + tot1 + cs2), axis=0, keepdims=True)
    lp_ref[...] = jnp.concatenate([p1, p2], axis=0).astype(jnp.int32)
    lane = lax.broadcasted_iota(jnp.int32, (N_EXPERTS, 128), 1)
    runs_ref[...] = jnp.where(lane == 0, run_b, jnp.where(lane == 1, start, cnt_ref[...]))
    cnt_ref[...] = cnt_ref[...] + run_b


def moe_router(acts, w_out, x, g, mods, w_router, b_router, tm=MOE_TOK):
    T = x.shape[0]
    tri = jnp.asarray(np.triu(np.ones((tm, tm), np.float32), k=1)).astype(BF16)
    tok2 = lambda dt: jax.ShapeDtypeStruct((2, T), dt)
    p_specs, p_args, proj = _proj_inputs(acts, w_out, x, mods, tm)
    return pl.pallas_call(
        functools.partial(_router_kernel, proj=proj),
        grid=(T // tm,),
        in_specs=p_specs + [
                  pl.BlockSpec((1, D), lambda i: (0, 0)),
                  _mod_spec(3, tm), _mod_spec(4, tm),
                  pl.BlockSpec((N_EXPERTS, D), lambda i: (0, 0)),
                  pl.BlockSpec((N_EXPERTS, 1), lambda i: (0, 0)),
                  _const_spec(tri)],
        out_specs=[pl.BlockSpec((tm, D), lambda i: (i, 0)),
                   pl.BlockSpec((2, tm), lambda i: (0, i)),
                   pl.BlockSpec((2, tm), lambda i: (0, i)),
                   pl.BlockSpec((None, N_EXPERTS, 128), lambda i: (i, 0, 0)),
                   pl.BlockSpec((N_EXPERTS, 128), lambda i: (0, 0))],
        out_shape=[jax.ShapeDtypeStruct((T, D), F32), tok2(jnp.int32), tok2(F32),
                   jax.ShapeDtypeStruct((T // tm, N_EXPERTS, 128), F32),
                   jax.ShapeDtypeStruct((N_EXPERTS, 128), F32)],
        compiler_params=_cparams("arbitrary"),
        name="moe_router",
    )(*p_args, g, mods, mods, w_router.T, b_router[:, None], tri)


def moe_layout(runs, totals):
    rows = totals[:, 0].astype(jnp.int32)
    tiles = (rows + MOE_TM - 1) // MOE_TM
    tile_end = jnp.cumsum(tiles)
    group = (tile_end - tiles) * MOE_TM
    run_len = runs[:, :, 0].astype(jnp.int32)
    run_src = runs[:, :, 1].astype(jnp.int32)
    run_dst = group[None, :] + runs[:, :, 2].astype(jnp.int32)
    tail = jnp.stack([group + rows, tiles * MOE_TM - rows]).astype(jnp.int32)
    n_tiles = tile_end[-1]
    t = jnp.arange(MOE_MAX_TILES, dtype=jnp.int32)
    tile_e = jnp.sum(t[:, None] >= tile_end[None, :], axis=1).astype(jnp.int32)
    last_e = jnp.sum((n_tiles - 1) >= tile_end).astype(jnp.int32)
    tile_e = jnp.where(t < n_tiles, tile_e, last_e)
    first = jnp.sum(jnp.where(tile_e[:, None] == jnp.arange(N_EXPERTS), (tile_end - tiles)[None, :], 0), axis=1)
    e_rows = jnp.sum(jnp.where(tile_e[:, None] == jnp.arange(N_EXPERTS), rows[None, :], 0), axis=1)
    tile_rows = jnp.where(t < n_tiles, jnp.clip(e_rows - (t - first) * MOE_TM, 0, MOE_TM), 0).astype(jnp.int32)
    run_tab = jnp.stack([run_len, run_src, run_dst]).reshape(3, -1)
    live = t < n_tiles
    is_first = (t == first) & live
    n_items = jnp.where(is_first, MOE_CHUNKS, 1)
    item_end = jnp.cumsum(n_items)
    j = jnp.arange(MOE_MAX_ITEMS, dtype=jnp.int32)
    it_tile = jnp.minimum(jnp.sum(j[:, None] >= item_end[None, :], axis=1), MOE_MAX_TILES - 1).astype(jnp.int32)
    chunk = j - jnp.take(item_end - n_items, it_tile)
    kind = jnp.where(jnp.take(is_first, it_tile), chunk, jnp.where(jnp.take(live, it_tile), ITEM_FULL, ITEM_DEAD))
    kind = jnp.where(j >= item_end[-1], ITEM_NONE, kind)
    items = jnp.stack([it_tile, kind.astype(jnp.int32)])
    return run_tab, tail, tile_e, n_tiles.astype(jnp.int32).reshape(1), tile_rows, items


def _run_copies(tab_ref, i, local_ref, global_ref, sem, to_global):
    out = []
    for e in range(N_EXPERTS):
        k = i * N_EXPERTS + e
        n, src, dst = tab_ref[0, k], tab_ref[1, k], tab_ref[2, k]
        for size in RUN_SIZES:
            done = (n // (2 * size)) * (2 * size)
            loc = local_ref.at[pl.ds(pl.multiple_of(src + done, RUN_ALIGN), size), :]
            glo = global_ref.at[pl.ds(pl.multiple_of(dst + done, RUN_ALIGN), size), :]
            copy = pltpu.make_async_copy(loc, glo, sem) if to_global else pltpu.make_async_copy(glo, loc, sem)
            out.append(((n & size) != 0, copy))
    return out


def _start(copies, live=True):
    for pred, copy in copies:
        pl.when(pred & live)(copy.start)


def _wait(copies, live=True):
    for pred, copy in copies:
        pl.when(pred & live)(copy.wait)


def _start_then_wait(copies):
    _start(copies)
    _wait(copies)


def _dispatch_kernel(tab_ref, tail_ref, nt_ref, lp_ref, x_ref, g_ref, sh_ref, sc_ref, xs_ref, hs_scr, z_scr, sem):
    i = pl.program_id(0)
    tm = x_ref.shape[0]
    buf = i % 2
    tg = 256
    slot = lax.broadcasted_iota(jnp.int32, (MOE_LOCAL, tg), 0)
    acc = None
    for k in range(tm // tg):
        sl = slice(k * tg, (k + 1) * tg)
        h = _norm_mod(x_ref[sl, :], g_ref[...], sh_ref[...], sc_ref[...]).astype(BF16)
        perm = jnp.where((slot == lp_ref[0:1, sl]) | (slot == lp_ref[1:2, sl]), 1.0, 0.0).astype(BF16)
        part = jnp.dot(perm, h, preferred_element_type=F32)
        acc = part if acc is None else acc + part
    hs_scr[buf] = acc.astype(BF16)
    copies = _run_copies(tab_ref, i, hs_scr.at[buf], xs_ref, sem.at[buf], to_global=True)
    _start(copies)
    _wait(_run_copies(tab_ref, jnp.maximum(i - 1, 0), hs_scr.at[1 - buf], xs_ref, sem.at[1 - buf], to_global=True),
          live=i > 0)

    @pl.when(i == 0)
    def _():
        z_scr[...] = jnp.zeros_like(z_scr)
        zrows = z_scr.shape[0]

        def zero_tile(t, carry):
            for part in range(MOE_TM // zrows):
                dst = xs_ref.at[pl.ds(pl.multiple_of(t * MOE_TM + part * zrows, zrows), zrows), :]
                copy = pltpu.make_async_copy(z_scr, dst, sem.at[2])
                copy.start()
                copy.wait()
            return carry

        lax.fori_loop(nt_ref[0], MOE_MAX_TILES, zero_tile, 0)
        tails = []
        for e in range(N_EXPERTS):
            start, n = tail_ref[0, e], tail_ref[1, e]
            for size in RUN_SIZES:
                if size >= MOE_TM:
                    continue
                done = (n // (2 * size)) * (2 * size)
                dst = xs_ref.at[pl.ds(pl.multiple_of(start + done, RUN_ALIGN), size), :]
                tails.append(((n & size) != 0, pltpu.make_async_copy(z_scr.at[pl.ds(0, size), :], dst, sem.at[2])))
        _start_then_wait(tails)

    _wait(copies, live=i == pl.num_programs(0) - 1)


def moe_dispatch(x, g, mods, lp, run_tab, tail, n_tiles, tm=MOE_TOK):
    T = x.shape[0]
    n_rows = MOE_MAX_TILES * MOE_TM
    return pl.pallas_call(
        _dispatch_kernel,
        grid_spec=pltpu.PrefetchScalarGridSpec(
            num_scalar_prefetch=3,
            grid=(T // tm,),
            in_specs=[pl.BlockSpec((2, tm), lambda i, *_: (0, i)),
                      pl.BlockSpec((tm, D), lambda i, *_: (i, 0)),
                      pl.BlockSpec((1, D), lambda i, *_: (0, 0)),
                      _mod_spec(3, tm), _mod_spec(4, tm)],
            out_specs=pl.BlockSpec(memory_space=pl.ANY),
            scratch_shapes=[pltpu.VMEM((2, MOE_LOCAL, D), BF16), pltpu.VMEM((MOE_TM // 2, D), BF16),
                            pltpu.SemaphoreType.DMA((3,))]),
        out_shape=jax.ShapeDtypeStruct((n_rows, D), BF16),
        compiler_params=_cparams("arbitrary"),
        name="moe_dispatch",
    )(run_tab, tail, n_tiles, lp, x, g, mods, mods)


def _moe_group_kernel(it_ref, te_ref, nt_ref, tr_ref, x_ref, w1_ref, w3_ref, w2_ref, o_ref, acc_scr, b1_scr, b3_scr,
                      b2_scr):
    j = pl.program_id(0)
    kind = it_ref[1, j]
    rows = tr_ref[it_ref[0, j]]
    half = MOE_TM // 2

    def chunk(c):
        @pl.when(rows > half)
        def _():
            _swiglu_accumulate(x_ref, acc_scr, b1_scr.at[c], b3_scr.at[c], b2_scr.at[c], MOE_TM)

        @pl.when(rows <= half)
        def _():
            _swiglu_accumulate(x_ref, acc_scr, b1_scr.at[c], b3_scr.at[c], b2_scr.at[c], half)

    @pl.when((kind == 0) | (kind == ITEM_FULL))
    def _():
        acc_scr[...] = jnp.zeros_like(acc_scr)

    @pl.when(kind >= 0)
    def _():
        b1_scr[kind] = w1_ref[...].astype(BF16)
        b3_scr[kind] = w3_ref[...].astype(BF16)
        b2_scr[kind] = w2_ref[...].astype(BF16)
        chunk(kind)

    for n_rows, pred in ((MOE_TM, rows > half), (half, rows <= half)):
        @pl.when((kind == ITEM_FULL) & pred)
        def _():
            for c in range(MOE_CHUNKS):
                _swiglu_accumulate(x_ref, acc_scr, b1_scr.at[c], b3_scr.at[c], b2_scr.at[c], n_rows)

    @pl.when((kind == MOE_CHUNKS - 1) | (kind == ITEM_FULL))
    def _():
        o_ref[...] = acc_scr[...].astype(o_ref.dtype)

    @pl.when(kind == ITEM_DEAD)
    def _():
        o_ref[...] = jnp.zeros_like(o_ref)


def moe_grouped_swiglu(xs, items, tile_e, n_tiles, tile_rows, w1, w3, w2):
    tf = D_FF // MOE_CHUNKS
    tile = lambda j, it, te, nt, tr: it[0, j]
    row_in = lambda j, it, te, nt, tr: (jnp.minimum(tile(j, it, te, nt, tr), jnp.maximum(nt[0] - 1, 0)), 0)
    wchunk = lambda j, it: jnp.where(it[1, j] >= 0, it[1, j], MOE_CHUNKS - 1)
    wcol = lambda j, it, te, nt, tr: (te[it[0, j]], 0, wchunk(j, it))
    wrow = lambda j, it, te, nt, tr: (te[it[0, j]], wchunk(j, it), 0)
    return pl.pallas_call(
        _moe_group_kernel,
        grid_spec=pltpu.PrefetchScalarGridSpec(
            num_scalar_prefetch=4,
            grid=(MOE_MAX_ITEMS,),
            in_specs=[pl.BlockSpec((MOE_TM, D), row_in),
                      pl.BlockSpec((None, D, tf), wcol),
                      pl.BlockSpec((None, D, tf), wcol),
                      pl.BlockSpec((None, tf, D), wrow)],
            out_specs=pl.BlockSpec((MOE_TM, D), lambda j, it, te, nt, tr: (it[0, j], 0)),
            scratch_shapes=[pltpu.VMEM((MOE_TM, D), F32), pltpu.VMEM((MOE_CHUNKS, D, tf), BF16),
                            pltpu.VMEM((MOE_CHUNKS, D, tf), BF16), pltpu.VMEM((MOE_CHUNKS, tf, D), BF16)]),
        out_shape=jax.ShapeDtypeStruct(xs.shape, BF16),
        compiler_params=_cparams("arbitrary"),
        name="moe_grouped",
    )(items, tile_e, n_tiles, tile_rows, xs, w1, w3, w2)


def _combine_kernel(tab_ref, lp_ref, wt_ref, x_ref, gate_ref, ys_ref, op_ref, os_ref, yl_scr, sem, *, n_p):
    i = pl.program_id(0)
    tm = x_ref.shape[0]
    buf = i % 2
    last = pl.num_programs(0) - 1
    gather = lambda t, b: _run_copies(tab_ref, t, yl_scr.at[b], ys_ref, sem.at[b], to_global=False)
    _start(gather(i, buf), live=i == 0)
    _start(gather(jnp.minimum(i + 1, last), 1 - buf), live=i < last)
    _wait(gather(i, buf))
    bounds = (0, 512, 1024, 1536, MOE_LOCAL)
    moe = None
    for lo, hi in zip(bounds[:-1], bounds[1:]):
        slot = lo + lax.broadcasted_iota(jnp.int32, (tm, hi - lo), 1)
        mix = (jnp.where(slot == lp_ref[:, 0:1], wt_ref[:, 0:1], 0.0)
               + jnp.where(slot == lp_ref[:, 1:2], wt_ref[:, 1:2], 0.0)).astype(BF16)
        part = jnp.dot(mix, yl_scr[buf, lo:hi, :], preferred_element_type=F32)
        moe = part if moe is None else moe + part
    out = x_ref[...] + gate_ref[...] * moe

    @pl.when(pl.program_id(0) < n_p)
    def _():
        op_ref[...] = out

    @pl.when(pl.program_id(0) >= n_p)
    def _():
        os_ref[...] = out


def moe_combine(x, mods, lp, wts, run_tab, ys, t_prompt, tm=MOE_TOK):
    T = x.shape[0]
    n_p = t_prompt // tm
    return pl.pallas_call(
        functools.partial(_combine_kernel, n_p=n_p),
        grid_spec=pltpu.PrefetchScalarGridSpec(
            num_scalar_prefetch=1,
            grid=(T // tm,),
            in_specs=[pl.BlockSpec((tm, 2), lambda i, *_: (i, 0)),
                      pl.BlockSpec((tm, 2), lambda i, *_: (i, 0)),
                      pl.BlockSpec((tm, D), lambda i, *_: (i, 0)),
                      _mod_spec(5, tm),
                      pl.BlockSpec(memory_space=pl.ANY)],
            out_specs=_part_specs((tm, D), n_p),
            scratch_shapes=[pltpu.VMEM((2, MOE_LOCAL, D), BF16), pltpu.SemaphoreType.DMA((2,))]),
        out_shape=[jax.ShapeDtypeStruct((t_prompt, D), F32), jax.ShapeDtypeStruct((T - t_prompt, D), F32)],
        compiler_params=_cparams("arbitrary"),
        name="moe_combine",
    )(run_tab, lp.T, wts.T, x, mods, ys)


def kernel(x_prompt, x_sample, state_C, state_n, state_m, cache_k, cache_v, c, c_ctx, norm1_g, norm2_g, w_ada, b_ada, ev_w_in, ev_conv, hy_w1, hy_b1, hy_w2, hy_b2, hy_w3, hy_freq, hy_d, ml_b_gate, ml_norm_g, ev_w_out, ff_w1, ff_w3, ff_w2, at_w_qkv, at_q_g, at_k_g, at_sink, at_w_out, moe_w_router, moe_b_router, moe_w1, moe_w3, moe_w2):
    BP, LP, _ = x_prompt.shape
    BS, LS, _ = x_sample.shape
    TP = BP * LP
    assert TP % GROUP == 0 and TP // GROUP == N_PROMPT_GROUPS and LS == GROUP and BS == 8

    xp, xs = x_prompt.reshape(TP, D), x_sample.reshape(BS * LS, D)
    cond = jnp.concatenate([c_ctx[None], c, jnp.zeros((16 - 1 - BS, D), F32)], axis=0)
    mods = adaln_table(cond, w_ada, b_ada)

    u, gates = even_in_proj(xp, xs, norm1_g[0:1], mods[0], ev_w_in[0], ml_b_gate[0].reshape(1, N_GATES))
    hy = []
    for seq0, B, L, nb in ((0, BP, LP, 4), (TP // LS, BS, LS, 1)):
        fwd, inv = (jnp.asarray(t).astype(BF16) for t in _dft_tables(L))
        ka, kb = hyena_filter_spectra(L, hy_w1[0], hy_b1[0], hy_w2[0], hy_b2[0], hy_w3[0], hy_freq[0], fwd)
        hy.append(hyena_mix(u, seq0, B, L, ev_conv[0], hy_d[0], fwd, inv, ka, kb, nb))
    ml_p, new_C, new_n, new_m = mlstm_mix(u, gates, 0, BP, LP, ml_norm_g[0], want_state=True)
    ml_s = mlstm_mix(u, gates, TP // LS, BS, LS, ml_norm_g[0],
                     state=(state_C[:, 0], state_n[:, 0], state_m[:, 0]))
    x = ffn_residual([hy, (ml_p, ml_s)], ev_w_out[0], (xp, xs), norm2_g[0:1], mods[0], ff_w1[0], ff_w3[0], ff_w2[0])

    q, kv = qkv_proj(x, norm1_g[1:2], mods[1], at_w_qkv[0])
    o_p, new_k, new_v = attention(q, kv, 0, at_q_g[0], at_k_g[0], at_sink[0], BP, LP)
    o_s = attention(q, kv, TP, at_q_g[0], at_k_g[0], at_sink[0], BS, LS, cache=(cache_k[:, 0], cache_v[:, 0]))
    x, lp, wts, runs, totals = moe_router([(o_p, o_s)], at_w_out[0], x, norm2_g[1:2], mods[1],
                                          moe_w_router[0], moe_b_router[0])
    run_tab, tail, tile_e, n_tiles, tile_rows, items = moe_layout(runs, totals)
    xsort = moe_dispatch(x, norm2_g[1:2], mods[1], lp, run_tab, tail, n_tiles)
    ysort = moe_grouped_swiglu(xsort, items, tile_e, n_tiles, tile_rows, moe_w1[0], moe_w3[0], moe_w2[0])
    yp, ys = moe_combine(x, mods[1], lp, wts, run_tab, ysort, TP)

    return (yp.reshape(BP, LP, D), ys.reshape(BS, LS, D),
            new_C[:, None], new_n[:, None], new_m[:, None], new_k[:, None], new_v[:, None])
```

```python
import functools
import math

import numpy as np
import jax
import jax.numpy as jnp
from jax import lax
from jax.experimental import pallas as pl
from jax.experimental.pallas import tpu as pltpu

F32 = jnp.float32
BF16 = jnp.bfloat16
HIGHEST = lax.Precision.HIGHEST

D = 1024
GROUP = 1024
N_PROMPT_GROUPS = 8
HY_W = 512
ML_HEADS = 4
ML_HD = 128
ML_CHUNK = 256
EVEN_MAIN = 3 * HY_W + 4 * 512
N_GATES = 16
ATT_HD = 64
ATT_HEADS = 16
ATT_KV = 4
WINDOW = 128
GRID_W = 64
ROPE_BASE = 10000.0
D_FF = 2816
N_EXPERTS = 8
EPS = 1e-6
NEG = -1e30
VMEM_LIMIT = 56 * 1024 * 1024


def _cparams(*sem):
    return pltpu.CompilerParams(dimension_semantics=sem, vmem_limit_bytes=VMEM_LIMIT)


def _mod_row(i, tm):
    return jnp.maximum(i * tm // GROUP - (N_PROMPT_GROUPS - 1), 0)


def _silu(x):
    return x * jax.nn.sigmoid(x)


def _bdot(a, b):
    return jnp.dot(a.astype(BF16), b.astype(BF16), preferred_element_type=F32)


def _norm_mod(x, g, sh, sc):
    y = x * lax.rsqrt(jnp.mean(x * x, axis=-1, keepdims=True) + EPS) * g
    return y * (1.0 + sc) + sh


def _adaln_kernel(c_ref, w_ref, b_ref, o_ref):
    s = _silu(c_ref[...])
    s_hi = s.astype(BF16)
    s_lo = (s - s_hi.astype(F32)).astype(BF16)
    w_hi = w_ref[...].astype(BF16)
    w_lo = (w_ref[...] - w_hi.astype(F32)).astype(BF16)
    dot = functools.partial(jnp.dot, preferred_element_type=F32)
    o_ref[...] = dot(s_hi, w_hi) + dot(s_lo, w_hi) + dot(s_hi, w_lo) + b_ref[...]


def adaln_table(cond, w_ada, b_ada):
    depth = w_ada.shape[0]
    tn = 1536
    out = pl.pallas_call(
        _adaln_kernel,
        grid=(depth, 6 * D // tn),
        in_specs=[pl.BlockSpec((16, D), lambda l, j: (0, 0)),
                  pl.BlockSpec((None, D, tn), lambda l, j: (l, 0, j)),
                  pl.BlockSpec((None, 1, tn), lambda l, j: (l, 0, j))],
        out_specs=pl.BlockSpec((None, 16, tn), lambda l, j: (l, 0, j)),
        out_shape=jax.ShapeDtypeStruct((depth, 16, 6 * D), F32),
        compiler_params=_cparams("parallel", "parallel"),
        name="adaln",
    )(cond, w_ada, b_ada.reshape(depth, 1, 6 * D))
    return out.reshape(depth, 16, 1, 6 * D)


def _same_tile(i):
    return i


def _part_specs(block, n_p, tile_of=_same_tile):
    return [pl.BlockSpec(block, lambda i, *_: (jnp.minimum(tile_of(i), n_p - 1), 0)),
            pl.BlockSpec(block, lambda i, *_: (jnp.maximum(tile_of(i) - n_p, 0), 0))]


def _pick(is_prompt, p_ref, s_ref):
    return jnp.where(is_prompt, p_ref[...], s_ref[...])


def _mod_spec(k, tm, tile_of=_same_tile):
    return pl.BlockSpec((None, 1, D), lambda i, *_: (_mod_row(tile_of(i), tm), 0, k))


def _log_sigmoid(x):
    return jnp.minimum(x, 0.0) - jnp.log(1.0 + jnp.exp(-jnp.abs(x)))


def _split3(x):
    hi = x.astype(BF16)
    r = x - hi.astype(F32)
    mid = r.astype(BF16)
    return hi, mid, (r - mid.astype(F32)).astype(BF16)


def _even_in_kernel(xp_ref, xs_ref, g_ref, sh_ref, sc_ref, w_ref, bg_ref, lo_ref, up_ref, u_ref, gate_ref, *, n_p, tn):
    is_prompt = pl.program_id(0) < n_p
    h = _norm_mod(_pick(is_prompt, xp_ref, xs_ref), g_ref[...], sh_ref[...], sc_ref[...]).astype(BF16)

    def main_chunks(j0, j1):
        for j in range(j0, j1):
            u_ref[:, j * tn:(j + 1) * tn] = _bdot(h, w_ref[:, j * tn:(j + 1) * tn]).astype(u_ref.dtype)

    n_main = EVEN_MAIN // tn
    gates = _bdot(h, w_ref[:, EVEN_MAIN:]) + bg_ref[...]
    main_chunks(0, n_main // 2)
    lf = _log_sigmoid(gates)
    col = lax.broadcasted_iota(jnp.int32, (1, N_GATES), 1)
    is_forget = (col // ML_HEADS) % 2 == 1
    is_rev = col >= N_GATES // 2
    for ch in range(h.shape[0] // ML_CHUNK):
        sl = slice(ch * ML_CHUNK, (ch + 1) * ML_CHUNK)
        parts = _split3(lf[sl])
        cf = sum(jnp.dot(lo_ref[...], p, preferred_element_type=F32) for p in parts)
        cr = sum(jnp.dot(up_ref[...], p, preferred_element_type=F32) for p in parts)
        gate_ref[sl, :] = jnp.where(is_forget, jnp.where(is_rev, cr, cf), gates[sl])
    main_chunks(n_main // 2, n_main)


def even_in_proj(xp, xs, g, mods, w_in, b_gate, tm=1024, tn=512):
    T = xp.shape[0] + xs.shape[0]
    tri = np.tril(np.ones((ML_CHUNK, ML_CHUNK), np.float32))
    lo, up = jnp.asarray(tri).astype(BF16), jnp.asarray(tri.T).astype(BF16)
    return pl.pallas_call(
        functools.partial(_even_in_kernel, n_p=xp.shape[0] // tm, tn=tn),
        grid=(T // tm,),
        in_specs=_part_specs((tm, D), xp.shape[0] // tm) + [
                  pl.BlockSpec((1, D), lambda i: (0, 0)),
                  _mod_spec(0, tm), _mod_spec(1, tm),
                  _const_spec(w_in),
                  pl.BlockSpec((1, N_GATES), lambda i: (0, 0)),
                  _const_spec(lo), _const_spec(up)],
        out_specs=[pl.BlockSpec((tm, EVEN_MAIN), lambda i: (i, 0)),
                   pl.BlockSpec((tm, N_GATES), lambda i: (i, 0))],
        out_shape=[jax.ShapeDtypeStruct((T, EVEN_MAIN), BF16),
                   jax.ShapeDtypeStruct((T, N_GATES), F32)],
        compiler_params=_cparams("parallel"),
        name="even_in_proj",
    )(xp, xs, g, mods, mods, w_in, b_gate, lo, up)


def _dft_tables(L):
    n = 2 * L
    f = np.arange(L, dtype=np.int64)[:, None]
    s = np.arange(L, dtype=np.int64)[None, :]
    ang = 2.0 * np.pi * ((f * s) % n).astype(np.float64) / n
    fwd = np.concatenate([np.cos(ang), -np.sin(ang)], axis=0)
    fwd[L, :] = np.where(np.arange(L) % 2 == 0, 1.0, -1.0)
    t = np.arange(L, dtype=np.int64)[:, None]
    ff = np.arange(L, dtype=np.int64)[None, :]
    ang = 2.0 * np.pi * ((t * ff) % n).astype(np.float64) / n
    inv_re = 2.0 * np.cos(ang) / n
    inv_re[:, 0] = 1.0 / n
    inv_im = -2.0 * np.sin(ang) / n
    inv_im[:, 0] = np.where(np.arange(L) % 2 == 0, 1.0, -1.0) / n
    inv = np.concatenate([inv_re, inv_im], axis=1)
    return fwd.astype(np.float32), inv.astype(np.float32)


def _filter_tables(L):
    t = np.linspace(0.0, 1.0, L, dtype=np.float32).astype(np.float64)[:, None]
    w = 2.0 * math.pi * np.arange(L, dtype=np.float64)[:, None] / L
    bands = np.linspace(1e-4, 16 - 1, 16, dtype=np.float32).astype(np.float64)[None, :]
    z = np.concatenate([t, np.cos(bands * w), -np.sin(bands * w)], axis=-1)
    zp = np.zeros((L, 128), np.float64)
    zp[:, :z.shape[1]] = z
    max_decay = math.log(1e-2) / 0.3
    min_decay = math.log(1e-2) / 1.5
    deltas = np.linspace(min_decay, max_decay, HY_W, dtype=np.float32).astype(np.float64)
    decay = np.exp(-t * np.abs(deltas))
    return zp.astype(np.float32), decay.astype(np.float32)


def _hy_filter_kernel(z_ref, dec_ref, w1_ref, b1_ref, w2_ref, b2_ref, w3_ref, fr_ref, fwd_ref,
                      ka_ref, kb_ref):
    L = z_ref.shape[0]
    hdot = functools.partial(jnp.dot, precision=HIGHEST, preferred_element_type=F32)
    h = jnp.sin(fr_ref[0:1, :] * (hdot(z_ref[...], w1_ref[...]) + b1_ref[...]))
    h = jnp.sin(fr_ref[1:2, :] * (hdot(h, w2_ref[...]) + b2_ref[...]))
    h = hdot(h, w3_ref[...])
    row0 = lax.broadcasted_iota(jnp.int32, (L, 1), 0) == 0
    h0 = h[:, :HY_W] * dec_ref[...]
    h1 = h[:, HY_W:] * dec_ref[...]
    l1 = jnp.sum(jnp.abs(h0), axis=0, keepdims=True) + jnp.sum(jnp.abs(h1), axis=0, keepdims=True)
    inv = 1.0 / l1
    h0 = h0 * inv
    h1 = jnp.where(row0, 0.0, h1 * inv)
    f0 = _bdot(fwd_ref[...], h0)
    f1 = _bdot(fwd_ref[...], h1)
    ka_ref[...] = f0[:L] + f1[:L]
    kb_ref[...] = jnp.where(row0, f0[L:] + f1[L:], f0[L:] - f1[L:])


def _const_spec(a):
    return pl.BlockSpec(a.shape, lambda *_: (0,) * a.ndim, pipeline_mode=pl.Buffered(1))


def hyena_filter_spectra(L, w1, b1, w2, b2, w3, freq, fwd):
    z, dec = _filter_tables(L)
    pad2 = lambda a, r, c: jnp.pad(a, ((0, r - a.shape[0]), (0, c - a.shape[1])))
    args = (jnp.asarray(z), jnp.asarray(dec), pad2(w1, 128, 128), pad2(b1[None], 1, 128),
            pad2(w2, 128, 128), pad2(b2[None], 1, 128), pad2(w3, 128, 4 * HY_W), pad2(freq, 2, 128), fwd)
    in_specs = [_const_spec(a) for a in args]
    in_specs[6] = pl.BlockSpec((128, 2 * HY_W), lambda o: (0, o))
    shp = jax.ShapeDtypeStruct((2, L, HY_W), F32)
    out_spec = pl.BlockSpec((None, L, HY_W), lambda o: (o, 0, 0))
    return pl.pallas_call(
        _hy_filter_kernel,
        grid=(2,),
        in_specs=in_specs,
        out_specs=[out_spec, out_spec],
        out_shape=[shp, shp],
        compiler_params=_cparams("arbitrary"),
        name=f"hyena_filter_{L}",
    )(*args)


def _hyena_kernel(u_ref, cw_ref, d_ref, fwd_ref, inv_ref, ka_ref, kb_ref, o_ref):
    nb, L = u_ref.shape[0], u_ref.shape[1]
    row = lax.broadcasted_iota(jnp.int32, (L, 1), 0)
    first, last = row == 0, row == L - 1
    fwd = fwd_ref[...].astype(BF16)
    inv = inv_ref[...].astype(BF16)

    def long_conv(z, o):
        zf = jnp.dot(fwd, z.astype(BF16), preferred_element_type=F32)
        a, b = zf[:L], zf[L:]
        ka, kb = ka_ref[o], kb_ref[o]
        yr = a * ka - jnp.where(first, 0.0, b * kb)
        yi = jnp.where(first, b * kb, a * kb + b * ka)
        return (jnp.dot(inv[:, :L], yr.astype(BF16), preferred_element_type=F32)
                + jnp.dot(inv[:, L:], yi.astype(BF16), preferred_element_type=F32))

    for bi in range(nb):
        u = u_ref[bi].astype(F32)
        prev = jnp.where(first, 0.0, pltpu.roll(u, 1, 0))
        nxt = jnp.where(last, 0.0, pltpu.roll(u, L - 1, 0))
        u = prev * cw_ref[0:1, :] + u * cw_ref[1:2, :] + nxt * cw_ref[2:3, :]
        v, x1, x2 = u[:, :HY_W], u[:, HY_W:2 * HY_W], u[:, 2 * HY_W:]
        z = x1 * (long_conv(v, 0) + d_ref[0:1, :] * v)
        z = x2 * (long_conv(z, 1) + d_ref[1:2, :] * z)
        o_ref[bi] = z.astype(o_ref.dtype)


def hyena_mix(u, seq0, B, L, conv_w, d_skip, fwd, inv, ka, kb, nb):
    u3 = u.reshape(-1, L, EVEN_MAIN)
    full = _const_spec
    out = pl.pallas_call(
        _hyena_kernel,
        grid=(B // nb,),
        in_specs=[pl.BlockSpec((nb, L, 3 * HY_W), lambda b: (b + seq0 // nb, 0, 0)),
                  full(conv_w), full(d_skip), full(fwd), full(inv), full(ka), full(kb)],
        out_specs=pl.BlockSpec((nb, L, HY_W), lambda b: (b, 0, 0)),
        out_shape=jax.ShapeDtypeStruct((B, L, HY_W), BF16),
        compiler_params=_cparams("parallel"),
        name=f"hyena_{L}",
    )(u3, conv_w, d_skip, fwd, inv, ka, kb)
    return out.reshape(B * L, HY_W)


def _mlstm_kernel(*refs, has_state, want_state):
    q_ref, k_ref, v_ref, o_ref, gc_ref, gr_ref, ng_ref = refs[:7]
    refs = refs[7:]
    if has_state:
        c0t_ref, n0b_ref, m0_ref = refs[:3]
        refs = refs[3:]
    y_ref = refs[0]
    if want_state:
        c_out, n_out, m_out = refs[1:4]
    L, d = q_ref.shape[0], ML_HD
    T = min(ML_CHUNK, L)
    nc = L // T
    scale = 1.0 / math.sqrt(d)
    nt = (((1,), (1,)), ((), ()))
    si = lax.broadcasted_iota(jnp.int32, (T, T), 0)
    ti = lax.broadcasted_iota(jnp.int32, (T, T), 1)
    allowed = (si <= ti, si >= ti)
    chains = [(dr, h) for dr in range(2) for h in range(ML_HEADS)]
    gcol = lambda dr, gi, h: dr * 2 * ML_HEADS + gi * ML_HEADS + h

    caug_t, m = {}, {}
    for ch in chains:
        dr, h = ch
        if has_state:
            caug_t[ch] = jnp.concatenate([c0t_ref[dr, h], n0b_ref[dr, h]], axis=0)
            m[ch] = m0_ref[dr, h:h + 1, 0:1]
        else:
            caug_t[ch], m[ch] = jnp.zeros((2 * d, d), F32), jnp.zeros((1, 1), F32)

    chunk_cache = {}

    def chunk_data(h, j):
        if (h, j) not in chunk_cache:
            sl, hl = slice(j * T, (j + 1) * T), slice(h * d, (h + 1) * d)
            q = q_ref[sl, hl]
            ks = (k_ref[sl, hl].astype(F32) * scale).astype(BF16)
            v_t = v_ref[sl, hl].astype(F32).T
            vaug_t = jnp.concatenate([v_t, jnp.ones((d, T), F32)], axis=0).astype(BF16)
            s_raw = lax.dot_general(ks, q, nt, preferred_element_type=F32)
            chunk_cache[(h, j)] = (q, ks, v_t, vaug_t, s_raw)
        return chunk_cache[(h, j)]

    h_sum = {}
    for it in range(nc):
        step = {ch: (it if ch[0] == 0 else nc - 1 - it) for ch in chains}
        data = {ch: chunk_data(ch[1], step[ch]) for ch in chains}
        inter_t = {ch: lax.dot_general(caug_t[ch].astype(BF16), data[ch][0], nt, preferred_element_type=F32)
                   for ch in chains}
        gate = {}
        for ch in chains:
            dr, h = ch
            sl = slice(step[ch] * T, (step[ch] + 1) * T)
            li_r, b_r = gr_ref[gcol(dr, 0, h):gcol(dr, 0, h) + 1, sl], gr_ref[gcol(dr, 1, h):gcol(dr, 1, h) + 1, sl]
            src = gc_ref[sl, gcol(dr, 0, h):gcol(dr, 0, h) + 1] - gc_ref[sl, gcol(dr, 1, h):gcol(dr, 1, h) + 1]
            dm = jnp.where(allowed[dr], src + b_r, NEG)
            inter = b_r + m[ch]
            m_t = jnp.maximum(inter, jnp.max(dm, axis=0, keepdims=True))
            b_end = b_r[:, T - 1:T] if dr == 0 else b_r[:, 0:1]
            g_r = b_end - b_r + li_r
            m_new = jnp.maximum(b_end + m[ch], jnp.max(g_r, axis=1, keepdims=True))
            gate[ch] = (jnp.exp(dm - m_t), jnp.exp(inter - m_t), jnp.exp(-m_t), jnp.exp(g_r - m_new),
                        jnp.exp(b_end + m[ch] - m_new), m_new)
        for ch in chains:
            q, ks, v_t, vaug_t, s_raw = data[ch]
            w_intra, w_inter, floor, w_tok, decay, m_new = gate[ch]
            acc = jnp.dot(vaug_t, (s_raw * w_intra).astype(BF16), preferred_element_type=F32) + w_inter * inter_t[ch]
            h_t = acc[:d] / jnp.maximum(jnp.abs(acc[d:]), floor)
            key = (ch[1], step[ch])
            h_sum[key] = h_t if key not in h_sum else h_sum[key] + h_t
            vw_t = jnp.concatenate([v_t * w_tok, jnp.broadcast_to(w_tok, (d, T))], axis=0).astype(BF16)
            caug_t[ch] = decay * caug_t[ch] + jnp.dot(vw_t, ks, preferred_element_type=F32)
            m[ch] = m_new

    if want_state:
        for ch in chains:
            dr, h = ch
            c_out[dr, h] = caug_t[ch][:d].T
            n_out[dr, h:h + 1, :] = caug_t[ch][d:d + 1, :]
            m_out[dr, h:h + 1, :] = jnp.broadcast_to(m[ch], (1, d))
    for h in range(ML_HEADS):
        for j in range(nc):
            sl, hl = slice(j * T, (j + 1) * T), slice(h * d, (h + 1) * d)
            hv = h_sum[(h, j)].T
            y = hv * lax.rsqrt(jnp.mean(hv * hv, axis=-1, keepdims=True) + EPS) * ng_ref[:, hl]
            y_ref[sl, hl] = (y * jax.nn.sigmoid(o_ref[sl, hl].astype(F32))).astype(y_ref.dtype)


def mlstm_mix(u, gates, seq0, B, L, norm_g, state=None, want_state=False):
    u3 = u.reshape(-1, L, EVEN_MAIN)
    gc = gates.reshape(-1, L, N_GATES)[seq0:seq0 + B]
    gr = gc.transpose(0, 2, 1)
    width = ML_HEADS * ML_HD
    col = lambda i: pl.BlockSpec((None, L, width), lambda b: (b + seq0, 0, (3 * HY_W + i * width) // width))
    in_specs = [col(0), col(1), col(2), col(3),
                pl.BlockSpec((None, L, N_GATES), lambda b: (b, 0, 0)),
                pl.BlockSpec((None, N_GATES, L), lambda b: (b, 0, 0)),
                pl.BlockSpec((1, width), lambda b: (0, 0))]
    args = [u3, u3, u3, u3, gc, gr, norm_g.reshape(1, width)]
    sspec = pl.BlockSpec((None, 2, ML_HEADS, ML_HD, ML_HD), lambda b: (b, 0, 0, 0, 0))
    vspec = pl.BlockSpec((None, 2, ML_HEADS, ML_HD), lambda b: (b, 0, 0, 0))
    if state is not None:
        C0, n0, m0 = state
        in_specs += [sspec, sspec, vspec]
        args += [C0.swapaxes(-1, -2), jnp.broadcast_to(n0[..., None, :], C0.shape),
                 jnp.broadcast_to(m0[..., None], n0.shape)]
    out_specs = [pl.BlockSpec((None, L, width), lambda b: (b, 0, 0))]
    out_shape = [jax.ShapeDtypeStruct((B, L, width), BF16)]
    if want_state:
        out_specs += [pl.BlockSpec((None, None, 2, ML_HEADS, ML_HD, ML_HD), lambda b: (b, 0, 0, 0, 0, 0)),
                      pl.BlockSpec((None, None, 2, ML_HEADS, ML_HD), lambda b: (b, 0, 0, 0, 0)),
                      pl.BlockSpec((None, None, 2, ML_HEADS, ML_HD), lambda b: (b, 0, 0, 0, 0))]
        out_shape += [jax.ShapeDtypeStruct((B, 1, 2, ML_HEADS, ML_HD, ML_HD), F32),
                      jax.ShapeDtypeStruct((B, 1, 2, ML_HEADS, ML_HD), F32),
                      jax.ShapeDtypeStruct((B, 1, 2, ML_HEADS, ML_HD), F32)]
    outs = pl.pallas_call(
        functools.partial(_mlstm_kernel, has_state=state is not None, want_state=want_state),
        grid=(B,),
        in_specs=in_specs, out_specs=out_specs, out_shape=out_shape,
        compiler_params=_cparams("parallel"),
        name=f"mlstm_{L}",
    )(*args)
    y = outs[0].reshape(B * L, width)
    if not want_state:
        return y
    _, C, n, m = outs
    return y, C, n, m[..., 0]


def _proj_inputs(acts, w, x, mods, tm, tile_of=_same_tile):
    xs = tuple(x) if isinstance(x, (tuple, list)) else (x,)
    n_p = acts[0][0].shape[0] // tm
    specs = []
    for pair in acts:
        specs += _part_specs((tm, pair[0].shape[1]), n_p, tile_of)
    specs.append(pl.BlockSpec(w.shape, lambda *_: (0, 0), pipeline_mode=pl.Buffered(1)))
    specs += (_part_specs((tm, D), n_p, tile_of) if len(xs) == 2
              else [pl.BlockSpec((tm, D), lambda i, *_: (tile_of(i), 0))])
    specs.append(_mod_spec(2, tm, tile_of))
    args = [a for pair in acts for a in pair] + [w, *xs, mods]
    return specs, args, dict(n_in=len(acts), n_x=len(xs), n_p=n_p)


def _proj_value(refs, is_prompt, n_in, n_x, n_p):
    a_refs = refs[:2 * n_in]
    w_ref = refs[2 * n_in]
    x_refs = refs[2 * n_in + 1:2 * n_in + 1 + n_x]
    gate_ref = refs[2 * n_in + 1 + n_x]
    k0 = 0
    acc = None
    for j in range(n_in):
        a = _pick(is_prompt, a_refs[2 * j], a_refs[2 * j + 1])
        kw = a.shape[1]
        part = _bdot(a, w_ref[k0:k0 + kw, :])
        acc = part if acc is None else acc + part
        k0 += kw
    x = _pick(is_prompt, *x_refs) if n_x == 2 else x_refs[0][...]
    return x + gate_ref[...] * acc


def _n_proj_refs(n_in, n_x, n_p):
    return 2 * n_in + 1 + n_x + 1


SWIGLU_ROWS = 512


def _swiglu_accumulate(h_scr, acc_scr, w1_ref, w3_ref, w2_ref, rows, scale=None):
    w1, w3, w2 = w1_ref[...].astype(BF16), w3_ref[...].astype(BF16), w2_ref[...].astype(BF16)
    groups = [slice(r, r + SWIGLU_ROWS) for r in range(0, rows, SWIGLU_ROWS)]
    ups = []
    for sl in groups:
        h = h_scr[sl, :]
        ups.append((jnp.dot(h, w1, preferred_element_type=F32), jnp.dot(h, w3, preferred_element_type=F32)))
    for sl, (a, b) in zip(groups, ups):
        mid = (_silu(a) * b).astype(BF16)
        down = jnp.dot(mid, w2, preferred_element_type=F32)
        acc_scr[sl, :] += down if scale is None else scale * down


def _ffn_kernel(*refs, proj, nc):
    n = _n_proj_refs(**proj)
    g_ref, sh_ref, sc_ref, gate_ref, w1_ref, w3_ref, w2_ref, o_ref, h_scr, b1_scr, b3_scr, b2_scr = refs[n:]
    i = pl.program_id(0)
    tile = jnp.maximum(i - (nc - 1), 0)
    is_prompt = tile < proj["n_p"]
    rows = h_scr.shape[0]

    def start_tile():
        x = _proj_value(refs[:n], is_prompt, **proj)
        o_ref[...] = x
        h_scr[...] = _norm_mod(x, g_ref[...], sh_ref[...], sc_ref[...]).astype(BF16)

    def chunk(c):
        _swiglu_accumulate(h_scr, o_ref, b1_scr.at[c], b3_scr.at[c], b2_scr.at[c], rows, scale=gate_ref[...])

    pl.when(i == 0)(start_tile)

    @pl.when(i < nc)
    def _():
        b1_scr[i] = w1_ref[...].astype(BF16)
        b3_scr[i] = w3_ref[...].astype(BF16)
        b2_scr[i] = w2_ref[...].astype(BF16)
        chunk(i)

    @pl.when(i >= nc)
    def _():
        start_tile()
        for c in range(nc):
            chunk(c)


def ffn_residual(acts, w_out, x, g, mods, w1, w3, w2, tm=512, tf=256):
    T = sum(a.shape[0] for a in acts[0])
    nc = D_FF // tf
    tile_of = lambda i: jnp.maximum(i - (nc - 1), 0)
    chunk_of = lambda i: jnp.minimum(i, nc - 1)
    p_specs, p_args, proj = _proj_inputs(acts, w_out, x, mods, tm, tile_of)
    return pl.pallas_call(
        functools.partial(_ffn_kernel, proj=proj, nc=nc),
        grid=(T // tm + nc - 1,),
        in_specs=p_specs + [
                  pl.BlockSpec((1, D), lambda i: (0, 0)),
                  _mod_spec(3, tm, tile_of), _mod_spec(4, tm, tile_of), _mod_spec(5, tm, tile_of),
                  pl.BlockSpec((D, tf), lambda i: (0, chunk_of(i))),
                  pl.BlockSpec((D, tf), lambda i: (0, chunk_of(i))),
                  pl.BlockSpec((tf, D), lambda i: (chunk_of(i), 0))],
        out_specs=pl.BlockSpec((tm, D), lambda i: (tile_of(i), 0)),
        out_shape=jax.ShapeDtypeStruct((T, D), F32),
        scratch_shapes=[pltpu.VMEM((tm, D), BF16), pltpu.VMEM((nc, D, tf), BF16),
                        pltpu.VMEM((nc, D, tf), BF16), pltpu.VMEM((nc, tf, D), BF16)],
        compiler_params=_cparams("arbitrary"),
        name="ffn",
    )(*p_args, g, mods, mods, mods, w1, w3, w2)


def _qkv_kernel(x_ref, g_ref, sh_ref, sc_ref, w_ref, q_ref, kv_ref):
    h = _norm_mod(x_ref[...], g_ref[...], sh_ref[...], sc_ref[...]).astype(BF16)
    nq = q_ref.shape[1]
    q_ref[...] = _bdot(h, w_ref[:, :nq]).astype(q_ref.dtype)
    kv_ref[...] = _bdot(h, w_ref[:, nq:])


def qkv_proj(x, g, mods, w_qkv, tm=1024):
    T = x.shape[0]
    nq, nkv = ATT_HEADS * ATT_HD, 2 * ATT_KV * ATT_HD
    return pl.pallas_call(
        _qkv_kernel,
        grid=(T // tm,),
        in_specs=[pl.BlockSpec((tm, D), lambda i: (i, 0)),
                  pl.BlockSpec((1, D), lambda i: (0, 0)),
                  _mod_spec(0, tm), _mod_spec(1, tm),
                  _const_spec(w_qkv)],
        out_specs=[pl.BlockSpec((tm, nq), lambda i: (i, 0)),
                   pl.BlockSpec((tm, nkv), lambda i: (i, 0))],
        out_shape=[jax.ShapeDtypeStruct((T, nq), BF16), jax.ShapeDtypeStruct((T, nkv), F32)],
        compiler_params=_cparams("parallel"),
        name="qkv_proj",
    )(x, g, mods, mods, w_qkv)


def _rope_tables(L):
    half = ATT_HD // 2
    pos_r = (np.arange(L) // GRID_W).astype(np.float32)
    pos_c = (np.arange(L) % GRID_W).astype(np.float32)
    inv = (ROPE_BASE ** (-np.arange(0, half, 2, dtype=np.float32) / half)).astype(np.float32)
    cos = np.zeros((L, ATT_HD), np.float64)
    sin = np.zeros((L, ATT_HD), np.float64)
    for base, pos in ((0, pos_r), (half, pos_c)):
        ang = (pos[:, None] * inv[None, :]).astype(np.float32).astype(np.float64)
        cos[:, base:base + half] = np.concatenate([np.cos(ang), np.cos(ang)], axis=1)
        sin[:, base:base + half] = np.concatenate([-np.sin(ang), np.sin(ang)], axis=1)
    return (np.tile(cos, (1, 4)).astype(np.float32), np.tile(sin, (1, 4)).astype(np.float32))


def _seg_rms(x):
    w = x.shape[1]
    ri = lax.broadcasted_iota(jnp.int32, (w, w), 0) // ATT_HD
    ci = lax.broadcasted_iota(jnp.int32, (w, w), 1) // ATT_HD
    ss = _bdot(x * x, (ri == ci).astype(F32))
    return x * lax.rsqrt(ss * (1.0 / ATT_HD) + EPS)


LOG2E = 1.4426950408889634
ATTN_LOOKAHEAD = 1


def _exp2_bf16(x):
    return jnp.exp2(x.astype(BF16))


def _both_halves(tile, low):
    lane = lax.broadcasted_iota(jnp.int32, tile.shape, 1)
    other = pltpu.roll(tile, ATT_HD, 1)
    return jnp.where((lane < ATT_HD) == low, tile, other)


def _swap16(x):
    w = x.shape[1]
    lane = lax.broadcasted_iota(jnp.int32, x.shape, 1)
    return jnp.where(lane % 32 < 16, pltpu.roll(x, w - 16, 1), pltpu.roll(x, 16, 1))


def _attn_kernel(*refs, latent, tq):
    if latent:
        (q_ref, kv_ref, ck_ref, cv_ref, qg_ref, kg_ref, sink_ref, cosq_ref, sinq_ref, cosk_ref, sink_t_ref,
         o_ref, kk_scr, vt_scr, ckk_scr, cvt_scr) = refs
    else:
        q_ref, kv_ref, qg_ref, kg_ref, sink_ref, o_ref, ko_ref, vo_ref, kk_scr, vt_scr = refs
    L = kv_ref.shape[0]
    gw = ATT_KV * ATT_HD
    pw = 2 * ATT_HD
    qb = pl.program_id(1)

    vrows = vt_scr.shape[2]
    nblk = L // pw

    def vt_aug(tile, low):
        vt = tile.T[0:ATT_HD, :] if low else tile.T[ATT_HD:, :]
        return jnp.concatenate([vt, jnp.ones((vrows - ATT_HD, tile.shape[0]), F32)], axis=0).astype(BF16)

    @pl.when(qb == 0)
    def _():
        kn = _seg_rms(kv_ref[:, :gw]) * kg_ref[...]
        v = kv_ref[:, gw:]
        if latent:
            kn = kn * cosk_ref[...] + _swap16(kn) * sink_t_ref[...]
        else:
            for c in range(ATT_KV):
                ko_ref[c] = kn[:, c * ATT_HD:(c + 1) * ATT_HD]
                vo_ref[c] = v[:, c * ATT_HD:(c + 1) * ATT_HD]
        for c in range(ATT_KV):
            tile, low = slice((c // 2) * pw, (c // 2 + 1) * pw), c % 2 == 0
            kk_scr[c] = _both_halves(kn[:, tile], low).astype(BF16)
            for j in range(nblk):
                vt_scr[c, j] = vt_aug(v[j * pw:(j + 1) * pw, tile], low)
            if latent:
                ck, cv = ck_ref[c], cv_ref[c]
                ckk_scr[c] = jnp.concatenate([ck, ck], axis=1).astype(BF16)
                cvt_scr[c] = vt_aug(jnp.concatenate([cv, cv], axis=1), True)

    if latent:
        span = tq + 2 * WINDOW
        start = pl.multiple_of(jnp.clip(qb * tq - WINDOW, 0, L - span), WINDOW)
        blk0 = start // pw
        s_pos = start + lax.broadcasted_iota(jnp.int32, (span, tq), 0)
        t_pos = qb * tq + lax.broadcasted_iota(jnp.int32, (span, tq), 1)
        win_bias = jnp.where(jnp.abs(t_pos - s_pos) <= WINDOW, 0.0, NEG)
    else:
        span, blk0 = L, 0

    nt = (((1,), (1,)), ((), ()))
    low_q = lax.broadcasted_iota(jnp.int32, (tq, pw), 1) < ATT_HD
    def group_scores(c):
        qc = _seg_rms(q_ref[:, c * gw:(c + 1) * gw].astype(F32)) * qg_ref[...]
        if latent:
            qc = qc * cosq_ref[...] + _swap16(qc) * sinq_ref[...]
            kw = kk_scr[c, pl.ds(start, span), :]
        else:
            kw = kk_scr[c]
        qc = qc * (LOG2E / math.sqrt(ATT_HD))
        scores = []
        for g in range(ATT_KV):
            qt = qc[:, (g // 2) * pw:(g // 2 + 1) * pw]
            qm = jnp.where(low_q if g % 2 == 0 else ~low_q, qt, 0.0).astype(BF16)
            lw = lax.dot_general(kw, qm, nt, preferred_element_type=F32)
            lc = lax.dot_general(ckk_scr[c], qm, nt, preferred_element_type=F32) if latent else None
            scores.append((lw, lc))
        return scores

    def group_outputs(c, scores):
        vw = jnp.concatenate([vt_scr[c, blk0 + j] for j in range(span // pw)], axis=1)
        outs = []
        for g in range(ATT_KV):
            head = c * ATT_KV + g
            sink = sink_ref[:, head:head + 1] * LOG2E
            lw, lc = scores[g]
            if latent:
                lw = lw + win_bias
                mx = jnp.maximum(jnp.maximum(jnp.max(lw, axis=0, keepdims=True),
                                             jnp.max(lc, axis=0, keepdims=True)), sink)
                r = jnp.dot(vw, _exp2_bf16(lw - mx), preferred_element_type=F32) + jnp.dot(
                    cvt_scr[c], _exp2_bf16(lc - mx), preferred_element_type=F32)
            else:
                mx = jnp.maximum(jnp.max(lw, axis=0, keepdims=True), sink)
                r = jnp.dot(vw, _exp2_bf16(lw - mx), preferred_element_type=F32)
            den = r[ATT_HD:ATT_HD + 1, :] + jnp.exp2(sink - mx)
            outs.append(r[0:ATT_HD, :] / den)
        for t in range(2):
            o_ref[:, c * gw + t * pw:c * gw + (t + 1) * pw] = (
                jnp.concatenate(outs[2 * t:2 * t + 2], axis=0).T.astype(o_ref.dtype))

    pending = [group_scores(c) for c in range(ATTN_LOOKAHEAD)]
    for c in range(ATT_KV):
        if c + ATTN_LOOKAHEAD < ATT_KV:
            pending.append(group_scores(c + ATTN_LOOKAHEAD))
        group_outputs(c, pending.pop(0))


def attention(q, kv, row0, q_g, k_g, sink, B, L, cache=None, tq=256):
    latent = cache is not None
    gw = ATT_KV * ATT_HD
    qg = jnp.tile(q_g, ATT_KV)[None]
    kg = jnp.tile(k_g, ATT_KV)[None]
    nq = L // tq
    const = lambda a: pl.BlockSpec(a.shape, lambda b, i: (0,) * a.ndim)
    in_specs = [pl.BlockSpec((tq, ATT_HEADS * ATT_HD), lambda b, i: (row0 // tq + b * nq + i, 0)),
                pl.BlockSpec((L, 2 * gw), lambda b, i: (row0 // L + b, 0))]
    args = [q, kv]
    vrows = ATT_HD + 16
    scratch = [pltpu.VMEM((ATT_KV, L, 2 * ATT_HD), BF16),
               pltpu.VMEM((ATT_KV, L // (2 * ATT_HD), vrows, 2 * ATT_HD), BF16)]
    out_specs = [pl.BlockSpec((tq, ATT_HEADS * ATT_HD), lambda b, i: (b * nq + i, 0))]
    out_shape = [jax.ShapeDtypeStruct((B * L, ATT_HEADS * ATT_HD), BF16)]
    if latent:
        ck, cv = cache
        P = ck.shape[2]
        cos, sin = (jnp.asarray(t) for t in _rope_tables(L))
        in_specs += [pl.BlockSpec((None, ATT_KV, P, ATT_HD), lambda b, i: (b, 0, 0, 0))] * 2
        args += [ck, cv]
        in_specs += [const(qg), const(kg), pl.BlockSpec((1, ATT_HEADS), lambda b, i: (0, 0)),
                     pl.BlockSpec((tq, gw), lambda b, i: (i, 0)), pl.BlockSpec((tq, gw), lambda b, i: (i, 0)),
                     const(cos), const(sin)]
        args += [qg, kg, sink[None], cos, sin, cos, sin]
        scratch += [pltpu.VMEM((ATT_KV, P, 2 * ATT_HD), BF16), pltpu.VMEM((ATT_KV, vrows, P), BF16)]
    else:
        in_specs += [const(qg), const(kg), pl.BlockSpec((1, ATT_HEADS), lambda b, i: (0, 0))]
        args += [qg, kg, sink[None]]
        cache_spec = pl.BlockSpec((None, None, ATT_KV, L, ATT_HD), lambda b, i: (b, 0, 0, 0, 0))
        out_specs += [cache_spec, cache_spec]
        out_shape += [jax.ShapeDtypeStruct((B, 1, ATT_KV, L, ATT_HD), F32)] * 2
    outs = pl.pallas_call(
        functools.partial(_attn_kernel, latent=latent, tq=tq),
        grid=(B, nq),
        in_specs=in_specs, out_specs=out_specs, out_shape=out_shape,
        scratch_shapes=scratch,
        compiler_params=_cparams("parallel", "arbitrary"),
        name="attn_latent" if latent else "attn_context",
    )(*args)
    return outs[0] if latent else outs


MOE_TM = 1024
MOE_TOK = 1024
RUN_ALIGN = 16
MOE_LOCAL = 2 * MOE_TOK + N_EXPERTS * RUN_ALIGN
MOE_MAX_TILES = (2 * 16384 + (16384 // MOE_TOK) * N_EXPERTS * (RUN_ALIGN - 1)) // MOE_TM + N_EXPERTS + 1
RUN_SIZES = tuple(RUN_ALIGN << b for b in range(7, -1, -1))
MOE_CHUNKS = 11
MOE_MAX_ITEMS = N_EXPERTS * MOE_CHUNKS + MOE_MAX_TILES - N_EXPERTS
ITEM_FULL, ITEM_DEAD, ITEM_NONE = -1, -2, -3


def _router_kernel(*refs, proj):
    n = _n_proj_refs(**proj)
    g_ref, sh_ref, sc_ref, wr_ref, br_ref, tri_ref, x_ref, lp_ref, wts_ref, runs_ref, cnt_ref = refs[n:]

    @pl.when(pl.program_id(0) == 0)
    def _():
        cnt_ref[...] = jnp.zeros_like(cnt_ref)

    x = _proj_value(refs[:n], pl.program_id(0) < proj["n_p"], **proj)
    x_ref[...] = x
    h = _norm_mod(x, g_ref[...], sh_ref[...], sc_ref[...])
    nt = (((1,), (1,)), ((), ()))
    h_hi = h.astype(BF16)
    h_lo = (h - h_hi.astype(F32)).astype(BF16)
    w_hi = wr_ref[...].astype(BF16)
    w_lo = (wr_ref[...] - w_hi.astype(F32)).astype(BF16)
    lg = (lax.dot_general(w_hi, h_hi, nt, preferred_element_type=F32)
          + lax.dot_general(w_hi, h_lo, nt, preferred_element_type=F32)
          + lax.dot_general(w_lo, h_hi, nt, preferred_element_type=F32)) + br_ref[...]
    row = lax.broadcasted_iota(jnp.int32, lg.shape, 0)
    m1 = jnp.max(lg, axis=0, keepdims=True)
    i1 = jnp.min(jnp.where(lg == m1, row, N_EXPERTS), axis=0, keepdims=True)
    l2 = jnp.where(row == i1, -jnp.inf, lg)
    m2 = jnp.max(l2, axis=0, keepdims=True)
    i2 = jnp.min(jnp.where(l2 == m2, row, N_EXPERTS), axis=0, keepdims=True)
    e2 = jnp.exp(m2 - m1)
    w1 = 1.0 / (1.0 + e2)
    wts_ref[...] = jnp.concatenate([w1, e2 * w1], axis=0)
    oh1 = (row == i1).astype(F32)
    oh2 = (row == i2).astype(F32)
    cs1 = _bdot(oh1, tri_ref[...])
    cs2 = _bdot(oh2, tri_ref[...])
    tot1 = jnp.sum(oh1, axis=1, keepdims=True)
    run = jnp.ceil((tot1 + jnp.sum(oh2, axis=1, keepdims=True)) * (1.0 / RUN_ALIGN)) * RUN_ALIGN
    run_b = jnp.broadcast_to(run, (N_EXPERTS, 128))
    er = lax.broadcasted_iota(jnp.int32, (N_EXPERTS, N_EXPERTS), 0)
    ec = lax.broadcasted_iota(jnp.int32, (N_EXPERTS, N_EXPERTS), 1)
    start = jnp.dot((ec < er).astype(F32), run_b, precision=HIGHEST, preferred_element_type=F32)
    last = lax.broadcasted_iota(jnp.int32, (N_EXPERTS, 128), 0) == N_EXPERTS - 1
    run_b = jnp.where(last, MOE_LOCAL - start, run_b)
    st = start[:, 0:1]
    p1 = jnp.sum(oh1 * (st + cs1), axis=0, keepdims=True)
    p2 = jnp.sum(oh2 * (st + tot1 + cs2), axis=0, keepdims=True)
    lp_ref[...] = jnp.concatenate([p1, p2], axis=0).astype(jnp.int32)
    lane = lax.broadcasted_iota(jnp.int32, (N_EXPERTS, 128), 1)
    runs_ref[...] = jnp.where(lane == 0, run_b, jnp.where(lane == 1, start, cnt_ref[...]))
    cnt_ref[...] = cnt_ref[...] + run_b


def moe_router(acts, w_out, x, g, mods, w_router, b_router, tm=MOE_TOK):
    T = x.shape[0]
    tri = jnp.asarray(np.triu(np.ones((tm, tm), np.float32), k=1)).astype(BF16)
    tok2 = lambda dt: jax.ShapeDtypeStruct((2, T), dt)
    p_specs, p_args, proj = _proj_inputs(acts, w_out, x, mods, tm)
    return pl.pallas_call(
        functools.partial(_router_kernel, proj=proj),
        grid=(T // tm,),
        in_specs=p_specs + [
                  pl.BlockSpec((1, D), lambda i: (0, 0)),
                  _mod_spec(3, tm), _mod_spec(4, tm),
                  pl.BlockSpec((N_EXPERTS, D), lambda i: (0, 0)),
                  pl.BlockSpec((N_EXPERTS, 1), lambda i: (0, 0)),
                  _const_spec(tri)],
        out_specs=[pl.BlockSpec((tm, D), lambda i: (i, 0)),
                   pl.BlockSpec((2, tm), lambda i: (0, i)),
                   pl.BlockSpec((2, tm), lambda i: (0, i)),
                   pl.BlockSpec((None, N_EXPERTS, 128), lambda i: (i, 0, 0)),
                   pl.BlockSpec((N_EXPERTS, 128), lambda i: (0, 0))],
        out_shape=[jax.ShapeDtypeStruct((T, D), F32), tok2(jnp.int32), tok2(F32),
                   jax.ShapeDtypeStruct((T // tm, N_EXPERTS, 128), F32),
                   jax.ShapeDtypeStruct((N_EXPERTS, 128), F32)],
        compiler_params=_cparams("arbitrary"),
        name="moe_router",
    )(*p_args, g, mods, mods, w_router.T, b_router[:, None], tri)


def moe_layout(runs, totals):
    rows = totals[:, 0].astype(jnp.int32)
    tiles = (rows + MOE_TM - 1) // MOE_TM
    tile_end = jnp.cumsum(tiles)
    group = (tile_end - tiles) * MOE_TM
    run_len = runs[:, :, 0].astype(jnp.int32)
    run_src = runs[:, :, 1].astype(jnp.int32)
    run_dst = group[None, :] + runs[:, :, 2].astype(jnp.int32)
    tail = jnp.stack([group + rows, tiles * MOE_TM - rows]).astype(jnp.int32)
    n_tiles = tile_end[-1]
    t = jnp.arange(MOE_MAX_TILES, dtype=jnp.int32)
    tile_e = jnp.sum(t[:, None] >= tile_end[None, :], axis=1).astype(jnp.int32)
    last_e = jnp.sum((n_tiles - 1) >= tile_end).astype(jnp.int32)
    tile_e = jnp.where(t < n_tiles, tile_e, last_e)
    first = jnp.sum(jnp.where(tile_e[:, None] == jnp.arange(N_EXPERTS), (tile_end - tiles)[None, :], 0), axis=1)
    e_rows = jnp.sum(jnp.where(tile_e[:, None] == jnp.arange(N_EXPERTS), rows[None, :], 0), axis=1)
    tile_rows = jnp.where(t < n_tiles, jnp.clip(e_rows - (t - first) * MOE_TM, 0, MOE_TM), 0).astype(jnp.int32)
    run_tab = jnp.stack([run_len, run_src, run_dst]).reshape(3, -1)
    live = t < n_tiles
    is_first = (t == first) & live
    n_items = jnp.where(is_first, MOE_CHUNKS, 1)
    item_end = jnp.cumsum(n_items)
    j = jnp.arange(MOE_MAX_ITEMS, dtype=jnp.int32)
    it_tile = jnp.minimum(jnp.sum(j[:, None] >= item_end[None, :], axis=1), MOE_MAX_TILES - 1).astype(jnp.int32)
    chunk = j - jnp.take(item_end - n_items, it_tile)
    kind = jnp.where(jnp.take(is_first, it_tile), chunk, jnp.where(jnp.take(live, it_tile), ITEM_FULL, ITEM_DEAD))
    kind = jnp.where(j >= item_end[-1], ITEM_NONE, kind)
    items = jnp.stack([it_tile, kind.astype(jnp.int32)])
    return run_tab, tail, tile_e, n_tiles.astype(jnp.int32).reshape(1), tile_rows, items


def _run_copies(tab_ref, i, local_ref, global_ref, sem, to_global):
    out = []
    for e in range(N_EXPERTS):
        k = i * N_EXPERTS + e
        n, src, dst = tab_ref[0, k], tab_ref[1, k], tab_ref[2, k]
        for size in RUN_SIZES:
            done = (n // (2 * size)) * (2 * size)
            loc = local_ref.at[pl.ds(pl.multiple_of(src + done, RUN_ALIGN), size), :]
            glo = global_ref.at[pl.ds(pl.multiple_of(dst + done, RUN_ALIGN), size), :]
            copy = pltpu.make_async_copy(loc, glo, sem) if to_global else pltpu.make_async_copy(glo, loc, sem)
            out.append(((n & size) != 0, copy))
    return out


def _start(copies, live=True):
    for pred, copy in copies:
        pl.when(pred & live)(copy.start)


def _wait(copies, live=True):
    for pred, copy in copies:
        pl.when(pred & live)(copy.wait)


def _start_then_wait(copies):
    _start(copies)
    _wait(copies)


def _dispatch_kernel(tab_ref, tail_ref, nt_ref, lp_ref, x_ref, g_ref, sh_ref, sc_ref, xs_ref, hs_scr, z_scr, sem):
    i = pl.program_id(0)
    tm = x_ref.shape[0]
    buf = i % 2
    tg = 256
    slot = lax.broadcasted_iota(jnp.int32, (MOE_LOCAL, tg), 0)
    acc = None
    for k in range(tm // tg):
        sl = slice(k * tg, (k + 1) * tg)
        h = _norm_mod(x_ref[sl, :], g_ref[...], sh_ref[...], sc_ref[...]).astype(BF16)
        perm = jnp.where((slot == lp_ref[0:1, sl]) | (slot == lp_ref[1:2, sl]), 1.0, 0.0).astype(BF16)
        part = jnp.dot(perm, h, preferred_element_type=F32)
        acc = part if acc is None else acc + part
    hs_scr[buf] = acc.astype(BF16)
    copies = _run_copies(tab_ref, i, hs_scr.at[buf], xs_ref, sem.at[buf], to_global=True)
    _start(copies)
    _wait(_run_copies(tab_ref, jnp.maximum(i - 1, 0), hs_scr.at[1 - buf], xs_ref, sem.at[1 - buf], to_global=True),
          live=i > 0)

    @pl.when(i == 0)
    def _():
        z_scr[...] = jnp.zeros_like(z_scr)
        zrows = z_scr.shape[0]

        def zero_tile(t, carry):
            for part in range(MOE_TM // zrows):
                dst = xs_ref.at[pl.ds(pl.multiple_of(t * MOE_TM + part * zrows, zrows), zrows), :]
                copy = pltpu.make_async_copy(z_scr, dst, sem.at[2])
                copy.start()
                copy.wait()
            return carry

        lax.fori_loop(nt_ref[0], MOE_MAX_TILES, zero_tile, 0)
        tails = []
        for e in range(N_EXPERTS):
            start, n = tail_ref[0, e], tail_ref[1, e]
            for size in RUN_SIZES:
                if size >= MOE_TM:
                    continue
                done = (n // (2 * size)) * (2 * size)
                dst = xs_ref.at[pl.ds(pl.multiple_of(start + done, RUN_ALIGN), size), :]
                tails.append(((n & size) != 0, pltpu.make_async_copy(z_scr.at[pl.ds(0, size), :], dst, sem.at[2])))
        _start_then_wait(tails)

    _wait(copies, live=i == pl.num_programs(0) - 1)


def moe_dispatch(x, g, mods, lp, run_tab, tail, n_tiles, tm=MOE_TOK):
    T = x.shape[0]
    n_rows = MOE_MAX_TILES * MOE_TM
    return pl.pallas_call(
        _dispatch_kernel,
        grid_spec=pltpu.PrefetchScalarGridSpec(
            num_scalar_prefetch=3,
            grid=(T // tm,),
            in_specs=[pl.BlockSpec((2, tm), lambda i, *_: (0, i)),
                      pl.BlockSpec((tm, D), lambda i, *_: (i, 0)),
                      pl.BlockSpec((1, D), lambda i, *_: (0, 0)),
                      _mod_spec(3, tm), _mod_spec(4, tm)],
            out_specs=pl.BlockSpec(memory_space=pl.ANY),
            scratch_shapes=[pltpu.VMEM((2, MOE_LOCAL, D), BF16), pltpu.VMEM((MOE_TM // 2, D), BF16),
                            pltpu.SemaphoreType.DMA((3,))]),
        out_shape=jax.ShapeDtypeStruct((n_rows, D), BF16),
        compiler_params=_cparams("arbitrary"),
        name="moe_dispatch",
    )(run_tab, tail, n_tiles, lp, x, g, mods, mods)


def _moe_group_kernel(it_ref, te_ref, nt_ref, tr_ref, x_ref, w1_ref, w3_ref, w2_ref, o_ref, acc_scr, b1_scr, b3_scr,
                      b2_scr):
    j = pl.program_id(0)
    kind = it_ref[1, j]
    rows = tr_ref[it_ref[0, j]]
    half = MOE_TM // 2

    def chunk(c):
        @pl.when(rows > half)
        def _():
            _swiglu_accumulate(x_ref, acc_scr, b1_scr.at[c], b3_scr.at[c], b2_scr.at[c], MOE_TM)

        @pl.when(rows <= half)
        def _():
            _swiglu_accumulate(x_ref, acc_scr, b1_scr.at[c], b3_scr.at[c], b2_scr.at[c], half)

    @pl.when((kind == 0) | (kind == ITEM_FULL))
    def _():
        acc_scr[...] = jnp.zeros_like(acc_scr)

    @pl.when(kind >= 0)
    def _():
        b1_scr[kind] = w1_ref[...].astype(BF16)
        b3_scr[kind] = w3_ref[...].astype(BF16)
        b2_scr[kind] = w2_ref[...].astype(BF16)
        chunk(kind)

    for n_rows, pred in ((MOE_TM, rows > half), (half, rows <= half)):
        @pl.when((kind == ITEM_FULL) & pred)
        def _():
            for c in range(MOE_CHUNKS):
                _swiglu_accumulate(x_ref, acc_scr, b1_scr.at[c], b3_scr.at[c], b2_scr.at[c], n_rows)

    @pl.when((kind == MOE_CHUNKS - 1) | (kind == ITEM_FULL))
    def _():
        o_ref[...] = acc_scr[...].astype(o_ref.dtype)

    @pl.when(kind == ITEM_DEAD)
    def _():
        o_ref[...] = jnp.zeros_like(o_ref)


def moe_grouped_swiglu(xs, items, tile_e, n_tiles, tile_rows, w1, w3, w2):
    tf = D_FF // MOE_CHUNKS
    tile = lambda j, it, te, nt, tr: it[0, j]
    row_in = lambda j, it, te, nt, tr: (jnp.minimum(tile(j, it, te, nt, tr), jnp.maximum(nt[0] - 1, 0)), 0)
    wchunk = lambda j, it: jnp.where(it[1, j] >= 0, it[1, j], MOE_CHUNKS - 1)
    wcol = lambda j, it, te, nt, tr: (te[it[0, j]], 0, wchunk(j, it))
    wrow = lambda j, it, te, nt, tr: (te[it[0, j]], wchunk(j, it), 0)
    return pl.pallas_call(
        _moe_group_kernel,
        grid_spec=pltpu.PrefetchScalarGridSpec(
            num_scalar_prefetch=4,
            grid=(MOE_MAX_ITEMS,),
            in_specs=[pl.BlockSpec((MOE_TM, D), row_in),
                      pl.BlockSpec((None, D, tf), wcol),
                      pl.BlockSpec((None, D, tf), wcol),
                      pl.BlockSpec((None, tf, D), wrow)],
            out_specs=pl.BlockSpec((MOE_TM, D), lambda j, it, te, nt, tr: (it[0, j], 0)),
            scratch_shapes=[pltpu.VMEM((MOE_TM, D), F32), pltpu.VMEM((MOE_CHUNKS, D, tf), BF16),
                            pltpu.VMEM((MOE_CHUNKS, D, tf), BF16), pltpu.VMEM((MOE_CHUNKS, tf, D), BF16)]),
        out_shape=jax.ShapeDtypeStruct(xs.shape, BF16),
        compiler_params=_cparams("arbitrary"),
        name="moe_grouped",
    )(items, tile_e, n_tiles, tile_rows, xs, w1, w3, w2)


def _combine_kernel(tab_ref, lp_ref, wt_ref, x_ref, gate_ref, ys_ref, op_ref, os_ref, yl_scr, sem, *, n_p):
    i = pl.program_id(0)
    tm = x_ref.shape[0]
    buf = i % 2
    last = pl.num_programs(0) - 1
    gather = lambda t, b: _run_copies(tab_ref, t, yl_scr.at[b], ys_ref, sem.at[b], to_global=False)
    _start(gather(i, buf), live=i == 0)
    _start(gather(jnp.minimum(i + 1, last), 1 - buf), live=i < last)
    _wait(gather(i, buf))
    bounds = (0, 512, 1024, 1536, MOE_LOCAL)
    moe = None
    for lo, hi in zip(bounds[:-1], bounds[1:]):
        slot = lo + lax.broadcasted_iota(jnp.int32, (tm, hi - lo), 1)
        mix = (jnp.where(slot == lp_ref[:, 0:1], wt_ref[:, 0:1], 0.0)
               + jnp.where(slot == lp_ref[:, 1:2], wt_ref[:, 1:2], 0.0)).astype(BF16)
        part = jnp.dot(mix, yl_scr[buf, lo:hi, :], preferred_element_type=F32)
        moe = part if moe is None else moe + part
    out = x_ref[...] + gate_ref[...] * moe

    @pl.when(pl.program_id(0) < n_p)
    def _():
        op_ref[...] = out

    @pl.when(pl.program_id(0) >= n_p)
    def _():
        os_ref[...] = out


def moe_combine(x, mods, lp, wts, run_tab, ys, t_prompt, tm=MOE_TOK):
    T = x.shape[0]
    n_p = t_prompt // tm
    return pl.pallas_call(
        functools.partial(_combine_kernel, n_p=n_p),
        grid_spec=pltpu.PrefetchScalarGridSpec(
            num_scalar_prefetch=1,
            grid=(T // tm,),
            in_specs=[pl.BlockSpec((tm, 2), lambda i, *_: (i, 0)),
                      pl.BlockSpec((tm, 2), lambda i, *_: (i, 0)),
                      pl.BlockSpec((tm, D), lambda i, *_: (i, 0)),
                      _mod_spec(5, tm),
                      pl.BlockSpec(memory_space=pl.ANY)],
            out_specs=_part_specs((tm, D), n_p),
            scratch_shapes=[pltpu.VMEM((2, MOE_LOCAL, D), BF16), pltpu.SemaphoreType.DMA((2,))]),
        out_shape=[jax.ShapeDtypeStruct((t_prompt, D), F32), jax.ShapeDtypeStruct((T - t_prompt, D), F32)],
        compiler_params=_cparams("arbitrary"),
        name="moe_combine",
    )(run_tab, lp.T, wts.T, x, mods, ys)


def kernel(x_prompt, x_sample, state_C, state_n, state_m, cache_k, cache_v, c, c_ctx, norm1_g, norm2_g, w_ada, b_ada, ev_w_in, ev_conv, hy_w1, hy_b1, hy_w2, hy_b2, hy_w3, hy_freq, hy_d, ml_b_gate, ml_norm_g, ev_w_out, ff_w1, ff_w3, ff_w2, at_w_qkv, at_q_g, at_k_g, at_sink, at_w_out, moe_w_router, moe_b_router, moe_w1, moe_w3, moe_w2):
    BP, LP, _ = x_prompt.shape
    BS, LS, _ = x_sample.shape
    TP = BP * LP
    assert TP % GROUP == 0 and TP // GROUP == N_PROMPT_GROUPS and LS == GROUP and BS == 8

    xp, xs = x_prompt.reshape(TP, D), x_sample.reshape(BS * LS, D)
    cond = jnp.concatenate([c_ctx[None], c, jnp.zeros((16 - 1 - BS, D), F32)], axis=0)
    mods = adaln_table(cond, w_ada, b_ada)

    u, gates = even_in_proj(xp, xs, norm1_g[0:1], mods[0], ev_w_in[0], ml_b_gate[0].reshape(1, N_GATES))
    hy = []
    for seq0, B, L, nb in ((0, BP, LP, 4), (TP // LS, BS, LS, 1)):
        fwd, inv = (jnp.asarray(t).astype(BF16) for t in _dft_tables(L))
        ka, kb = hyena_filter_spectra(L, hy_w1[0], hy_b1[0], hy_w2[0], hy_b2[0], hy_w3[0], hy_freq[0], fwd)
        hy.append(hyena_mix(u, seq0, B, L, ev_conv[0], hy_d[0], fwd, inv, ka, kb, nb))
    ml_p, new_C, new_n, new_m = mlstm_mix(u, gates, 0, BP, LP, ml_norm_g[0], want_state=True)
    ml_s = mlstm_mix(u, gates, TP // LS, BS, LS, ml_norm_g[0],
                     state=(state_C[:, 0], state_n[:, 0], state_m[:, 0]))
    x = ffn_residual([hy, (ml_p, ml_s)], ev_w_out[0], (xp, xs), norm2_g[0:1], mods[0], ff_w1[0], ff_w3[0], ff_w2[0])

    q, kv = qkv_proj(x, norm1_g[1:2], mods[1], at_w_qkv[0])
    o_p, new_k, new_v = attention(q, kv, 0, at_q_g[0], at_k_g[0], at_sink[0], BP, LP)
    o_s = attention(q, kv, TP, at_q_g[0], at_k_g[0], at_sink[0], BS, LS, cache=(cache_k[:, 0], cache_v[:, 0]))
    x, lp, wts, runs, totals = moe_router([(o_p, o_s)], at_w_out[0], x, norm2_g[1:2], mods[1],
                                          moe_w_router[0], moe_b_router[0])
    run_tab, tail, tile_e, n_tiles, tile_rows, items = moe_layout(runs, totals)
    xsort = moe_dispatch(x, norm2_g[1:2], mods[1], lp, run_tab, tail, n_tiles)
    ysort = moe_grouped_swiglu(xsort, items, tile_e, n_tiles, tile_rows, moe_w1[0], moe_w3[0], moe_w2[0])
    yp, ys = moe_combine(x, mods[1], lp, wts, run_tab, ysort, TP)

    return (yp.reshape(BP, LP, D), ys.reshape(BS, LS, D),
            new_C, new_n, new_m, new_k, new_v)
```

```python
import functools
import math

import numpy as np
import jax
import jax.numpy as jnp
from jax import lax
from jax.experimental import pallas as pl
from jax.experimental.pallas import tpu as pltpu

F32 = jnp.float32
BF16 = jnp.bfloat16
HIGHEST = lax.Precision.HIGHEST

D = 1024
GROUP = 1024
N_PROMPT_GROUPS = 8
HY_W = 512
ML_HEADS = 4
ML_HD = 128
ML_CHUNK = 256
EVEN_MAIN = 3 * HY_W + 4 * 512
N_GATES = 16
ATT_HD = 64
ATT_HEADS = 16
ATT_KV = 4
WINDOW = 128
GRID_W = 64
ROPE_BASE = 10000.0
D_FF = 2816
N_EXPERTS = 8
EPS = 1e-6
NEG = -1e30
VMEM_LIMIT = 56 * 1024 * 1024


def _cparams(*sem):
    return pltpu.CompilerParams(dimension_semantics=sem, vmem_limit_bytes=VMEM_LIMIT)


def _mod_row(i, tm):
    return jnp.maximum(i * tm // GROUP - (N_PROMPT_GROUPS - 1), 0)


def _silu(x):
    return x * jax.nn.sigmoid(x)


def _bdot(a, b):
    return jnp.dot(a.astype(BF16), b.astype(BF16), preferred_element_type=F32)


def _norm_mod(x, g, sh, sc):
    y = x * lax.rsqrt(jnp.mean(x * x, axis=-1, keepdims=True) + EPS) * g
    return y * (1.0 + sc) + sh


def _adaln_kernel(c_ref, w_ref, b_ref, o_ref):
    s = _silu(c_ref[...])
    s_hi = s.astype(BF16)
    s_lo = (s - s_hi.astype(F32)).astype(BF16)
    w_hi = w_ref[...].astype(BF16)
    w_lo = (w_ref[...] - w_hi.astype(F32)).astype(BF16)
    dot = functools.partial(jnp.dot, preferred_element_type=F32)
    o_ref[...] = dot(s_hi, w_hi) + dot(s_lo, w_hi) + dot(s_hi, w_lo) + b_ref[...]


def adaln_table(cond, w_ada, b_ada):
    depth = w_ada.shape[0]
    tn = 1536
    out = pl.pallas_call(
        _adaln_kernel,
        grid=(depth, 6 * D // tn),
        in_specs=[pl.BlockSpec((16, D), lambda l, j: (0, 0)),
                  pl.BlockSpec((None, D, tn), lambda l, j: (l, 0, j)),
                  pl.BlockSpec((None, 1, tn), lambda l, j: (l, 0, j))],
        out_specs=pl.BlockSpec((None, 16, tn), lambda l, j: (l, 0, j)),
        out_shape=jax.ShapeDtypeStruct((depth, 16, 6 * D), F32),
        compiler_params=_cparams("parallel", "parallel"),
        name="adaln",
    )(cond, w_ada, b_ada.reshape(depth, 1, 6 * D))
    return out.reshape(depth, 16, 1, 6 * D)


def _same_tile(i):
    return i


def _part_specs(block, n_p, tile_of=_same_tile):
    return [pl.BlockSpec(block, lambda i, *_: (jnp.minimum(tile_of(i), n_p - 1), 0)),
            pl.BlockSpec(block, lambda i, *_: (jnp.maximum(tile_of(i) - n_p, 0), 0))]


def _pick(is_prompt, p_ref, s_ref):
    return jnp.where(is_prompt, p_ref[...], s_ref[...])


def _mod_spec(k, tm, tile_of=_same_tile):
    return pl.BlockSpec((None, 1, D), lambda i, *_: (_mod_row(tile_of(i), tm), 0, k))


def _log_sigmoid(x):
    return jnp.minimum(x, 0.0) - jnp.log(1.0 + jnp.exp(-jnp.abs(x)))


def _split3(x):
    hi = x.astype(BF16)
    r = x - hi.astype(F32)
    mid = r.astype(BF16)
    return hi, mid, (r - mid.astype(F32)).astype(BF16)


def _even_in_kernel(xp_ref, xs_ref, g_ref, sh_ref, sc_ref, w_ref, bg_ref, lo_ref, up_ref, u_ref, gate_ref, *, n_p, tn):
    is_prompt = pl.program_id(0) < n_p
    h = _norm_mod(_pick(is_prompt, xp_ref, xs_ref), g_ref[...], sh_ref[...], sc_ref[...]).astype(BF16)

    def main_chunks(j0, j1):
        for j in range(j0, j1):
            u_ref[:, j * tn:(j + 1) * tn] = _bdot(h, w_ref[:, j * tn:(j + 1) * tn]).astype(u_ref.dtype)

    n_main = EVEN_MAIN // tn
    gates = _bdot(h, w_ref[:, EVEN_MAIN:]) + bg_ref[...]
    main_chunks(0, n_main // 2)
    lf = _log_sigmoid(gates)
    col = lax.broadcasted_iota(jnp.int32, (1, N_GATES), 1)
    is_forget = (col // ML_HEADS) % 2 == 1
    is_rev = col >= N_GATES // 2
    for ch in range(h.shape[0] // ML_CHUNK):
        sl = slice(ch * ML_CHUNK, (ch + 1) * ML_CHUNK)
        parts = _split3(lf[sl])
        cf = sum(jnp.dot(lo_ref[...], p, preferred_element_type=F32) for p in parts)
        cr = sum(jnp.dot(up_ref[...], p, preferred_element_type=F32) for p in parts)
        gate_ref[sl, :] = jnp.where(is_forget, jnp.where(is_rev, cr, cf), gates[sl])
    main_chunks(n_main // 2, n_main)


def even_in_proj(xp, xs, g, mods, w_in, b_gate, tm=1024, tn=512):
    T = xp.shape[0] + xs.shape[0]
    tri = np.tril(np.ones((ML_CHUNK, ML_CHUNK), np.float32))
    lo, up = jnp.asarray(tri).astype(BF16), jnp.asarray(tri.T).astype(BF16)
    return pl.pallas_call(
        functools.partial(_even_in_kernel, n_p=xp.shape[0] // tm, tn=tn),
        grid=(T // tm,),
        in_specs=_part_specs((tm, D), xp.shape[0] // tm) + [
                  pl.BlockSpec((1, D), lambda i: (0, 0)),
                  _mod_spec(0, tm), _mod_spec(1, tm),
                  _const_spec(w_in),
                  pl.BlockSpec((1, N_GATES), lambda i: (0, 0)),
                  _const_spec(lo), _const_spec(up)],
        out_specs=[pl.BlockSpec((tm, EVEN_MAIN), lambda i: (i, 0)),
                   pl.BlockSpec((tm, N_GATES), lambda i: (i, 0))],
        out_shape=[jax.ShapeDtypeStruct((T, EVEN_MAIN), BF16),
                   jax.ShapeDtypeStruct((T, N_GATES), F32)],
        compiler_params=_cparams("parallel"),
        name="even_in_proj",
    )(xp, xs, g, mods, mods, w_in, b_gate, lo, up)


def _dft_tables(L):
    n = 2 * L
    f = np.arange(L, dtype=np.int64)[:, None]
    s = np.arange(L, dtype=np.int64)[None, :]
    ang = 2.0 * np.pi * ((f * s) % n).astype(np.float64) / n
    fwd = np.concatenate([np.cos(ang), -np.sin(ang)], axis=0)
    fwd[L, :] = np.where(np.arange(L) % 2 == 0, 1.0, -1.0)
    t = np.arange(L, dtype=np.int64)[:, None]
    ff = np.arange(L, dtype=np.int64)[None, :]
    ang = 2.0 * np.pi * ((t * ff) % n).astype(np.float64) / n
    inv_re = 2.0 * np.cos(ang) / n
    inv_re[:, 0] = 1.0 / n
    inv_im = -2.0 * np.sin(ang) / n
    inv_im[:, 0] = np.where(np.arange(L) % 2 == 0, 1.0, -1.0) / n
    inv = np.concatenate([inv_re, inv_im], axis=1)
    return fwd.astype(np.float32), inv.astype(np.float32)


def _filter_tables(L):
    t = np.linspace(0.0, 1.0, L, dtype=np.float32).astype(np.float64)[:, None]
    w = 2.0 * math.pi * np.arange(L, dtype=np.float64)[:, None] / L
    bands = np.linspace(1e-4, 16 - 1, 16, dtype=np.float32).astype(np.float64)[None, :]
    z = np.concatenate([t, np.cos(bands * w), -np.sin(bands * w)], axis=-1)
    zp = np.zeros((L, 128), np.float64)
    zp[:, :z.shape[1]] = z
    max_decay = math.log(1e-2) / 0.3
    min_decay = math.log(1e-2) / 1.5
    deltas = np.linspace(min_decay, max_decay, HY_W, dtype=np.float32).astype(np.float64)
    decay = np.exp(-t * np.abs(deltas))
    return zp.astype(np.float32), decay.astype(np.float32)


def _hy_filter_kernel(z_ref, dec_ref, w1_ref, b1_ref, w2_ref, b2_ref, w3_ref, fr_ref, fwd_ref,
                      ka_ref, kb_ref):
    L = z_ref.shape[0]
    hdot = functools.partial(jnp.dot, precision=HIGHEST, preferred_element_type=F32)
    h = jnp.sin(fr_ref[0:1, :] * (hdot(z_ref[...], w1_ref[...]) + b1_ref[...]))
    h = jnp.sin(fr_ref[1:2, :] * (hdot(h, w2_ref[...]) + b2_ref[...]))
    h = hdot(h, w3_ref[...])
    row0 = lax.broadcasted_iota(jnp.int32, (L, 1), 0) == 0
    h0 = h[:, :HY_W] * dec_ref[...]
    h1 = h[:, HY_W:] * dec_ref[...]
    l1 = jnp.sum(jnp.abs(h0), axis=0, keepdims=True) + jnp.sum(jnp.abs(h1), axis=0, keepdims=True)
    inv = 1.0 / l1
    h0 = h0 * inv
    h1 = jnp.where(row0, 0.0, h1 * inv)
    f0 = _bdot(fwd_ref[...], h0)
    f1 = _bdot(fwd_ref[...], h1)
    ka_ref[...] = f0[:L] + f1[:L]
    kb_ref[...] = jnp.where(row0, f0[L:] + f1[L:], f0[L:] - f1[L:])


def _const_spec(a):
    return pl.BlockSpec(a.shape, lambda *_: (0,) * a.ndim, pipeline_mode=pl.Buffered(1))


def hyena_filter_spectra(L, w1, b1, w2, b2, w3, freq, fwd):
    z, dec = _filter_tables(L)
    pad2 = lambda a, r, c: jnp.pad(a, ((0, r - a.shape[0]), (0, c - a.shape[1])))
    args = (jnp.asarray(z), jnp.asarray(dec), pad2(w1, 128, 128), pad2(b1[None], 1, 128),
            pad2(w2, 128, 128), pad2(b2[None], 1, 128), pad2(w3, 128, 4 * HY_W), pad2(freq, 2, 128), fwd)
    in_specs = [_const_spec(a) for a in args]
    in_specs[6] = pl.BlockSpec((128, 2 * HY_W), lambda o: (0, o))
    shp = jax.ShapeDtypeStruct((2, L, HY_W), F32)
    out_spec = pl.BlockSpec((None, L, HY_W), lambda o: (o, 0, 0))
    return pl.pallas_call(
        _hy_filter_kernel,
        grid=(2,),
        in_specs=in_specs,
        out_specs=[out_spec, out_spec],
        out_shape=[shp, shp],
        compiler_params=_cparams("arbitrary"),
        name=f"hyena_filter_{L}",
    )(*args)


def _hyena_kernel(u_ref, cw_ref, d_ref, fwd_ref, inv_ref, ka_ref, kb_ref, o_ref):
    nb, L = u_ref.shape[0], u_ref.shape[1]
    row = lax.broadcasted_iota(jnp.int32, (L, 1), 0)
    first, last = row == 0, row == L - 1
    fwd = fwd_ref[...].astype(BF16)
    inv = inv_ref[...].astype(BF16)

    def long_conv(z, o):
        zf = jnp.dot(fwd, z.astype(BF16), preferred_element_type=F32)
        a, b = zf[:L], zf[L:]
        ka, kb = ka_ref[o], kb_ref[o]
        yr = a * ka - jnp.where(first, 0.0, b * kb)
        yi = jnp.where(first, b * kb, a * kb + b * ka)
        return (jnp.dot(inv[:, :L], yr.astype(BF16), preferred_element_type=F32)
                + jnp.dot(inv[:, L:], yi.astype(BF16), preferred_element_type=F32))

    for bi in range(nb):
        u = u_ref[bi].astype(F32)
        prev = jnp.where(first, 0.0, pltpu.roll(u, 1, 0))
        nxt = jnp.where(last, 0.0, pltpu.roll(u, L - 1, 0))
        u = prev * cw_ref[0:1, :] + u * cw_ref[1:2, :] + nxt * cw_ref[2:3, :]
        v, x1, x2 = u[:, :HY_W], u[:, HY_W:2 * HY_W], u[:, 2 * HY_W:]
        z = x1 * (long_conv(v, 0) + d_ref[0:1, :] * v)
        z = x2 * (long_conv(z, 1) + d_ref[1:2, :] * z)
        o_ref[bi] = z.astype(o_ref.dtype)


def hyena_mix(u, seq0, B, L, conv_w, d_skip, fwd, inv, ka, kb, nb):
    u3 = u.reshape(-1, L, EVEN_MAIN)
    full = _const_spec
    out = pl.pallas_call(
        _hyena_kernel,
        grid=(B // nb,),
        in_specs=[pl.BlockSpec((nb, L, 3 * HY_W), lambda b: (b + seq0 // nb, 0, 0)),
                  full(conv_w), full(d_skip), full(fwd), full(inv), full(ka), full(kb)],
        out_specs=pl.BlockSpec((nb, L, HY_W), lambda b: (b, 0, 0)),
        out_shape=jax.ShapeDtypeStruct((B, L, HY_W), BF16),
        compiler_params=_cparams("parallel"),
        name=f"hyena_{L}",
    )(u3, conv_w, d_skip, fwd, inv, ka, kb)
    return out.reshape(B * L, HY_W)


def _mlstm_kernel(*refs, has_state, want_state):
    q_ref, k_ref, v_ref, o_ref, gc_ref, gr_ref, ng_ref = refs[:7]
    refs = refs[7:]
    if has_state:
        c0t_ref, n0b_ref, m0_ref = refs[:3]
        refs = refs[3:]
    y_ref = refs[0]
    if want_state:
        c_out, n_out, m_out = refs[1:4]
    L, d = q_ref.shape[0], ML_HD
    T = min(ML_CHUNK, L)
    nc = L // T
    scale = 1.0 / math.sqrt(d)
    nt = (((1,), (1,)), ((), ()))
    si = lax.broadcasted_iota(jnp.int32, (T, T), 0)
    ti = lax.broadcasted_iota(jnp.int32, (T, T), 1)
    allowed = (si <= ti, si >= ti)
    chains = [(dr, h) for dr in range(2) for h in range(ML_HEADS)]
    gcol = lambda dr, gi, h: dr * 2 * ML_HEADS + gi * ML_HEADS + h

    caug_t, m = {}, {}
    for ch in chains:
        dr, h = ch
        if has_state:
            caug_t[ch] = jnp.concatenate([c0t_ref[dr, h], n0b_ref[dr, h]], axis=0)
            m[ch] = m0_ref[dr, h:h + 1, 0:1]
        else:
            caug_t[ch], m[ch] = jnp.zeros((2 * d, d), F32), jnp.zeros((1, 1), F32)

    chunk_cache = {}

    def chunk_data(h, j):
        if (h, j) not in chunk_cache:
            sl, hl = slice(j * T, (j + 1) * T), slice(h * d, (h + 1) * d)
            q = q_ref[sl, hl]
            ks = (k_ref[sl, hl].astype(F32) * scale).astype(BF16)
            v_t = v_ref[sl, hl].astype(F32).T
            vaug_t = jnp.concatenate([v_t, jnp.ones((d, T), F32)], axis=0).astype(BF16)
            s_raw = lax.dot_general(ks, q, nt, preferred_element_type=F32)
            chunk_cache[(h, j)] = (q, ks, v_t, vaug_t, s_raw)
        return chunk_cache[(h, j)]

    h_sum = {}
    for it in range(nc):
        step = {ch: (it if ch[0] == 0 else nc - 1 - it) for ch in chains}
        data = {ch: chunk_data(ch[1], step[ch]) for ch in chains}
        inter_t = {ch: lax.dot_general(caug_t[ch].astype(BF16), data[ch][0], nt, preferred_element_type=F32)
                   for ch in chains}
        gate = {}
        for ch in chains:
            dr, h = ch
            sl = slice(step[ch] * T, (step[ch] + 1) * T)
            li_r, b_r = gr_ref[gcol(dr, 0, h):gcol(dr, 0, h) + 1, sl], gr_ref[gcol(dr, 1, h):gcol(dr, 1, h) + 1, sl]
            src = gc_ref[sl, gcol(dr, 0, h):gcol(dr, 0, h) + 1] - gc_ref[sl, gcol(dr, 1, h):gcol(dr, 1, h) + 1]
            dm = jnp.where(allowed[dr], src + b_r, NEG)
            inter = b_r + m[ch]
            m_t = jnp.maximum(inter, jnp.max(dm, axis=0, keepdims=True))
            b_end = b_r[:, T - 1:T] if dr == 0 else b_r[:, 0:1]
            g_r = b_end - b_r + li_r
            m_new = jnp.maximum(b_end + m[ch], jnp.max(g_r, axis=1, keepdims=True))
            gate[ch] = (jnp.exp(dm - m_t), jnp.exp(inter - m_t), jnp.exp(-m_t), jnp.exp(g_r - m_new),
                        jnp.exp(b_end + m[ch] - m_new), m_new)
        for ch in chains:
            q, ks, v_t, vaug_t, s_raw = data[ch]
            w_intra, w_inter, floor, w_tok, decay, m_new = gate[ch]
            acc = jnp.dot(vaug_t, (s_raw * w_intra).astype(BF16), preferred_element_type=F32) + w_inter * inter_t[ch]
            h_t = acc[:d] / jnp.maximum(jnp.abs(acc[d:]), floor)
            key = (ch[1], step[ch])
            h_sum[key] = h_t if key not in h_sum else h_sum[key] + h_t
            vw_t = jnp.concatenate([v_t * w_tok, jnp.broadcast_to(w_tok, (d, T))], axis=0).astype(BF16)
            caug_t[ch] = decay * caug_t[ch] + jnp.dot(vw_t, ks, preferred_element_type=F32)
            m[ch] = m_new

    if want_state:
        for ch in chains:
            dr, h = ch
            c_out[dr, h] = caug_t[ch][:d].T
            n_out[dr, h:h + 1, :] = caug_t[ch][d:d + 1, :]
            m_out[dr, h:h + 1, :] = jnp.broadcast_to(m[ch], (1, d))
    for h in range(ML_HEADS):
        for j in range(nc):
            sl, hl = slice(j * T, (j + 1) * T), slice(h * d, (h + 1) * d)
            hv = h_sum[(h, j)].T
            y = hv * lax.rsqrt(jnp.mean(hv * hv, axis=-1, keepdims=True) + EPS) * ng_ref[:, hl]
            y_ref[sl, hl] = (y * jax.nn.sigmoid(o_ref[sl, hl].astype(F32))).astype(y_ref.dtype)


def mlstm_mix(u, gates, seq0, B, L, norm_g, state=None, want_state=False):
    u3 = u.reshape(-1, L, EVEN_MAIN)
    gc = gates.reshape(-1, L, N_GATES)[seq0:seq0 + B]
    gr = gc.transpose(0, 2, 1)
    width = ML_HEADS * ML_HD
    col = lambda i: pl.BlockSpec((None, L, width), lambda b: (b + seq0, 0, (3 * HY_W + i * width) // width))
    in_specs = [col(0), col(1), col(2), col(3),
                pl.BlockSpec((None, L, N_GATES), lambda b: (b, 0, 0)),
                pl.BlockSpec((None, N_GATES, L), lambda b: (b, 0, 0)),
                pl.BlockSpec((1, width), lambda b: (0, 0))]
    args = [u3, u3, u3, u3, gc, gr, norm_g.reshape(1, width)]
    sspec = pl.BlockSpec((None, 2, ML_HEADS, ML_HD, ML_HD), lambda b: (b, 0, 0, 0, 0))
    vspec = pl.BlockSpec((None, 2, ML_HEADS, ML_HD), lambda b: (b, 0, 0, 0))
    if state is not None:
        C0, n0, m0 = state
        in_specs += [sspec, sspec, vspec]
        args += [C0.swapaxes(-1, -2), jnp.broadcast_to(n0[..., None, :], C0.shape),
                 jnp.broadcast_to(m0[..., None], n0.shape)]
    out_specs = [pl.BlockSpec((None, L, width), lambda b: (b, 0, 0))]
    out_shape = [jax.ShapeDtypeStruct((B, L, width), BF16)]
    if want_state:
        out_specs += [sspec, vspec, vspec]
        out_shape += [jax.ShapeDtypeStruct((B, 2, ML_HEADS, ML_HD, ML_HD), F32),
                      jax.ShapeDtypeStruct((B, 2, ML_HEADS, ML_HD), F32),
                      jax.ShapeDtypeStruct((B, 2, ML_HEADS, ML_HD), F32)]
    outs = pl.pallas_call(
        functools.partial(_mlstm_kernel, has_state=state is not None, want_state=want_state),
        grid=(B,),
        in_specs=in_specs, out_specs=out_specs, out_shape=out_shape,
        compiler_params=_cparams("parallel"),
        name=f"mlstm_{L}",
    )(*args)
    y = outs[0].reshape(B * L, width)
    if not want_state:
        return y
    _, C, n, m = outs
    return y, C, n, m[..., 0]


def _proj_inputs(acts, w, x, mods, tm, tile_of=_same_tile):
    xs = tuple(x) if isinstance(x, (tuple, list)) else (x,)
    n_p = acts[0][0].shape[0] // tm
    specs = []
    for pair in acts:
        specs += _part_specs((tm, pair[0].shape[1]), n_p, tile_of)
    specs.append(pl.BlockSpec(w.shape, lambda *_: (0, 0), pipeline_mode=pl.Buffered(1)))
    specs += (_part_specs((tm, D), n_p, tile_of) if len(xs) == 2
              else [pl.BlockSpec((tm, D), lambda i, *_: (tile_of(i), 0))])
    specs.append(_mod_spec(2, tm, tile_of))
    args = [a for pair in acts for a in pair] + [w, *xs, mods]
    return specs, args, dict(n_in=len(acts), n_x=len(xs), n_p=n_p)


def _proj_value(refs, is_prompt, n_in, n_x, n_p):
    a_refs = refs[:2 * n_in]
    w_ref = refs[2 * n_in]
    x_refs = refs[2 * n_in + 1:2 * n_in + 1 + n_x]
    gate_ref = refs[2 * n_in + 1 + n_x]
    k0 = 0
    acc = None
    for j in range(n_in):
        a = _pick(is_prompt, a_refs[2 * j], a_refs[2 * j + 1])
        kw = a.shape[1]
        part = _bdot(a, w_ref[k0:k0 + kw, :])
        acc = part if acc is None else acc + part
        k0 += kw
    x = _pick(is_prompt, *x_refs) if n_x == 2 else x_refs[0][...]
    return x + gate_ref[...] * acc


def _n_proj_refs(n_in, n_x, n_p):
    return 2 * n_in + 1 + n_x + 1


SWIGLU_ROWS = 512


def _swiglu_accumulate(h_scr, acc_scr, w1_ref, w3_ref, w2_ref, rows, scale=None):
    w1, w3, w2 = w1_ref[...].astype(BF16), w3_ref[...].astype(BF16), w2_ref[...].astype(BF16)
    groups = [slice(r, r + SWIGLU_ROWS) for r in range(0, rows, SWIGLU_ROWS)]
    ups = []
    for sl in groups:
        h = h_scr[sl, :]
        ups.append((jnp.dot(h, w1, preferred_element_type=F32), jnp.dot(h, w3, preferred_element_type=F32)))
    for sl, (a, b) in zip(groups, ups):
        mid = (_silu(a) * b).astype(BF16)
        down = jnp.dot(mid, w2, preferred_element_type=F32)
        acc_scr[sl, :] += down if scale is None else scale * down


def _ffn_kernel(*refs, proj, nc):
    n = _n_proj_refs(**proj)
    g_ref, sh_ref, sc_ref, gate_ref, w1_ref, w3_ref, w2_ref, o_ref, h_scr, b1_scr, b3_scr, b2_scr = refs[n:]
    i = pl.program_id(0)
    tile = jnp.maximum(i - (nc - 1), 0)
    is_prompt = tile < proj["n_p"]
    rows = h_scr.shape[0]

    def start_tile():
        x = _proj_value(refs[:n], is_prompt, **proj)
        o_ref[...] = x
        h_scr[...] = _norm_mod(x, g_ref[...], sh_ref[...], sc_ref[...]).astype(BF16)

    def chunk(c):
        _swiglu_accumulate(h_scr, o_ref, b1_scr.at[c], b3_scr.at[c], b2_scr.at[c], rows, scale=gate_ref[...])

    pl.when(i == 0)(start_tile)

    @pl.when(i < nc)
    def _():
        b1_scr[i] = w1_ref[...].astype(BF16)
        b3_scr[i] = w3_ref[...].astype(BF16)
        b2_scr[i] = w2_ref[...].astype(BF16)
        chunk(i)

    @pl.when(i >= nc)
    def _():
        start_tile()
        for c in range(nc):
            chunk(c)


def ffn_residual(acts, w_out, x, g, mods, w1, w3, w2, tm=512, tf=256):
    T = sum(a.shape[0] for a in acts[0])
    nc = D_FF // tf
    tile_of = lambda i: jnp.maximum(i - (nc - 1), 0)
    chunk_of = lambda i: jnp.minimum(i, nc - 1)
    p_specs, p_args, proj = _proj_inputs(acts, w_out, x, mods, tm, tile_of)
    return pl.pallas_call(
        functools.partial(_ffn_kernel, proj=proj, nc=nc),
        grid=(T // tm + nc - 1,),
        in_specs=p_specs + [
                  pl.BlockSpec((1, D), lambda i: (0, 0)),
                  _mod_spec(3, tm, tile_of), _mod_spec(4, tm, tile_of), _mod_spec(5, tm, tile_of),
                  pl.BlockSpec((D, tf), lambda i: (0, chunk_of(i))),
                  pl.BlockSpec((D, tf), lambda i: (0, chunk_of(i))),
                  pl.BlockSpec((tf, D), lambda i: (chunk_of(i), 0))],
        out_specs=pl.BlockSpec((tm, D), lambda i: (tile_of(i), 0)),
        out_shape=jax.ShapeDtypeStruct((T, D), F32),
        scratch_shapes=[pltpu.VMEM((tm, D), BF16), pltpu.VMEM((nc, D, tf), BF16),
                        pltpu.VMEM((nc, D, tf), BF16), pltpu.VMEM((nc, tf, D), BF16)],
        compiler_params=_cparams("arbitrary"),
        name="ffn",
    )(*p_args, g, mods, mods, mods, w1, w3, w2)


def _qkv_kernel(x_ref, g_ref, sh_ref, sc_ref, w_ref, q_ref, kv_ref):
    h = _norm_mod(x_ref[...], g_ref[...], sh_ref[...], sc_ref[...]).astype(BF16)
    nq = q_ref.shape[1]
    q_ref[...] = _bdot(h, w_ref[:, :nq]).astype(q_ref.dtype)
    kv_ref[...] = _bdot(h, w_ref[:, nq:])


def qkv_proj(x, g, mods, w_qkv, tm=1024):
    T = x.shape[0]
    nq, nkv = ATT_HEADS * ATT_HD, 2 * ATT_KV * ATT_HD
    return pl.pallas_call(
        _qkv_kernel,
        grid=(T // tm,),
        in_specs=[pl.BlockSpec((tm, D), lambda i: (i, 0)),
                  pl.BlockSpec((1, D), lambda i: (0, 0)),
                  _mod_spec(0, tm), _mod_spec(1, tm),
                  _const_spec(w_qkv)],
        out_specs=[pl.BlockSpec((tm, nq), lambda i: (i, 0)),
                   pl.BlockSpec((tm, nkv), lambda i: (i, 0))],
        out_shape=[jax.ShapeDtypeStruct((T, nq), BF16), jax.ShapeDtypeStruct((T, nkv), F32)],
        compiler_params=_cparams("parallel"),
        name="qkv_proj",
    )(x, g, mods, mods, w_qkv)


def _rope_tables(L):
    half = ATT_HD // 2
    pos_r = (np.arange(L) // GRID_W).astype(np.float32)
    pos_c = (np.arange(L) % GRID_W).astype(np.float32)
    inv = (ROPE_BASE ** (-np.arange(0, half, 2, dtype=np.float32) / half)).astype(np.float32)
    cos = np.zeros((L, ATT_HD), np.float64)
    sin = np.zeros((L, ATT_HD), np.float64)
    for base, pos in ((0, pos_r), (half, pos_c)):
        ang = (pos[:, None] * inv[None, :]).astype(np.float32).astype(np.float64)
        cos[:, base:base + half] = np.concatenate([np.cos(ang), np.cos(ang)], axis=1)
        sin[:, base:base + half] = np.concatenate([-np.sin(ang), np.sin(ang)], axis=1)
    return (np.tile(cos, (1, 4)).astype(np.float32), np.tile(sin, (1, 4)).astype(np.float32))


def _seg_rms(x):
    w = x.shape[1]
    ri = lax.broadcasted_iota(jnp.int32, (w, w), 0) // ATT_HD
    ci = lax.broadcasted_iota(jnp.int32, (w, w), 1) // ATT_HD
    ss = _bdot(x * x, (ri == ci).astype(F32))
    return x * lax.rsqrt(ss * (1.0 / ATT_HD) + EPS)


LOG2E = 1.4426950408889634
ATTN_LOOKAHEAD = 1


def _exp2_bf16(x):
    return jnp.exp2(x.astype(BF16))


def _both_halves(tile, low):
    lane = lax.broadcasted_iota(jnp.int32, tile.shape, 1)
    other = pltpu.roll(tile, ATT_HD, 1)
    return jnp.where((lane < ATT_HD) == low, tile, other)


def _swap16(x):
    w = x.shape[1]
    lane = lax.broadcasted_iota(jnp.int32, x.shape, 1)
    return jnp.where(lane % 32 < 16, pltpu.roll(x, w - 16, 1), pltpu.roll(x, 16, 1))


def _attn_kernel(*refs, latent, tq):
    if latent:
        (q_ref, kv_ref, ck_ref, cv_ref, qg_ref, kg_ref, sink_ref, cosq_ref, sinq_ref, cosk_ref, sink_t_ref,
         o_ref, kk_scr, vt_scr, ckk_scr, cvt_scr) = refs
    else:
        q_ref, kv_ref, qg_ref, kg_ref, sink_ref, o_ref, ko_ref, vo_ref, kk_scr, vt_scr = refs
    L = kv_ref.shape[0]
    gw = ATT_KV * ATT_HD
    pw = 2 * ATT_HD
    qb = pl.program_id(1)

    vrows = vt_scr.shape[2]
    nblk = L // pw

    def vt_aug(tile, low):
        vt = tile.T[0:ATT_HD, :] if low else tile.T[ATT_HD:, :]
        return jnp.concatenate([vt, jnp.ones((vrows - ATT_HD, tile.shape[0]), F32)], axis=0).astype(BF16)

    @pl.when(qb == 0)
    def _():
        kn = _seg_rms(kv_ref[:, :gw]) * kg_ref[...]
        v = kv_ref[:, gw:]
        if latent:
            kn = kn * cosk_ref[...] + _swap16(kn) * sink_t_ref[...]
        else:
            for c in range(ATT_KV):
                ko_ref[c] = kn[:, c * ATT_HD:(c + 1) * ATT_HD]
                vo_ref[c] = v[:, c * ATT_HD:(c + 1) * ATT_HD]
        for c in range(ATT_KV):
            tile, low = slice((c // 2) * pw, (c // 2 + 1) * pw), c % 2 == 0
            kk_scr[c] = _both_halves(kn[:, tile], low).astype(BF16)
            for j in range(nblk):
                vt_scr[c, j] = vt_aug(v[j * pw:(j + 1) * pw, tile], low)
            if latent:
                ck, cv = ck_ref[c], cv_ref[c]
                ckk_scr[c] = jnp.concatenate([ck, ck], axis=1).astype(BF16)
                cvt_scr[c] = vt_aug(jnp.concatenate([cv, cv], axis=1), True)

    if latent:
        span = tq + 2 * WINDOW
        start = pl.multiple_of(jnp.clip(qb * tq - WINDOW, 0, L - span), WINDOW)
        blk0 = start // pw
        s_pos = start + lax.broadcasted_iota(jnp.int32, (span, tq), 0)
        t_pos = qb * tq + lax.broadcasted_iota(jnp.int32, (span, tq), 1)
        win_bias = jnp.where(jnp.abs(t_pos - s_pos) <= WINDOW, 0.0, NEG)
    else:
        span, blk0 = L, 0

    nt = (((1,), (1,)), ((), ()))
    low_q = lax.broadcasted_iota(jnp.int32, (tq, pw), 1) < ATT_HD
    def group_scores(c):
        qc = _seg_rms(q_ref[:, c * gw:(c + 1) * gw].astype(F32)) * qg_ref[...]
        if latent:
            qc = qc * cosq_ref[...] + _swap16(qc) * sinq_ref[...]
            kw = kk_scr[c, pl.ds(start, span), :]
        else:
            kw = kk_scr[c]
        qc = qc * (LOG2E / math.sqrt(ATT_HD))
        scores = []
        for g in range(ATT_KV):
            qt = qc[:, (g // 2) * pw:(g // 2 + 1) * pw]
            qm = jnp.where(low_q if g % 2 == 0 else ~low_q, qt, 0.0).astype(BF16)
            lw = lax.dot_general(kw, qm, nt, preferred_element_type=F32)
            lc = lax.dot_general(ckk_scr[c], qm, nt, preferred_element_type=F32) if latent else None
            scores.append((lw, lc))
        return scores

    def group_outputs(c, scores):
        vw = jnp.concatenate([vt_scr[c, blk0 + j] for j in range(span // pw)], axis=1)
        outs = []
        for g in range(ATT_KV):
            head = c * ATT_KV + g
            sink = sink_ref[:, head:head + 1] * LOG2E
            lw, lc = scores[g]
            if latent:
                lw = lw + win_bias
                mx = jnp.maximum(jnp.maximum(jnp.max(lw, axis=0, keepdims=True),
                                             jnp.max(lc, axis=0, keepdims=True)), sink)
                r = jnp.dot(vw, _exp2_bf16(lw - mx), preferred_element_type=F32) + jnp.dot(
                    cvt_scr[c], _exp2_bf16(lc - mx), preferred_element_type=F32)
            else:
                mx = jnp.maximum(jnp.max(lw, axis=0, keepdims=True), sink)
                r = jnp.dot(vw, _exp2_bf16(lw - mx), preferred_element_type=F32)
            den = r[ATT_HD:ATT_HD + 1, :] + jnp.exp2(sink - mx)
            outs.append(r[0:ATT_HD, :] / den)
        for t in range(2):
            o_ref[:, c * gw + t * pw:c * gw + (t + 1) * pw] = (
                jnp.concatenate(outs[2 * t:2 * t + 2], axis=0).T.astype(o_ref.dtype))

    pending = [group_scores(c) for c in range(ATTN_LOOKAHEAD)]
    for c in range(ATT_KV):
        if c + ATTN_LOOKAHEAD < ATT_KV:
            pending.append(group_scores(c + ATTN_LOOKAHEAD))
        group_outputs(c, pending.pop(0))


def attention(q, kv, row0, q_g, k_g, sink, B, L, cache=None, tq=256):
    latent = cache is not None
    gw = ATT_KV * ATT_HD
    qg = jnp.tile(q_g, ATT_KV)[None]
    kg = jnp.tile(k_g, ATT_KV)[None]
    nq = L // tq
    const = lambda a: pl.BlockSpec(a.shape, lambda b, i: (0,) * a.ndim)
    in_specs = [pl.BlockSpec((tq, ATT_HEADS * ATT_HD), lambda b, i: (row0 // tq + b * nq + i, 0)),
                pl.BlockSpec((L, 2 * gw), lambda b, i: (row0 // L + b, 0))]
    args = [q, kv]
    vrows = ATT_HD + 16
    scratch = [pltpu.VMEM((ATT_KV, L, 2 * ATT_HD), BF16),
               pltpu.VMEM((ATT_KV, L // (2 * ATT_HD), vrows, 2 * ATT_HD), BF16)]
    out_specs = [pl.BlockSpec((tq, ATT_HEADS * ATT_HD), lambda b, i: (b * nq + i, 0))]
    out_shape = [jax.ShapeDtypeStruct((B * L, ATT_HEADS * ATT_HD), BF16)]
    if latent:
        ck, cv = cache
        P = ck.shape[2]
        cos, sin = (jnp.asarray(t) for t in _rope_tables(L))
        in_specs += [pl.BlockSpec((None, ATT_KV, P, ATT_HD), lambda b, i: (b, 0, 0, 0))] * 2
        args += [ck, cv]
        in_specs += [const(qg), const(kg), pl.BlockSpec((1, ATT_HEADS), lambda b, i: (0, 0)),
                     pl.BlockSpec((tq, gw), lambda b, i: (i, 0)), pl.BlockSpec((tq, gw), lambda b, i: (i, 0)),
                     const(cos), const(sin)]
        args += [qg, kg, sink[None], cos, sin, cos, sin]
        scratch += [pltpu.VMEM((ATT_KV, P, 2 * ATT_HD), BF16), pltpu.VMEM((ATT_KV, vrows, P), BF16)]
    else:
        in_specs += [const(qg), const(kg), pl.BlockSpec((1, ATT_HEADS), lambda b, i: (0, 0))]
        args += [qg, kg, sink[None]]
        cache_spec = pl.BlockSpec((None, ATT_KV, L, ATT_HD), lambda b, i: (b, 0, 0, 0))
        out_specs += [cache_spec, cache_spec]
        out_shape += [jax.ShapeDtypeStruct((B, ATT_KV, L, ATT_HD), F32)] * 2
    outs = pl.pallas_call(
        functools.partial(_attn_kernel, latent=latent, tq=tq),
        grid=(B, nq),
        in_specs=in_specs, out_specs=out_specs, out_shape=out_shape,
        scratch_shapes=scratch,
        compiler_params=_cparams("parallel", "arbitrary"),
        name="attn_latent" if latent else "attn_context",
    )(*args)
    return outs[0] if latent else outs


MOE_TM = 1024
MOE_TOK = 1024
RUN_ALIGN = 16
MOE_LOCAL = 2 * MOE_TOK + N_EXPERTS * RUN_ALIGN
MOE_MAX_TILES = (2 * 16384 + (16384 // MOE_TOK) * N_EXPERTS * (RUN_ALIGN - 1)) // MOE_TM + N_EXPERTS + 1
RUN_SIZES = tuple(RUN_ALIGN << b for b in range(7, -1, -1))
MOE_CHUNKS = 11
MOE_MAX_ITEMS = N_EXPERTS * MOE_CHUNKS + MOE_MAX_TILES - N_EXPERTS
ITEM_FULL, ITEM_DEAD, ITEM_NONE = -1, -2, -3


def _router_kernel(*refs, proj):
    n = _n_proj_refs(**proj)
    g_ref, sh_ref, sc_ref, wr_ref, br_ref, tri_ref, x_ref, lp_ref, wts_ref, runs_ref, cnt_ref = refs[n:]

    @pl.when(pl.program_id(0) == 0)
    def _():
        cnt_ref[...] = jnp.zeros_like(cnt_ref)

    x = _proj_value(refs[:n], pl.program_id(0) < proj["n_p"], **proj)
    x_ref[...] = x
    h = _norm_mod(x, g_ref[...], sh_ref[...], sc_ref[...])
    nt = (((1,), (1,)), ((), ()))
    h_hi = h.astype(BF16)
    h_lo = (h - h_hi.astype(F32)).astype(BF16)
    w_hi = wr_ref[...].astype(BF16)
    w_lo = (wr_ref[...] - w_hi.astype(F32)).astype(BF16)
    lg = (lax.dot_general(w_hi, h_hi, nt, preferred_element_type=F32)
          + lax.dot_general(w_hi, h_lo, nt, preferred_element_type=F32)
          + lax.dot_general(w_lo, h_hi, nt, preferred_element_type=F32)) + br_ref[...]
    row = lax.broadcasted_iota(jnp.int32, lg.shape, 0)
    m1 = jnp.max(lg, axis=0, keepdims=True)
    i1 = jnp.min(jnp.where(lg == m1, row, N_EXPERTS), axis=0, keepdims=True)
    l2 = jnp.where(row == i1, -jnp.inf, lg)
    m2 = jnp.max(l2, axis=0, keepdims=True)
    i2 = jnp.min(jnp.where(l2 == m2, row, N_EXPERTS), axis=0, keepdims=True)
    e2 = jnp.exp(m2 - m1)
    w1 = 1.0 / (1.0 + e2)
    wts_ref[...] = jnp.concatenate([w1, e2 * w1], axis=0)
    oh1 = (row == i1).astype(F32)
    oh2 = (row == i2).astype(F32)
    cs1 = _bdot(oh1, tri_ref[...])
    cs2 = _bdot(oh2, tri_ref[...])
    tot1 = jnp.sum(oh1, axis=1, keepdims=True)
    run = jnp.ceil((tot1 + jnp.sum(oh2, axis=1, keepdims=True)) * (1.0 / RUN_ALIGN)) * RUN_ALIGN
    run_b = jnp.broadcast_to(run, (N_EXPERTS, 128))
    er = lax.broadcasted_iota(jnp.int32, (N_EXPERTS, N_EXPERTS), 0)
    ec = lax.broadcasted_iota(jnp.int32, (N_EXPERTS, N_EXPERTS), 1)
    start = jnp.dot((ec < er).astype(F32), run_b, precision=HIGHEST, preferred_element_type=F32)
    last = lax.broadcasted_iota(jnp.int32, (N_EXPERTS, 128), 0) == N_EXPERTS - 1
    run_b = jnp.where(last, MOE_LOCAL - start, run_b)
    st = start[:, 0:1]
    p1 = jnp.sum(oh1 * (st + cs1), axis=0, keepdims=True)
    p2 = jnp.sum(oh2 * (st + tot1 + cs2), axis=0, keepdims=True)
    lp_ref[...] = jnp.concatenate([p1, p2], axis=0).astype(jnp.int32)
    lane = lax.broadcasted_iota(jnp.int32, (N_EXPERTS, 128), 1)
    runs_ref[...] = jnp.where(lane == 0, run_b, jnp.where(lane == 1, start, cnt_ref[...]))
    cnt_ref[...] = cnt_ref[...] + run_b


def moe_router(acts, w_out, x, g, mods, w_router, b_router, tm=MOE_TOK):
    T = x.shape[0]
    tri = jnp.asarray(np.triu(np.ones((tm, tm), np.float32), k=1)).astype(BF16)
    tok2 = lambda dt: jax.ShapeDtypeStruct((2, T), dt)
    p_specs, p_args, proj = _proj_inputs(acts, w_out, x, mods, tm)
    return pl.pallas_call(
        functools.partial(_router_kernel, proj=proj),
        grid=(T // tm,),
        in_specs=p_specs + [
                  pl.BlockSpec((1, D), lambda i: (0, 0)),
                  _mod_spec(3, tm), _mod_spec(4, tm),
                  pl.BlockSpec((N_EXPERTS, D), lambda i: (0, 0)),
                  pl.BlockSpec((N_EXPERTS, 1), lambda i: (0, 0)),
                  _const_spec(tri)],
        out_specs=[pl.BlockSpec((tm, D), lambda i: (i, 0)),
                   pl.BlockSpec((2, tm), lambda i: (0, i)),
                   pl.BlockSpec((2, tm), lambda i: (0, i)),
                   pl.BlockSpec((None, N_EXPERTS, 128), lambda i: (i, 0, 0)),
                   pl.BlockSpec((N_EXPERTS, 128), lambda i: (0, 0))],
        out_shape=[jax.ShapeDtypeStruct((T, D), F32), tok2(jnp.int32), tok2(F32),
                   jax.ShapeDtypeStruct((T // tm, N_EXPERTS, 128), F32),
                   jax.ShapeDtypeStruct((N_EXPERTS, 128), F32)],
        compiler_params=_cparams("arbitrary"),
        name="moe_router",
    )(*p_args, g, mods, mods, w_router.T, b_router[:, None], tri)


def moe_layout(runs, totals):
    rows = totals[:, 0].astype(jnp.int32)
    tiles = (rows + MOE_TM - 1) // MOE_TM
    tile_end = jnp.cumsum(tiles)
    group = (tile_end - tiles) * MOE_TM
    run_len = runs[:, :, 0].astype(jnp.int32)
    run_src = runs[:, :, 1].astype(jnp.int32)
    run_dst = group[None, :] + runs[:, :, 2].astype(jnp.int32)
    tail = jnp.stack([group + rows, tiles * MOE_TM - rows]).astype(jnp.int32)
    n_tiles = tile_end[-1]
    t = jnp.arange(MOE_MAX_TILES, dtype=jnp.int32)
    tile_e = jnp.sum(t[:, None] >= tile_end[None, :], axis=1).astype(jnp.int32)
    last_e = jnp.sum((n_tiles - 1) >= tile_end).astype(jnp.int32)
    tile_e = jnp.where(t < n_tiles, tile_e, last_e)
    first = jnp.sum(jnp.where(tile_e[:, None] == jnp.arange(N_EXPERTS), (tile_end - tiles)[None, :], 0), axis=1)
    e_rows = jnp.sum(jnp.where(tile_e[:, None] == jnp.arange(N_EXPERTS), rows[None, :], 0), axis=1)
    tile_rows = jnp.where(t < n_tiles, jnp.clip(e_rows - (t - first) * MOE_TM, 0, MOE_TM), 0).astype(jnp.int32)
    run_tab = jnp.stack([run_len, run_src, run_dst]).reshape(3, -1)
    live = t < n_tiles
    is_first = (t == first) & live
    n_items = jnp.where(is_first, MOE_CHUNKS, 1)
    item_end = jnp.cumsum(n_items)
    j = jnp.arange(MOE_MAX_ITEMS, dtype=jnp.int32)
    it_tile = jnp.minimum(jnp.sum(j[:, None] >= item_end[None, :], axis=1), MOE_MAX_TILES - 1).astype(jnp.int32)
    chunk = j - jnp.take(item_end - n_items, it_tile)
    kind = jnp.where(jnp.take(is_first, it_tile), chunk, jnp.where(jnp.take(live, it_tile), ITEM_FULL, ITEM_DEAD))
    kind = jnp.where(j >= item_end[-1], ITEM_NONE, kind)
    items = jnp.stack([it_tile, kind.astype(jnp.int32)])
    return run_tab, tail, tile_e, n_tiles.astype(jnp.int32).reshape(1), tile_rows, items


def _run_copies(tab_ref, i, local_ref, global_ref, sem, to_global):
    out = []
    for e in range(N_EXPERTS):
        k = i * N_EXPERTS + e
        n, src, dst = tab_ref[0, k], tab_ref[1, k], tab_ref[2, k]
        for size in RUN_SIZES:
            done = (n // (2 * size)) * (2 * size)
            loc = local_ref.at[pl.ds(pl.multiple_of(src + done, RUN_ALIGN), size), :]
            glo = global_ref.at[pl.ds(pl.multiple_of(dst + done, RUN_ALIGN), size), :]
            copy = pltpu.make_async_copy(loc, glo, sem) if to_global else pltpu.make_async_copy(glo, loc, sem)
            out.append(((n & size) != 0, copy))
    return out


def _start(copies, live=True):
    for pred, copy in copies:
        pl.when(pred & live)(copy.start)


def _wait(copies, live=True):
    for pred, copy in copies:
        pl.when(pred & live)(copy.wait)


def _start_then_wait(copies):
    _start(copies)
    _wait(copies)


def _dispatch_kernel(tab_ref, tail_ref, nt_ref, lp_ref, x_ref, g_ref, sh_ref, sc_ref, xs_ref, hs_scr, z_scr, sem):
    i = pl.program_id(0)
    tm = x_ref.shape[0]
    buf = i % 2
    tg = 256
    slot = lax.broadcasted_iota(jnp.int32, (MOE_LOCAL, tg), 0)
    acc = None
    for k in range(tm // tg):
        sl = slice(k * tg, (k + 1) * tg)
        h = _norm_mod(x_ref[sl, :], g_ref[...], sh_ref[...], sc_ref[...]).astype(BF16)
        perm = jnp.where((slot == lp_ref[0:1, sl]) | (slot == lp_ref[1:2, sl]), 1.0, 0.0).astype(BF16)
        part = jnp.dot(perm, h, preferred_element_type=F32)
        acc = part if acc is None else acc + part
    hs_scr[buf] = acc.astype(BF16)
    copies = _run_copies(tab_ref, i, hs_scr.at[buf], xs_ref, sem.at[buf], to_global=True)
    _start(copies)
    _wait(_run_copies(tab_ref, jnp.maximum(i - 1, 0), hs_scr.at[1 - buf], xs_ref, sem.at[1 - buf], to_global=True),
          live=i > 0)

    @pl.when(i == 0)
    def _():
        z_scr[...] = jnp.zeros_like(z_scr)
        zrows = z_scr.shape[0]

        def zero_tile(t, carry):
            for part in range(MOE_TM // zrows):
                dst = xs_ref.at[pl.ds(pl.multiple_of(t * MOE_TM + part * zrows, zrows), zrows), :]
                copy = pltpu.make_async_copy(z_scr, dst, sem.at[2])
                copy.start()
                copy.wait()
            return carry

        lax.fori_loop(nt_ref[0], MOE_MAX_TILES, zero_tile, 0)
        tails = []
        for e in range(N_EXPERTS):
            start, n = tail_ref[0, e], tail_ref[1, e]
            for size in RUN_SIZES:
                if size >= MOE_TM:
                    continue
                done = (n // (2 * size)) * (2 * size)
                dst = xs_ref.at[pl.ds(pl.multiple_of(start + done, RUN_ALIGN), size), :]
                tails.append(((n & size) != 0, pltpu.make_async_copy(z_scr.at[pl.ds(0, size), :], dst, sem.at[2])))
        _start_then_wait(tails)

    _wait(copies, live=i == pl.num_programs(0) - 1)


def moe_dispatch(x, g, mods, lp, run_tab, tail, n_tiles, tm=MOE_TOK):
    T = x.shape[0]
    n_rows = MOE_MAX_TILES * MOE_TM
    return pl.pallas_call(
        _dispatch_kernel,
        grid_spec=pltpu.PrefetchScalarGridSpec(
            num_scalar_prefetch=3,
            grid=(T // tm,),
            in_specs=[pl.BlockSpec((2, tm), lambda i, *_: (0, i)),
                      pl.BlockSpec((tm, D), lambda i, *_: (i, 0)),
                      pl.BlockSpec((1, D), lambda i, *_: (0, 0)),
                      _mod_spec(3, tm), _mod_spec(4, tm)],
            out_specs=pl.BlockSpec(memory_space=pl.ANY),
            scratch_shapes=[pltpu.VMEM((2, MOE_LOCAL, D), BF16), pltpu.VMEM((MOE_TM // 2, D), BF16),
                            pltpu.SemaphoreType.DMA((3,))]),
        out_shape=jax.ShapeDtypeStruct((n_rows, D), BF16),
        compiler_params=_cparams("arbitrary"),
        name="moe_dispatch",
    )(run_tab, tail, n_tiles, lp, x, g, mods, mods)


def _moe_group_kernel(it_ref, te_ref, nt_ref, tr_ref, x_ref, w1_ref, w3_ref, w2_ref, o_ref, acc_scr, b1_scr, b3_scr,
                      b2_scr):
    j = pl.program_id(0)
    kind = it_ref[1, j]
    rows = tr_ref[it_ref[0, j]]
    half = MOE_TM // 2

    def chunk(c):
        @pl.when(rows > half)
        def _():
            _swiglu_accumulate(x_ref, acc_scr, b1_scr.at[c], b3_scr.at[c], b2_scr.at[c], MOE_TM)

        @pl.when(rows <= half)
        def _():
            _swiglu_accumulate(x_ref, acc_scr, b1_scr.at[c], b3_scr.at[c], b2_scr.at[c], half)

    @pl.when((kind == 0) | (kind == ITEM_FULL))
    def _():
        acc_scr[...] = jnp.zeros_like(acc_scr)

    @pl.when(kind >= 0)
    def _():
        b1_scr[kind] = w1_ref[...].astype(BF16)
        b3_scr[kind] = w3_ref[...].astype(BF16)
        b2_scr[kind] = w2_ref[...].astype(BF16)
        chunk(kind)

    for n_rows, pred in ((MOE_TM, rows > half), (half, rows <= half)):
        @pl.when((kind == ITEM_FULL) & pred)
        def _():
            for c in range(MOE_CHUNKS):
                _swiglu_accumulate(x_ref, acc_scr, b1_scr.at[c], b3_scr.at[c], b2_scr.at[c], n_rows)

    @pl.when((kind == MOE_CHUNKS - 1) | (kind == ITEM_FULL))
    def _():
        o_ref[...] = acc_scr[...].astype(o_ref.dtype)

    @pl.when(kind == ITEM_DEAD)
    def _():
        o_ref[...] = jnp.zeros_like(o_ref)


def moe_grouped_swiglu(xs, items, tile_e, n_tiles, tile_rows, w1, w3, w2):
    tf = D_FF // MOE_CHUNKS
    tile = lambda j, it, te, nt, tr: it[0, j]
    row_in = lambda j, it, te, nt, tr: (jnp.minimum(tile(j, it, te, nt, tr), jnp.maximum(nt[0] - 1, 0)), 0)
    wchunk = lambda j, it: jnp.where(it[1, j] >= 0, it[1, j], MOE_CHUNKS - 1)
    wcol = lambda j, it, te, nt, tr: (te[it[0, j]], 0, wchunk(j, it))
    wrow = lambda j, it, te, nt, tr: (te[it[0, j]], wchunk(j, it), 0)
    return pl.pallas_call(
        _moe_group_kernel,
        grid_spec=pltpu.PrefetchScalarGridSpec(
            num_scalar_prefetch=4,
            grid=(MOE_MAX_ITEMS,),
            in_specs=[pl.BlockSpec((MOE_TM, D), row_in),
                      pl.BlockSpec((None, D, tf), wcol),
                      pl.BlockSpec((None, D, tf), wcol),
                      pl.BlockSpec((None, tf, D), wrow)],
            out_specs=pl.BlockSpec((MOE_TM, D), lambda j, it, te, nt, tr: (it[0, j], 0)),
            scratch_shapes=[pltpu.VMEM((MOE_TM, D), F32), pltpu.VMEM((MOE_CHUNKS, D, tf), BF16),
                            pltpu.VMEM((MOE_CHUNKS, D, tf), BF16), pltpu.VMEM((MOE_CHUNKS, tf, D), BF16)]),
        out_shape=jax.ShapeDtypeStruct(xs.shape, BF16),
        compiler_params=_cparams("arbitrary"),
        name="moe_grouped",
    )(items, tile_e, n_tiles, tile_rows, xs, w1, w3, w2)


def _combine_kernel(tab_ref, lp_ref, wt_ref, x_ref, gate_ref, ys_ref, op_ref, os_ref, yl_scr, sem, *, n_p):
    i = pl.program_id(0)
    tm = x_ref.shape[0]
    buf = i % 2
    last = pl.num_programs(0) - 1
    gather = lambda t, b: _run_copies(tab_ref, t, yl_scr.at[b], ys_ref, sem.at[b], to_global=False)
    _start(gather(i, buf), live=i == 0)
    _start(gather(jnp.minimum(i + 1, last), 1 - buf), live=i < last)
    _wait(gather(i, buf))
    bounds = (0, 512, 1024, 1536, MOE_LOCAL)
    moe = None
    for lo, hi in zip(bounds[:-1], bounds[1:]):
        slot = lo + lax.broadcasted_iota(jnp.int32, (tm, hi - lo), 1)
        mix = (jnp.where(slot == lp_ref[:, 0:1], wt_ref[:, 0:1], 0.0)
               + jnp.where(slot == lp_ref[:, 1:2], wt_ref[:, 1:2], 0.0)).astype(BF16)
        part = jnp.dot(mix, yl_scr[buf, lo:hi, :], preferred_element_type=F32)
        moe = part if moe is None else moe + part
    out = x_ref[...] + gate_ref[...] * moe

    @pl.when(pl.program_id(0) < n_p)
    def _():
        op_ref[...] = out

    @pl.when(pl.program_id(0) >= n_p)
    def _():
        os_ref[...] = out


def moe_combine(x, mods, lp, wts, run_tab, ys, t_prompt, tm=MOE_TOK):
    T = x.shape[0]
    n_p = t_prompt // tm
    return pl.pallas_call(
        functools.partial(_combine_kernel, n_p=n_p),
        grid_spec=pltpu.PrefetchScalarGridSpec(
            num_scalar_prefetch=1,
            grid=(T // tm,),
            in_specs=[pl.BlockSpec((tm, 2), lambda i, *_: (i, 0)),
                      pl.BlockSpec((tm, 2), lambda i, *_: (i, 0)),
                      pl.BlockSpec((tm, D), lambda i, *_: (i, 0)),
                      _mod_spec(5, tm),
                      pl.BlockSpec(memory_space=pl.ANY)],
            out_specs=_part_specs((tm, D), n_p),
            scratch_shapes=[pltpu.VMEM((2, MOE_LOCAL, D), BF16), pltpu.SemaphoreType.DMA((2,))]),
        out_shape=[jax.ShapeDtypeStruct((t_prompt, D), F32), jax.ShapeDtypeStruct((T - t_prompt, D), F32)],
        compiler_params=_cparams("arbitrary"),
        name="moe_combine",
    )(run_tab, lp.T, wts.T, x, mods, ys)


def kernel(x_prompt, x_sample, state_C, state_n, state_m, cache_k, cache_v, c, c_ctx, norm1_g, norm2_g, w_ada, b_ada, ev_w_in, ev_conv, hy_w1, hy_b1, hy_w2, hy_b2, hy_w3, hy_freq, hy_d, ml_b_gate, ml_norm_g, ev_w_out, ff_w1, ff_w3, ff_w2, at_w_qkv, at_q_g, at_k_g, at_sink, at_w_out, moe_w_router, moe_b_router, moe_w1, moe_w3, moe_w2):
    BP, LP, _ = x_prompt.shape
    BS, LS, _ = x_sample.shape
    TP = BP * LP
    assert TP % GROUP == 0 and TP // GROUP == N_PROMPT_GROUPS and LS == GROUP and BS == 8

    xp, xs = x_prompt.reshape(TP, D), x_sample.reshape(BS * LS, D)
    cond = jnp.concatenate([c_ctx[None], c, jnp.zeros((16 - 1 - BS, D), F32)], axis=0)
    mods = adaln_table(cond, w_ada, b_ada)

    u, gates = even_in_proj(xp, xs, norm1_g[0:1], mods[0], ev_w_in[0], ml_b_gate[0].reshape(1, N_GATES))
    hy = []
    for seq0, B, L, nb in ((0, BP, LP, 4), (TP // LS, BS, LS, 1)):
        fwd, inv = (jnp.asarray(t).astype(BF16) for t in _dft_tables(L))
        ka, kb = hyena_filter_spectra(L, hy_w1[0], hy_b1[0], hy_w2[0], hy_b2[0], hy_w3[0], hy_freq[0], fwd)
        hy.append(hyena_mix(u, seq0, B, L, ev_conv[0], hy_d[0], fwd, inv, ka, kb, nb))
    ml_p, new_C, new_n, new_m = mlstm_mix(u, gates, 0, BP, LP, ml_norm_g[0], want_state=True)
    ml_s = mlstm_mix(u, gates, TP // LS, BS, LS, ml_norm_g[0],
                     state=(state_C[:, 0], state_n[:, 0], state_m[:, 0]))
    x = ffn_residual([hy, (ml_p, ml_s)], ev_w_out[0], (xp, xs), norm2_g[0:1], mods[0], ff_w1[0], ff_w3[0], ff_w2[0])

    q, kv = qkv_proj(x, norm1_g[1:2], mods[1], at_w_qkv[0])
    o_p, new_k, new_v = attention(q, kv, 0, at_q_g[0], at_k_g[0], at_sink[0], BP, LP)
    o_s = attention(q, kv, TP, at_q_g[0], at_k_g[0], at_sink[0], BS, LS, cache=(cache_k[:, 0], cache_v[:, 0]))
    x, lp, wts, runs, totals = moe_router([(o_p, o_s)], at_w_out[0], x, norm2_g[1:2], mods[1],
                                          moe_w_router[0], moe_b_router[0])
    run_tab, tail, tile_e, n_tiles, tile_rows, items = moe_layout(runs, totals)
    xsort = moe_dispatch(x, norm2_g[1:2], mods[1], lp, run_tab, tail, n_tiles)
    ysort = moe_grouped_swiglu(xsort, items, tile_e, n_tiles, tile_rows, moe_w1[0], moe_w3[0], moe_w2[0])
    yp, ys = moe_combine(x, mods[1], lp, wts, run_tab, ysort, TP)

    return (yp.reshape(BP, LP, D), ys.reshape(BS, LS, D),
            new_C[:, None], new_n[:, None], new_m[:, None], new_k[:, None], new_v[:, None])
```

```python
import functools
import math

import numpy as np
import jax
import jax.numpy as jnp
from jax import lax
from jax.experimental import pallas as pl
from jax.experimental.pallas import tpu as pltpu

F32 = jnp.float32
BF16 = jnp.bfloat16
HIGHEST = lax.Precision.HIGHEST

D = 1024
GROUP = 1024
N_PROMPT_GROUPS = 8
HY_W = 512
ML_HEADS = 4
ML_HD = 128
ML_CHUNK = 256
EVEN_MAIN = 3 * HY_W + 4 * 512
N_GATES = 16
ATT_HD = 64
ATT_HEADS = 16
ATT_KV = 4
WINDOW = 128
GRID_W = 64
ROPE_BASE = 10000.0
D_FF = 2816
N_EXPERTS = 8
EPS = 1e-6
NEG = -1e30
VMEM_LIMIT = 56 * 1024 * 1024


def _cparams(*sem):
    return pltpu.CompilerParams(dimension_semantics=sem, vmem_limit_bytes=VMEM_LIMIT)


def _mod_row(i, tm):
    return jnp.maximum(i * tm // GROUP - (N_PROMPT_GROUPS - 1), 0)


def _silu(x):
    return x * jax.nn.sigmoid(x)


def _bdot(a, b):
    return jnp.dot(a.astype(BF16), b.astype(BF16), preferred_element_type=F32)


def _norm_mod(x, g, sh, sc):
    y = x * lax.rsqrt(jnp.mean(x * x, axis=-1, keepdims=True) + EPS) * g
    return y * (1.0 + sc) + sh


def _adaln_kernel(c_ref, w_ref, b_ref, o_ref):
    s = _silu(c_ref[...])
    s_hi = s.astype(BF16)
    s_lo = (s - s_hi.astype(F32)).astype(BF16)
    w_hi = w_ref[...].astype(BF16)
    w_lo = (w_ref[...] - w_hi.astype(F32)).astype(BF16)
    dot = functools.partial(jnp.dot, preferred_element_type=F32)
    o_ref[...] = dot(s_hi, w_hi) + dot(s_lo, w_hi) + dot(s_hi, w_lo) + b_ref[...]


def adaln_table(cond, w_ada, b_ada):
    depth = w_ada.shape[0]
    tn = 1536
    out = pl.pallas_call(
        _adaln_kernel,
        grid=(depth, 6 * D // tn),
        in_specs=[pl.BlockSpec((16, D), lambda l, j: (0, 0)),
                  pl.BlockSpec((None, D, tn), lambda l, j: (l, 0, j)),
                  pl.BlockSpec((None, 1, tn), lambda l, j: (l, 0, j))],
        out_specs=pl.BlockSpec((None, 16, tn), lambda l, j: (l, 0, j)),
        out_shape=jax.ShapeDtypeStruct((depth, 16, 6 * D), F32),
        compiler_params=_cparams("parallel", "parallel"),
        name="adaln",
    )(cond, w_ada, b_ada.reshape(depth, 1, 6 * D))
    return out.reshape(depth, 16, 1, 6 * D)


def _same_tile(i):
    return i


def _part_specs(block, n_p, tile_of=_same_tile):
    return [pl.BlockSpec(block, lambda i, *_: (jnp.minimum(tile_of(i), n_p - 1), 0)),
            pl.BlockSpec(block, lambda i, *_: (jnp.maximum(tile_of(i) - n_p, 0), 0))]


def _pick(is_prompt, p_ref, s_ref):
    return jnp.where(is_prompt, p_ref[...], s_ref[...])


def _mod_spec(k, tm, tile_of=_same_tile):
    return pl.BlockSpec((None, 1, D), lambda i, *_: (_mod_row(tile_of(i), tm), 0, k))


def _log_sigmoid(x):
    return jnp.minimum(x, 0.0) - jnp.log(1.0 + jnp.exp(-jnp.abs(x)))


def _split3(x):
    hi = x.astype(BF16)
    r = x - hi.astype(F32)
    mid = r.astype(BF16)
    return hi, mid, (r - mid.astype(F32)).astype(BF16)


def _even_in_kernel(xp_ref, xs_ref, g_ref, sh_ref, sc_ref, w_ref, bg_ref, lo_ref, up_ref, u_ref, gate_ref, *, n_p, tn):
    is_prompt = pl.program_id(0) < n_p
    h = _norm_mod(_pick(is_prompt, xp_ref, xs_ref), g_ref[...], sh_ref[...], sc_ref[...]).astype(BF16)

    def main_chunks(j0, j1):
        for j in range(j0, j1):
            u_ref[:, j * tn:(j + 1) * tn] = _bdot(h, w_ref[:, j * tn:(j + 1) * tn]).astype(u_ref.dtype)

    n_main = EVEN_MAIN // tn
    gates = _bdot(h, w_ref[:, EVEN_MAIN:]) + bg_ref[...]
    main_chunks(0, n_main // 2)
    lf = _log_sigmoid(gates)
    col = lax.broadcasted_iota(jnp.int32, (1, N_GATES), 1)
    is_forget = (col // ML_HEADS) % 2 == 1
    is_rev = col >= N_GATES // 2
    for ch in range(h.shape[0] // ML_CHUNK):
        sl = slice(ch * ML_CHUNK, (ch + 1) * ML_CHUNK)
        parts = _split3(lf[sl])
        cf = sum(jnp.dot(lo_ref[...], p, preferred_element_type=F32) for p in parts)
        cr = sum(jnp.dot(up_ref[...], p, preferred_element_type=F32) for p in parts)
        gate_ref[sl, :] = jnp.where(is_forget, jnp.where(is_rev, cr, cf), gates[sl])
    main_chunks(n_main // 2, n_main)


def even_in_proj(xp, xs, g, mods, w_in, b_gate, tm=1024, tn=512):
    T = xp.shape[0] + xs.shape[0]
    tri = np.tril(np.ones((ML_CHUNK, ML_CHUNK), np.float32))
    lo, up = jnp.asarray(tri).astype(BF16), jnp.asarray(tri.T).astype(BF16)
    return pl.pallas_call(
        functools.partial(_even_in_kernel, n_p=xp.shape[0] // tm, tn=tn),
        grid=(T // tm,),
        in_specs=_part_specs((tm, D), xp.shape[0] // tm) + [
                  pl.BlockSpec((1, D), lambda i: (0, 0)),
                  _mod_spec(0, tm), _mod_spec(1, tm),
                  _const_spec(w_in),
                  pl.BlockSpec((1, N_GATES), lambda i: (0, 0)),
                  _const_spec(lo), _const_spec(up)],
        out_specs=[pl.BlockSpec((tm, EVEN_MAIN), lambda i: (i, 0)),
                   pl.BlockSpec((tm, N_GATES), lambda i: (i, 0))],
        out_shape=[jax.ShapeDtypeStruct((T, EVEN_MAIN), BF16),
                   jax.ShapeDtypeStruct((T, N_GATES), F32)],
        compiler_params=_cparams("parallel"),
        name="even_in_proj",
    )(xp, xs, g, mods, mods, w_in, b_gate, lo, up)


def _dft_tables(L):
    n = 2 * L
    f = np.arange(L, dtype=np.int64)[:, None]
    s = np.arange(L, dtype=np.int64)[None, :]
    ang = 2.0 * np.pi * ((f * s) % n).astype(np.float64) / n
    fwd = np.concatenate([np.cos(ang), -np.sin(ang)], axis=0)
    fwd[L, :] = np.where(np.arange(L) % 2 == 0, 1.0, -1.0)
    t = np.arange(L, dtype=np.int64)[:, None]
    ff = np.arange(L, dtype=np.int64)[None, :]
    ang = 2.0 * np.pi * ((t * ff) % n).astype(np.float64) / n
    inv_re = 2.0 * np.cos(ang) / n
    inv_re[:, 0] = 1.0 / n
    inv_im = -2.0 * np.sin(ang) / n
    inv_im[:, 0] = np.where(np.arange(L) % 2 == 0, 1.0, -1.0) / n
    inv = np.concatenate([inv_re, inv_im], axis=1)
    return fwd.astype(np.float32), inv.astype(np.float32)


def _filter_tables(L):
    t = np.linspace(0.0, 1.0, L, dtype=np.float32).astype(np.float64)[:, None]
    w = 2.0 * math.pi * np.arange(L, dtype=np.float64)[:, None] / L
    bands = np.linspace(1e-4, 16 - 1, 16, dtype=np.float32).astype(np.float64)[None, :]
    z = np.concatenate([t, np.cos(bands * w), -np.sin(bands * w)], axis=-1)
    zp = np.zeros((L, 128), np.float64)
    zp[:, :z.shape[1]] = z
    max_decay = math.log(1e-2) / 0.3
    min_decay = math.log(1e-2) / 1.5
    deltas = np.linspace(min_decay, max_decay, HY_W, dtype=np.float32).astype(np.float64)
    decay = np.exp(-t * np.abs(deltas))
    return zp.astype(np.float32), decay.astype(np.float32)


def _hy_filter_kernel(z_ref, dec_ref, w1_ref, b1_ref, w2_ref, b2_ref, w3_ref, fr_ref, fwd_ref,
                      ka_ref, kb_ref):
    L = z_ref.shape[0]
    hdot = functools.partial(jnp.dot, precision=HIGHEST, preferred_element_type=F32)
    h = jnp.sin(fr_ref[0:1, :] * (hdot(z_ref[...], w1_ref[...]) + b1_ref[...]))
    h = jnp.sin(fr_ref[1:2, :] * (hdot(h, w2_ref[...]) + b2_ref[...]))
    h = hdot(h, w3_ref[...])
    row0 = lax.broadcasted_iota(jnp.int32, (L, 1), 0) == 0
    h0 = h[:, :HY_W] * dec_ref[...]
    h1 = h[:, HY_W:] * dec_ref[...]
    l1 = jnp.sum(jnp.abs(h0), axis=0, keepdims=True) + jnp.sum(jnp.abs(h1), axis=0, keepdims=True)
    inv = 1.0 / l1
    h0 = h0 * inv
    h1 = jnp.where(row0, 0.0, h1 * inv)
    f0 = _bdot(fwd_ref[...], h0)
    f1 = _bdot(fwd_ref[...], h1)
    ka_ref[...] = f0[:L] + f1[:L]
    kb_ref[...] = jnp.where(row0, f0[L:] + f1[L:], f0[L:] - f1[L:])


def _const_spec(a):
    return pl.BlockSpec(a.shape, lambda *_: (0,) * a.ndim, pipeline_mode=pl.Buffered(1))


def hyena_filter_spectra(L, w1, b1, w2, b2, w3, freq, fwd):
    z, dec = _filter_tables(L)
    pad2 = lambda a, r, c: jnp.pad(a, ((0, r - a.shape[0]), (0, c - a.shape[1])))
    args = (jnp.asarray(z), jnp.asarray(dec), pad2(w1, 128, 128), pad2(b1[None], 1, 128),
            pad2(w2, 128, 128), pad2(b2[None], 1, 128), pad2(w3, 128, 4 * HY_W), pad2(freq, 2, 128), fwd)
    in_specs = [_const_spec(a) for a in args]
    in_specs[6] = pl.BlockSpec((128, 2 * HY_W), lambda o: (0, o))
    shp = jax.ShapeDtypeStruct((2, L, HY_W), F32)
    out_spec = pl.BlockSpec((None, L, HY_W), lambda o: (o, 0, 0))
    return pl.pallas_call(
        _hy_filter_kernel,
        grid=(2,),
        in_specs=in_specs,
        out_specs=[out_spec, out_spec],
        out_shape=[shp, shp],
        compiler_params=_cparams("arbitrary"),
        name=f"hyena_filter_{L}",
    )(*args)


SHIFT_MATMUL_MAX_LEN = 256


def _hyena_kernel(u_ref, cw_ref, d_ref, fwd_ref, inv_ref, ka_ref, kb_ref, o_ref):
    nb, L = u_ref.shape[0], u_ref.shape[1]
    row = lax.broadcasted_iota(jnp.int32, (L, 1), 0)
    first, last = row == 0, row == L - 1
    fwd = fwd_ref[...].astype(BF16)
    inv = inv_ref[...].astype(BF16)

    def long_conv(z, o):
        zf = jnp.dot(fwd, z.astype(BF16), preferred_element_type=F32)
        a, b = zf[:L], zf[L:]
        ka, kb = ka_ref[o], kb_ref[o]
        yr = a * ka - jnp.where(first, 0.0, b * kb)
        yi = jnp.where(first, b * kb, a * kb + b * ka)
        return (jnp.dot(inv[:, :L], yr.astype(BF16), preferred_element_type=F32)
                + jnp.dot(inv[:, L:], yi.astype(BF16), preferred_element_type=F32))

    shift_on_mxu = L <= SHIFT_MATMUL_MAX_LEN
    if shift_on_mxu:
        ri = lax.broadcasted_iota(jnp.int32, (L, L), 0)
        ci = lax.broadcasted_iota(jnp.int32, (L, L), 1)
        down = jnp.where(ci == ri - 1, 1.0, 0.0).astype(BF16)
        up = jnp.where(ci == ri + 1, 1.0, 0.0).astype(BF16)

    for bi in range(nb):
        if shift_on_mxu:
            ub = u_ref[bi]
            prev = jnp.dot(down, ub, preferred_element_type=F32)
            nxt = jnp.dot(up, ub, preferred_element_type=F32)
            u = ub.astype(F32)
        else:
            u = u_ref[bi].astype(F32)
            prev = jnp.where(first, 0.0, pltpu.roll(u, 1, 0))
            nxt = jnp.where(last, 0.0, pltpu.roll(u, L - 1, 0))
        u = prev * cw_ref[0:1, :] + u * cw_ref[1:2, :] + nxt * cw_ref[2:3, :]
        v, x1, x2 = u[:, :HY_W], u[:, HY_W:2 * HY_W], u[:, 2 * HY_W:]
        z = x1 * (long_conv(v, 0) + d_ref[0:1, :] * v)
        z = x2 * (long_conv(z, 1) + d_ref[1:2, :] * z)
        o_ref[bi] = z.astype(o_ref.dtype)


def hyena_mix(u, seq0, B, L, conv_w, d_skip, fwd, inv, ka, kb, nb):
    u3 = u.reshape(-1, L, EVEN_MAIN)
    full = _const_spec
    out = pl.pallas_call(
        _hyena_kernel,
        grid=(B // nb,),
        in_specs=[pl.BlockSpec((nb, L, 3 * HY_W), lambda b: (b + seq0 // nb, 0, 0)),
                  full(conv_w), full(d_skip), full(fwd), full(inv), full(ka), full(kb)],
        out_specs=pl.BlockSpec((nb, L, HY_W), lambda b: (b, 0, 0)),
        out_shape=jax.ShapeDtypeStruct((B, L, HY_W), BF16),
        compiler_params=_cparams("parallel"),
        name=f"hyena_{L}",
    )(u3, conv_w, d_skip, fwd, inv, ka, kb)
    return out.reshape(B * L, HY_W)


def _mlstm_kernel(*refs, has_state, want_state):
    q_ref, k_ref, v_ref, o_ref, gc_ref, gr_ref, ng_ref = refs[:7]
    refs = refs[7:]
    if has_state:
        c0t_ref, n0b_ref, m0_ref = refs[:3]
        refs = refs[3:]
    y_ref = refs[0]
    if want_state:
        c_out, n_out, m_out = refs[1:4]
    L, d = q_ref.shape[0], ML_HD
    T = min(ML_CHUNK, L)
    nc = L // T
    scale = 1.0 / math.sqrt(d)
    nt = (((1,), (1,)), ((), ()))
    si = lax.broadcasted_iota(jnp.int32, (T, T), 0)
    ti = lax.broadcasted_iota(jnp.int32, (T, T), 1)
    allowed = (si <= ti, si >= ti)
    chains = [(dr, h) for dr in range(2) for h in range(ML_HEADS)]
    gcol = lambda dr, gi, h: dr * 2 * ML_HEADS + gi * ML_HEADS + h

    caug_t, m = {}, {}
    for ch in chains:
        dr, h = ch
        if has_state:
            caug_t[ch] = jnp.concatenate([c0t_ref[dr, h], n0b_ref[dr, h]], axis=0)
            m[ch] = m0_ref[dr, h:h + 1, 0:1]
        else:
            caug_t[ch], m[ch] = jnp.zeros((2 * d, d), F32), jnp.zeros((1, 1), F32)

    chunk_cache = {}

    def chunk_data(h, j):
        if (h, j) not in chunk_cache:
            sl, hl = slice(j * T, (j + 1) * T), slice(h * d, (h + 1) * d)
            q = q_ref[sl, hl]
            ks = (k_ref[sl, hl].astype(F32) * scale).astype(BF16)
            v_t = v_ref[sl, hl].astype(F32).T
            vaug_t = jnp.concatenate([v_t, jnp.ones((d, T), F32)], axis=0).astype(BF16)
            s_raw = lax.dot_general(ks, q, nt, preferred_element_type=F32)
            chunk_cache[(h, j)] = (q, ks, v_t, vaug_t, s_raw)
        return chunk_cache[(h, j)]

    h_sum = {}
    for it in range(nc):
        step = {ch: (it if ch[0] == 0 else nc - 1 - it) for ch in chains}
        data = {ch: chunk_data(ch[1], step[ch]) for ch in chains}
        inter_t = {ch: lax.dot_general(caug_t[ch].astype(BF16), data[ch][0], nt, preferred_element_type=F32)
                   for ch in chains}
        gate = {}
        for ch in chains:
            dr, h = ch
            sl = slice(step[ch] * T, (step[ch] + 1) * T)
            li_r, b_r = gr_ref[gcol(dr, 0, h):gcol(dr, 0, h) + 1, sl], gr_ref[gcol(dr, 1, h):gcol(dr, 1, h) + 1, sl]
            src = gc_ref[sl, gcol(dr, 0, h):gcol(dr, 0, h) + 1] - gc_ref[sl, gcol(dr, 1, h):gcol(dr, 1, h) + 1]
            dm = jnp.where(allowed[dr], src + b_r, NEG)
            inter = b_r + m[ch]
            m_t = jnp.maximum(inter, jnp.max(dm, axis=0, keepdims=True))
            b_end = b_r[:, T - 1:T] if dr == 0 else b_r[:, 0:1]
            g_r = b_end - b_r + li_r
            m_new = jnp.maximum(b_end + m[ch], jnp.max(g_r, axis=1, keepdims=True))
            gate[ch] = (jnp.exp(dm - m_t), jnp.exp(inter - m_t), jnp.exp(-m_t), jnp.exp(g_r - m_new),
                        jnp.exp(b_end + m[ch] - m_new), m_new)
        for ch in chains:
            q, ks, v_t, vaug_t, s_raw = data[ch]
            w_intra, w_inter, floor, w_tok, decay, m_new = gate[ch]
            acc = jnp.dot(vaug_t, (s_raw * w_intra).astype(BF16), preferred_element_type=F32) + w_inter * inter_t[ch]
            h_t = acc[:d] / jnp.maximum(jnp.abs(acc[d:]), floor)
            key = (ch[1], step[ch])
            h_sum[key] = h_t if key not in h_sum else h_sum[key] + h_t
            vw_t = jnp.concatenate([v_t * w_tok, jnp.broadcast_to(w_tok, (d, T))], axis=0).astype(BF16)
            caug_t[ch] = decay * caug_t[ch] + jnp.dot(vw_t, ks, preferred_element_type=F32)
            m[ch] = m_new

    if want_state:
        for ch in chains:
            dr, h = ch
            c_out[dr, h] = caug_t[ch][:d].T
            n_out[dr, h:h + 1, :] = caug_t[ch][d:d + 1, :]
            m_out[dr, h:h + 1, :] = jnp.broadcast_to(m[ch], (1, d))
    for h in range(ML_HEADS):
        for j in range(nc):
            sl, hl = slice(j * T, (j + 1) * T), slice(h * d, (h + 1) * d)
            hv = h_sum[(h, j)].T
            y = hv * lax.rsqrt(jnp.mean(hv * hv, axis=-1, keepdims=True) + EPS) * ng_ref[:, hl]
            y_ref[sl, hl] = (y * jax.nn.sigmoid(o_ref[sl, hl].astype(F32))).astype(y_ref.dtype)


def mlstm_mix(u, gates, seq0, B, L, norm_g, state=None, want_state=False):
    u3 = u.reshape(-1, L, EVEN_MAIN)
    gc = gates.reshape(-1, L, N_GATES)[seq0:seq0 + B]
    gr = gc.transpose(0, 2, 1)
    width = ML_HEADS * ML_HD
    col = lambda i: pl.BlockSpec((None, L, width), lambda b: (b + seq0, 0, (3 * HY_W + i * width) // width))
    in_specs = [col(0), col(1), col(2), col(3),
                pl.BlockSpec((None, L, N_GATES), lambda b: (b, 0, 0)),
                pl.BlockSpec((None, N_GATES, L), lambda b: (b, 0, 0)),
                pl.BlockSpec((1, width), lambda b: (0, 0))]
    args = [u3, u3, u3, u3, gc, gr, norm_g.reshape(1, width)]
    sspec = pl.BlockSpec((None, 2, ML_HEADS, ML_HD, ML_HD), lambda b: (b, 0, 0, 0, 0))
    vspec = pl.BlockSpec((None, 2, ML_HEADS, ML_HD), lambda b: (b, 0, 0, 0))
    if state is not None:
        C0, n0, m0 = state
        in_specs += [sspec, sspec, vspec]
        args += [C0.swapaxes(-1, -2), jnp.broadcast_to(n0[..., None, :], C0.shape),
                 jnp.broadcast_to(m0[..., None], n0.shape)]
    out_specs = [pl.BlockSpec((None, L, width), lambda b: (b, 0, 0))]
    out_shape = [jax.ShapeDtypeStruct((B, L, width), BF16)]
    if want_state:
        out_specs += [sspec, vspec, vspec]
        out_shape += [jax.ShapeDtypeStruct((B, 2, ML_HEADS, ML_HD, ML_HD), F32),
                      jax.ShapeDtypeStruct((B, 2, ML_HEADS, ML_HD), F32),
                      jax.ShapeDtypeStruct((B, 2, ML_HEADS, ML_HD), F32)]
    outs = pl.pallas_call(
        functools.partial(_mlstm_kernel, has_state=state is not None, want_state=want_state),
        grid=(B,),
        in_specs=in_specs, out_specs=out_specs, out_shape=out_shape,
        compiler_params=_cparams("parallel"),
        name=f"mlstm_{L}",
    )(*args)
    y = outs[0].reshape(B * L, width)
    if not want_state:
        return y
    _, C, n, m = outs
    return y, C, n, m[..., 0]


def _proj_inputs(acts, w, x, mods, tm, tile_of=_same_tile):
    xs = tuple(x) if isinstance(x, (tuple, list)) else (x,)
    n_p = acts[0][0].shape[0] // tm
    specs = []
    for pair in acts:
        specs += _part_specs((tm, pair[0].shape[1]), n_p, tile_of)
    specs.append(pl.BlockSpec(w.shape, lambda *_: (0, 0), pipeline_mode=pl.Buffered(1)))
    specs += (_part_specs((tm, D), n_p, tile_of) if len(xs) == 2
              else [pl.BlockSpec((tm, D), lambda i, *_: (tile_of(i), 0))])
    specs.append(_mod_spec(2, tm, tile_of))
    args = [a for pair in acts for a in pair] + [w, *xs, mods]
    return specs, args, dict(n_in=len(acts), n_x=len(xs), n_p=n_p)


def _proj_value(refs, is_prompt, n_in, n_x, n_p):
    a_refs = refs[:2 * n_in]
    w_ref = refs[2 * n_in]
    x_refs = refs[2 * n_in + 1:2 * n_in + 1 + n_x]
    gate_ref = refs[2 * n_in + 1 + n_x]
    k0 = 0
    acc = None
    for j in range(n_in):
        a = _pick(is_prompt, a_refs[2 * j], a_refs[2 * j + 1])
        kw = a.shape[1]
        part = _bdot(a, w_ref[k0:k0 + kw, :])
        acc = part if acc is None else acc + part
        k0 += kw
    x = _pick(is_prompt, *x_refs) if n_x == 2 else x_refs[0][...]
    return x + gate_ref[...] * acc


def _n_proj_refs(n_in, n_x, n_p):
    return 2 * n_in + 1 + n_x + 1


SWIGLU_ROWS = 512


def _swiglu_accumulate(h_scr, acc_scr, w1_ref, w3_ref, w2_ref, rows, scale=None):
    w1, w3, w2 = w1_ref[...].astype(BF16), w3_ref[...].astype(BF16), w2_ref[...].astype(BF16)
    groups = [slice(r, r + SWIGLU_ROWS) for r in range(0, rows, SWIGLU_ROWS)]
    ups = []
    for sl in groups:
        h = h_scr[sl, :]
        ups.append((jnp.dot(h, w1, preferred_element_type=F32), jnp.dot(h, w3, preferred_element_type=F32)))
    for sl, (a, b) in zip(groups, ups):
        mid = (_silu(a) * b).astype(BF16)
        down = jnp.dot(mid, w2, preferred_element_type=F32)
        acc_scr[sl, :] += down if scale is None else scale * down


def _ffn_kernel(*refs, proj, nc):
    n = _n_proj_refs(**proj)
    g_ref, sh_ref, sc_ref, gate_ref, w1_ref, w3_ref, w2_ref, o_ref, h_scr, b1_scr, b3_scr, b2_scr = refs[n:]
    i = pl.program_id(0)
    tile = jnp.maximum(i - (nc - 1), 0)
    is_prompt = tile < proj["n_p"]
    rows = h_scr.shape[0]

    def start_tile():
        x = _proj_value(refs[:n], is_prompt, **proj)
        o_ref[...] = x
        h_scr[...] = _norm_mod(x, g_ref[...], sh_ref[...], sc_ref[...]).astype(BF16)

    def chunk(c):
        _swiglu_accumulate(h_scr, o_ref, b1_scr.at[c], b3_scr.at[c], b2_scr.at[c], rows, scale=gate_ref[...])

    pl.when(i == 0)(start_tile)

    @pl.when(i < nc)
    def _():
        b1_scr[i] = w1_ref[...].astype(BF16)
        b3_scr[i] = w3_ref[...].astype(BF16)
        b2_scr[i] = w2_ref[...].astype(BF16)
        chunk(i)

    @pl.when(i >= nc)
    def _():
        start_tile()
        for c in range(nc):
            chunk(c)


def ffn_residual(acts, w_out, x, g, mods, w1, w3, w2, tm=512, tf=256):
    T = sum(a.shape[0] for a in acts[0])
    nc = D_FF // tf
    tile_of = lambda i: jnp.maximum(i - (nc - 1), 0)
    chunk_of = lambda i: jnp.minimum(i, nc - 1)
    p_specs, p_args, proj = _proj_inputs(acts, w_out, x, mods, tm, tile_of)
    return pl.pallas_call(
        functools.partial(_ffn_kernel, proj=proj, nc=nc),
        grid=(T // tm + nc - 1,),
        in_specs=p_specs + [
                  pl.BlockSpec((1, D), lambda i: (0, 0)),
                  _mod_spec(3, tm, tile_of), _mod_spec(4, tm, tile_of), _mod_spec(5, tm, tile_of),
                  pl.BlockSpec((D, tf), lambda i: (0, chunk_of(i))),
                  pl.BlockSpec((D, tf), lambda i: (0, chunk_of(i))),
                  pl.BlockSpec((tf, D), lambda i: (chunk_of(i), 0))],
        out_specs=pl.BlockSpec((tm, D), lambda i: (tile_of(i), 0)),
        out_shape=jax.ShapeDtypeStruct((T, D), F32),
        scratch_shapes=[pltpu.VMEM((tm, D), BF16), pltpu.VMEM((nc, D, tf), BF16),
                        pltpu.VMEM((nc, D, tf), BF16), pltpu.VMEM((nc, tf, D), BF16)],
        compiler_params=_cparams("arbitrary"),
        name="ffn",
    )(*p_args, g, mods, mods, mods, w1, w3, w2)


def _qkv_kernel(x_ref, g_ref, sh_ref, sc_ref, w_ref, q_ref, kv_ref):
    h = _norm_mod(x_ref[...], g_ref[...], sh_ref[...], sc_ref[...]).astype(BF16)
    nq = q_ref.shape[1]
    q_ref[...] = _bdot(h, w_ref[:, :nq]).astype(q_ref.dtype)
    kv_ref[...] = _bdot(h, w_ref[:, nq:])


def qkv_proj(x, g, mods, w_qkv, tm=1024):
    T = x.shape[0]
    nq, nkv = ATT_HEADS * ATT_HD, 2 * ATT_KV * ATT_HD
    return pl.pallas_call(
        _qkv_kernel,
        grid=(T // tm,),
        in_specs=[pl.BlockSpec((tm, D), lambda i: (i, 0)),
                  pl.BlockSpec((1, D), lambda i: (0, 0)),
                  _mod_spec(0, tm), _mod_spec(1, tm),
                  _const_spec(w_qkv)],
        out_specs=[pl.BlockSpec((tm, nq), lambda i: (i, 0)),
                   pl.BlockSpec((tm, nkv), lambda i: (i, 0))],
        out_shape=[jax.ShapeDtypeStruct((T, nq), BF16), jax.ShapeDtypeStruct((T, nkv), F32)],
        compiler_params=_cparams("parallel"),
        name="qkv_proj",
    )(x, g, mods, mods, w_qkv)


def _rope_tables(L):
    half = ATT_HD // 2
    pos_r = (np.arange(L) // GRID_W).astype(np.float32)
    pos_c = (np.arange(L) % GRID_W).astype(np.float32)
    inv = (ROPE_BASE ** (-np.arange(0, half, 2, dtype=np.float32) / half)).astype(np.float32)
    cos = np.zeros((L, ATT_HD), np.float64)
    sin = np.zeros((L, ATT_HD), np.float64)
    for base, pos in ((0, pos_r), (half, pos_c)):
        ang = (pos[:, None] * inv[None, :]).astype(np.float32).astype(np.float64)
        cos[:, base:base + half] = np.concatenate([np.cos(ang), np.cos(ang)], axis=1)
        sin[:, base:base + half] = np.concatenate([-np.sin(ang), np.sin(ang)], axis=1)
    return (np.tile(cos, (1, 4)).astype(np.float32), np.tile(sin, (1, 4)).astype(np.float32))


def _seg_rms(x):
    w = x.shape[1]
    ri = lax.broadcasted_iota(jnp.int32, (w, w), 0) // ATT_HD
    ci = lax.broadcasted_iota(jnp.int32, (w, w), 1) // ATT_HD
    ss = _bdot(x * x, (ri == ci).astype(F32))
    return x * lax.rsqrt(ss * (1.0 / ATT_HD) + EPS)


LOG2E = 1.4426950408889634
ATTN_LOOKAHEAD = 1


def _exp2_bf16(x):
    return jnp.exp2(x.astype(BF16))


def _both_halves(tile, low):
    lane = lax.broadcasted_iota(jnp.int32, tile.shape, 1)
    other = pltpu.roll(tile, ATT_HD, 1)
    return jnp.where((lane < ATT_HD) == low, tile, other)


def _swap16(x):
    w = x.shape[1]
    lane = lax.broadcasted_iota(jnp.int32, x.shape, 1)
    return jnp.where(lane % 32 < 16, pltpu.roll(x, w - 16, 1), pltpu.roll(x, 16, 1))


def _attn_kernel(*refs, latent, tq):
    if latent:
        (q_ref, kv_ref, ck_ref, cv_ref, qg_ref, kg_ref, sink_ref, cosq_ref, sinq_ref, cosk_ref, sink_t_ref,
         o_ref, kk_scr, vt_scr, ckk_scr, cvt_scr) = refs
    else:
        q_ref, kv_ref, qg_ref, kg_ref, sink_ref, o_ref, ko_ref, vo_ref, kk_scr, vt_scr = refs
    L = kv_ref.shape[0]
    gw = ATT_KV * ATT_HD
    pw = 2 * ATT_HD
    qb = pl.program_id(1)

    vrows = vt_scr.shape[2]
    nblk = L // pw

    def vt_aug(tile, low):
        vt = tile.T[0:ATT_HD, :] if low else tile.T[ATT_HD:, :]
        return jnp.concatenate([vt, jnp.ones((vrows - ATT_HD, tile.shape[0]), F32)], axis=0).astype(BF16)

    @pl.when(qb == 0)
    def _():
        kn = _seg_rms(kv_ref[:, :gw]) * kg_ref[...]
        v = kv_ref[:, gw:]
        if latent:
            kn = kn * cosk_ref[...] + _swap16(kn) * sink_t_ref[...]
        else:
            for c in range(ATT_KV):
                ko_ref[c] = kn[:, c * ATT_HD:(c + 1) * ATT_HD]
                vo_ref[c] = v[:, c * ATT_HD:(c + 1) * ATT_HD]
        for c in range(ATT_KV):
            tile, low = slice((c // 2) * pw, (c // 2 + 1) * pw), c % 2 == 0
            kk_scr[c] = _both_halves(kn[:, tile], low).astype(BF16)
            for j in range(nblk):
                vt_scr[c, j] = vt_aug(v[j * pw:(j + 1) * pw, tile], low)
            if latent:
                ck, cv = ck_ref[c], cv_ref[c]
                ckk_scr[c] = jnp.concatenate([ck, ck], axis=1).astype(BF16)
                cvt_scr[c] = vt_aug(jnp.concatenate([cv, cv], axis=1), True)

    if latent:
        span = tq + 2 * WINDOW
        start = pl.multiple_of(jnp.clip(qb * tq - WINDOW, 0, L - span), WINDOW)
        blk0 = start // pw
        s_pos = start + lax.broadcasted_iota(jnp.int32, (span, tq), 0)
        t_pos = qb * tq + lax.broadcasted_iota(jnp.int32, (span, tq), 1)
        win_bias = jnp.where(jnp.abs(t_pos - s_pos) <= WINDOW, 0.0, NEG)
    else:
        span, blk0 = L, 0

    nt = (((1,), (1,)), ((), ()))
    low_q = lax.broadcasted_iota(jnp.int32, (tq, pw), 1) < ATT_HD
    def group_scores(c):
        qc = _seg_rms(q_ref[:, c * gw:(c + 1) * gw].astype(F32)) * qg_ref[...]
        if latent:
            qc = qc * cosq_ref[...] + _swap16(qc) * sinq_ref[...]
            kw = kk_scr[c, pl.ds(start, span), :]
        else:
            kw = kk_scr[c]
        qc = qc * (LOG2E / math.sqrt(ATT_HD))
        scores = []
        for g in range(ATT_KV):
            qt = qc[:, (g // 2) * pw:(g // 2 + 1) * pw]
            qm = jnp.where(low_q if g % 2 == 0 else ~low_q, qt, 0.0).astype(BF16)
            lw = lax.dot_general(kw, qm, nt, preferred_element_type=F32)
            lc = lax.dot_general(ckk_scr[c], qm, nt, preferred_element_type=F32) if latent else None
            scores.append((lw, lc))
        return scores

    def group_outputs(c, scores):
        vw = jnp.concatenate([vt_scr[c, blk0 + j] for j in range(span // pw)], axis=1)
        outs = []
        for g in range(ATT_KV):
            head = c * ATT_KV + g
            sink = sink_ref[:, head:head + 1] * LOG2E
            lw, lc = scores[g]
            if latent:
                lw = lw + win_bias
                mx = jnp.maximum(jnp.maximum(jnp.max(lw, axis=0, keepdims=True),
                                             jnp.max(lc, axis=0, keepdims=True)), sink)
                r = jnp.dot(vw, _exp2_bf16(lw - mx), preferred_element_type=F32) + jnp.dot(
                    cvt_scr[c], _exp2_bf16(lc - mx), preferred_element_type=F32)
            else:
                mx = jnp.maximum(jnp.max(lw, axis=0, keepdims=True), sink)
                r = jnp.dot(vw, _exp2_bf16(lw - mx), preferred_element_type=F32)
            den = r[ATT_HD:ATT_HD + 1, :] + jnp.exp2(sink - mx)
            outs.append(r[0:ATT_HD, :] / den)
        for t in range(2):
            o_ref[:, c * gw + t * pw:c * gw + (t + 1) * pw] = (
                jnp.concatenate(outs[2 * t:2 * t + 2], axis=0).T.astype(o_ref.dtype))

    pending = [group_scores(c) for c in range(ATTN_LOOKAHEAD)]
    for c in range(ATT_KV):
        if c + ATTN_LOOKAHEAD < ATT_KV:
            pending.append(group_scores(c + ATTN_LOOKAHEAD))
        group_outputs(c, pending.pop(0))


def attention(q, kv, row0, q_g, k_g, sink, B, L, cache=None, tq=256):
    latent = cache is not None
    gw = ATT_KV * ATT_HD
    qg = jnp.tile(q_g, ATT_KV)[None]
    kg = jnp.tile(k_g, ATT_KV)[None]
    nq = L // tq
    const = lambda a: pl.BlockSpec(a.shape, lambda b, i: (0,) * a.ndim)
    in_specs = [pl.BlockSpec((tq, ATT_HEADS * ATT_HD), lambda b, i: (row0 // tq + b * nq + i, 0)),
                pl.BlockSpec((L, 2 * gw), lambda b, i: (row0 // L + b, 0))]
    args = [q, kv]
    vrows = ATT_HD + 16
    scratch = [pltpu.VMEM((ATT_KV, L, 2 * ATT_HD), BF16),
               pltpu.VMEM((ATT_KV, L // (2 * ATT_HD), vrows, 2 * ATT_HD), BF16)]
    out_specs = [pl.BlockSpec((tq, ATT_HEADS * ATT_HD), lambda b, i: (b * nq + i, 0))]
    out_shape = [jax.ShapeDtypeStruct((B * L, ATT_HEADS * ATT_HD), BF16)]
    if latent:
        ck, cv = cache
        P = ck.shape[2]
        cos, sin = (jnp.asarray(t) for t in _rope_tables(L))
        in_specs += [pl.BlockSpec((None, ATT_KV, P, ATT_HD), lambda b, i: (b, 0, 0, 0))] * 2
        args += [ck, cv]
        in_specs += [const(qg), const(kg), pl.BlockSpec((1, ATT_HEADS), lambda b, i: (0, 0)),
                     pl.BlockSpec((tq, gw), lambda b, i: (i, 0)), pl.BlockSpec((tq, gw), lambda b, i: (i, 0)),
                     const(cos), const(sin)]
        args += [qg, kg, sink[None], cos, sin, cos, sin]
        scratch += [pltpu.VMEM((ATT_KV, P, 2 * ATT_HD), BF16), pltpu.VMEM((ATT_KV, vrows, P), BF16)]
    else:
        in_specs += [const(qg), const(kg), pl.BlockSpec((1, ATT_HEADS), lambda b, i: (0, 0))]
        args += [qg, kg, sink[None]]
        cache_spec = pl.BlockSpec((None, ATT_KV, L, ATT_HD), lambda b, i: (b, 0, 0, 0))
        out_specs += [cache_spec, cache_spec]
        out_shape += [jax.ShapeDtypeStruct((B, ATT_KV, L, ATT_HD), F32)] * 2
    outs = pl.pallas_call(
        functools.partial(_attn_kernel, latent=latent, tq=tq),
        grid=(B, nq),
        in_specs=in_specs, out_specs=out_specs, out_shape=out_shape,
        scratch_shapes=scratch,
        compiler_params=_cparams("parallel", "arbitrary"),
        name="attn_latent" if latent else "attn_context",
    )(*args)
    return outs[0] if latent else outs


MOE_TM = 1024
MOE_TOK = 1024
RUN_ALIGN = 16
MOE_LOCAL = 2 * MOE_TOK + N_EXPERTS * RUN_ALIGN
MOE_MAX_TILES = (2 * 16384 + (16384 // MOE_TOK) * N_EXPERTS * (RUN_ALIGN - 1)) // MOE_TM + N_EXPERTS + 1
RUN_SIZES = tuple(RUN_ALIGN << b for b in range(7, -1, -1))
MOE_CHUNKS = 11
MOE_MAX_ITEMS = N_EXPERTS * MOE_CHUNKS + MOE_MAX_TILES - N_EXPERTS
ITEM_FULL, ITEM_DEAD, ITEM_NONE = -1, -2, -3


def _router_kernel(*refs, proj):
    n = _n_proj_refs(**proj)
    g_ref, sh_ref, sc_ref, wr_ref, br_ref, tri_ref, x_ref, lp_ref, wts_ref, runs_ref, cnt_ref = refs[n:]

    @pl.when(pl.program_id(0) == 0)
    def _():
        cnt_ref[...] = jnp.zeros_like(cnt_ref)

    x = _proj_value(refs[:n], pl.program_id(0) < proj["n_p"], **proj)
    x_ref[...] = x
    h = _norm_mod(x, g_ref[...], sh_ref[...], sc_ref[...])
    nt = (((1,), (1,)), ((), ()))
    h_hi = h.astype(BF16)
    h_lo = (h - h_hi.astype(F32)).astype(BF16)
    w_hi = wr_ref[...].astype(BF16)
    w_lo = (wr_ref[...] - w_hi.astype(F32)).astype(BF16)
    lg = (lax.dot_general(w_hi, h_hi, nt, preferred_element_type=F32)
          + lax.dot_general(w_hi, h_lo, nt, preferred_element_type=F32)
          + lax.dot_general(w_lo, h_hi, nt, preferred_element_type=F32)) + br_ref[...]
    row = lax.broadcasted_iota(jnp.int32, lg.shape, 0)
    m1 = jnp.max(lg, axis=0, keepdims=True)
    i1 = jnp.min(jnp.where(lg == m1, row, N_EXPERTS), axis=0, keepdims=True)
    l2 = jnp.where(row == i1, -jnp.inf, lg)
    m2 = jnp.max(l2, axis=0, keepdims=True)
    i2 = jnp.min(jnp.where(l2 == m2, row, N_EXPERTS), axis=0, keepdims=True)
    e2 = jnp.exp(m2 - m1)
    w1 = 1.0 / (1.0 + e2)
    wts_ref[...] = jnp.concatenate([w1, e2 * w1], axis=0)
    oh1 = (row == i1).astype(F32)
    oh2 = (row == i2).astype(F32)
    cs1 = _bdot(oh1, tri_ref[...])
    cs2 = _bdot(oh2, tri_ref[...])
    tot1 = jnp.sum(oh1, axis=1, keepdims=True)
    run = jnp.ceil((tot1 + jnp.sum(oh2, axis=1, keepdims=True)) * (1.0 / RUN_ALIGN)) * RUN_ALIGN
    run_b = jnp.broadcast_to(run, (N_EXPERTS, 128))
    er = lax.broadcasted_iota(jnp.int32, (N_EXPERTS, N_EXPERTS), 0)
    ec = lax.broadcasted_iota(jnp.int32, (N_EXPERTS, N_EXPERTS), 1)
    start = jnp.dot((ec < er).astype(F32), run_b, precision=HIGHEST, preferred_element_type=F32)
    last = lax.broadcasted_iota(jnp.int32, (N_EXPERTS, 128), 0) == N_EXPERTS - 1
    run_b = jnp.where(last, MOE_LOCAL - start, run_b)
    st = start[:, 0:1]
    p1 = jnp.sum(oh1 * (st + cs1), axis=0, keepdims=True)
    p2 = jnp.sum(oh2 * (st + tot1 + cs2), axis=0, keepdims=True)
    lp_ref[...] = jnp.concatenate([p1, p2], axis=0).astype(jnp.int32)
    lane = lax.broadcasted_iota(jnp.int32, (N_EXPERTS, 128), 1)
    runs_ref[...] = jnp.where(lane == 0, run_b, jnp.where(lane == 1, start, cnt_ref[...]))
    cnt_ref[...] = cnt_ref[...] + run_b


def moe_router(acts, w_out, x, g, mods, w_router, b_router, tm=MOE_TOK):
    T = x.shape[0]
    tri = jnp.asarray(np.triu(np.ones((tm, tm), np.float32), k=1)).astype(BF16)
    tok2 = lambda dt: jax.ShapeDtypeStruct((2, T), dt)
    p_specs, p_args, proj = _proj_inputs(acts, w_out, x, mods, tm)
    return pl.pallas_call(
        functools.partial(_router_kernel, proj=proj),
        grid=(T // tm,),
        in_specs=p_specs + [
                  pl.BlockSpec((1, D), lambda i: (0, 0)),
                  _mod_spec(3, tm), _mod_spec(4, tm),
                  pl.BlockSpec((N_EXPERTS, D), lambda i: (0, 0)),
                  pl.BlockSpec((N_EXPERTS, 1), lambda i: (0, 0)),
                  _const_spec(tri)],
        out_specs=[pl.BlockSpec((tm, D), lambda i: (i, 0)),
                   pl.BlockSpec((2, tm), lambda i: (0, i)),
                   pl.BlockSpec((2, tm), lambda i: (0, i)),
                   pl.BlockSpec((None, N_EXPERTS, 128), lambda i: (i, 0, 0)),
                   pl.BlockSpec((N_EXPERTS, 128), lambda i: (0, 0))],
        out_shape=[jax.ShapeDtypeStruct((T, D), F32), tok2(jnp.int32), tok2(F32),
                   jax.ShapeDtypeStruct((T // tm, N_EXPERTS, 128), F32),
                   jax.ShapeDtypeStruct((N_EXPERTS, 128), F32)],
        compiler_params=_cparams("arbitrary"),
        name="moe_router",
    )(*p_args, g, mods, mods, w_router.T, b_router[:, None], tri)


def moe_layout(runs, totals):
    rows = totals[:, 0].astype(jnp.int32)
    tiles = (rows + MOE_TM - 1) // MOE_TM
    tile_end = jnp.cumsum(tiles)
    group = (tile_end - tiles) * MOE_TM
    run_len = runs[:, :, 0].astype(jnp.int32)
    run_src = runs[:, :, 1].astype(jnp.int32)
    run_dst = group[None, :] + runs[:, :, 2].astype(jnp.int32)
    tail = jnp.stack([group + rows, tiles * MOE_TM - rows]).astype(jnp.int32)
    n_tiles = tile_end[-1]
    t = jnp.arange(MOE_MAX_TILES, dtype=jnp.int32)
    tile_e = jnp.sum(t[:, None] >= tile_end[None, :], axis=1).astype(jnp.int32)
    last_e = jnp.sum((n_tiles - 1) >= tile_end).astype(jnp.int32)
    tile_e = jnp.where(t < n_tiles, tile_e, last_e)
    first = jnp.sum(jnp.where(tile_e[:, None] == jnp.arange(N_EXPERTS), (tile_end - tiles)[None, :], 0), axis=1)
    e_rows = jnp.sum(jnp.where(tile_e[:, None] == jnp.arange(N_EXPERTS), rows[None, :], 0), axis=1)
    tile_rows = jnp.where(t < n_tiles, jnp.clip(e_rows - (t - first) * MOE_TM, 0, MOE_TM), 0).astype(jnp.int32)
    run_tab = jnp.stack([run_len, run_src, run_dst]).reshape(3, -1)
    live = t < n_tiles
    is_first = (t == first) & live
    n_items = jnp.where(is_first, MOE_CHUNKS, 1)
    item_end = jnp.cumsum(n_items)
    j = jnp.arange(MOE_MAX_ITEMS, dtype=jnp.int32)
    it_tile = jnp.minimum(jnp.sum(j[:, None] >= item_end[None, :], axis=1), MOE_MAX_TILES - 1).astype(jnp.int32)
    chunk = j - jnp.take(item_end - n_items, it_tile)
    kind = jnp.where(jnp.take(is_first, it_tile), chunk, jnp.where(jnp.take(live, it_tile), ITEM_FULL, ITEM_DEAD))
    kind = jnp.where(j >= item_end[-1], ITEM_NONE, kind)
    items = jnp.stack([it_tile, kind.astype(jnp.int32)])
    return run_tab, tail, tile_e, n_tiles.astype(jnp.int32).reshape(1), tile_rows, items


def _run_copies(tab_ref, i, local_ref, global_ref, sem, to_global):
    out = []
    for e in range(N_EXPERTS):
        k = i * N_EXPERTS + e
        n, src, dst = tab_ref[0, k], tab_ref[1, k], tab_ref[2, k]
        for size in RUN_SIZES:
            done = (n // (2 * size)) * (2 * size)
            loc = local_ref.at[pl.ds(pl.multiple_of(src + done, RUN_ALIGN), size), :]
            glo = global_ref.at[pl.ds(pl.multiple_of(dst + done, RUN_ALIGN), size), :]
            copy = pltpu.make_async_copy(loc, glo, sem) if to_global else pltpu.make_async_copy(glo, loc, sem)
            out.append(((n & size) != 0, copy))
    return out


def _start(copies, live=True):
    for pred, copy in copies:
        pl.when(pred & live)(copy.start)


def _wait(copies, live=True):
    for pred, copy in copies:
        pl.when(pred & live)(copy.wait)


def _start_then_wait(copies):
    _start(copies)
    _wait(copies)


def _dispatch_kernel(tab_ref, tail_ref, nt_ref, lp_ref, x_ref, g_ref, sh_ref, sc_ref, xs_ref, hs_scr, z_scr, sem):
    i = pl.program_id(0)
    tm = x_ref.shape[0]
    buf = i % 2
    tg = 256
    slot = lax.broadcasted_iota(jnp.int32, (MOE_LOCAL, tg), 0)
    acc = None
    for k in range(tm // tg):
        sl = slice(k * tg, (k + 1) * tg)
        h = _norm_mod(x_ref[sl, :], g_ref[...], sh_ref[...], sc_ref[...]).astype(BF16)
        perm = jnp.where((slot == lp_ref[0:1, sl]) | (slot == lp_ref[1:2, sl]), 1.0, 0.0).astype(BF16)
        part = jnp.dot(perm, h, preferred_element_type=F32)
        acc = part if acc is None else acc + part
    hs_scr[buf] = acc.astype(BF16)
    copies = _run_copies(tab_ref, i, hs_scr.at[buf], xs_ref, sem.at[buf], to_global=True)
    _start(copies)
    _wait(_run_copies(tab_ref, jnp.maximum(i - 1, 0), hs_scr.at[1 - buf], xs_ref, sem.at[1 - buf], to_global=True),
          live=i > 0)

    @pl.when(i == 0)
    def _():
        z_scr[...] = jnp.zeros_like(z_scr)
        zrows = z_scr.shape[0]

        def zero_tile(t, carry):
            for part in range(MOE_TM // zrows):
                dst = xs_ref.at[pl.ds(pl.multiple_of(t * MOE_TM + part * zrows, zrows), zrows), :]
                copy = pltpu.make_async_copy(z_scr, dst, sem.at[2])
                copy.start()
                copy.wait()
            return carry

        lax.fori_loop(nt_ref[0], MOE_MAX_TILES, zero_tile, 0)
        tails = []
        for e in range(N_EXPERTS):
            start, n = tail_ref[0, e], tail_ref[1, e]
            for size in RUN_SIZES:
                if size >= MOE_TM:
                    continue
                done = (n // (2 * size)) * (2 * size)
                dst = xs_ref.at[pl.ds(pl.multiple_of(start + done, RUN_ALIGN), size), :]
                tails.append(((n & size) != 0, pltpu.make_async_copy(z_scr.at[pl.ds(0, size), :], dst, sem.at[2])))
        _start_then_wait(tails)

    _wait(copies, live=i == pl.num_programs(0) - 1)


def moe_dispatch(x, g, mods, lp, run_tab, tail, n_tiles, tm=MOE_TOK):
    T = x.shape[0]
    n_rows = MOE_MAX_TILES * MOE_TM
    return pl.pallas_call(
        _dispatch_kernel,
        grid_spec=pltpu.PrefetchScalarGridSpec(
            num_scalar_prefetch=3,
            grid=(T // tm,),
            in_specs=[pl.BlockSpec((2, tm), lambda i, *_: (0, i)),
                      pl.BlockSpec((tm, D), lambda i, *_: (i, 0)),
                      pl.BlockSpec((1, D), lambda i, *_: (0, 0)),
                      _mod_spec(3, tm), _mod_spec(4, tm)],
            out_specs=pl.BlockSpec(memory_space=pl.ANY),
            scratch_shapes=[pltpu.VMEM((2, MOE_LOCAL, D), BF16), pltpu.VMEM((MOE_TM // 2, D), BF16),
                            pltpu.SemaphoreType.DMA((3,))]),
        out_shape=jax.ShapeDtypeStruct((n_rows, D), BF16),
        compiler_params=_cparams("arbitrary"),
        name="moe_dispatch",
    )(run_tab, tail, n_tiles, lp, x, g, mods, mods)


def _moe_group_kernel(it_ref, te_ref, nt_ref, tr_ref, x_ref, w1_ref, w3_ref, w2_ref, o_ref, acc_scr, b1_scr, b3_scr,
                      b2_scr):
    j = pl.program_id(0)
    kind = it_ref[1, j]
    rows = tr_ref[it_ref[0, j]]
    half = MOE_TM // 2

    def chunk(c):
        @pl.when(rows > half)
        def _():
            _swiglu_accumulate(x_ref, acc_scr, b1_scr.at[c], b3_scr.at[c], b2_scr.at[c], MOE_TM)

        @pl.when(rows <= half)
        def _():
            _swiglu_accumulate(x_ref, acc_scr, b1_scr.at[c], b3_scr.at[c], b2_scr.at[c], half)

    @pl.when((kind == 0) | (kind == ITEM_FULL))
    def _():
        acc_scr[...] = jnp.zeros_like(acc_scr)

    @pl.when(kind >= 0)
    def _():
        b1_scr[kind] = w1_ref[...].astype(BF16)
        b3_scr[kind] = w3_ref[...].astype(BF16)
        b2_scr[kind] = w2_ref[...].astype(BF16)
        chunk(kind)

    for n_rows, pred in ((MOE_TM, rows > half), (half, rows <= half)):
        @pl.when((kind == ITEM_FULL) & pred)
        def _():
            for c in range(MOE_CHUNKS):
                _swiglu_accumulate(x_ref, acc_scr, b1_scr.at[c], b3_scr.at[c], b2_scr.at[c], n_rows)

    @pl.when((kind == MOE_CHUNKS - 1) | (kind == ITEM_FULL))
    def _():
        o_ref[...] = acc_scr[...].astype(o_ref.dtype)

    @pl.when(kind == ITEM_DEAD)
    def _():
        o_ref[...] = jnp.zeros_like(o_ref)


def moe_grouped_swiglu(xs, items, tile_e, n_tiles, tile_rows, w1, w3, w2):
    tf = D_FF // MOE_CHUNKS
    tile = lambda j, it, te, nt, tr: it[0, j]
    row_in = lambda j, it, te, nt, tr: (jnp.minimum(tile(j, it, te, nt, tr), jnp.maximum(nt[0] - 1, 0)), 0)
    wchunk = lambda j, it: jnp.where(it[1, j] >= 0, it[1, j], MOE_CHUNKS - 1)
    wcol = lambda j, it, te, nt, tr: (te[it[0, j]], 0, wchunk(j, it))
    wrow = lambda j, it, te, nt, tr: (te[it[0, j]], wchunk(j, it), 0)
    return pl.pallas_call(
        _moe_group_kernel,
        grid_spec=pltpu.PrefetchScalarGridSpec(
            num_scalar_prefetch=4,
            grid=(MOE_MAX_ITEMS,),
            in_specs=[pl.BlockSpec((MOE_TM, D), row_in),
                      pl.BlockSpec((None, D, tf), wcol),
                      pl.BlockSpec((None, D, tf), wcol),
                      pl.BlockSpec((None, tf, D), wrow)],
            out_specs=pl.BlockSpec((MOE_TM, D), lambda j, it, te, nt, tr: (it[0, j], 0)),
            scratch_shapes=[pltpu.VMEM((MOE_TM, D), F32), pltpu.VMEM((MOE_CHUNKS, D, tf), BF16),
                            pltpu.VMEM((MOE_CHUNKS, D, tf), BF16), pltpu.VMEM((MOE_CHUNKS, tf, D), BF16)]),
        out_shape=jax.ShapeDtypeStruct(xs.shape, BF16),
        compiler_params=_cparams("arbitrary"),
        name="moe_grouped",
    )(items, tile_e, n_tiles, tile_rows, xs, w1, w3, w2)


def _combine_kernel(tab_ref, lp_ref, wt_ref, x_ref, gate_ref, ys_ref, op_ref, os_ref, yl_scr, sem, *, n_p):
    i = pl.program_id(0)
    tm = x_ref.shape[0]
    buf = i % 2
    last = pl.num_programs(0) - 1
    gather = lambda t, b: _run_copies(tab_ref, t, yl_scr.at[b], ys_ref, sem.at[b], to_global=False)
    _start(gather(i, buf), live=i == 0)
    _start(gather(jnp.minimum(i + 1, last), 1 - buf), live=i < last)
    _wait(gather(i, buf))
    bounds = (0, 512, 1024, 1536, MOE_LOCAL)
    moe = None
    for lo, hi in zip(bounds[:-1], bounds[1:]):
        slot = lo + lax.broadcasted_iota(jnp.int32, (tm, hi - lo), 1)
        mix = (jnp.where(slot == lp_ref[:, 0:1], wt_ref[:, 0:1], 0.0)
               + jnp.where(slot == lp_ref[:, 1:2], wt_ref[:, 1:2], 0.0)).astype(BF16)
        part = jnp.dot(mix, yl_scr[buf, lo:hi, :], preferred_element_type=F32)
        moe = part if moe is None else moe + part
    out = x_ref[...] + gate_ref[...] * moe

    @pl.when(pl.program_id(0) < n_p)
    def _():
        op_ref[...] = out

    @pl.when(pl.program_id(0) >= n_p)
    def _():
        os_ref[...] = out


def moe_combine(x, mods, lp, wts, run_tab, ys, t_prompt, tm=MOE_TOK):
    T = x.shape[0]
    n_p = t_prompt // tm
    return pl.pallas_call(
        functools.partial(_combine_kernel, n_p=n_p),
        grid_spec=pltpu.PrefetchScalarGridSpec(
            num_scalar_prefetch=1,
            grid=(T // tm,),
            in_specs=[pl.BlockSpec((tm, 2), lambda i, *_: (i, 0)),
                      pl.BlockSpec((tm, 2), lambda i, *_: (i, 0)),
                      pl.BlockSpec((tm, D), lambda i, *_: (i, 0)),
                      _mod_spec(5, tm),
                      pl.BlockSpec(memory_space=pl.ANY)],
            out_specs=_part_specs((tm, D), n_p),
            scratch_shapes=[pltpu.VMEM((2, MOE_LOCAL, D), BF16), pltpu.SemaphoreType.DMA((2,))]),
        out_shape=[jax.ShapeDtypeStruct((t_prompt, D), F32), jax.ShapeDtypeStruct((T - t_prompt, D), F32)],
        compiler_params=_cparams("arbitrary"),
        name="moe_combine",
    )(run_tab, lp.T, wts.T, x, mods, ys)


def kernel(x_prompt, x_sample, state_C, state_n, state_m, cache_k, cache_v, c, c_ctx, norm1_g, norm2_g, w_ada, b_ada, ev_w_in, ev_conv, hy_w1, hy_b1, hy_w2, hy_b2, hy_w3, hy_freq, hy_d, ml_b_gate, ml_norm_g, ev_w_out, ff_w1, ff_w3, ff_w2, at_w_qkv, at_q_g, at_k_g, at_sink, at_w_out, moe_w_router, moe_b_router, moe_w1, moe_w3, moe_w2):
    BP, LP, _ = x_prompt.shape
    BS, LS, _ = x_sample.shape
    TP = BP * LP
    assert TP % GROUP == 0 and TP // GROUP == N_PROMPT_GROUPS and LS == GROUP and BS == 8

    xp, xs = x_prompt.reshape(TP, D), x_sample.reshape(BS * LS, D)
    cond = jnp.concatenate([c_ctx[None], c, jnp.zeros((16 - 1 - BS, D), F32)], axis=0)
    mods = adaln_table(cond, w_ada, b_ada)

    u, gates = even_in_proj(xp, xs, norm1_g[0:1], mods[0], ev_w_in[0], ml_b_gate[0].reshape(1, N_GATES))
    hy = []
    for seq0, B, L, nb in ((0, BP, LP, 4), (TP // LS, BS, LS, 1)):
        fwd, inv = (jnp.asarray(t).astype(BF16) for t in _dft_tables(L))
        ka, kb = hyena_filter_spectra(L, hy_w1[0], hy_b1[0], hy_w2[0], hy_b2[0], hy_w3[0], hy_freq[0], fwd)
        hy.append(hyena_mix(u, seq0, B, L, ev_conv[0], hy_d[0], fwd, inv, ka, kb, nb))
    ml_p, new_C, new_n, new_m = mlstm_mix(u, gates, 0, BP, LP, ml_norm_g[0], want_state=True)
    ml_s = mlstm_mix(u, gates, TP // LS, BS, LS, ml_norm_g[0],
                     state=(state_C[:, 0], state_n[:, 0], state_m[:, 0]))
    x = ffn_residual([hy, (ml_p, ml_s)], ev_w_out[0], (xp, xs), norm2_g[0:1], mods[0], ff_w1[0], ff_w3[0], ff_w2[0])

    q, kv = qkv_proj(x, norm1_g[1:2], mods[1], at_w_qkv[0])
    o_p, new_k, new_v = attention(q, kv, 0, at_q_g[0], at_k_g[0], at_sink[0], BP, LP)
    o_s = attention(q, kv, TP, at_q_g[0], at_k_g[0], at_sink[0], BS, LS, cache=(cache_k[:, 0], cache_v[:, 0]))
    x, lp, wts, runs, totals = moe_router([(o_p, o_s)], at_w_out[0], x, norm2_g[1:2], mods[1],
                                          moe_w_router[0], moe_b_router[0])
    run_tab, tail, tile_e, n_tiles, tile_rows, items = moe_layout(runs, totals)
    xsort = moe_dispatch(x, norm2_g[1:2], mods[1], lp, run_tab, tail, n_tiles)
    ysort = moe_grouped_swiglu(xsort, items, tile_e, n_tiles, tile_rows, moe_w1[0], moe_w3[0], moe_w2[0])
    yp, ys = moe_combine(x, mods[1], lp, wts, run_tab, ysort, TP)

    return (yp.reshape(BP, LP, D), ys.reshape(BS, LS, D),
            new_C[:, None], new_n[:, None], new_m[:, None], new_k[:, None], new_v[:, None])
```
